```python
import math
import jax, jax.numpy as jnp
from jax import lax
import numpy as np

D_MODEL = 1024
BATCH = 8
SEQ = 4096
DEPTH = 4

ATTN_WIDTH = D_MODEL // 2
SSM_WIDTH = D_MODEL - ATTN_WIDTH
HEAD_DIM = 64
N_HEADS = ATTN_WIDTH // HEAD_DIM
SSM_GROUP = 16
N_SSM_GROUPS = SSM_WIDTH // SSM_GROUP
SSM_STATE = 64
IN_WIDTH = 3 * ATTN_WIDTH + SSM_WIDTH
D_FF = ((8 * D_MODEL // 3) + 127) // 128 * 128
CONV_WIDTH = 3
BLOCK_Q = 128
N_MOD = 6
EPS = 1e-6

kernel_name = "hybrid_stickbreak_s5_convffn_adaln"


def _rmsnorm(x, g):
    xf = x.astype(jnp.float32)
    inv = lax.rsqrt(jnp.mean(xf * xf, axis=-1, keepdims=True) + EPS)
    return (xf * inv * g.astype(jnp.float32)).astype(x.dtype)


def _modulate(h, shift, scale):
    return h * (1.0 + scale[:, None, :]) + shift[:, None, :]


def _stick_breaking(q, k, v):
    B, H, L, Dh = q.shape
    nb = L // BLOCK_Q
    qf = q.astype(jnp.float32)
    kf = k.astype(jnp.float32)
    vf = v.astype(jnp.float32)
    q_blocks = qf.reshape(B, H, nb, BLOCK_Q, Dh).transpose(2, 0, 1, 3, 4)
    k_pos = jnp.arange(L, dtype=jnp.int32)
    inv_sqrt = 1.0 / math.sqrt(Dh)

    def block_fn(args):
        qb, bi = args
        z = jnp.einsum('bhqd,bhkd->bhqk', qb, kf) * inv_sqrt
        q_pos = bi * BLOCK_Q + jnp.arange(BLOCK_Q, dtype=jnp.int32)
        mask = k_pos[None, :] < q_pos[:, None]
        log_1m = jnp.where(mask, jax.nn.log_sigmoid(-z), 0.0)
        tail = lax.cumsum(log_1m, axis=3, reverse=True) - log_1m
        w = jnp.where(mask, jnp.exp(jax.nn.log_sigmoid(z) + tail), 0.0)
        return jnp.einsum('bhqk,bhkd->bhqd', w, vf)

    o = lax.map(block_fn, (q_blocks, jnp.arange(nb, dtype=jnp.int32)))
    o = o.transpose(1, 0, 3, 2, 4).reshape(B, L, H * Dh)
    return o.astype(q.dtype)


def _s5(u, a_re, a_im, log_dt, b_re, b_im, c_re, c_im, d_skip, glu_w, glu_b):
    Bsz, L, _ = u.shape
    f32 = jnp.float32
    ug = u.astype(f32).reshape(Bsz, L, N_SSM_GROUPS, SSM_GROUP)
    dt = jnp.exp(log_dt.astype(f32))[:, None]
    ar = a_re.astype(f32)
    ai = a_im.astype(f32)
    mag = jnp.exp(dt * ar)
    abar_re = mag * jnp.cos(dt * ai)
    abar_im = mag * jnp.sin(dt * ai)
    em_re = abar_re - 1.0
    em_im = abar_im
    den = ar * ar + ai * ai
    f_re = (em_re * ar + em_im * ai) / den
    f_im = (em_im * ar - em_re * ai) / den
    br = b_re.astype(f32)
    bim = b_im.astype(f32)
    bb_re = f_re[..., None] * br - f_im[..., None] * bim
    bb_im = f_re[..., None] * bim + f_im[..., None] * br
    bu_re = jnp.einsum('blgh,gph->blgp', ug, bb_re)
    bu_im = jnp.einsum('blgh,gph->blgp', ug, bb_im)
    a_r = jnp.broadcast_to(abar_re, bu_re.shape)
    a_i = jnp.broadcast_to(abar_im, bu_re.shape)

    def combine(e1, e2):
        a1r, a1i, b1r, b1i = e1
        a2r, a2i, b2r, b2i = e2
        return (a2r * a1r - a2i * a1i,
                a2r * a1i + a2i * a1r,
                a2r * b1r - a2i * b1i + b2r,
                a2r * b1i + a2i * b1r + b2i)

    _, _, s_re, s_im = lax.associative_scan(combine, (a_r, a_i, bu_re, bu_im), axis=1)
    y = (jnp.einsum('ghp,blgp->blgh', c_re.astype(f32), s_re)
         - jnp.einsum('ghp,blgp->blgh', c_im.astype(f32), s_im)
         + d_skip.astype(f32) * ug)
    y = jax.nn.gelu(y)
    gate = jax.nn.sigmoid(jnp.einsum('blgh,ghk->blgk', y, glu_w.astype(f32)) + glu_b.astype(f32))
    return (y * gate).reshape(Bsz, L, SSM_WIDTH).astype(u.dtype)


def _causal_dwconv(h, w, b):
    L = h.shape[1]
    hp = jnp.pad(h, ((0, 0), (CONV_WIDTH - 1, 0), (0, 0)))
    out = b
    for i in range(CONV_WIDTH):
        out = out + hp[:, i:i + L, :] * w[i]
    return out


def _fwd_setup_inputs(seed: int = 0) -> dict:
    key = jax.random.key(seed)
    ks = jax.random.split(key, 32)
    f32 = jnp.float32
    D, G, P, H = D_MODEL, N_SSM_GROUPS, SSM_STATE, SSM_GROUP
    nrm = lambda k, shape, s: jax.random.normal(k, shape, f32) * s
    n_idx = jnp.arange(P, dtype=f32)
    return {
        "x": jax.random.normal(ks[0], (BATCH, SEQ, D), f32),
        "c": jax.random.normal(ks[1], (BATCH, D), f32),
        "ada_w": nrm(ks[2], (DEPTH, D, N_MOD * D), 0.5 * D ** -0.5),
        "ada_b": nrm(ks[3], (DEPTH, N_MOD * D), 0.01),
        "norm1_g": 1.0 + nrm(ks[4], (DEPTH, D), 0.02),
        "w_in": nrm(ks[5], (DEPTH, D, IN_WIDTH), D ** -0.5),
        "q_norm_g": 1.0 + nrm(ks[6], (DEPTH, HEAD_DIM), 0.02),
        "k_norm_g": 1.0 + nrm(ks[7], (DEPTH, HEAD_DIM), 0.02),
        "ssm_a_re": -0.5 + nrm(ks[8], (DEPTH, G, P), 0.01),
        "ssm_a_im": math.pi * n_idx + nrm(ks[9], (DEPTH, G, P), 0.01),
        "ssm_log_dt": jax.random.uniform(ks[10], (DEPTH, G), f32, math.log(1e-3), math.log(1e-1)),
        "ssm_b_re": nrm(ks[11], (DEPTH, G, P, H), (2.0 * H) ** -0.5),
        "ssm_b_im": nrm(ks[12], (DEPTH, G, P, H), (2.0 * H) ** -0.5),
        "ssm_c_re": nrm(ks[13], (DEPTH, G, H, P), (2.0 * P) ** -0.5),
        "ssm_c_im": nrm(ks[14], (DEPTH, G, H, P), (2.0 * P) ** -0.5),
        "ssm_d": nrm(ks[15], (DEPTH, G, H), 1.0),
        "glu_w": nrm(ks[16], (DEPTH, G, H, H), H ** -0.5),
        "glu_b": nrm(ks[17], (DEPTH, G, H), 0.01),
        "attn_out_g": 1.0 + nrm(ks[18], (DEPTH, ATTN_WIDTH), 0.02),
        "ssm_out_g": 1.0 + nrm(ks[19], (DEPTH, SSM_WIDTH), 0.02),
        "w_out": nrm(ks[20], (DEPTH, D, D), D ** -0.5),
        "norm2_g": 1.0 + nrm(ks[21], (DEPTH, D), 0.02),
        "ffn_w_up": nrm(ks[22], (DEPTH, D, 2 * D_FF), D ** -0.5),
        "ffn_conv_w": nrm(ks[23], (DEPTH, CONV_WIDTH, 2 * D_FF), CONV_WIDTH ** -0.5),
        "ffn_conv_b": nrm(ks[24], (DEPTH, 2 * D_FF), 0.01),
        "ffn_w_down": nrm(ks[25], (DEPTH, D_FF, D), D_FF ** -0.5),
    }


def _fwd_reference(x, c, ada_w, ada_b, norm1_g, w_in, q_norm_g, k_norm_g,
              ssm_a_re, ssm_a_im, ssm_log_dt, ssm_b_re, ssm_b_im, ssm_c_re, ssm_c_im,
              ssm_d, glu_w, glu_b, attn_out_g, ssm_out_g, w_out, norm2_g,
              ffn_w_up, ffn_conv_w, ffn_conv_b, ffn_w_down):
    B, L, D = x.shape
    c_act = jax.nn.silu(c)
    for l in range(DEPTH):
        mod = c_act @ ada_w[l] + ada_b[l]
        sh1, sc1, g1, sh2, sc2, g2 = jnp.split(mod, N_MOD, axis=-1)

        h = _modulate(_rmsnorm(x, norm1_g[l]), sh1, sc1)
        p = h @ w_in[l]
        q = p[..., :ATTN_WIDTH].reshape(B, L, N_HEADS, HEAD_DIM)
        k = p[..., ATTN_WIDTH:2 * ATTN_WIDTH].reshape(B, L, N_HEADS, HEAD_DIM)
        v = p[..., 2 * ATTN_WIDTH:3 * ATTN_WIDTH].reshape(B, L, N_HEADS, HEAD_DIM)
        u = p[..., 3 * ATTN_WIDTH:]
        q = _rmsnorm(q, q_norm_g[l]).transpose(0, 2, 1, 3)
        k = _rmsnorm(k, k_norm_g[l]).transpose(0, 2, 1, 3)
        v = v.transpose(0, 2, 1, 3)
        o_attn = _stick_breaking(q, k, v)
        o_ssm = _s5(u, ssm_a_re[l], ssm_a_im[l], ssm_log_dt[l], ssm_b_re[l], ssm_b_im[l],
                    ssm_c_re[l], ssm_c_im[l], ssm_d[l], glu_w[l], glu_b[l])
        o = jnp.concatenate([_rmsnorm(o_attn, attn_out_g[l]),
                             _rmsnorm(o_ssm, ssm_out_g[l])], axis=-1)
        x = x + g1[:, None, :] * (o @ w_out[l])

        h = _modulate(_rmsnorm(x, norm2_g[l]), sh2, sc2)
        up = _causal_dwconv(h @ ffn_w_up[l], ffn_conv_w[l], ffn_conv_b[l])
        val, gate = jnp.split(up, 2, axis=-1)
        x = x + g2[:, None, :] * ((jax.nn.gelu(gate) * val) @ ffn_w_down[l])
    return x


import jax as _jax
import jax.numpy as _jnp

TWIN_FORMAT = 'train_step'
FWD_PARAMS = ['x', 'c', 'ada_w', 'ada_b', 'norm1_g', 'w_in', 'q_norm_g', 'k_norm_g', 'ssm_a_re', 'ssm_a_im', 'ssm_log_dt', 'ssm_b_re', 'ssm_b_im', 'ssm_c_re', 'ssm_c_im', 'ssm_d', 'glu_w', 'glu_b', 'attn_out_g', 'ssm_out_g', 'w_out', 'norm2_g', 'ffn_w_up', 'ffn_conv_w', 'ffn_conv_b', 'ffn_w_down']
TWIN_WEIGHTS = ['ada_w', 'ada_b', 'norm1_g', 'w_in', 'q_norm_g', 'k_norm_g', 'ssm_a_re', 'ssm_a_im', 'ssm_log_dt', 'ssm_b_re', 'ssm_b_im', 'ssm_c_re', 'ssm_c_im', 'ssm_d', 'glu_w', 'glu_b', 'attn_out_g', 'ssm_out_g', 'w_out', 'norm2_g', 'ffn_w_up', 'ffn_conv_w', 'ffn_conv_b', 'ffn_w_down']
TWIN_DIFF_INPUT = 'x'
TWIN_INPUTS = ['x', 'c', 'ada_w', 'ada_b', 'norm1_g', 'w_in', 'q_norm_g', 'k_norm_g', 'ssm_a_re', 'ssm_a_im', 'ssm_log_dt', 'ssm_b_re', 'ssm_b_im', 'ssm_c_re', 'ssm_c_im', 'ssm_d', 'glu_w', 'glu_b', 'attn_out_g', 'ssm_out_g', 'w_out', 'norm2_g', 'ffn_w_up', 'ffn_conv_w', 'ffn_conv_b', 'ffn_w_down', 'loss_target', 'm_ada_w', 'm_ada_b', 'm_norm1_g', 'm_w_in', 'm_q_norm_g', 'm_k_norm_g', 'm_ssm_a_re', 'm_ssm_a_im', 'm_ssm_log_dt', 'm_ssm_b_re', 'm_ssm_b_im', 'm_ssm_c_re', 'm_ssm_c_im', 'm_ssm_d', 'm_glu_w', 'm_glu_b', 'm_attn_out_g', 'm_ssm_out_g', 'm_w_out', 'm_norm2_g', 'm_ffn_w_up', 'm_ffn_conv_w', 'm_ffn_conv_b', 'm_ffn_w_down', 'v_ada_w', 'v_ada_b', 'v_norm1_g', 'v_w_in', 'v_q_norm_g', 'v_k_norm_g', 'v_ssm_a_re', 'v_ssm_a_im', 'v_ssm_log_dt', 'v_ssm_b_re', 'v_ssm_b_im', 'v_ssm_c_re', 'v_ssm_c_im', 'v_ssm_d', 'v_glu_w', 'v_glu_b', 'v_attn_out_g', 'v_ssm_out_g', 'v_w_out', 'v_norm2_g', 'v_ffn_w_up', 'v_ffn_conv_w', 'v_ffn_conv_b', 'v_ffn_w_down']
TWIN_OUTPUTS = ['loss', 'grad_x', 'grad_ada_w', 'grad_ada_b', 'grad_norm1_g', 'grad_w_in', 'grad_q_norm_g', 'grad_k_norm_g', 'grad_ssm_a_re', 'grad_ssm_a_im', 'grad_ssm_log_dt', 'grad_ssm_b_re', 'grad_ssm_b_im', 'grad_ssm_c_re', 'grad_ssm_c_im', 'grad_ssm_d', 'grad_glu_w', 'grad_glu_b', 'grad_attn_out_g', 'grad_ssm_out_g', 'grad_w_out', 'grad_norm2_g', 'grad_ffn_w_up', 'grad_ffn_conv_w', 'grad_ffn_conv_b', 'grad_ffn_w_down', 'delta_ada_w', 'delta_ada_b', 'delta_norm1_g', 'delta_w_in', 'delta_q_norm_g', 'delta_k_norm_g', 'delta_ssm_a_re', 'delta_ssm_a_im', 'delta_ssm_log_dt', 'delta_ssm_b_re', 'delta_ssm_b_im', 'delta_ssm_c_re', 'delta_ssm_c_im', 'delta_ssm_d', 'delta_glu_w', 'delta_glu_b', 'delta_attn_out_g', 'delta_ssm_out_g', 'delta_w_out', 'delta_norm2_g', 'delta_ffn_w_up', 'delta_ffn_conv_w', 'delta_ffn_conv_b', 'delta_ffn_w_down', 'new_m_ada_w', 'new_m_ada_b', 'new_m_norm1_g', 'new_m_w_in', 'new_m_q_norm_g', 'new_m_k_norm_g', 'new_m_ssm_a_re', 'new_m_ssm_a_im', 'new_m_ssm_log_dt', 'new_m_ssm_b_re', 'new_m_ssm_b_im', 'new_m_ssm_c_re', 'new_m_ssm_c_im', 'new_m_ssm_d', 'new_m_glu_w', 'new_m_glu_b', 'new_m_attn_out_g', 'new_m_ssm_out_g', 'new_m_w_out', 'new_m_norm2_g', 'new_m_ffn_w_up', 'new_m_ffn_conv_w', 'new_m_ffn_conv_b', 'new_m_ffn_w_down', 'new_v_ada_w', 'new_v_ada_b', 'new_v_norm1_g', 'new_v_w_in', 'new_v_q_norm_g', 'new_v_k_norm_g', 'new_v_ssm_a_re', 'new_v_ssm_a_im', 'new_v_ssm_log_dt', 'new_v_ssm_b_re', 'new_v_ssm_b_im', 'new_v_ssm_c_re', 'new_v_ssm_c_im', 'new_v_ssm_d', 'new_v_glu_w', 'new_v_glu_b', 'new_v_attn_out_g', 'new_v_ssm_out_g', 'new_v_w_out', 'new_v_norm2_g', 'new_v_ffn_w_up', 'new_v_ffn_conv_w', 'new_v_ffn_conv_b', 'new_v_ffn_w_down']
TWIN_LEAF_KINDS = {'loss': 'loss', 'grad_x': 'grad_x', 'grad_ada_w': 'grad_w', 'grad_ada_b': 'grad_w', 'grad_norm1_g': 'grad_w', 'grad_w_in': 'grad_w', 'grad_q_norm_g': 'grad_w', 'grad_k_norm_g': 'grad_w', 'grad_ssm_a_re': 'grad_w', 'grad_ssm_a_im': 'grad_w', 'grad_ssm_log_dt': 'grad_w', 'grad_ssm_b_re': 'grad_w', 'grad_ssm_b_im': 'grad_w', 'grad_ssm_c_re': 'grad_w', 'grad_ssm_c_im': 'grad_w', 'grad_ssm_d': 'grad_w', 'grad_glu_w': 'grad_w', 'grad_glu_b': 'grad_w', 'grad_attn_out_g': 'grad_w', 'grad_ssm_out_g': 'grad_w', 'grad_w_out': 'grad_w', 'grad_norm2_g': 'grad_w', 'grad_ffn_w_up': 'grad_w', 'grad_ffn_conv_w': 'grad_w', 'grad_ffn_conv_b': 'grad_w', 'grad_ffn_w_down': 'grad_w', 'delta_ada_w': 'delta_w', 'delta_ada_b': 'delta_w', 'delta_norm1_g': 'delta_w', 'delta_w_in': 'delta_w', 'delta_q_norm_g': 'delta_w', 'delta_k_norm_g': 'delta_w', 'delta_ssm_a_re': 'delta_w', 'delta_ssm_a_im': 'delta_w', 'delta_ssm_log_dt': 'delta_w', 'delta_ssm_b_re': 'delta_w', 'delta_ssm_b_im': 'delta_w', 'delta_ssm_c_re': 'delta_w', 'delta_ssm_c_im': 'delta_w', 'delta_ssm_d': 'delta_w', 'delta_glu_w': 'delta_w', 'delta_glu_b': 'delta_w', 'delta_attn_out_g': 'delta_w', 'delta_ssm_out_g': 'delta_w', 'delta_w_out': 'delta_w', 'delta_norm2_g': 'delta_w', 'delta_ffn_w_up': 'delta_w', 'delta_ffn_conv_w': 'delta_w', 'delta_ffn_conv_b': 'delta_w', 'delta_ffn_w_down': 'delta_w', 'new_m_ada_w': 'new_m', 'new_m_ada_b': 'new_m', 'new_m_norm1_g': 'new_m', 'new_m_w_in': 'new_m', 'new_m_q_norm_g': 'new_m', 'new_m_k_norm_g': 'new_m', 'new_m_ssm_a_re': 'new_m', 'new_m_ssm_a_im': 'new_m', 'new_m_ssm_log_dt': 'new_m', 'new_m_ssm_b_re': 'new_m', 'new_m_ssm_b_im': 'new_m', 'new_m_ssm_c_re': 'new_m', 'new_m_ssm_c_im': 'new_m', 'new_m_ssm_d': 'new_m', 'new_m_glu_w': 'new_m', 'new_m_glu_b': 'new_m', 'new_m_attn_out_g': 'new_m', 'new_m_ssm_out_g': 'new_m', 'new_m_w_out': 'new_m', 'new_m_norm2_g': 'new_m', 'new_m_ffn_w_up': 'new_m', 'new_m_ffn_conv_w': 'new_m', 'new_m_ffn_conv_b': 'new_m', 'new_m_ffn_w_down': 'new_m', 'new_v_ada_w': 'new_v', 'new_v_ada_b': 'new_v', 'new_v_norm1_g': 'new_v', 'new_v_w_in': 'new_v', 'new_v_q_norm_g': 'new_v', 'new_v_k_norm_g': 'new_v', 'new_v_ssm_a_re': 'new_v', 'new_v_ssm_a_im': 'new_v', 'new_v_ssm_log_dt': 'new_v', 'new_v_ssm_b_re': 'new_v', 'new_v_ssm_b_im': 'new_v', 'new_v_ssm_c_re': 'new_v', 'new_v_ssm_c_im': 'new_v', 'new_v_ssm_d': 'new_v', 'new_v_glu_w': 'new_v', 'new_v_glu_b': 'new_v', 'new_v_attn_out_g': 'new_v', 'new_v_ssm_out_g': 'new_v', 'new_v_w_out': 'new_v', 'new_v_norm2_g': 'new_v', 'new_v_ffn_w_up': 'new_v', 'new_v_ffn_conv_w': 'new_v', 'new_v_ffn_conv_b': 'new_v', 'new_v_ffn_w_down': 'new_v'}


def _forward(args):
    return _fwd_reference(*[args[k] for k in FWD_PARAMS])


def _output_shape():
    out = _jax.eval_shape(lambda: _forward(_fwd_setup_inputs(0)))
    return out.shape, out.dtype

N_MICROBATCH = 1
ADAM_LR = 0.001
ADAM_B1 = 0.9
ADAM_B2 = 0.999
ADAM_EPS = 1e-08
ADAM_WD = 0.01
ADAM_STEP = 10
PER_EXAMPLE_BATCH_AXIS = {'x': 0, 'c': 0, 'loss_target': 0}
SHARED_INPUTS = []
_WEIGHT_DTYPES = {'ada_w': _jnp.float32, 'ada_b': _jnp.float32, 'norm1_g': _jnp.float32, 'w_in': _jnp.float32, 'q_norm_g': _jnp.float32, 'k_norm_g': _jnp.float32, 'ssm_a_re': _jnp.float32, 'ssm_a_im': _jnp.float32, 'ssm_log_dt': _jnp.float32, 'ssm_b_re': _jnp.float32, 'ssm_b_im': _jnp.float32, 'ssm_c_re': _jnp.float32, 'ssm_c_im': _jnp.float32, 'ssm_d': _jnp.float32, 'glu_w': _jnp.float32, 'glu_b': _jnp.float32, 'attn_out_g': _jnp.float32, 'ssm_out_g': _jnp.float32, 'w_out': _jnp.float32, 'norm2_g': _jnp.float32, 'ffn_w_up': _jnp.float32, 'ffn_conv_w': _jnp.float32, 'ffn_conv_b': _jnp.float32, 'ffn_w_down': _jnp.float32}
MOMENT_SCALE = {'ada_w': 1.864629e+00, 'ada_b': 3.566594e+00, 'norm1_g': 2.801229e-01, 'w_in': 4.655143e-01, 'q_norm_g': 1.323089e-01, 'k_norm_g': 1.328977e-01, 'ssm_a_re': 5.180486e-02, 'ssm_a_im': 4.438469e-02, 'ssm_log_dt': 5.285022e+00, 'ssm_b_re': 3.587910e-02, 'ssm_b_im': 3.314422e-02, 'ssm_c_re': 6.247753e-02, 'ssm_c_im': 6.261109e-02, 'ssm_d': 1.163431e+00, 'glu_w': 3.194724e-01, 'glu_b': 4.538264e-01, 'attn_out_g': 3.224632e+00, 'ssm_out_g': 6.789931e+00, 'w_out': 1.012367e+00, 'norm2_g': 3.990959e+00, 'ffn_w_up': 1.926298e-01, 'ffn_conv_w': 6.132269e-01, 'ffn_conv_b': 6.051227e-01, 'ffn_w_down': 2.153050e-01}


def _to_microbatches(a, axis):
    t = _jnp.moveaxis(a, axis, 0)
    t = t.reshape((N_MICROBATCH, t.shape[0] // N_MICROBATCH) + t.shape[1:])
    return _jnp.moveaxis(t, 1, axis + 1)


def setup_inputs(seed: int = 0) -> dict:
    inp = _fwd_setup_inputs(seed)
    key = _jax.random.fold_in(_jax.random.key(seed), 7919)
    shape, _ = _output_shape()
    out = dict(inp)
    out["loss_target"] = _jax.random.normal(_jax.random.fold_in(key, 0), shape, _jnp.float32)
    for i, name in enumerate(TWIN_WEIGHTS):
        w = inp[name].astype(_jnp.float32)
        if MOMENT_SCALE is None:
            s = _jnp.sqrt(_jnp.mean(_jnp.square(w)) + 1e-30)
        else:
            s = MOMENT_SCALE[name]
        km, kv = _jax.random.split(_jax.random.fold_in(key, i + 1))
        out[name] = w
        out["m_" + name] = s * _jax.random.normal(km, w.shape, _jnp.float32)
        out["v_" + name] = (s * s) * _jax.random.uniform(kv, w.shape, _jnp.float32, 0.5, 1.5)
    if N_MICROBATCH > 1:
        for name, axis in PER_EXAMPLE_BATCH_AXIS.items():
            out[name] = _to_microbatches(out[name], axis)
    return {'x': out['x'], 'c': out['c'], 'ada_w': out['ada_w'], 'ada_b': out['ada_b'], 'norm1_g': out['norm1_g'], 'w_in': out['w_in'], 'q_norm_g': out['q_norm_g'], 'k_norm_g': out['k_norm_g'], 'ssm_a_re': out['ssm_a_re'], 'ssm_a_im': out['ssm_a_im'], 'ssm_log_dt': out['ssm_log_dt'], 'ssm_b_re': out['ssm_b_re'], 'ssm_b_im': out['ssm_b_im'], 'ssm_c_re': out['ssm_c_re'], 'ssm_c_im': out['ssm_c_im'], 'ssm_d': out['ssm_d'], 'glu_w': out['glu_w'], 'glu_b': out['glu_b'], 'attn_out_g': out['attn_out_g'], 'ssm_out_g': out['ssm_out_g'], 'w_out': out['w_out'], 'norm2_g': out['norm2_g'], 'ffn_w_up': out['ffn_w_up'], 'ffn_conv_w': out['ffn_conv_w'], 'ffn_conv_b': out['ffn_conv_b'], 'ffn_w_down': out['ffn_w_down'], 'loss_target': out['loss_target'], 'm_ada_w': out['m_ada_w'], 'm_ada_b': out['m_ada_b'], 'm_norm1_g': out['m_norm1_g'], 'm_w_in': out['m_w_in'], 'm_q_norm_g': out['m_q_norm_g'], 'm_k_norm_g': out['m_k_norm_g'], 'm_ssm_a_re': out['m_ssm_a_re'], 'm_ssm_a_im': out['m_ssm_a_im'], 'm_ssm_log_dt': out['m_ssm_log_dt'], 'm_ssm_b_re': out['m_ssm_b_re'], 'm_ssm_b_im': out['m_ssm_b_im'], 'm_ssm_c_re': out['m_ssm_c_re'], 'm_ssm_c_im': out['m_ssm_c_im'], 'm_ssm_d': out['m_ssm_d'], 'm_glu_w': out['m_glu_w'], 'm_glu_b': out['m_glu_b'], 'm_attn_out_g': out['m_attn_out_g'], 'm_ssm_out_g': out['m_ssm_out_g'], 'm_w_out': out['m_w_out'], 'm_norm2_g': out['m_norm2_g'], 'm_ffn_w_up': out['m_ffn_w_up'], 'm_ffn_conv_w': out['m_ffn_conv_w'], 'm_ffn_conv_b': out['m_ffn_conv_b'], 'm_ffn_w_down': out['m_ffn_w_down'], 'v_ada_w': out['v_ada_w'], 'v_ada_b': out['v_ada_b'], 'v_norm1_g': out['v_norm1_g'], 'v_w_in': out['v_w_in'], 'v_q_norm_g': out['v_q_norm_g'], 'v_k_norm_g': out['v_k_norm_g'], 'v_ssm_a_re': out['v_ssm_a_re'], 'v_ssm_a_im': out['v_ssm_a_im'], 'v_ssm_log_dt': out['v_ssm_log_dt'], 'v_ssm_b_re': out['v_ssm_b_re'], 'v_ssm_b_im': out['v_ssm_b_im'], 'v_ssm_c_re': out['v_ssm_c_re'], 'v_ssm_c_im': out['v_ssm_c_im'], 'v_ssm_d': out['v_ssm_d'], 'v_glu_w': out['v_glu_w'], 'v_glu_b': out['v_glu_b'], 'v_attn_out_g': out['v_attn_out_g'], 'v_ssm_out_g': out['v_ssm_out_g'], 'v_w_out': out['v_w_out'], 'v_norm2_g': out['v_norm2_g'], 'v_ffn_w_up': out['v_ffn_w_up'], 'v_ffn_conv_w': out['v_ffn_conv_w'], 'v_ffn_conv_b': out['v_ffn_conv_b'], 'v_ffn_w_down': out['v_ffn_w_down']}


def _loss(weights, diff, rest, loss_target):
    with _jax.named_scope("forward"):
        args = {**rest, TWIN_DIFF_INPUT: diff, **{k: w.astype(_WEIGHT_DTYPES[k]) for k, w in weights.items()}}
        y = _forward(args)
    with _jax.named_scope("loss_head"):
        err = _jnp.square(y.astype(_jnp.float32) - loss_target)
        return 0.5 * _jnp.sum(_jnp.mean(err, axis=-1)) if err.ndim else 0.5 * err


def _adamw(w, g, m, v):
    m = ADAM_B1 * m + (1.0 - ADAM_B1) * g
    v = ADAM_B2 * v + (1.0 - ADAM_B2) * _jnp.square(g)
    m_hat = m / (1.0 - ADAM_B1 ** ADAM_STEP)
    v_hat = v / (1.0 - ADAM_B2 ** ADAM_STEP)
    delta = -ADAM_LR * (m_hat / (_jnp.sqrt(v_hat) + ADAM_EPS) + ADAM_WD * w)
    return delta, m, v


def reference(x, c, ada_w, ada_b, norm1_g, w_in, q_norm_g, k_norm_g, ssm_a_re, ssm_a_im, ssm_log_dt, ssm_b_re, ssm_b_im, ssm_c_re, ssm_c_im, ssm_d, glu_w, glu_b, attn_out_g, ssm_out_g, w_out, norm2_g, ffn_w_up, ffn_conv_w, ffn_conv_b, ffn_w_down, loss_target, m_ada_w, m_ada_b, m_norm1_g, m_w_in, m_q_norm_g, m_k_norm_g, m_ssm_a_re, m_ssm_a_im, m_ssm_log_dt, m_ssm_b_re, m_ssm_b_im, m_ssm_c_re, m_ssm_c_im, m_ssm_d, m_glu_w, m_glu_b, m_attn_out_g, m_ssm_out_g, m_w_out, m_norm2_g, m_ffn_w_up, m_ffn_conv_w, m_ffn_conv_b, m_ffn_w_down, v_ada_w, v_ada_b, v_norm1_g, v_w_in, v_q_norm_g, v_k_norm_g, v_ssm_a_re, v_ssm_a_im, v_ssm_log_dt, v_ssm_b_re, v_ssm_b_im, v_ssm_c_re, v_ssm_c_im, v_ssm_d, v_glu_w, v_glu_b, v_attn_out_g, v_ssm_out_g, v_w_out, v_norm2_g, v_ffn_w_up, v_ffn_conv_w, v_ffn_conv_b, v_ffn_w_down):
    given = dict(x=x, c=c, ada_w=ada_w, ada_b=ada_b, norm1_g=norm1_g, w_in=w_in, q_norm_g=q_norm_g, k_norm_g=k_norm_g, ssm_a_re=ssm_a_re, ssm_a_im=ssm_a_im, ssm_log_dt=ssm_log_dt, ssm_b_re=ssm_b_re, ssm_b_im=ssm_b_im, ssm_c_re=ssm_c_re, ssm_c_im=ssm_c_im, ssm_d=ssm_d, glu_w=glu_w, glu_b=glu_b, attn_out_g=attn_out_g, ssm_out_g=ssm_out_g, w_out=w_out, norm2_g=norm2_g, ffn_w_up=ffn_w_up, ffn_conv_w=ffn_conv_w, ffn_conv_b=ffn_conv_b, ffn_w_down=ffn_w_down, loss_target=loss_target, m_ada_w=m_ada_w, m_ada_b=m_ada_b, m_norm1_g=m_norm1_g, m_w_in=m_w_in, m_q_norm_g=m_q_norm_g, m_k_norm_g=m_k_norm_g, m_ssm_a_re=m_ssm_a_re, m_ssm_a_im=m_ssm_a_im, m_ssm_log_dt=m_ssm_log_dt, m_ssm_b_re=m_ssm_b_re, m_ssm_b_im=m_ssm_b_im, m_ssm_c_re=m_ssm_c_re, m_ssm_c_im=m_ssm_c_im, m_ssm_d=m_ssm_d, m_glu_w=m_glu_w, m_glu_b=m_glu_b, m_attn_out_g=m_attn_out_g, m_ssm_out_g=m_ssm_out_g, m_w_out=m_w_out, m_norm2_g=m_norm2_g, m_ffn_w_up=m_ffn_w_up, m_ffn_conv_w=m_ffn_conv_w, m_ffn_conv_b=m_ffn_conv_b, m_ffn_w_down=m_ffn_w_down, v_ada_w=v_ada_w, v_ada_b=v_ada_b, v_norm1_g=v_norm1_g, v_w_in=v_w_in, v_q_norm_g=v_q_norm_g, v_k_norm_g=v_k_norm_g, v_ssm_a_re=v_ssm_a_re, v_ssm_a_im=v_ssm_a_im, v_ssm_log_dt=v_ssm_log_dt, v_ssm_b_re=v_ssm_b_re, v_ssm_b_im=v_ssm_b_im, v_ssm_c_re=v_ssm_c_re, v_ssm_c_im=v_ssm_c_im, v_ssm_d=v_ssm_d, v_glu_w=v_glu_w, v_glu_b=v_glu_b, v_attn_out_g=v_attn_out_g, v_ssm_out_g=v_ssm_out_g, v_w_out=v_w_out, v_norm2_g=v_norm2_g, v_ffn_w_up=v_ffn_w_up, v_ffn_conv_w=v_ffn_conv_w, v_ffn_conv_b=v_ffn_conv_b, v_ffn_w_down=v_ffn_w_down)
    weights = {n: given[n] for n in TWIN_WEIGHTS}
    shared = {n: given[n] for n in SHARED_INPUTS}
    per_example = {n: given[n] for n in ['x', 'c']}
    grad_fn = _jax.value_and_grad(_loss, argnums=(0, 1))

    def one_microbatch(ex, loss_target):
        ex = dict(ex)
        diff = ex.pop(TWIN_DIFF_INPUT)
        return grad_fn(weights, diff, {**shared, **ex}, loss_target)

    if N_MICROBATCH == 1:
        loss, (grad_w, grad_x) = one_microbatch(per_example, given["loss_target"])
    else:
        def body(carry, xs):
            loss_sum, grad_sum = carry
            l_k, (gw_k, gx_k) = one_microbatch(xs[0], xs[1])
            with _jax.named_scope("update"):
                return (loss_sum + l_k, _jax.tree.map(_jnp.add, grad_sum, gw_k)), gx_k

        init = (_jnp.zeros((), _jnp.float32), _jax.tree.map(_jnp.zeros_like, weights))
        (loss, grad_w), grad_x = _jax.lax.scan(body, init, (per_example, given["loss_target"]))
    with _jax.named_scope("update"):
        delta_w, new_m, new_v = {}, {}, {}
        for n in TWIN_WEIGHTS:
            delta_w[n], new_m[n], new_v[n] = _adamw(weights[n], grad_w[n], given["m_" + n], given["v_" + n])
    return (loss, grad_x, *[grad_w[n] for n in TWIN_WEIGHTS], *[delta_w[n] for n in TWIN_WEIGHTS],
            *[new_m[n] for n in TWIN_WEIGHTS], *[new_v[n] for n in TWIN_WEIGHTS])
```

```python
import functools
import math

import jax
import jax.numpy as jnp
from jax import lax
from jax.experimental import pallas as pl
from jax.experimental.pallas import tpu as pltpu

f32 = jnp.float32
bf16 = jnp.bfloat16
_MXU = jnp.bfloat16

D = 1024
ATT = 512
SSMW = 512
HD = 64
NG = 32
NP = 64
NH16 = 16
NS = NG * NP
INW = 2048
DFF = 2816
DUP = 2 * DFF
NMOD = 6
DEPTH = 4
EPS = 1e-6
NCHIP = 4
NDEV = 8

ADAM_LR = 0.001
ADAM_B1 = 0.9
ADAM_B2 = 0.999
ADAM_EPS = 1e-08
ADAM_WD = 0.01
ADAM_STEP = 10

VMEM_LIMIT = 56 * 1024 * 1024
MESH = pl.DeviceIdType.MESH

NN = (((1,), (0,)), ((), ()))
NT = (((1,), (1,)), ((), ()))
TN = (((0,), (0,)), ((), ()))


def _cp(sem=None):
    if sem is None:
        return pltpu.CompilerParams(vmem_limit_bytes=VMEM_LIMIT)
    return pltpu.CompilerParams(dimension_semantics=sem, vmem_limit_bytes=VMEM_LIMIT)


def _dot(a, b, dims=NN):
    return lax.dot_general(a.astype(_MXU), b.astype(_MXU), dims, preferred_element_type=f32)


def _dot_exact(x, m):
    hi = x.astype(bf16)
    r1 = x - hi.astype(f32)
    mid = r1.astype(bf16)
    lo = (r1 - mid.astype(f32)).astype(bf16)
    d = lambda p: lax.dot_general(p, m, NN, preferred_element_type=f32)
    return d(hi) + d(mid) + d(lo)


def _gelu(x):
    c = math.sqrt(2.0 / math.pi)
    return 0.5 * x * (1.0 + jnp.tanh(c * (x + 0.044715 * (x * x * x))))


def _gelu_grad(x):
    c = math.sqrt(2.0 / math.pi)
    t = jnp.tanh(c * (x + 0.044715 * (x * x * x)))
    return 0.5 * (1.0 + t) + 0.5 * x * (1.0 - t * t) * c * (1.0 + 3.0 * 0.044715 * (x * x))


def _sigmoid(x):
    return 1.0 / (1.0 + jnp.exp(-x))


def _mm(name, a, b, *, dims, grid, a_block, a_map, b_block, b_map, out_shape, o_block, o_map,
        out_dtype=f32):
    nk = grid[2]
    acc_shape = tuple(s for s in o_block if s is not None)

    def body(a_ref, b_ref, o_ref, acc_ref):
        k = pl.program_id(2)
        part = _dot(a_ref[...], b_ref[...], dims)
        if nk == 1:
            o_ref[...] = part.astype(o_ref.dtype)
        else:
            @pl.when(k == 0)
            def _():
                acc_ref[...] = part

            @pl.when(k > 0)
            def _():
                acc_ref[...] += part

            @pl.when(k == nk - 1)
            def _():
                o_ref[...] = acc_ref[...].astype(o_ref.dtype)

    return pl.pallas_call(
        body, name=name, grid=grid,
        in_specs=[pl.BlockSpec(a_block, a_map), pl.BlockSpec(b_block, b_map)],
        out_specs=pl.BlockSpec(o_block, o_map),
        out_shape=jax.ShapeDtypeStruct(out_shape, out_dtype),
        scratch_shapes=[pltpu.VMEM(acc_shape if nk > 1 else (8, 128), f32)],
        compiler_params=_cp(("parallel", "parallel", "arbitrary")),
    )(a, b)


def _row_block(L, want):
    return want if L % want == 0 else L


def _rmsmod_fwd(x, res, gate, g, sh, sc):
    L = x.shape[0]
    bm = _row_block(L, 256)
    with_res = res is not None

    def body(*refs):
        if with_res:
            x_ref, r_ref, gt_ref, g_ref, sh_ref, sc_ref, xo_ref, h_ref = refs
            xin = x_ref[...] + gt_ref[...] * r_ref[...]
            xo_ref[...] = xin
        else:
            x_ref, g_ref, sh_ref, sc_ref, h_ref = refs
            xin = x_ref[...]
        inv = lax.rsqrt(jnp.mean(xin * xin, axis=-1, keepdims=True) + EPS)
        xn = xin * inv * g_ref[...]
        h_ref[...] = (xn * (1.0 + sc_ref[...]) + sh_ref[...]).astype(h_ref.dtype)

    row = pl.BlockSpec((bm, D), lambda i: (i, 0))
    vec = pl.BlockSpec((1, D), lambda i: (0, 0))
    if with_res:
        return pl.pallas_call(
            body, name="rmsmod_res_fwd", grid=(L // bm,),
            in_specs=[row, row, vec, vec, vec, vec], out_specs=[row, row],
            out_shape=[jax.ShapeDtypeStruct((L, D), f32), jax.ShapeDtypeStruct((L, D), bf16)],
            compiler_params=_cp(("parallel",)),
        )(x, res, gate, g, sh, sc)
    h = pl.pallas_call(
        body, name="rmsmod_fwd", grid=(L // bm,),
        in_specs=[row, vec, vec, vec], out_specs=row,
        out_shape=jax.ShapeDtypeStruct((L, D), bf16),
        compiler_params=_cp(("parallel",)),
    )(x, g, sh, sc)
    return None, h


def _rmsmod_bwd(dh, x, g, sc, dres):
    L = x.shape[0]
    bm = _row_block(L, 256)

    def body(dh_ref, x_ref, g_ref, sc_ref, dr_ref, dx_ref, dsh_ref, dsc_ref, dg_ref):
        i = pl.program_id(0)
        xv = x_ref[...]
        dhv = dh_ref[...]
        inv = lax.rsqrt(jnp.mean(xv * xv, axis=-1, keepdims=True) + EPS)
        xh = xv * inv
        gv = g_ref[...]
        xn = xh * gv
        dxn = dhv * (1.0 + sc_ref[...])
        dxh = dxn * gv
        dx_ref[...] = inv * (dxh - xh * jnp.mean(dxh * xh, axis=-1, keepdims=True)) + dr_ref[...]
        p_sh = jnp.sum(dhv, axis=0, keepdims=True)
        p_sc = jnp.sum(dhv * xn, axis=0, keepdims=True)
        p_g = jnp.sum(dxn * xh, axis=0, keepdims=True)

        @pl.when(i == 0)
        def _():
            dsh_ref[...] = p_sh
            dsc_ref[...] = p_sc
            dg_ref[...] = p_g

        @pl.when(i > 0)
        def _():
            dsh_ref[...] += p_sh
            dsc_ref[...] += p_sc
            dg_ref[...] += p_g

    row = pl.BlockSpec((bm, D), lambda i: (i, 0))
    vec = pl.BlockSpec((1, D), lambda i: (0, 0))
    return pl.pallas_call(
        body, name="rmsmod_bwd", grid=(L // bm,),
        in_specs=[row, row, vec, vec, row], out_specs=[row, vec, vec, vec],
        out_shape=[jax.ShapeDtypeStruct((L, D), f32)] + [jax.ShapeDtypeStruct((1, D), f32)] * 3,
        compiler_params=_cp(("arbitrary",)),
    )(dh, x, g, sc, dres)


def _gate_bwd(dx, y, gate):
    L = dx.shape[0]
    bm = _row_block(L, 256)

    def body(dx_ref, y_ref, gt_ref, dy_ref, dgt_ref):
        i = pl.program_id(0)
        dxv = dx_ref[...]
        dy_ref[...] = (gt_ref[...] * dxv).astype(dy_ref.dtype)
        part = jnp.sum(dxv * y_ref[...], axis=0, keepdims=True)

        @pl.when(i == 0)
        def _():
            dgt_ref[...] = part

        @pl.when(i > 0)
        def _():
            dgt_ref[...] += part

    row = pl.BlockSpec((bm, D), lambda i: (i, 0))
    vec = pl.BlockSpec((1, D), lambda i: (0, 0))
    return pl.pallas_call(
        body, name="gate_bwd", grid=(L // bm,),
        in_specs=[row, row, vec], out_specs=[row, vec],
        out_shape=[jax.ShapeDtypeStruct((L, D), bf16), jax.ShapeDtypeStruct((1, D), f32)],
        compiler_params=_cp(("arbitrary",)),
    )(dx, y, gate)


def _loss_fwd_bwd(x1, down, gate, target):
    L = x1.shape[0]
    bm = _row_block(L, 256)
    nsteps = L // bm

    def body(x_ref, d_ref, gt_ref, t_ref, dy_ref, loss_ref, acc_ref):
        i = pl.program_id(0)
        diff = x_ref[...] + gt_ref[...] * d_ref[...] - t_ref[...]
        dy_ref[...] = diff * (1.0 / D)
        part = jnp.sum(diff * diff, axis=0, keepdims=True)

        @pl.when(i == 0)
        def _():
            acc_ref[...] = part

        @pl.when(i > 0)
        def _():
            acc_ref[...] += part

        @pl.when(i == nsteps - 1)
        def _():
            tot = jnp.sum(acc_ref[...], axis=1, keepdims=True) * (0.5 / D)
            loss_ref[...] = jnp.broadcast_to(tot, (8, 128))

    row = pl.BlockSpec((bm, D), lambda i: (i, 0))
    vec = pl.BlockSpec((1, D), lambda i: (0, 0))
    dy, loss = pl.pallas_call(
        body, name="loss_fwd_bwd", grid=(nsteps,),
        in_specs=[row, row, vec, row],
        out_specs=[row, pl.BlockSpec((8, 128), lambda i: (0, 0))],
        out_shape=[jax.ShapeDtypeStruct((L, D), f32), jax.ShapeDtypeStruct((8, 128), f32)],
        scratch_shapes=[pltpu.VMEM((1, D), f32)],
        compiler_params=_cp(("arbitrary",)),
    )(x1, down, gate, target)
    return loss[0, 0], dy


def _head_mean_matrix():
    r = lax.broadcasted_iota(jnp.int32, (128, 128), 0) // HD
    c = lax.broadcasted_iota(jnp.int32, (128, 128), 1) // HD
    return jnp.where(r == c, 1.0 / HD, 0.0).astype(bf16)


def _qknorm_fwd(p, gqk):
    L = p.shape[0]
    bm = _row_block(L, 256)

    def body(p_ref, g_ref, o_ref):
        e = _head_mean_matrix()
        for c in range(8):
            sl = slice(128 * c, 128 * (c + 1))
            xv = p_ref[:, sl]
            inv = lax.rsqrt(_dot_exact(xv * xv, e) + EPS)
            o_ref[:, sl] = (xv * inv * g_ref[:, sl]).astype(o_ref.dtype)
        o_ref[:, 1024:1536] = p_ref[:, 1024:1536].astype(o_ref.dtype)

    return pl.pallas_call(
        body, name="qknorm_fwd", grid=(L // bm,),
        in_specs=[pl.BlockSpec((bm, 1536), lambda i: (i, 0)), pl.BlockSpec((1, 1024), lambda i: (0, 0))],
        out_specs=pl.BlockSpec((bm, 1536), lambda i: (i, 0)),
        out_shape=jax.ShapeDtypeStruct((L, 1536), bf16),
        compiler_params=_cp(("parallel",)),
    )(p, gqk)


def _qknorm_bwd(dqk, p, gqk):
    L = p.shape[0]
    bm = _row_block(L, 256)
    nsteps = L // bm

    def body(d_ref, p_ref, g_ref, o_ref, dg_ref, acc_ref):
        i = pl.program_id(0)
        e = _head_mean_matrix()
        for c in range(8):
            sl = slice(128 * c, 128 * (c + 1))
            xv = p_ref[:, sl]
            dv = d_ref[:, sl]
            gv = g_ref[:, sl]
            inv = lax.rsqrt(_dot_exact(xv * xv, e) + EPS)
            xh = xv * inv
            dxh = dv * gv
            o_ref[:, sl] = (inv * (dxh - xh * _dot_exact(dxh * xh, e))).astype(o_ref.dtype)
            part = jnp.sum(dv * xh, axis=0, keepdims=True)

            @pl.when(i == 0)
            def _():
                acc_ref[:, sl] = part

            @pl.when(i > 0)
            def _():
                acc_ref[:, sl] += part

        @pl.when(i == nsteps - 1)
        def _():
            r = lax.broadcasted_iota(jnp.int32, (1024, 128), 0)
            col = lax.broadcasted_iota(jnp.int32, (1024, 128), 1)
            fold = jnp.where(col == (r // 512) * HD + r % HD, 1.0, 0.0).astype(bf16)
            dg_ref[...] = _dot_exact(jnp.broadcast_to(acc_ref[...], (8, 1024)), fold)

    return pl.pallas_call(
        body, name="qknorm_bwd", grid=(nsteps,),
        in_specs=[pl.BlockSpec((bm, 1024), lambda i: (i, 0)), pl.BlockSpec((bm, 1024), lambda i: (i, 0)),
                  pl.BlockSpec((1, 1024), lambda i: (0, 0))],
        out_specs=[pl.BlockSpec((bm, 1024), lambda i: (i, 0)), pl.BlockSpec((8, 128), lambda i: (0, 0))],
        out_shape=[jax.ShapeDtypeStruct((L, 1024), bf16), jax.ShapeDtypeStruct((8, 128), f32)],
        scratch_shapes=[pltpu.VMEM((1, 1024), f32)],
        compiler_params=_cp(("arbitrary",)),
    )(dqk, p, gqk)


def _outnorm_fwd(oa, os_, ga, gs):
    L = oa.shape[0]
    bm = _row_block(L, 256)

    def body(a_ref, s_ref, ga_ref, gs_ref, o_ref):
        for x_ref, g_ref, off in ((a_ref, ga_ref, 0), (s_ref, gs_ref, 512)):
            xv = x_ref[...]
            inv = lax.rsqrt(jnp.mean(xv * xv, axis=-1, keepdims=True) + EPS)
            o_ref[:, off:off + 512] = (xv * inv * g_ref[...]).astype(o_ref.dtype)

    half = pl.BlockSpec((bm, 512), lambda i: (i, 0))
    vec = pl.BlockSpec((1, 512), lambda i: (0, 0))
    return pl.pallas_call(
        body, name="outnorm_fwd", grid=(L // bm,),
        in_specs=[half, half, vec, vec], out_specs=pl.BlockSpec((bm, D), lambda i: (i, 0)),
        out_shape=jax.ShapeDtypeStruct((L, D), bf16),
        compiler_params=_cp(("parallel",)),
    )(oa, os_, ga, gs)


def _outnorm_bwd(do, oa, os_, ga, gs):
    L = oa.shape[0]
    bm = _row_block(L, 256)

    def body(do_ref, a_ref, s_ref, ga_ref, gs_ref, da_ref, ds_ref, dga_ref, dgs_ref):
        i = pl.program_id(0)
        for x_ref, g_ref, dx_ref, dg_ref, off in ((a_ref, ga_ref, da_ref, dga_ref, 0),
                                                  (s_ref, gs_ref, ds_ref, dgs_ref, 512)):
            xv = x_ref[...]
            dv = do_ref[:, off:off + 512]
            inv = lax.rsqrt(jnp.mean(xv * xv, axis=-1, keepdims=True) + EPS)
            xh = xv * inv
            dxh = dv * g_ref[...]
            dx_ref[...] = inv * (dxh - xh * jnp.mean(dxh * xh, axis=-1, keepdims=True))
            part = jnp.sum(dv * xh, axis=0, keepdims=True)

            @pl.when(i == 0)
            def _():
                dg_ref[...] = part

            @pl.when(i > 0)
            def _():
                dg_ref[...] += part

    half = pl.BlockSpec((bm, 512), lambda i: (i, 0))
    vec = pl.BlockSpec((1, 512), lambda i: (0, 0))
    return pl.pallas_call(
        body, name="outnorm_bwd", grid=(L // bm,),
        in_specs=[pl.BlockSpec((bm, D), lambda i: (i, 0)), half, half, vec, vec],
        out_specs=[half, half, vec, vec],
        out_shape=[jax.ShapeDtypeStruct((L, 512), f32)] * 2 + [jax.ShapeDtypeStruct((1, 512), f32)] * 2,
        compiler_params=_cp(("arbitrary",)),
    )(do, oa, os_, ga, gs)


NCB = DFF // 128


def _shift_down(x, s, row):
    return jnp.where(row >= s, pltpu.roll(x, s, 0), 0.0)


def _shift_up(x, s, row, L):
    return jnp.where(row < L - s, pltpu.roll(x, L - s, 0), 0.0)


def _conv(x, w_ref, b_ref, row):
    out = b_ref[...] + _shift_down(x, 2, row) * w_ref[0:1, :]
    out = out + _shift_down(x, 1, row) * w_ref[1:2, :]
    return out + x * w_ref[2:3, :]


def _convglu_fwd(up_pre, cw, cb):
    L = up_pre.shape[0]

    def body(v_ref, g_ref, wv_ref, wg_ref, bv_ref, bg_ref, a_ref):
        row = lax.broadcasted_iota(jnp.int32, (L, 128), 0)
        val = _conv(v_ref[...], wv_ref, bv_ref, row)
        gate = _conv(g_ref[...], wg_ref, bg_ref, row)
        a_ref[...] = (_gelu(gate) * val).astype(a_ref.dtype)

    col = lambda off: pl.BlockSpec((L, 128), lambda j: (0, j + off))
    wsp = lambda off: pl.BlockSpec((3, 128), lambda j: (0, j + off))
    bsp = lambda off: pl.BlockSpec((1, 128), lambda j: (0, j + off))
    return pl.pallas_call(
        body, name="convglu_fwd", grid=(NCB,),
        in_specs=[col(0), col(NCB), wsp(0), wsp(NCB), bsp(0), bsp(NCB)],
        out_specs=pl.BlockSpec((L, 128), lambda j: (0, j)),
        out_shape=jax.ShapeDtypeStruct((L, DFF), bf16),
        compiler_params=_cp(("parallel",)),
    )(up_pre, up_pre, cw, cw, cb, cb)


def _convglu_bwd(da, up_pre, cw, cb):
    L = up_pre.shape[0]

    def body(da_ref, own_ref, par_ref, wo_ref, wp_ref, bo_ref, bp_ref, dx_ref, dw_ref, db_ref):
        j = pl.program_id(0)
        row = lax.broadcasted_iota(jnp.int32, (L, 128), 0)
        xo = own_ref[...]
        u_own = _conv(xo, wo_ref, bo_ref, row)
        u_par = _conv(par_ref[...], wp_ref, bp_ref, row)
        dav = da_ref[...]

        def finish(d):
            dx = d * wo_ref[2:3, :] + _shift_up(d, 1, row, L) * wo_ref[1:2, :]
            dx_ref[...] = (dx + _shift_up(d, 2, row, L) * wo_ref[0:1, :]).astype(dx_ref.dtype)
            dw_ref[0:1, :] = jnp.sum(d * _shift_down(xo, 2, row), axis=0, keepdims=True)
            dw_ref[1:2, :] = jnp.sum(d * _shift_down(xo, 1, row), axis=0, keepdims=True)
            dw_ref[2:3, :] = jnp.sum(d * xo, axis=0, keepdims=True)
            db_ref[...] = jnp.sum(d, axis=0, keepdims=True)

        @pl.when(j < NCB)
        def _():
            finish(dav * _gelu(u_par))

        @pl.when(j >= NCB)
        def _():
            finish(dav * u_par * _gelu_grad(u_own))

    own = lambda j: (0, j)
    par = lambda j: (0, (j + NCB) % (2 * NCB))
    return pl.pallas_call(
        body, name="convglu_bwd", grid=(2 * NCB,),
        in_specs=[pl.BlockSpec((L, 128), lambda j: (0, j % NCB)),
                  pl.BlockSpec((L, 128), own), pl.BlockSpec((L, 128), par),
                  pl.BlockSpec((3, 128), own), pl.BlockSpec((3, 128), par),
                  pl.BlockSpec((1, 128), own), pl.BlockSpec((1, 128), par)],
        out_specs=[pl.BlockSpec((L, 128), own), pl.BlockSpec((3, 128), own), pl.BlockSpec((1, 128), own)],
        out_shape=[jax.ShapeDtypeStruct((L, DUP), bf16), jax.ShapeDtypeStruct((3, DUP), f32),
                   jax.ShapeDtypeStruct((1, DUP), f32)],
        compiler_params=_cp(("parallel",)),
    )(da, up_pre, up_pre, cw, cw, cb, cb)


def _attn_block(L):
    return 256 if L % 256 == 0 else 128


def _sb_weights(z, mask, tri_gt, carry):
    e = jnp.exp(-jnp.abs(z))
    l1p = jnp.log1p(e)
    ls_pos = jnp.minimum(z, 0.0) - l1p
    lm = jnp.where(mask, ls_pos - z, 0.0)
    tail = _dot_exact(lm, tri_gt) + carry
    w = jnp.where(mask, jnp.exp(ls_pos + tail), 0.0)
    return w, ls_pos, lm


def _attn_fwd(qkv):
    L = qkv.shape[0]
    B = _attn_block(L)
    nq = L // B

    def body(q_ref, k_ref, v_ref, o_ref):
        qi = pl.program_id(1)
        lane = lax.broadcasted_iota(jnp.int32, (B, 128), 1)
        ti = lax.broadcasted_iota(jnp.int32, (B, B), 0)
        si = lax.broadcasted_iota(jnp.int32, (B, B), 1)
        tri_gt = jnp.where(ti > si, 1.0, 0.0).astype(bf16)
        qv = q_ref[...]
        acc = jnp.zeros((B, 128), f32)
        for h in range(2):
            in_head = (lane // HD) == h

            def step(n, st):
                o_acc, carry = st
                jb = qi - n
                ks = pl.multiple_of(jb * B, B)
                kh = jnp.where(in_head, k_ref[pl.ds(ks, B), :], jnp.zeros((), bf16))
                vh = jnp.where(in_head, v_ref[pl.ds(ks, B), :], jnp.zeros((), bf16))
                z = lax.dot_general(qv, kh, NT, preferred_element_type=f32) * (1.0 / math.sqrt(HD))
                mask = (ks + si) < (qi * B + ti)
                w, _, lm = _sb_weights(z, mask, tri_gt, carry)
                o_acc = o_acc + lax.dot_general(w.astype(bf16), vh, NN, preferred_element_type=f32)
                return o_acc, carry + jnp.sum(lm, axis=1, keepdims=True)

            acc, _ = lax.fori_loop(0, qi + 1, step, (acc, jnp.zeros((B, 1), f32)))
        o_ref[...] = acc

    return pl.pallas_call(
        body, name="attn_fwd", grid=(4, nq),
        in_specs=[pl.BlockSpec((B, 128), lambda hp, i: (i, hp)),
                  pl.BlockSpec((L, 128), lambda hp, i: (0, 4 + hp)),
                  pl.BlockSpec((L, 128), lambda hp, i: (0, 8 + hp))],
        out_specs=pl.BlockSpec((B, 128), lambda hp, i: (i, hp)),
        out_shape=jax.ShapeDtypeStruct((L, ATT), f32),
        compiler_params=_cp(("parallel", "parallel")),
    )(qkv, qkv, qkv)


def _attn_bwd(qkv, o, do):
    L = qkv.shape[0]
    B = _attn_block(L)
    nq = L // B

    def body(q_ref, k_ref, v_ref, o_ref, do_ref, dq_ref, dk_ref, dv_ref):
        qi = pl.program_id(1)

        @pl.when(qi == 0)
        def _():
            dk_ref[...] = jnp.zeros_like(dk_ref)
            dv_ref[...] = jnp.zeros_like(dv_ref)

        lane = lax.broadcasted_iota(jnp.int32, (B, 128), 1)
        ti = lax.broadcasted_iota(jnp.int32, (B, B), 0)
        si = lax.broadcasted_iota(jnp.int32, (B, B), 1)
        tri_gt = jnp.where(ti > si, 1.0, 0.0).astype(bf16)
        tri_ge = jnp.where(ti >= si, 1.0, 0.0).astype(bf16)
        qv = q_ref[...]
        do16 = do_ref[...].astype(bf16)
        dsum_lanes = do16.astype(f32) * o_ref[...]
        dq_acc = jnp.zeros((B, 128), f32)
        for h in range(2):
            in_head = (lane // HD) == h
            zero16 = jnp.zeros((), bf16)
            qh = jnp.where(in_head, qv, zero16)
            doh = jnp.where(in_head, do16, zero16)
            dsum = jnp.sum(jnp.where(in_head, dsum_lanes, 0.0), axis=1, keepdims=True)

            def step(n, st):
                dq_a, carry, suffix = st
                jb = qi - n
                ks = pl.multiple_of(jb * B, B)
                kh = jnp.where(in_head, k_ref[pl.ds(ks, B), :], zero16)
                vh = jnp.where(in_head, v_ref[pl.ds(ks, B), :], zero16)
                z = lax.dot_general(qv, kh, NT, preferred_element_type=f32) * (1.0 / math.sqrt(HD))
                mask = (ks + si) < (qi * B + ti)
                w, ls_pos, lm = _sb_weights(z, mask, tri_gt, carry)
                w16 = w.astype(bf16)
                dw = lax.dot_general(doh, vh, NT, preferred_element_type=f32)
                da = w16.astype(f32) * dw
                sig = jnp.exp(ls_pos)
                suf = _dot_exact(da, tri_ge) + suffix
                dz = jnp.where(mask, da * (1.0 - sig) - sig * (dsum - suf), 0.0)
                dz16 = (dz * (1.0 / math.sqrt(HD))).astype(bf16)
                dq_a = dq_a + lax.dot_general(dz16, kh, NN, preferred_element_type=f32)
                dk_ref[pl.ds(ks, B), :] += lax.dot_general(dz16, qh, TN, preferred_element_type=f32)
                dv_ref[pl.ds(ks, B), :] += lax.dot_general(w16, doh, TN, preferred_element_type=f32)
                return (dq_a, carry + jnp.sum(lm, axis=1, keepdims=True),
                        suffix + jnp.sum(da, axis=1, keepdims=True))

            zc = jnp.zeros((B, 1), f32)
            dq_acc, _, _ = lax.fori_loop(0, qi + 1, step, (dq_acc, zc, zc))
        dq_ref[...] = dq_acc

    blk = pl.BlockSpec((B, 128), lambda hp, i: (i, hp))
    dqkv = pl.pallas_call(
        body, name="attn_bwd", grid=(4, nq),
        in_specs=[blk,
                  pl.BlockSpec((L, 128), lambda hp, i: (0, 4 + hp)),
                  pl.BlockSpec((L, 128), lambda hp, i: (0, 8 + hp)),
                  blk, blk],
        out_specs=[blk, pl.BlockSpec((L, 128), lambda hp, i: (0, hp)),
                   pl.BlockSpec((L, 128), lambda hp, i: (0, hp))],
        out_shape=[jax.ShapeDtypeStruct((L, ATT), f32)] * 3,
        compiler_params=_cp(("parallel", "arbitrary")),
    )(qkv, qkv, qkv, o, do)
    return dqkv


def _s5_disc(ldt, ar, ai, br, bi):
    dt = jnp.exp(ldt)
    mag = jnp.exp(dt * ar)
    abar_re = mag * jnp.cos(dt * ai)
    abar_im = mag * jnp.sin(dt * ai)
    em_re = abar_re - 1.0
    em_im = abar_im
    den = ar * ar + ai * ai
    f_re = (em_re * ar + em_im * ai) / den
    f_im = (em_im * ar - em_re * ai) / den
    bb_re = f_re * br - f_im * bi
    bb_im = f_re * bi + f_im * br
    return abar_re, abar_im, bb_re, bb_im


def _s5_disc_fwd(ldt, ar, ai, br, bi):
    def body(ldt_ref, ar_ref, ai_ref, br_ref, bi_ref, o1, o2, o3, o4):
        outs = _s5_disc(ldt_ref[...], ar_ref[...], ai_ref[...], br_ref[...], bi_ref[...])
        for o_ref, v in zip((o1, o2, o3, o4), outs):
            o_ref[...] = v

    col = jax.ShapeDtypeStruct((NS, 1), f32)
    mat = jax.ShapeDtypeStruct((NS, NH16), f32)
    return pl.pallas_call(body, name="s5_disc_fwd", out_shape=[col, col, mat, mat],
                          compiler_params=_cp())(ldt, ar, ai, br, bi)


def _s5_disc_bwd(ldt, ar, ai, br, bi, d_are, d_aim, d_bbre, d_bbim):
    def body(ldt_ref, ar_ref, ai_ref, br_ref, bi_ref, c1, c2, c3, c4, o_ldt, o_ar, o_ai, o_br, o_bi):
        prim = (ldt_ref[...], ar_ref[...], ai_ref[...], br_ref[...], bi_ref[...])
        _, vjp = jax.vjp(_s5_disc, *prim)
        g_ldt, g_ar, g_ai, g_br, g_bi = vjp((c1[...], c2[...], c3[...], c4[...]))
        o_ar[...] = g_ar
        o_ai[...] = g_ai
        o_br[...] = g_br
        o_bi[...] = g_bi
        r = lax.broadcasted_iota(jnp.int32, (NG, NS), 0)
        c = lax.broadcasted_iota(jnp.int32, (NG, NS), 1)
        fold = jnp.where(c // NP == r, 1.0, 0.0).astype(bf16)
        hi = jnp.broadcast_to(g_ldt, (NS, 128))
        p1 = hi.astype(bf16)
        r1 = hi - p1.astype(f32)
        p2 = r1.astype(bf16)
        p3 = (r1 - p2.astype(f32)).astype(bf16)
        dd = lambda p: lax.dot_general(fold, p, NN, preferred_element_type=f32)
        o_ldt[...] = dd(p1) + dd(p2) + dd(p3)

    col = jax.ShapeDtypeStruct((NS, 1), f32)
    mat = jax.ShapeDtypeStruct((NS, NH16), f32)
    return pl.pallas_call(
        body, name="s5_disc_bwd",
        out_shape=[jax.ShapeDtypeStruct((NG, 128), f32), col, col, mat, mat],
        compiler_params=_cp())(ldt, ar, ai, br, bi, d_are, d_aim, d_bbre, d_bbim)


SCAN_LANES = 256


def _cmul(ar, ai, br, bi):
    return ar * br - ai * bi, ar * bi + ai * br


def _pow_tables(ar, ai):
    a2 = _cmul(ar, ai, ar, ai)
    a4 = _cmul(*a2, *a2)
    rows = [(ar, ai)]
    for _ in range(7):
        rows.append(_cmul(*rows[-1], ar, ai))
    return (ar, ai), a2, a4, rows


def _scan_rows(xr_ref, xi_ref, abar_re, abar_im, cr_ref, ci_ref, reverse, on_tile=None):
    T = xr_ref.shape[0]
    nt = T // 8
    row = lax.broadcasted_iota(jnp.int32, (8, SCAN_LANES), 0)
    for c in range(NS // SCAN_LANES):
        ls = slice(c * SCAN_LANES, (c + 1) * SCAN_LANES)
        ar = abar_re[:, ls]
        ai = abar_im[:, ls]
        if reverse:
            ai = -ai
        a1, a2, a4, prow = _pow_tables(ar, ai)
        if reverse:
            prow = prow[::-1]
        pr = jnp.concatenate([p[0] for p in prow], axis=0)
        pi = jnp.concatenate([p[1] for p in prow], axis=0)
        steps = [(1, a1), (2, a2), (4, a4)]

        def tile(n, carry):
            cr, ci = carry
            it = (nt - 1 - n) if reverse else n
            rs = pl.ds(pl.multiple_of(it * 8, 8), 8)
            xr = xr_ref[rs, ls]
            xi = xi_ref[rs, ls]
            for d, (dr, di) in steps:
                if reverse:
                    keep = row < 8 - d
                    sr = jnp.where(keep, pltpu.roll(xr, 8 - d, 0), 0.0)
                    si = jnp.where(keep, pltpu.roll(xi, 8 - d, 0), 0.0)
                else:
                    keep = row >= d
                    sr = jnp.where(keep, pltpu.roll(xr, d, 0), 0.0)
                    si = jnp.where(keep, pltpu.roll(xi, d, 0), 0.0)
                mr, mi = _cmul(dr, di, sr, si)
                xr = xr + mr
                xi = xi + mi
            mr, mi = _cmul(pr, pi, cr, ci)
            xr = xr + mr
            xi = xi + mi
            xr_ref[rs, ls] = xr
            xi_ref[rs, ls] = xi
            if on_tile is not None:
                nxt_r = jnp.where(row < 7, pltpu.roll(xr, 7, 0), cr)
                nxt_i = jnp.where(row < 7, pltpu.roll(xi, 7, 0), ci)
                on_tile(ls, rs, nxt_r, nxt_i)
            edge = slice(0, 1) if reverse else slice(7, 8)
            return xr[edge, :], xi[edge, :]

        cr, ci = lax.fori_loop(0, nt, tile, (cr_ref[:, ls], ci_ref[:, ls]))
        cr_ref[:, ls] = cr
        ci_ref[:, ls] = ci


def _s5_chunk(L):
    return 256 if L % 256 == 0 else L


def _s5_gate(s_re, s_im, u, ccr_ref, cci_ref, dsk_ref, gw_ref, gb_ref):
    y = _dot(s_re, ccr_ref[...]) - _dot(s_im, cci_ref[...]) + dsk_ref[...] * u
    yg = _gelu(y)
    gate = _sigmoid(_dot(yg, gw_ref[...]) + gb_ref[...])
    return y, yg, gate


def _s5_fwd(p, abar_re, abar_im, bbr, bbi, ccr, cci, dsk, gw, gb):
    L = p.shape[0]
    T = _s5_chunk(L)

    def body(u_ref, are_ref, aim_ref, bbr_ref, bbi_ref, ccr_ref, cci_ref, dsk_ref, gw_ref, gb_ref,
             o_ref, sr_ref, si_ref, cr_ref, ci_ref):
        @pl.when(pl.program_id(0) == 0)
        def _():
            cr_ref[...] = jnp.zeros_like(cr_ref)
            ci_ref[...] = jnp.zeros_like(ci_ref)

        u = u_ref[...]
        sr_ref[...] = _dot(u, bbr_ref[...])
        si_ref[...] = _dot(u, bbi_ref[...])
        _scan_rows(sr_ref, si_ref, are_ref[...], aim_ref[...], cr_ref, ci_ref, reverse=False)
        _, yg, gate = _s5_gate(sr_ref[...], si_ref[...], u, ccr_ref, cci_ref, dsk_ref, gw_ref, gb_ref)
        o_ref[...] = yg * gate

    full = lambda shape: pl.BlockSpec(shape, lambda i: (0, 0))
    return pl.pallas_call(
        body, name="s5_fwd", grid=(L // T,),
        in_specs=[pl.BlockSpec((T, SSMW), lambda i: (i, 3)), full((1, NS)), full((1, NS)),
                  full((SSMW, NS)), full((SSMW, NS)), full((NS, SSMW)), full((NS, SSMW)),
                  full((1, SSMW)), full((SSMW, SSMW)), full((1, SSMW))],
        out_specs=[pl.BlockSpec((T, SSMW), lambda i: (i, 0)), pl.BlockSpec((T, NS), lambda i: (i, 0)),
                   pl.BlockSpec((T, NS), lambda i: (i, 0))],
        out_shape=[jax.ShapeDtypeStruct((L, SSMW), f32), jax.ShapeDtypeStruct((L, NS), f32),
                   jax.ShapeDtypeStruct((L, NS), f32)],
        scratch_shapes=[pltpu.VMEM((1, NS), f32), pltpu.VMEM((1, NS), f32)],
        compiler_params=_cp(("arbitrary",)),
    )(p, abar_re, abar_im, bbr, bbi, ccr, cci, dsk, gw, gb)


def _s5_bwd(dout, p, s_re, s_im, abar_re, abar_im, bbr, bbi, ccr, cci, dsk, gw, gb):
    L = p.shape[0]
    T = _s5_chunk(L)
    nchunk = L // T

    def body(do_ref, u_ref, sr_ref, si_ref, are_ref, aim_ref, bbr_ref, bbi_ref, ccr_ref, cci_ref,
             dsk_ref, gw_ref, gb_ref,
             du_ref, dy_ref, yg_ref, dgl_ref, lr16_ref, li16_ref, dare_ref, daim_ref, dd_ref, dgb_ref,
             lr_ref, li_ref, cr_ref, ci_ref, accr_ref, acci_ref):
        i = pl.program_id(0)

        @pl.when(i == 0)
        def _():
            cr_ref[...] = jnp.zeros_like(cr_ref)
            ci_ref[...] = jnp.zeros_like(ci_ref)
            accr_ref[...] = jnp.zeros_like(accr_ref)
            acci_ref[...] = jnp.zeros_like(acci_ref)
            dd_ref[...] = jnp.zeros_like(dd_ref)
            dgb_ref[...] = jnp.zeros_like(dgb_ref)

        u = u_ref[...]
        dov = do_ref[...]
        y, yg, gate = _s5_gate(sr_ref[...], si_ref[...], u, ccr_ref, cci_ref, dsk_ref, gw_ref, gb_ref)
        dgl = dov * yg * gate * (1.0 - gate)
        dyg = dov * gate + _dot(dgl, gw_ref[...], NT)
        dy = dyg * _gelu_grad(y)
        dy_ref[...] = dy.astype(dy_ref.dtype)
        yg_ref[...] = yg.astype(yg_ref.dtype)
        dgl_ref[...] = dgl.astype(dgl_ref.dtype)
        dd_ref[...] += jnp.sum(dy * u, axis=0, keepdims=True)
        dgb_ref[...] += jnp.sum(dgl, axis=0, keepdims=True)
        lr_ref[...] = _dot(dy, ccr_ref[...], NT)
        li_ref[...] = -_dot(dy, cci_ref[...], NT)

        def on_tile(ls, rs, nxt_r, nxt_i):
            s_r = sr_ref[rs, ls]
            s_i = si_ref[rs, ls]
            accr_ref[:, ls] += nxt_r * s_r + nxt_i * s_i
            acci_ref[:, ls] += nxt_i * s_r - nxt_r * s_i

        _scan_rows(lr_ref, li_ref, are_ref[...], aim_ref[...], cr_ref, ci_ref, reverse=True,
                   on_tile=on_tile)
        lam_r = lr_ref[...]
        lam_i = li_ref[...]
        lr16_ref[...] = lam_r.astype(lr16_ref.dtype)
        li16_ref[...] = lam_i.astype(li16_ref.dtype)
        du_ref[...] = dy * dsk_ref[...] + _dot(lam_r, bbr_ref[...], NT) + _dot(lam_i, bbi_ref[...], NT)

        @pl.when(i == nchunk - 1)
        def _():
            dare_ref[...] = jnp.sum(accr_ref[...], axis=0, keepdims=True)
            daim_ref[...] = jnp.sum(acci_ref[...], axis=0, keepdims=True)

    rev = lambda i: (nchunk - 1 - i, 0)
    full = lambda shape: pl.BlockSpec(shape, lambda i: (0, 0))
    half = pl.BlockSpec((T, SSMW), rev)
    wide = pl.BlockSpec((T, NS), rev)
    return pl.pallas_call(
        body, name="s5_bwd", grid=(nchunk,),
        in_specs=[half, pl.BlockSpec((T, SSMW), lambda i: (nchunk - 1 - i, 3)), wide, wide,
                  full((1, NS)), full((1, NS)), full((SSMW, NS)), full((SSMW, NS)),
                  full((NS, SSMW)), full((NS, SSMW)), full((1, SSMW)), full((SSMW, SSMW)), full((1, SSMW))],
        out_specs=[half, half, half, half, wide, wide, full((1, NS)), full((1, NS)),
                   full((1, SSMW)), full((1, SSMW))],
        out_shape=[jax.ShapeDtypeStruct((L, SSMW), f32)] + [jax.ShapeDtypeStruct((L, SSMW), bf16)] * 3
                  + [jax.ShapeDtypeStruct((L, NS), bf16)] * 2 + [jax.ShapeDtypeStruct((1, NS), f32)] * 2
                  + [jax.ShapeDtypeStruct((1, SSMW), f32)] * 2,
        scratch_shapes=[pltpu.VMEM((T, NS), f32), pltpu.VMEM((T, NS), f32),
                        pltpu.VMEM((1, NS), f32), pltpu.VMEM((1, NS), f32),
                        pltpu.VMEM((8, NS), f32), pltpu.VMEM((8, NS), f32)],
        compiler_params=_cp(("arbitrary",)),
    )(dout, p, s_re, s_im, abar_re, abar_im, bbr, bbi, ccr, cci, dsk, gw, gb)


def _block_diag(x):
    g, r, c = x.shape
    eye = jnp.eye(g, dtype=x.dtype)
    return (x[:, :, None, :] * eye[:, None, :, None]).reshape(g * r, g * c)


def _diag_blocks(dense, r, c):
    g = dense.shape[0] // r
    return jnp.stack([dense[k * r:(k + 1) * r, k * c:(k + 1) * c] for k in range(g)])


def _mod_fwd(c16, ada_w, ada_b_loc):
    nloc = ada_w.shape[2]

    def body(c_ref, w_ref, b_ref, o_ref):
        cv = c_ref[...]
        act = cv * _sigmoid(cv)
        o_ref[...] = _dot(act, w_ref[...]) + b_ref[...]

    return pl.pallas_call(
        body, name="mod_fwd", grid=(DEPTH,),
        in_specs=[pl.BlockSpec((16, D), lambda l: (0, 0)), pl.BlockSpec((None, D, nloc), lambda l: (l, 0, 0)),
                  pl.BlockSpec((None, 1, nloc), lambda l: (l, 0, 0))],
        out_specs=pl.BlockSpec((None, 16, nloc), lambda l: (l, 0, 0)),
        out_shape=jax.ShapeDtypeStruct((DEPTH, 16, nloc), f32),
        compiler_params=_cp(("parallel",)),
    )(c16, ada_w, ada_b_loc)


def _ada_grad(c16, dmod16):
    nloc = dmod16.shape[2]

    def body(c_ref, d_ref, o_ref):
        cv = c_ref[...]
        act = cv * _sigmoid(cv)
        o_ref[...] = _dot(act, d_ref[...], TN)

    return pl.pallas_call(
        body, name="ada_grad", grid=(DEPTH,),
        in_specs=[pl.BlockSpec((16, D), lambda l: (0, 0)), pl.BlockSpec((None, 16, nloc), lambda l: (l, 0, 0))],
        out_specs=pl.BlockSpec((None, D, nloc), lambda l: (l, 0, 0)),
        out_shape=jax.ShapeDtypeStruct((DEPTH, D, nloc), f32),
        compiler_params=_cp(("parallel",)),
    )(c16, dmod16)


def _as2d(a):
    return a.reshape(-1, a.shape[-1])


def _ew_rows(rows):
    for cand in (512, 256, 128, 64, 32, 16, 8):
        if rows % cand == 0:
            return cand
    return rows


def _cast_bf16(w):
    w2 = _as2d(w)
    rows, cols = w2.shape
    bm = _ew_rows(rows)

    def body(x_ref, o_ref):
        o_ref[...] = x_ref[...].astype(bf16)

    spec = pl.BlockSpec((bm, cols), lambda i: (i, 0))
    out = pl.pallas_call(body, name="cast_bf16", grid=(rows // bm,), in_specs=[spec], out_specs=spec,
                         out_shape=jax.ShapeDtypeStruct((rows, cols), bf16),
                         compiler_params=_cp(("parallel",)))(w2)
    return out.reshape(w.shape)


def _add2(a, b):
    a2, b2 = _as2d(a), _as2d(b)
    rows, cols = a2.shape
    bm = _ew_rows(rows)

    def body(a_ref, b_ref, o_ref):
        o_ref[...] = a_ref[...] + b_ref[...]

    spec = pl.BlockSpec((bm, cols), lambda i: (i, 0))
    out = pl.pallas_call(body, name="add2", grid=(rows // bm,), in_specs=[spec, spec], out_specs=spec,
                         out_shape=jax.ShapeDtypeStruct((rows, cols), f32),
                         compiler_params=_cp(("parallel",)))(a2, b2)
    return out.reshape(a.shape)


def _adamw(parts, w, m, v):
    n = parts.shape[0]
    w2, m2, v2 = _as2d(w), _as2d(m), _as2d(v)
    rows, cols = w2.shape
    p3 = parts.reshape(n, rows, cols)
    bm = _ew_rows(rows)
    if bm * cols * 4 > (1 << 20) and bm % 16 == 0:
        bm //= 2

    def body(p_ref, w_ref, m_ref, v_ref, g_ref, d_ref, nm_ref, nv_ref):
        g = p_ref[0]
        for k in range(1, n):
            g = g + p_ref[k]
        mn = ADAM_B1 * m_ref[...] + (1.0 - ADAM_B1) * g
        vn = ADAM_B2 * v_ref[...] + (1.0 - ADAM_B2) * (g * g)
        m_hat = mn / (1.0 - ADAM_B1 ** ADAM_STEP)
        v_hat = vn / (1.0 - ADAM_B2 ** ADAM_STEP)
        g_ref[...] = g
        d_ref[...] = -ADAM_LR * (m_hat / (jnp.sqrt(v_hat) + ADAM_EPS) + ADAM_WD * w_ref[...])
        nm_ref[...] = mn
        nv_ref[...] = vn

    spec = pl.BlockSpec((bm, cols), lambda i: (i, 0))
    outs = pl.pallas_call(
        body, name="adamw", grid=(rows // bm,),
        in_specs=[pl.BlockSpec((n, bm, cols), lambda i: (0, i, 0)), spec, spec, spec],
        out_specs=[spec] * 4, out_shape=[jax.ShapeDtypeStruct((rows, cols), f32)] * 4,
        compiler_params=_cp(("parallel",)),
    )(p3, w2, m2, v2)
    return tuple(o.reshape(w.shape) for o in outs)


ANY = pl.BlockSpec(memory_space=pl.ANY)


def _my_pos():
    return lax.axis_index("x"), lax.axis_index("y"), lax.axis_index("c")


def _allgather8(name, x):
    def body(x_ref, out_ref, send_sems, recv_sems, local_sem):
        mx, my, mc = _my_pos()
        me, sibling = (mx, my, mc), (mx, my, 1 - mc)
        chips = [(1 - mx, my), (mx, 1 - my), (1 - mx, 1 - my)]

        def slot(px, py, pc):
            return out_ref.at[4 * px + 2 * py + pc]

        def copy(k, block, to, src=None):
            return pltpu.make_async_remote_copy(
                src_ref=slot(*block) if src is None else src, dst_ref=slot(*block),
                send_sem=send_sems.at[k], recv_sem=recv_sems.at[k], device_id=to, device_id_type=MESH)

        mine = pltpu.make_async_copy(x_ref, slot(*me), local_sem)
        mine.start()
        first = [copy(0, me, sibling, src=x_ref)]
        first += [copy(1 + j, me, (*chip, mc), src=x_ref) for j, chip in enumerate(chips)]
        for cp in first:
            cp.start()
        passed = [copy(4 + j, (*chip, mc), sibling) for j, chip in enumerate(chips)]
        for j, chip in enumerate(chips):
            copy(1 + j, (*chip, mc), me).wait_recv()
            passed[j].start()
        copy(0, sibling, me).wait_recv()
        for j, chip in enumerate(chips):
            copy(4 + j, (*chip, 1 - mc), me).wait_recv()
        for cp in first + passed:
            cp.wait_send()
        mine.wait()

    return pl.pallas_call(
        body, name=name, in_specs=[ANY], out_specs=ANY,
        out_shape=jax.ShapeDtypeStruct((NDEV,) + x.shape, x.dtype),
        scratch_shapes=[pltpu.SemaphoreType.DMA((7,)), pltpu.SemaphoreType.DMA((7,)), pltpu.SemaphoreType.DMA],
    )(x)


def _chip_gather(arrs):
    n = len(arrs)

    def body(*refs):
        ins, outs = refs[:n], refs[n:2 * n]
        send_sems, recv_sems, local_sems = refs[2 * n:]
        mx, my, mc = _my_pos()
        chips = [(1 - mx, my), (mx, 1 - my), (1 - mx, 1 - my)]
        mine = 2 * mx + my
        local = [pltpu.make_async_copy(ins[a], outs[a].at[mine], local_sems.at[a]) for a in range(n)]
        for cp in local:
            cp.start()
        sends = []
        for a in range(n):
            for j, (px, py) in enumerate(chips):
                sends.append(pltpu.make_async_remote_copy(
                    src_ref=ins[a], dst_ref=outs[a].at[mine],
                    send_sem=send_sems.at[3 * a + j], recv_sem=recv_sems.at[3 * a + j],
                    device_id=(px, py, mc), device_id_type=MESH))
        for cp in sends:
            cp.start()
        for a in range(n):
            for j, (px, py) in enumerate(chips):
                pltpu.make_async_remote_copy(
                    src_ref=ins[a], dst_ref=outs[a].at[2 * px + py],
                    send_sem=send_sems.at[3 * a + j], recv_sem=recv_sems.at[3 * a + j],
                    device_id=(px, py, mc), device_id_type=MESH).wait_recv()
        for cp in sends:
            cp.wait_send()
        for cp in local:
            cp.wait()

    return pl.pallas_call(
        body, name="chip_gather", in_specs=[ANY] * n, out_specs=[ANY] * n,
        out_shape=[jax.ShapeDtypeStruct((NCHIP,) + a.shape, a.dtype) for a in arrs],
        scratch_shapes=[pltpu.SemaphoreType.DMA((3 * n,)), pltpu.SemaphoreType.DMA((3 * n,)),
                        pltpu.SemaphoreType.DMA((n,))],
    )(*arrs)


def _sibling_exchange(arrs):
    n = len(arrs)

    def body(*refs):
        ins, outs = refs[:n], refs[n:2 * n]
        send_sems, recv_sems = refs[2 * n:]
        mx, my, mc = _my_pos()
        cps = [pltpu.make_async_remote_copy(
            src_ref=ins[a], dst_ref=outs[a], send_sem=send_sems.at[a], recv_sem=recv_sems.at[a],
            device_id=(mx, my, 1 - mc), device_id_type=MESH) for a in range(n)]
        for cp in cps:
            cp.start()
        for cp in cps:
            cp.wait()

    return pl.pallas_call(
        body, name="sibling_exchange", in_specs=[ANY] * n, out_specs=[ANY] * n,
        out_shape=[jax.ShapeDtypeStruct(a.shape, a.dtype) for a in arrs],
        scratch_shapes=[pltpu.SemaphoreType.DMA((n,)), pltpu.SemaphoreType.DMA((n,))],
    )(*arrs)


def _chip_scatter(arrs):
    n = len(arrs)

    def body(*refs):
        ins, outs = refs[:n], refs[n:2 * n]
        send_sems, recv_sems, local_sems = refs[2 * n:]
        mx, my, mc = _my_pos()
        chips = [(1 - mx, my), (mx, 1 - my), (1 - mx, 1 - my)]
        mine = 2 * mx + my
        local = [pltpu.make_async_copy(ins[a].at[mine], outs[a].at[mine], local_sems.at[a]) for a in range(n)]
        for cp in local:
            cp.start()
        sends = []
        for a in range(n):
            for j, (px, py) in enumerate(chips):
                sends.append(pltpu.make_async_remote_copy(
                    src_ref=ins[a].at[2 * px + py], dst_ref=outs[a].at[mine],
                    send_sem=send_sems.at[3 * a + j], recv_sem=recv_sems.at[3 * a + j],
                    device_id=(px, py, mc), device_id_type=MESH))
        for cp in sends:
            cp.start()
        for a in range(n):
            for j, (px, py) in enumerate(chips):
                pltpu.make_async_remote_copy(
                    src_ref=ins[a].at[mine], dst_ref=outs[a].at[2 * px + py],
                    send_sem=send_sems.at[3 * a + j], recv_sem=recv_sems.at[3 * a + j],
                    device_id=(px, py, mc), device_id_type=MESH).wait_recv()
        for cp in sends:
            cp.wait_send()
        for cp in local:
            cp.wait()

    return pl.pallas_call(
        body, name="chip_scatter", in_specs=[ANY] * n, out_specs=[ANY] * n,
        out_shape=[jax.ShapeDtypeStruct(a.shape, a.dtype) for a in arrs],
        scratch_shapes=[pltpu.SemaphoreType.DMA((3 * n,)), pltpu.SemaphoreType.DMA((3 * n,)),
                        pltpu.SemaphoreType.DMA((n,))],
    )(*arrs)


def _layer_fwd(x, res, res_gate, wl, mod):
    L = x.shape[0]
    bm = _row_block(L, 512)
    nb = L // bm
    sh1, sc1, g1, sh2, sc2, g2 = [mod[k * D:(k + 1) * D][None, :] for k in range(NMOD)]
    sv = {}
    xin, h = _rmsmod_fwd(x, res, res_gate, wl["norm1_g"], sh1, sc1)
    if xin is None:
        xin = x
    p = _mm("mm_in", h, wl["w_in"], dims=NN, grid=(nb, 4, 1),
            a_block=(bm, D), a_map=lambda i, j, k: (i, 0),
            b_block=(None, D, 512), b_map=lambda i, j, k: (j, 0, 0),
            out_shape=(L, INW), o_block=(bm, 512), o_map=lambda i, j, k: (i, j))
    qkv = _qknorm_fwd(p, wl["gqk"])
    oa = _attn_fwd(qkv)
    os_, s_re, s_im = _s5_fwd(p, wl["abar_re"], wl["abar_im"], wl["bbr"], wl["bbi"], wl["ccr"], wl["cci"],
                              wl["dsk"], wl["gw"], wl["gb"])
    o = _outnorm_fwd(oa, os_, wl["attn_out_g"], wl["ssm_out_g"])
    proj = _mm("mm_out", o, wl["w_out"], dims=NN, grid=(nb, 2, 1),
               a_block=(bm, D), a_map=lambda i, j, k: (i, 0),
               b_block=(D, 512), b_map=lambda i, j, k: (0, j),
               out_shape=(L, D), o_block=(bm, 512), o_map=lambda i, j, k: (i, j))
    x1, h2 = _rmsmod_fwd(xin, proj, g1, wl["norm2_g"], sh2, sc2)
    up_pre = _mm("mm_up", h2, wl["w_up"], dims=NN, grid=(nb, 4, 1),
                 a_block=(bm, D), a_map=lambda i, j, k: (i, 0),
                 b_block=(None, D, 1408), b_map=lambda i, j, k: (j, 0, 0),
                 out_shape=(L, DUP), o_block=(bm, 1408), o_map=lambda i, j, k: (i, j))
    a = _convglu_fwd(up_pre, wl["conv_w"], wl["conv_b"])
    down = _mm("mm_down", a, wl["w_down"], dims=NN, grid=(nb, 2, 1),
               a_block=(bm, DFF), a_map=lambda i, j, k: (i, 0),
               b_block=(DFF, 512), b_map=lambda i, j, k: (0, j),
               out_shape=(L, D), o_block=(bm, 512), o_map=lambda i, j, k: (i, j))
    sv.update(xin=xin, h=h, p=p, qkv=qkv, oa=oa, os=os_, s_re=s_re, s_im=s_im, o=o, proj=proj,
              x1=x1, h2=h2, up_pre=up_pre, a=a, down=down,
              sc1=sc1, g1=g1, sc2=sc2, g2=g2)
    return x1, down, g2, sv


def _layer_bwd(dx_out, wl, sv):
    L = dx_out.shape[0]
    bm = _row_block(L, 512)
    nb = L // bm
    bk = _row_block(L, 512)
    nk = L // bk
    g = {}
    dd, dg2 = _gate_bwd(dx_out, sv["down"], sv["g2"])
    da = _mm("mm_down_dx", dd, wl["w_down"], dims=NT, grid=(nb, 2, 1),
             a_block=(bm, D), a_map=lambda i, j, k: (i, 0),
             b_block=(1408, D), b_map=lambda i, j, k: (j, 0),
             out_shape=(L, DFF), o_block=(bm, 1408), o_map=lambda i, j, k: (i, j))
    g["w_down"] = _mm("mm_down_dw", sv["a"], dd, dims=TN, grid=(2, 2, nk),
                      a_block=(bk, 1408), a_map=lambda i, j, k: (k, i),
                      b_block=(bk, 512), b_map=lambda i, j, k: (k, j),
                      out_shape=(DFF, D), o_block=(1408, 512), o_map=lambda i, j, k: (i, j))
    dup, g["conv_w"], g["conv_b"] = _convglu_bwd(da, sv["up_pre"], wl["conv_w"], wl["conv_b"])
    dh2 = _mm("mm_up_dx", dup, wl["w_up"], dims=NT, grid=(nb, 1, 4),
              a_block=(bm, 1408), a_map=lambda i, j, k: (i, k),
              b_block=(None, D, 1408), b_map=lambda i, j, k: (k, 0, 0),
              out_shape=(L, D), o_block=(bm, D), o_map=lambda i, j, k: (i, 0))
    g["w_up"] = _mm("mm_up_dw", sv["h2"], dup, dims=TN, grid=(2, 4, nk),
                    a_block=(bk, 512), a_map=lambda i, j, k: (k, i),
                    b_block=(bk, 1408), b_map=lambda i, j, k: (k, j),
                    out_shape=(NCHIP, D, 1408), o_block=(None, 512, 1408), o_map=lambda i, j, k: (j, i, 0))
    dx1, dsh2, dsc2, g["norm2_g"] = _rmsmod_bwd(dh2, sv["x1"], wl["norm2_g"], sv["sc2"], dx_out)
    dproj, dg1 = _gate_bwd(dx1, sv["proj"], sv["g1"])
    do = _mm("mm_out_dx", dproj, wl["w_out"], dims=NT, grid=(nb, 2, 1),
             a_block=(bm, D), a_map=lambda i, j, k: (i, 0),
             b_block=(512, D), b_map=lambda i, j, k: (j, 0),
             out_shape=(L, D), o_block=(bm, 512), o_map=lambda i, j, k: (i, j))
    g["w_out"] = _mm("mm_out_dw", sv["o"], dproj, dims=TN, grid=(2, 2, nk),
                     a_block=(bk, 512), a_map=lambda i, j, k: (k, i),
                     b_block=(bk, 512), b_map=lambda i, j, k: (k, j),
                     out_shape=(D, D), o_block=(512, 512), o_map=lambda i, j, k: (i, j))
    doa, dos, g["attn_out_g"], g["ssm_out_g"] = _outnorm_bwd(do, sv["oa"], sv["os"],
                                                             wl["attn_out_g"], wl["ssm_out_g"])
    dqn, dkn, dv = _attn_bwd(sv["qkv"], sv["oa"], doa)
    dqk, dgqk = _qknorm_bwd(jnp.concatenate([dqn, dkn], axis=1), sv["p"], wl["gqk"])
    g["q_norm_g"] = dgqk[0, :HD]
    g["k_norm_g"] = dgqk[0, HD:2 * HD]
    (du, dy16, yg16, dgl16, lam_re, lam_im, d_abar_re, d_abar_im, g["ssm_d"], g["glu_b"]) = _s5_bwd(
        dos, sv["p"], sv["s_re"], sv["s_im"], wl["abar_re"], wl["abar_im"], wl["bbr"], wl["bbi"],
        wl["ccr"], wl["cci"], wl["dsk"], wl["gw"], wl["gb"])

    def tn_dense(name, a, a_col, a_cols, b, b_cols):
        return _mm(name, a, b, dims=TN, grid=(a_cols // 512, b_cols // 512, nk),
                   a_block=(bk, 512), a_map=lambda i, j, k: (k, a_col + i),
                   b_block=(bk, 512), b_map=lambda i, j, k: (k, j),
                   out_shape=(a_cols, b_cols), o_block=(512, 512), o_map=lambda i, j, k: (i, j))

    d_ccr = tn_dense("s5_dcc", sv["s_re"], 0, NS, dy16, SSMW)
    d_cci = tn_dense("s5_dcc", sv["s_im"], 0, NS, dy16, SSMW)
    d_bbr = tn_dense("s5_dbb", sv["p"], 3, SSMW, lam_re, NS)
    d_bbi = tn_dense("s5_dbb", sv["p"], 3, SSMW, lam_im, NS)
    d_gw = tn_dense("s5_dgw", yg16, 0, SSMW, dgl16, SSMW)
    g["ssm_c_re"] = _diag_blocks(d_ccr, NP, NH16).transpose(0, 2, 1)
    g["ssm_c_im"] = -_diag_blocks(d_cci, NP, NH16).transpose(0, 2, 1)
    g["glu_w"] = _diag_blocks(d_gw, NH16, NH16)
    d_bbre = _diag_blocks(d_bbr, NH16, NP).transpose(0, 2, 1).reshape(NS, NH16)
    d_bbim = _diag_blocks(d_bbi, NH16, NP).transpose(0, 2, 1).reshape(NS, NH16)
    d_ldt, d_ar, d_ai, d_br, d_bi = _s5_disc_bwd(
        wl["ldt_col"], wl["ar_col"], wl["ai_col"], wl["br_mat"], wl["bi_mat"],
        d_abar_re.reshape(NS, 1), d_abar_im.reshape(NS, 1), d_bbre, d_bbim)
    g["ssm_log_dt"] = d_ldt[:, 0]
    g["ssm_a_re"] = d_ar.reshape(NG, NP)
    g["ssm_a_im"] = d_ai.reshape(NG, NP)
    g["ssm_b_re"] = d_br.reshape(NG, NP, NH16)
    g["ssm_b_im"] = d_bi.reshape(NG, NP, NH16)
    dp = jnp.concatenate([dqk, dv.astype(bf16), du.astype(bf16)], axis=1)
    dh = _mm("mm_in_dx", dp, wl["w_in"], dims=NT, grid=(nb, 1, 4),
             a_block=(bm, 512), a_map=lambda i, j, k: (i, k),
             b_block=(None, D, 512), b_map=lambda i, j, k: (k, 0, 0),
             out_shape=(L, D), o_block=(bm, D), o_map=lambda i, j, k: (i, 0))
    g["w_in"] = _mm("mm_in_dw", sv["h"], dp, dims=TN, grid=(2, 4, nk),
                    a_block=(bk, 512), a_map=lambda i, j, k: (k, i),
                    b_block=(bk, 512), b_map=lambda i, j, k: (k, j),
                    out_shape=(NCHIP, D, 512), o_block=(None, 512, 512), o_map=lambda i, j, k: (j, i, 0))
    dx_in, dsh1, dsc1, g["norm1_g"] = _rmsmod_bwd(dh, sv["xin"], wl["norm1_g"], sv["sc1"], dx1)
    g["dmod"] = jnp.concatenate([dsh1, dsc1, dg1, dsh2, dsc2, dg2], axis=1)[0]
    return dx_in, g


def _prep_layer(l, big, small):
    wl = {}
    wl["w_in"] = big["w_in"][:, l]
    wl["w_up"] = big["w_up"][:, l]
    wl["w_out"] = big["w_out"][:, l].reshape(D, D)
    wl["w_down"] = big["w_down"][:, l].reshape(DFF, D)
    wl["conv_w"] = big["conv_w"][:, l].transpose(1, 0, 2).reshape(3, DUP)
    wl["conv_b"] = small["ffn_conv_b"][l][None, :]
    wl["norm1_g"] = small["norm1_g"][l][None, :]
    wl["norm2_g"] = small["norm2_g"][l][None, :]
    wl["attn_out_g"] = small["attn_out_g"][l][None, :]
    wl["ssm_out_g"] = small["ssm_out_g"][l][None, :]
    wl["gqk"] = jnp.concatenate([jnp.tile(small["q_norm_g"][l], 8), jnp.tile(small["k_norm_g"][l], 8)])[None, :]
    wl["ldt_col"] = jnp.repeat(small["ssm_log_dt"][l], NP)[:, None]
    wl["ar_col"] = small["ssm_a_re"][l].reshape(NS, 1)
    wl["ai_col"] = small["ssm_a_im"][l].reshape(NS, 1)
    wl["br_mat"] = small["ssm_b_re"][l].reshape(NS, NH16)
    wl["bi_mat"] = small["ssm_b_im"][l].reshape(NS, NH16)
    abar_re, abar_im, bb_re, bb_im = _s5_disc_fwd(wl["ldt_col"], wl["ar_col"], wl["ai_col"],
                                                  wl["br_mat"], wl["bi_mat"])
    wl["abar_re"] = abar_re.reshape(1, NS)
    wl["abar_im"] = abar_im.reshape(1, NS)
    wl["bbr"] = _block_diag(bb_re.reshape(NG, NP, NH16).transpose(0, 2, 1)).astype(bf16)
    wl["bbi"] = _block_diag(bb_im.reshape(NG, NP, NH16).transpose(0, 2, 1)).astype(bf16)
    wl["ccr"] = _block_diag(small["ssm_c_re"][l].transpose(0, 2, 1)).astype(bf16)
    wl["cci"] = _block_diag(small["ssm_c_im"][l].transpose(0, 2, 1)).astype(bf16)
    wl["gw"] = _block_diag(small["glu_w"][l]).astype(bf16)
    wl["dsk"] = small["ssm_d"][l].reshape(1, SSMW)
    wl["gb"] = small["glu_b"][l].reshape(1, SSMW)
    return wl


def _local_step(x, target, mods, big, small):
    wls = [_prep_layer(l, big, small) for l in range(DEPTH)]
    saved = []
    cur, res, res_gate = x, None, None
    for l in range(DEPTH):
        cur, res, res_gate, sv = _layer_fwd(cur, res, res_gate, wls[l], mods[l])
        saved.append(sv)
    loss, dx = _loss_fwd_bwd(cur, res, res_gate, target)
    grads = [None] * DEPTH
    for l in reversed(range(DEPTH)):
        dx, grads[l] = _layer_bwd(dx, wls[l], saved[l])
    return loss, dx, grads


BIG = ("w_in", "w_out", "ffn_w_up", "ffn_w_down")
SMALL = ("ada_b", "norm1_g", "q_norm_g", "k_norm_g", "ssm_a_re", "ssm_a_im", "ssm_log_dt",
         "ssm_b_re", "ssm_b_im", "ssm_c_re", "ssm_c_im", "ssm_d", "glu_w", "glu_b",
         "attn_out_g", "ssm_out_g", "norm2_g", "ffn_conv_b")
NAMES = ("ada_w", "ada_b", "norm1_g", "w_in", "q_norm_g", "k_norm_g", "ssm_a_re", "ssm_a_im",
         "ssm_log_dt", "ssm_b_re", "ssm_b_im", "ssm_c_re", "ssm_c_im", "ssm_d", "glu_w", "glu_b",
         "attn_out_g", "ssm_out_g", "w_out", "norm2_g", "ffn_w_up", "ffn_conv_w", "ffn_conv_b",
         "ffn_w_down")
PACK_COLS = 1024


def _pack(arrs):
    flat = jnp.concatenate([a.reshape(-1) for a in arrs])
    rows = -(-flat.shape[0] // PACK_COLS)
    rows = -(-rows // 8) * 8
    flat = jnp.pad(flat, (0, rows * PACK_COLS - flat.shape[0]))
    return flat.reshape(rows, PACK_COLS)


def _unpack(packed, shapes):
    flat = packed.reshape(-1)
    out, off = [], 0
    for s in shapes:
        n = math.prod(s)
        out.append(flat[off:off + n].reshape(s))
        off += n
    return out


def kernel(x, c, ada_w, ada_b, norm1_g, w_in, q_norm_g, k_norm_g, ssm_a_re, ssm_a_im, ssm_log_dt, ssm_b_re, ssm_b_im, ssm_c_re, ssm_c_im, ssm_d, glu_w, glu_b, attn_out_g, ssm_out_g, w_out, norm2_g, ffn_w_up, ffn_conv_w, ffn_conv_b, ffn_w_down, loss_target, m_ada_w, m_ada_b, m_norm1_g, m_w_in, m_q_norm_g, m_k_norm_g, m_ssm_a_re, m_ssm_a_im, m_ssm_log_dt, m_ssm_b_re, m_ssm_b_im, m_ssm_c_re, m_ssm_c_im, m_ssm_d, m_glu_w, m_glu_b, m_attn_out_g, m_ssm_out_g, m_w_out, m_norm2_g, m_ffn_w_up, m_ffn_conv_w, m_ffn_conv_b, m_ffn_w_down, v_ada_w, v_ada_b, v_norm1_g, v_w_in, v_q_norm_g, v_k_norm_g, v_ssm_a_re, v_ssm_a_im, v_ssm_log_dt, v_ssm_b_re, v_ssm_b_im, v_ssm_c_re, v_ssm_c_im, v_ssm_d, v_glu_w, v_glu_b, v_attn_out_g, v_ssm_out_g, v_w_out, v_norm2_g, v_ffn_w_up, v_ffn_conv_w, v_ffn_conv_b, v_ffn_w_down):
    env = dict(locals())
    w = {n: env[n] for n in NAMES}
    m = {n: env["m_" + n] for n in NAMES}
    v = {n: env["v_" + n] for n in NAMES}
    mx, my, mc = _my_pos()
    chip = 2 * mx + my
    dev = 4 * mx + 2 * my + mc
    xl = x[0]
    tl = loss_target[0]

    c_all = _allgather8("gather_c", jnp.pad(c, ((0, 7), (0, 0))))[:, 0, :]
    c16 = jnp.pad(c_all, ((0, 8), (0, 0)))
    nloc = NMOD * D // NCHIP
    ada_b_loc = lax.dynamic_slice(ada_b, (0, chip * nloc), (DEPTH, nloc))[:, None, :]
    mod_loc = _mod_fwd(c16, ada_w, ada_b_loc)
    mod_g = _allgather8("gather_mod", mod_loc.reshape(DEPTH * 16, nloc))
    mod_all = mod_g[0::2].reshape(NCHIP, DEPTH, 16, nloc).transpose(1, 2, 0, 3).reshape(DEPTH, 16, NMOD * D)
    mods = lax.dynamic_index_in_dim(mod_all, dev, axis=1, keepdims=False)

    gathered = _chip_gather([_cast_bf16(w["w_in"]), _cast_bf16(w["w_out"]), _cast_bf16(w["ffn_w_up"]),
                             _cast_bf16(w["ffn_w_down"]), w["ffn_conv_w"]])
    big = dict(zip(("w_in", "w_out", "w_up", "w_down", "conv_w"), gathered))
    small = {n: w[n] for n in SMALL}

    loss_loc, grad_x, grads = _local_step(xl, tl, mods, big, small)
    loss = lax.psum(loss_loc, ("x", "y", "c"))

    def stacked(name):
        return jnp.stack([grads[l][name] for l in range(DEPTH)], axis=1)

    loc = [stacked("w_in"),
           jnp.stack([grads[l]["w_out"].reshape(NCHIP, D // NCHIP, D) for l in range(DEPTH)], axis=1),
           stacked("w_up"),
           jnp.stack([grads[l]["w_down"].reshape(NCHIP, DFF // NCHIP, D) for l in range(DEPTH)], axis=1)]
    sib = _sibling_exchange(loc)
    chip_sum = [_add2(a, b) for a, b in zip(loc, sib)]
    parts = _chip_scatter(chip_sum)
    outs = {}
    for name, part in zip(BIG, parts):
        outs[name] = _adamw(part, w[name], m[name], v[name])

    small_names = SMALL[1:] + ("ffn_conv_w_full",)
    small_grads = []
    for n in SMALL[1:]:
        key = {"ffn_conv_b": "conv_b"}.get(n, n)
        small_grads.append(jnp.stack([grads[l][key].reshape(w[n].shape[1:]) for l in range(DEPTH)]))
    small_grads.append(jnp.stack([grads[l]["conv_w"] for l in range(DEPTH)]))
    dmod = jnp.stack([grads[l]["dmod"] for l in range(DEPTH)])
    packed = _pack([dmod] + small_grads)
    allp = _allgather8("gather_small", packed)
    shapes = [dmod.shape] + [a.shape for a in small_grads]
    dmod_all = allp[:, :DEPTH * NMOD, :].reshape(NDEV, DEPTH, NMOD * D)
    dmod_loc = lax.dynamic_slice(dmod_all, (0, 0, chip * nloc), (NDEV, DEPTH, nloc)).transpose(1, 0, 2)
    g_ada = _ada_grad(c16, jnp.pad(dmod_loc, ((0, 0), (0, 8), (0, 0))))
    outs["ada_w"] = _adamw(g_ada[None], w["ada_w"], m["ada_w"], v["ada_w"])
    conv_w_shape = (DEPTH, 3, DUP)
    w_small = _pack([w[n] for n in SMALL] + [jnp.zeros(conv_w_shape, f32)])
    m_small = _pack([m[n] for n in SMALL] + [jnp.zeros(conv_w_shape, f32)])
    v_small = _pack([v[n] for n in SMALL] + [jnp.ones(conv_w_shape, f32)])
    res_small = _adamw(allp, w_small, m_small, v_small)
    unpacked = [_unpack(r, shapes) for r in res_small]
    for i, n in enumerate(SMALL):
        outs[n] = tuple(unpacked[k][i] for k in range(4))
    g_conv_full = unpacked[0][len(SMALL)]
    ncw = DUP // NCHIP
    g_conv = lax.dynamic_slice(g_conv_full, (0, 0, chip * ncw), (DEPTH, 3, ncw))
    outs["ffn_conv_w"] = _adamw(g_conv[None], w["ffn_conv_w"], m["ffn_conv_w"], v["ffn_conv_w"])

    result = [loss, grad_x[None]]
    for k in range(4):
        result += [outs[n][k] for n in NAMES]
    return tuple(result)
```

```python
import functools
import math

import jax
import jax.numpy as jnp
from jax import lax
from jax.experimental import pallas as pl
from jax.experimental.pallas import tpu as pltpu

f32 = jnp.float32
bf16 = jnp.bfloat16
_MXU = jnp.bfloat16

D = 1024
ATT = 512
SSMW = 512
HD = 64
NG = 32
NP = 64
NH16 = 16
NS = NG * NP
INW = 2048
DFF = 2816
DUP = 2 * DFF
NMOD = 6
DEPTH = 4
EPS = 1e-6
NCHIP = 4
NDEV = 8

ADAM_LR = 0.001
ADAM_B1 = 0.9
ADAM_B2 = 0.999
ADAM_EPS = 1e-08
ADAM_WD = 0.01
ADAM_STEP = 10

VMEM_LIMIT = 56 * 1024 * 1024
MESH = pl.DeviceIdType.MESH

NN = (((1,), (0,)), ((), ()))
NT = (((1,), (1,)), ((), ()))
TN = (((0,), (0,)), ((), ()))


def _cp(sem=None):
    if sem is None:
        return pltpu.CompilerParams(vmem_limit_bytes=VMEM_LIMIT)
    return pltpu.CompilerParams(dimension_semantics=sem, vmem_limit_bytes=VMEM_LIMIT)


def _dot(a, b, dims=NN):
    return lax.dot_general(a.astype(_MXU), b.astype(_MXU), dims, preferred_element_type=f32)


def _dot_exact(x, m):
    hi = x.astype(bf16)
    r1 = x - hi.astype(f32)
    mid = r1.astype(bf16)
    lo = (r1 - mid.astype(f32)).astype(bf16)
    d = lambda p: lax.dot_general(p, m, NN, preferred_element_type=f32)
    return d(hi) + d(mid) + d(lo)


def _gelu(x):
    c = math.sqrt(2.0 / math.pi)
    return 0.5 * x * (1.0 + jnp.tanh(c * (x + 0.044715 * (x * x * x))))


def _gelu_grad(x):
    c = math.sqrt(2.0 / math.pi)
    t = jnp.tanh(c * (x + 0.044715 * (x * x * x)))
    return 0.5 * (1.0 + t) + 0.5 * x * (1.0 - t * t) * c * (1.0 + 3.0 * 0.044715 * (x * x))


def _sigmoid(x):
    return 1.0 / (1.0 + jnp.exp(-x))


def _mm(name, a, b, *, dims, grid, a_block, a_map, b_block, b_map, out_shape, o_block, o_map,
        out_dtype=f32):
    nk = grid[2]
    acc_shape = tuple(s for s in o_block if s is not None)

    def body(a_ref, b_ref, o_ref, acc_ref):
        k = pl.program_id(2)
        part = _dot(a_ref[...], b_ref[...], dims)
        if nk == 1:
            o_ref[...] = part.astype(o_ref.dtype)
        else:
            @pl.when(k == 0)
            def _():
                acc_ref[...] = part

            @pl.when(k > 0)
            def _():
                acc_ref[...] += part

            @pl.when(k == nk - 1)
            def _():
                o_ref[...] = acc_ref[...].astype(o_ref.dtype)

    return pl.pallas_call(
        body, name=name, grid=grid,
        in_specs=[pl.BlockSpec(a_block, a_map), pl.BlockSpec(b_block, b_map)],
        out_specs=pl.BlockSpec(o_block, o_map),
        out_shape=jax.ShapeDtypeStruct(out_shape, out_dtype),
        scratch_shapes=[pltpu.VMEM(acc_shape if nk > 1 else (8, 128), f32)],
        compiler_params=_cp(("parallel", "parallel", "arbitrary")),
    )(a, b)


def _row_block(L, want):
    return want if L % want == 0 else L


def _rmsmod_fwd(x, res, gate, g, sh, sc):
    L = x.shape[0]
    bm = _row_block(L, 256)
    with_res = res is not None

    def body(*refs):
        if with_res:
            x_ref, r_ref, gt_ref, g_ref, sh_ref, sc_ref, xo_ref, h_ref = refs
            xin = x_ref[...] + gt_ref[...] * r_ref[...]
            xo_ref[...] = xin
        else:
            x_ref, g_ref, sh_ref, sc_ref, h_ref = refs
            xin = x_ref[...]
        inv = lax.rsqrt(jnp.mean(xin * xin, axis=-1, keepdims=True) + EPS)
        xn = xin * inv * g_ref[...]
        h_ref[...] = (xn * (1.0 + sc_ref[...]) + sh_ref[...]).astype(h_ref.dtype)

    row = pl.BlockSpec((bm, D), lambda i: (i, 0))
    vec = pl.BlockSpec((1, D), lambda i: (0, 0))
    if with_res:
        return pl.pallas_call(
            body, name="rmsmod_res_fwd", grid=(L // bm,),
            in_specs=[row, row, vec, vec, vec, vec], out_specs=[row, row],
            out_shape=[jax.ShapeDtypeStruct((L, D), f32), jax.ShapeDtypeStruct((L, D), bf16)],
            compiler_params=_cp(("parallel",)),
        )(x, res, gate, g, sh, sc)
    h = pl.pallas_call(
        body, name="rmsmod_fwd", grid=(L // bm,),
        in_specs=[row, vec, vec, vec], out_specs=row,
        out_shape=jax.ShapeDtypeStruct((L, D), bf16),
        compiler_params=_cp(("parallel",)),
    )(x, g, sh, sc)
    return None, h


def _rmsmod_bwd(dh, x, g, sc, dres):
    L = x.shape[0]
    bm = _row_block(L, 256)

    def body(dh_ref, x_ref, g_ref, sc_ref, dr_ref, dx_ref, dsh_ref, dsc_ref, dg_ref):
        i = pl.program_id(0)
        xv = x_ref[...]
        dhv = dh_ref[...]
        inv = lax.rsqrt(jnp.mean(xv * xv, axis=-1, keepdims=True) + EPS)
        xh = xv * inv
        gv = g_ref[...]
        xn = xh * gv
        dxn = dhv * (1.0 + sc_ref[...])
        dxh = dxn * gv
        dx_ref[...] = inv * (dxh - xh * jnp.mean(dxh * xh, axis=-1, keepdims=True)) + dr_ref[...]
        p_sh = jnp.sum(dhv, axis=0, keepdims=True)
        p_sc = jnp.sum(dhv * xn, axis=0, keepdims=True)
        p_g = jnp.sum(dxn * xh, axis=0, keepdims=True)

        @pl.when(i == 0)
        def _():
            dsh_ref[...] = p_sh
            dsc_ref[...] = p_sc
            dg_ref[...] = p_g

        @pl.when(i > 0)
        def _():
            dsh_ref[...] += p_sh
            dsc_ref[...] += p_sc
            dg_ref[...] += p_g

    row = pl.BlockSpec((bm, D), lambda i: (i, 0))
    vec = pl.BlockSpec((1, D), lambda i: (0, 0))
    return pl.pallas_call(
        body, name="rmsmod_bwd", grid=(L // bm,),
        in_specs=[row, row, vec, vec, row], out_specs=[row, vec, vec, vec],
        out_shape=[jax.ShapeDtypeStruct((L, D), f32)] + [jax.ShapeDtypeStruct((1, D), f32)] * 3,
        compiler_params=_cp(("arbitrary",)),
    )(dh, x, g, sc, dres)


def _gate_bwd(dx, y, gate):
    L = dx.shape[0]
    bm = _row_block(L, 256)

    def body(dx_ref, y_ref, gt_ref, dy_ref, dgt_ref):
        i = pl.program_id(0)
        dxv = dx_ref[...]
        dy_ref[...] = (gt_ref[...] * dxv).astype(dy_ref.dtype)
        part = jnp.sum(dxv * y_ref[...], axis=0, keepdims=True)

        @pl.when(i == 0)
        def _():
            dgt_ref[...] = part

        @pl.when(i > 0)
        def _():
            dgt_ref[...] += part

    row = pl.BlockSpec((bm, D), lambda i: (i, 0))
    vec = pl.BlockSpec((1, D), lambda i: (0, 0))
    return pl.pallas_call(
        body, name="gate_bwd", grid=(L // bm,),
        in_specs=[row, row, vec], out_specs=[row, vec],
        out_shape=[jax.ShapeDtypeStruct((L, D), bf16), jax.ShapeDtypeStruct((1, D), f32)],
        compiler_params=_cp(("arbitrary",)),
    )(dx, y, gate)


def _loss_fwd_bwd(x1, down, gate, target):
    L = x1.shape[0]
    bm = _row_block(L, 256)
    nsteps = L // bm

    def body(x_ref, d_ref, gt_ref, t_ref, dy_ref, loss_ref, acc_ref):
        i = pl.program_id(0)
        diff = x_ref[...] + gt_ref[...] * d_ref[...] - t_ref[...]
        dy_ref[...] = diff * (1.0 / D)
        part = jnp.sum(diff * diff, axis=0, keepdims=True)

        @pl.when(i == 0)
        def _():
            acc_ref[...] = part

        @pl.when(i > 0)
        def _():
            acc_ref[...] += part

        @pl.when(i == nsteps - 1)
        def _():
            tot = jnp.sum(acc_ref[...], axis=1, keepdims=True) * (0.5 / D)
            loss_ref[...] = jnp.broadcast_to(tot, (8, 128))

    row = pl.BlockSpec((bm, D), lambda i: (i, 0))
    vec = pl.BlockSpec((1, D), lambda i: (0, 0))
    dy, loss = pl.pallas_call(
        body, name="loss_fwd_bwd", grid=(nsteps,),
        in_specs=[row, row, vec, row],
        out_specs=[row, pl.BlockSpec((8, 128), lambda i: (0, 0))],
        out_shape=[jax.ShapeDtypeStruct((L, D), f32), jax.ShapeDtypeStruct((8, 128), f32)],
        scratch_shapes=[pltpu.VMEM((1, D), f32)],
        compiler_params=_cp(("arbitrary",)),
    )(x1, down, gate, target)
    return loss[0, 0], dy


def _head_mean_matrix():
    r = lax.broadcasted_iota(jnp.int32, (128, 128), 0) // HD
    c = lax.broadcasted_iota(jnp.int32, (128, 128), 1) // HD
    return jnp.where(r == c, 1.0 / HD, 0.0).astype(bf16)


def _qknorm_fwd(p, gqk):
    L = p.shape[0]
    bm = _row_block(L, 256)

    def body(p_ref, g_ref, o_ref):
        e = _head_mean_matrix()
        for c in range(8):
            sl = slice(128 * c, 128 * (c + 1))
            xv = p_ref[:, sl]
            inv = lax.rsqrt(_dot_exact(xv * xv, e) + EPS)
            yv = xv * inv * g_ref[:, sl]
            if c < 4:
                yv = yv * (1.0 / math.sqrt(HD))
            o_ref[:, sl] = yv.astype(o_ref.dtype)
        o_ref[:, 1024:1536] = p_ref[:, 1024:1536].astype(o_ref.dtype)

    return pl.pallas_call(
        body, name="qknorm_fwd", grid=(L // bm,),
        in_specs=[pl.BlockSpec((bm, 1536), lambda i: (i, 0)), pl.BlockSpec((1, 1024), lambda i: (0, 0))],
        out_specs=pl.BlockSpec((bm, 1536), lambda i: (i, 0)),
        out_shape=jax.ShapeDtypeStruct((L, 1536), bf16),
        compiler_params=_cp(("parallel",)),
    )(p, gqk)


def _qknorm_bwd(dqk, p, gqk):
    L = p.shape[0]
    bm = _row_block(L, 256)
    nsteps = L // bm

    def body(d_ref, p_ref, g_ref, o_ref, dg_ref, acc_ref):
        i = pl.program_id(0)
        e = _head_mean_matrix()
        for c in range(8):
            sl = slice(128 * c, 128 * (c + 1))
            xv = p_ref[:, sl]
            dv = d_ref[:, sl]
            gv = g_ref[:, sl]
            inv = lax.rsqrt(_dot_exact(xv * xv, e) + EPS)
            xh = xv * inv
            dxh = dv * gv
            o_ref[:, sl] = (inv * (dxh - xh * _dot_exact(dxh * xh, e))).astype(o_ref.dtype)
            part = jnp.sum(dv * xh, axis=0, keepdims=True)

            @pl.when(i == 0)
            def _():
                acc_ref[:, sl] = part

            @pl.when(i > 0)
            def _():
                acc_ref[:, sl] += part

        @pl.when(i == nsteps - 1)
        def _():
            r = lax.broadcasted_iota(jnp.int32, (1024, 128), 0)
            col = lax.broadcasted_iota(jnp.int32, (1024, 128), 1)
            fold = jnp.where(col == (r // 512) * HD + r % HD, 1.0, 0.0).astype(bf16)
            dg_ref[...] = _dot_exact(jnp.broadcast_to(acc_ref[...], (8, 1024)), fold)

    return pl.pallas_call(
        body, name="qknorm_bwd", grid=(nsteps,),
        in_specs=[pl.BlockSpec((bm, 1024), lambda i: (i, 0)), pl.BlockSpec((bm, 1024), lambda i: (i, 0)),
                  pl.BlockSpec((1, 1024), lambda i: (0, 0))],
        out_specs=[pl.BlockSpec((bm, 1024), lambda i: (i, 0)), pl.BlockSpec((8, 128), lambda i: (0, 0))],
        out_shape=[jax.ShapeDtypeStruct((L, 1024), bf16), jax.ShapeDtypeStruct((8, 128), f32)],
        scratch_shapes=[pltpu.VMEM((1, 1024), f32)],
        compiler_params=_cp(("arbitrary",)),
    )(dqk, p, gqk)


def _outnorm_fwd(oa, os_, ga, gs):
    L = oa.shape[0]
    bm = _row_block(L, 256)

    def body(a_ref, s_ref, ga_ref, gs_ref, o_ref):
        for x_ref, g_ref, off in ((a_ref, ga_ref, 0), (s_ref, gs_ref, 512)):
            xv = x_ref[...]
            inv = lax.rsqrt(jnp.mean(xv * xv, axis=-1, keepdims=True) + EPS)
            o_ref[:, off:off + 512] = (xv * inv * g_ref[...]).astype(o_ref.dtype)

    half = pl.BlockSpec((bm, 512), lambda i: (i, 0))
    vec = pl.BlockSpec((1, 512), lambda i: (0, 0))
    return pl.pallas_call(
        body, name="outnorm_fwd", grid=(L // bm,),
        in_specs=[half, half, vec, vec], out_specs=pl.BlockSpec((bm, D), lambda i: (i, 0)),
        out_shape=jax.ShapeDtypeStruct((L, D), bf16),
        compiler_params=_cp(("parallel",)),
    )(oa, os_, ga, gs)


def _outnorm_bwd(do, oa, os_, ga, gs):
    L = oa.shape[0]
    bm = _row_block(L, 256)

    def body(do_ref, a_ref, s_ref, ga_ref, gs_ref, da_ref, ds_ref, dga_ref, dgs_ref):
        i = pl.program_id(0)
        for x_ref, g_ref, dx_ref, dg_ref, off in ((a_ref, ga_ref, da_ref, dga_ref, 0),
                                                  (s_ref, gs_ref, ds_ref, dgs_ref, 512)):
            xv = x_ref[...]
            dv = do_ref[:, off:off + 512]
            inv = lax.rsqrt(jnp.mean(xv * xv, axis=-1, keepdims=True) + EPS)
            xh = xv * inv
            dxh = dv * g_ref[...]
            dx_ref[...] = inv * (dxh - xh * jnp.mean(dxh * xh, axis=-1, keepdims=True))
            part = jnp.sum(dv * xh, axis=0, keepdims=True)

            @pl.when(i == 0)
            def _():
                dg_ref[...] = part

            @pl.when(i > 0)
            def _():
                dg_ref[...] += part

    half = pl.BlockSpec((bm, 512), lambda i: (i, 0))
    vec = pl.BlockSpec((1, 512), lambda i: (0, 0))
    return pl.pallas_call(
        body, name="outnorm_bwd", grid=(L // bm,),
        in_specs=[pl.BlockSpec((bm, D), lambda i: (i, 0)), half, half, vec, vec],
        out_specs=[half, half, vec, vec],
        out_shape=[jax.ShapeDtypeStruct((L, 512), f32)] * 2 + [jax.ShapeDtypeStruct((1, 512), f32)] * 2,
        compiler_params=_cp(("arbitrary",)),
    )(do, oa, os_, ga, gs)


NCB = DFF // 128


def _shift_down(x, s, row):
    return jnp.where(row >= s, pltpu.roll(x, s, 0), 0.0)


def _shift_up(x, s, row, L):
    return jnp.where(row < L - s, pltpu.roll(x, L - s, 0), 0.0)


def _conv(x, w_ref, b_ref, row):
    out = b_ref[...] + _shift_down(x, 2, row) * w_ref[0:1, :]
    out = out + _shift_down(x, 1, row) * w_ref[1:2, :]
    return out + x * w_ref[2:3, :]


def _convglu_fwd(up_pre, cw, cb):
    L = up_pre.shape[0]

    def body(v_ref, g_ref, wv_ref, wg_ref, bv_ref, bg_ref, a_ref):
        row = lax.broadcasted_iota(jnp.int32, (L, 128), 0)
        val = _conv(v_ref[...], wv_ref, bv_ref, row)
        gate = _conv(g_ref[...], wg_ref, bg_ref, row)
        a_ref[...] = (_gelu(gate) * val).astype(a_ref.dtype)

    col = lambda off: pl.BlockSpec((L, 128), lambda j: (0, j + off))
    wsp = lambda off: pl.BlockSpec((3, 128), lambda j: (0, j + off))
    bsp = lambda off: pl.BlockSpec((1, 128), lambda j: (0, j + off))
    return pl.pallas_call(
        body, name="convglu_fwd", grid=(NCB,),
        in_specs=[col(0), col(NCB), wsp(0), wsp(NCB), bsp(0), bsp(NCB)],
        out_specs=pl.BlockSpec((L, 128), lambda j: (0, j)),
        out_shape=jax.ShapeDtypeStruct((L, DFF), bf16),
        compiler_params=_cp(("parallel",)),
    )(up_pre, up_pre, cw, cw, cb, cb)


def _convglu_bwd(da, up_pre, cw, cb):
    L = up_pre.shape[0]

    def body(da_ref, own_ref, par_ref, wo_ref, wp_ref, bo_ref, bp_ref, dx_ref, dw_ref, db_ref):
        j = pl.program_id(0)
        row = lax.broadcasted_iota(jnp.int32, (L, 128), 0)
        xo = own_ref[...]
        u_own = _conv(xo, wo_ref, bo_ref, row)
        u_par = _conv(par_ref[...], wp_ref, bp_ref, row)
        dav = da_ref[...]

        def finish(d):
            dx = d * wo_ref[2:3, :] + _shift_up(d, 1, row, L) * wo_ref[1:2, :]
            dx_ref[...] = (dx + _shift_up(d, 2, row, L) * wo_ref[0:1, :]).astype(dx_ref.dtype)
            dw_ref[0:1, :] = jnp.sum(d * _shift_down(xo, 2, row), axis=0, keepdims=True)
            dw_ref[1:2, :] = jnp.sum(d * _shift_down(xo, 1, row), axis=0, keepdims=True)
            dw_ref[2:3, :] = jnp.sum(d * xo, axis=0, keepdims=True)
            db_ref[...] = jnp.sum(d, axis=0, keepdims=True)

        @pl.when(j < NCB)
        def _():
            finish(dav * _gelu(u_par))

        @pl.when(j >= NCB)
        def _():
            finish(dav * u_par * _gelu_grad(u_own))

    own = lambda j: (0, j)
    par = lambda j: (0, (j + NCB) % (2 * NCB))
    return pl.pallas_call(
        body, name="convglu_bwd", grid=(2 * NCB,),
        in_specs=[pl.BlockSpec((L, 128), lambda j: (0, j % NCB)),
                  pl.BlockSpec((L, 128), own), pl.BlockSpec((L, 128), par),
                  pl.BlockSpec((3, 128), own), pl.BlockSpec((3, 128), par),
                  pl.BlockSpec((1, 128), own), pl.BlockSpec((1, 128), par)],
        out_specs=[pl.BlockSpec((L, 128), own), pl.BlockSpec((3, 128), own), pl.BlockSpec((1, 128), own)],
        out_shape=[jax.ShapeDtypeStruct((L, DUP), bf16), jax.ShapeDtypeStruct((3, DUP), f32),
                   jax.ShapeDtypeStruct((1, DUP), f32)],
        compiler_params=_cp(("parallel",)),
    )(da, up_pre, up_pre, cw, cw, cb, cb)


def _attn_block(L):
    return 256 if L % 256 == 0 else 128


TAIL_DEAD = -110.0


def _dot_exact2(x, m):
    hi = x.astype(bf16)
    lo = (x - hi.astype(f32)).astype(bf16)
    return (lax.dot_general(hi, m, NN, preferred_element_type=f32)
            + lax.dot_general(lo, m, NN, preferred_element_type=f32))


def _sb_weights(z, mask, tri_gt, carry):
    l1p = jnp.log(1.0 + jnp.exp(-jnp.abs(z)))
    ls_pos = jnp.minimum(z, 0.0) - l1p
    lm = ls_pos - z
    if mask is not None:
        lm = jnp.where(mask, lm, 0.0)
    tail = _dot_exact2(lm, tri_gt) + carry
    w = jnp.exp(ls_pos + tail)
    if mask is not None:
        w = jnp.where(mask, w, 0.0)
    return w, ls_pos, lm


def _head_masks(B):
    lane = lax.broadcasted_iota(jnp.int32, (B, 128), 1)
    return lane < HD, lane >= HD


def _attn_fwd(qkv):
    L = qkv.shape[0]
    B = _attn_block(L)
    nq = L // B

    def body(q_ref, k_ref, v_ref, o_ref):
        qi = pl.program_id(1)
        heads = _head_masks(B)
        ti = lax.broadcasted_iota(jnp.int32, (B, B), 0)
        si = lax.broadcasted_iota(jnp.int32, (B, B), 1)
        tri_gt = jnp.where(ti > si, 1.0, 0.0).astype(bf16)
        diag = si < ti
        qv = q_ref[...]
        zero16 = jnp.zeros((), bf16)

        def tile(jb, mask, acc, carries):
            ks = pl.multiple_of(jb * B, B)
            kb = k_ref[pl.ds(ks, B), :]
            vb = v_ref[pl.ds(ks, B), :]
            out = []
            for in_head, carry in zip(heads, carries):
                kh = jnp.where(in_head, kb, zero16)
                vh = jnp.where(in_head, vb, zero16)
                z = lax.dot_general(qv, kh, NT, preferred_element_type=f32)
                w, _, lm = _sb_weights(z, mask, tri_gt, carry)
                acc = acc + lax.dot_general(w.astype(bf16), vh, NN, preferred_element_type=f32)
                out.append(carry + jnp.sum(lm, axis=1, keepdims=True))
            return acc, out

        zc = jnp.zeros((B, 1), f32)
        acc, (c0, c1) = tile(qi, diag, jnp.zeros((B, 128), f32), (zc, zc))

        def cond(st):
            return jnp.logical_and(st[0] <= qi, st[4] > TAIL_DEAD)

        def step(st):
            n, a, p0, p1, _ = st
            a, (p0, p1) = tile(qi - n, None, a, (p0, p1))
            return n + 1, a, p0, p1, jnp.maximum(jnp.max(p0), jnp.max(p1))

        st = lax.while_loop(cond, step, (jnp.int32(1), acc, c0, c1, jnp.maximum(jnp.max(c0), jnp.max(c1))))
        o_ref[...] = st[1]

    return pl.pallas_call(
        body, name="attn_fwd", grid=(4, nq),
        in_specs=[pl.BlockSpec((B, 128), lambda hp, i: (i, hp)),
                  pl.BlockSpec((L, 128), lambda hp, i: (0, 4 + hp)),
                  pl.BlockSpec((L, 128), lambda hp, i: (0, 8 + hp))],
        out_specs=pl.BlockSpec((B, 128), lambda hp, i: (i, hp)),
        out_shape=jax.ShapeDtypeStruct((L, ATT), f32),
        compiler_params=_cp(("parallel", "parallel")),
    )(qkv, qkv, qkv)


def _attn_bwd(qkv, o, do):
    L = qkv.shape[0]
    B = _attn_block(L)
    nq = L // B

    def body(q_ref, k_ref, v_ref, o_ref, do_ref, dq_ref, dk_ref, dv_ref):
        qi = pl.program_id(1)

        @pl.when(qi == 0)
        def _():
            dk_ref[...] = jnp.zeros_like(dk_ref)
            dv_ref[...] = jnp.zeros_like(dv_ref)

        heads = _head_masks(B)
        ti = lax.broadcasted_iota(jnp.int32, (B, B), 0)
        si = lax.broadcasted_iota(jnp.int32, (B, B), 1)
        tri_gt = jnp.where(ti > si, 1.0, 0.0).astype(bf16)
        tri_ge = jnp.where(ti >= si, 1.0, 0.0).astype(bf16)
        diag = si < ti
        zero16 = jnp.zeros((), bf16)
        qv = q_ref[...]
        do16 = do_ref[...].astype(bf16)
        dsum_lanes = do16.astype(f32) * o_ref[...]
        qhs = [jnp.where(m, qv, zero16) for m in heads]
        dohs = [jnp.where(m, do16, zero16) for m in heads]
        dsums = [jnp.sum(jnp.where(m, dsum_lanes, 0.0), axis=1, keepdims=True) for m in heads]

        def tile(jb, mask, dq_a, state):
            ks = pl.multiple_of(jb * B, B)
            kb = k_ref[pl.ds(ks, B), :]
            vb = v_ref[pl.ds(ks, B), :]
            dk_blk = jnp.zeros((B, 128), f32)
            dv_blk = jnp.zeros((B, 128), f32)
            out = []
            for in_head, qh, doh, dsum, (carry, suffix) in zip(heads, qhs, dohs, dsums, state):
                kh = jnp.where(in_head, kb, zero16)
                vh = jnp.where(in_head, vb, zero16)
                z = lax.dot_general(qv, kh, NT, preferred_element_type=f32)
                w, ls_pos, lm = _sb_weights(z, mask, tri_gt, carry)
                w16 = w.astype(bf16)
                dw = lax.dot_general(doh, vh, NT, preferred_element_type=f32)
                da = w16.astype(f32) * dw
                sig = jnp.exp(ls_pos)
                suf = _dot_exact2(da, tri_ge) + suffix
                dz = da * (1.0 - sig) - sig * (dsum - suf)
                if mask is not None:
                    dz = jnp.where(mask, dz, 0.0)
                dz16 = dz.astype(bf16)
                dq_a = dq_a + lax.dot_general(dz16, kh, NN, preferred_element_type=f32)
                dk_blk = dk_blk + lax.dot_general(dz16, qh, TN, preferred_element_type=f32)
                dv_blk = dv_blk + lax.dot_general(w16, doh, TN, preferred_element_type=f32)
                out.append((carry + jnp.sum(lm, axis=1, keepdims=True),
                            suffix + jnp.sum(da, axis=1, keepdims=True)))
            dk_ref[pl.ds(ks, B), :] += dk_blk
            dv_ref[pl.ds(ks, B), :] += dv_blk
            return dq_a, out

        def alive(state):
            return jnp.maximum(jnp.max(state[0][0]), jnp.max(state[1][0]))

        zc = jnp.zeros((B, 1), f32)
        dq_acc, state = tile(qi, diag, jnp.zeros((B, 128), f32), ((zc, zc), (zc, zc)))

        def cond(st):
            return jnp.logical_and(st[0] <= qi, st[6] > TAIL_DEAD)

        def step(st):
            n, a, c0, s0, c1, s1, _ = st
            a, new = tile(qi - n, None, a, ((c0, s0), (c1, s1)))
            return n + 1, a, new[0][0], new[0][1], new[1][0], new[1][1], alive(new)

        st = lax.while_loop(cond, step, (jnp.int32(1), dq_acc, state[0][0], state[0][1],
                                         state[1][0], state[1][1], alive(state)))
        dq_ref[...] = st[1] * (1.0 / math.sqrt(HD))

    blk = pl.BlockSpec((B, 128), lambda hp, i: (i, hp))
    dqkv = pl.pallas_call(
        body, name="attn_bwd", grid=(4, nq),
        in_specs=[blk,
                  pl.BlockSpec((L, 128), lambda hp, i: (0, 4 + hp)),
                  pl.BlockSpec((L, 128), lambda hp, i: (0, 8 + hp)),
                  blk, blk],
        out_specs=[blk, pl.BlockSpec((L, 128), lambda hp, i: (0, hp)),
                   pl.BlockSpec((L, 128), lambda hp, i: (0, hp))],
        out_shape=[jax.ShapeDtypeStruct((L, ATT), f32)] * 3,
        compiler_params=_cp(("parallel", "arbitrary")),
    )(qkv, qkv, qkv, o, do)
    return dqkv


def _s5_disc(ldt, ar, ai, br, bi):
    dt = jnp.exp(ldt)
    mag = jnp.exp(dt * ar)
    abar_re = mag * jnp.cos(dt * ai)
    abar_im = mag * jnp.sin(dt * ai)
    em_re = abar_re - 1.0
    em_im = abar_im
    den = ar * ar + ai * ai
    f_re = (em_re * ar + em_im * ai) / den
    f_im = (em_im * ar - em_re * ai) / den
    bb_re = f_re * br - f_im * bi
    bb_im = f_re * bi + f_im * br
    return abar_re, abar_im, bb_re, bb_im


def _s5_disc_fwd(ldt, ar, ai, br, bi):
    def body(ldt_ref, ar_ref, ai_ref, br_ref, bi_ref, o1, o2, o3, o4):
        outs = _s5_disc(ldt_ref[...], ar_ref[...], ai_ref[...], br_ref[...], bi_ref[...])
        for o_ref, v in zip((o1, o2, o3, o4), outs):
            o_ref[...] = v

    col = jax.ShapeDtypeStruct((NS, 1), f32)
    mat = jax.ShapeDtypeStruct((NS, NH16), f32)
    return pl.pallas_call(body, name="s5_disc_fwd", out_shape=[col, col, mat, mat],
                          compiler_params=_cp())(ldt, ar, ai, br, bi)


def _s5_disc_bwd(ldt, ar, ai, br, bi, d_are, d_aim, d_bbre, d_bbim):
    def body(ldt_ref, ar_ref, ai_ref, br_ref, bi_ref, c1, c2, c3, c4, o_ldt, o_ar, o_ai, o_br, o_bi):
        prim = (ldt_ref[...], ar_ref[...], ai_ref[...], br_ref[...], bi_ref[...])
        _, vjp = jax.vjp(_s5_disc, *prim)
        g_ldt, g_ar, g_ai, g_br, g_bi = vjp((c1[...], c2[...], c3[...], c4[...]))
        o_ar[...] = g_ar
        o_ai[...] = g_ai
        o_br[...] = g_br
        o_bi[...] = g_bi
        r = lax.broadcasted_iota(jnp.int32, (NG, NS), 0)
        c = lax.broadcasted_iota(jnp.int32, (NG, NS), 1)
        fold = jnp.where(c // NP == r, 1.0, 0.0).astype(bf16)
        hi = jnp.broadcast_to(g_ldt, (NS, 128))
        p1 = hi.astype(bf16)
        r1 = hi - p1.astype(f32)
        p2 = r1.astype(bf16)
        p3 = (r1 - p2.astype(f32)).astype(bf16)
        dd = lambda p: lax.dot_general(fold, p, NN, preferred_element_type=f32)
        o_ldt[...] = dd(p1) + dd(p2) + dd(p3)

    col = jax.ShapeDtypeStruct((NS, 1), f32)
    mat = jax.ShapeDtypeStruct((NS, NH16), f32)
    return pl.pallas_call(
        body, name="s5_disc_bwd",
        out_shape=[jax.ShapeDtypeStruct((NG, 128), f32), col, col, mat, mat],
        compiler_params=_cp())(ldt, ar, ai, br, bi, d_are, d_aim, d_bbre, d_bbim)


SCAN_LANES = 256


def _cmul(ar, ai, br, bi):
    return ar * br - ai * bi, ar * bi + ai * br


def _pow_tables(ar, ai):
    a2 = _cmul(ar, ai, ar, ai)
    a4 = _cmul(*a2, *a2)
    rows = [(ar, ai)]
    for _ in range(7):
        rows.append(_cmul(*rows[-1], ar, ai))
    return (ar, ai), a2, a4, rows


def _scan_rows(xr_ref, xi_ref, abar_re, abar_im, cr_ref, ci_ref, reverse, on_tile=None):
    T = xr_ref.shape[0]
    nt = T // 8
    row = lax.broadcasted_iota(jnp.int32, (8, SCAN_LANES), 0)
    for c in range(NS // SCAN_LANES):
        ls = slice(c * SCAN_LANES, (c + 1) * SCAN_LANES)
        ar = abar_re[:, ls]
        ai = abar_im[:, ls]
        if reverse:
            ai = -ai
        a1, a2, a4, prow = _pow_tables(ar, ai)
        if reverse:
            prow = prow[::-1]
        pr = jnp.concatenate([p[0] for p in prow], axis=0)
        pi = jnp.concatenate([p[1] for p in prow], axis=0)
        steps = [(1, a1), (2, a2), (4, a4)]

        def tile(n, carry):
            cr, ci = carry
            it = (nt - 1 - n) if reverse else n
            rs = pl.ds(pl.multiple_of(it * 8, 8), 8)
            xr = xr_ref[rs, ls]
            xi = xi_ref[rs, ls]
            for d, (dr, di) in steps:
                if reverse:
                    keep = row < 8 - d
                    sr = jnp.where(keep, pltpu.roll(xr, 8 - d, 0), 0.0)
                    si = jnp.where(keep, pltpu.roll(xi, 8 - d, 0), 0.0)
                else:
                    keep = row >= d
                    sr = jnp.where(keep, pltpu.roll(xr, d, 0), 0.0)
                    si = jnp.where(keep, pltpu.roll(xi, d, 0), 0.0)
                mr, mi = _cmul(dr, di, sr, si)
                xr = xr + mr
                xi = xi + mi
            mr, mi = _cmul(pr, pi, cr, ci)
            xr = xr + mr
            xi = xi + mi
            xr_ref[rs, ls] = xr
            xi_ref[rs, ls] = xi
            if on_tile is not None:
                nxt_r = jnp.where(row < 7, pltpu.roll(xr, 7, 0), cr)
                nxt_i = jnp.where(row < 7, pltpu.roll(xi, 7, 0), ci)
                on_tile(ls, rs, nxt_r, nxt_i)
            edge = slice(0, 1) if reverse else slice(7, 8)
            return xr[edge, :], xi[edge, :]

        cr, ci = lax.fori_loop(0, nt, tile, (cr_ref[:, ls], ci_ref[:, ls]))
        cr_ref[:, ls] = cr
        ci_ref[:, ls] = ci


def _s5_chunk(L):
    return 256 if L % 256 == 0 else L


def _s5_gate(s_re, s_im, u, ccr_ref, cci_ref, dsk_ref, gw_ref, gb_ref):
    y = _dot(s_re, ccr_ref[...]) - _dot(s_im, cci_ref[...]) + dsk_ref[...] * u
    yg = _gelu(y)
    gate = _sigmoid(_dot(yg, gw_ref[...]) + gb_ref[...])
    return y, yg, gate


def _s5_fwd(p, abar_re, abar_im, bbr, bbi, ccr, cci, dsk, gw, gb):
    L = p.shape[0]
    T = _s5_chunk(L)

    def body(u_ref, are_ref, aim_ref, bbr_ref, bbi_ref, ccr_ref, cci_ref, dsk_ref, gw_ref, gb_ref,
             o_ref, sr_ref, si_ref, cr_ref, ci_ref):
        @pl.when(pl.program_id(0) == 0)
        def _():
            cr_ref[...] = jnp.zeros_like(cr_ref)
            ci_ref[...] = jnp.zeros_like(ci_ref)

        u = u_ref[...]
        sr_ref[...] = _dot(u, bbr_ref[...])
        si_ref[...] = _dot(u, bbi_ref[...])
        _scan_rows(sr_ref, si_ref, are_ref[...], aim_ref[...], cr_ref, ci_ref, reverse=False)
        _, yg, gate = _s5_gate(sr_ref[...], si_ref[...], u, ccr_ref, cci_ref, dsk_ref, gw_ref, gb_ref)
        o_ref[...] = yg * gate

    full = lambda shape: pl.BlockSpec(shape, lambda i: (0, 0))
    return pl.pallas_call(
        body, name="s5_fwd", grid=(L // T,),
        in_specs=[pl.BlockSpec((T, SSMW), lambda i: (i, 3)), full((1, NS)), full((1, NS)),
                  full((SSMW, NS)), full((SSMW, NS)), full((NS, SSMW)), full((NS, SSMW)),
                  full((1, SSMW)), full((SSMW, SSMW)), full((1, SSMW))],
        out_specs=[pl.BlockSpec((T, SSMW), lambda i: (i, 0)), pl.BlockSpec((T, NS), lambda i: (i, 0)),
                   pl.BlockSpec((T, NS), lambda i: (i, 0))],
        out_shape=[jax.ShapeDtypeStruct((L, SSMW), f32), jax.ShapeDtypeStruct((L, NS), f32),
                   jax.ShapeDtypeStruct((L, NS), f32)],
        scratch_shapes=[pltpu.VMEM((1, NS), f32), pltpu.VMEM((1, NS), f32)],
        compiler_params=_cp(("arbitrary",)),
    )(p, abar_re, abar_im, bbr, bbi, ccr, cci, dsk, gw, gb)


def _s5_bwd(dout, p, s_re, s_im, abar_re, abar_im, bbr, bbi, ccr, cci, dsk, gw, gb):
    L = p.shape[0]
    T = _s5_chunk(L)
    nchunk = L // T

    def body(do_ref, u_ref, sr_ref, si_ref, are_ref, aim_ref, bbr_ref, bbi_ref, ccr_ref, cci_ref,
             dsk_ref, gw_ref, gb_ref,
             du_ref, dy_ref, yg_ref, dgl_ref, lr16_ref, li16_ref, dare_ref, daim_ref, dd_ref, dgb_ref,
             lr_ref, li_ref, cr_ref, ci_ref, accr_ref, acci_ref):
        i = pl.program_id(0)

        @pl.when(i == 0)
        def _():
            cr_ref[...] = jnp.zeros_like(cr_ref)
            ci_ref[...] = jnp.zeros_like(ci_ref)
            accr_ref[...] = jnp.zeros_like(accr_ref)
            acci_ref[...] = jnp.zeros_like(acci_ref)
            dd_ref[...] = jnp.zeros_like(dd_ref)
            dgb_ref[...] = jnp.zeros_like(dgb_ref)

        u = u_ref[...]
        dov = do_ref[...]
        y, yg, gate = _s5_gate(sr_ref[...], si_ref[...], u, ccr_ref, cci_ref, dsk_ref, gw_ref, gb_ref)
        dgl = dov * yg * gate * (1.0 - gate)
        dyg = dov * gate + _dot(dgl, gw_ref[...], NT)
        dy = dyg * _gelu_grad(y)
        dy_ref[...] = dy.astype(dy_ref.dtype)
        yg_ref[...] = yg.astype(yg_ref.dtype)
        dgl_ref[...] = dgl.astype(dgl_ref.dtype)
        dd_ref[...] += jnp.sum(dy * u, axis=0, keepdims=True)
        dgb_ref[...] += jnp.sum(dgl, axis=0, keepdims=True)
        lr_ref[...] = _dot(dy, ccr_ref[...], NT)
        li_ref[...] = -_dot(dy, cci_ref[...], NT)

        def on_tile(ls, rs, nxt_r, nxt_i):
            s_r = sr_ref[rs, ls]
            s_i = si_ref[rs, ls]
            accr_ref[:, ls] += nxt_r * s_r + nxt_i * s_i
            acci_ref[:, ls] += nxt_i * s_r - nxt_r * s_i

        _scan_rows(lr_ref, li_ref, are_ref[...], aim_ref[...], cr_ref, ci_ref, reverse=True,
                   on_tile=on_tile)
        lam_r = lr_ref[...]
        lam_i = li_ref[...]
        lr16_ref[...] = lam_r.astype(lr16_ref.dtype)
        li16_ref[...] = lam_i.astype(li16_ref.dtype)
        du_ref[...] = dy * dsk_ref[...] + _dot(lam_r, bbr_ref[...], NT) + _dot(lam_i, bbi_ref[...], NT)

        @pl.when(i == nchunk - 1)
        def _():
            dare_ref[...] = jnp.sum(accr_ref[...], axis=0, keepdims=True)
            daim_ref[...] = jnp.sum(acci_ref[...], axis=0, keepdims=True)

    rev = lambda i: (nchunk - 1 - i, 0)
    full = lambda shape: pl.BlockSpec(shape, lambda i: (0, 0))
    half = pl.BlockSpec((T, SSMW), rev)
    wide = pl.BlockSpec((T, NS), rev)
    return pl.pallas_call(
        body, name="s5_bwd", grid=(nchunk,),
        in_specs=[half, pl.BlockSpec((T, SSMW), lambda i: (nchunk - 1 - i, 3)), wide, wide,
                  full((1, NS)), full((1, NS)), full((SSMW, NS)), full((SSMW, NS)),
                  full((NS, SSMW)), full((NS, SSMW)), full((1, SSMW)), full((SSMW, SSMW)), full((1, SSMW))],
        out_specs=[half, half, half, half, wide, wide, full((1, NS)), full((1, NS)),
                   full((1, SSMW)), full((1, SSMW))],
        out_shape=[jax.ShapeDtypeStruct((L, SSMW), f32)] + [jax.ShapeDtypeStruct((L, SSMW), bf16)] * 3
                  + [jax.ShapeDtypeStruct((L, NS), bf16)] * 2 + [jax.ShapeDtypeStruct((1, NS), f32)] * 2
                  + [jax.ShapeDtypeStruct((1, SSMW), f32)] * 2,
        scratch_shapes=[pltpu.VMEM((T, NS), f32), pltpu.VMEM((T, NS), f32),
                        pltpu.VMEM((1, NS), f32), pltpu.VMEM((1, NS), f32),
                        pltpu.VMEM((8, NS), f32), pltpu.VMEM((8, NS), f32)],
        compiler_params=_cp(("arbitrary",)),
    )(dout, p, s_re, s_im, abar_re, abar_im, bbr, bbi, ccr, cci, dsk, gw, gb)


def _block_diag(x):
    g, r, c = x.shape
    eye = jnp.eye(g, dtype=x.dtype)
    return (x[:, :, None, :] * eye[:, None, :, None]).reshape(g * r, g * c)


def _diag_blocks(dense, r, c):
    g = dense.shape[0] // r
    return jnp.stack([dense[k * r:(k + 1) * r, k * c:(k + 1) * c] for k in range(g)])


def _mod_fwd(c16, ada_w, ada_b_loc):
    nloc = ada_w.shape[2]

    def body(c_ref, w_ref, b_ref, o_ref):
        cv = c_ref[...]
        act = cv * _sigmoid(cv)
        o_ref[...] = _dot(act, w_ref[...]) + b_ref[...]

    return pl.pallas_call(
        body, name="mod_fwd", grid=(DEPTH,),
        in_specs=[pl.BlockSpec((16, D), lambda l: (0, 0)), pl.BlockSpec((None, D, nloc), lambda l: (l, 0, 0)),
                  pl.BlockSpec((None, 1, nloc), lambda l: (l, 0, 0))],
        out_specs=pl.BlockSpec((None, 16, nloc), lambda l: (l, 0, 0)),
        out_shape=jax.ShapeDtypeStruct((DEPTH, 16, nloc), f32),
        compiler_params=_cp(("parallel",)),
    )(c16, ada_w, ada_b_loc)


def _ada_grad(c16, dmod16):
    nloc = dmod16.shape[2]

    def body(c_ref, d_ref, o_ref):
        cv = c_ref[...]
        act = cv * _sigmoid(cv)
        o_ref[...] = _dot(act, d_ref[...], TN)

    return pl.pallas_call(
        body, name="ada_grad", grid=(DEPTH,),
        in_specs=[pl.BlockSpec((16, D), lambda l: (0, 0)), pl.BlockSpec((None, 16, nloc), lambda l: (l, 0, 0))],
        out_specs=pl.BlockSpec((None, D, nloc), lambda l: (l, 0, 0)),
        out_shape=jax.ShapeDtypeStruct((DEPTH, D, nloc), f32),
        compiler_params=_cp(("parallel",)),
    )(c16, dmod16)


def _as2d(a):
    return a.reshape(-1, a.shape[-1])


def _ew_rows(rows):
    for cand in (512, 256, 128, 64, 32, 16, 8):
        if rows % cand == 0:
            return cand
    return rows


def _cast_bf16(w):
    w2 = _as2d(w)
    rows, cols = w2.shape
    bm = _ew_rows(rows)

    def body(x_ref, o_ref):
        o_ref[...] = x_ref[...].astype(bf16)

    spec = pl.BlockSpec((bm, cols), lambda i: (i, 0))
    out = pl.pallas_call(body, name="cast_bf16", grid=(rows // bm,), in_specs=[spec], out_specs=spec,
                         out_shape=jax.ShapeDtypeStruct((rows, cols), bf16),
                         compiler_params=_cp(("parallel",)))(w2)
    return out.reshape(w.shape)


def _add2(a, b):
    a2, b2 = _as2d(a), _as2d(b)
    rows, cols = a2.shape
    bm = _ew_rows(rows)

    def body(a_ref, b_ref, o_ref):
        o_ref[...] = a_ref[...] + b_ref[...]

    spec = pl.BlockSpec((bm, cols), lambda i: (i, 0))
    out = pl.pallas_call(body, name="add2", grid=(rows // bm,), in_specs=[spec, spec], out_specs=spec,
                         out_shape=jax.ShapeDtypeStruct((rows, cols), f32),
                         compiler_params=_cp(("parallel",)))(a2, b2)
    return out.reshape(a.shape)


def _adamw(parts, w, m, v):
    n = parts.shape[0]
    w2, m2, v2 = _as2d(w), _as2d(m), _as2d(v)
    rows, cols = w2.shape
    p3 = parts.reshape(n, rows, cols)
    bm = _ew_rows(rows)
    if bm * cols * 4 > (1 << 20) and bm % 16 == 0:
        bm //= 2

    def body(p_ref, w_ref, m_ref, v_ref, g_ref, d_ref, nm_ref, nv_ref):
        g = p_ref[0]
        for k in range(1, n):
            g = g + p_ref[k]
        mn = ADAM_B1 * m_ref[...] + (1.0 - ADAM_B1) * g
        vn = ADAM_B2 * v_ref[...] + (1.0 - ADAM_B2) * (g * g)
        m_hat = mn / (1.0 - ADAM_B1 ** ADAM_STEP)
        v_hat = vn / (1.0 - ADAM_B2 ** ADAM_STEP)
        g_ref[...] = g
        d_ref[...] = -ADAM_LR * (m_hat / (jnp.sqrt(v_hat) + ADAM_EPS) + ADAM_WD * w_ref[...])
        nm_ref[...] = mn
        nv_ref[...] = vn

    spec = pl.BlockSpec((bm, cols), lambda i: (i, 0))
    outs = pl.pallas_call(
        body, name="adamw", grid=(rows // bm,),
        in_specs=[pl.BlockSpec((n, bm, cols), lambda i: (0, i, 0)), spec, spec, spec],
        out_specs=[spec] * 4, out_shape=[jax.ShapeDtypeStruct((rows, cols), f32)] * 4,
        compiler_params=_cp(("parallel",)),
    )(p3, w2, m2, v2)
    return tuple(o.reshape(w.shape) for o in outs)


ANY = pl.BlockSpec(memory_space=pl.ANY)


def _my_pos():
    return lax.axis_index("x"), lax.axis_index("y"), lax.axis_index("c")


def _allgather8(name, x):
    def body(x_ref, out_ref, send_sems, recv_sems, local_sem):
        mx, my, mc = _my_pos()
        me, sibling = (mx, my, mc), (mx, my, 1 - mc)
        chips = [(1 - mx, my), (mx, 1 - my), (1 - mx, 1 - my)]

        def slot(px, py, pc):
            return out_ref.at[4 * px + 2 * py + pc]

        def copy(k, block, to, src=None):
            return pltpu.make_async_remote_copy(
                src_ref=slot(*block) if src is None else src, dst_ref=slot(*block),
                send_sem=send_sems.at[k], recv_sem=recv_sems.at[k], device_id=to, device_id_type=MESH)

        mine = pltpu.make_async_copy(x_ref, slot(*me), local_sem)
        mine.start()
        first = [copy(0, me, sibling, src=x_ref)]
        first += [copy(1 + j, me, (*chip, mc), src=x_ref) for j, chip in enumerate(chips)]
        for cp in first:
            cp.start()
        passed = [copy(4 + j, (*chip, mc), sibling) for j, chip in enumerate(chips)]
        for j, chip in enumerate(chips):
            copy(1 + j, (*chip, mc), me).wait_recv()
            passed[j].start()
        copy(0, sibling, me).wait_recv()
        for j, chip in enumerate(chips):
            copy(4 + j, (*chip, 1 - mc), me).wait_recv()
        for cp in first + passed:
            cp.wait_send()
        mine.wait()

    return pl.pallas_call(
        body, name=name, in_specs=[ANY], out_specs=ANY,
        out_shape=jax.ShapeDtypeStruct((NDEV,) + x.shape, x.dtype),
        scratch_shapes=[pltpu.SemaphoreType.DMA((7,)), pltpu.SemaphoreType.DMA((7,)), pltpu.SemaphoreType.DMA],
    )(x)


def _chip_gather(arrs):
    n = len(arrs)

    def body(*refs):
        ins, outs = refs[:n], refs[n:2 * n]
        send_sems, recv_sems, local_sems = refs[2 * n:]
        mx, my, mc = _my_pos()
        chips = [(1 - mx, my), (mx, 1 - my), (1 - mx, 1 - my)]
        mine = 2 * mx + my
        local = [pltpu.make_async_copy(ins[a], outs[a].at[mine], local_sems.at[a]) for a in range(n)]
        for cp in local:
            cp.start()
        sends = []
        for a in range(n):
            for j, (px, py) in enumerate(chips):
                sends.append(pltpu.make_async_remote_copy(
                    src_ref=ins[a], dst_ref=outs[a].at[mine],
                    send_sem=send_sems.at[3 * a + j], recv_sem=recv_sems.at[3 * a + j],
                    device_id=(px, py, mc), device_id_type=MESH))
        for cp in sends:
            cp.start()
        for a in range(n):
            for j, (px, py) in enumerate(chips):
                pltpu.make_async_remote_copy(
                    src_ref=ins[a], dst_ref=outs[a].at[2 * px + py],
                    send_sem=send_sems.at[3 * a + j], recv_sem=recv_sems.at[3 * a + j],
                    device_id=(px, py, mc), device_id_type=MESH).wait_recv()
        for cp in sends:
            cp.wait_send()
        for cp in local:
            cp.wait()

    return pl.pallas_call(
        body, name="chip_gather", in_specs=[ANY] * n, out_specs=[ANY] * n,
        out_shape=[jax.ShapeDtypeStruct((NCHIP,) + a.shape, a.dtype) for a in arrs],
        scratch_shapes=[pltpu.SemaphoreType.DMA((3 * n,)), pltpu.SemaphoreType.DMA((3 * n,)),
                        pltpu.SemaphoreType.DMA((n,))],
    )(*arrs)


def _sibling_exchange(arrs):
    n = len(arrs)

    def body(*refs):
        ins, outs = refs[:n], refs[n:2 * n]
        send_sems, recv_sems = refs[2 * n:]
        mx, my, mc = _my_pos()
        cps = [pltpu.make_async_remote_copy(
            src_ref=ins[a], dst_ref=outs[a], send_sem=send_sems.at[a], recv_sem=recv_sems.at[a],
            device_id=(mx, my, 1 - mc), device_id_type=MESH) for a in range(n)]
        for cp in cps:
            cp.start()
        for cp in cps:
            cp.wait()

    return pl.pallas_call(
        body, name="sibling_exchange", in_specs=[ANY] * n, out_specs=[ANY] * n,
        out_shape=[jax.ShapeDtypeStruct(a.shape, a.dtype) for a in arrs],
        scratch_shapes=[pltpu.SemaphoreType.DMA((n,)), pltpu.SemaphoreType.DMA((n,))],
    )(*arrs)


def _chip_scatter(arrs):
    n = len(arrs)

    def body(*refs):
        ins, outs = refs[:n], refs[n:2 * n]
        send_sems, recv_sems, local_sems = refs[2 * n:]
        mx, my, mc = _my_pos()
        chips = [(1 - mx, my), (mx, 1 - my), (1 - mx, 1 - my)]
        mine = 2 * mx + my
        local = [pltpu.make_async_copy(ins[a].at[mine], outs[a].at[mine], local_sems.at[a]) for a in range(n)]
        for cp in local:
            cp.start()
        sends = []
        for a in range(n):
            for j, (px, py) in enumerate(chips):
                sends.append(pltpu.make_async_remote_copy(
                    src_ref=ins[a].at[2 * px + py], dst_ref=outs[a].at[mine],
                    send_sem=send_sems.at[3 * a + j], recv_sem=recv_sems.at[3 * a + j],
                    device_id=(px, py, mc), device_id_type=MESH))
        for cp in sends:
            cp.start()
        for a in range(n):
            for j, (px, py) in enumerate(chips):
                pltpu.make_async_remote_copy(
                    src_ref=ins[a].at[mine], dst_ref=outs[a].at[2 * px + py],
                    send_sem=send_sems.at[3 * a + j], recv_sem=recv_sems.at[3 * a + j],
                    device_id=(px, py, mc), device_id_type=MESH).wait_recv()
        for cp in sends:
            cp.wait_send()
        for cp in local:
            cp.wait()

    return pl.pallas_call(
        body, name="chip_scatter", in_specs=[ANY] * n, out_specs=[ANY] * n,
        out_shape=[jax.ShapeDtypeStruct(a.shape, a.dtype) for a in arrs],
        scratch_shapes=[pltpu.SemaphoreType.DMA((3 * n,)), pltpu.SemaphoreType.DMA((3 * n,)),
                        pltpu.SemaphoreType.DMA((n,))],
    )(*arrs)


def _layer_fwd(x, res, res_gate, wl, mod):
    L = x.shape[0]
    bm = _row_block(L, 512)
    nb = L // bm
    sh1, sc1, g1, sh2, sc2, g2 = [mod[k * D:(k + 1) * D][None, :] for k in range(NMOD)]
    sv = {}
    xin, h = _rmsmod_fwd(x, res, res_gate, wl["norm1_g"], sh1, sc1)
    if xin is None:
        xin = x
    p = _mm("mm_in", h, wl["w_in"], dims=NN, grid=(nb, 4, 1),
            a_block=(bm, D), a_map=lambda i, j, k: (i, 0),
            b_block=(None, D, 512), b_map=lambda i, j, k: (j, 0, 0),
            out_shape=(L, INW), o_block=(bm, 512), o_map=lambda i, j, k: (i, j))
    qkv = _qknorm_fwd(p, wl["gqk"])
    oa = _attn_fwd(qkv)
    os_, s_re, s_im = _s5_fwd(p, wl["abar_re"], wl["abar_im"], wl["bbr"], wl["bbi"], wl["ccr"], wl["cci"],
                              wl["dsk"], wl["gw"], wl["gb"])
    o = _outnorm_fwd(oa, os_, wl["attn_out_g"], wl["ssm_out_g"])
    proj = _mm("mm_out", o, wl["w_out"], dims=NN, grid=(nb, 2, 1),
               a_block=(bm, D), a_map=lambda i, j, k: (i, 0),
               b_block=(D, 512), b_map=lambda i, j, k: (0, j),
               out_shape=(L, D), o_block=(bm, 512), o_map=lambda i, j, k: (i, j))
    x1, h2 = _rmsmod_fwd(xin, proj, g1, wl["norm2_g"], sh2, sc2)
    up_pre = _mm("mm_up", h2, wl["w_up"], dims=NN, grid=(nb, 4, 1),
                 a_block=(bm, D), a_map=lambda i, j, k: (i, 0),
                 b_block=(None, D, 1408), b_map=lambda i, j, k: (j, 0, 0),
                 out_shape=(L, DUP), o_block=(bm, 1408), o_map=lambda i, j, k: (i, j))
    a = _convglu_fwd(up_pre, wl["conv_w"], wl["conv_b"])
    down = _mm("mm_down", a, wl["w_down"], dims=NN, grid=(nb, 2, 1),
               a_block=(bm, DFF), a_map=lambda i, j, k: (i, 0),
               b_block=(DFF, 512), b_map=lambda i, j, k: (0, j),
               out_shape=(L, D), o_block=(bm, 512), o_map=lambda i, j, k: (i, j))
    sv.update(xin=xin, h=h, p=p, qkv=qkv, oa=oa, os=os_, s_re=s_re, s_im=s_im, o=o, proj=proj,
              x1=x1, h2=h2, up_pre=up_pre, a=a, down=down,
              sc1=sc1, g1=g1, sc2=sc2, g2=g2)
    return x1, down, g2, sv


def _layer_bwd(dx_out, wl, sv):
    L = dx_out.shape[0]
    bm = _row_block(L, 512)
    nb = L // bm
    bk = _row_block(L, 512)
    nk = L // bk
    g = {}
    dd, dg2 = _gate_bwd(dx_out, sv["down"], sv["g2"])
    da = _mm("mm_down_dx", dd, wl["w_down"], dims=NT, grid=(nb, 2, 1),
             a_block=(bm, D), a_map=lambda i, j, k: (i, 0),
             b_block=(1408, D), b_map=lambda i, j, k: (j, 0),
             out_shape=(L, DFF), o_block=(bm, 1408), o_map=lambda i, j, k: (i, j))
    g["w_down"] = _mm("mm_down_dw", sv["a"], dd, dims=TN, grid=(2, 2, nk),
                      a_block=(bk, 1408), a_map=lambda i, j, k: (k, i),
                      b_block=(bk, 512), b_map=lambda i, j, k: (k, j),
                      out_shape=(DFF, D), o_block=(1408, 512), o_map=lambda i, j, k: (i, j))
    dup, g["conv_w"], g["conv_b"] = _convglu_bwd(da, sv["up_pre"], wl["conv_w"], wl["conv_b"])
    dh2 = _mm("mm_up_dx", dup, wl["w_up"], dims=NT, grid=(nb, 1, 4),
              a_block=(bm, 1408), a_map=lambda i, j, k: (i, k),
              b_block=(None, D, 1408), b_map=lambda i, j, k: (k, 0, 0),
              out_shape=(L, D), o_block=(bm, D), o_map=lambda i, j, k: (i, 0))
    g["w_up"] = _mm("mm_up_dw", sv["h2"], dup, dims=TN, grid=(2, 4, nk),
                    a_block=(bk, 512), a_map=lambda i, j, k: (k, i),
                    b_block=(bk, 1408), b_map=lambda i, j, k: (k, j),
                    out_shape=(NCHIP, D, 1408), o_block=(None, 512, 1408), o_map=lambda i, j, k: (j, i, 0))
    dx1, dsh2, dsc2, g["norm2_g"] = _rmsmod_bwd(dh2, sv["x1"], wl["norm2_g"], sv["sc2"], dx_out)
    dproj, dg1 = _gate_bwd(dx1, sv["proj"], sv["g1"])
    do = _mm("mm_out_dx", dproj, wl["w_out"], dims=NT, grid=(nb, 2, 1),
             a_block=(bm, D), a_map=lambda i, j, k: (i, 0),
             b_block=(512, D), b_map=lambda i, j, k: (j, 0),
             out_shape=(L, D), o_block=(bm, 512), o_map=lambda i, j, k: (i, j))
    g["w_out"] = _mm("mm_out_dw", sv["o"], dproj, dims=TN, grid=(2, 2, nk),
                     a_block=(bk, 512), a_map=lambda i, j, k: (k, i),
                     b_block=(bk, 512), b_map=lambda i, j, k: (k, j),
                     out_shape=(D, D), o_block=(512, 512), o_map=lambda i, j, k: (i, j))
    doa, dos, g["attn_out_g"], g["ssm_out_g"] = _outnorm_bwd(do, sv["oa"], sv["os"],
                                                             wl["attn_out_g"], wl["ssm_out_g"])
    dqn, dkn, dv = _attn_bwd(sv["qkv"], sv["oa"], doa)
    dqk, dgqk = _qknorm_bwd(jnp.concatenate([dqn, dkn], axis=1), sv["p"], wl["gqk"])
    g["q_norm_g"] = dgqk[0, :HD]
    g["k_norm_g"] = dgqk[0, HD:2 * HD]
    (du, dy16, yg16, dgl16, lam_re, lam_im, d_abar_re, d_abar_im, g["ssm_d"], g["glu_b"]) = _s5_bwd(
        dos, sv["p"], sv["s_re"], sv["s_im"], wl["abar_re"], wl["abar_im"], wl["bbr"], wl["bbi"],
        wl["ccr"], wl["cci"], wl["dsk"], wl["gw"], wl["gb"])

    def tn_dense(name, a, a_col, a_cols, b, b_cols):
        return _mm(name, a, b, dims=TN, grid=(a_cols // 512, b_cols // 512, nk),
                   a_block=(bk, 512), a_map=lambda i, j, k: (k, a_col + i),
                   b_block=(bk, 512), b_map=lambda i, j, k: (k, j),
                   out_shape=(a_cols, b_cols), o_block=(512, 512), o_map=lambda i, j, k: (i, j))

    d_ccr = tn_dense("s5_dcc", sv["s_re"], 0, NS, dy16, SSMW)
    d_cci = tn_dense("s5_dcc", sv["s_im"], 0, NS, dy16, SSMW)
    d_bbr = tn_dense("s5_dbb", sv["p"], 3, SSMW, lam_re, NS)
    d_bbi = tn_dense("s5_dbb", sv["p"], 3, SSMW, lam_im, NS)
    d_gw = tn_dense("s5_dgw", yg16, 0, SSMW, dgl16, SSMW)
    g["ssm_c_re"] = _diag_blocks(d_ccr, NP, NH16).transpose(0, 2, 1)
    g["ssm_c_im"] = -_diag_blocks(d_cci, NP, NH16).transpose(0, 2, 1)
    g["glu_w"] = _diag_blocks(d_gw, NH16, NH16)
    d_bbre = _diag_blocks(d_bbr, NH16, NP).transpose(0, 2, 1).reshape(NS, NH16)
    d_bbim = _diag_blocks(d_bbi, NH16, NP).transpose(0, 2, 1).reshape(NS, NH16)
    d_ldt, d_ar, d_ai, d_br, d_bi = _s5_disc_bwd(
        wl["ldt_col"], wl["ar_col"], wl["ai_col"], wl["br_mat"], wl["bi_mat"],
        d_abar_re.reshape(NS, 1), d_abar_im.reshape(NS, 1), d_bbre, d_bbim)
    g["ssm_log_dt"] = d_ldt[:, 0]
    g["ssm_a_re"] = d_ar.reshape(NG, NP)
    g["ssm_a_im"] = d_ai.reshape(NG, NP)
    g["ssm_b_re"] = d_br.reshape(NG, NP, NH16)
    g["ssm_b_im"] = d_bi.reshape(NG, NP, NH16)
    dp = jnp.concatenate([dqk, dv.astype(bf16), du.astype(bf16)], axis=1)
    dh = _mm("mm_in_dx", dp, wl["w_in"], dims=NT, grid=(nb, 1, 4),
             a_block=(bm, 512), a_map=lambda i, j, k: (i, k),
             b_block=(None, D, 512), b_map=lambda i, j, k: (k, 0, 0),
             out_shape=(L, D), o_block=(bm, D), o_map=lambda i, j, k: (i, 0))
    g["w_in"] = _mm("mm_in_dw", sv["h"], dp, dims=TN, grid=(2, 4, nk),
                    a_block=(bk, 512), a_map=lambda i, j, k: (k, i),
                    b_block=(bk, 512), b_map=lambda i, j, k: (k, j),
                    out_shape=(NCHIP, D, 512), o_block=(None, 512, 512), o_map=lambda i, j, k: (j, i, 0))
    dx_in, dsh1, dsc1, g["norm1_g"] = _rmsmod_bwd(dh, sv["xin"], wl["norm1_g"], sv["sc1"], dx1)
    g["dmod"] = jnp.concatenate([dsh1, dsc1, dg1, dsh2, dsc2, dg2], axis=1)[0]
    return dx_in, g


def _prep_layer(l, big, small):
    wl = {}
    wl["w_in"] = big["w_in"][:, l]
    wl["w_up"] = big["w_up"][:, l]
    wl["w_out"] = big["w_out"][:, l].reshape(D, D)
    wl["w_down"] = big["w_down"][:, l].reshape(DFF, D)
    wl["conv_w"] = big["conv_w"][:, l].transpose(1, 0, 2).reshape(3, DUP)
    wl["conv_b"] = small["ffn_conv_b"][l][None, :]
    wl["norm1_g"] = small["norm1_g"][l][None, :]
    wl["norm2_g"] = small["norm2_g"][l][None, :]
    wl["attn_out_g"] = small["attn_out_g"][l][None, :]
    wl["ssm_out_g"] = small["ssm_out_g"][l][None, :]
    wl["gqk"] = jnp.concatenate([jnp.tile(small["q_norm_g"][l], 8), jnp.tile(small["k_norm_g"][l], 8)])[None, :]
    wl["ldt_col"] = jnp.repeat(small["ssm_log_dt"][l], NP)[:, None]
    wl["ar_col"] = small["ssm_a_re"][l].reshape(NS, 1)
    wl["ai_col"] = small["ssm_a_im"][l].reshape(NS, 1)
    wl["br_mat"] = small["ssm_b_re"][l].reshape(NS, NH16)
    wl["bi_mat"] = small["ssm_b_im"][l].reshape(NS, NH16)
    abar_re, abar_im, bb_re, bb_im = _s5_disc_fwd(wl["ldt_col"], wl["ar_col"], wl["ai_col"],
                                                  wl["br_mat"], wl["bi_mat"])
    wl["abar_re"] = abar_re.reshape(1, NS)
    wl["abar_im"] = abar_im.reshape(1, NS)
    wl["bbr"] = _block_diag(bb_re.reshape(NG, NP, NH16).transpose(0, 2, 1)).astype(bf16)
    wl["bbi"] = _block_diag(bb_im.reshape(NG, NP, NH16).transpose(0, 2, 1)).astype(bf16)
    wl["ccr"] = _block_diag(small["ssm_c_re"][l].transpose(0, 2, 1)).astype(bf16)
    wl["cci"] = _block_diag(small["ssm_c_im"][l].transpose(0, 2, 1)).astype(bf16)
    wl["gw"] = _block_diag(small["glu_w"][l]).astype(bf16)
    wl["dsk"] = small["ssm_d"][l].reshape(1, SSMW)
    wl["gb"] = small["glu_b"][l].reshape(1, SSMW)
    return wl


def _local_step(x, target, mods, big, small):
    wls = [_prep_layer(l, big, small) for l in range(DEPTH)]
    saved = []
    cur, res, res_gate = x, None, None
    for l in range(DEPTH):
        cur, res, res_gate, sv = _layer_fwd(cur, res, res_gate, wls[l], mods[l])
        saved.append(sv)
    loss, dx = _loss_fwd_bwd(cur, res, res_gate, target)
    grads = [None] * DEPTH
    for l in reversed(range(DEPTH)):
        dx, grads[l] = _layer_bwd(dx, wls[l], saved[l])
    return loss, dx, grads


BIG = ("w_in", "w_out", "ffn_w_up", "ffn_w_down")
SMALL = ("ada_b", "norm1_g", "q_norm_g", "k_norm_g", "ssm_a_re", "ssm_a_im", "ssm_log_dt",
         "ssm_b_re", "ssm_b_im", "ssm_c_re", "ssm_c_im", "ssm_d", "glu_w", "glu_b",
         "attn_out_g", "ssm_out_g", "norm2_g", "ffn_conv_b")
NAMES = ("ada_w", "ada_b", "norm1_g", "w_in", "q_norm_g", "k_norm_g", "ssm_a_re", "ssm_a_im",
         "ssm_log_dt", "ssm_b_re", "ssm_b_im", "ssm_c_re", "ssm_c_im", "ssm_d", "glu_w", "glu_b",
         "attn_out_g", "ssm_out_g", "w_out", "norm2_g", "ffn_w_up", "ffn_conv_w", "ffn_conv_b",
         "ffn_w_down")
PACK_COLS = 1024


def _pack(arrs):
    flat = jnp.concatenate([a.reshape(-1) for a in arrs])
    rows = -(-flat.shape[0] // PACK_COLS)
    rows = -(-rows // 8) * 8
    flat = jnp.pad(flat, (0, rows * PACK_COLS - flat.shape[0]))
    return flat.reshape(rows, PACK_COLS)


def _unpack(packed, shapes):
    flat = packed.reshape(-1)
    out, off = [], 0
    for s in shapes:
        n = math.prod(s)
        out.append(flat[off:off + n].reshape(s))
        off += n
    return out


def kernel(x, c, ada_w, ada_b, norm1_g, w_in, q_norm_g, k_norm_g, ssm_a_re, ssm_a_im, ssm_log_dt, ssm_b_re, ssm_b_im, ssm_c_re, ssm_c_im, ssm_d, glu_w, glu_b, attn_out_g, ssm_out_g, w_out, norm2_g, ffn_w_up, ffn_conv_w, ffn_conv_b, ffn_w_down, loss_target, m_ada_w, m_ada_b, m_norm1_g, m_w_in, m_q_norm_g, m_k_norm_g, m_ssm_a_re, m_ssm_a_im, m_ssm_log_dt, m_ssm_b_re, m_ssm_b_im, m_ssm_c_re, m_ssm_c_im, m_ssm_d, m_glu_w, m_glu_b, m_attn_out_g, m_ssm_out_g, m_w_out, m_norm2_g, m_ffn_w_up, m_ffn_conv_w, m_ffn_conv_b, m_ffn_w_down, v_ada_w, v_ada_b, v_norm1_g, v_w_in, v_q_norm_g, v_k_norm_g, v_ssm_a_re, v_ssm_a_im, v_ssm_log_dt, v_ssm_b_re, v_ssm_b_im, v_ssm_c_re, v_ssm_c_im, v_ssm_d, v_glu_w, v_glu_b, v_attn_out_g, v_ssm_out_g, v_w_out, v_norm2_g, v_ffn_w_up, v_ffn_conv_w, v_ffn_conv_b, v_ffn_w_down):
    env = dict(locals())
    w = {n: env[n] for n in NAMES}
    m = {n: env["m_" + n] for n in NAMES}
    v = {n: env["v_" + n] for n in NAMES}
    mx, my, mc = _my_pos()
    chip = 2 * mx + my
    dev = 4 * mx + 2 * my + mc
    xl = x[0]
    tl = loss_target[0]

    c_all = _allgather8("gather_c", jnp.pad(c, ((0, 7), (0, 0))))[:, 0, :]
    c16 = jnp.pad(c_all, ((0, 8), (0, 0)))
    nloc = NMOD * D // NCHIP
    ada_b_loc = lax.dynamic_slice(ada_b, (0, chip * nloc), (DEPTH, nloc))[:, None, :]
    mod_loc = _mod_fwd(c16, ada_w, ada_b_loc)
    mod_g = _allgather8("gather_mod", mod_loc.reshape(DEPTH * 16, nloc))
    mod_all = mod_g[0::2].reshape(NCHIP, DEPTH, 16, nloc).transpose(1, 2, 0, 3).reshape(DEPTH, 16, NMOD * D)
    mods = lax.dynamic_index_in_dim(mod_all, dev, axis=1, keepdims=False)

    gathered = _chip_gather([_cast_bf16(w["w_in"]), _cast_bf16(w["w_out"]), _cast_bf16(w["ffn_w_up"]),
                             _cast_bf16(w["ffn_w_down"]), w["ffn_conv_w"]])
    big = dict(zip(("w_in", "w_out", "w_up", "w_down", "conv_w"), gathered))
    small = {n: w[n] for n in SMALL}

    loss_loc, grad_x, grads = _local_step(xl, tl, mods, big, small)
    loss = lax.psum(loss_loc, ("x", "y", "c"))

    def stacked(name):
        return jnp.stack([grads[l][name] for l in range(DEPTH)], axis=1)

    loc = [stacked("w_in"),
           jnp.stack([grads[l]["w_out"].reshape(NCHIP, D // NCHIP, D) for l in range(DEPTH)], axis=1),
           stacked("w_up"),
           jnp.stack([grads[l]["w_down"].reshape(NCHIP, DFF // NCHIP, D) for l in range(DEPTH)], axis=1)]
    sib = _sibling_exchange(loc)
    chip_sum = [_add2(a, b) for a, b in zip(loc, sib)]
    parts = _chip_scatter(chip_sum)
    outs = {}
    for name, part in zip(BIG, parts):
        outs[name] = _adamw(part, w[name], m[name], v[name])

    small_names = SMALL[1:] + ("ffn_conv_w_full",)
    small_grads = []
    for n in SMALL[1:]:
        key = {"ffn_conv_b": "conv_b"}.get(n, n)
        small_grads.append(jnp.stack([grads[l][key].reshape(w[n].shape[1:]) for l in range(DEPTH)]))
    small_grads.append(jnp.stack([grads[l]["conv_w"] for l in range(DEPTH)]))
    dmod = jnp.stack([grads[l]["dmod"] for l in range(DEPTH)])
    packed = _pack([dmod] + small_grads)
    allp = _allgather8("gather_small", packed)
    shapes = [dmod.shape] + [a.shape for a in small_grads]
    dmod_all = allp[:, :DEPTH * NMOD, :].reshape(NDEV, DEPTH, NMOD * D)
    dmod_loc = lax.dynamic_slice(dmod_all, (0, 0, chip * nloc), (NDEV, DEPTH, nloc)).transpose(1, 0, 2)
    g_ada = _ada_grad(c16, jnp.pad(dmod_loc, ((0, 0), (0, 8), (0, 0))))
    outs["ada_w"] = _adamw(g_ada[None], w["ada_w"], m["ada_w"], v["ada_w"])
    conv_w_shape = (DEPTH, 3, DUP)
    w_small = _pack([w[n] for n in SMALL] + [jnp.zeros(conv_w_shape, f32)])
    m_small = _pack([m[n] for n in SMALL] + [jnp.zeros(conv_w_shape, f32)])
    v_small = _pack([v[n] for n in SMALL] + [jnp.ones(conv_w_shape, f32)])
    res_small = _adamw(allp, w_small, m_small, v_small)
    unpacked = [_unpack(r, shapes) for r in res_small]
    for i, n in enumerate(SMALL):
        outs[n] = tuple(unpacked[k][i] for k in range(4))
    g_conv_full = unpacked[0][len(SMALL)]
    ncw = DUP // NCHIP
    g_conv = lax.dynamic_slice(g_conv_full, (0, 0, chip * ncw), (DEPTH, 3, ncw))
    outs["ffn_conv_w"] = _adamw(g_conv[None], w["ffn_conv_w"], m["ffn_conv_w"], v["ffn_conv_w"])

    result = [loss, grad_x[None]]
    for k in range(4):
        result += [outs[n][k] for n in NAMES]
    return tuple(result)
```

```python
import functools
import math

import jax
import jax.numpy as jnp
from jax import lax
from jax.experimental import pallas as pl
from jax.experimental.pallas import tpu as pltpu

f32 = jnp.float32
bf16 = jnp.bfloat16
_MXU = jnp.bfloat16

D = 1024
ATT = 512
SSMW = 512
HD = 64
NG = 32
NP = 64
NH16 = 16
NS = NG * NP
INW = 2048
DFF = 2816
DUP = 2 * DFF
NMOD = 6
DEPTH = 4
EPS = 1e-6
NCHIP = 4
NDEV = 8

ADAM_LR = 0.001
ADAM_B1 = 0.9
ADAM_B2 = 0.999
ADAM_EPS = 1e-08
ADAM_WD = 0.01
ADAM_STEP = 10

VMEM_LIMIT = 56 * 1024 * 1024
MESH = pl.DeviceIdType.MESH

NN = (((1,), (0,)), ((), ()))
NT = (((1,), (1,)), ((), ()))
TN = (((0,), (0,)), ((), ()))


def _cp(sem=None):
    if sem is None:
        return pltpu.CompilerParams(vmem_limit_bytes=VMEM_LIMIT)
    return pltpu.CompilerParams(dimension_semantics=sem, vmem_limit_bytes=VMEM_LIMIT)


def _dot(a, b, dims=NN):
    return lax.dot_general(a.astype(_MXU), b.astype(_MXU), dims, preferred_element_type=f32)


def _dot_exact(x, m):
    hi = x.astype(bf16)
    r1 = x - hi.astype(f32)
    mid = r1.astype(bf16)
    lo = (r1 - mid.astype(f32)).astype(bf16)
    d = lambda p: lax.dot_general(p, m, NN, preferred_element_type=f32)
    return d(hi) + d(mid) + d(lo)


def _gelu(x):
    c = math.sqrt(2.0 / math.pi)
    return 0.5 * x * (1.0 + jnp.tanh(c * (x + 0.044715 * (x * x * x))))


def _gelu_grad(x):
    c = math.sqrt(2.0 / math.pi)
    t = jnp.tanh(c * (x + 0.044715 * (x * x * x)))
    return 0.5 * (1.0 + t) + 0.5 * x * (1.0 - t * t) * c * (1.0 + 3.0 * 0.044715 * (x * x))


def _sigmoid(x):
    return 1.0 / (1.0 + jnp.exp(-x))


def _mm(name, a, b, *, dims, grid, a_block, a_map, b_block, b_map, out_shape, o_block, o_map,
        out_dtype=f32):
    nk = grid[2]
    acc_shape = tuple(s for s in o_block if s is not None)

    def body(a_ref, b_ref, o_ref, acc_ref):
        k = pl.program_id(2)
        part = _dot(a_ref[...], b_ref[...], dims)
        if nk == 1:
            o_ref[...] = part.astype(o_ref.dtype)
        else:
            @pl.when(k == 0)
            def _():
                acc_ref[...] = part

            @pl.when(k > 0)
            def _():
                acc_ref[...] += part

            @pl.when(k == nk - 1)
            def _():
                o_ref[...] = acc_ref[...].astype(o_ref.dtype)

    return pl.pallas_call(
        body, name=name, grid=grid,
        in_specs=[pl.BlockSpec(a_block, a_map), pl.BlockSpec(b_block, b_map)],
        out_specs=pl.BlockSpec(o_block, o_map),
        out_shape=jax.ShapeDtypeStruct(out_shape, out_dtype),
        scratch_shapes=[pltpu.VMEM(acc_shape if nk > 1 else (8, 128), f32)],
        compiler_params=_cp(("parallel", "parallel", "arbitrary")),
    )(a, b)


def _row_block(L, want):
    return want if L % want == 0 else L


def _rmsmod_fwd(x, res, gate, g, sh, sc):
    L = x.shape[0]
    bm = _row_block(L, 256)
    with_res = res is not None

    def body(*refs):
        if with_res:
            x_ref, r_ref, gt_ref, g_ref, sh_ref, sc_ref, xo_ref, h_ref = refs
            xin = x_ref[...] + gt_ref[...] * r_ref[...]
            xo_ref[...] = xin
        else:
            x_ref, g_ref, sh_ref, sc_ref, h_ref = refs
            xin = x_ref[...]
        inv = lax.rsqrt(jnp.mean(xin * xin, axis=-1, keepdims=True) + EPS)
        xn = xin * inv * g_ref[...]
        h_ref[...] = (xn * (1.0 + sc_ref[...]) + sh_ref[...]).astype(h_ref.dtype)

    row = pl.BlockSpec((bm, D), lambda i: (i, 0))
    vec = pl.BlockSpec((1, D), lambda i: (0, 0))
    if with_res:
        return pl.pallas_call(
            body, name="rmsmod_res_fwd", grid=(L // bm,),
            in_specs=[row, row, vec, vec, vec, vec], out_specs=[row, row],
            out_shape=[jax.ShapeDtypeStruct((L, D), f32), jax.ShapeDtypeStruct((L, D), bf16)],
            compiler_params=_cp(("parallel",)),
        )(x, res, gate, g, sh, sc)
    h = pl.pallas_call(
        body, name="rmsmod_fwd", grid=(L // bm,),
        in_specs=[row, vec, vec, vec], out_specs=row,
        out_shape=jax.ShapeDtypeStruct((L, D), bf16),
        compiler_params=_cp(("parallel",)),
    )(x, g, sh, sc)
    return None, h


def _rmsmod_bwd(dh, x, g, sc, dres):
    L = x.shape[0]
    bm = _row_block(L, 256)

    def body(dh_ref, x_ref, g_ref, sc_ref, dr_ref, dx_ref, dsh_ref, dsc_ref, dg_ref):
        i = pl.program_id(0)
        xv = x_ref[...]
        dhv = dh_ref[...]
        inv = lax.rsqrt(jnp.mean(xv * xv, axis=-1, keepdims=True) + EPS)
        xh = xv * inv
        gv = g_ref[...]
        xn = xh * gv
        dxn = dhv * (1.0 + sc_ref[...])
        dxh = dxn * gv
        dx_ref[...] = inv * (dxh - xh * jnp.mean(dxh * xh, axis=-1, keepdims=True)) + dr_ref[...]
        p_sh = jnp.sum(dhv, axis=0, keepdims=True)
        p_sc = jnp.sum(dhv * xn, axis=0, keepdims=True)
        p_g = jnp.sum(dxn * xh, axis=0, keepdims=True)

        @pl.when(i == 0)
        def _():
            dsh_ref[...] = p_sh
            dsc_ref[...] = p_sc
            dg_ref[...] = p_g

        @pl.when(i > 0)
        def _():
            dsh_ref[...] += p_sh
            dsc_ref[...] += p_sc
            dg_ref[...] += p_g

    row = pl.BlockSpec((bm, D), lambda i: (i, 0))
    vec = pl.BlockSpec((1, D), lambda i: (0, 0))
    return pl.pallas_call(
        body, name="rmsmod_bwd", grid=(L // bm,),
        in_specs=[row, row, vec, vec, row], out_specs=[row, vec, vec, vec],
        out_shape=[jax.ShapeDtypeStruct((L, D), f32)] + [jax.ShapeDtypeStruct((1, D), f32)] * 3,
        compiler_params=_cp(("arbitrary",)),
    )(dh, x, g, sc, dres)


def _gate_bwd(dx, y, gate):
    L = dx.shape[0]
    bm = _row_block(L, 256)

    def body(dx_ref, y_ref, gt_ref, dy_ref, dgt_ref):
        i = pl.program_id(0)
        dxv = dx_ref[...]
        dy_ref[...] = (gt_ref[...] * dxv).astype(dy_ref.dtype)
        part = jnp.sum(dxv * y_ref[...], axis=0, keepdims=True)

        @pl.when(i == 0)
        def _():
            dgt_ref[...] = part

        @pl.when(i > 0)
        def _():
            dgt_ref[...] += part

    row = pl.BlockSpec((bm, D), lambda i: (i, 0))
    vec = pl.BlockSpec((1, D), lambda i: (0, 0))
    return pl.pallas_call(
        body, name="gate_bwd", grid=(L // bm,),
        in_specs=[row, row, vec], out_specs=[row, vec],
        out_shape=[jax.ShapeDtypeStruct((L, D), bf16), jax.ShapeDtypeStruct((1, D), f32)],
        compiler_params=_cp(("arbitrary",)),
    )(dx, y, gate)


def _loss_fwd_bwd(x1, down, gate, target):
    L = x1.shape[0]
    bm = _row_block(L, 256)
    nsteps = L // bm

    def body(x_ref, d_ref, gt_ref, t_ref, dy_ref, loss_ref, acc_ref):
        i = pl.program_id(0)
        diff = x_ref[...] + gt_ref[...] * d_ref[...] - t_ref[...]
        dy_ref[...] = diff * (1.0 / D)
        part = jnp.sum(diff * diff, axis=0, keepdims=True)

        @pl.when(i == 0)
        def _():
            acc_ref[...] = part

        @pl.when(i > 0)
        def _():
            acc_ref[...] += part

        @pl.when(i == nsteps - 1)
        def _():
            tot = jnp.sum(acc_ref[...], axis=1, keepdims=True) * (0.5 / D)
            loss_ref[...] = jnp.broadcast_to(tot, (8, 128))

    row = pl.BlockSpec((bm, D), lambda i: (i, 0))
    vec = pl.BlockSpec((1, D), lambda i: (0, 0))
    dy, loss = pl.pallas_call(
        body, name="loss_fwd_bwd", grid=(nsteps,),
        in_specs=[row, row, vec, row],
        out_specs=[row, pl.BlockSpec((8, 128), lambda i: (0, 0))],
        out_shape=[jax.ShapeDtypeStruct((L, D), f32), jax.ShapeDtypeStruct((8, 128), f32)],
        scratch_shapes=[pltpu.VMEM((1, D), f32)],
        compiler_params=_cp(("arbitrary",)),
    )(x1, down, gate, target)
    return loss[0, 0], dy


def _head_mean_matrix():
    r = lax.broadcasted_iota(jnp.int32, (128, 128), 0) // HD
    c = lax.broadcasted_iota(jnp.int32, (128, 128), 1) // HD
    return jnp.where(r == c, 1.0 / HD, 0.0).astype(bf16)


def _qknorm_fwd(p, gqk):
    L = p.shape[0]
    bm = _row_block(L, 256)

    def body(p_ref, g_ref, o_ref):
        e = _head_mean_matrix()
        for c in range(8):
            sl = slice(128 * c, 128 * (c + 1))
            xv = p_ref[:, sl]
            inv = lax.rsqrt(_dot_exact(xv * xv, e) + EPS)
            yv = xv * inv * g_ref[:, sl]
            if c < 4:
                yv = yv * (1.0 / math.sqrt(HD))
            o_ref[:, sl] = yv.astype(o_ref.dtype)
        o_ref[:, 1024:1536] = p_ref[:, 1024:1536].astype(o_ref.dtype)

    return pl.pallas_call(
        body, name="qknorm_fwd", grid=(L // bm,),
        in_specs=[pl.BlockSpec((bm, 1536), lambda i: (i, 0)), pl.BlockSpec((1, 1024), lambda i: (0, 0))],
        out_specs=pl.BlockSpec((bm, 1536), lambda i: (i, 0)),
        out_shape=jax.ShapeDtypeStruct((L, 1536), bf16),
        compiler_params=_cp(("parallel",)),
    )(p, gqk)


def _qknorm_bwd(dqk, p, gqk):
    L = p.shape[0]
    bm = _row_block(L, 256)
    nsteps = L // bm

    def body(d_ref, p_ref, g_ref, o_ref, dg_ref, acc_ref):
        i = pl.program_id(0)
        e = _head_mean_matrix()
        for c in range(8):
            sl = slice(128 * c, 128 * (c + 1))
            xv = p_ref[:, sl]
            dv = d_ref[:, sl]
            gv = g_ref[:, sl]
            inv = lax.rsqrt(_dot_exact(xv * xv, e) + EPS)
            xh = xv * inv
            dxh = dv * gv
            o_ref[:, sl] = (inv * (dxh - xh * _dot_exact(dxh * xh, e))).astype(o_ref.dtype)
            part = jnp.sum(dv * xh, axis=0, keepdims=True)

            @pl.when(i == 0)
            def _():
                acc_ref[:, sl] = part

            @pl.when(i > 0)
            def _():
                acc_ref[:, sl] += part

        @pl.when(i == nsteps - 1)
        def _():
            r = lax.broadcasted_iota(jnp.int32, (1024, 128), 0)
            col = lax.broadcasted_iota(jnp.int32, (1024, 128), 1)
            fold = jnp.where(col == (r // 512) * HD + r % HD, 1.0, 0.0).astype(bf16)
            dg_ref[...] = _dot_exact(jnp.broadcast_to(acc_ref[...], (8, 1024)), fold)

    return pl.pallas_call(
        body, name="qknorm_bwd", grid=(nsteps,),
        in_specs=[pl.BlockSpec((bm, 1024), lambda i: (i, 0)), pl.BlockSpec((bm, 1024), lambda i: (i, 0)),
                  pl.BlockSpec((1, 1024), lambda i: (0, 0))],
        out_specs=[pl.BlockSpec((bm, 1024), lambda i: (i, 0)), pl.BlockSpec((8, 128), lambda i: (0, 0))],
        out_shape=[jax.ShapeDtypeStruct((L, 1024), bf16), jax.ShapeDtypeStruct((8, 128), f32)],
        scratch_shapes=[pltpu.VMEM((1, 1024), f32)],
        compiler_params=_cp(("arbitrary",)),
    )(dqk, p, gqk)


def _outnorm_fwd(oa, os_, ga, gs):
    L = oa.shape[0]
    bm = _row_block(L, 256)

    def body(a_ref, s_ref, ga_ref, gs_ref, o_ref):
        for x_ref, g_ref, off in ((a_ref, ga_ref, 0), (s_ref, gs_ref, 512)):
            xv = x_ref[...]
            inv = lax.rsqrt(jnp.mean(xv * xv, axis=-1, keepdims=True) + EPS)
            o_ref[:, off:off + 512] = (xv * inv * g_ref[...]).astype(o_ref.dtype)

    half = pl.BlockSpec((bm, 512), lambda i: (i, 0))
    vec = pl.BlockSpec((1, 512), lambda i: (0, 0))
    return pl.pallas_call(
        body, name="outnorm_fwd", grid=(L // bm,),
        in_specs=[half, half, vec, vec], out_specs=pl.BlockSpec((bm, D), lambda i: (i, 0)),
        out_shape=jax.ShapeDtypeStruct((L, D), bf16),
        compiler_params=_cp(("parallel",)),
    )(oa, os_, ga, gs)


def _outnorm_bwd(do, oa, os_, ga, gs):
    L = oa.shape[0]
    bm = _row_block(L, 256)

    def body(do_ref, a_ref, s_ref, ga_ref, gs_ref, da_ref, ds_ref, dga_ref, dgs_ref):
        i = pl.program_id(0)
        for x_ref, g_ref, dx_ref, dg_ref, off in ((a_ref, ga_ref, da_ref, dga_ref, 0),
                                                  (s_ref, gs_ref, ds_ref, dgs_ref, 512)):
            xv = x_ref[...]
            dv = do_ref[:, off:off + 512]
            inv = lax.rsqrt(jnp.mean(xv * xv, axis=-1, keepdims=True) + EPS)
            xh = xv * inv
            dxh = dv * g_ref[...]
            dx_ref[...] = inv * (dxh - xh * jnp.mean(dxh * xh, axis=-1, keepdims=True))
            part = jnp.sum(dv * xh, axis=0, keepdims=True)

            @pl.when(i == 0)
            def _():
                dg_ref[...] = part

            @pl.when(i > 0)
            def _():
                dg_ref[...] += part

    half = pl.BlockSpec((bm, 512), lambda i: (i, 0))
    vec = pl.BlockSpec((1, 512), lambda i: (0, 0))
    return pl.pallas_call(
        body, name="outnorm_bwd", grid=(L // bm,),
        in_specs=[pl.BlockSpec((bm, D), lambda i: (i, 0)), half, half, vec, vec],
        out_specs=[half, half, vec, vec],
        out_shape=[jax.ShapeDtypeStruct((L, 512), f32)] * 2 + [jax.ShapeDtypeStruct((1, 512), f32)] * 2,
        compiler_params=_cp(("arbitrary",)),
    )(do, oa, os_, ga, gs)


NCB = DFF // 128


def _shift_down(x, s, row):
    return jnp.where(row >= s, pltpu.roll(x, s, 0), 0.0)


def _shift_up(x, s, row, L):
    return jnp.where(row < L - s, pltpu.roll(x, L - s, 0), 0.0)


def _conv(x, w_ref, b_ref, row):
    out = b_ref[...] + _shift_down(x, 2, row) * w_ref[0:1, :]
    out = out + _shift_down(x, 1, row) * w_ref[1:2, :]
    return out + x * w_ref[2:3, :]


def _convglu_fwd(up_pre, cw, cb):
    L = up_pre.shape[0]

    def body(v_ref, g_ref, wv_ref, wg_ref, bv_ref, bg_ref, a_ref):
        row = lax.broadcasted_iota(jnp.int32, (L, 128), 0)
        val = _conv(v_ref[...], wv_ref, bv_ref, row)
        gate = _conv(g_ref[...], wg_ref, bg_ref, row)
        a_ref[...] = (_gelu(gate) * val).astype(a_ref.dtype)

    col = lambda off: pl.BlockSpec((L, 128), lambda j: (0, j + off))
    wsp = lambda off: pl.BlockSpec((3, 128), lambda j: (0, j + off))
    bsp = lambda off: pl.BlockSpec((1, 128), lambda j: (0, j + off))
    return pl.pallas_call(
        body, name="convglu_fwd", grid=(NCB,),
        in_specs=[col(0), col(NCB), wsp(0), wsp(NCB), bsp(0), bsp(NCB)],
        out_specs=pl.BlockSpec((L, 128), lambda j: (0, j)),
        out_shape=jax.ShapeDtypeStruct((L, DFF), bf16),
        compiler_params=_cp(("parallel",)),
    )(up_pre, up_pre, cw, cw, cb, cb)


def _convglu_bwd(da, up_pre, cw, cb):
    L = up_pre.shape[0]

    def body(da_ref, own_ref, par_ref, wo_ref, wp_ref, bo_ref, bp_ref, dx_ref, dw_ref, db_ref):
        j = pl.program_id(0)
        row = lax.broadcasted_iota(jnp.int32, (L, 128), 0)
        xo = own_ref[...]
        u_own = _conv(xo, wo_ref, bo_ref, row)
        u_par = _conv(par_ref[...], wp_ref, bp_ref, row)
        dav = da_ref[...]

        def finish(d):
            dx = d * wo_ref[2:3, :] + _shift_up(d, 1, row, L) * wo_ref[1:2, :]
            dx_ref[...] = (dx + _shift_up(d, 2, row, L) * wo_ref[0:1, :]).astype(dx_ref.dtype)
            dw_ref[0:1, :] = jnp.sum(d * _shift_down(xo, 2, row), axis=0, keepdims=True)
            dw_ref[1:2, :] = jnp.sum(d * _shift_down(xo, 1, row), axis=0, keepdims=True)
            dw_ref[2:3, :] = jnp.sum(d * xo, axis=0, keepdims=True)
            db_ref[...] = jnp.sum(d, axis=0, keepdims=True)

        @pl.when(j < NCB)
        def _():
            finish(dav * _gelu(u_par))

        @pl.when(j >= NCB)
        def _():
            finish(dav * u_par * _gelu_grad(u_own))

    own = lambda j: (0, j)
    par = lambda j: (0, (j + NCB) % (2 * NCB))
    return pl.pallas_call(
        body, name="convglu_bwd", grid=(2 * NCB,),
        in_specs=[pl.BlockSpec((L, 128), lambda j: (0, j % NCB)),
                  pl.BlockSpec((L, 128), own), pl.BlockSpec((L, 128), par),
                  pl.BlockSpec((3, 128), own), pl.BlockSpec((3, 128), par),
                  pl.BlockSpec((1, 128), own), pl.BlockSpec((1, 128), par)],
        out_specs=[pl.BlockSpec((L, 128), own), pl.BlockSpec((3, 128), own), pl.BlockSpec((1, 128), own)],
        out_shape=[jax.ShapeDtypeStruct((L, DUP), bf16), jax.ShapeDtypeStruct((3, DUP), f32),
                   jax.ShapeDtypeStruct((1, DUP), f32)],
        compiler_params=_cp(("parallel",)),
    )(da, up_pre, up_pre, cw, cw, cb, cb)


def _attn_block(L):
    return 256 if L % 256 == 0 else 128


TAIL_DEAD = -110.0


def _dot_exact2(x, m):
    hi = x.astype(bf16)
    lo = (x - hi.astype(f32)).astype(bf16)
    return (lax.dot_general(hi, m, NN, preferred_element_type=f32)
            + lax.dot_general(lo, m, NN, preferred_element_type=f32))


def _sb_weights(z, mask, tri_gt, carry):
    l1p = jnp.log(1.0 + jnp.exp(-jnp.abs(z)))
    ls_pos = jnp.minimum(z, 0.0) - l1p
    lm = ls_pos - z
    if mask is not None:
        lm = jnp.where(mask, lm, 0.0)
    tail = _dot_exact2(lm, tri_gt) + carry
    w = jnp.exp(ls_pos + tail)
    if mask is not None:
        w = jnp.where(mask, w, 0.0)
    return w, ls_pos, lm


def _head_masks(B):
    lane = lax.broadcasted_iota(jnp.int32, (B, 128), 1)
    return lane < HD, lane >= HD


def _attn_fwd(qkv):
    L = qkv.shape[0]
    B = _attn_block(L)
    nq = L // B

    def body(q_ref, k_ref, v_ref, o_ref):
        qi = pl.program_id(1)
        heads = _head_masks(B)
        ti = lax.broadcasted_iota(jnp.int32, (B, B), 0)
        si = lax.broadcasted_iota(jnp.int32, (B, B), 1)
        tri_gt = jnp.where(ti > si, 1.0, 0.0).astype(bf16)
        diag = si < ti
        qv = q_ref[...]
        zero16 = jnp.zeros((), bf16)

        def tile(jb, mask, acc, carries):
            ks = pl.multiple_of(jb * B, B)
            kb = k_ref[pl.ds(ks, B), :]
            vb = v_ref[pl.ds(ks, B), :]
            out = []
            for in_head, carry in zip(heads, carries):
                kh = jnp.where(in_head, kb, zero16)
                vh = jnp.where(in_head, vb, zero16)
                z = lax.dot_general(qv, kh, NT, preferred_element_type=f32)
                w, _, lm = _sb_weights(z, mask, tri_gt, carry)
                acc = acc + lax.dot_general(w.astype(bf16), vh, NN, preferred_element_type=f32)
                out.append(carry + jnp.sum(lm, axis=1, keepdims=True))
            return acc, out

        zc = jnp.zeros((B, 1), f32)
        acc, (c0, c1) = tile(qi, diag, jnp.zeros((B, 128), f32), (zc, zc))

        def cond(st):
            return jnp.logical_and(st[0] <= qi, st[4] > TAIL_DEAD)

        def step(st):
            n, a, p0, p1, _ = st
            a, (p0, p1) = tile(qi - n, None, a, (p0, p1))
            return n + 1, a, p0, p1, jnp.maximum(jnp.max(p0), jnp.max(p1))

        st = lax.while_loop(cond, step, (jnp.int32(1), acc, c0, c1, jnp.maximum(jnp.max(c0), jnp.max(c1))))
        o_ref[...] = st[1]

    return pl.pallas_call(
        body, name="attn_fwd", grid=(4, nq),
        in_specs=[pl.BlockSpec((B, 128), lambda hp, i: (i, hp)),
                  pl.BlockSpec((L, 128), lambda hp, i: (0, 4 + hp)),
                  pl.BlockSpec((L, 128), lambda hp, i: (0, 8 + hp))],
        out_specs=pl.BlockSpec((B, 128), lambda hp, i: (i, hp)),
        out_shape=jax.ShapeDtypeStruct((L, ATT), f32),
        compiler_params=_cp(("parallel", "parallel")),
    )(qkv, qkv, qkv)


def _attn_bwd(qkv, o, do):
    L = qkv.shape[0]
    B = _attn_block(L)
    nq = L // B

    def body(q_ref, k_ref, v_ref, o_ref, do_ref, dq_ref, dk_ref, dv_ref):
        qi = pl.program_id(1)

        @pl.when(qi == 0)
        def _():
            dk_ref[...] = jnp.zeros_like(dk_ref)
            dv_ref[...] = jnp.zeros_like(dv_ref)

        heads = _head_masks(B)
        ti = lax.broadcasted_iota(jnp.int32, (B, B), 0)
        si = lax.broadcasted_iota(jnp.int32, (B, B), 1)
        tri_gt = jnp.where(ti > si, 1.0, 0.0).astype(bf16)
        tri_ge = jnp.where(ti >= si, 1.0, 0.0).astype(bf16)
        diag = si < ti
        zero16 = jnp.zeros((), bf16)
        qv = q_ref[...]
        do16 = do_ref[...].astype(bf16)
        dsum_lanes = do16.astype(f32) * o_ref[...]
        qhs = [jnp.where(m, qv, zero16) for m in heads]
        dohs = [jnp.where(m, do16, zero16) for m in heads]
        dsums = [jnp.sum(jnp.where(m, dsum_lanes, 0.0), axis=1, keepdims=True) for m in heads]

        def tile(jb, mask, dq_a, state):
            ks = pl.multiple_of(jb * B, B)
            kb = k_ref[pl.ds(ks, B), :]
            vb = v_ref[pl.ds(ks, B), :]
            dk_blk = jnp.zeros((B, 128), f32)
            dv_blk = jnp.zeros((B, 128), f32)
            out = []
            for in_head, qh, doh, dsum, (carry, suffix) in zip(heads, qhs, dohs, dsums, state):
                kh = jnp.where(in_head, kb, zero16)
                vh = jnp.where(in_head, vb, zero16)
                z = lax.dot_general(qv, kh, NT, preferred_element_type=f32)
                w, ls_pos, lm = _sb_weights(z, mask, tri_gt, carry)
                w16 = w.astype(bf16)
                dw = lax.dot_general(doh, vh, NT, preferred_element_type=f32)
                da = w16.astype(f32) * dw
                sig = jnp.exp(ls_pos)
                suf = _dot_exact2(da, tri_ge) + suffix
                dz = da * (1.0 - sig) - sig * (dsum - suf)
                if mask is not None:
                    dz = jnp.where(mask, dz, 0.0)
                dz16 = dz.astype(bf16)
                dq_a = dq_a + lax.dot_general(dz16, kh, NN, preferred_element_type=f32)
                dk_blk = dk_blk + lax.dot_general(dz16, qh, TN, preferred_element_type=f32)
                dv_blk = dv_blk + lax.dot_general(w16, doh, TN, preferred_element_type=f32)
                out.append((carry + jnp.sum(lm, axis=1, keepdims=True),
                            suffix + jnp.sum(da, axis=1, keepdims=True)))
            dk_ref[pl.ds(ks, B), :] += dk_blk
            dv_ref[pl.ds(ks, B), :] += dv_blk
            return dq_a, out

        def alive(state):
            return jnp.maximum(jnp.max(state[0][0]), jnp.max(state[1][0]))

        zc = jnp.zeros((B, 1), f32)
        dq_acc, state = tile(qi, diag, jnp.zeros((B, 128), f32), ((zc, zc), (zc, zc)))

        def cond(st):
            return jnp.logical_and(st[0] <= qi, st[6] > TAIL_DEAD)

        def step(st):
            n, a, c0, s0, c1, s1, _ = st
            a, new = tile(qi - n, None, a, ((c0, s0), (c1, s1)))
            return n + 1, a, new[0][0], new[0][1], new[1][0], new[1][1], alive(new)

        st = lax.while_loop(cond, step, (jnp.int32(1), dq_acc, state[0][0], state[0][1],
                                         state[1][0], state[1][1], alive(state)))
        dq_ref[...] = st[1] * (1.0 / math.sqrt(HD))

    blk = pl.BlockSpec((B, 128), lambda hp, i: (i, hp))
    dqkv = pl.pallas_call(
        body, name="attn_bwd", grid=(4, nq),
        in_specs=[blk,
                  pl.BlockSpec((L, 128), lambda hp, i: (0, 4 + hp)),
                  pl.BlockSpec((L, 128), lambda hp, i: (0, 8 + hp)),
                  blk, blk],
        out_specs=[blk, pl.BlockSpec((L, 128), lambda hp, i: (0, hp)),
                   pl.BlockSpec((L, 128), lambda hp, i: (0, hp))],
        out_shape=[jax.ShapeDtypeStruct((L, ATT), f32)] * 3,
        compiler_params=_cp(("parallel", "arbitrary")),
    )(qkv, qkv, qkv, o, do)
    return dqkv


def _s5_disc(ldt, ar, ai, br, bi):
    dt = jnp.exp(ldt)
    mag = jnp.exp(dt * ar)
    abar_re = mag * jnp.cos(dt * ai)
    abar_im = mag * jnp.sin(dt * ai)
    em_re = abar_re - 1.0
    em_im = abar_im
    den = ar * ar + ai * ai
    f_re = (em_re * ar + em_im * ai) / den
    f_im = (em_im * ar - em_re * ai) / den
    bb_re = f_re * br - f_im * bi
    bb_im = f_re * bi + f_im * br
    return abar_re, abar_im, bb_re, bb_im


def _s5_disc_fwd(ldt, ar, ai, br, bi):
    def body(ldt_ref, ar_ref, ai_ref, br_ref, bi_ref, o1, o2, o3, o4):
        outs = _s5_disc(ldt_ref[...], ar_ref[...], ai_ref[...], br_ref[...], bi_ref[...])
        for o_ref, v in zip((o1, o2, o3, o4), outs):
            o_ref[...] = v

    col = jax.ShapeDtypeStruct((NS, 1), f32)
    mat = jax.ShapeDtypeStruct((NS, NH16), f32)
    return pl.pallas_call(body, name="s5_disc_fwd", out_shape=[col, col, mat, mat],
                          compiler_params=_cp())(ldt, ar, ai, br, bi)


def _s5_disc_bwd(ldt, ar, ai, br, bi, d_are, d_aim, d_bbre, d_bbim):
    def body(ldt_ref, ar_ref, ai_ref, br_ref, bi_ref, c1, c2, c3, c4, o_ldt, o_ar, o_ai, o_br, o_bi):
        prim = (ldt_ref[...], ar_ref[...], ai_ref[...], br_ref[...], bi_ref[...])
        _, vjp = jax.vjp(_s5_disc, *prim)
        g_ldt, g_ar, g_ai, g_br, g_bi = vjp((c1[...], c2[...], c3[...], c4[...]))
        o_ar[...] = g_ar
        o_ai[...] = g_ai
        o_br[...] = g_br
        o_bi[...] = g_bi
        r = lax.broadcasted_iota(jnp.int32, (NG, NS), 0)
        c = lax.broadcasted_iota(jnp.int32, (NG, NS), 1)
        fold = jnp.where(c // NP == r, 1.0, 0.0).astype(bf16)
        hi = jnp.broadcast_to(g_ldt, (NS, 128))
        p1 = hi.astype(bf16)
        r1 = hi - p1.astype(f32)
        p2 = r1.astype(bf16)
        p3 = (r1 - p2.astype(f32)).astype(bf16)
        dd = lambda p: lax.dot_general(fold, p, NN, preferred_element_type=f32)
        o_ldt[...] = dd(p1) + dd(p2) + dd(p3)

    col = jax.ShapeDtypeStruct((NS, 1), f32)
    mat = jax.ShapeDtypeStruct((NS, NH16), f32)
    return pl.pallas_call(
        body, name="s5_disc_bwd",
        out_shape=[jax.ShapeDtypeStruct((NG, 128), f32), col, col, mat, mat],
        compiler_params=_cp())(ldt, ar, ai, br, bi, d_are, d_aim, d_bbre, d_bbim)


SCAN_LANES = 256


def _cmul(ar, ai, br, bi):
    return ar * br - ai * bi, ar * bi + ai * br


def _pow_tables(ar, ai):
    a2 = _cmul(ar, ai, ar, ai)
    a4 = _cmul(*a2, *a2)
    rows = [(ar, ai)]
    for _ in range(7):
        rows.append(_cmul(*rows[-1], ar, ai))
    return (ar, ai), a2, a4, rows


def _scan_rows(xr_ref, xi_ref, abar_re, abar_im, cr_ref, ci_ref, reverse, on_tile=None):
    T = xr_ref.shape[0]
    nt = T // 8
    row = lax.broadcasted_iota(jnp.int32, (8, SCAN_LANES), 0)
    for c in range(NS // SCAN_LANES):
        ls = slice(c * SCAN_LANES, (c + 1) * SCAN_LANES)
        ar = abar_re[:, ls]
        ai = abar_im[:, ls]
        if reverse:
            ai = -ai
        a1, a2, a4, prow = _pow_tables(ar, ai)
        if reverse:
            prow = prow[::-1]
        pr = jnp.concatenate([p[0] for p in prow], axis=0)
        pi = jnp.concatenate([p[1] for p in prow], axis=0)
        steps = [(1, a1), (2, a2), (4, a4)]

        def tile(n, carry):
            cr, ci = carry
            it = (nt - 1 - n) if reverse else n
            rs = pl.ds(pl.multiple_of(it * 8, 8), 8)
            xr = xr_ref[rs, ls]
            xi = xi_ref[rs, ls]
            for d, (dr, di) in steps:
                if reverse:
                    keep = row < 8 - d
                    sr = jnp.where(keep, pltpu.roll(xr, 8 - d, 0), 0.0)
                    si = jnp.where(keep, pltpu.roll(xi, 8 - d, 0), 0.0)
                else:
                    keep = row >= d
                    sr = jnp.where(keep, pltpu.roll(xr, d, 0), 0.0)
                    si = jnp.where(keep, pltpu.roll(xi, d, 0), 0.0)
                mr, mi = _cmul(dr, di, sr, si)
                xr = xr + mr
                xi = xi + mi
            mr, mi = _cmul(pr, pi, cr, ci)
            xr = xr + mr
            xi = xi + mi
            xr_ref[rs, ls] = xr
            xi_ref[rs, ls] = xi
            if on_tile is not None:
                nxt_r = jnp.where(row < 7, pltpu.roll(xr, 7, 0), cr)
                nxt_i = jnp.where(row < 7, pltpu.roll(xi, 7, 0), ci)
                on_tile(ls, rs, nxt_r, nxt_i)
            edge = slice(0, 1) if reverse else slice(7, 8)
            return xr[edge, :], xi[edge, :]

        cr, ci = lax.fori_loop(0, nt, tile, (cr_ref[:, ls], ci_ref[:, ls]))
        cr_ref[:, ls] = cr
        ci_ref[:, ls] = ci


def _s5_chunk(L):
    return 256 if L % 256 == 0 else L


def _s5_gate(s_re, s_im, u, ccr_ref, cci_ref, dsk_ref, gw_ref, gb_ref):
    y = _dot(s_re, ccr_ref[...]) - _dot(s_im, cci_ref[...]) + dsk_ref[...] * u
    yg = _gelu(y)
    gate = _sigmoid(_dot(yg, gw_ref[...]) + gb_ref[...])
    return y, yg, gate


def _s5_fwd(p, abar_re, abar_im, bbr, bbi, ccr, cci, dsk, gw, gb):
    L = p.shape[0]
    T = _s5_chunk(L)

    def body(u_ref, are_ref, aim_ref, bbr_ref, bbi_ref, ccr_ref, cci_ref, dsk_ref, gw_ref, gb_ref,
             o_ref, sr_ref, si_ref, cr_ref, ci_ref):
        @pl.when(pl.program_id(0) == 0)
        def _():
            cr_ref[...] = jnp.zeros_like(cr_ref)
            ci_ref[...] = jnp.zeros_like(ci_ref)

        u = u_ref[...]
        sr_ref[...] = _dot(u, bbr_ref[...])
        si_ref[...] = _dot(u, bbi_ref[...])
        _scan_rows(sr_ref, si_ref, are_ref[...], aim_ref[...], cr_ref, ci_ref, reverse=False)
        _, yg, gate = _s5_gate(sr_ref[...], si_ref[...], u, ccr_ref, cci_ref, dsk_ref, gw_ref, gb_ref)
        o_ref[...] = yg * gate

    full = lambda shape: pl.BlockSpec(shape, lambda i: (0, 0))
    return pl.pallas_call(
        body, name="s5_fwd", grid=(L // T,),
        in_specs=[pl.BlockSpec((T, SSMW), lambda i: (i, 3)), full((1, NS)), full((1, NS)),
                  full((SSMW, NS)), full((SSMW, NS)), full((NS, SSMW)), full((NS, SSMW)),
                  full((1, SSMW)), full((SSMW, SSMW)), full((1, SSMW))],
        out_specs=[pl.BlockSpec((T, SSMW), lambda i: (i, 0)), pl.BlockSpec((T, NS), lambda i: (i, 0)),
                   pl.BlockSpec((T, NS), lambda i: (i, 0))],
        out_shape=[jax.ShapeDtypeStruct((L, SSMW), f32), jax.ShapeDtypeStruct((L, NS), f32),
                   jax.ShapeDtypeStruct((L, NS), f32)],
        scratch_shapes=[pltpu.VMEM((1, NS), f32), pltpu.VMEM((1, NS), f32)],
        compiler_params=_cp(("arbitrary",)),
    )(p, abar_re, abar_im, bbr, bbi, ccr, cci, dsk, gw, gb)


def _s5_bwd(dout, p, s_re, s_im, abar_re, abar_im, bbr, bbi, ccr, cci, dsk, gw, gb):
    L = p.shape[0]
    T = _s5_chunk(L)
    nchunk = L // T

    def body(do_ref, u_ref, sr_ref, si_ref, are_ref, aim_ref, bbr_ref, bbi_ref, ccr_ref, cci_ref,
             dsk_ref, gw_ref, gb_ref,
             du_ref, dy_ref, yg_ref, dgl_ref, lr16_ref, li16_ref, dare_ref, daim_ref, dd_ref, dgb_ref,
             lr_ref, li_ref, cr_ref, ci_ref, accr_ref, acci_ref):
        i = pl.program_id(0)

        @pl.when(i == 0)
        def _():
            cr_ref[...] = jnp.zeros_like(cr_ref)
            ci_ref[...] = jnp.zeros_like(ci_ref)
            accr_ref[...] = jnp.zeros_like(accr_ref)
            acci_ref[...] = jnp.zeros_like(acci_ref)
            dd_ref[...] = jnp.zeros_like(dd_ref)
            dgb_ref[...] = jnp.zeros_like(dgb_ref)

        u = u_ref[...]
        dov = do_ref[...]
        y, yg, gate = _s5_gate(sr_ref[...], si_ref[...], u, ccr_ref, cci_ref, dsk_ref, gw_ref, gb_ref)
        dgl = dov * yg * gate * (1.0 - gate)
        dyg = dov * gate + _dot(dgl, gw_ref[...], NT)
        dy = dyg * _gelu_grad(y)
        dy_ref[...] = dy.astype(dy_ref.dtype)
        yg_ref[...] = yg.astype(yg_ref.dtype)
        dgl_ref[...] = dgl.astype(dgl_ref.dtype)
        dd_ref[...] += jnp.sum(dy * u, axis=0, keepdims=True)
        dgb_ref[...] += jnp.sum(dgl, axis=0, keepdims=True)
        lr_ref[...] = _dot(dy, ccr_ref[...], NT)
        li_ref[...] = -_dot(dy, cci_ref[...], NT)

        def on_tile(ls, rs, nxt_r, nxt_i):
            s_r = sr_ref[rs, ls]
            s_i = si_ref[rs, ls]
            accr_ref[:, ls] += nxt_r * s_r + nxt_i * s_i
            acci_ref[:, ls] += nxt_i * s_r - nxt_r * s_i

        _scan_rows(lr_ref, li_ref, are_ref[...], aim_ref[...], cr_ref, ci_ref, reverse=True,
                   on_tile=on_tile)
        lam_r = lr_ref[...]
        lam_i = li_ref[...]
        lr16_ref[...] = lam_r.astype(lr16_ref.dtype)
        li16_ref[...] = lam_i.astype(li16_ref.dtype)
        du_ref[...] = dy * dsk_ref[...] + _dot(lam_r, bbr_ref[...], NT) + _dot(lam_i, bbi_ref[...], NT)

        @pl.when(i == nchunk - 1)
        def _():
            dare_ref[...] = jnp.sum(accr_ref[...], axis=0, keepdims=True)
            daim_ref[...] = jnp.sum(acci_ref[...], axis=0, keepdims=True)

    rev = lambda i: (nchunk - 1 - i, 0)
    full = lambda shape: pl.BlockSpec(shape, lambda i: (0, 0))
    half = pl.BlockSpec((T, SSMW), rev)
    wide = pl.BlockSpec((T, NS), rev)
    return pl.pallas_call(
        body, name="s5_bwd", grid=(nchunk,),
        in_specs=[half, pl.BlockSpec((T, SSMW), lambda i: (nchunk - 1 - i, 3)), wide, wide,
                  full((1, NS)), full((1, NS)), full((SSMW, NS)), full((SSMW, NS)),
                  full((NS, SSMW)), full((NS, SSMW)), full((1, SSMW)), full((SSMW, SSMW)), full((1, SSMW))],
        out_specs=[half, half, half, half, wide, wide, full((1, NS)), full((1, NS)),
                   full((1, SSMW)), full((1, SSMW))],
        out_shape=[jax.ShapeDtypeStruct((L, SSMW), f32)] + [jax.ShapeDtypeStruct((L, SSMW), bf16)] * 3
                  + [jax.ShapeDtypeStruct((L, NS), bf16)] * 2 + [jax.ShapeDtypeStruct((1, NS), f32)] * 2
                  + [jax.ShapeDtypeStruct((1, SSMW), f32)] * 2,
        scratch_shapes=[pltpu.VMEM((T, NS), f32), pltpu.VMEM((T, NS), f32),
                        pltpu.VMEM((1, NS), f32), pltpu.VMEM((1, NS), f32),
                        pltpu.VMEM((8, NS), f32), pltpu.VMEM((8, NS), f32)],
        compiler_params=_cp(("arbitrary",)),
    )(dout, p, s_re, s_im, abar_re, abar_im, bbr, bbi, ccr, cci, dsk, gw, gb)


def _block_diag(x):
    g, r, c = x.shape
    eye = jnp.eye(g, dtype=x.dtype)
    return (x[:, :, None, :] * eye[:, None, :, None]).reshape(g * r, g * c)


def _diag_blocks(dense, r, c):
    g = dense.shape[0] // r
    return jnp.stack([dense[k * r:(k + 1) * r, k * c:(k + 1) * c] for k in range(g)])


def _mod_fwd(c16, ada_w, ada_b_loc):
    nloc = ada_w.shape[2]

    def body(c_ref, w_ref, b_ref, o_ref):
        cv = c_ref[...]
        act = cv * _sigmoid(cv)
        o_ref[...] = _dot(act, w_ref[...]) + b_ref[...]

    return pl.pallas_call(
        body, name="mod_fwd", grid=(DEPTH,),
        in_specs=[pl.BlockSpec((16, D), lambda l: (0, 0)), pl.BlockSpec((None, D, nloc), lambda l: (l, 0, 0)),
                  pl.BlockSpec((None, 1, nloc), lambda l: (l, 0, 0))],
        out_specs=pl.BlockSpec((None, 16, nloc), lambda l: (l, 0, 0)),
        out_shape=jax.ShapeDtypeStruct((DEPTH, 16, nloc), f32),
        compiler_params=_cp(("parallel",)),
    )(c16, ada_w, ada_b_loc)


def _ada_grad(c16, dmod16):
    nloc = dmod16.shape[2]

    def body(c_ref, d_ref, o_ref):
        cv = c_ref[...]
        act = cv * _sigmoid(cv)
        o_ref[...] = _dot(act, d_ref[...], TN)

    return pl.pallas_call(
        body, name="ada_grad", grid=(DEPTH,),
        in_specs=[pl.BlockSpec((16, D), lambda l: (0, 0)), pl.BlockSpec((None, 16, nloc), lambda l: (l, 0, 0))],
        out_specs=pl.BlockSpec((None, D, nloc), lambda l: (l, 0, 0)),
        out_shape=jax.ShapeDtypeStruct((DEPTH, D, nloc), f32),
        compiler_params=_cp(("parallel",)),
    )(c16, dmod16)


def _as2d(a):
    return a.reshape(-1, a.shape[-1])


def _ew_rows(rows):
    for cand in (512, 256, 128, 64, 32, 16, 8):
        if rows % cand == 0:
            return cand
    return rows


def _cast_bf16(w):
    w2 = _as2d(w)
    rows, cols = w2.shape
    bm = _ew_rows(rows)

    def body(x_ref, o_ref):
        o_ref[...] = x_ref[...].astype(bf16)

    spec = pl.BlockSpec((bm, cols), lambda i: (i, 0))
    out = pl.pallas_call(body, name="cast_bf16", grid=(rows // bm,), in_specs=[spec], out_specs=spec,
                         out_shape=jax.ShapeDtypeStruct((rows, cols), bf16),
                         compiler_params=_cp(("parallel",)))(w2)
    return out.reshape(w.shape)


def _add2(a, b):
    a2, b2 = _as2d(a), _as2d(b)
    rows, cols = a2.shape
    bm = _ew_rows(rows)

    def body(a_ref, b_ref, o_ref):
        o_ref[...] = a_ref[...] + b_ref[...]

    spec = pl.BlockSpec((bm, cols), lambda i: (i, 0))
    out = pl.pallas_call(body, name="add2", grid=(rows // bm,), in_specs=[spec, spec], out_specs=spec,
                         out_shape=jax.ShapeDtypeStruct((rows, cols), f32),
                         compiler_params=_cp(("parallel",)))(a2, b2)
    return out.reshape(a.shape)


def _adamw(parts, w, m, v):
    n = parts.shape[0]
    w2, m2, v2 = _as2d(w), _as2d(m), _as2d(v)
    rows, cols = w2.shape
    p3 = parts.reshape(n, rows, cols)
    bm = _ew_rows(rows)
    if bm * cols * 4 > (1 << 20) and bm % 16 == 0:
        bm //= 2

    def body(p_ref, w_ref, m_ref, v_ref, g_ref, d_ref, nm_ref, nv_ref):
        g = p_ref[0]
        for k in range(1, n):
            g = g + p_ref[k]
        mn = ADAM_B1 * m_ref[...] + (1.0 - ADAM_B1) * g
        vn = ADAM_B2 * v_ref[...] + (1.0 - ADAM_B2) * (g * g)
        m_hat = mn / (1.0 - ADAM_B1 ** ADAM_STEP)
        v_hat = vn / (1.0 - ADAM_B2 ** ADAM_STEP)
        g_ref[...] = g
        d_ref[...] = -ADAM_LR * (m_hat / (jnp.sqrt(v_hat) + ADAM_EPS) + ADAM_WD * w_ref[...])
        nm_ref[...] = mn
        nv_ref[...] = vn

    spec = pl.BlockSpec((bm, cols), lambda i: (i, 0))
    outs = pl.pallas_call(
        body, name="adamw", grid=(rows // bm,),
        in_specs=[pl.BlockSpec((n, bm, cols), lambda i: (0, i, 0)), spec, spec, spec],
        out_specs=[spec] * 4, out_shape=[jax.ShapeDtypeStruct((rows, cols), f32)] * 4,
        compiler_params=_cp(("parallel",)),
    )(p3, w2, m2, v2)
    return tuple(o.reshape(w.shape) for o in outs)


ANY = pl.BlockSpec(memory_space=pl.ANY)


def _my_pos():
    return lax.axis_index("x"), lax.axis_index("y"), lax.axis_index("c")


def _allgather8(name, x):
    def body(x_ref, out_ref, send_sems, recv_sems, local_sem):
        mx, my, mc = _my_pos()
        me, sibling = (mx, my, mc), (mx, my, 1 - mc)
        chips = [(1 - mx, my), (mx, 1 - my), (1 - mx, 1 - my)]

        def slot(px, py, pc):
            return out_ref.at[4 * px + 2 * py + pc]

        def copy(k, block, to, src=None):
            return pltpu.make_async_remote_copy(
                src_ref=slot(*block) if src is None else src, dst_ref=slot(*block),
                send_sem=send_sems.at[k], recv_sem=recv_sems.at[k], device_id=to, device_id_type=MESH)

        mine = pltpu.make_async_copy(x_ref, slot(*me), local_sem)
        mine.start()
        first = [copy(0, me, sibling, src=x_ref)]
        first += [copy(1 + j, me, (*chip, mc), src=x_ref) for j, chip in enumerate(chips)]
        for cp in first:
            cp.start()
        passed = [copy(4 + j, (*chip, mc), sibling) for j, chip in enumerate(chips)]
        for j, chip in enumerate(chips):
            copy(1 + j, (*chip, mc), me).wait_recv()
            passed[j].start()
        copy(0, sibling, me).wait_recv()
        for j, chip in enumerate(chips):
            copy(4 + j, (*chip, 1 - mc), me).wait_recv()
        for cp in first + passed:
            cp.wait_send()
        mine.wait()

    return pl.pallas_call(
        body, name=name, in_specs=[ANY], out_specs=ANY,
        out_shape=jax.ShapeDtypeStruct((NDEV,) + x.shape, x.dtype),
        scratch_shapes=[pltpu.SemaphoreType.DMA((7,)), pltpu.SemaphoreType.DMA((7,)), pltpu.SemaphoreType.DMA],
    )(x)


HALF = DEPTH // 2


def _chip_gather(arrs):
    n = len(arrs)

    def body(*refs):
        ins, outs = refs[:n], refs[n:2 * n]
        send_sems, recv_sems, local_sems = refs[2 * n:]
        mx, my, mc = _my_pos()
        chips = [(1 - mx, my), (mx, 1 - my), (1 - mx, 1 - my)]
        mine = 2 * mx + my
        my_half = pl.ds(HALF * mc, HALF)
        sib_half = pl.ds(HALF * (1 - mc), HALF)
        local = [pltpu.make_async_copy(ins[a], outs[a].at[mine], local_sems.at[a]) for a in range(n)]
        for cp in local:
            cp.start()

        def ici(a, j, src_chip, half):
            px, py = chips[j]
            return pltpu.make_async_remote_copy(
                src_ref=ins[a].at[half], dst_ref=outs[a].at[src_chip, half],
                send_sem=send_sems.at[3 * a + j], recv_sem=recv_sems.at[3 * a + j],
                device_id=(px, py, mc), device_id_type=MESH)

        def d2d(a, j, half):
            px, py = chips[j]
            blk = outs[a].at[2 * px + py, half]
            return pltpu.make_async_remote_copy(
                src_ref=blk, dst_ref=blk,
                send_sem=send_sems.at[3 * n + 3 * a + j], recv_sem=recv_sems.at[3 * n + 3 * a + j],
                device_id=(mx, my, 1 - mc), device_id_type=MESH)

        sends = [ici(a, j, mine, my_half) for a in range(n) for j in range(3)]
        for cp in sends:
            cp.start()
        passed = []
        for a in range(n):
            for j, (px, py) in enumerate(chips):
                ici(a, j, 2 * px + py, my_half).wait_recv()
                fwd = d2d(a, j, my_half)
                fwd.start()
                passed.append(fwd)
        for a in range(n):
            for j in range(3):
                d2d(a, j, sib_half).wait_recv()
        for cp in sends + passed:
            cp.wait_send()
        for cp in local:
            cp.wait()

    return pl.pallas_call(
        body, name="chip_gather", in_specs=[ANY] * n, out_specs=[ANY] * n,
        out_shape=[jax.ShapeDtypeStruct((NCHIP,) + a.shape, a.dtype) for a in arrs],
        scratch_shapes=[pltpu.SemaphoreType.DMA((6 * n,)), pltpu.SemaphoreType.DMA((6 * n,)),
                        pltpu.SemaphoreType.DMA((n,))],
    )(*arrs)


def _sibling_half_swap(arrs):
    n = len(arrs)

    def body(*refs):
        ins, outs = refs[:n], refs[n:2 * n]
        send_sems, recv_sems = refs[2 * n:]
        mx, my, mc = _my_pos()
        sib_half = pl.ds(HALF * (1 - mc), HALF)
        cps = [pltpu.make_async_remote_copy(
            src_ref=ins[a].at[j, sib_half], dst_ref=outs[a].at[j],
            send_sem=send_sems.at[NCHIP * a + j], recv_sem=recv_sems.at[NCHIP * a + j],
            device_id=(mx, my, 1 - mc), device_id_type=MESH) for a in range(n) for j in range(NCHIP)]
        for cp in cps:
            cp.start()
        for cp in cps:
            cp.wait()

    return pl.pallas_call(
        body, name="sibling_half_swap", in_specs=[ANY] * n, out_specs=[ANY] * n,
        out_shape=[jax.ShapeDtypeStruct((NCHIP, HALF) + a.shape[2:], a.dtype) for a in arrs],
        scratch_shapes=[pltpu.SemaphoreType.DMA((NCHIP * n,)), pltpu.SemaphoreType.DMA((NCHIP * n,))],
    )(*arrs)


def _sibling_merge_halves(arrs):
    n = len(arrs)

    def body(*refs):
        ins, outs = refs[:n], refs[n:2 * n]
        send_sems, recv_sems, local_sems = refs[2 * n:]
        mx, my, mc = _my_pos()
        my_half = pl.ds(HALF * mc, HALF)
        local = [pltpu.make_async_copy(ins[a], outs[a].at[my_half], local_sems.at[a]) for a in range(n)]
        cps = [pltpu.make_async_remote_copy(
            src_ref=ins[a], dst_ref=outs[a].at[my_half],
            send_sem=send_sems.at[a], recv_sem=recv_sems.at[a],
            device_id=(mx, my, 1 - mc), device_id_type=MESH) for a in range(n)]
        for cp in local + cps:
            cp.start()
        for cp in cps:
            cp.wait()
        for cp in local:
            cp.wait()

    return pl.pallas_call(
        body, name="sibling_merge_halves", in_specs=[ANY] * n, out_specs=[ANY] * n,
        out_shape=[jax.ShapeDtypeStruct((DEPTH,) + a.shape[1:], a.dtype) for a in arrs],
        scratch_shapes=[pltpu.SemaphoreType.DMA((n,)), pltpu.SemaphoreType.DMA((n,)),
                        pltpu.SemaphoreType.DMA((n,))],
    )(*arrs)


def _sum_parts(parts):
    n = parts.shape[0]
    p3 = parts.reshape(n, -1, parts.shape[-1])
    rows, cols = p3.shape[1:]
    bm = _ew_rows(rows)
    if bm * cols * 4 > (1 << 20) and bm % 16 == 0:
        bm //= 2

    def body(p_ref, o_ref):
        g = p_ref[0]
        for k in range(1, n):
            g = g + p_ref[k]
        o_ref[...] = g

    out = pl.pallas_call(
        body, name="sum_parts", grid=(rows // bm,),
        in_specs=[pl.BlockSpec((n, bm, cols), lambda i: (0, i, 0))],
        out_specs=pl.BlockSpec((bm, cols), lambda i: (i, 0)),
        out_shape=jax.ShapeDtypeStruct((rows, cols), f32),
        compiler_params=_cp(("parallel",)),
    )(p3)
    return out.reshape(parts.shape[1:])


def _chip_scatter(arrs):
    n = len(arrs)

    def body(*refs):
        ins, outs = refs[:n], refs[n:2 * n]
        send_sems, recv_sems, local_sems = refs[2 * n:]
        mx, my, mc = _my_pos()
        chips = [(1 - mx, my), (mx, 1 - my), (1 - mx, 1 - my)]
        mine = 2 * mx + my
        local = [pltpu.make_async_copy(ins[a].at[mine], outs[a].at[mine], local_sems.at[a]) for a in range(n)]
        for cp in local:
            cp.start()
        sends = []
        for a in range(n):
            for j, (px, py) in enumerate(chips):
                sends.append(pltpu.make_async_remote_copy(
                    src_ref=ins[a].at[2 * px + py], dst_ref=outs[a].at[mine],
                    send_sem=send_sems.at[3 * a + j], recv_sem=recv_sems.at[3 * a + j],
                    device_id=(px, py, mc), device_id_type=MESH))
        for cp in sends:
            cp.start()
        for a in range(n):
            for j, (px, py) in enumerate(chips):
                pltpu.make_async_remote_copy(
                    src_ref=ins[a].at[mine], dst_ref=outs[a].at[2 * px + py],
                    send_sem=send_sems.at[3 * a + j], recv_sem=recv_sems.at[3 * a + j],
                    device_id=(px, py, mc), device_id_type=MESH).wait_recv()
        for cp in sends:
            cp.wait_send()
        for cp in local:
            cp.wait()

    return pl.pallas_call(
        body, name="chip_scatter", in_specs=[ANY] * n, out_specs=[ANY] * n,
        out_shape=[jax.ShapeDtypeStruct(a.shape, a.dtype) for a in arrs],
        scratch_shapes=[pltpu.SemaphoreType.DMA((3 * n,)), pltpu.SemaphoreType.DMA((3 * n,)),
                        pltpu.SemaphoreType.DMA((n,))],
    )(*arrs)


def _layer_fwd(x, res, res_gate, wl, mod):
    L = x.shape[0]
    bm = _row_block(L, 512)
    nb = L // bm
    sh1, sc1, g1, sh2, sc2, g2 = [mod[k * D:(k + 1) * D][None, :] for k in range(NMOD)]
    sv = {}
    xin, h = _rmsmod_fwd(x, res, res_gate, wl["norm1_g"], sh1, sc1)
    if xin is None:
        xin = x
    p = _mm("mm_in", h, wl["w_in"], dims=NN, grid=(nb, 4, 1),
            a_block=(bm, D), a_map=lambda i, j, k: (i, 0),
            b_block=(None, D, 512), b_map=lambda i, j, k: (j, 0, 0),
            out_shape=(L, INW), o_block=(bm, 512), o_map=lambda i, j, k: (i, j))
    qkv = _qknorm_fwd(p, wl["gqk"])
    oa = _attn_fwd(qkv)
    os_, s_re, s_im = _s5_fwd(p, wl["abar_re"], wl["abar_im"], wl["bbr"], wl["bbi"], wl["ccr"], wl["cci"],
                              wl["dsk"], wl["gw"], wl["gb"])
    o = _outnorm_fwd(oa, os_, wl["attn_out_g"], wl["ssm_out_g"])
    proj = _mm("mm_out", o, wl["w_out"], dims=NN, grid=(nb, 2, 1),
               a_block=(bm, D), a_map=lambda i, j, k: (i, 0),
               b_block=(D, 512), b_map=lambda i, j, k: (0, j),
               out_shape=(L, D), o_block=(bm, 512), o_map=lambda i, j, k: (i, j))
    x1, h2 = _rmsmod_fwd(xin, proj, g1, wl["norm2_g"], sh2, sc2)
    up_pre = _mm("mm_up", h2, wl["w_up"], dims=NN, grid=(nb, 4, 1),
                 a_block=(bm, D), a_map=lambda i, j, k: (i, 0),
                 b_block=(None, D, 1408), b_map=lambda i, j, k: (j, 0, 0),
                 out_shape=(L, DUP), o_block=(bm, 1408), o_map=lambda i, j, k: (i, j))
    a = _convglu_fwd(up_pre, wl["conv_w"], wl["conv_b"])
    down = _mm("mm_down", a, wl["w_down"], dims=NN, grid=(nb, 2, 1),
               a_block=(bm, DFF), a_map=lambda i, j, k: (i, 0),
               b_block=(DFF, 512), b_map=lambda i, j, k: (0, j),
               out_shape=(L, D), o_block=(bm, 512), o_map=lambda i, j, k: (i, j))
    sv.update(xin=xin, h=h, p=p, qkv=qkv, oa=oa, os=os_, s_re=s_re, s_im=s_im, o=o, proj=proj,
              x1=x1, h2=h2, up_pre=up_pre, a=a, down=down,
              sc1=sc1, g1=g1, sc2=sc2, g2=g2)
    return x1, down, g2, sv


def _layer_bwd(dx_out, wl, sv):
    L = dx_out.shape[0]
    bm = _row_block(L, 512)
    nb = L // bm
    bk = _row_block(L, 512)
    nk = L // bk
    g = {}
    dd, dg2 = _gate_bwd(dx_out, sv["down"], sv["g2"])
    da = _mm("mm_down_dx", dd, wl["w_down"], dims=NT, grid=(nb, 2, 1),
             a_block=(bm, D), a_map=lambda i, j, k: (i, 0),
             b_block=(1408, D), b_map=lambda i, j, k: (j, 0),
             out_shape=(L, DFF), o_block=(bm, 1408), o_map=lambda i, j, k: (i, j))
    g["w_down"] = _mm("mm_down_dw", sv["a"], dd, dims=TN, grid=(2, 2, nk),
                      a_block=(bk, 1408), a_map=lambda i, j, k: (k, i),
                      b_block=(bk, 512), b_map=lambda i, j, k: (k, j),
                      out_shape=(DFF, D), o_block=(1408, 512), o_map=lambda i, j, k: (i, j))
    dup, g["conv_w"], g["conv_b"] = _convglu_bwd(da, sv["up_pre"], wl["conv_w"], wl["conv_b"])
    dh2 = _mm("mm_up_dx", dup, wl["w_up"], dims=NT, grid=(nb, 1, 4),
              a_block=(bm, 1408), a_map=lambda i, j, k: (i, k),
              b_block=(None, D, 1408), b_map=lambda i, j, k: (k, 0, 0),
              out_shape=(L, D), o_block=(bm, D), o_map=lambda i, j, k: (i, 0))
    g["w_up"] = _mm("mm_up_dw", sv["h2"], dup, dims=TN, grid=(2, 4, nk),
                    a_block=(bk, 512), a_map=lambda i, j, k: (k, i),
                    b_block=(bk, 1408), b_map=lambda i, j, k: (k, j),
                    out_shape=(NCHIP, D, 1408), o_block=(None, 512, 1408), o_map=lambda i, j, k: (j, i, 0))
    dx1, dsh2, dsc2, g["norm2_g"] = _rmsmod_bwd(dh2, sv["x1"], wl["norm2_g"], sv["sc2"], dx_out)
    dproj, dg1 = _gate_bwd(dx1, sv["proj"], sv["g1"])
    do = _mm("mm_out_dx", dproj, wl["w_out"], dims=NT, grid=(nb, 2, 1),
             a_block=(bm, D), a_map=lambda i, j, k: (i, 0),
             b_block=(512, D), b_map=lambda i, j, k: (j, 0),
             out_shape=(L, D), o_block=(bm, 512), o_map=lambda i, j, k: (i, j))
    g["w_out"] = _mm("mm_out_dw", sv["o"], dproj, dims=TN, grid=(2, 2, nk),
                     a_block=(bk, 512), a_map=lambda i, j, k: (k, i),
                     b_block=(bk, 512), b_map=lambda i, j, k: (k, j),
                     out_shape=(D, D), o_block=(512, 512), o_map=lambda i, j, k: (i, j))
    doa, dos, g["attn_out_g"], g["ssm_out_g"] = _outnorm_bwd(do, sv["oa"], sv["os"],
                                                             wl["attn_out_g"], wl["ssm_out_g"])
    dqn, dkn, dv = _attn_bwd(sv["qkv"], sv["oa"], doa)
    dqk, dgqk = _qknorm_bwd(jnp.concatenate([dqn, dkn], axis=1), sv["p"], wl["gqk"])
    g["q_norm_g"] = dgqk[0, :HD]
    g["k_norm_g"] = dgqk[0, HD:2 * HD]
    (du, dy16, yg16, dgl16, lam_re, lam_im, d_abar_re, d_abar_im, g["ssm_d"], g["glu_b"]) = _s5_bwd(
        dos, sv["p"], sv["s_re"], sv["s_im"], wl["abar_re"], wl["abar_im"], wl["bbr"], wl["bbi"],
        wl["ccr"], wl["cci"], wl["dsk"], wl["gw"], wl["gb"])

    def tn_dense(name, a, a_col, a_cols, b, b_cols):
        return _mm(name, a, b, dims=TN, grid=(a_cols // 512, b_cols // 512, nk),
                   a_block=(bk, 512), a_map=lambda i, j, k: (k, a_col + i),
                   b_block=(bk, 512), b_map=lambda i, j, k: (k, j),
                   out_shape=(a_cols, b_cols), o_block=(512, 512), o_map=lambda i, j, k: (i, j))

    d_ccr = tn_dense("s5_dcc", sv["s_re"], 0, NS, dy16, SSMW)
    d_cci = tn_dense("s5_dcc", sv["s_im"], 0, NS, dy16, SSMW)
    d_bbr = tn_dense("s5_dbb", sv["p"], 3, SSMW, lam_re, NS)
    d_bbi = tn_dense("s5_dbb", sv["p"], 3, SSMW, lam_im, NS)
    d_gw = tn_dense("s5_dgw", yg16, 0, SSMW, dgl16, SSMW)
    g["ssm_c_re"] = _diag_blocks(d_ccr, NP, NH16).transpose(0, 2, 1)
    g["ssm_c_im"] = -_diag_blocks(d_cci, NP, NH16).transpose(0, 2, 1)
    g["glu_w"] = _diag_blocks(d_gw, NH16, NH16)
    d_bbre = _diag_blocks(d_bbr, NH16, NP).transpose(0, 2, 1).reshape(NS, NH16)
    d_bbim = _diag_blocks(d_bbi, NH16, NP).transpose(0, 2, 1).reshape(NS, NH16)
    d_ldt, d_ar, d_ai, d_br, d_bi = _s5_disc_bwd(
        wl["ldt_col"], wl["ar_col"], wl["ai_col"], wl["br_mat"], wl["bi_mat"],
        d_abar_re.reshape(NS, 1), d_abar_im.reshape(NS, 1), d_bbre, d_bbim)
    g["ssm_log_dt"] = d_ldt[:, 0]
    g["ssm_a_re"] = d_ar.reshape(NG, NP)
    g["ssm_a_im"] = d_ai.reshape(NG, NP)
    g["ssm_b_re"] = d_br.reshape(NG, NP, NH16)
    g["ssm_b_im"] = d_bi.reshape(NG, NP, NH16)
    dp = jnp.concatenate([dqk, dv.astype(bf16), du.astype(bf16)], axis=1)
    dh = _mm("mm_in_dx", dp, wl["w_in"], dims=NT, grid=(nb, 1, 4),
             a_block=(bm, 512), a_map=lambda i, j, k: (i, k),
             b_block=(None, D, 512), b_map=lambda i, j, k: (k, 0, 0),
             out_shape=(L, D), o_block=(bm, D), o_map=lambda i, j, k: (i, 0))
    g["w_in"] = _mm("mm_in_dw", sv["h"], dp, dims=TN, grid=(2, 4, nk),
                    a_block=(bk, 512), a_map=lambda i, j, k: (k, i),
                    b_block=(bk, 512), b_map=lambda i, j, k: (k, j),
                    out_shape=(NCHIP, D, 512), o_block=(None, 512, 512), o_map=lambda i, j, k: (j, i, 0))
    dx_in, dsh1, dsc1, g["norm1_g"] = _rmsmod_bwd(dh, sv["xin"], wl["norm1_g"], sv["sc1"], dx1)
    g["dmod"] = jnp.concatenate([dsh1, dsc1, dg1, dsh2, dsc2, dg2], axis=1)[0]
    return dx_in, g


def _prep_layer(l, big, small):
    wl = {}
    wl["w_in"] = big["w_in"][:, l]
    wl["w_up"] = big["w_up"][:, l]
    wl["w_out"] = big["w_out"][:, l].reshape(D, D)
    wl["w_down"] = big["w_down"][:, l].reshape(DFF, D)
    wl["conv_w"] = big["conv_w"][:, l].transpose(1, 0, 2).reshape(3, DUP)
    wl["conv_b"] = small["ffn_conv_b"][l][None, :]
    wl["norm1_g"] = small["norm1_g"][l][None, :]
    wl["norm2_g"] = small["norm2_g"][l][None, :]
    wl["attn_out_g"] = small["attn_out_g"][l][None, :]
    wl["ssm_out_g"] = small["ssm_out_g"][l][None, :]
    wl["gqk"] = jnp.concatenate([jnp.tile(small["q_norm_g"][l], 8), jnp.tile(small["k_norm_g"][l], 8)])[None, :]
    wl["ldt_col"] = jnp.repeat(small["ssm_log_dt"][l], NP)[:, None]
    wl["ar_col"] = small["ssm_a_re"][l].reshape(NS, 1)
    wl["ai_col"] = small["ssm_a_im"][l].reshape(NS, 1)
    wl["br_mat"] = small["ssm_b_re"][l].reshape(NS, NH16)
    wl["bi_mat"] = small["ssm_b_im"][l].reshape(NS, NH16)
    abar_re, abar_im, bb_re, bb_im = _s5_disc_fwd(wl["ldt_col"], wl["ar_col"], wl["ai_col"],
                                                  wl["br_mat"], wl["bi_mat"])
    wl["abar_re"] = abar_re.reshape(1, NS)
    wl["abar_im"] = abar_im.reshape(1, NS)
    wl["bbr"] = _block_diag(bb_re.reshape(NG, NP, NH16).transpose(0, 2, 1)).astype(bf16)
    wl["bbi"] = _block_diag(bb_im.reshape(NG, NP, NH16).transpose(0, 2, 1)).astype(bf16)
    wl["ccr"] = _block_diag(small["ssm_c_re"][l].transpose(0, 2, 1)).astype(bf16)
    wl["cci"] = _block_diag(small["ssm_c_im"][l].transpose(0, 2, 1)).astype(bf16)
    wl["gw"] = _block_diag(small["glu_w"][l]).astype(bf16)
    wl["dsk"] = small["ssm_d"][l].reshape(1, SSMW)
    wl["gb"] = small["glu_b"][l].reshape(1, SSMW)
    return wl


def _local_step(x, target, mods, big, small):
    wls = [_prep_layer(l, big, small) for l in range(DEPTH)]
    saved = []
    cur, res, res_gate = x, None, None
    for l in range(DEPTH):
        cur, res, res_gate, sv = _layer_fwd(cur, res, res_gate, wls[l], mods[l])
        saved.append(sv)
    loss, dx = _loss_fwd_bwd(cur, res, res_gate, target)
    grads = [None] * DEPTH
    for l in reversed(range(DEPTH)):
        dx, grads[l] = _layer_bwd(dx, wls[l], saved[l])
    return loss, dx, grads


BIG = ("w_in", "w_out", "ffn_w_up", "ffn_w_down")
SMALL = ("ada_b", "norm1_g", "q_norm_g", "k_norm_g", "ssm_a_re", "ssm_a_im", "ssm_log_dt",
         "ssm_b_re", "ssm_b_im", "ssm_c_re", "ssm_c_im", "ssm_d", "glu_w", "glu_b",
         "attn_out_g", "ssm_out_g", "norm2_g", "ffn_conv_b")
NAMES = ("ada_w", "ada_b", "norm1_g", "w_in", "q_norm_g", "k_norm_g", "ssm_a_re", "ssm_a_im",
         "ssm_log_dt", "ssm_b_re", "ssm_b_im", "ssm_c_re", "ssm_c_im", "ssm_d", "glu_w", "glu_b",
         "attn_out_g", "ssm_out_g", "w_out", "norm2_g", "ffn_w_up", "ffn_conv_w", "ffn_conv_b",
         "ffn_w_down")
PACK_COLS = 1024


def _pack(arrs):
    flat = jnp.concatenate([a.reshape(-1) for a in arrs])
    rows = -(-flat.shape[0] // PACK_COLS)
    rows = -(-rows // 8) * 8
    flat = jnp.pad(flat, (0, rows * PACK_COLS - flat.shape[0]))
    return flat.reshape(rows, PACK_COLS)


def _unpack(packed, shapes):
    flat = packed.reshape(-1)
    out, off = [], 0
    for s in shapes:
        n = math.prod(s)
        out.append(flat[off:off + n].reshape(s))
        off += n
    return out


def kernel(x, c, ada_w, ada_b, norm1_g, w_in, q_norm_g, k_norm_g, ssm_a_re, ssm_a_im, ssm_log_dt, ssm_b_re, ssm_b_im, ssm_c_re, ssm_c_im, ssm_d, glu_w, glu_b, attn_out_g, ssm_out_g, w_out, norm2_g, ffn_w_up, ffn_conv_w, ffn_conv_b, ffn_w_down, loss_target, m_ada_w, m_ada_b, m_norm1_g, m_w_in, m_q_norm_g, m_k_norm_g, m_ssm_a_re, m_ssm_a_im, m_ssm_log_dt, m_ssm_b_re, m_ssm_b_im, m_ssm_c_re, m_ssm_c_im, m_ssm_d, m_glu_w, m_glu_b, m_attn_out_g, m_ssm_out_g, m_w_out, m_norm2_g, m_ffn_w_up, m_ffn_conv_w, m_ffn_conv_b, m_ffn_w_down, v_ada_w, v_ada_b, v_norm1_g, v_w_in, v_q_norm_g, v_k_norm_g, v_ssm_a_re, v_ssm_a_im, v_ssm_log_dt, v_ssm_b_re, v_ssm_b_im, v_ssm_c_re, v_ssm_c_im, v_ssm_d, v_glu_w, v_glu_b, v_attn_out_g, v_ssm_out_g, v_w_out, v_norm2_g, v_ffn_w_up, v_ffn_conv_w, v_ffn_conv_b, v_ffn_w_down):
    env = dict(locals())
    w = {n: env[n] for n in NAMES}
    m = {n: env["m_" + n] for n in NAMES}
    v = {n: env["v_" + n] for n in NAMES}
    mx, my, mc = _my_pos()
    chip = 2 * mx + my
    dev = 4 * mx + 2 * my + mc
    xl = x[0]
    tl = loss_target[0]

    c_all = _allgather8("gather_c", jnp.pad(c, ((0, 7), (0, 0))))[:, 0, :]
    c16 = jnp.pad(c_all, ((0, 8), (0, 0)))
    nloc = NMOD * D // NCHIP
    ada_b_loc = lax.dynamic_slice(ada_b, (0, chip * nloc), (DEPTH, nloc))[:, None, :]
    mod_loc = _mod_fwd(c16, ada_w, ada_b_loc)
    mod_g = _allgather8("gather_mod", mod_loc.reshape(DEPTH * 16, nloc))
    mod_all = mod_g[0::2].reshape(NCHIP, DEPTH, 16, nloc).transpose(1, 2, 0, 3).reshape(DEPTH, 16, NMOD * D)
    mods = lax.dynamic_index_in_dim(mod_all, dev, axis=1, keepdims=False)

    gathered = _chip_gather([_cast_bf16(w["w_in"]), _cast_bf16(w["w_out"]), _cast_bf16(w["ffn_w_up"]),
                             _cast_bf16(w["ffn_w_down"]), w["ffn_conv_w"]])
    big = dict(zip(("w_in", "w_out", "w_up", "w_down", "conv_w"), gathered))
    small = {n: w[n] for n in SMALL}

    loss_loc, grad_x, grads = _local_step(xl, tl, mods, big, small)
    loss = lax.psum(loss_loc, ("x", "y", "c"))

    def stacked(name):
        return jnp.stack([grads[l][name] for l in range(DEPTH)], axis=1)

    loc = [stacked("w_in"),
           jnp.stack([grads[l]["w_out"].reshape(NCHIP, D // NCHIP, D) for l in range(DEPTH)], axis=1),
           stacked("w_up"),
           jnp.stack([grads[l]["w_down"].reshape(NCHIP, DFF // NCHIP, D) for l in range(DEPTH)], axis=1)]
    recv = _sibling_half_swap(loc)
    mine_half = [lax.dynamic_slice_in_dim(a, HALF * mc, HALF, axis=1) for a in loc]
    chip_sum = [_add2(a, b) for a, b in zip(mine_half, recv)]
    parts = _chip_scatter(chip_sum)
    g_full = _sibling_merge_halves([_sum_parts(p) for p in parts])
    outs = {}
    for name, g_big in zip(BIG, g_full):
        outs[name] = _adamw(g_big[None], w[name], m[name], v[name])

    small_names = SMALL[1:] + ("ffn_conv_w_full",)
    small_grads = []
    for n in SMALL[1:]:
        key = {"ffn_conv_b": "conv_b"}.get(n, n)
        small_grads.append(jnp.stack([grads[l][key].reshape(w[n].shape[1:]) for l in range(DEPTH)]))
    small_grads.append(jnp.stack([grads[l]["conv_w"] for l in range(DEPTH)]))
    dmod = jnp.stack([grads[l]["dmod"] for l in range(DEPTH)])
    packed = _pack([dmod] + small_grads)
    allp = _allgather8("gather_small", packed)
    shapes = [dmod.shape] + [a.shape for a in small_grads]
    dmod_all = allp[:, :DEPTH * NMOD, :].reshape(NDEV, DEPTH, NMOD * D)
    dmod_loc = lax.dynamic_slice(dmod_all, (0, 0, chip * nloc), (NDEV, DEPTH, nloc)).transpose(1, 0, 2)
    g_ada = _ada_grad(c16, jnp.pad(dmod_loc, ((0, 0), (0, 8), (0, 0))))
    outs["ada_w"] = _adamw(g_ada[None], w["ada_w"], m["ada_w"], v["ada_w"])
    conv_w_shape = (DEPTH, 3, DUP)
    w_small = _pack([w[n] for n in SMALL] + [jnp.zeros(conv_w_shape, f32)])
    m_small = _pack([m[n] for n in SMALL] + [jnp.zeros(conv_w_shape, f32)])
    v_small = _pack([v[n] for n in SMALL] + [jnp.ones(conv_w_shape, f32)])
    res_small = _adamw(allp, w_small, m_small, v_small)
    unpacked = [_unpack(r, shapes) for r in res_small]
    for i, n in enumerate(SMALL):
        outs[n] = tuple(unpacked[k][i] for k in range(4))
    g_conv_full = unpacked[0][len(SMALL)]
    ncw = DUP // NCHIP
    g_conv = lax.dynamic_slice(g_conv_full, (0, 0, chip * ncw), (DEPTH, 3, ncw))
    outs["ffn_conv_w"] = _adamw(g_conv[None], w["ffn_conv_w"], m["ffn_conv_w"], v["ffn_conv_w"])

    result = [loss, grad_x[None]]
    for k in range(4):
        result += [outs[n][k] for n in NAMES]
    return tuple(result)
```

```python
import functools
import math

import jax
import jax.numpy as jnp
from jax import lax
from jax.experimental import pallas as pl
from jax.experimental.pallas import tpu as pltpu

f32 = jnp.float32
bf16 = jnp.bfloat16
_MXU = jnp.bfloat16

D = 1024
ATT = 512
SSMW = 512
HD = 64
NG = 32
NP = 64
NH16 = 16
NS = NG * NP
INW = 2048
DFF = 2816
DUP = 2 * DFF
NMOD = 6
DEPTH = 4
EPS = 1e-6
NCHIP = 4
NDEV = 8

ADAM_LR = 0.001
ADAM_B1 = 0.9
ADAM_B2 = 0.999
ADAM_EPS = 1e-08
ADAM_WD = 0.01
ADAM_STEP = 10

VMEM_LIMIT = 56 * 1024 * 1024
MESH = pl.DeviceIdType.MESH

NN = (((1,), (0,)), ((), ()))
NT = (((1,), (1,)), ((), ()))
TN = (((0,), (0,)), ((), ()))


def _cp(sem=None):
    if sem is None:
        return pltpu.CompilerParams(vmem_limit_bytes=VMEM_LIMIT)
    return pltpu.CompilerParams(dimension_semantics=sem, vmem_limit_bytes=VMEM_LIMIT)


def _dot(a, b, dims=NN):
    return lax.dot_general(a.astype(_MXU), b.astype(_MXU), dims, preferred_element_type=f32)


def _dot_exact(x, m):
    hi = x.astype(bf16)
    r1 = x - hi.astype(f32)
    mid = r1.astype(bf16)
    lo = (r1 - mid.astype(f32)).astype(bf16)
    d = lambda p: lax.dot_general(p, m, NN, preferred_element_type=f32)
    return d(hi) + d(mid) + d(lo)


def _gelu(x):
    c = math.sqrt(2.0 / math.pi)
    return 0.5 * x * (1.0 + jnp.tanh(c * (x + 0.044715 * (x * x * x))))


def _gelu_grad(x):
    c = math.sqrt(2.0 / math.pi)
    t = jnp.tanh(c * (x + 0.044715 * (x * x * x)))
    return 0.5 * (1.0 + t) + 0.5 * x * (1.0 - t * t) * c * (1.0 + 3.0 * 0.044715 * (x * x))


def _sigmoid(x):
    return 1.0 / (1.0 + jnp.exp(-x))


def _mm(name, a, b, *, dims, grid, a_block, a_map, b_block, b_map, out_shape, o_block, o_map,
        out_dtype=f32):
    nk = grid[2]
    acc_shape = tuple(s for s in o_block if s is not None)

    def body(a_ref, b_ref, o_ref, acc_ref):
        k = pl.program_id(2)
        part = _dot(a_ref[...], b_ref[...], dims)
        if nk == 1:
            o_ref[...] = part.astype(o_ref.dtype)
        else:
            @pl.when(k == 0)
            def _():
                acc_ref[...] = part

            @pl.when(k > 0)
            def _():
                acc_ref[...] += part

            @pl.when(k == nk - 1)
            def _():
                o_ref[...] = acc_ref[...].astype(o_ref.dtype)

    return pl.pallas_call(
        body, name=name, grid=grid,
        in_specs=[pl.BlockSpec(a_block, a_map), pl.BlockSpec(b_block, b_map)],
        out_specs=pl.BlockSpec(o_block, o_map),
        out_shape=jax.ShapeDtypeStruct(out_shape, out_dtype),
        scratch_shapes=[pltpu.VMEM(acc_shape if nk > 1 else (8, 128), f32)],
        compiler_params=_cp(("parallel", "parallel", "arbitrary")),
    )(a, b)


def _row_block(L, want):
    return want if L % want == 0 else L


def _rmsmod_fwd(x, res, gate, g, sh, sc):
    L = x.shape[0]
    bm = _row_block(L, 256)
    with_res = res is not None

    def body(*refs):
        if with_res:
            x_ref, r_ref, gt_ref, g_ref, sh_ref, sc_ref, xo_ref, h_ref = refs
            xin = x_ref[...] + gt_ref[...] * r_ref[...]
            xo_ref[...] = xin
        else:
            x_ref, g_ref, sh_ref, sc_ref, h_ref = refs
            xin = x_ref[...]
        inv = lax.rsqrt(jnp.mean(xin * xin, axis=-1, keepdims=True) + EPS)
        xn = xin * inv * g_ref[...]
        h_ref[...] = (xn * (1.0 + sc_ref[...]) + sh_ref[...]).astype(h_ref.dtype)

    row = pl.BlockSpec((bm, D), lambda i: (i, 0))
    vec = pl.BlockSpec((1, D), lambda i: (0, 0))
    if with_res:
        return pl.pallas_call(
            body, name="rmsmod_res_fwd", grid=(L // bm,),
            in_specs=[row, row, vec, vec, vec, vec], out_specs=[row, row],
            out_shape=[jax.ShapeDtypeStruct((L, D), f32), jax.ShapeDtypeStruct((L, D), bf16)],
            compiler_params=_cp(("parallel",)),
        )(x, res, gate, g, sh, sc)
    h = pl.pallas_call(
        body, name="rmsmod_fwd", grid=(L // bm,),
        in_specs=[row, vec, vec, vec], out_specs=row,
        out_shape=jax.ShapeDtypeStruct((L, D), bf16),
        compiler_params=_cp(("parallel",)),
    )(x, g, sh, sc)
    return None, h


def _rmsmod_bwd(dh, x, g, sc, dres):
    L = x.shape[0]
    bm = _row_block(L, 256)

    def body(dh_ref, x_ref, g_ref, sc_ref, dr_ref, dx_ref, dsh_ref, dsc_ref, dg_ref):
        i = pl.program_id(0)
        xv = x_ref[...]
        dhv = dh_ref[...]
        inv = lax.rsqrt(jnp.mean(xv * xv, axis=-1, keepdims=True) + EPS)
        xh = xv * inv
        gv = g_ref[...]
        xn = xh * gv
        dxn = dhv * (1.0 + sc_ref[...])
        dxh = dxn * gv
        dx_ref[...] = inv * (dxh - xh * jnp.mean(dxh * xh, axis=-1, keepdims=True)) + dr_ref[...]
        p_sh = jnp.sum(dhv, axis=0, keepdims=True)
        p_sc = jnp.sum(dhv * xn, axis=0, keepdims=True)
        p_g = jnp.sum(dxn * xh, axis=0, keepdims=True)

        @pl.when(i == 0)
        def _():
            dsh_ref[...] = p_sh
            dsc_ref[...] = p_sc
            dg_ref[...] = p_g

        @pl.when(i > 0)
        def _():
            dsh_ref[...] += p_sh
            dsc_ref[...] += p_sc
            dg_ref[...] += p_g

    row = pl.BlockSpec((bm, D), lambda i: (i, 0))
    vec = pl.BlockSpec((1, D), lambda i: (0, 0))
    return pl.pallas_call(
        body, name="rmsmod_bwd", grid=(L // bm,),
        in_specs=[row, row, vec, vec, row], out_specs=[row, vec, vec, vec],
        out_shape=[jax.ShapeDtypeStruct((L, D), f32)] + [jax.ShapeDtypeStruct((1, D), f32)] * 3,
        compiler_params=_cp(("arbitrary",)),
    )(dh, x, g, sc, dres)


def _gate_bwd(dx, y, gate):
    L = dx.shape[0]
    bm = _row_block(L, 256)

    def body(dx_ref, y_ref, gt_ref, dy_ref, dgt_ref):
        i = pl.program_id(0)
        dxv = dx_ref[...]
        dy_ref[...] = (gt_ref[...] * dxv).astype(dy_ref.dtype)
        part = jnp.sum(dxv * y_ref[...], axis=0, keepdims=True)

        @pl.when(i == 0)
        def _():
            dgt_ref[...] = part

        @pl.when(i > 0)
        def _():
            dgt_ref[...] += part

    row = pl.BlockSpec((bm, D), lambda i: (i, 0))
    vec = pl.BlockSpec((1, D), lambda i: (0, 0))
    return pl.pallas_call(
        body, name="gate_bwd", grid=(L // bm,),
        in_specs=[row, row, vec], out_specs=[row, vec],
        out_shape=[jax.ShapeDtypeStruct((L, D), bf16), jax.ShapeDtypeStruct((1, D), f32)],
        compiler_params=_cp(("arbitrary",)),
    )(dx, y, gate)


def _loss_fwd_bwd(x1, down, gate, target):
    L = x1.shape[0]
    bm = _row_block(L, 256)
    nsteps = L // bm

    def body(x_ref, d_ref, gt_ref, t_ref, dy_ref, loss_ref, acc_ref):
        i = pl.program_id(0)
        diff = x_ref[...] + gt_ref[...] * d_ref[...] - t_ref[...]
        dy_ref[...] = diff * (1.0 / D)
        part = jnp.sum(diff * diff, axis=0, keepdims=True)

        @pl.when(i == 0)
        def _():
            acc_ref[...] = part

        @pl.when(i > 0)
        def _():
            acc_ref[...] += part

        @pl.when(i == nsteps - 1)
        def _():
            tot = jnp.sum(acc_ref[...], axis=1, keepdims=True) * (0.5 / D)
            loss_ref[...] = jnp.broadcast_to(tot, (8, 128))

    row = pl.BlockSpec((bm, D), lambda i: (i, 0))
    vec = pl.BlockSpec((1, D), lambda i: (0, 0))
    dy, loss = pl.pallas_call(
        body, name="loss_fwd_bwd", grid=(nsteps,),
        in_specs=[row, row, vec, row],
        out_specs=[row, pl.BlockSpec((8, 128), lambda i: (0, 0))],
        out_shape=[jax.ShapeDtypeStruct((L, D), f32), jax.ShapeDtypeStruct((8, 128), f32)],
        scratch_shapes=[pltpu.VMEM((1, D), f32)],
        compiler_params=_cp(("arbitrary",)),
    )(x1, down, gate, target)
    return loss[0, 0], dy


def _head_mean_matrix():
    r = lax.broadcasted_iota(jnp.int32, (128, 128), 0) // HD
    c = lax.broadcasted_iota(jnp.int32, (128, 128), 1) // HD
    return jnp.where(r == c, 1.0 / HD, 0.0).astype(bf16)


def _qknorm_fwd(p, gqk):
    L = p.shape[0]
    bm = _row_block(L, 256)

    def body(p_ref, g_ref, o_ref):
        e = _head_mean_matrix()
        for c in range(8):
            sl = slice(128 * c, 128 * (c + 1))
            xv = p_ref[:, sl]
            inv = lax.rsqrt(_dot_exact(xv * xv, e) + EPS)
            yv = xv * inv * g_ref[:, sl]
            if c < 4:
                yv = yv * (1.0 / math.sqrt(HD))
            o_ref[:, sl] = yv.astype(o_ref.dtype)
        o_ref[:, 1024:1536] = p_ref[:, 1024:1536].astype(o_ref.dtype)

    return pl.pallas_call(
        body, name="qknorm_fwd", grid=(L // bm,),
        in_specs=[pl.BlockSpec((bm, 1536), lambda i: (i, 0)), pl.BlockSpec((1, 1024), lambda i: (0, 0))],
        out_specs=pl.BlockSpec((bm, 1536), lambda i: (i, 0)),
        out_shape=jax.ShapeDtypeStruct((L, 1536), bf16),
        compiler_params=_cp(("parallel",)),
    )(p, gqk)


def _qknorm_bwd(dqk, p, gqk):
    L = p.shape[0]
    bm = _row_block(L, 256)
    nsteps = L // bm

    def body(d_ref, p_ref, g_ref, o_ref, dg_ref, acc_ref):
        i = pl.program_id(0)
        e = _head_mean_matrix()
        for c in range(8):
            sl = slice(128 * c, 128 * (c + 1))
            xv = p_ref[:, sl]
            dv = d_ref[:, sl]
            gv = g_ref[:, sl]
            inv = lax.rsqrt(_dot_exact(xv * xv, e) + EPS)
            xh = xv * inv
            dxh = dv * gv
            o_ref[:, sl] = (inv * (dxh - xh * _dot_exact(dxh * xh, e))).astype(o_ref.dtype)
            part = jnp.sum(dv * xh, axis=0, keepdims=True)

            @pl.when(i == 0)
            def _():
                acc_ref[:, sl] = part

            @pl.when(i > 0)
            def _():
                acc_ref[:, sl] += part

        @pl.when(i == nsteps - 1)
        def _():
            r = lax.broadcasted_iota(jnp.int32, (1024, 128), 0)
            col = lax.broadcasted_iota(jnp.int32, (1024, 128), 1)
            fold = jnp.where(col == (r // 512) * HD + r % HD, 1.0, 0.0).astype(bf16)
            dg_ref[...] = _dot_exact(jnp.broadcast_to(acc_ref[...], (8, 1024)), fold)

    return pl.pallas_call(
        body, name="qknorm_bwd", grid=(nsteps,),
        in_specs=[pl.BlockSpec((bm, 1024), lambda i: (i, 0)), pl.BlockSpec((bm, 1024), lambda i: (i, 0)),
                  pl.BlockSpec((1, 1024), lambda i: (0, 0))],
        out_specs=[pl.BlockSpec((bm, 1024), lambda i: (i, 0)), pl.BlockSpec((8, 128), lambda i: (0, 0))],
        out_shape=[jax.ShapeDtypeStruct((L, 1024), bf16), jax.ShapeDtypeStruct((8, 128), f32)],
        scratch_shapes=[pltpu.VMEM((1, 1024), f32)],
        compiler_params=_cp(("arbitrary",)),
    )(dqk, p, gqk)


def _outnorm_fwd(oa, os_, ga, gs):
    L = oa.shape[0]
    bm = _row_block(L, 256)

    def body(a_ref, s_ref, ga_ref, gs_ref, o_ref):
        for x_ref, g_ref, off in ((a_ref, ga_ref, 0), (s_ref, gs_ref, 512)):
            xv = x_ref[...]
            inv = lax.rsqrt(jnp.mean(xv * xv, axis=-1, keepdims=True) + EPS)
            o_ref[:, off:off + 512] = (xv * inv * g_ref[...]).astype(o_ref.dtype)

    half = pl.BlockSpec((bm, 512), lambda i: (i, 0))
    vec = pl.BlockSpec((1, 512), lambda i: (0, 0))
    return pl.pallas_call(
        body, name="outnorm_fwd", grid=(L // bm,),
        in_specs=[half, half, vec, vec], out_specs=pl.BlockSpec((bm, D), lambda i: (i, 0)),
        out_shape=jax.ShapeDtypeStruct((L, D), bf16),
        compiler_params=_cp(("parallel",)),
    )(oa, os_, ga, gs)


def _outnorm_bwd(do, oa, os_, ga, gs):
    L = oa.shape[0]
    bm = _row_block(L, 256)

    def body(do_ref, a_ref, s_ref, ga_ref, gs_ref, da_ref, ds_ref, dga_ref, dgs_ref):
        i = pl.program_id(0)
        for x_ref, g_ref, dx_ref, dg_ref, off in ((a_ref, ga_ref, da_ref, dga_ref, 0),
                                                  (s_ref, gs_ref, ds_ref, dgs_ref, 512)):
            xv = x_ref[...]
            dv = do_ref[:, off:off + 512]
            inv = lax.rsqrt(jnp.mean(xv * xv, axis=-1, keepdims=True) + EPS)
            xh = xv * inv
            dxh = dv * g_ref[...]
            dx_ref[...] = inv * (dxh - xh * jnp.mean(dxh * xh, axis=-1, keepdims=True))
            part = jnp.sum(dv * xh, axis=0, keepdims=True)

            @pl.when(i == 0)
            def _():
                dg_ref[...] = part

            @pl.when(i > 0)
            def _():
                dg_ref[...] += part

    half = pl.BlockSpec((bm, 512), lambda i: (i, 0))
    vec = pl.BlockSpec((1, 512), lambda i: (0, 0))
    return pl.pallas_call(
        body, name="outnorm_bwd", grid=(L // bm,),
        in_specs=[pl.BlockSpec((bm, D), lambda i: (i, 0)), half, half, vec, vec],
        out_specs=[half, half, vec, vec],
        out_shape=[jax.ShapeDtypeStruct((L, 512), f32)] * 2 + [jax.ShapeDtypeStruct((1, 512), f32)] * 2,
        compiler_params=_cp(("arbitrary",)),
    )(do, oa, os_, ga, gs)


NCB = DFF // 128


def _shift_down(x, s, row):
    return jnp.where(row >= s, pltpu.roll(x, s, 0), 0.0)


def _shift_up(x, s, row, L):
    return jnp.where(row < L - s, pltpu.roll(x, L - s, 0), 0.0)


def _conv(x, w_ref, b_ref, row):
    out = b_ref[...] + _shift_down(x, 2, row) * w_ref[0:1, :]
    out = out + _shift_down(x, 1, row) * w_ref[1:2, :]
    return out + x * w_ref[2:3, :]


def _convglu_fwd(up_pre, cw, cb):
    L = up_pre.shape[0]

    def body(v_ref, g_ref, wv_ref, wg_ref, bv_ref, bg_ref, a_ref):
        row = lax.broadcasted_iota(jnp.int32, (L, 128), 0)
        val = _conv(v_ref[...], wv_ref, bv_ref, row)
        gate = _conv(g_ref[...], wg_ref, bg_ref, row)
        a_ref[...] = (_gelu(gate) * val).astype(a_ref.dtype)

    col = lambda off: pl.BlockSpec((L, 128), lambda j: (0, j + off))
    wsp = lambda off: pl.BlockSpec((3, 128), lambda j: (0, j + off))
    bsp = lambda off: pl.BlockSpec((1, 128), lambda j: (0, j + off))
    return pl.pallas_call(
        body, name="convglu_fwd", grid=(NCB,),
        in_specs=[col(0), col(NCB), wsp(0), wsp(NCB), bsp(0), bsp(NCB)],
        out_specs=pl.BlockSpec((L, 128), lambda j: (0, j)),
        out_shape=jax.ShapeDtypeStruct((L, DFF), bf16),
        compiler_params=_cp(("parallel",)),
    )(up_pre, up_pre, cw, cw, cb, cb)


def _convglu_bwd(da, up_pre, cw, cb):
    L = up_pre.shape[0]

    def body(da_ref, own_ref, par_ref, wo_ref, wp_ref, bo_ref, bp_ref, dx_ref, dw_ref, db_ref):
        j = pl.program_id(0)
        row = lax.broadcasted_iota(jnp.int32, (L, 128), 0)
        xo = own_ref[...]
        u_own = _conv(xo, wo_ref, bo_ref, row)
        u_par = _conv(par_ref[...], wp_ref, bp_ref, row)
        dav = da_ref[...]

        def finish(d):
            dx = d * wo_ref[2:3, :] + _shift_up(d, 1, row, L) * wo_ref[1:2, :]
            dx_ref[...] = (dx + _shift_up(d, 2, row, L) * wo_ref[0:1, :]).astype(dx_ref.dtype)
            dw_ref[0:1, :] = jnp.sum(d * _shift_down(xo, 2, row), axis=0, keepdims=True)
            dw_ref[1:2, :] = jnp.sum(d * _shift_down(xo, 1, row), axis=0, keepdims=True)
            dw_ref[2:3, :] = jnp.sum(d * xo, axis=0, keepdims=True)
            db_ref[...] = jnp.sum(d, axis=0, keepdims=True)

        @pl.when(j < NCB)
        def _():
            finish(dav * _gelu(u_par))

        @pl.when(j >= NCB)
        def _():
            finish(dav * u_par * _gelu_grad(u_own))

    own = lambda j: (0, j)
    par = lambda j: (0, (j + NCB) % (2 * NCB))
    return pl.pallas_call(
        body, name="convglu_bwd", grid=(2 * NCB,),
        in_specs=[pl.BlockSpec((L, 128), lambda j: (0, j % NCB)),
                  pl.BlockSpec((L, 128), own), pl.BlockSpec((L, 128), par),
                  pl.BlockSpec((3, 128), own), pl.BlockSpec((3, 128), par),
                  pl.BlockSpec((1, 128), own), pl.BlockSpec((1, 128), par)],
        out_specs=[pl.BlockSpec((L, 128), own), pl.BlockSpec((3, 128), own), pl.BlockSpec((1, 128), own)],
        out_shape=[jax.ShapeDtypeStruct((L, DUP), bf16), jax.ShapeDtypeStruct((3, DUP), f32),
                   jax.ShapeDtypeStruct((1, DUP), f32)],
        compiler_params=_cp(("parallel",)),
    )(da, up_pre, up_pre, cw, cw, cb, cb)


def _attn_block(L):
    return 256 if L % 256 == 0 else 128


TAIL_DEAD = -110.0


def _dot_exact2(x, m):
    hi = x.astype(bf16)
    lo = (x - hi.astype(f32)).astype(bf16)
    return (lax.dot_general(hi, m, NN, preferred_element_type=f32)
            + lax.dot_general(lo, m, NN, preferred_element_type=f32))


def _sb_weights(z, mask, tri_gt, carry):
    l1p = jnp.log(1.0 + jnp.exp(-jnp.abs(z)))
    ls_pos = jnp.minimum(z, 0.0) - l1p
    lm = ls_pos - z
    if mask is not None:
        lm = jnp.where(mask, lm, 0.0)
    tail = _dot_exact2(lm, tri_gt) + carry
    w = jnp.exp(ls_pos + tail)
    if mask is not None:
        w = jnp.where(mask, w, 0.0)
    return w, ls_pos, lm


def _head_masks(B):
    lane = lax.broadcasted_iota(jnp.int32, (B, 128), 1)
    return lane < HD, lane >= HD


def _attn_fwd(qkv):
    L = qkv.shape[0]
    B = _attn_block(L)
    nq = L // B

    def body(q_ref, k_ref, v_ref, o_ref):
        qi = pl.program_id(1)
        heads = _head_masks(B)
        ti = lax.broadcasted_iota(jnp.int32, (B, B), 0)
        si = lax.broadcasted_iota(jnp.int32, (B, B), 1)
        tri_gt = jnp.where(ti > si, 1.0, 0.0).astype(bf16)
        diag = si < ti
        qv = q_ref[...]
        zero16 = jnp.zeros((), bf16)

        def tile(jb, mask, acc, carries):
            ks = pl.multiple_of(jb * B, B)
            kb = k_ref[pl.ds(ks, B), :]
            vb = v_ref[pl.ds(ks, B), :]
            out = []
            for in_head, carry in zip(heads, carries):
                kh = jnp.where(in_head, kb, zero16)
                vh = jnp.where(in_head, vb, zero16)
                z = lax.dot_general(qv, kh, NT, preferred_element_type=f32)
                w, _, lm = _sb_weights(z, mask, tri_gt, carry)
                acc = acc + lax.dot_general(w.astype(bf16), vh, NN, preferred_element_type=f32)
                out.append(carry + jnp.sum(lm, axis=1, keepdims=True))
            return acc, out

        zc = jnp.zeros((B, 1), f32)
        acc, (c0, c1) = tile(qi, diag, jnp.zeros((B, 128), f32), (zc, zc))

        def cond(st):
            return jnp.logical_and(st[0] <= qi, st[4] > TAIL_DEAD)

        def step(st):
            n, a, p0, p1, _ = st
            a, (p0, p1) = tile(qi - n, None, a, (p0, p1))
            return n + 1, a, p0, p1, jnp.maximum(jnp.max(p0), jnp.max(p1))

        st = lax.while_loop(cond, step, (jnp.int32(1), acc, c0, c1, jnp.maximum(jnp.max(c0), jnp.max(c1))))
        o_ref[...] = st[1]

    return pl.pallas_call(
        body, name="attn_fwd", grid=(4, nq),
        in_specs=[pl.BlockSpec((B, 128), lambda hp, i: (i, hp)),
                  pl.BlockSpec((L, 128), lambda hp, i: (0, 4 + hp)),
                  pl.BlockSpec((L, 128), lambda hp, i: (0, 8 + hp))],
        out_specs=pl.BlockSpec((B, 128), lambda hp, i: (i, hp)),
        out_shape=jax.ShapeDtypeStruct((L, ATT), f32),
        compiler_params=_cp(("parallel", "parallel")),
    )(qkv, qkv, qkv)


def _attn_bwd(qkv, o, do):
    L = qkv.shape[0]
    B = _attn_block(L)
    nq = L // B

    def body(q_ref, k_ref, v_ref, o_ref, do_ref, dq_ref, dk_ref, dv_ref):
        qi = pl.program_id(1)

        @pl.when(qi == 0)
        def _():
            dk_ref[...] = jnp.zeros_like(dk_ref)
            dv_ref[...] = jnp.zeros_like(dv_ref)

        heads = _head_masks(B)
        ti = lax.broadcasted_iota(jnp.int32, (B, B), 0)
        si = lax.broadcasted_iota(jnp.int32, (B, B), 1)
        tri_gt = jnp.where(ti > si, 1.0, 0.0).astype(bf16)
        tri_ge = jnp.where(ti >= si, 1.0, 0.0).astype(bf16)
        diag = si < ti
        zero16 = jnp.zeros((), bf16)
        qv = q_ref[...]
        do16 = do_ref[...].astype(bf16)
        dsum_lanes = do16.astype(f32) * o_ref[...]
        qhs = [jnp.where(m, qv, zero16) for m in heads]
        dohs = [jnp.where(m, do16, zero16) for m in heads]
        dsums = [jnp.sum(jnp.where(m, dsum_lanes, 0.0), axis=1, keepdims=True) for m in heads]

        def tile(jb, mask, dq_a, state):
            ks = pl.multiple_of(jb * B, B)
            kb = k_ref[pl.ds(ks, B), :]
            vb = v_ref[pl.ds(ks, B), :]
            dk_blk = jnp.zeros((B, 128), f32)
            dv_blk = jnp.zeros((B, 128), f32)
            out = []
            for in_head, qh, doh, dsum, (carry, suffix) in zip(heads, qhs, dohs, dsums, state):
                kh = jnp.where(in_head, kb, zero16)
                vh = jnp.where(in_head, vb, zero16)
                z = lax.dot_general(qv, kh, NT, preferred_element_type=f32)
                w, ls_pos, lm = _sb_weights(z, mask, tri_gt, carry)
                w16 = w.astype(bf16)
                dw = lax.dot_general(doh, vh, NT, preferred_element_type=f32)
                da = w16.astype(f32) * dw
                sig = jnp.exp(ls_pos)
                suf = _dot_exact2(da, tri_ge) + suffix
                dz = da * (1.0 - sig) - sig * (dsum - suf)
                if mask is not None:
                    dz = jnp.where(mask, dz, 0.0)
                dz16 = dz.astype(bf16)
                dq_a = dq_a + lax.dot_general(dz16, kh, NN, preferred_element_type=f32)
                dk_blk = dk_blk + lax.dot_general(dz16, qh, TN, preferred_element_type=f32)
                dv_blk = dv_blk + lax.dot_general(w16, doh, TN, preferred_element_type=f32)
                out.append((carry + jnp.sum(lm, axis=1, keepdims=True),
                            suffix + jnp.sum(da, axis=1, keepdims=True)))
            dk_ref[pl.ds(ks, B), :] += dk_blk
            dv_ref[pl.ds(ks, B), :] += dv_blk
            return dq_a, out

        def alive(state):
            return jnp.maximum(jnp.max(state[0][0]), jnp.max(state[1][0]))

        zc = jnp.zeros((B, 1), f32)
        dq_acc, state = tile(qi, diag, jnp.zeros((B, 128), f32), ((zc, zc), (zc, zc)))

        def cond(st):
            return jnp.logical_and(st[0] <= qi, st[6] > TAIL_DEAD)

        def step(st):
            n, a, c0, s0, c1, s1, _ = st
            a, new = tile(qi - n, None, a, ((c0, s0), (c1, s1)))
            return n + 1, a, new[0][0], new[0][1], new[1][0], new[1][1], alive(new)

        st = lax.while_loop(cond, step, (jnp.int32(1), dq_acc, state[0][0], state[0][1],
                                         state[1][0], state[1][1], alive(state)))
        dq_ref[...] = st[1] * (1.0 / math.sqrt(HD))

    blk = pl.BlockSpec((B, 128), lambda hp, i: (i, hp))
    dqkv = pl.pallas_call(
        body, name="attn_bwd", grid=(4, nq),
        in_specs=[blk,
                  pl.BlockSpec((L, 128), lambda hp, i: (0, 4 + hp)),
                  pl.BlockSpec((L, 128), lambda hp, i: (0, 8 + hp)),
                  blk, blk],
        out_specs=[blk, pl.BlockSpec((L, 128), lambda hp, i: (0, hp)),
                   pl.BlockSpec((L, 128), lambda hp, i: (0, hp))],
        out_shape=[jax.ShapeDtypeStruct((L, ATT), f32)] * 3,
        compiler_params=_cp(("parallel", "arbitrary")),
    )(qkv, qkv, qkv, o, do)
    return dqkv


def _s5_disc(ldt, ar, ai, br, bi):
    dt = jnp.exp(ldt)
    mag = jnp.exp(dt * ar)
    abar_re = mag * jnp.cos(dt * ai)
    abar_im = mag * jnp.sin(dt * ai)
    em_re = abar_re - 1.0
    em_im = abar_im
    den = ar * ar + ai * ai
    f_re = (em_re * ar + em_im * ai) / den
    f_im = (em_im * ar - em_re * ai) / den
    bb_re = f_re * br - f_im * bi
    bb_im = f_re * bi + f_im * br
    return abar_re, abar_im, bb_re, bb_im


def _s5_disc_fwd(ldt, ar, ai, br, bi):
    def body(ldt_ref, ar_ref, ai_ref, br_ref, bi_ref, o1, o2, o3, o4):
        outs = _s5_disc(ldt_ref[...], ar_ref[...], ai_ref[...], br_ref[...], bi_ref[...])
        for o_ref, v in zip((o1, o2, o3, o4), outs):
            o_ref[...] = v

    col = jax.ShapeDtypeStruct((NS, 1), f32)
    mat = jax.ShapeDtypeStruct((NS, NH16), f32)
    return pl.pallas_call(body, name="s5_disc_fwd", out_shape=[col, col, mat, mat],
                          compiler_params=_cp())(ldt, ar, ai, br, bi)


def _s5_disc_bwd(ldt, ar, ai, br, bi, d_are, d_aim, d_bbre, d_bbim):
    def body(ldt_ref, ar_ref, ai_ref, br_ref, bi_ref, c1, c2, c3, c4, o_ldt, o_ar, o_ai, o_br, o_bi):
        prim = (ldt_ref[...], ar_ref[...], ai_ref[...], br_ref[...], bi_ref[...])
        _, vjp = jax.vjp(_s5_disc, *prim)
        g_ldt, g_ar, g_ai, g_br, g_bi = vjp((c1[...], c2[...], c3[...], c4[...]))
        o_ar[...] = g_ar
        o_ai[...] = g_ai
        o_br[...] = g_br
        o_bi[...] = g_bi
        r = lax.broadcasted_iota(jnp.int32, (NG, NS), 0)
        c = lax.broadcasted_iota(jnp.int32, (NG, NS), 1)
        fold = jnp.where(c // NP == r, 1.0, 0.0).astype(bf16)
        hi = jnp.broadcast_to(g_ldt, (NS, 128))
        p1 = hi.astype(bf16)
        r1 = hi - p1.astype(f32)
        p2 = r1.astype(bf16)
        p3 = (r1 - p2.astype(f32)).astype(bf16)
        dd = lambda p: lax.dot_general(fold, p, NN, preferred_element_type=f32)
        o_ldt[...] = dd(p1) + dd(p2) + dd(p3)

    col = jax.ShapeDtypeStruct((NS, 1), f32)
    mat = jax.ShapeDtypeStruct((NS, NH16), f32)
    return pl.pallas_call(
        body, name="s5_disc_bwd",
        out_shape=[jax.ShapeDtypeStruct((NG, 128), f32), col, col, mat, mat],
        compiler_params=_cp())(ldt, ar, ai, br, bi, d_are, d_aim, d_bbre, d_bbim)


SCAN_LANES = 256


def _cmul(ar, ai, br, bi):
    return ar * br - ai * bi, ar * bi + ai * br


def _pow_tables(ar, ai):
    a2 = _cmul(ar, ai, ar, ai)
    a4 = _cmul(*a2, *a2)
    rows = [(ar, ai)]
    for _ in range(7):
        rows.append(_cmul(*rows[-1], ar, ai))
    return (ar, ai), a2, a4, rows


def _scan_rows(xr_ref, xi_ref, abar_re, abar_im, cr_ref, ci_ref, reverse, on_tile=None):
    T = xr_ref.shape[0]
    nt = T // 8
    row = lax.broadcasted_iota(jnp.int32, (8, SCAN_LANES), 0)
    for c in range(NS // SCAN_LANES):
        ls = slice(c * SCAN_LANES, (c + 1) * SCAN_LANES)
        ar = abar_re[:, ls]
        ai = abar_im[:, ls]
        if reverse:
            ai = -ai
        a1, a2, a4, prow = _pow_tables(ar, ai)
        if reverse:
            prow = prow[::-1]
        pr = jnp.concatenate([p[0] for p in prow], axis=0)
        pi = jnp.concatenate([p[1] for p in prow], axis=0)
        steps = [(1, a1), (2, a2), (4, a4)]

        def tile(n, carry):
            cr, ci = carry
            it = (nt - 1 - n) if reverse else n
            rs = pl.ds(pl.multiple_of(it * 8, 8), 8)
            xr = xr_ref[rs, ls]
            xi = xi_ref[rs, ls]
            for d, (dr, di) in steps:
                if reverse:
                    keep = row < 8 - d
                    sr = jnp.where(keep, pltpu.roll(xr, 8 - d, 0), 0.0)
                    si = jnp.where(keep, pltpu.roll(xi, 8 - d, 0), 0.0)
                else:
                    keep = row >= d
                    sr = jnp.where(keep, pltpu.roll(xr, d, 0), 0.0)
                    si = jnp.where(keep, pltpu.roll(xi, d, 0), 0.0)
                mr, mi = _cmul(dr, di, sr, si)
                xr = xr + mr
                xi = xi + mi
            mr, mi = _cmul(pr, pi, cr, ci)
            xr = xr + mr
            xi = xi + mi
            xr_ref[rs, ls] = xr
            xi_ref[rs, ls] = xi
            if on_tile is not None:
                nxt_r = jnp.where(row < 7, pltpu.roll(xr, 7, 0), cr)
                nxt_i = jnp.where(row < 7, pltpu.roll(xi, 7, 0), ci)
                on_tile(ls, rs, nxt_r, nxt_i)
            edge = slice(0, 1) if reverse else slice(7, 8)
            return xr[edge, :], xi[edge, :]

        cr, ci = lax.fori_loop(0, nt, tile, (cr_ref[:, ls], ci_ref[:, ls]))
        cr_ref[:, ls] = cr
        ci_ref[:, ls] = ci


def _s5_chunk(L):
    return 256 if L % 256 == 0 else L


def _s5_gate(s_re, s_im, u, ccr_ref, cci_ref, dsk_ref, gw_ref, gb_ref):
    y = _dot(s_re, ccr_ref[...]) - _dot(s_im, cci_ref[...]) + dsk_ref[...] * u
    yg = _gelu(y)
    gate = _sigmoid(_dot(yg, gw_ref[...]) + gb_ref[...])
    return y, yg, gate


def _s5_fwd(p, abar_re, abar_im, bbr, bbi, ccr, cci, dsk, gw, gb):
    L = p.shape[0]
    T = _s5_chunk(L)

    def body(u_ref, are_ref, aim_ref, bbr_ref, bbi_ref, ccr_ref, cci_ref, dsk_ref, gw_ref, gb_ref,
             o_ref, sr_ref, si_ref, cr_ref, ci_ref):
        @pl.when(pl.program_id(0) == 0)
        def _():
            cr_ref[...] = jnp.zeros_like(cr_ref)
            ci_ref[...] = jnp.zeros_like(ci_ref)

        u = u_ref[...]
        sr_ref[...] = _dot(u, bbr_ref[...])
        si_ref[...] = _dot(u, bbi_ref[...])
        _scan_rows(sr_ref, si_ref, are_ref[...], aim_ref[...], cr_ref, ci_ref, reverse=False)
        _, yg, gate = _s5_gate(sr_ref[...], si_ref[...], u, ccr_ref, cci_ref, dsk_ref, gw_ref, gb_ref)
        o_ref[...] = yg * gate

    full = lambda shape: pl.BlockSpec(shape, lambda i: (0, 0))
    return pl.pallas_call(
        body, name="s5_fwd", grid=(L // T,),
        in_specs=[pl.BlockSpec((T, SSMW), lambda i: (i, 3)), full((1, NS)), full((1, NS)),
                  full((SSMW, NS)), full((SSMW, NS)), full((NS, SSMW)), full((NS, SSMW)),
                  full((1, SSMW)), full((SSMW, SSMW)), full((1, SSMW))],
        out_specs=[pl.BlockSpec((T, SSMW), lambda i: (i, 0)), pl.BlockSpec((T, NS), lambda i: (i, 0)),
                   pl.BlockSpec((T, NS), lambda i: (i, 0))],
        out_shape=[jax.ShapeDtypeStruct((L, SSMW), f32), jax.ShapeDtypeStruct((L, NS), f32),
                   jax.ShapeDtypeStruct((L, NS), f32)],
        scratch_shapes=[pltpu.VMEM((1, NS), f32), pltpu.VMEM((1, NS), f32)],
        compiler_params=_cp(("arbitrary",)),
    )(p, abar_re, abar_im, bbr, bbi, ccr, cci, dsk, gw, gb)


def _s5_bwd(dout, p, s_re, s_im, abar_re, abar_im, bbr, bbi, ccr, cci, dsk, gw, gb):
    L = p.shape[0]
    T = _s5_chunk(L)
    nchunk = L // T

    def body(do_ref, u_ref, sr_ref, si_ref, are_ref, aim_ref, bbr_ref, bbi_ref, ccr_ref, cci_ref,
             dsk_ref, gw_ref, gb_ref,
             du_ref, dy_ref, yg_ref, dgl_ref, lr16_ref, li16_ref, dare_ref, daim_ref, dd_ref, dgb_ref,
             lr_ref, li_ref, cr_ref, ci_ref, accr_ref, acci_ref):
        i = pl.program_id(0)

        @pl.when(i == 0)
        def _():
            cr_ref[...] = jnp.zeros_like(cr_ref)
            ci_ref[...] = jnp.zeros_like(ci_ref)
            accr_ref[...] = jnp.zeros_like(accr_ref)
            acci_ref[...] = jnp.zeros_like(acci_ref)
            dd_ref[...] = jnp.zeros_like(dd_ref)
            dgb_ref[...] = jnp.zeros_like(dgb_ref)

        u = u_ref[...]
        dov = do_ref[...]
        y, yg, gate = _s5_gate(sr_ref[...], si_ref[...], u, ccr_ref, cci_ref, dsk_ref, gw_ref, gb_ref)
        dgl = dov * yg * gate * (1.0 - gate)
        dyg = dov * gate + _dot(dgl, gw_ref[...], NT)
        dy = dyg * _gelu_grad(y)
        dy_ref[...] = dy.astype(dy_ref.dtype)
        yg_ref[...] = yg.astype(yg_ref.dtype)
        dgl_ref[...] = dgl.astype(dgl_ref.dtype)
        dd_ref[...] += jnp.sum(dy * u, axis=0, keepdims=True)
        dgb_ref[...] += jnp.sum(dgl, axis=0, keepdims=True)
        lr_ref[...] = _dot(dy, ccr_ref[...], NT)
        li_ref[...] = -_dot(dy, cci_ref[...], NT)

        def on_tile(ls, rs, nxt_r, nxt_i):
            s_r = sr_ref[rs, ls]
            s_i = si_ref[rs, ls]
            accr_ref[:, ls] += nxt_r * s_r + nxt_i * s_i
            acci_ref[:, ls] += nxt_i * s_r - nxt_r * s_i

        _scan_rows(lr_ref, li_ref, are_ref[...], aim_ref[...], cr_ref, ci_ref, reverse=True,
                   on_tile=on_tile)
        lam_r = lr_ref[...]
        lam_i = li_ref[...]
        lr16_ref[...] = lam_r.astype(lr16_ref.dtype)
        li16_ref[...] = lam_i.astype(li16_ref.dtype)
        du_ref[...] = dy * dsk_ref[...] + _dot(lam_r, bbr_ref[...], NT) + _dot(lam_i, bbi_ref[...], NT)

        @pl.when(i == nchunk - 1)
        def _():
            dare_ref[...] = jnp.sum(accr_ref[...], axis=0, keepdims=True)
            daim_ref[...] = jnp.sum(acci_ref[...], axis=0, keepdims=True)

    rev = lambda i: (nchunk - 1 - i, 0)
    full = lambda shape: pl.BlockSpec(shape, lambda i: (0, 0))
    half = pl.BlockSpec((T, SSMW), rev)
    wide = pl.BlockSpec((T, NS), rev)
    return pl.pallas_call(
        body, name="s5_bwd", grid=(nchunk,),
        in_specs=[half, pl.BlockSpec((T, SSMW), lambda i: (nchunk - 1 - i, 3)), wide, wide,
                  full((1, NS)), full((1, NS)), full((SSMW, NS)), full((SSMW, NS)),
                  full((NS, SSMW)), full((NS, SSMW)), full((1, SSMW)), full((SSMW, SSMW)), full((1, SSMW))],
        out_specs=[half, half, half, half, wide, wide, full((1, NS)), full((1, NS)),
                   full((1, SSMW)), full((1, SSMW))],
        out_shape=[jax.ShapeDtypeStruct((L, SSMW), f32)] + [jax.ShapeDtypeStruct((L, SSMW), bf16)] * 3
                  + [jax.ShapeDtypeStruct((L, NS), bf16)] * 2 + [jax.ShapeDtypeStruct((1, NS), f32)] * 2
                  + [jax.ShapeDtypeStruct((1, SSMW), f32)] * 2,
        scratch_shapes=[pltpu.VMEM((T, NS), f32), pltpu.VMEM((T, NS), f32),
                        pltpu.VMEM((1, NS), f32), pltpu.VMEM((1, NS), f32),
                        pltpu.VMEM((8, NS), f32), pltpu.VMEM((8, NS), f32)],
        compiler_params=_cp(("arbitrary",)),
    )(dout, p, s_re, s_im, abar_re, abar_im, bbr, bbi, ccr, cci, dsk, gw, gb)


def _block_diag(x):
    g, r, c = x.shape
    eye = jnp.eye(g, dtype=x.dtype)
    return (x[:, :, None, :] * eye[:, None, :, None]).reshape(g * r, g * c)


def _diag_blocks(dense, r, c):
    g = dense.shape[0] // r
    return jnp.stack([dense[k * r:(k + 1) * r, k * c:(k + 1) * c] for k in range(g)])


def _mod_fwd(c16, ada_w, ada_b_loc):
    nloc = ada_w.shape[2]

    def body(c_ref, w_ref, b_ref, o_ref):
        cv = c_ref[...]
        act = cv * _sigmoid(cv)
        o_ref[...] = _dot(act, w_ref[...]) + b_ref[...]

    return pl.pallas_call(
        body, name="mod_fwd", grid=(DEPTH,),
        in_specs=[pl.BlockSpec((16, D), lambda l: (0, 0)), pl.BlockSpec((None, D, nloc), lambda l: (l, 0, 0)),
                  pl.BlockSpec((None, 1, nloc), lambda l: (l, 0, 0))],
        out_specs=pl.BlockSpec((None, 16, nloc), lambda l: (l, 0, 0)),
        out_shape=jax.ShapeDtypeStruct((DEPTH, 16, nloc), f32),
        compiler_params=_cp(("parallel",)),
    )(c16, ada_w, ada_b_loc)


def _ada_grad(c16, dmod16):
    nloc = dmod16.shape[2]

    def body(c_ref, d_ref, o_ref):
        cv = c_ref[...]
        act = cv * _sigmoid(cv)
        o_ref[...] = _dot(act, d_ref[...], TN)

    return pl.pallas_call(
        body, name="ada_grad", grid=(DEPTH,),
        in_specs=[pl.BlockSpec((16, D), lambda l: (0, 0)), pl.BlockSpec((None, 16, nloc), lambda l: (l, 0, 0))],
        out_specs=pl.BlockSpec((None, D, nloc), lambda l: (l, 0, 0)),
        out_shape=jax.ShapeDtypeStruct((DEPTH, D, nloc), f32),
        compiler_params=_cp(("parallel",)),
    )(c16, dmod16)


def _as2d(a):
    return a.reshape(-1, a.shape[-1])


def _ew_rows(rows):
    for cand in (512, 256, 128, 64, 32, 16, 8):
        if rows % cand == 0:
            return cand
    return rows


def _cast_bf16(w):
    w2 = _as2d(w)
    rows, cols = w2.shape
    bm = _ew_rows(rows)

    def body(x_ref, o_ref):
        o_ref[...] = x_ref[...].astype(bf16)

    spec = pl.BlockSpec((bm, cols), lambda i: (i, 0))
    out = pl.pallas_call(body, name="cast_bf16", grid=(rows // bm,), in_specs=[spec], out_specs=spec,
                         out_shape=jax.ShapeDtypeStruct((rows, cols), bf16),
                         compiler_params=_cp(("parallel",)))(w2)
    return out.reshape(w.shape)


def _add2(a, b, out_dtype=f32):
    a2, b2 = _as2d(a), _as2d(b)
    rows, cols = a2.shape
    bm = _ew_rows(rows)

    def body(a_ref, b_ref, o_ref):
        o_ref[...] = (a_ref[...] + b_ref[...]).astype(o_ref.dtype)

    spec = pl.BlockSpec((bm, cols), lambda i: (i, 0))
    out = pl.pallas_call(body, name="add2", grid=(rows // bm,), in_specs=[spec, spec], out_specs=spec,
                         out_shape=jax.ShapeDtypeStruct((rows, cols), out_dtype),
                         compiler_params=_cp(("parallel",)))(a2, b2)
    return out.reshape(a.shape)


def _adamw(parts, w, m, v):
    n = parts.shape[0]
    w2, m2, v2 = _as2d(w), _as2d(m), _as2d(v)
    rows, cols = w2.shape
    p3 = parts.reshape(n, rows, cols)
    bm = _ew_rows(rows)
    if bm * cols * 4 > (1 << 20) and bm % 16 == 0:
        bm //= 2

    def body(p_ref, w_ref, m_ref, v_ref, g_ref, d_ref, nm_ref, nv_ref):
        g = p_ref[0]
        for k in range(1, n):
            g = g + p_ref[k]
        mn = ADAM_B1 * m_ref[...] + (1.0 - ADAM_B1) * g
        vn = ADAM_B2 * v_ref[...] + (1.0 - ADAM_B2) * (g * g)
        m_hat = mn / (1.0 - ADAM_B1 ** ADAM_STEP)
        v_hat = vn / (1.0 - ADAM_B2 ** ADAM_STEP)
        g_ref[...] = g
        d_ref[...] = -ADAM_LR * (m_hat / (jnp.sqrt(v_hat) + ADAM_EPS) + ADAM_WD * w_ref[...])
        nm_ref[...] = mn
        nv_ref[...] = vn

    spec = pl.BlockSpec((bm, cols), lambda i: (i, 0))
    outs = pl.pallas_call(
        body, name="adamw", grid=(rows // bm,),
        in_specs=[pl.BlockSpec((n, bm, cols), lambda i: (0, i, 0)), spec, spec, spec],
        out_specs=[spec] * 4, out_shape=[jax.ShapeDtypeStruct((rows, cols), f32)] * 4,
        compiler_params=_cp(("parallel",)),
    )(p3, w2, m2, v2)
    return tuple(o.reshape(w.shape) for o in outs)


ANY = pl.BlockSpec(memory_space=pl.ANY)


def _my_pos():
    return lax.axis_index("x"), lax.axis_index("y"), lax.axis_index("c")


def _allgather8(name, x):
    def body(x_ref, out_ref, send_sems, recv_sems):
        mx, my, mc = _my_pos()
        me, sibling = (mx, my, mc), (mx, my, 1 - mc)
        chips = [(1 - mx, my), (mx, 1 - my), (1 - mx, 1 - my)]

        def slot(px, py, pc):
            return out_ref.at[4 * px + 2 * py + pc]

        def copy(k, block, to, src=None):
            return pltpu.make_async_remote_copy(
                src_ref=slot(*block) if src is None else src, dst_ref=slot(*block),
                send_sem=send_sems.at[k], recv_sem=recv_sems.at[k], device_id=to, device_id_type=MESH)

        first = [copy(0, me, sibling, src=x_ref)]
        first += [copy(1 + j, me, (*chip, mc), src=x_ref) for j, chip in enumerate(chips)]
        for cp in first:
            cp.start()
        passed = [copy(4 + j, (*chip, mc), sibling) for j, chip in enumerate(chips)]
        for j, chip in enumerate(chips):
            copy(1 + j, (*chip, mc), me).wait_recv()
            passed[j].start()
        copy(0, sibling, me).wait_recv()
        for j, chip in enumerate(chips):
            copy(4 + j, (*chip, 1 - mc), me).wait_recv()
        for cp in first + passed:
            cp.wait_send()

    out = pl.pallas_call(
        body, name=name, in_specs=[ANY], out_specs=ANY,
        out_shape=jax.ShapeDtypeStruct((NDEV,) + x.shape, x.dtype),
        scratch_shapes=[pltpu.SemaphoreType.DMA((7,)), pltpu.SemaphoreType.DMA((7,))],
    )(x)
    dev = 4 * lax.axis_index("x") + 2 * lax.axis_index("y") + lax.axis_index("c")
    return lax.dynamic_update_index_in_dim(out, x, dev, 0)


HALF = DEPTH // 2


def _chip_gather(arrs):
    n = len(arrs)

    def body(*refs):
        ins, outs = refs[:n], refs[n:2 * n]
        send_sems, recv_sems = refs[2 * n:]
        mx, my, mc = _my_pos()
        chips = [(1 - mx, my), (mx, 1 - my), (1 - mx, 1 - my)]
        mine = 2 * mx + my
        my_half = pl.ds(HALF * mc, HALF)
        sib_half = pl.ds(HALF * (1 - mc), HALF)

        def ici(a, j, src_chip, half):
            px, py = chips[j]
            return pltpu.make_async_remote_copy(
                src_ref=ins[a].at[half], dst_ref=outs[a].at[src_chip, half],
                send_sem=send_sems.at[3 * a + j], recv_sem=recv_sems.at[3 * a + j],
                device_id=(px, py, mc), device_id_type=MESH)

        def d2d(a, j, half):
            px, py = chips[j]
            blk = outs[a].at[2 * px + py, half]
            return pltpu.make_async_remote_copy(
                src_ref=blk, dst_ref=blk,
                send_sem=send_sems.at[3 * n + 3 * a + j], recv_sem=recv_sems.at[3 * n + 3 * a + j],
                device_id=(mx, my, 1 - mc), device_id_type=MESH)

        sends = [ici(a, j, mine, my_half) for a in range(n) for j in range(3)]
        for cp in sends:
            cp.start()
        passed = []
        for a in range(n):
            for j, (px, py) in enumerate(chips):
                ici(a, j, 2 * px + py, my_half).wait_recv()
                fwd = d2d(a, j, my_half)
                fwd.start()
                passed.append(fwd)
        for a in range(n):
            for j in range(3):
                d2d(a, j, sib_half).wait_recv()
        for cp in sends + passed:
            cp.wait_send()

    outs = pl.pallas_call(
        body, name="chip_gather", in_specs=[ANY] * n, out_specs=[ANY] * n,
        out_shape=[jax.ShapeDtypeStruct((NCHIP,) + a.shape, a.dtype) for a in arrs],
        scratch_shapes=[pltpu.SemaphoreType.DMA((6 * n,)), pltpu.SemaphoreType.DMA((6 * n,))],
    )(*arrs)
    mine = 2 * lax.axis_index("x") + lax.axis_index("y")
    return [lax.dynamic_update_index_in_dim(o, a, mine, 0) for o, a in zip(outs, arrs)]


def _sibling_half_swap(arrs):
    n = len(arrs)

    def body(*refs):
        ins, outs = refs[:n], refs[n:2 * n]
        send_sems, recv_sems = refs[2 * n:]
        mx, my, mc = _my_pos()
        sib_half = pl.ds(HALF * (1 - mc), HALF)
        cps = [pltpu.make_async_remote_copy(
            src_ref=ins[a].at[j, sib_half], dst_ref=outs[a].at[j],
            send_sem=send_sems.at[NCHIP * a + j], recv_sem=recv_sems.at[NCHIP * a + j],
            device_id=(mx, my, 1 - mc), device_id_type=MESH) for a in range(n) for j in range(NCHIP)]
        for cp in cps:
            cp.start()
        for cp in cps:
            cp.wait()

    return pl.pallas_call(
        body, name="sibling_half_swap", in_specs=[ANY] * n, out_specs=[ANY] * n,
        out_shape=[jax.ShapeDtypeStruct((NCHIP, HALF) + a.shape[2:], a.dtype) for a in arrs],
        scratch_shapes=[pltpu.SemaphoreType.DMA((NCHIP * n,)), pltpu.SemaphoreType.DMA((NCHIP * n,))],
    )(*arrs)


def _sibling_merge_halves(arrs):
    n = len(arrs)

    def body(*refs):
        ins, outs = refs[:n], refs[n:2 * n]
        send_sems, recv_sems = refs[2 * n:]
        mx, my, mc = _my_pos()
        my_half = pl.ds(HALF * mc, HALF)
        cps = [pltpu.make_async_remote_copy(
            src_ref=ins[a], dst_ref=outs[a].at[my_half],
            send_sem=send_sems.at[a], recv_sem=recv_sems.at[a],
            device_id=(mx, my, 1 - mc), device_id_type=MESH) for a in range(n)]
        for cp in cps:
            cp.start()
        for cp in cps:
            cp.wait()

    outs = pl.pallas_call(
        body, name="sibling_merge_halves", in_specs=[ANY] * n, out_specs=[ANY] * n,
        out_shape=[jax.ShapeDtypeStruct((DEPTH,) + a.shape[1:], a.dtype) for a in arrs],
        scratch_shapes=[pltpu.SemaphoreType.DMA((n,)), pltpu.SemaphoreType.DMA((n,))],
    )(*arrs)
    start = HALF * lax.axis_index("c")
    return [lax.dynamic_update_slice_in_dim(o, a, start, 0) for o, a in zip(outs, arrs)]


def _sum_parts(parts):
    n = parts.shape[0]
    p3 = parts.reshape(n, -1, parts.shape[-1])
    rows, cols = p3.shape[1:]
    bm = _ew_rows(rows)
    if bm * cols * 4 > (1 << 20) and bm % 16 == 0:
        bm //= 2

    def body(p_ref, o_ref):
        g = p_ref[0].astype(f32)
        for k in range(1, n):
            g = g + p_ref[k].astype(f32)
        o_ref[...] = g

    out = pl.pallas_call(
        body, name="sum_parts", grid=(rows // bm,),
        in_specs=[pl.BlockSpec((n, bm, cols), lambda i: (0, i, 0))],
        out_specs=pl.BlockSpec((bm, cols), lambda i: (i, 0)),
        out_shape=jax.ShapeDtypeStruct((rows, cols), f32),
        compiler_params=_cp(("parallel",)),
    )(p3)
    return out.reshape(parts.shape[1:])


def _chip_scatter(arrs):
    n = len(arrs)

    def body(*refs):
        ins, outs = refs[:n], refs[n:2 * n]
        send_sems, recv_sems = refs[2 * n:]
        mx, my, mc = _my_pos()
        chips = [(1 - mx, my), (mx, 1 - my), (1 - mx, 1 - my)]
        mine = 2 * mx + my
        sends = []
        for a in range(n):
            for j, (px, py) in enumerate(chips):
                sends.append(pltpu.make_async_remote_copy(
                    src_ref=ins[a].at[2 * px + py], dst_ref=outs[a].at[mine],
                    send_sem=send_sems.at[3 * a + j], recv_sem=recv_sems.at[3 * a + j],
                    device_id=(px, py, mc), device_id_type=MESH))
        for cp in sends:
            cp.start()
        for a in range(n):
            for j, (px, py) in enumerate(chips):
                pltpu.make_async_remote_copy(
                    src_ref=ins[a].at[mine], dst_ref=outs[a].at[2 * px + py],
                    send_sem=send_sems.at[3 * a + j], recv_sem=recv_sems.at[3 * a + j],
                    device_id=(px, py, mc), device_id_type=MESH).wait_recv()
        for cp in sends:
            cp.wait_send()

    outs = pl.pallas_call(
        body, name="chip_scatter", in_specs=[ANY] * n, out_specs=[ANY] * n,
        out_shape=[jax.ShapeDtypeStruct(a.shape, a.dtype) for a in arrs],
        scratch_shapes=[pltpu.SemaphoreType.DMA((3 * n,)), pltpu.SemaphoreType.DMA((3 * n,))],
    )(*arrs)
    mine = 2 * lax.axis_index("x") + lax.axis_index("y")
    return [lax.dynamic_update_index_in_dim(o, lax.dynamic_index_in_dim(a, mine, 0, keepdims=False), mine, 0)
            for o, a in zip(outs, arrs)]


def _layer_fwd(x, res, res_gate, wl, mod):
    L = x.shape[0]
    bm = _row_block(L, 512)
    nb = L // bm
    sh1, sc1, g1, sh2, sc2, g2 = [mod[k * D:(k + 1) * D][None, :] for k in range(NMOD)]
    sv = {}
    xin, h = _rmsmod_fwd(x, res, res_gate, wl["norm1_g"], sh1, sc1)
    if xin is None:
        xin = x
    p = _mm("mm_in", h, wl["w_in"], dims=NN, grid=(nb, 4, 1),
            a_block=(bm, D), a_map=lambda i, j, k: (i, 0),
            b_block=(None, D, 512), b_map=lambda i, j, k: (j, 0, 0),
            out_shape=(L, INW), o_block=(bm, 512), o_map=lambda i, j, k: (i, j))
    qkv = _qknorm_fwd(p, wl["gqk"])
    oa = _attn_fwd(qkv)
    os_, s_re, s_im = _s5_fwd(p, wl["abar_re"], wl["abar_im"], wl["bbr"], wl["bbi"], wl["ccr"], wl["cci"],
                              wl["dsk"], wl["gw"], wl["gb"])
    o = _outnorm_fwd(oa, os_, wl["attn_out_g"], wl["ssm_out_g"])
    proj = _mm("mm_out", o, wl["w_out"], dims=NN, grid=(nb, 2, 1),
               a_block=(bm, D), a_map=lambda i, j, k: (i, 0),
               b_block=(D, 512), b_map=lambda i, j, k: (0, j),
               out_shape=(L, D), o_block=(bm, 512), o_map=lambda i, j, k: (i, j))
    x1, h2 = _rmsmod_fwd(xin, proj, g1, wl["norm2_g"], sh2, sc2)
    up_pre = _mm("mm_up", h2, wl["w_up"], dims=NN, grid=(nb, 4, 1),
                 a_block=(bm, D), a_map=lambda i, j, k: (i, 0),
                 b_block=(None, D, 1408), b_map=lambda i, j, k: (j, 0, 0),
                 out_shape=(L, DUP), o_block=(bm, 1408), o_map=lambda i, j, k: (i, j))
    a = _convglu_fwd(up_pre, wl["conv_w"], wl["conv_b"])
    down = _mm("mm_down", a, wl["w_down"], dims=NN, grid=(nb, 2, 1),
               a_block=(bm, DFF), a_map=lambda i, j, k: (i, 0),
               b_block=(DFF, 512), b_map=lambda i, j, k: (0, j),
               out_shape=(L, D), o_block=(bm, 512), o_map=lambda i, j, k: (i, j))
    sv.update(xin=xin, h=h, p=p, qkv=qkv, oa=oa, os=os_, s_re=s_re, s_im=s_im, o=o, proj=proj,
              x1=x1, h2=h2, up_pre=up_pre, a=a, down=down,
              sc1=sc1, g1=g1, sc2=sc2, g2=g2)
    return x1, down, g2, sv


def _layer_bwd(dx_out, wl, sv):
    L = dx_out.shape[0]
    bm = _row_block(L, 512)
    nb = L // bm
    bk = _row_block(L, 512)
    nk = L // bk
    g = {}
    dd, dg2 = _gate_bwd(dx_out, sv["down"], sv["g2"])
    da = _mm("mm_down_dx", dd, wl["w_down"], dims=NT, grid=(nb, 2, 1),
             a_block=(bm, D), a_map=lambda i, j, k: (i, 0),
             b_block=(1408, D), b_map=lambda i, j, k: (j, 0),
             out_shape=(L, DFF), o_block=(bm, 1408), o_map=lambda i, j, k: (i, j))
    g["w_down"] = _mm("mm_down_dw", sv["a"], dd, dims=TN, grid=(2, 2, nk),
                      a_block=(bk, 1408), a_map=lambda i, j, k: (k, i),
                      b_block=(bk, 512), b_map=lambda i, j, k: (k, j),
                      out_shape=(DFF, D), o_block=(1408, 512), o_map=lambda i, j, k: (i, j))
    dup, g["conv_w"], g["conv_b"] = _convglu_bwd(da, sv["up_pre"], wl["conv_w"], wl["conv_b"])
    dh2 = _mm("mm_up_dx", dup, wl["w_up"], dims=NT, grid=(nb, 1, 4),
              a_block=(bm, 1408), a_map=lambda i, j, k: (i, k),
              b_block=(None, D, 1408), b_map=lambda i, j, k: (k, 0, 0),
              out_shape=(L, D), o_block=(bm, D), o_map=lambda i, j, k: (i, 0))
    g["w_up"] = _mm("mm_up_dw", sv["h2"], dup, dims=TN, grid=(2, 4, nk),
                    a_block=(bk, 512), a_map=lambda i, j, k: (k, i),
                    b_block=(bk, 1408), b_map=lambda i, j, k: (k, j),
                    out_shape=(NCHIP, D, 1408), o_block=(None, 512, 1408), o_map=lambda i, j, k: (j, i, 0))
    dx1, dsh2, dsc2, g["norm2_g"] = _rmsmod_bwd(dh2, sv["x1"], wl["norm2_g"], sv["sc2"], dx_out)
    dproj, dg1 = _gate_bwd(dx1, sv["proj"], sv["g1"])
    do = _mm("mm_out_dx", dproj, wl["w_out"], dims=NT, grid=(nb, 2, 1),
             a_block=(bm, D), a_map=lambda i, j, k: (i, 0),
             b_block=(512, D), b_map=lambda i, j, k: (j, 0),
             out_shape=(L, D), o_block=(bm, 512), o_map=lambda i, j, k: (i, j))
    g["w_out"] = _mm("mm_out_dw", sv["o"], dproj, dims=TN, grid=(2, 2, nk),
                     a_block=(bk, 512), a_map=lambda i, j, k: (k, i),
                     b_block=(bk, 512), b_map=lambda i, j, k: (k, j),
                     out_shape=(D, D), o_block=(512, 512), o_map=lambda i, j, k: (i, j))
    doa, dos, g["attn_out_g"], g["ssm_out_g"] = _outnorm_bwd(do, sv["oa"], sv["os"],
                                                             wl["attn_out_g"], wl["ssm_out_g"])
    dqn, dkn, dv = _attn_bwd(sv["qkv"], sv["oa"], doa)
    dqk, dgqk = _qknorm_bwd(jnp.concatenate([dqn, dkn], axis=1), sv["p"], wl["gqk"])
    g["q_norm_g"] = dgqk[0, :HD]
    g["k_norm_g"] = dgqk[0, HD:2 * HD]
    (du, dy16, yg16, dgl16, lam_re, lam_im, d_abar_re, d_abar_im, g["ssm_d"], g["glu_b"]) = _s5_bwd(
        dos, sv["p"], sv["s_re"], sv["s_im"], wl["abar_re"], wl["abar_im"], wl["bbr"], wl["bbi"],
        wl["ccr"], wl["cci"], wl["dsk"], wl["gw"], wl["gb"])

    def tn_dense(name, a, a_col, a_cols, b, b_cols):
        return _mm(name, a, b, dims=TN, grid=(a_cols // 512, b_cols // 512, nk),
                   a_block=(bk, 512), a_map=lambda i, j, k: (k, a_col + i),
                   b_block=(bk, 512), b_map=lambda i, j, k: (k, j),
                   out_shape=(a_cols, b_cols), o_block=(512, 512), o_map=lambda i, j, k: (i, j))

    d_ccr = tn_dense("s5_dcc", sv["s_re"], 0, NS, dy16, SSMW)
    d_cci = tn_dense("s5_dcc", sv["s_im"], 0, NS, dy16, SSMW)
    d_bbr = tn_dense("s5_dbb", sv["p"], 3, SSMW, lam_re, NS)
    d_bbi = tn_dense("s5_dbb", sv["p"], 3, SSMW, lam_im, NS)
    d_gw = tn_dense("s5_dgw", yg16, 0, SSMW, dgl16, SSMW)
    g["ssm_c_re"] = _diag_blocks(d_ccr, NP, NH16).transpose(0, 2, 1)
    g["ssm_c_im"] = -_diag_blocks(d_cci, NP, NH16).transpose(0, 2, 1)
    g["glu_w"] = _diag_blocks(d_gw, NH16, NH16)
    d_bbre = _diag_blocks(d_bbr, NH16, NP).transpose(0, 2, 1).reshape(NS, NH16)
    d_bbim = _diag_blocks(d_bbi, NH16, NP).transpose(0, 2, 1).reshape(NS, NH16)
    d_ldt, d_ar, d_ai, d_br, d_bi = _s5_disc_bwd(
        wl["ldt_col"], wl["ar_col"], wl["ai_col"], wl["br_mat"], wl["bi_mat"],
        d_abar_re.reshape(NS, 1), d_abar_im.reshape(NS, 1), d_bbre, d_bbim)
    g["ssm_log_dt"] = d_ldt[:, 0]
    g["ssm_a_re"] = d_ar.reshape(NG, NP)
    g["ssm_a_im"] = d_ai.reshape(NG, NP)
    g["ssm_b_re"] = d_br.reshape(NG, NP, NH16)
    g["ssm_b_im"] = d_bi.reshape(NG, NP, NH16)
    dp = jnp.concatenate([dqk, dv.astype(bf16), du.astype(bf16)], axis=1)
    dh = _mm("mm_in_dx", dp, wl["w_in"], dims=NT, grid=(nb, 1, 4),
             a_block=(bm, 512), a_map=lambda i, j, k: (i, k),
             b_block=(None, D, 512), b_map=lambda i, j, k: (k, 0, 0),
             out_shape=(L, D), o_block=(bm, D), o_map=lambda i, j, k: (i, 0))
    g["w_in"] = _mm("mm_in_dw", sv["h"], dp, dims=TN, grid=(2, 4, nk),
                    a_block=(bk, 512), a_map=lambda i, j, k: (k, i),
                    b_block=(bk, 512), b_map=lambda i, j, k: (k, j),
                    out_shape=(NCHIP, D, 512), o_block=(None, 512, 512), o_map=lambda i, j, k: (j, i, 0))
    dx_in, dsh1, dsc1, g["norm1_g"] = _rmsmod_bwd(dh, sv["xin"], wl["norm1_g"], sv["sc1"], dx1)
    g["dmod"] = jnp.concatenate([dsh1, dsc1, dg1, dsh2, dsc2, dg2], axis=1)[0]
    return dx_in, g


def _prep_layer(l, big, small):
    wl = {}
    wl["w_in"] = big["w_in"][:, l]
    wl["w_up"] = big["w_up"][:, l]
    wl["w_out"] = big["w_out"][:, l].reshape(D, D)
    wl["w_down"] = big["w_down"][:, l].reshape(DFF, D)
    wl["conv_w"] = big["conv_w"][:, l].transpose(1, 0, 2).reshape(3, DUP)
    wl["conv_b"] = small["ffn_conv_b"][l][None, :]
    wl["norm1_g"] = small["norm1_g"][l][None, :]
    wl["norm2_g"] = small["norm2_g"][l][None, :]
    wl["attn_out_g"] = small["attn_out_g"][l][None, :]
    wl["ssm_out_g"] = small["ssm_out_g"][l][None, :]
    wl["gqk"] = jnp.concatenate([jnp.tile(small["q_norm_g"][l], 8), jnp.tile(small["k_norm_g"][l], 8)])[None, :]
    wl["ldt_col"] = jnp.repeat(small["ssm_log_dt"][l], NP)[:, None]
    wl["ar_col"] = small["ssm_a_re"][l].reshape(NS, 1)
    wl["ai_col"] = small["ssm_a_im"][l].reshape(NS, 1)
    wl["br_mat"] = small["ssm_b_re"][l].reshape(NS, NH16)
    wl["bi_mat"] = small["ssm_b_im"][l].reshape(NS, NH16)
    abar_re, abar_im, bb_re, bb_im = _s5_disc_fwd(wl["ldt_col"], wl["ar_col"], wl["ai_col"],
                                                  wl["br_mat"], wl["bi_mat"])
    wl["abar_re"] = abar_re.reshape(1, NS)
    wl["abar_im"] = abar_im.reshape(1, NS)
    wl["bbr"] = _block_diag(bb_re.reshape(NG, NP, NH16).transpose(0, 2, 1)).astype(bf16)
    wl["bbi"] = _block_diag(bb_im.reshape(NG, NP, NH16).transpose(0, 2, 1)).astype(bf16)
    wl["ccr"] = _block_diag(small["ssm_c_re"][l].transpose(0, 2, 1)).astype(bf16)
    wl["cci"] = _block_diag(small["ssm_c_im"][l].transpose(0, 2, 1)).astype(bf16)
    wl["gw"] = _block_diag(small["glu_w"][l]).astype(bf16)
    wl["dsk"] = small["ssm_d"][l].reshape(1, SSMW)
    wl["gb"] = small["glu_b"][l].reshape(1, SSMW)
    return wl


def _local_step(x, target, mods, big, small):
    wls = [_prep_layer(l, big, small) for l in range(DEPTH)]
    saved = []
    cur, res, res_gate = x, None, None
    for l in range(DEPTH):
        cur, res, res_gate, sv = _layer_fwd(cur, res, res_gate, wls[l], mods[l])
        saved.append(sv)
    loss, dx = _loss_fwd_bwd(cur, res, res_gate, target)
    grads = [None] * DEPTH
    for l in reversed(range(DEPTH)):
        dx, grads[l] = _layer_bwd(dx, wls[l], saved[l])
    return loss, dx, grads


BIG = ("w_in", "w_out", "ffn_w_up", "ffn_w_down")
SMALL = ("ada_b", "norm1_g", "q_norm_g", "k_norm_g", "ssm_a_re", "ssm_a_im", "ssm_log_dt",
         "ssm_b_re", "ssm_b_im", "ssm_c_re", "ssm_c_im", "ssm_d", "glu_w", "glu_b",
         "attn_out_g", "ssm_out_g", "norm2_g", "ffn_conv_b")
NAMES = ("ada_w", "ada_b", "norm1_g", "w_in", "q_norm_g", "k_norm_g", "ssm_a_re", "ssm_a_im",
         "ssm_log_dt", "ssm_b_re", "ssm_b_im", "ssm_c_re", "ssm_c_im", "ssm_d", "glu_w", "glu_b",
         "attn_out_g", "ssm_out_g", "w_out", "norm2_g", "ffn_w_up", "ffn_conv_w", "ffn_conv_b",
         "ffn_w_down")
PACK_COLS = 1024


def _pack(arrs):
    flat = jnp.concatenate([a.reshape(-1) for a in arrs])
    rows = -(-flat.shape[0] // PACK_COLS)
    rows = -(-rows // 8) * 8
    flat = jnp.pad(flat, (0, rows * PACK_COLS - flat.shape[0]))
    return flat.reshape(rows, PACK_COLS)


def _unpack(packed, shapes):
    flat = packed.reshape(-1)
    out, off = [], 0
    for s in shapes:
        n = math.prod(s)
        out.append(flat[off:off + n].reshape(s))
        off += n
    return out


def kernel(x, c, ada_w, ada_b, norm1_g, w_in, q_norm_g, k_norm_g, ssm_a_re, ssm_a_im, ssm_log_dt, ssm_b_re, ssm_b_im, ssm_c_re, ssm_c_im, ssm_d, glu_w, glu_b, attn_out_g, ssm_out_g, w_out, norm2_g, ffn_w_up, ffn_conv_w, ffn_conv_b, ffn_w_down, loss_target, m_ada_w, m_ada_b, m_norm1_g, m_w_in, m_q_norm_g, m_k_norm_g, m_ssm_a_re, m_ssm_a_im, m_ssm_log_dt, m_ssm_b_re, m_ssm_b_im, m_ssm_c_re, m_ssm_c_im, m_ssm_d, m_glu_w, m_glu_b, m_attn_out_g, m_ssm_out_g, m_w_out, m_norm2_g, m_ffn_w_up, m_ffn_conv_w, m_ffn_conv_b, m_ffn_w_down, v_ada_w, v_ada_b, v_norm1_g, v_w_in, v_q_norm_g, v_k_norm_g, v_ssm_a_re, v_ssm_a_im, v_ssm_log_dt, v_ssm_b_re, v_ssm_b_im, v_ssm_c_re, v_ssm_c_im, v_ssm_d, v_glu_w, v_glu_b, v_attn_out_g, v_ssm_out_g, v_w_out, v_norm2_g, v_ffn_w_up, v_ffn_conv_w, v_ffn_conv_b, v_ffn_w_down):
    env = dict(locals())
    w = {n: env[n] for n in NAMES}
    m = {n: env["m_" + n] for n in NAMES}
    v = {n: env["v_" + n] for n in NAMES}
    mx, my, mc = _my_pos()
    chip = 2 * mx + my
    dev = 4 * mx + 2 * my + mc
    xl = x[0]
    tl = loss_target[0]

    c_all = _allgather8("gather_c", jnp.pad(c, ((0, 7), (0, 0))))[:, 0, :]
    c16 = jnp.pad(c_all, ((0, 8), (0, 0)))
    nloc = NMOD * D // NCHIP
    ada_b_loc = lax.dynamic_slice(ada_b, (0, chip * nloc), (DEPTH, nloc))[:, None, :]
    mod_loc = _mod_fwd(c16, ada_w, ada_b_loc)
    mod_g = _allgather8("gather_mod", mod_loc.reshape(DEPTH * 16, nloc))
    mod_all = mod_g[0::2].reshape(NCHIP, DEPTH, 16, nloc).transpose(1, 2, 0, 3).reshape(DEPTH, 16, NMOD * D)
    mods = lax.dynamic_index_in_dim(mod_all, dev, axis=1, keepdims=False)

    gathered = _chip_gather([_cast_bf16(w["w_in"]), _cast_bf16(w["w_out"]), _cast_bf16(w["ffn_w_up"]),
                             _cast_bf16(w["ffn_w_down"]), w["ffn_conv_w"]])
    big = dict(zip(("w_in", "w_out", "w_up", "w_down", "conv_w"), gathered))
    small = {n: w[n] for n in SMALL}

    loss_loc, grad_x, grads = _local_step(xl, tl, mods, big, small)
    loss = lax.psum(loss_loc, ("x", "y", "c"))

    def stacked(name):
        return jnp.stack([grads[l][name] for l in range(DEPTH)], axis=1)

    loc = [stacked("w_in"),
           jnp.stack([grads[l]["w_out"].reshape(NCHIP, D // NCHIP, D) for l in range(DEPTH)], axis=1),
           stacked("w_up"),
           jnp.stack([grads[l]["w_down"].reshape(NCHIP, DFF // NCHIP, D) for l in range(DEPTH)], axis=1)]
    recv = _sibling_half_swap(loc)
    mine_half = [lax.dynamic_slice_in_dim(a, HALF * mc, HALF, axis=1) for a in loc]
    chip_sum = [_add2(a, b, bf16) for a, b in zip(mine_half, recv)]
    parts = _chip_scatter(chip_sum)
    g_full = _sibling_merge_halves([_sum_parts(p) for p in parts])
    outs = {}
    for name, g_big in zip(BIG, g_full):
        outs[name] = _adamw(g_big[None], w[name], m[name], v[name])

    small_names = SMALL[1:] + ("ffn_conv_w_full",)
    small_grads = []
    for n in SMALL[1:]:
        key = {"ffn_conv_b": "conv_b"}.get(n, n)
        small_grads.append(jnp.stack([grads[l][key].reshape(w[n].shape[1:]) for l in range(DEPTH)]))
    small_grads.append(jnp.stack([grads[l]["conv_w"] for l in range(DEPTH)]))
    dmod = jnp.stack([grads[l]["dmod"] for l in range(DEPTH)])
    packed = _pack([dmod] + small_grads)
    allp = _allgather8("gather_small", packed)
    shapes = [dmod.shape] + [a.shape for a in small_grads]
    dmod_all = allp[:, :DEPTH * NMOD, :].reshape(NDEV, DEPTH, NMOD * D)
    dmod_loc = lax.dynamic_slice(dmod_all, (0, 0, chip * nloc), (NDEV, DEPTH, nloc)).transpose(1, 0, 2)
    g_ada = _ada_grad(c16, jnp.pad(dmod_loc, ((0, 0), (0, 8), (0, 0))))
    outs["ada_w"] = _adamw(g_ada[None], w["ada_w"], m["ada_w"], v["ada_w"])
    conv_w_shape = (DEPTH, 3, DUP)
    w_small = _pack([w[n] for n in SMALL] + [jnp.zeros(conv_w_shape, f32)])
    m_small = _pack([m[n] for n in SMALL] + [jnp.zeros(conv_w_shape, f32)])
    v_small = _pack([v[n] for n in SMALL] + [jnp.ones(conv_w_shape, f32)])
    res_small = _adamw(allp, w_small, m_small, v_small)
    unpacked = [_unpack(r, shapes) for r in res_small]
    for i, n in enumerate(SMALL):
        outs[n] = tuple(unpacked[k][i] for k in range(4))
    g_conv_full = unpacked[0][len(SMALL)]
    ncw = DUP // NCHIP
    g_conv = lax.dynamic_slice(g_conv_full, (0, 0, chip * ncw), (DEPTH, 3, ncw))
    outs["ffn_conv_w"] = _adamw(g_conv[None], w["ffn_conv_w"], m["ffn_conv_w"], v["ffn_conv_w"])

    result = [loss, grad_x[None]]
    for k in range(4):
        result += [outs[n][k] for n in NAMES]
    return tuple(result)
```

```python
import functools
import math

import jax
import jax.numpy as jnp
from jax import lax
from jax.experimental import pallas as pl
from jax.experimental.pallas import tpu as pltpu

f32 = jnp.float32
bf16 = jnp.bfloat16
_MXU = jnp.bfloat16

D = 1024
ATT = 512
SSMW = 512
HD = 64
NG = 32
NP = 64
NH16 = 16
NS = NG * NP
INW = 2048
DFF = 2816
DUP = 2 * DFF
NMOD = 6
DEPTH = 4
EPS = 1e-6
NCHIP = 4
NDEV = 8

ADAM_LR = 0.001
ADAM_B1 = 0.9
ADAM_B2 = 0.999
ADAM_EPS = 1e-08
ADAM_WD = 0.01
ADAM_STEP = 10

VMEM_LIMIT = 56 * 1024 * 1024
MESH = pl.DeviceIdType.MESH

NN = (((1,), (0,)), ((), ()))
NT = (((1,), (1,)), ((), ()))
TN = (((0,), (0,)), ((), ()))


def _cp(sem=None):
    if sem is None:
        return pltpu.CompilerParams(vmem_limit_bytes=VMEM_LIMIT)
    return pltpu.CompilerParams(dimension_semantics=sem, vmem_limit_bytes=VMEM_LIMIT)


def _dot(a, b, dims=NN):
    return lax.dot_general(a.astype(_MXU), b.astype(_MXU), dims, preferred_element_type=f32)


def _dot_exact(x, m):
    hi = x.astype(bf16)
    r1 = x - hi.astype(f32)
    mid = r1.astype(bf16)
    lo = (r1 - mid.astype(f32)).astype(bf16)
    d = lambda p: lax.dot_general(p, m, NN, preferred_element_type=f32)
    return d(hi) + d(mid) + d(lo)


def _gelu(x):
    c = math.sqrt(2.0 / math.pi)
    return 0.5 * x * (1.0 + jnp.tanh(c * (x + 0.044715 * (x * x * x))))


def _gelu_grad(x):
    c = math.sqrt(2.0 / math.pi)
    t = jnp.tanh(c * (x + 0.044715 * (x * x * x)))
    return 0.5 * (1.0 + t) + 0.5 * x * (1.0 - t * t) * c * (1.0 + 3.0 * 0.044715 * (x * x))


def _sigmoid(x):
    return 1.0 / (1.0 + jnp.exp(-x))


def _mm(name, a, b, *, dims, grid, a_block, a_map, b_block, b_map, out_shape, o_block, o_map,
        out_dtype=f32):
    nk = grid[2]
    acc_shape = tuple(s for s in o_block if s is not None)

    def body(a_ref, b_ref, o_ref, acc_ref):
        k = pl.program_id(2)
        part = _dot(a_ref[...], b_ref[...], dims)
        if nk == 1:
            o_ref[...] = part.astype(o_ref.dtype)
        else:
            @pl.when(k == 0)
            def _():
                acc_ref[...] = part

            @pl.when(k > 0)
            def _():
                acc_ref[...] += part

            @pl.when(k == nk - 1)
            def _():
                o_ref[...] = acc_ref[...].astype(o_ref.dtype)

    return pl.pallas_call(
        body, name=name, grid=grid,
        in_specs=[pl.BlockSpec(a_block, a_map), pl.BlockSpec(b_block, b_map)],
        out_specs=pl.BlockSpec(o_block, o_map),
        out_shape=jax.ShapeDtypeStruct(out_shape, out_dtype),
        scratch_shapes=[pltpu.VMEM(acc_shape if nk > 1 else (8, 128), f32)],
        compiler_params=_cp(("parallel", "parallel", "arbitrary")),
    )(a, b)


def _row_block(L, want):
    return want if L % want == 0 else L


def _rmsmod_fwd(x, res, gate, g, sh, sc):
    L = x.shape[0]
    bm = _row_block(L, 256)
    with_res = res is not None

    def body(*refs):
        if with_res:
            x_ref, r_ref, gt_ref, g_ref, sh_ref, sc_ref, xo_ref, h_ref = refs
            xin = x_ref[...] + gt_ref[...] * r_ref[...]
            xo_ref[...] = xin
        else:
            x_ref, g_ref, sh_ref, sc_ref, h_ref = refs
            xin = x_ref[...]
        inv = lax.rsqrt(jnp.mean(xin * xin, axis=-1, keepdims=True) + EPS)
        xn = xin * inv * g_ref[...]
        h_ref[...] = (xn * (1.0 + sc_ref[...]) + sh_ref[...]).astype(h_ref.dtype)

    row = pl.BlockSpec((bm, D), lambda i: (i, 0))
    vec = pl.BlockSpec((1, D), lambda i: (0, 0))
    if with_res:
        return pl.pallas_call(
            body, name="rmsmod_res_fwd", grid=(L // bm,),
            in_specs=[row, row, vec, vec, vec, vec], out_specs=[row, row],
            out_shape=[jax.ShapeDtypeStruct((L, D), f32), jax.ShapeDtypeStruct((L, D), bf16)],
            compiler_params=_cp(("parallel",)),
        )(x, res, gate, g, sh, sc)
    h = pl.pallas_call(
        body, name="rmsmod_fwd", grid=(L // bm,),
        in_specs=[row, vec, vec, vec], out_specs=row,
        out_shape=jax.ShapeDtypeStruct((L, D), bf16),
        compiler_params=_cp(("parallel",)),
    )(x, g, sh, sc)
    return None, h


def _rmsmod_bwd(dh, x, g, sc, dres):
    L = x.shape[0]
    bm = _row_block(L, 256)

    def body(dh_ref, x_ref, g_ref, sc_ref, dr_ref, dx_ref, dsh_ref, dsc_ref, dg_ref):
        i = pl.program_id(0)
        xv = x_ref[...]
        dhv = dh_ref[...]
        inv = lax.rsqrt(jnp.mean(xv * xv, axis=-1, keepdims=True) + EPS)
        xh = xv * inv
        gv = g_ref[...]
        xn = xh * gv
        dxn = dhv * (1.0 + sc_ref[...])
        dxh = dxn * gv
        dx_ref[...] = inv * (dxh - xh * jnp.mean(dxh * xh, axis=-1, keepdims=True)) + dr_ref[...]
        p_sh = jnp.sum(dhv, axis=0, keepdims=True)
        p_sc = jnp.sum(dhv * xn, axis=0, keepdims=True)
        p_g = jnp.sum(dxn * xh, axis=0, keepdims=True)

        @pl.when(i == 0)
        def _():
            dsh_ref[...] = p_sh
            dsc_ref[...] = p_sc
            dg_ref[...] = p_g

        @pl.when(i > 0)
        def _():
            dsh_ref[...] += p_sh
            dsc_ref[...] += p_sc
            dg_ref[...] += p_g

    row = pl.BlockSpec((bm, D), lambda i: (i, 0))
    vec = pl.BlockSpec((1, D), lambda i: (0, 0))
    return pl.pallas_call(
        body, name="rmsmod_bwd", grid=(L // bm,),
        in_specs=[row, row, vec, vec, row], out_specs=[row, vec, vec, vec],
        out_shape=[jax.ShapeDtypeStruct((L, D), f32)] + [jax.ShapeDtypeStruct((1, D), f32)] * 3,
        compiler_params=_cp(("arbitrary",)),
    )(dh, x, g, sc, dres)


def _gate_bwd(dx, y, gate):
    L = dx.shape[0]
    bm = _row_block(L, 256)

    def body(dx_ref, y_ref, gt_ref, dy_ref, dgt_ref):
        i = pl.program_id(0)
        dxv = dx_ref[...]
        dy_ref[...] = (gt_ref[...] * dxv).astype(dy_ref.dtype)
        part = jnp.sum(dxv * y_ref[...], axis=0, keepdims=True)

        @pl.when(i == 0)
        def _():
            dgt_ref[...] = part

        @pl.when(i > 0)
        def _():
            dgt_ref[...] += part

    row = pl.BlockSpec((bm, D), lambda i: (i, 0))
    vec = pl.BlockSpec((1, D), lambda i: (0, 0))
    return pl.pallas_call(
        body, name="gate_bwd", grid=(L // bm,),
        in_specs=[row, row, vec], out_specs=[row, vec],
        out_shape=[jax.ShapeDtypeStruct((L, D), bf16), jax.ShapeDtypeStruct((1, D), f32)],
        compiler_params=_cp(("arbitrary",)),
    )(dx, y, gate)


def _loss_fwd_bwd(x1, down, gate, target):
    L = x1.shape[0]
    bm = _row_block(L, 256)
    nsteps = L // bm

    def body(x_ref, d_ref, gt_ref, t_ref, dy_ref, loss_ref, acc_ref):
        i = pl.program_id(0)
        diff = x_ref[...] + gt_ref[...] * d_ref[...] - t_ref[...]
        dy_ref[...] = diff * (1.0 / D)
        part = jnp.sum(diff * diff, axis=0, keepdims=True)

        @pl.when(i == 0)
        def _():
            acc_ref[...] = part

        @pl.when(i > 0)
        def _():
            acc_ref[...] += part

        @pl.when(i == nsteps - 1)
        def _():
            tot = jnp.sum(acc_ref[...], axis=1, keepdims=True) * (0.5 / D)
            loss_ref[...] = jnp.broadcast_to(tot, (8, 128))

    row = pl.BlockSpec((bm, D), lambda i: (i, 0))
    vec = pl.BlockSpec((1, D), lambda i: (0, 0))
    dy, loss = pl.pallas_call(
        body, name="loss_fwd_bwd", grid=(nsteps,),
        in_specs=[row, row, vec, row],
        out_specs=[row, pl.BlockSpec((8, 128), lambda i: (0, 0))],
        out_shape=[jax.ShapeDtypeStruct((L, D), f32), jax.ShapeDtypeStruct((8, 128), f32)],
        scratch_shapes=[pltpu.VMEM((1, D), f32)],
        compiler_params=_cp(("arbitrary",)),
    )(x1, down, gate, target)
    return loss[0, 0], dy


def _head_mean_matrix():
    r = lax.broadcasted_iota(jnp.int32, (128, 128), 0) // HD
    c = lax.broadcasted_iota(jnp.int32, (128, 128), 1) // HD
    return jnp.where(r == c, 1.0 / HD, 0.0).astype(bf16)


def _qknorm_fwd(p, gqk):
    L = p.shape[0]
    bm = _row_block(L, 256)

    def body(p_ref, g_ref, o_ref):
        e = _head_mean_matrix()
        for c in range(8):
            sl = slice(128 * c, 128 * (c + 1))
            xv = p_ref[:, sl]
            inv = lax.rsqrt(_dot_exact(xv * xv, e) + EPS)
            yv = xv * inv * g_ref[:, sl]
            if c < 4:
                yv = yv * (1.0 / math.sqrt(HD))
            o_ref[:, sl] = yv.astype(o_ref.dtype)
        o_ref[:, 1024:1536] = p_ref[:, 1024:1536].astype(o_ref.dtype)

    return pl.pallas_call(
        body, name="qknorm_fwd", grid=(L // bm,),
        in_specs=[pl.BlockSpec((bm, 1536), lambda i: (i, 0)), pl.BlockSpec((1, 1024), lambda i: (0, 0))],
        out_specs=pl.BlockSpec((bm, 1536), lambda i: (i, 0)),
        out_shape=jax.ShapeDtypeStruct((L, 1536), bf16),
        compiler_params=_cp(("parallel",)),
    )(p, gqk)


def _qknorm_bwd(dqn, dkn, dval, du, p, gqk):
    L = p.shape[0]
    bm = _row_block(L, 256)
    nsteps = L // bm

    def body(dq_ref, dk_ref, dv_ref, du_ref, p_ref, g_ref, o_ref, dg_ref, acc_ref):
        i = pl.program_id(0)
        e = _head_mean_matrix()
        o_ref[:, 1024:1536] = dv_ref[...].astype(o_ref.dtype)
        o_ref[:, 1536:2048] = du_ref[...].astype(o_ref.dtype)
        for c in range(8):
            sl = slice(128 * c, 128 * (c + 1))
            xv = p_ref[:, sl]
            d_ref = dq_ref if c < 4 else dk_ref
            dv = d_ref[:, 128 * (c % 4):128 * (c % 4 + 1)]
            gv = g_ref[:, sl]
            inv = lax.rsqrt(_dot_exact(xv * xv, e) + EPS)
            xh = xv * inv
            dxh = dv * gv
            o_ref[:, sl] = (inv * (dxh - xh * _dot_exact(dxh * xh, e))).astype(o_ref.dtype)
            part = jnp.sum(dv * xh, axis=0, keepdims=True)

            @pl.when(i == 0)
            def _():
                acc_ref[:, sl] = part

            @pl.when(i > 0)
            def _():
                acc_ref[:, sl] += part

        @pl.when(i == nsteps - 1)
        def _():
            r = lax.broadcasted_iota(jnp.int32, (1024, 128), 0)
            col = lax.broadcasted_iota(jnp.int32, (1024, 128), 1)
            fold = jnp.where(col == (r // 512) * HD + r % HD, 1.0, 0.0).astype(bf16)
            dg_ref[...] = _dot_exact(jnp.broadcast_to(acc_ref[...], (8, 1024)), fold)

    half = pl.BlockSpec((bm, 512), lambda i: (i, 0))
    return pl.pallas_call(
        body, name="qknorm_bwd", grid=(nsteps,),
        in_specs=[half, half, half, half, pl.BlockSpec((bm, 1024), lambda i: (i, 0)),
                  pl.BlockSpec((1, 1024), lambda i: (0, 0))],
        out_specs=[pl.BlockSpec((bm, INW), lambda i: (i, 0)), pl.BlockSpec((8, 128), lambda i: (0, 0))],
        out_shape=[jax.ShapeDtypeStruct((L, INW), bf16), jax.ShapeDtypeStruct((8, 128), f32)],
        scratch_shapes=[pltpu.VMEM((1, 1024), f32)],
        compiler_params=_cp(("arbitrary",)),
    )(dqn, dkn, dval, du, p, gqk)


def _diag_tn(name, a, b, rpg, b_blk0=0):
    L = a.shape[0]
    aw = 8 * rpg
    bk = _row_block(L, 512)
    nk = L // bk

    def body(a_ref, b_ref, o_ref, acc_ref):
        k = pl.program_id(1)
        part = _dot(a_ref[...], b_ref[...], TN)

        @pl.when(k == 0)
        def _():
            acc_ref[...] = part

        @pl.when(k > 0)
        def _():
            acc_ref[...] += part

        @pl.when(k == nk - 1)
        def _():
            r = lax.broadcasted_iota(jnp.int32, (aw, 128), 0) // rpg
            c = lax.broadcasted_iota(jnp.int32, (aw, 128), 1) // NH16
            fr = lax.broadcasted_iota(jnp.int32, (128, NH16), 0) % NH16
            fc = lax.broadcasted_iota(jnp.int32, (128, NH16), 1)
            fold = jnp.where(fr == fc, 1.0, 0.0).astype(bf16)
            o_ref[...] = _dot_exact(jnp.where(r == c, acc_ref[...], 0.0), fold)

    return pl.pallas_call(
        body, name=name, grid=(4, nk),
        in_specs=[pl.BlockSpec((bk, aw), lambda i, k: (k, i)),
                  pl.BlockSpec((bk, 128), lambda i, k: (k, b_blk0 + i))],
        out_specs=pl.BlockSpec((aw, NH16), lambda i, k: (i, 0)),
        out_shape=jax.ShapeDtypeStruct((4 * aw, NH16), f32),
        scratch_shapes=[pltpu.VMEM((aw, 128), f32)],
        compiler_params=_cp(("parallel", "arbitrary")),
    )(a, b)


def _outnorm_fwd(oa, os_, ga, gs):
    L = oa.shape[0]
    bm = _row_block(L, 256)

    def body(a_ref, s_ref, ga_ref, gs_ref, o_ref):
        for x_ref, g_ref, off in ((a_ref, ga_ref, 0), (s_ref, gs_ref, 512)):
            xv = x_ref[...]
            inv = lax.rsqrt(jnp.mean(xv * xv, axis=-1, keepdims=True) + EPS)
            o_ref[:, off:off + 512] = (xv * inv * g_ref[...]).astype(o_ref.dtype)

    half = pl.BlockSpec((bm, 512), lambda i: (i, 0))
    vec = pl.BlockSpec((1, 512), lambda i: (0, 0))
    return pl.pallas_call(
        body, name="outnorm_fwd", grid=(L // bm,),
        in_specs=[half, half, vec, vec], out_specs=pl.BlockSpec((bm, D), lambda i: (i, 0)),
        out_shape=jax.ShapeDtypeStruct((L, D), bf16),
        compiler_params=_cp(("parallel",)),
    )(oa, os_, ga, gs)


def _outnorm_bwd(do, oa, os_, ga, gs):
    L = oa.shape[0]
    bm = _row_block(L, 256)

    def body(do_ref, a_ref, s_ref, ga_ref, gs_ref, da_ref, ds_ref, dga_ref, dgs_ref):
        i = pl.program_id(0)
        for x_ref, g_ref, dx_ref, dg_ref, off in ((a_ref, ga_ref, da_ref, dga_ref, 0),
                                                  (s_ref, gs_ref, ds_ref, dgs_ref, 512)):
            xv = x_ref[...]
            dv = do_ref[:, off:off + 512]
            inv = lax.rsqrt(jnp.mean(xv * xv, axis=-1, keepdims=True) + EPS)
            xh = xv * inv
            dxh = dv * g_ref[...]
            dx_ref[...] = inv * (dxh - xh * jnp.mean(dxh * xh, axis=-1, keepdims=True))
            part = jnp.sum(dv * xh, axis=0, keepdims=True)

            @pl.when(i == 0)
            def _():
                dg_ref[...] = part

            @pl.when(i > 0)
            def _():
                dg_ref[...] += part

    half = pl.BlockSpec((bm, 512), lambda i: (i, 0))
    vec = pl.BlockSpec((1, 512), lambda i: (0, 0))
    return pl.pallas_call(
        body, name="outnorm_bwd", grid=(L // bm,),
        in_specs=[pl.BlockSpec((bm, D), lambda i: (i, 0)), half, half, vec, vec],
        out_specs=[half, half, vec, vec],
        out_shape=[jax.ShapeDtypeStruct((L, 512), f32)] * 2 + [jax.ShapeDtypeStruct((1, 512), f32)] * 2,
        compiler_params=_cp(("arbitrary",)),
    )(do, oa, os_, ga, gs)


NCB = DFF // 128


def _shift_down(x, s, row):
    return jnp.where(row >= s, pltpu.roll(x, s, 0), 0.0)


def _shift_up(x, s, row, L):
    return jnp.where(row < L - s, pltpu.roll(x, L - s, 0), 0.0)


def _conv(x, w_ref, b_ref, row):
    out = b_ref[...] + _shift_down(x, 2, row) * w_ref[0:1, :]
    out = out + _shift_down(x, 1, row) * w_ref[1:2, :]
    return out + x * w_ref[2:3, :]


def _convglu_fwd(up_pre, cw, cb):
    L = up_pre.shape[0]

    def body(v_ref, g_ref, wv_ref, wg_ref, bv_ref, bg_ref, a_ref):
        row = lax.broadcasted_iota(jnp.int32, (L, 128), 0)
        val = _conv(v_ref[...], wv_ref, bv_ref, row)
        gate = _conv(g_ref[...], wg_ref, bg_ref, row)
        a_ref[...] = (_gelu(gate) * val).astype(a_ref.dtype)

    col = lambda off: pl.BlockSpec((L, 128), lambda j: (0, j + off))
    wsp = lambda off: pl.BlockSpec((3, 128), lambda j: (0, j + off))
    bsp = lambda off: pl.BlockSpec((1, 128), lambda j: (0, j + off))
    return pl.pallas_call(
        body, name="convglu_fwd", grid=(NCB,),
        in_specs=[col(0), col(NCB), wsp(0), wsp(NCB), bsp(0), bsp(NCB)],
        out_specs=pl.BlockSpec((L, 128), lambda j: (0, j)),
        out_shape=jax.ShapeDtypeStruct((L, DFF), bf16),
        compiler_params=_cp(("parallel",)),
    )(up_pre, up_pre, cw, cw, cb, cb)


def _convglu_bwd(da, up_pre, cw, cb):
    L = up_pre.shape[0]

    def body(da_ref, v_ref, g_ref, wv_ref, wg_ref, bv_ref, bg_ref, dx_ref, dw_ref, db_ref):
        row = lax.broadcasted_iota(jnp.int32, (L, 128), 0)
        dav = da_ref[...]
        u = []
        for x_ref, w_ref, b_ref in ((v_ref, wv_ref, bv_ref), (g_ref, wg_ref, bg_ref)):
            u.append(_conv(x_ref[...], w_ref, b_ref, row))
        val, gate = u
        c = math.sqrt(2.0 / math.pi)
        t = jnp.tanh(c * (gate + 0.044715 * (gate * gate * gate)))
        d_val = dav * (0.5 * gate * (1.0 + t))
        d_gate = dav * val * (0.5 * (1.0 + t)
                              + 0.5 * gate * (1.0 - t * t) * c * (1.0 + 3.0 * 0.044715 * (gate * gate)))
        for k, (d, x_ref, w_ref) in enumerate(((d_val, v_ref, wv_ref), (d_gate, g_ref, wg_ref))):
            dx = d * w_ref[2:3, :] + _shift_up(d, 1, row, L) * w_ref[1:2, :]
            dx_ref[k] = (dx + _shift_up(d, 2, row, L) * w_ref[0:1, :]).astype(dx_ref.dtype)
            xv = x_ref[...]
            dw_ref[k, 0:1, :] = jnp.sum(d * _shift_down(xv, 2, row), axis=0, keepdims=True)
            dw_ref[k, 1:2, :] = jnp.sum(d * _shift_down(xv, 1, row), axis=0, keepdims=True)
            dw_ref[k, 2:3, :] = jnp.sum(d * xv, axis=0, keepdims=True)
            db_ref[k] = jnp.sum(d, axis=0, keepdims=True)

    col = lambda off: pl.BlockSpec((L, 128), lambda j: (0, j + off))
    wsp = lambda off: pl.BlockSpec((3, 128), lambda j: (0, j + off))
    bsp = lambda off: pl.BlockSpec((1, 128), lambda j: (0, j + off))
    dx, dw, db = pl.pallas_call(
        body, name="convglu_bwd", grid=(NCB,),
        in_specs=[col(0), col(0), col(NCB), wsp(0), wsp(NCB), bsp(0), bsp(NCB)],
        out_specs=[pl.BlockSpec((2, L, 128), lambda j: (0, 0, j)), pl.BlockSpec((2, 3, 128), lambda j: (0, 0, j)),
                   pl.BlockSpec((2, 1, 128), lambda j: (0, 0, j))],
        out_shape=[jax.ShapeDtypeStruct((2, L, DFF), bf16), jax.ShapeDtypeStruct((2, 3, DFF), f32),
                   jax.ShapeDtypeStruct((2, 1, DFF), f32)],
        compiler_params=_cp(("parallel",)),
    )(da, up_pre, up_pre, cw, cw, cb, cb)
    return dx, dw.transpose(1, 0, 2).reshape(3, DUP), db.transpose(1, 0, 2).reshape(1, DUP)


def _attn_block(L):
    return 256 if L % 256 == 0 else 128


TAIL_DEAD = -110.0


def _dot_exact2(x, m):
    hi = x.astype(bf16)
    lo = (x - hi.astype(f32)).astype(bf16)
    return (lax.dot_general(hi, m, NN, preferred_element_type=f32)
            + lax.dot_general(lo, m, NN, preferred_element_type=f32))


def _sb_weights(z, mask, tri_gt, carry):
    l1p = jnp.log(1.0 + jnp.exp(-jnp.abs(z)))
    ls_pos = jnp.minimum(z, 0.0) - l1p
    lm = ls_pos - z
    if mask is not None:
        lm = jnp.where(mask, lm, 0.0)
    tail = _dot_exact2(lm, tri_gt) + carry
    w = jnp.exp(ls_pos + tail)
    if mask is not None:
        w = jnp.where(mask, w, 0.0)
    return w, ls_pos, lm


def _head_masks(B):
    lane = lax.broadcasted_iota(jnp.int32, (B, 128), 1)
    return lane < HD, lane >= HD


def _attn_fwd(qkv):
    L = qkv.shape[0]
    B = _attn_block(L)
    nq = L // B

    def body(q_ref, k_ref, v_ref, o_ref):
        qi = pl.program_id(1)
        heads = _head_masks(B)
        ti = lax.broadcasted_iota(jnp.int32, (B, B), 0)
        si = lax.broadcasted_iota(jnp.int32, (B, B), 1)
        tri_gt = jnp.where(ti > si, 1.0, 0.0).astype(bf16)
        diag = si < ti
        qv = q_ref[...]
        zero16 = jnp.zeros((), bf16)

        def tile(jb, mask, acc, carries):
            ks = pl.multiple_of(jb * B, B)
            kb = k_ref[pl.ds(ks, B), :]
            vb = v_ref[pl.ds(ks, B), :]
            out = []
            for in_head, carry in zip(heads, carries):
                kh = jnp.where(in_head, kb, zero16)
                vh = jnp.where(in_head, vb, zero16)
                z = lax.dot_general(qv, kh, NT, preferred_element_type=f32)
                w, _, lm = _sb_weights(z, mask, tri_gt, carry)
                acc = acc + lax.dot_general(w.astype(bf16), vh, NN, preferred_element_type=f32)
                out.append(carry + jnp.sum(lm, axis=1, keepdims=True))
            return acc, out

        zc = jnp.zeros((B, 1), f32)
        acc, (c0, c1) = tile(qi, diag, jnp.zeros((B, 128), f32), (zc, zc))

        def cond(st):
            return jnp.logical_and(st[0] <= qi, st[4] > TAIL_DEAD)

        def step(st):
            n, a, p0, p1, _ = st
            a, (p0, p1) = tile(qi - n, None, a, (p0, p1))
            return n + 1, a, p0, p1, jnp.maximum(jnp.max(p0), jnp.max(p1))

        st = lax.while_loop(cond, step, (jnp.int32(1), acc, c0, c1, jnp.maximum(jnp.max(c0), jnp.max(c1))))
        o_ref[...] = st[1]

    return pl.pallas_call(
        body, name="attn_fwd", grid=(4, nq),
        in_specs=[pl.BlockSpec((B, 128), lambda hp, i: (i, hp)),
                  pl.BlockSpec((L, 128), lambda hp, i: (0, 4 + hp)),
                  pl.BlockSpec((L, 128), lambda hp, i: (0, 8 + hp))],
        out_specs=pl.BlockSpec((B, 128), lambda hp, i: (i, hp)),
        out_shape=jax.ShapeDtypeStruct((L, ATT), f32),
        compiler_params=_cp(("parallel", "parallel")),
    )(qkv, qkv, qkv)


def _attn_bwd(qkv, o, do):
    L = qkv.shape[0]
    B = _attn_block(L)
    nq = L // B

    def body(q_ref, k_ref, v_ref, o_ref, do_ref, dq_ref, dk_ref, dv_ref):
        qi = pl.program_id(1)

        @pl.when(qi == 0)
        def _():
            dk_ref[...] = jnp.zeros_like(dk_ref)
            dv_ref[...] = jnp.zeros_like(dv_ref)

        heads = _head_masks(B)
        ti = lax.broadcasted_iota(jnp.int32, (B, B), 0)
        si = lax.broadcasted_iota(jnp.int32, (B, B), 1)
        tri_gt = jnp.where(ti > si, 1.0, 0.0).astype(bf16)
        tri_ge = jnp.where(ti >= si, 1.0, 0.0).astype(bf16)
        diag = si < ti
        zero16 = jnp.zeros((), bf16)
        qv = q_ref[...]
        do16 = do_ref[...].astype(bf16)
        dsum_lanes = do16.astype(f32) * o_ref[...]
        qhs = [jnp.where(m, qv, zero16) for m in heads]
        dohs = [jnp.where(m, do16, zero16) for m in heads]
        dsums = [jnp.sum(jnp.where(m, dsum_lanes, 0.0), axis=1, keepdims=True) for m in heads]

        def tile(jb, mask, dq_a, state):
            ks = pl.multiple_of(jb * B, B)
            kb = k_ref[pl.ds(ks, B), :]
            vb = v_ref[pl.ds(ks, B), :]
            dk_blk = jnp.zeros((B, 128), f32)
            dv_blk = jnp.zeros((B, 128), f32)
            out = []
            for in_head, qh, doh, dsum, (carry, suffix) in zip(heads, qhs, dohs, dsums, state):
                kh = jnp.where(in_head, kb, zero16)
                vh = jnp.where(in_head, vb, zero16)
                z = lax.dot_general(qv, kh, NT, preferred_element_type=f32)
                w, ls_pos, lm = _sb_weights(z, mask, tri_gt, carry)
                w16 = w.astype(bf16)
                dw = lax.dot_general(doh, vh, NT, preferred_element_type=f32)
                da = w16.astype(f32) * dw
                sig = jnp.exp(ls_pos)
                suf = _dot_exact2(da, tri_ge) + suffix
                dz = da * (1.0 - sig) - sig * (dsum - suf)
                if mask is not None:
                    dz = jnp.where(mask, dz, 0.0)
                dz16 = dz.astype(bf16)
                dq_a = dq_a + lax.dot_general(dz16, kh, NN, preferred_element_type=f32)
                dk_blk = dk_blk + lax.dot_general(dz16, qh, TN, preferred_element_type=f32)
                dv_blk = dv_blk + lax.dot_general(w16, doh, TN, preferred_element_type=f32)
                out.append((carry + jnp.sum(lm, axis=1, keepdims=True),
                            suffix + jnp.sum(da, axis=1, keepdims=True)))
            dk_ref[pl.ds(ks, B), :] += dk_blk
            dv_ref[pl.ds(ks, B), :] += dv_blk
            return dq_a, out

        def alive(state):
            return jnp.maximum(jnp.max(state[0][0]), jnp.max(state[1][0]))

        zc = jnp.zeros((B, 1), f32)
        dq_acc, state = tile(qi, diag, jnp.zeros((B, 128), f32), ((zc, zc), (zc, zc)))

        def cond(st):
            return jnp.logical_and(st[0] <= qi, st[6] > TAIL_DEAD)

        def step(st):
            n, a, c0, s0, c1, s1, _ = st
            a, new = tile(qi - n, None, a, ((c0, s0), (c1, s1)))
            return n + 1, a, new[0][0], new[0][1], new[1][0], new[1][1], alive(new)

        st = lax.while_loop(cond, step, (jnp.int32(1), dq_acc, state[0][0], state[0][1],
                                         state[1][0], state[1][1], alive(state)))
        dq_ref[...] = st[1] * (1.0 / math.sqrt(HD))

    blk = pl.BlockSpec((B, 128), lambda hp, i: (i, hp))
    dqkv = pl.pallas_call(
        body, name="attn_bwd", grid=(4, nq),
        in_specs=[blk,
                  pl.BlockSpec((L, 128), lambda hp, i: (0, 4 + hp)),
                  pl.BlockSpec((L, 128), lambda hp, i: (0, 8 + hp)),
                  blk, blk],
        out_specs=[blk, pl.BlockSpec((L, 128), lambda hp, i: (0, hp)),
                   pl.BlockSpec((L, 128), lambda hp, i: (0, hp))],
        out_shape=[jax.ShapeDtypeStruct((L, ATT), f32)] * 3,
        compiler_params=_cp(("parallel", "arbitrary")),
    )(qkv, qkv, qkv, o, do)
    return dqkv


def _s5_disc(ldt, ar, ai, br, bi):
    dt = jnp.exp(ldt)
    mag = jnp.exp(dt * ar)
    abar_re = mag * jnp.cos(dt * ai)
    abar_im = mag * jnp.sin(dt * ai)
    em_re = abar_re - 1.0
    em_im = abar_im
    den = ar * ar + ai * ai
    f_re = (em_re * ar + em_im * ai) / den
    f_im = (em_im * ar - em_re * ai) / den
    bb_re = f_re * br - f_im * bi
    bb_im = f_re * bi + f_im * br
    return abar_re, abar_im, bb_re, bb_im


def _s5_disc_fwd(ldt, ar, ai, br, bi):
    def body(ldt_ref, ar_ref, ai_ref, br_ref, bi_ref, o1, o2, o3, o4):
        outs = _s5_disc(ldt_ref[...], ar_ref[...], ai_ref[...], br_ref[...], bi_ref[...])
        for o_ref, v in zip((o1, o2, o3, o4), outs):
            o_ref[...] = v

    col = jax.ShapeDtypeStruct((NS, 1), f32)
    mat = jax.ShapeDtypeStruct((NS, NH16), f32)
    return pl.pallas_call(body, name="s5_disc_fwd", out_shape=[col, col, mat, mat],
                          compiler_params=_cp())(ldt, ar, ai, br, bi)


def _s5_disc_bwd(ldt, ar, ai, br, bi, d_are, d_aim, d_bbre, d_bbim):
    def body(ldt_ref, ar_ref, ai_ref, br_ref, bi_ref, c1, c2, c3, c4, o_ldt, o_ar, o_ai, o_br, o_bi):
        prim = (ldt_ref[...], ar_ref[...], ai_ref[...], br_ref[...], bi_ref[...])
        _, vjp = jax.vjp(_s5_disc, *prim)
        g_ldt, g_ar, g_ai, g_br, g_bi = vjp((c1[...], c2[...], c3[...], c4[...]))
        o_ar[...] = g_ar
        o_ai[...] = g_ai
        o_br[...] = g_br
        o_bi[...] = g_bi
        r = lax.broadcasted_iota(jnp.int32, (NG, NS), 0)
        c = lax.broadcasted_iota(jnp.int32, (NG, NS), 1)
        fold = jnp.where(c // NP == r, 1.0, 0.0).astype(bf16)
        hi = jnp.broadcast_to(g_ldt, (NS, 128))
        p1 = hi.astype(bf16)
        r1 = hi - p1.astype(f32)
        p2 = r1.astype(bf16)
        p3 = (r1 - p2.astype(f32)).astype(bf16)
        dd = lambda p: lax.dot_general(fold, p, NN, preferred_element_type=f32)
        o_ldt[...] = dd(p1) + dd(p2) + dd(p3)

    col = jax.ShapeDtypeStruct((NS, 1), f32)
    mat = jax.ShapeDtypeStruct((NS, NH16), f32)
    return pl.pallas_call(
        body, name="s5_disc_bwd",
        out_shape=[jax.ShapeDtypeStruct((NG, 128), f32), col, col, mat, mat],
        compiler_params=_cp())(ldt, ar, ai, br, bi, d_are, d_aim, d_bbre, d_bbim)


SCAN_LANES = 256


def _cmul(ar, ai, br, bi):
    return ar * br - ai * bi, ar * bi + ai * br


def _pow_tables(ar, ai):
    a2 = _cmul(ar, ai, ar, ai)
    a4 = _cmul(*a2, *a2)
    rows = [(ar, ai)]
    for _ in range(7):
        rows.append(_cmul(*rows[-1], ar, ai))
    return (ar, ai), a2, a4, rows


def _scan_rows(xr_ref, xi_ref, abar_re, abar_im, cr_ref, ci_ref, reverse, on_tile=None):
    T = xr_ref.shape[0]
    nt = T // 8
    row = lax.broadcasted_iota(jnp.int32, (8, SCAN_LANES), 0)
    for c in range(NS // SCAN_LANES):
        ls = slice(c * SCAN_LANES, (c + 1) * SCAN_LANES)
        ar = abar_re[:, ls]
        ai = abar_im[:, ls]
        if reverse:
            ai = -ai
        a1, a2, a4, prow = _pow_tables(ar, ai)
        if reverse:
            prow = prow[::-1]
        pr = jnp.concatenate([p[0] for p in prow], axis=0)
        pi = jnp.concatenate([p[1] for p in prow], axis=0)
        a8r, a8i = prow[0] if reverse else prow[7]
        edge = slice(0, 1) if reverse else slice(7, 8)
        steps = []
        for d, (dr, di) in ((1, a1), (2, a2), (4, a4)):
            keep = (row < 8 - d) if reverse else (row >= d)
            steps.append((8 - d if reverse else d, jnp.where(keep, dr, 0.0), jnp.where(keep, di, 0.0)))

        def local(n, _):
            rs = pl.ds(pl.multiple_of(n * 8, 8), 8)
            xr = xr_ref[rs, ls]
            xi = xi_ref[rs, ls]
            for shift, dr, di in steps:
                mr, mi = _cmul(dr, di, pltpu.roll(xr, shift, 0), pltpu.roll(xi, shift, 0))
                xr = xr + mr
                xi = xi + mi
            xr_ref[rs, ls] = xr
            xi_ref[rs, ls] = xi
            return 0

        lax.fori_loop(0, nt, local, 0, unroll=4)

        def chain(n, carry):
            cr, ci = carry
            it = (nt - 1 - n) if reverse else n
            rs = pl.ds(pl.multiple_of(it * 8, 8), 8)
            xr = xr_ref[rs, ls]
            xi = xi_ref[rs, ls]
            nr, ni = _cmul(a8r, a8i, cr, ci)
            nr = nr + xr[edge, :]
            ni = ni + xi[edge, :]
            mr, mi = _cmul(pr, pi, cr, ci)
            xr = xr + mr
            xi = xi + mi
            xr_ref[rs, ls] = xr
            xi_ref[rs, ls] = xi
            if on_tile is not None:
                nxt_r = jnp.where(row < 7, pltpu.roll(xr, 7, 0), cr)
                nxt_i = jnp.where(row < 7, pltpu.roll(xi, 7, 0), ci)
                on_tile(ls, rs, nxt_r, nxt_i)
            return nr, ni

        cr, ci = lax.fori_loop(0, nt, chain, (cr_ref[:, ls], ci_ref[:, ls]), unroll=2)
        cr_ref[:, ls] = cr
        ci_ref[:, ls] = ci


def _s5_chunk(L):
    return 256 if L % 256 == 0 else L


def _s5_gate(s_re, s_im, u, ccr_ref, cci_ref, dsk_ref, gw_ref, gb_ref):
    y = _dot(s_re, ccr_ref[...]) - _dot(s_im, cci_ref[...]) + dsk_ref[...] * u
    yg = _gelu(y)
    gate = _sigmoid(_dot(yg, gw_ref[...]) + gb_ref[...])
    return y, yg, gate


def _s5_fwd(p, abar_re, abar_im, bbr, bbi, ccr, cci, dsk, gw, gb):
    L = p.shape[0]
    T = _s5_chunk(L)

    def body(u_ref, are_ref, aim_ref, bbr_ref, bbi_ref, ccr_ref, cci_ref, dsk_ref, gw_ref, gb_ref,
             o_ref, sr_ref, si_ref, cr_ref, ci_ref):
        @pl.when(pl.program_id(0) == 0)
        def _():
            cr_ref[...] = jnp.zeros_like(cr_ref)
            ci_ref[...] = jnp.zeros_like(ci_ref)

        u = u_ref[...]
        sr_ref[...] = _dot(u, bbr_ref[...])
        si_ref[...] = _dot(u, bbi_ref[...])
        _scan_rows(sr_ref, si_ref, are_ref[...], aim_ref[...], cr_ref, ci_ref, reverse=False)
        _, yg, gate = _s5_gate(sr_ref[...], si_ref[...], u, ccr_ref, cci_ref, dsk_ref, gw_ref, gb_ref)
        o_ref[...] = yg * gate

    full = lambda shape: pl.BlockSpec(shape, lambda i: (0, 0))
    return pl.pallas_call(
        body, name="s5_fwd", grid=(L // T,),
        in_specs=[pl.BlockSpec((T, SSMW), lambda i: (i, 3)), full((1, NS)), full((1, NS)),
                  full((SSMW, NS)), full((SSMW, NS)), full((NS, SSMW)), full((NS, SSMW)),
                  full((1, SSMW)), full((SSMW, SSMW)), full((1, SSMW))],
        out_specs=[pl.BlockSpec((T, SSMW), lambda i: (i, 0)), pl.BlockSpec((T, NS), lambda i: (i, 0)),
                   pl.BlockSpec((T, NS), lambda i: (i, 0))],
        out_shape=[jax.ShapeDtypeStruct((L, SSMW), f32), jax.ShapeDtypeStruct((L, NS), f32),
                   jax.ShapeDtypeStruct((L, NS), f32)],
        scratch_shapes=[pltpu.VMEM((1, NS), f32), pltpu.VMEM((1, NS), f32)],
        compiler_params=_cp(("arbitrary",)),
    )(p, abar_re, abar_im, bbr, bbi, ccr, cci, dsk, gw, gb)


def _s5_bwd(dout, p, s_re, s_im, abar_re, abar_im, bbr, bbi, ccr, cci, dsk, gw, gb):
    L = p.shape[0]
    T = _s5_chunk(L)
    nchunk = L // T

    def body(do_ref, u_ref, sr_ref, si_ref, are_ref, aim_ref, bbr_ref, bbi_ref, ccr_ref, cci_ref,
             dsk_ref, gw_ref, gb_ref,
             du_ref, dy_ref, yg_ref, dgl_ref, lr16_ref, li16_ref, dare_ref, daim_ref, dd_ref, dgb_ref,
             lr_ref, li_ref, cr_ref, ci_ref, accr_ref, acci_ref):
        i = pl.program_id(0)

        @pl.when(i == 0)
        def _():
            cr_ref[...] = jnp.zeros_like(cr_ref)
            ci_ref[...] = jnp.zeros_like(ci_ref)
            accr_ref[...] = jnp.zeros_like(accr_ref)
            acci_ref[...] = jnp.zeros_like(acci_ref)
            dd_ref[...] = jnp.zeros_like(dd_ref)
            dgb_ref[...] = jnp.zeros_like(dgb_ref)

        u = u_ref[...]
        dov = do_ref[...]
        y, yg, gate = _s5_gate(sr_ref[...], si_ref[...], u, ccr_ref, cci_ref, dsk_ref, gw_ref, gb_ref)
        dgl = dov * yg * gate * (1.0 - gate)
        dyg = dov * gate + _dot(dgl, gw_ref[...], NT)
        dy = dyg * _gelu_grad(y)
        dy_ref[...] = dy.astype(dy_ref.dtype)
        yg_ref[...] = yg.astype(yg_ref.dtype)
        dgl_ref[...] = dgl.astype(dgl_ref.dtype)
        dd_ref[...] += jnp.sum(dy * u, axis=0, keepdims=True)
        dgb_ref[...] += jnp.sum(dgl, axis=0, keepdims=True)
        lr_ref[...] = _dot(dy, ccr_ref[...], NT)
        li_ref[...] = -_dot(dy, cci_ref[...], NT)

        def on_tile(ls, rs, nxt_r, nxt_i):
            s_r = sr_ref[rs, ls]
            s_i = si_ref[rs, ls]
            accr_ref[:, ls] += nxt_r * s_r + nxt_i * s_i
            acci_ref[:, ls] += nxt_i * s_r - nxt_r * s_i

        _scan_rows(lr_ref, li_ref, are_ref[...], aim_ref[...], cr_ref, ci_ref, reverse=True,
                   on_tile=on_tile)
        lam_r = lr_ref[...]
        lam_i = li_ref[...]
        lr16_ref[...] = lam_r.astype(lr16_ref.dtype)
        li16_ref[...] = lam_i.astype(li16_ref.dtype)
        du_ref[...] = dy * dsk_ref[...] + _dot(lam_r, bbr_ref[...], NT) + _dot(lam_i, bbi_ref[...], NT)

        @pl.when(i == nchunk - 1)
        def _():
            dare_ref[...] = jnp.sum(accr_ref[...], axis=0, keepdims=True)
            daim_ref[...] = jnp.sum(acci_ref[...], axis=0, keepdims=True)

    rev = lambda i: (nchunk - 1 - i, 0)
    full = lambda shape: pl.BlockSpec(shape, lambda i: (0, 0))
    half = pl.BlockSpec((T, SSMW), rev)
    wide = pl.BlockSpec((T, NS), rev)
    return pl.pallas_call(
        body, name="s5_bwd", grid=(nchunk,),
        in_specs=[half, pl.BlockSpec((T, SSMW), lambda i: (nchunk - 1 - i, 3)), wide, wide,
                  full((1, NS)), full((1, NS)), full((SSMW, NS)), full((SSMW, NS)),
                  full((NS, SSMW)), full((NS, SSMW)), full((1, SSMW)), full((SSMW, SSMW)), full((1, SSMW))],
        out_specs=[half, half, half, half, wide, wide, full((1, NS)), full((1, NS)),
                   full((1, SSMW)), full((1, SSMW))],
        out_shape=[jax.ShapeDtypeStruct((L, SSMW), f32)] + [jax.ShapeDtypeStruct((L, SSMW), bf16)] * 3
                  + [jax.ShapeDtypeStruct((L, NS), bf16)] * 2 + [jax.ShapeDtypeStruct((1, NS), f32)] * 2
                  + [jax.ShapeDtypeStruct((1, SSMW), f32)] * 2,
        scratch_shapes=[pltpu.VMEM((T, NS), f32), pltpu.VMEM((T, NS), f32),
                        pltpu.VMEM((1, NS), f32), pltpu.VMEM((1, NS), f32),
                        pltpu.VMEM((8, NS), f32), pltpu.VMEM((8, NS), f32)],
        compiler_params=_cp(("arbitrary",)),
    )(dout, p, s_re, s_im, abar_re, abar_im, bbr, bbi, ccr, cci, dsk, gw, gb)


def _block_diag(x):
    g, r, c = x.shape
    eye = jnp.eye(g, dtype=x.dtype)
    return (x[:, :, None, :] * eye[:, None, :, None]).reshape(g * r, g * c)


def _mod_fwd(c16, ada_w, ada_b_loc):
    nloc = ada_w.shape[2]

    def body(c_ref, w_ref, b_ref, o_ref):
        cv = c_ref[...]
        act = cv * _sigmoid(cv)
        o_ref[...] = _dot(act, w_ref[...]) + b_ref[...]

    return pl.pallas_call(
        body, name="mod_fwd", grid=(DEPTH,),
        in_specs=[pl.BlockSpec((16, D), lambda l: (0, 0)), pl.BlockSpec((None, D, nloc), lambda l: (l, 0, 0)),
                  pl.BlockSpec((None, 1, nloc), lambda l: (l, 0, 0))],
        out_specs=pl.BlockSpec((None, 16, nloc), lambda l: (l, 0, 0)),
        out_shape=jax.ShapeDtypeStruct((DEPTH, 16, nloc), f32),
        compiler_params=_cp(("parallel",)),
    )(c16, ada_w, ada_b_loc)


def _ada_grad(c16, dmod16):
    nloc = dmod16.shape[2]

    def body(c_ref, d_ref, o_ref):
        cv = c_ref[...]
        act = cv * _sigmoid(cv)
        o_ref[...] = _dot(act, d_ref[...], TN)

    return pl.pallas_call(
        body, name="ada_grad", grid=(DEPTH,),
        in_specs=[pl.BlockSpec((16, D), lambda l: (0, 0)), pl.BlockSpec((None, 16, nloc), lambda l: (l, 0, 0))],
        out_specs=pl.BlockSpec((None, D, nloc), lambda l: (l, 0, 0)),
        out_shape=jax.ShapeDtypeStruct((DEPTH, D, nloc), f32),
        compiler_params=_cp(("parallel",)),
    )(c16, dmod16)


def _as2d(a):
    return a.reshape(-1, a.shape[-1])


def _ew_rows(rows):
    for cand in (512, 256, 128, 64, 32, 16, 8):
        if rows % cand == 0:
            return cand
    return rows


def _cast_bf16(w):
    w2 = _as2d(w)
    rows, cols = w2.shape
    bm = _ew_rows(rows)

    def body(x_ref, o_ref):
        o_ref[...] = x_ref[...].astype(bf16)

    spec = pl.BlockSpec((bm, cols), lambda i: (i, 0))
    out = pl.pallas_call(body, name="cast_bf16", grid=(rows // bm,), in_specs=[spec], out_specs=spec,
                         out_shape=jax.ShapeDtypeStruct((rows, cols), bf16),
                         compiler_params=_cp(("parallel",)))(w2)
    return out.reshape(w.shape)


def _add2(a, b, out_dtype=f32):
    a2, b2 = _as2d(a), _as2d(b)
    rows, cols = a2.shape
    bm = _ew_rows(rows)

    def body(a_ref, b_ref, o_ref):
        o_ref[...] = (a_ref[...] + b_ref[...]).astype(o_ref.dtype)

    spec = pl.BlockSpec((bm, cols), lambda i: (i, 0))
    out = pl.pallas_call(body, name="add2", grid=(rows // bm,), in_specs=[spec, spec], out_specs=spec,
                         out_shape=jax.ShapeDtypeStruct((rows, cols), out_dtype),
                         compiler_params=_cp(("parallel",)))(a2, b2)
    return out.reshape(a.shape)


def _adamw(parts, w, m, v):
    n = parts.shape[0]
    w2, m2, v2 = _as2d(w), _as2d(m), _as2d(v)
    rows, cols = w2.shape
    p3 = parts.reshape(n, rows, cols)
    bm = _ew_rows(rows)
    if bm * cols * 4 > (1 << 20) and bm % 16 == 0:
        bm //= 2

    def body(p_ref, w_ref, m_ref, v_ref, g_ref, d_ref, nm_ref, nv_ref):
        g = p_ref[0]
        for k in range(1, n):
            g = g + p_ref[k]
        mn = ADAM_B1 * m_ref[...] + (1.0 - ADAM_B1) * g
        vn = ADAM_B2 * v_ref[...] + (1.0 - ADAM_B2) * (g * g)
        m_hat = mn / (1.0 - ADAM_B1 ** ADAM_STEP)
        v_hat = vn / (1.0 - ADAM_B2 ** ADAM_STEP)
        g_ref[...] = g
        d_ref[...] = -ADAM_LR * (m_hat / (jnp.sqrt(v_hat) + ADAM_EPS) + ADAM_WD * w_ref[...])
        nm_ref[...] = mn
        nv_ref[...] = vn

    spec = pl.BlockSpec((bm, cols), lambda i: (i, 0))
    outs = pl.pallas_call(
        body, name="adamw", grid=(rows // bm,),
        in_specs=[pl.BlockSpec((n, bm, cols), lambda i: (0, i, 0)), spec, spec, spec],
        out_specs=[spec] * 4, out_shape=[jax.ShapeDtypeStruct((rows, cols), f32)] * 4,
        compiler_params=_cp(("parallel",)),
    )(p3, w2, m2, v2)
    return tuple(o.reshape(w.shape) for o in outs)


ANY = pl.BlockSpec(memory_space=pl.ANY)


def _my_pos():
    return lax.axis_index("x"), lax.axis_index("y"), lax.axis_index("c")


def _allgather8(name, x):
    def body(x_ref, out_ref, send_sems, recv_sems):
        mx, my, mc = _my_pos()
        me, sibling = (mx, my, mc), (mx, my, 1 - mc)
        chips = [(1 - mx, my), (mx, 1 - my), (1 - mx, 1 - my)]

        def slot(px, py, pc):
            return out_ref.at[4 * px + 2 * py + pc]

        def copy(k, block, to, src=None):
            return pltpu.make_async_remote_copy(
                src_ref=slot(*block) if src is None else src, dst_ref=slot(*block),
                send_sem=send_sems.at[k], recv_sem=recv_sems.at[k], device_id=to, device_id_type=MESH)

        first = [copy(0, me, sibling, src=x_ref)]
        first += [copy(1 + j, me, (*chip, mc), src=x_ref) for j, chip in enumerate(chips)]
        for cp in first:
            cp.start()
        passed = [copy(4 + j, (*chip, mc), sibling) for j, chip in enumerate(chips)]
        for j, chip in enumerate(chips):
            copy(1 + j, (*chip, mc), me).wait_recv()
            passed[j].start()
        copy(0, sibling, me).wait_recv()
        for j, chip in enumerate(chips):
            copy(4 + j, (*chip, 1 - mc), me).wait_recv()
        for cp in first + passed:
            cp.wait_send()

    out = pl.pallas_call(
        body, name=name, in_specs=[ANY], out_specs=ANY,
        out_shape=jax.ShapeDtypeStruct((NDEV,) + x.shape, x.dtype),
        scratch_shapes=[pltpu.SemaphoreType.DMA((7,)), pltpu.SemaphoreType.DMA((7,))],
    )(x)
    dev = 4 * lax.axis_index("x") + 2 * lax.axis_index("y") + lax.axis_index("c")
    return lax.dynamic_update_index_in_dim(out, x, dev, 0)


HALF = DEPTH // 2


def _chip_gather(arrs):
    n = len(arrs)

    def body(*refs):
        ins, outs = refs[:n], refs[n:2 * n]
        send_sems, recv_sems = refs[2 * n:]
        mx, my, mc = _my_pos()
        chips = [(1 - mx, my), (mx, 1 - my), (1 - mx, 1 - my)]
        mine = 2 * mx + my
        my_half = pl.ds(HALF * mc, HALF)
        sib_half = pl.ds(HALF * (1 - mc), HALF)

        def ici(a, j, src_chip, half):
            px, py = chips[j]
            return pltpu.make_async_remote_copy(
                src_ref=ins[a].at[half], dst_ref=outs[a].at[src_chip, half],
                send_sem=send_sems.at[3 * a + j], recv_sem=recv_sems.at[3 * a + j],
                device_id=(px, py, mc), device_id_type=MESH)

        def d2d(a, j, half):
            px, py = chips[j]
            blk = outs[a].at[2 * px + py, half]
            return pltpu.make_async_remote_copy(
                src_ref=blk, dst_ref=blk,
                send_sem=send_sems.at[3 * n + 3 * a + j], recv_sem=recv_sems.at[3 * n + 3 * a + j],
                device_id=(mx, my, 1 - mc), device_id_type=MESH)

        sends = [ici(a, j, mine, my_half) for a in range(n) for j in range(3)]
        for cp in sends:
            cp.start()
        passed = []
        for a in range(n):
            for j, (px, py) in enumerate(chips):
                ici(a, j, 2 * px + py, my_half).wait_recv()
                fwd = d2d(a, j, my_half)
                fwd.start()
                passed.append(fwd)
        for a in range(n):
            for j in range(3):
                d2d(a, j, sib_half).wait_recv()
        for cp in sends + passed:
            cp.wait_send()

    outs = pl.pallas_call(
        body, name="chip_gather", in_specs=[ANY] * n, out_specs=[ANY] * n,
        out_shape=[jax.ShapeDtypeStruct((NCHIP,) + a.shape, a.dtype) for a in arrs],
        scratch_shapes=[pltpu.SemaphoreType.DMA((6 * n,)), pltpu.SemaphoreType.DMA((6 * n,))],
    )(*arrs)
    mine = 2 * lax.axis_index("x") + lax.axis_index("y")
    return [lax.dynamic_update_index_in_dim(o, a, mine, 0) for o, a in zip(outs, arrs)]


def _sibling_half_swap(arrs):
    n = len(arrs)

    def body(*refs):
        ins, outs = refs[:n], refs[n:2 * n]
        send_sems, recv_sems = refs[2 * n:]
        mx, my, mc = _my_pos()
        sib_half = pl.ds(HALF * (1 - mc), HALF)
        cps = [pltpu.make_async_remote_copy(
            src_ref=ins[a].at[j, sib_half], dst_ref=outs[a].at[j],
            send_sem=send_sems.at[NCHIP * a + j], recv_sem=recv_sems.at[NCHIP * a + j],
            device_id=(mx, my, 1 - mc), device_id_type=MESH) for a in range(n) for j in range(NCHIP)]
        for cp in cps:
            cp.start()
        for cp in cps:
            cp.wait()

    return pl.pallas_call(
        body, name="sibling_half_swap", in_specs=[ANY] * n, out_specs=[ANY] * n,
        out_shape=[jax.ShapeDtypeStruct((NCHIP, HALF) + a.shape[2:], a.dtype) for a in arrs],
        scratch_shapes=[pltpu.SemaphoreType.DMA((NCHIP * n,)), pltpu.SemaphoreType.DMA((NCHIP * n,))],
    )(*arrs)


def _sibling_merge_halves(arrs):
    n = len(arrs)

    def body(*refs):
        ins, outs = refs[:n], refs[n:2 * n]
        send_sems, recv_sems = refs[2 * n:]
        mx, my, mc = _my_pos()
        my_half = pl.ds(HALF * mc, HALF)
        cps = [pltpu.make_async_remote_copy(
            src_ref=ins[a], dst_ref=outs[a].at[my_half],
            send_sem=send_sems.at[a], recv_sem=recv_sems.at[a],
            device_id=(mx, my, 1 - mc), device_id_type=MESH) for a in range(n)]
        for cp in cps:
            cp.start()
        for cp in cps:
            cp.wait()

    outs = pl.pallas_call(
        body, name="sibling_merge_halves", in_specs=[ANY] * n, out_specs=[ANY] * n,
        out_shape=[jax.ShapeDtypeStruct((DEPTH,) + a.shape[1:], a.dtype) for a in arrs],
        scratch_shapes=[pltpu.SemaphoreType.DMA((n,)), pltpu.SemaphoreType.DMA((n,))],
    )(*arrs)
    start = HALF * lax.axis_index("c")
    return [lax.dynamic_update_slice_in_dim(o, a, start, 0) for o, a in zip(outs, arrs)]


def _sum_parts(parts):
    n = parts.shape[0]
    p3 = parts.reshape(n, -1, parts.shape[-1])
    rows, cols = p3.shape[1:]
    bm = _ew_rows(rows)
    if bm * cols * 4 > (1 << 20) and bm % 16 == 0:
        bm //= 2

    def body(p_ref, o_ref):
        g = p_ref[0].astype(f32)
        for k in range(1, n):
            g = g + p_ref[k].astype(f32)
        o_ref[...] = g

    out = pl.pallas_call(
        body, name="sum_parts", grid=(rows // bm,),
        in_specs=[pl.BlockSpec((n, bm, cols), lambda i: (0, i, 0))],
        out_specs=pl.BlockSpec((bm, cols), lambda i: (i, 0)),
        out_shape=jax.ShapeDtypeStruct((rows, cols), f32),
        compiler_params=_cp(("parallel",)),
    )(p3)
    return out.reshape(parts.shape[1:])


def _chip_scatter(arrs):
    n = len(arrs)

    def body(*refs):
        ins, outs = refs[:n], refs[n:2 * n]
        send_sems, recv_sems = refs[2 * n:]
        mx, my, mc = _my_pos()
        chips = [(1 - mx, my), (mx, 1 - my), (1 - mx, 1 - my)]
        mine = 2 * mx + my
        sends = []
        for a in range(n):
            for j, (px, py) in enumerate(chips):
                sends.append(pltpu.make_async_remote_copy(
                    src_ref=ins[a].at[2 * px + py], dst_ref=outs[a].at[mine],
                    send_sem=send_sems.at[3 * a + j], recv_sem=recv_sems.at[3 * a + j],
                    device_id=(px, py, mc), device_id_type=MESH))
        for cp in sends:
            cp.start()
        for a in range(n):
            for j, (px, py) in enumerate(chips):
                pltpu.make_async_remote_copy(
                    src_ref=ins[a].at[mine], dst_ref=outs[a].at[2 * px + py],
                    send_sem=send_sems.at[3 * a + j], recv_sem=recv_sems.at[3 * a + j],
                    device_id=(px, py, mc), device_id_type=MESH).wait_recv()
        for cp in sends:
            cp.wait_send()

    outs = pl.pallas_call(
        body, name="chip_scatter", in_specs=[ANY] * n, out_specs=[ANY] * n,
        out_shape=[jax.ShapeDtypeStruct(a.shape, a.dtype) for a in arrs],
        scratch_shapes=[pltpu.SemaphoreType.DMA((3 * n,)), pltpu.SemaphoreType.DMA((3 * n,))],
    )(*arrs)
    mine = 2 * lax.axis_index("x") + lax.axis_index("y")
    return [lax.dynamic_update_index_in_dim(o, lax.dynamic_index_in_dim(a, mine, 0, keepdims=False), mine, 0)
            for o, a in zip(outs, arrs)]


def _layer_fwd(x, res, res_gate, wl, mod):
    L = x.shape[0]
    bm = _row_block(L, 512)
    nb = L // bm
    sh1, sc1, g1, sh2, sc2, g2 = [mod[k * D:(k + 1) * D][None, :] for k in range(NMOD)]
    sv = {}
    xin, h = _rmsmod_fwd(x, res, res_gate, wl["norm1_g"], sh1, sc1)
    if xin is None:
        xin = x
    p = _mm("mm_in", h, wl["w_in"], dims=NN, grid=(nb, 4, 1),
            a_block=(bm, D), a_map=lambda i, j, k: (i, 0),
            b_block=(None, D, 512), b_map=lambda i, j, k: (j, 0, 0),
            out_shape=(L, INW), o_block=(bm, 512), o_map=lambda i, j, k: (i, j))
    qkv = _qknorm_fwd(p, wl["gqk"])
    oa = _attn_fwd(qkv)
    os_, s_re, s_im = _s5_fwd(p, wl["abar_re"], wl["abar_im"], wl["bbr"], wl["bbi"], wl["ccr"], wl["cci"],
                              wl["dsk"], wl["gw"], wl["gb"])
    o = _outnorm_fwd(oa, os_, wl["attn_out_g"], wl["ssm_out_g"])
    proj = _mm("mm_out", o, wl["w_out"], dims=NN, grid=(nb, 2, 1),
               a_block=(bm, D), a_map=lambda i, j, k: (i, 0),
               b_block=(D, 512), b_map=lambda i, j, k: (0, j),
               out_shape=(L, D), o_block=(bm, 512), o_map=lambda i, j, k: (i, j))
    x1, h2 = _rmsmod_fwd(xin, proj, g1, wl["norm2_g"], sh2, sc2)
    up_pre = _mm("mm_up", h2, wl["w_up"], dims=NN, grid=(nb, 4, 1),
                 a_block=(bm, D), a_map=lambda i, j, k: (i, 0),
                 b_block=(None, D, 1408), b_map=lambda i, j, k: (j, 0, 0),
                 out_shape=(L, DUP), o_block=(bm, 1408), o_map=lambda i, j, k: (i, j))
    a = _convglu_fwd(up_pre, wl["conv_w"], wl["conv_b"])
    down = _mm("mm_down", a, wl["w_down"], dims=NN, grid=(nb, 2, 1),
               a_block=(bm, DFF), a_map=lambda i, j, k: (i, 0),
               b_block=(DFF, 512), b_map=lambda i, j, k: (0, j),
               out_shape=(L, D), o_block=(bm, 512), o_map=lambda i, j, k: (i, j))
    sv.update(xin=xin, h=h, p=p, qkv=qkv, oa=oa, os=os_, s_re=s_re, s_im=s_im, o=o, proj=proj,
              x1=x1, h2=h2, up_pre=up_pre, a=a, down=down,
              sc1=sc1, g1=g1, sc2=sc2, g2=g2)
    return x1, down, g2, sv


def _layer_bwd(dx_out, wl, sv):
    L = dx_out.shape[0]
    bm = _row_block(L, 512)
    nb = L // bm
    bk = _row_block(L, 512)
    nk = L // bk
    g = {}
    dd, dg2 = _gate_bwd(dx_out, sv["down"], sv["g2"])
    da = _mm("mm_down_dx", dd, wl["w_down"], dims=NT, grid=(nb, 2, 1),
             a_block=(bm, D), a_map=lambda i, j, k: (i, 0),
             b_block=(1408, D), b_map=lambda i, j, k: (j, 0),
             out_shape=(L, DFF), o_block=(bm, 1408), o_map=lambda i, j, k: (i, j))
    g["w_down"] = _mm("mm_down_dw", sv["a"], dd, dims=TN, grid=(2, 2, nk),
                      a_block=(bk, 1408), a_map=lambda i, j, k: (k, i),
                      b_block=(bk, 512), b_map=lambda i, j, k: (k, j),
                      out_shape=(DFF, D), o_block=(1408, 512), o_map=lambda i, j, k: (i, j))
    dup, g["conv_w"], g["conv_b"] = _convglu_bwd(da, sv["up_pre"], wl["conv_w"], wl["conv_b"])
    dh2 = _mm("mm_up_dx", dup, wl["w_up"], dims=NT, grid=(nb, 1, 4),
              a_block=(None, bm, 1408), a_map=lambda i, j, k: (k // 2, i, k % 2),
              b_block=(None, D, 1408), b_map=lambda i, j, k: (k, 0, 0),
              out_shape=(L, D), o_block=(bm, D), o_map=lambda i, j, k: (i, 0))
    g["w_up"] = _mm("mm_up_dw", sv["h2"], dup, dims=TN, grid=(2, 4, nk),
                    a_block=(bk, 512), a_map=lambda i, j, k: (k, i),
                    b_block=(None, bk, 1408), b_map=lambda i, j, k: (j // 2, k, j % 2),
                    out_shape=(NCHIP, D, 1408), o_block=(None, 512, 1408), o_map=lambda i, j, k: (j, i, 0))
    dx1, dsh2, dsc2, g["norm2_g"] = _rmsmod_bwd(dh2, sv["x1"], wl["norm2_g"], sv["sc2"], dx_out)
    dproj, dg1 = _gate_bwd(dx1, sv["proj"], sv["g1"])
    do = _mm("mm_out_dx", dproj, wl["w_out"], dims=NT, grid=(nb, 2, 1),
             a_block=(bm, D), a_map=lambda i, j, k: (i, 0),
             b_block=(512, D), b_map=lambda i, j, k: (j, 0),
             out_shape=(L, D), o_block=(bm, 512), o_map=lambda i, j, k: (i, j))
    g["w_out"] = _mm("mm_out_dw", sv["o"], dproj, dims=TN, grid=(2, 2, nk),
                     a_block=(bk, 512), a_map=lambda i, j, k: (k, i),
                     b_block=(bk, 512), b_map=lambda i, j, k: (k, j),
                     out_shape=(D, D), o_block=(512, 512), o_map=lambda i, j, k: (i, j))
    doa, dos, g["attn_out_g"], g["ssm_out_g"] = _outnorm_bwd(do, sv["oa"], sv["os"],
                                                             wl["attn_out_g"], wl["ssm_out_g"])
    dqn, dkn, dv = _attn_bwd(sv["qkv"], sv["oa"], doa)
    (du, dy16, yg16, dgl16, lam_re, lam_im, d_abar_re, d_abar_im, g["ssm_d"], g["glu_b"]) = _s5_bwd(
        dos, sv["p"], sv["s_re"], sv["s_im"], wl["abar_re"], wl["abar_im"], wl["bbr"], wl["bbi"],
        wl["ccr"], wl["cci"], wl["dsk"], wl["gw"], wl["gb"])

    d_ccr = _diag_tn("s5_dcc", sv["s_re"], dy16, NP)
    d_cci = _diag_tn("s5_dcc", sv["s_im"], dy16, NP)
    d_bbre = _diag_tn("s5_dbb", lam_re, sv["p"], NP, b_blk0=12)
    d_bbim = _diag_tn("s5_dbb", lam_im, sv["p"], NP, b_blk0=12)
    d_gw = _diag_tn("s5_dgw", yg16, dgl16, NH16)
    g["ssm_c_re"] = d_ccr.reshape(NG, NP, NH16).transpose(0, 2, 1)
    g["ssm_c_im"] = -d_cci.reshape(NG, NP, NH16).transpose(0, 2, 1)
    g["glu_w"] = d_gw.reshape(NG, NH16, NH16)
    d_ldt, d_ar, d_ai, d_br, d_bi = _s5_disc_bwd(
        wl["ldt_col"], wl["ar_col"], wl["ai_col"], wl["br_mat"], wl["bi_mat"],
        d_abar_re.reshape(NS, 1), d_abar_im.reshape(NS, 1), d_bbre, d_bbim)
    g["ssm_log_dt"] = d_ldt[:, 0]
    g["ssm_a_re"] = d_ar.reshape(NG, NP)
    g["ssm_a_im"] = d_ai.reshape(NG, NP)
    g["ssm_b_re"] = d_br.reshape(NG, NP, NH16)
    g["ssm_b_im"] = d_bi.reshape(NG, NP, NH16)
    dp, dgqk = _qknorm_bwd(dqn, dkn, dv, du, sv["p"], wl["gqk"])
    g["q_norm_g"] = dgqk[0, :HD]
    g["k_norm_g"] = dgqk[0, HD:2 * HD]
    dh = _mm("mm_in_dx", dp, wl["w_in"], dims=NT, grid=(nb, 1, 4),
             a_block=(bm, 512), a_map=lambda i, j, k: (i, k),
             b_block=(None, D, 512), b_map=lambda i, j, k: (k, 0, 0),
             out_shape=(L, D), o_block=(bm, D), o_map=lambda i, j, k: (i, 0))
    g["w_in"] = _mm("mm_in_dw", sv["h"], dp, dims=TN, grid=(2, 4, nk),
                    a_block=(bk, 512), a_map=lambda i, j, k: (k, i),
                    b_block=(bk, 512), b_map=lambda i, j, k: (k, j),
                    out_shape=(NCHIP, D, 512), o_block=(None, 512, 512), o_map=lambda i, j, k: (j, i, 0))
    dx_in, dsh1, dsc1, g["norm1_g"] = _rmsmod_bwd(dh, sv["xin"], wl["norm1_g"], sv["sc1"], dx1)
    g["dmod"] = jnp.concatenate([dsh1, dsc1, dg1, dsh2, dsc2, dg2], axis=1)[0]
    return dx_in, g


def _prep_layer(l, big, small):
    wl = {}
    wl["w_in"] = big["w_in"][:, l]
    wl["w_up"] = big["w_up"][:, l]
    wl["w_out"] = big["w_out"][:, l].reshape(D, D)
    wl["w_down"] = big["w_down"][:, l].reshape(DFF, D)
    wl["conv_w"] = big["conv_w"][:, l].transpose(1, 0, 2).reshape(3, DUP)
    wl["conv_b"] = small["ffn_conv_b"][l][None, :]
    wl["norm1_g"] = small["norm1_g"][l][None, :]
    wl["norm2_g"] = small["norm2_g"][l][None, :]
    wl["attn_out_g"] = small["attn_out_g"][l][None, :]
    wl["ssm_out_g"] = small["ssm_out_g"][l][None, :]
    wl["gqk"] = jnp.concatenate([jnp.tile(small["q_norm_g"][l], 8), jnp.tile(small["k_norm_g"][l], 8)])[None, :]
    wl["ldt_col"] = jnp.repeat(small["ssm_log_dt"][l], NP)[:, None]
    wl["ar_col"] = small["ssm_a_re"][l].reshape(NS, 1)
    wl["ai_col"] = small["ssm_a_im"][l].reshape(NS, 1)
    wl["br_mat"] = small["ssm_b_re"][l].reshape(NS, NH16)
    wl["bi_mat"] = small["ssm_b_im"][l].reshape(NS, NH16)
    abar_re, abar_im, bb_re, bb_im = _s5_disc_fwd(wl["ldt_col"], wl["ar_col"], wl["ai_col"],
                                                  wl["br_mat"], wl["bi_mat"])
    wl["abar_re"] = abar_re.reshape(1, NS)
    wl["abar_im"] = abar_im.reshape(1, NS)
    wl["bbr"] = _block_diag(bb_re.reshape(NG, NP, NH16).transpose(0, 2, 1)).astype(bf16)
    wl["bbi"] = _block_diag(bb_im.reshape(NG, NP, NH16).transpose(0, 2, 1)).astype(bf16)
    wl["ccr"] = _block_diag(small["ssm_c_re"][l].transpose(0, 2, 1)).astype(bf16)
    wl["cci"] = _block_diag(small["ssm_c_im"][l].transpose(0, 2, 1)).astype(bf16)
    wl["gw"] = _block_diag(small["glu_w"][l]).astype(bf16)
    wl["dsk"] = small["ssm_d"][l].reshape(1, SSMW)
    wl["gb"] = small["glu_b"][l].reshape(1, SSMW)
    return wl


def _local_step(x, target, mods, big, small):
    wls = [_prep_layer(l, big, small) for l in range(DEPTH)]
    saved = []
    cur, res, res_gate = x, None, None
    for l in range(DEPTH):
        cur, res, res_gate, sv = _layer_fwd(cur, res, res_gate, wls[l], mods[l])
        saved.append(sv)
    loss, dx = _loss_fwd_bwd(cur, res, res_gate, target)
    grads = [None] * DEPTH
    for l in reversed(range(DEPTH)):
        dx, grads[l] = _layer_bwd(dx, wls[l], saved[l])
    return loss, dx, grads


BIG = ("w_in", "w_out", "ffn_w_up", "ffn_w_down")
SMALL = ("ada_b", "norm1_g", "q_norm_g", "k_norm_g", "ssm_a_re", "ssm_a_im", "ssm_log_dt",
         "ssm_b_re", "ssm_b_im", "ssm_c_re", "ssm_c_im", "ssm_d", "glu_w", "glu_b",
         "attn_out_g", "ssm_out_g", "norm2_g", "ffn_conv_b")
NAMES = ("ada_w", "ada_b", "norm1_g", "w_in", "q_norm_g", "k_norm_g", "ssm_a_re", "ssm_a_im",
         "ssm_log_dt", "ssm_b_re", "ssm_b_im", "ssm_c_re", "ssm_c_im", "ssm_d", "glu_w", "glu_b",
         "attn_out_g", "ssm_out_g", "w_out", "norm2_g", "ffn_w_up", "ffn_conv_w", "ffn_conv_b",
         "ffn_w_down")
PACK_COLS = 1024


def _pack(arrs):
    flat = jnp.concatenate([a.reshape(-1) for a in arrs])
    rows = -(-flat.shape[0] // PACK_COLS)
    rows = -(-rows // 8) * 8
    flat = jnp.pad(flat, (0, rows * PACK_COLS - flat.shape[0]))
    return flat.reshape(rows, PACK_COLS)


def _unpack(packed, shapes):
    flat = packed.reshape(-1)
    out, off = [], 0
    for s in shapes:
        n = math.prod(s)
        out.append(flat[off:off + n].reshape(s))
        off += n
    return out


def kernel(x, c, ada_w, ada_b, norm1_g, w_in, q_norm_g, k_norm_g, ssm_a_re, ssm_a_im, ssm_log_dt, ssm_b_re, ssm_b_im, ssm_c_re, ssm_c_im, ssm_d, glu_w, glu_b, attn_out_g, ssm_out_g, w_out, norm2_g, ffn_w_up, ffn_conv_w, ffn_conv_b, ffn_w_down, loss_target, m_ada_w, m_ada_b, m_norm1_g, m_w_in, m_q_norm_g, m_k_norm_g, m_ssm_a_re, m_ssm_a_im, m_ssm_log_dt, m_ssm_b_re, m_ssm_b_im, m_ssm_c_re, m_ssm_c_im, m_ssm_d, m_glu_w, m_glu_b, m_attn_out_g, m_ssm_out_g, m_w_out, m_norm2_g, m_ffn_w_up, m_ffn_conv_w, m_ffn_conv_b, m_ffn_w_down, v_ada_w, v_ada_b, v_norm1_g, v_w_in, v_q_norm_g, v_k_norm_g, v_ssm_a_re, v_ssm_a_im, v_ssm_log_dt, v_ssm_b_re, v_ssm_b_im, v_ssm_c_re, v_ssm_c_im, v_ssm_d, v_glu_w, v_glu_b, v_attn_out_g, v_ssm_out_g, v_w_out, v_norm2_g, v_ffn_w_up, v_ffn_conv_w, v_ffn_conv_b, v_ffn_w_down):
    env = dict(locals())
    w = {n: env[n] for n in NAMES}
    m = {n: env["m_" + n] for n in NAMES}
    v = {n: env["v_" + n] for n in NAMES}
    mx, my, mc = _my_pos()
    chip = 2 * mx + my
    dev = 4 * mx + 2 * my + mc
    xl = x[0]
    tl = loss_target[0]

    c_all = _allgather8("gather_c", jnp.pad(c, ((0, 7), (0, 0))))[:, 0, :]
    c16 = jnp.pad(c_all, ((0, 8), (0, 0)))
    nloc = NMOD * D // NCHIP
    ada_b_loc = lax.dynamic_slice(ada_b, (0, chip * nloc), (DEPTH, nloc))[:, None, :]
    mod_loc = _mod_fwd(c16, ada_w, ada_b_loc)
    mod_g = _allgather8("gather_mod", mod_loc.reshape(DEPTH * 16, nloc))
    mod_all = mod_g[0::2].reshape(NCHIP, DEPTH, 16, nloc).transpose(1, 2, 0, 3).reshape(DEPTH, 16, NMOD * D)
    mods = lax.dynamic_index_in_dim(mod_all, dev, axis=1, keepdims=False)

    gathered = _chip_gather([_cast_bf16(w["w_in"]), _cast_bf16(w["w_out"]), _cast_bf16(w["ffn_w_up"]),
                             _cast_bf16(w["ffn_w_down"]), w["ffn_conv_w"]])
    big = dict(zip(("w_in", "w_out", "w_up", "w_down", "conv_w"), gathered))
    small = {n: w[n] for n in SMALL}

    loss_loc, grad_x, grads = _local_step(xl, tl, mods, big, small)
    loss = lax.psum(loss_loc, ("x", "y", "c"))

    def stacked(name):
        return jnp.stack([grads[l][name] for l in range(DEPTH)], axis=1)

    loc = [stacked("w_in"),
           jnp.stack([grads[l]["w_out"].reshape(NCHIP, D // NCHIP, D) for l in range(DEPTH)], axis=1),
           stacked("w_up"),
           jnp.stack([grads[l]["w_down"].reshape(NCHIP, DFF // NCHIP, D) for l in range(DEPTH)], axis=1)]
    recv = _sibling_half_swap(loc)
    mine_half = [lax.dynamic_slice_in_dim(a, HALF * mc, HALF, axis=1) for a in loc]
    chip_sum = [_add2(a, b, bf16) for a, b in zip(mine_half, recv)]
    parts = _chip_scatter(chip_sum)
    g_full = _sibling_merge_halves([_sum_parts(p) for p in parts])
    outs = {}
    for name, g_big in zip(BIG, g_full):
        outs[name] = _adamw(g_big[None], w[name], m[name], v[name])

    small_names = SMALL[1:] + ("ffn_conv_w_full",)
    small_grads = []
    for n in SMALL[1:]:
        key = {"ffn_conv_b": "conv_b"}.get(n, n)
        small_grads.append(jnp.stack([grads[l][key].reshape(w[n].shape[1:]) for l in range(DEPTH)]))
    small_grads.append(jnp.stack([grads[l]["conv_w"] for l in range(DEPTH)]))
    dmod = jnp.stack([grads[l]["dmod"] for l in range(DEPTH)])
    packed = _pack([dmod] + small_grads)
    allp = _allgather8("gather_small", packed)
    shapes = [dmod.shape] + [a.shape for a in small_grads]
    dmod_all = allp[:, :DEPTH * NMOD, :].reshape(NDEV, DEPTH, NMOD * D)
    dmod_loc = lax.dynamic_slice(dmod_all, (0, 0, chip * nloc), (NDEV, DEPTH, nloc)).transpose(1, 0, 2)
    g_ada = _ada_grad(c16, jnp.pad(dmod_loc, ((0, 0), (0, 8), (0, 0))))
    outs["ada_w"] = _adamw(g_ada[None], w["ada_w"], m["ada_w"], v["ada_w"])
    conv_w_shape = (DEPTH, 3, DUP)
    w_small = _pack([w[n] for n in SMALL] + [jnp.zeros(conv_w_shape, f32)])
    m_small = _pack([m[n] for n in SMALL] + [jnp.zeros(conv_w_shape, f32)])
    v_small = _pack([v[n] for n in SMALL] + [jnp.ones(conv_w_shape, f32)])
    res_small = _adamw(allp, w_small, m_small, v_small)
    unpacked = [_unpack(r, shapes) for r in res_small]
    for i, n in enumerate(SMALL):
        outs[n] = tuple(unpacked[k][i] for k in range(4))
    g_conv_full = unpacked[0][len(SMALL)]
    ncw = DUP // NCHIP
    g_conv = lax.dynamic_slice(g_conv_full, (0, 0, chip * ncw), (DEPTH, 3, ncw))
    outs["ffn_conv_w"] = _adamw(g_conv[None], w["ffn_conv_w"], m["ffn_conv_w"], v["ffn_conv_w"])

    result = [loss, grad_x[None]]
    for k in range(4):
        result += [outs[n][k] for n in NAMES]
    return tuple(result)
```

```python
import functools
import math

import jax
import jax.numpy as jnp
from jax import lax
from jax.experimental import pallas as pl
from jax.experimental.pallas import tpu as pltpu

f32 = jnp.float32
bf16 = jnp.bfloat16
_MXU = jnp.bfloat16

D = 1024
ATT = 512
SSMW = 512
HD = 64
NG = 32
NP = 64
NH16 = 16
NS = NG * NP
INW = 2048
DFF = 2816
DUP = 2 * DFF
NMOD = 6
DEPTH = 4
EPS = 1e-6
NCHIP = 4
NDEV = 8

ADAM_LR = 0.001
ADAM_B1 = 0.9
ADAM_B2 = 0.999
ADAM_EPS = 1e-08
ADAM_WD = 0.01
ADAM_STEP = 10

VMEM_LIMIT = 56 * 1024 * 1024
MESH = pl.DeviceIdType.MESH

NN = (((1,), (0,)), ((), ()))
NT = (((1,), (1,)), ((), ()))
TN = (((0,), (0,)), ((), ()))


def _cp(sem=None):
    if sem is None:
        return pltpu.CompilerParams(vmem_limit_bytes=VMEM_LIMIT)
    return pltpu.CompilerParams(dimension_semantics=sem, vmem_limit_bytes=VMEM_LIMIT)


def _dot(a, b, dims=NN):
    return lax.dot_general(a.astype(_MXU), b.astype(_MXU), dims, preferred_element_type=f32)


def _dot_exact(x, m):
    hi = x.astype(bf16)
    r1 = x - hi.astype(f32)
    mid = r1.astype(bf16)
    lo = (r1 - mid.astype(f32)).astype(bf16)
    d = lambda p: lax.dot_general(p, m, NN, preferred_element_type=f32)
    return d(hi) + d(mid) + d(lo)


def _gelu(x):
    c = math.sqrt(2.0 / math.pi)
    return 0.5 * x * (1.0 + jnp.tanh(c * (x + 0.044715 * (x * x * x))))


def _gelu_grad(x):
    c = math.sqrt(2.0 / math.pi)
    t = jnp.tanh(c * (x + 0.044715 * (x * x * x)))
    return 0.5 * (1.0 + t) + 0.5 * x * (1.0 - t * t) * c * (1.0 + 3.0 * 0.044715 * (x * x))


def _sigmoid(x):
    return 1.0 / (1.0 + jnp.exp(-x))


def _mm(name, a, b, *, dims, grid, a_block, a_map, b_block, b_map, out_shape, o_block, o_map,
        out_dtype=f32):
    nk = grid[2]
    acc_shape = tuple(s for s in o_block if s is not None)

    def body(a_ref, b_ref, o_ref, acc_ref):
        k = pl.program_id(2)
        part = _dot(a_ref[...], b_ref[...], dims)
        if nk == 1:
            o_ref[...] = part.astype(o_ref.dtype)
        else:
            @pl.when(k == 0)
            def _():
                acc_ref[...] = part

            @pl.when(k > 0)
            def _():
                acc_ref[...] += part

            @pl.when(k == nk - 1)
            def _():
                o_ref[...] = acc_ref[...].astype(o_ref.dtype)

    return pl.pallas_call(
        body, name=name, grid=grid,
        in_specs=[pl.BlockSpec(a_block, a_map), pl.BlockSpec(b_block, b_map)],
        out_specs=pl.BlockSpec(o_block, o_map),
        out_shape=jax.ShapeDtypeStruct(out_shape, out_dtype),
        scratch_shapes=[pltpu.VMEM(acc_shape if nk > 1 else (8, 128), f32)],
        compiler_params=_cp(("parallel", "parallel", "arbitrary")),
    )(a, b)


def _row_block(L, want):
    return want if L % want == 0 else L


def _rmsmod_fwd(x, res, gate, g, sh, sc):
    L = x.shape[0]
    bm = _row_block(L, 256)
    with_res = res is not None

    def body(*refs):
        if with_res:
            x_ref, r_ref, gt_ref, g_ref, sh_ref, sc_ref, xo_ref, h_ref = refs
            xin = x_ref[...] + gt_ref[...] * r_ref[...]
            xo_ref[...] = xin
        else:
            x_ref, g_ref, sh_ref, sc_ref, h_ref = refs
            xin = x_ref[...]
        inv = lax.rsqrt(jnp.mean(xin * xin, axis=-1, keepdims=True) + EPS)
        xn = xin * inv * g_ref[...]
        h_ref[...] = (xn * (1.0 + sc_ref[...]) + sh_ref[...]).astype(h_ref.dtype)

    row = pl.BlockSpec((bm, D), lambda i: (i, 0))
    vec = pl.BlockSpec((1, D), lambda i: (0, 0))
    if with_res:
        return pl.pallas_call(
            body, name="rmsmod_res_fwd", grid=(L // bm,),
            in_specs=[row, row, vec, vec, vec, vec], out_specs=[row, row],
            out_shape=[jax.ShapeDtypeStruct((L, D), f32), jax.ShapeDtypeStruct((L, D), bf16)],
            compiler_params=_cp(("parallel",)),
        )(x, res, gate, g, sh, sc)
    h = pl.pallas_call(
        body, name="rmsmod_fwd", grid=(L // bm,),
        in_specs=[row, vec, vec, vec], out_specs=row,
        out_shape=jax.ShapeDtypeStruct((L, D), bf16),
        compiler_params=_cp(("parallel",)),
    )(x, g, sh, sc)
    return None, h


def _rmsmod_bwd(dh, x, g, sc, dres):
    L = x.shape[0]
    bm = _row_block(L, 256)

    def body(dh_ref, x_ref, g_ref, sc_ref, dr_ref, dx_ref, dsh_ref, dsc_ref, dg_ref):
        i = pl.program_id(0)
        xv = x_ref[...]
        dhv = dh_ref[...]
        inv = lax.rsqrt(jnp.mean(xv * xv, axis=-1, keepdims=True) + EPS)
        xh = xv * inv
        gv = g_ref[...]
        xn = xh * gv
        dxn = dhv * (1.0 + sc_ref[...])
        dxh = dxn * gv
        dx_ref[...] = inv * (dxh - xh * jnp.mean(dxh * xh, axis=-1, keepdims=True)) + dr_ref[...]
        p_sh = jnp.sum(dhv, axis=0, keepdims=True)
        p_sc = jnp.sum(dhv * xn, axis=0, keepdims=True)
        p_g = jnp.sum(dxn * xh, axis=0, keepdims=True)

        @pl.when(i == 0)
        def _():
            dsh_ref[...] = p_sh
            dsc_ref[...] = p_sc
            dg_ref[...] = p_g

        @pl.when(i > 0)
        def _():
            dsh_ref[...] += p_sh
            dsc_ref[...] += p_sc
            dg_ref[...] += p_g

    row = pl.BlockSpec((bm, D), lambda i: (i, 0))
    vec = pl.BlockSpec((1, D), lambda i: (0, 0))
    return pl.pallas_call(
        body, name="rmsmod_bwd", grid=(L // bm,),
        in_specs=[row, row, vec, vec, row], out_specs=[row, vec, vec, vec],
        out_shape=[jax.ShapeDtypeStruct((L, D), f32)] + [jax.ShapeDtypeStruct((1, D), f32)] * 3,
        compiler_params=_cp(("arbitrary",)),
    )(dh, x, g, sc, dres)


def _gate_bwd(dx, y, gate):
    L = dx.shape[0]
    bm = _row_block(L, 256)

    def body(dx_ref, y_ref, gt_ref, dy_ref, dgt_ref):
        i = pl.program_id(0)
        dxv = dx_ref[...]
        dy_ref[...] = (gt_ref[...] * dxv).astype(dy_ref.dtype)
        part = jnp.sum(dxv * y_ref[...], axis=0, keepdims=True)

        @pl.when(i == 0)
        def _():
            dgt_ref[...] = part

        @pl.when(i > 0)
        def _():
            dgt_ref[...] += part

    row = pl.BlockSpec((bm, D), lambda i: (i, 0))
    vec = pl.BlockSpec((1, D), lambda i: (0, 0))
    return pl.pallas_call(
        body, name="gate_bwd", grid=(L // bm,),
        in_specs=[row, row, vec], out_specs=[row, vec],
        out_shape=[jax.ShapeDtypeStruct((L, D), bf16), jax.ShapeDtypeStruct((1, D), f32)],
        compiler_params=_cp(("arbitrary",)),
    )(dx, y, gate)


def _loss_fwd_bwd(x1, down, gate, target):
    L = x1.shape[0]
    bm = _row_block(L, 256)
    nsteps = L // bm

    def body(x_ref, d_ref, gt_ref, t_ref, dy_ref, loss_ref, acc_ref):
        i = pl.program_id(0)
        diff = x_ref[...] + gt_ref[...] * d_ref[...] - t_ref[...]
        dy_ref[...] = diff * (1.0 / D)
        part = jnp.sum(diff * diff, axis=0, keepdims=True)

        @pl.when(i == 0)
        def _():
            acc_ref[...] = part

        @pl.when(i > 0)
        def _():
            acc_ref[...] += part

        @pl.when(i == nsteps - 1)
        def _():
            tot = jnp.sum(acc_ref[...], axis=1, keepdims=True) * (0.5 / D)
            loss_ref[...] = jnp.broadcast_to(tot, (8, 128))

    row = pl.BlockSpec((bm, D), lambda i: (i, 0))
    vec = pl.BlockSpec((1, D), lambda i: (0, 0))
    dy, loss = pl.pallas_call(
        body, name="loss_fwd_bwd", grid=(nsteps,),
        in_specs=[row, row, vec, row],
        out_specs=[row, pl.BlockSpec((8, 128), lambda i: (0, 0))],
        out_shape=[jax.ShapeDtypeStruct((L, D), f32), jax.ShapeDtypeStruct((8, 128), f32)],
        scratch_shapes=[pltpu.VMEM((1, D), f32)],
        compiler_params=_cp(("arbitrary",)),
    )(x1, down, gate, target)
    return loss[0, 0], dy


def _head_mean_matrix():
    r = lax.broadcasted_iota(jnp.int32, (128, 128), 0) // HD
    c = lax.broadcasted_iota(jnp.int32, (128, 128), 1) // HD
    return jnp.where(r == c, 1.0 / HD, 0.0).astype(bf16)


def _qknorm_fwd(p, gqk):
    L = p.shape[0]
    bm = _row_block(L, 256)

    def body(p_ref, g_ref, o_ref):
        e = _head_mean_matrix()
        for c in range(8):
            sl = slice(128 * c, 128 * (c + 1))
            xv = p_ref[:, sl]
            inv = lax.rsqrt(_dot_exact(xv * xv, e) + EPS)
            yv = xv * inv * g_ref[:, sl]
            if c < 4:
                yv = yv * (1.0 / math.sqrt(HD))
            o_ref[:, sl] = yv.astype(o_ref.dtype)
        o_ref[:, 1024:1536] = p_ref[:, 1024:1536].astype(o_ref.dtype)

    return pl.pallas_call(
        body, name="qknorm_fwd", grid=(L // bm,),
        in_specs=[pl.BlockSpec((bm, 1536), lambda i: (i, 0)), pl.BlockSpec((1, 1024), lambda i: (0, 0))],
        out_specs=pl.BlockSpec((bm, 1536), lambda i: (i, 0)),
        out_shape=jax.ShapeDtypeStruct((L, 1536), bf16),
        compiler_params=_cp(("parallel",)),
    )(p, gqk)


def _qknorm_bwd(dqn, dkn, dval, du, p, gqk):
    L = p.shape[0]
    bm = _row_block(L, 256)
    nsteps = L // bm

    def body(dq_ref, dk_ref, dv_ref, du_ref, p_ref, g_ref, o_ref, dg_ref, acc_ref):
        i = pl.program_id(0)
        e = _head_mean_matrix()
        o_ref[:, 1024:1536] = dv_ref[...].astype(o_ref.dtype)
        o_ref[:, 1536:2048] = du_ref[...].astype(o_ref.dtype)
        for c in range(8):
            sl = slice(128 * c, 128 * (c + 1))
            xv = p_ref[:, sl]
            d_ref = dq_ref if c < 4 else dk_ref
            dv = d_ref[:, 128 * (c % 4):128 * (c % 4 + 1)]
            gv = g_ref[:, sl]
            inv = lax.rsqrt(_dot_exact(xv * xv, e) + EPS)
            xh = xv * inv
            dxh = dv * gv
            o_ref[:, sl] = (inv * (dxh - xh * _dot_exact(dxh * xh, e))).astype(o_ref.dtype)
            part = jnp.sum(dv * xh, axis=0, keepdims=True)

            @pl.when(i == 0)
            def _():
                acc_ref[:, sl] = part

            @pl.when(i > 0)
            def _():
                acc_ref[:, sl] += part

        @pl.when(i == nsteps - 1)
        def _():
            r = lax.broadcasted_iota(jnp.int32, (1024, 128), 0)
            col = lax.broadcasted_iota(jnp.int32, (1024, 128), 1)
            fold = jnp.where(col == (r // 512) * HD + r % HD, 1.0, 0.0).astype(bf16)
            dg_ref[...] = _dot_exact(jnp.broadcast_to(acc_ref[...], (8, 1024)), fold)

    half = pl.BlockSpec((bm, 512), lambda i: (i, 0))
    return pl.pallas_call(
        body, name="qknorm_bwd", grid=(nsteps,),
        in_specs=[half, half, half, half, pl.BlockSpec((bm, 1024), lambda i: (i, 0)),
                  pl.BlockSpec((1, 1024), lambda i: (0, 0))],
        out_specs=[pl.BlockSpec((bm, INW), lambda i: (i, 0)), pl.BlockSpec((8, 128), lambda i: (0, 0))],
        out_shape=[jax.ShapeDtypeStruct((L, INW), bf16), jax.ShapeDtypeStruct((8, 128), f32)],
        scratch_shapes=[pltpu.VMEM((1, 1024), f32)],
        compiler_params=_cp(("arbitrary",)),
    )(dqn, dkn, dval, du, p, gqk)


def _diag_tn(name, a, b, rpg, b_blk0=0):
    L = a.shape[0]
    aw = 8 * rpg
    bk = _row_block(L, 512)
    nk = L // bk

    def body(a_ref, b_ref, o_ref, acc_ref):
        k = pl.program_id(1)
        part = _dot(a_ref[...], b_ref[...], TN)

        @pl.when(k == 0)
        def _():
            acc_ref[...] = part

        @pl.when(k > 0)
        def _():
            acc_ref[...] += part

        @pl.when(k == nk - 1)
        def _():
            r = lax.broadcasted_iota(jnp.int32, (aw, 128), 0) // rpg
            c = lax.broadcasted_iota(jnp.int32, (aw, 128), 1) // NH16
            fr = lax.broadcasted_iota(jnp.int32, (128, NH16), 0) % NH16
            fc = lax.broadcasted_iota(jnp.int32, (128, NH16), 1)
            fold = jnp.where(fr == fc, 1.0, 0.0).astype(bf16)
            o_ref[...] = _dot_exact(jnp.where(r == c, acc_ref[...], 0.0), fold)

    return pl.pallas_call(
        body, name=name, grid=(4, nk),
        in_specs=[pl.BlockSpec((bk, aw), lambda i, k: (k, i)),
                  pl.BlockSpec((bk, 128), lambda i, k: (k, b_blk0 + i))],
        out_specs=pl.BlockSpec((aw, NH16), lambda i, k: (i, 0)),
        out_shape=jax.ShapeDtypeStruct((4 * aw, NH16), f32),
        scratch_shapes=[pltpu.VMEM((aw, 128), f32)],
        compiler_params=_cp(("parallel", "arbitrary")),
    )(a, b)


def _outnorm_fwd(oa, os_, ga, gs):
    L = oa.shape[0]
    bm = _row_block(L, 256)

    def body(a_ref, s_ref, ga_ref, gs_ref, o_ref):
        for x_ref, g_ref, off in ((a_ref, ga_ref, 0), (s_ref, gs_ref, 512)):
            xv = x_ref[...]
            inv = lax.rsqrt(jnp.mean(xv * xv, axis=-1, keepdims=True) + EPS)
            o_ref[:, off:off + 512] = (xv * inv * g_ref[...]).astype(o_ref.dtype)

    half = pl.BlockSpec((bm, 512), lambda i: (i, 0))
    vec = pl.BlockSpec((1, 512), lambda i: (0, 0))
    return pl.pallas_call(
        body, name="outnorm_fwd", grid=(L // bm,),
        in_specs=[half, half, vec, vec], out_specs=pl.BlockSpec((bm, D), lambda i: (i, 0)),
        out_shape=jax.ShapeDtypeStruct((L, D), bf16),
        compiler_params=_cp(("parallel",)),
    )(oa, os_, ga, gs)


def _outnorm_bwd(do, oa, os_, ga, gs):
    L = oa.shape[0]
    bm = _row_block(L, 256)

    def body(do_ref, a_ref, s_ref, ga_ref, gs_ref, da_ref, ds_ref, dga_ref, dgs_ref):
        i = pl.program_id(0)
        for x_ref, g_ref, dx_ref, dg_ref, off in ((a_ref, ga_ref, da_ref, dga_ref, 0),
                                                  (s_ref, gs_ref, ds_ref, dgs_ref, 512)):
            xv = x_ref[...]
            dv = do_ref[:, off:off + 512]
            inv = lax.rsqrt(jnp.mean(xv * xv, axis=-1, keepdims=True) + EPS)
            xh = xv * inv
            dxh = dv * g_ref[...]
            dx_ref[...] = inv * (dxh - xh * jnp.mean(dxh * xh, axis=-1, keepdims=True))
            part = jnp.sum(dv * xh, axis=0, keepdims=True)

            @pl.when(i == 0)
            def _():
                dg_ref[...] = part

            @pl.when(i > 0)
            def _():
                dg_ref[...] += part

    half = pl.BlockSpec((bm, 512), lambda i: (i, 0))
    vec = pl.BlockSpec((1, 512), lambda i: (0, 0))
    return pl.pallas_call(
        body, name="outnorm_bwd", grid=(L // bm,),
        in_specs=[pl.BlockSpec((bm, D), lambda i: (i, 0)), half, half, vec, vec],
        out_specs=[half, half, vec, vec],
        out_shape=[jax.ShapeDtypeStruct((L, 512), f32)] * 2 + [jax.ShapeDtypeStruct((1, 512), f32)] * 2,
        compiler_params=_cp(("arbitrary",)),
    )(do, oa, os_, ga, gs)


NCB = DFF // 128


def _shift_down(x, s, row):
    return jnp.where(row >= s, pltpu.roll(x, s, 0), 0.0)


def _shift_up(x, s, row, L):
    return jnp.where(row < L - s, pltpu.roll(x, L - s, 0), 0.0)


def _conv(x, w_ref, b_ref, row):
    out = b_ref[...] + _shift_down(x, 2, row) * w_ref[0:1, :]
    out = out + _shift_down(x, 1, row) * w_ref[1:2, :]
    return out + x * w_ref[2:3, :]


def _convglu_fwd(up_pre, cw, cb):
    L = up_pre.shape[0]

    def body(v_ref, g_ref, wv_ref, wg_ref, bv_ref, bg_ref, a_ref):
        row = lax.broadcasted_iota(jnp.int32, (L, 128), 0)
        val = _conv(v_ref[...], wv_ref, bv_ref, row)
        gate = _conv(g_ref[...], wg_ref, bg_ref, row)
        a_ref[...] = (_gelu(gate) * val).astype(a_ref.dtype)

    col = lambda off: pl.BlockSpec((L, 128), lambda j: (0, j + off))
    wsp = lambda off: pl.BlockSpec((3, 128), lambda j: (0, j + off))
    bsp = lambda off: pl.BlockSpec((1, 128), lambda j: (0, j + off))
    return pl.pallas_call(
        body, name="convglu_fwd", grid=(NCB,),
        in_specs=[col(0), col(NCB), wsp(0), wsp(NCB), bsp(0), bsp(NCB)],
        out_specs=pl.BlockSpec((L, 128), lambda j: (0, j)),
        out_shape=jax.ShapeDtypeStruct((L, DFF), bf16),
        compiler_params=_cp(("parallel",)),
    )(up_pre, up_pre, cw, cw, cb, cb)


def _convglu_bwd(da, up_pre, cw, cb):
    L = up_pre.shape[0]

    def body(da_ref, v_ref, g_ref, wv_ref, wg_ref, bv_ref, bg_ref, dx_ref, dw_ref, db_ref):
        row = lax.broadcasted_iota(jnp.int32, (L, 128), 0)
        dav = da_ref[...]
        u = []
        for x_ref, w_ref, b_ref in ((v_ref, wv_ref, bv_ref), (g_ref, wg_ref, bg_ref)):
            u.append(_conv(x_ref[...], w_ref, b_ref, row))
        val, gate = u
        c = math.sqrt(2.0 / math.pi)
        t = jnp.tanh(c * (gate + 0.044715 * (gate * gate * gate)))
        d_val = dav * (0.5 * gate * (1.0 + t))
        d_gate = dav * val * (0.5 * (1.0 + t)
                              + 0.5 * gate * (1.0 - t * t) * c * (1.0 + 3.0 * 0.044715 * (gate * gate)))
        for k, (d, x_ref, w_ref) in enumerate(((d_val, v_ref, wv_ref), (d_gate, g_ref, wg_ref))):
            dx = d * w_ref[2:3, :] + _shift_up(d, 1, row, L) * w_ref[1:2, :]
            dx_ref[k] = (dx + _shift_up(d, 2, row, L) * w_ref[0:1, :]).astype(dx_ref.dtype)
            xv = x_ref[...]
            dw_ref[k, 0:1, :] = jnp.sum(d * _shift_down(xv, 2, row), axis=0, keepdims=True)
            dw_ref[k, 1:2, :] = jnp.sum(d * _shift_down(xv, 1, row), axis=0, keepdims=True)
            dw_ref[k, 2:3, :] = jnp.sum(d * xv, axis=0, keepdims=True)
            db_ref[k] = jnp.sum(d, axis=0, keepdims=True)

    col = lambda off: pl.BlockSpec((L, 128), lambda j: (0, j + off))
    wsp = lambda off: pl.BlockSpec((3, 128), lambda j: (0, j + off))
    bsp = lambda off: pl.BlockSpec((1, 128), lambda j: (0, j + off))
    dx, dw, db = pl.pallas_call(
        body, name="convglu_bwd", grid=(NCB,),
        in_specs=[col(0), col(0), col(NCB), wsp(0), wsp(NCB), bsp(0), bsp(NCB)],
        out_specs=[pl.BlockSpec((2, L, 128), lambda j: (0, 0, j)), pl.BlockSpec((2, 3, 128), lambda j: (0, 0, j)),
                   pl.BlockSpec((2, 1, 128), lambda j: (0, 0, j))],
        out_shape=[jax.ShapeDtypeStruct((2, L, DFF), bf16), jax.ShapeDtypeStruct((2, 3, DFF), f32),
                   jax.ShapeDtypeStruct((2, 1, DFF), f32)],
        compiler_params=_cp(("parallel",)),
    )(da, up_pre, up_pre, cw, cw, cb, cb)
    return dx, dw.transpose(1, 0, 2).reshape(3, DUP), db.transpose(1, 0, 2).reshape(1, DUP)


def _attn_block(L):
    return 256 if L % 256 == 0 else 128


TAIL_DEAD = -110.0


def _dot_exact2(x, m):
    hi = x.astype(bf16)
    lo = (x - hi.astype(f32)).astype(bf16)
    return (lax.dot_general(hi, m, NN, preferred_element_type=f32)
            + lax.dot_general(lo, m, NN, preferred_element_type=f32))


def _sb_weights(z, mask, tri_gt, carry):
    l1p = jnp.log(1.0 + jnp.exp(-jnp.abs(z)))
    ls_pos = jnp.minimum(z, 0.0) - l1p
    lm = ls_pos - z
    if mask is not None:
        lm = jnp.where(mask, lm, 0.0)
    tail = _dot_exact2(lm, tri_gt) + carry
    w = jnp.exp(ls_pos + tail)
    if mask is not None:
        w = jnp.where(mask, w, 0.0)
    return w, ls_pos, lm


def _head_masks(B):
    lane = lax.broadcasted_iota(jnp.int32, (B, 128), 1)
    return lane < HD, lane >= HD


def _attn_fwd(qkv):
    L = qkv.shape[0]
    B = _attn_block(L)
    nq = L // B

    def body(q_ref, k_ref, v_ref, o_ref):
        qi = pl.program_id(1)
        heads = _head_masks(B)
        ti = lax.broadcasted_iota(jnp.int32, (B, B), 0)
        si = lax.broadcasted_iota(jnp.int32, (B, B), 1)
        tri_gt = jnp.where(ti > si, 1.0, 0.0).astype(bf16)
        diag = si < ti
        qv = q_ref[...]
        zero16 = jnp.zeros((), bf16)

        def tile(jb, mask, acc, carries):
            ks = pl.multiple_of(jb * B, B)
            kb = k_ref[pl.ds(ks, B), :]
            vb = v_ref[pl.ds(ks, B), :]
            out = []
            for in_head, carry in zip(heads, carries):
                kh = jnp.where(in_head, kb, zero16)
                vh = jnp.where(in_head, vb, zero16)
                z = lax.dot_general(qv, kh, NT, preferred_element_type=f32)
                w, _, lm = _sb_weights(z, mask, tri_gt, carry)
                acc = acc + lax.dot_general(w.astype(bf16), vh, NN, preferred_element_type=f32)
                out.append(carry + jnp.sum(lm, axis=1, keepdims=True))
            return acc, out

        zc = jnp.zeros((B, 1), f32)
        acc, (c0, c1) = tile(qi, diag, jnp.zeros((B, 128), f32), (zc, zc))

        def cond(st):
            return jnp.logical_and(st[0] <= qi, st[4] > TAIL_DEAD)

        def step(st):
            n, a, p0, p1, _ = st
            a, (p0, p1) = tile(qi - n, None, a, (p0, p1))
            return n + 1, a, p0, p1, jnp.maximum(jnp.max(p0), jnp.max(p1))

        st = lax.while_loop(cond, step, (jnp.int32(1), acc, c0, c1, jnp.maximum(jnp.max(c0), jnp.max(c1))))
        o_ref[...] = st[1]

    return pl.pallas_call(
        body, name="attn_fwd", grid=(4, nq),
        in_specs=[pl.BlockSpec((B, 128), lambda hp, i: (i, hp)),
                  pl.BlockSpec((L, 128), lambda hp, i: (0, 4 + hp)),
                  pl.BlockSpec((L, 128), lambda hp, i: (0, 8 + hp))],
        out_specs=pl.BlockSpec((B, 128), lambda hp, i: (i, hp)),
        out_shape=jax.ShapeDtypeStruct((L, ATT), f32),
        compiler_params=_cp(("parallel", "parallel")),
    )(qkv, qkv, qkv)


def _attn_bwd(qkv, o, do):
    L = qkv.shape[0]
    B = _attn_block(L)
    nq = L // B

    def body(q_ref, k_ref, v_ref, o_ref, do_ref, dq_ref, dk_ref, dv_ref):
        qi = pl.program_id(1)

        @pl.when(qi == 0)
        def _():
            dk_ref[...] = jnp.zeros_like(dk_ref)
            dv_ref[...] = jnp.zeros_like(dv_ref)

        heads = _head_masks(B)
        ti = lax.broadcasted_iota(jnp.int32, (B, B), 0)
        si = lax.broadcasted_iota(jnp.int32, (B, B), 1)
        tri_gt = jnp.where(ti > si, 1.0, 0.0).astype(bf16)
        tri_ge = jnp.where(ti >= si, 1.0, 0.0).astype(bf16)
        diag = si < ti
        zero16 = jnp.zeros((), bf16)
        qv = q_ref[...]
        do16 = do_ref[...].astype(bf16)
        dsum_lanes = do16.astype(f32) * o_ref[...]
        qhs = [jnp.where(m, qv, zero16) for m in heads]
        dohs = [jnp.where(m, do16, zero16) for m in heads]
        dsums = [jnp.sum(jnp.where(m, dsum_lanes, 0.0), axis=1, keepdims=True) for m in heads]

        def tile(jb, mask, dq_a, state):
            ks = pl.multiple_of(jb * B, B)
            kb = k_ref[pl.ds(ks, B), :]
            vb = v_ref[pl.ds(ks, B), :]
            dk_blk = jnp.zeros((B, 128), f32)
            dv_blk = jnp.zeros((B, 128), f32)
            out = []
            for in_head, qh, doh, dsum, (carry, suffix) in zip(heads, qhs, dohs, dsums, state):
                kh = jnp.where(in_head, kb, zero16)
                vh = jnp.where(in_head, vb, zero16)
                z = lax.dot_general(qv, kh, NT, preferred_element_type=f32)
                w, ls_pos, lm = _sb_weights(z, mask, tri_gt, carry)
                w16 = w.astype(bf16)
                dw = lax.dot_general(doh, vh, NT, preferred_element_type=f32)
                da = w16.astype(f32) * dw
                sig = jnp.exp(ls_pos)
                suf = _dot_exact2(da, tri_ge) + suffix
                dz = da * (1.0 - sig) - sig * (dsum - suf)
                if mask is not None:
                    dz = jnp.where(mask, dz, 0.0)
                dz16 = dz.astype(bf16)
                dq_a = dq_a + lax.dot_general(dz16, kh, NN, preferred_element_type=f32)
                dk_blk = dk_blk + lax.dot_general(dz16, qh, TN, preferred_element_type=f32)
                dv_blk = dv_blk + lax.dot_general(w16, doh, TN, preferred_element_type=f32)
                out.append((carry + jnp.sum(lm, axis=1, keepdims=True),
                            suffix + jnp.sum(da, axis=1, keepdims=True)))
            dk_ref[pl.ds(ks, B), :] += dk_blk
            dv_ref[pl.ds(ks, B), :] += dv_blk
            return dq_a, out

        def alive(state):
            return jnp.maximum(jnp.max(state[0][0]), jnp.max(state[1][0]))

        zc = jnp.zeros((B, 1), f32)
        dq_acc, state = tile(qi, diag, jnp.zeros((B, 128), f32), ((zc, zc), (zc, zc)))

        def cond(st):
            return jnp.logical_and(st[0] <= qi, st[6] > TAIL_DEAD)

        def step(st):
            n, a, c0, s0, c1, s1, _ = st
            a, new = tile(qi - n, None, a, ((c0, s0), (c1, s1)))
            return n + 1, a, new[0][0], new[0][1], new[1][0], new[1][1], alive(new)

        st = lax.while_loop(cond, step, (jnp.int32(1), dq_acc, state[0][0], state[0][1],
                                         state[1][0], state[1][1], alive(state)))
        dq_ref[...] = st[1] * (1.0 / math.sqrt(HD))

    blk = pl.BlockSpec((B, 128), lambda hp, i: (i, hp))
    dqkv = pl.pallas_call(
        body, name="attn_bwd", grid=(4, nq),
        in_specs=[blk,
                  pl.BlockSpec((L, 128), lambda hp, i: (0, 4 + hp)),
                  pl.BlockSpec((L, 128), lambda hp, i: (0, 8 + hp)),
                  blk, blk],
        out_specs=[blk, pl.BlockSpec((L, 128), lambda hp, i: (0, hp)),
                   pl.BlockSpec((L, 128), lambda hp, i: (0, hp))],
        out_shape=[jax.ShapeDtypeStruct((L, ATT), f32)] * 3,
        compiler_params=_cp(("parallel", "arbitrary")),
    )(qkv, qkv, qkv, o, do)
    return dqkv


def _s5_disc(ldt, ar, ai, br, bi):
    dt = jnp.exp(ldt)
    mag = jnp.exp(dt * ar)
    abar_re = mag * jnp.cos(dt * ai)
    abar_im = mag * jnp.sin(dt * ai)
    em_re = abar_re - 1.0
    em_im = abar_im
    den = ar * ar + ai * ai
    f_re = (em_re * ar + em_im * ai) / den
    f_im = (em_im * ar - em_re * ai) / den
    bb_re = f_re * br - f_im * bi
    bb_im = f_re * bi + f_im * br
    return abar_re, abar_im, bb_re, bb_im


def _s5_disc_fwd(ldt, ar, ai, br, bi):
    def body(ldt_ref, ar_ref, ai_ref, br_ref, bi_ref, o1, o2, o3, o4):
        outs = _s5_disc(ldt_ref[...], ar_ref[...], ai_ref[...], br_ref[...], bi_ref[...])
        for o_ref, v in zip((o1, o2, o3, o4), outs):
            o_ref[...] = v

    col = jax.ShapeDtypeStruct((NS, 1), f32)
    mat = jax.ShapeDtypeStruct((NS, NH16), f32)
    return pl.pallas_call(body, name="s5_disc_fwd", out_shape=[col, col, mat, mat],
                          compiler_params=_cp())(ldt, ar, ai, br, bi)


def _s5_disc_bwd(ldt, ar, ai, br, bi, d_are, d_aim, d_bbre, d_bbim):
    def body(ldt_ref, ar_ref, ai_ref, br_ref, bi_ref, c1, c2, c3, c4, o_ldt, o_ar, o_ai, o_br, o_bi):
        prim = (ldt_ref[...], ar_ref[...], ai_ref[...], br_ref[...], bi_ref[...])
        _, vjp = jax.vjp(_s5_disc, *prim)
        g_ldt, g_ar, g_ai, g_br, g_bi = vjp((c1[...], c2[...], c3[...], c4[...]))
        o_ar[...] = g_ar
        o_ai[...] = g_ai
        o_br[...] = g_br
        o_bi[...] = g_bi
        r = lax.broadcasted_iota(jnp.int32, (NG, NS), 0)
        c = lax.broadcasted_iota(jnp.int32, (NG, NS), 1)
        fold = jnp.where(c // NP == r, 1.0, 0.0).astype(bf16)
        hi = jnp.broadcast_to(g_ldt, (NS, 128))
        p1 = hi.astype(bf16)
        r1 = hi - p1.astype(f32)
        p2 = r1.astype(bf16)
        p3 = (r1 - p2.astype(f32)).astype(bf16)
        dd = lambda p: lax.dot_general(fold, p, NN, preferred_element_type=f32)
        o_ldt[...] = dd(p1) + dd(p2) + dd(p3)

    col = jax.ShapeDtypeStruct((NS, 1), f32)
    mat = jax.ShapeDtypeStruct((NS, NH16), f32)
    return pl.pallas_call(
        body, name="s5_disc_bwd",
        out_shape=[jax.ShapeDtypeStruct((NG, 128), f32), col, col, mat, mat],
        compiler_params=_cp())(ldt, ar, ai, br, bi, d_are, d_aim, d_bbre, d_bbim)


SCAN_LANES = 512


def _cmul(ar, ai, br, bi):
    return ar * br - ai * bi, ar * bi + ai * br


def _pow_tables(ar, ai):
    a2 = _cmul(ar, ai, ar, ai)
    a4 = _cmul(*a2, *a2)
    rows = [(ar, ai)]
    for _ in range(7):
        rows.append(_cmul(*rows[-1], ar, ai))
    return (ar, ai), a2, a4, rows


def _scan_rows(xr_ref, xi_ref, abar_re, abar_im, cr_ref, ci_ref, reverse, on_tile=None):
    T = xr_ref.shape[0]
    nt = T // 8
    row = lax.broadcasted_iota(jnp.int32, (8, SCAN_LANES), 0)
    for c in range(NS // SCAN_LANES):
        ls = slice(c * SCAN_LANES, (c + 1) * SCAN_LANES)
        ar = abar_re[:, ls]
        ai = abar_im[:, ls]
        if reverse:
            ai = -ai
        a1, a2, a4, prow = _pow_tables(ar, ai)
        if reverse:
            prow = prow[::-1]
        pr = jnp.concatenate([p[0] for p in prow], axis=0)
        pi = jnp.concatenate([p[1] for p in prow], axis=0)
        a8r, a8i = prow[0] if reverse else prow[7]
        edge = slice(0, 1) if reverse else slice(7, 8)
        steps = []
        for d, (dr, di) in ((1, a1), (2, a2), (4, a4)):
            keep = (row < 8 - d) if reverse else (row >= d)
            steps.append((8 - d if reverse else d, jnp.where(keep, dr, 0.0), jnp.where(keep, di, 0.0)))

        def local(n, _):
            rs = pl.ds(pl.multiple_of(n * 8, 8), 8)
            xr = xr_ref[rs, ls]
            xi = xi_ref[rs, ls]
            for shift, dr, di in steps:
                mr, mi = _cmul(dr, di, pltpu.roll(xr, shift, 0), pltpu.roll(xi, shift, 0))
                xr = xr + mr
                xi = xi + mi
            xr_ref[rs, ls] = xr
            xi_ref[rs, ls] = xi
            return 0

        lax.fori_loop(0, nt, local, 0, unroll=4)

        def chain(n, carry):
            cr, ci = carry
            it = (nt - 1 - n) if reverse else n
            rs = pl.ds(pl.multiple_of(it * 8, 8), 8)
            xr = xr_ref[rs, ls]
            xi = xi_ref[rs, ls]
            nr, ni = _cmul(a8r, a8i, cr, ci)
            nr = nr + xr[edge, :]
            ni = ni + xi[edge, :]
            mr, mi = _cmul(pr, pi, cr, ci)
            xr = xr + mr
            xi = xi + mi
            xr_ref[rs, ls] = xr
            xi_ref[rs, ls] = xi
            if on_tile is not None:
                nxt_r = jnp.where(row < 7, pltpu.roll(xr, 7, 0), cr)
                nxt_i = jnp.where(row < 7, pltpu.roll(xi, 7, 0), ci)
                on_tile(ls, rs, nxt_r, nxt_i)
            return nr, ni

        cr, ci = lax.fori_loop(0, nt, chain, (cr_ref[:, ls], ci_ref[:, ls]), unroll=2)
        cr_ref[:, ls] = cr
        ci_ref[:, ls] = ci


def _s5_chunk(L):
    return 256 if L % 256 == 0 else L


def _s5_gate(s_re, s_im, u, ccr_ref, cci_ref, dsk_ref, gw_ref, gb_ref):
    y = _dot(s_re, ccr_ref[...]) - _dot(s_im, cci_ref[...]) + dsk_ref[...] * u
    yg = _gelu(y)
    gate = _sigmoid(_dot(yg, gw_ref[...]) + gb_ref[...])
    return y, yg, gate


def _s5_fwd(p, abar_re, abar_im, bbr, bbi, ccr, cci, dsk, gw, gb):
    L = p.shape[0]
    T = _s5_chunk(L)

    def body(u_ref, are_ref, aim_ref, bbr_ref, bbi_ref, ccr_ref, cci_ref, dsk_ref, gw_ref, gb_ref,
             o_ref, sr_ref, si_ref, cr_ref, ci_ref):
        @pl.when(pl.program_id(0) == 0)
        def _():
            cr_ref[...] = jnp.zeros_like(cr_ref)
            ci_ref[...] = jnp.zeros_like(ci_ref)

        u = u_ref[...]
        sr_ref[...] = _dot(u, bbr_ref[...])
        si_ref[...] = _dot(u, bbi_ref[...])
        _scan_rows(sr_ref, si_ref, are_ref[...], aim_ref[...], cr_ref, ci_ref, reverse=False)
        _, yg, gate = _s5_gate(sr_ref[...], si_ref[...], u, ccr_ref, cci_ref, dsk_ref, gw_ref, gb_ref)
        o_ref[...] = yg * gate

    full = lambda shape: pl.BlockSpec(shape, lambda i: (0, 0))
    return pl.pallas_call(
        body, name="s5_fwd", grid=(L // T,),
        in_specs=[pl.BlockSpec((T, SSMW), lambda i: (i, 3)), full((1, NS)), full((1, NS)),
                  full((SSMW, NS)), full((SSMW, NS)), full((NS, SSMW)), full((NS, SSMW)),
                  full((1, SSMW)), full((SSMW, SSMW)), full((1, SSMW))],
        out_specs=[pl.BlockSpec((T, SSMW), lambda i: (i, 0)), pl.BlockSpec((T, NS), lambda i: (i, 0)),
                   pl.BlockSpec((T, NS), lambda i: (i, 0))],
        out_shape=[jax.ShapeDtypeStruct((L, SSMW), f32), jax.ShapeDtypeStruct((L, NS), f32),
                   jax.ShapeDtypeStruct((L, NS), f32)],
        scratch_shapes=[pltpu.VMEM((1, NS), f32), pltpu.VMEM((1, NS), f32)],
        compiler_params=_cp(("arbitrary",)),
    )(p, abar_re, abar_im, bbr, bbi, ccr, cci, dsk, gw, gb)


def _s5_bwd(dout, p, s_re, s_im, abar_re, abar_im, bbr, bbi, ccr, cci, dsk, gw, gb):
    L = p.shape[0]
    T = _s5_chunk(L)
    nchunk = L // T

    def body(do_ref, u_ref, sr_ref, si_ref, are_ref, aim_ref, bbr_ref, bbi_ref, ccr_ref, cci_ref,
             dsk_ref, gw_ref, gb_ref,
             du_ref, dy_ref, yg_ref, dgl_ref, lr16_ref, li16_ref, dare_ref, daim_ref, dd_ref, dgb_ref,
             lr_ref, li_ref, cr_ref, ci_ref, accr_ref, acci_ref):
        i = pl.program_id(0)

        @pl.when(i == 0)
        def _():
            cr_ref[...] = jnp.zeros_like(cr_ref)
            ci_ref[...] = jnp.zeros_like(ci_ref)
            accr_ref[...] = jnp.zeros_like(accr_ref)
            acci_ref[...] = jnp.zeros_like(acci_ref)
            dd_ref[...] = jnp.zeros_like(dd_ref)
            dgb_ref[...] = jnp.zeros_like(dgb_ref)

        u = u_ref[...]
        dov = do_ref[...]
        y, yg, gate = _s5_gate(sr_ref[...], si_ref[...], u, ccr_ref, cci_ref, dsk_ref, gw_ref, gb_ref)
        dgl = dov * yg * gate * (1.0 - gate)
        dyg = dov * gate + _dot(dgl, gw_ref[...], NT)
        dy = dyg * _gelu_grad(y)
        dy_ref[...] = dy.astype(dy_ref.dtype)
        yg_ref[...] = yg.astype(yg_ref.dtype)
        dgl_ref[...] = dgl.astype(dgl_ref.dtype)
        dd_ref[...] += jnp.sum(dy * u, axis=0, keepdims=True)
        dgb_ref[...] += jnp.sum(dgl, axis=0, keepdims=True)
        lr_ref[...] = _dot(dy, ccr_ref[...], NT)
        li_ref[...] = -_dot(dy, cci_ref[...], NT)

        def on_tile(ls, rs, nxt_r, nxt_i):
            s_r = sr_ref[rs, ls]
            s_i = si_ref[rs, ls]
            accr_ref[:, ls] += nxt_r * s_r + nxt_i * s_i
            acci_ref[:, ls] += nxt_i * s_r - nxt_r * s_i

        _scan_rows(lr_ref, li_ref, are_ref[...], aim_ref[...], cr_ref, ci_ref, reverse=True,
                   on_tile=on_tile)
        lam_r = lr_ref[...]
        lam_i = li_ref[...]
        lr16_ref[...] = lam_r.astype(lr16_ref.dtype)
        li16_ref[...] = lam_i.astype(li16_ref.dtype)
        du_ref[...] = dy * dsk_ref[...] + _dot(lam_r, bbr_ref[...], NT) + _dot(lam_i, bbi_ref[...], NT)

        @pl.when(i == nchunk - 1)
        def _():
            dare_ref[...] = jnp.sum(accr_ref[...], axis=0, keepdims=True)
            daim_ref[...] = jnp.sum(acci_ref[...], axis=0, keepdims=True)

    rev = lambda i: (nchunk - 1 - i, 0)
    full = lambda shape: pl.BlockSpec(shape, lambda i: (0, 0))
    half = pl.BlockSpec((T, SSMW), rev)
    wide = pl.BlockSpec((T, NS), rev)
    return pl.pallas_call(
        body, name="s5_bwd", grid=(nchunk,),
        in_specs=[half, pl.BlockSpec((T, SSMW), lambda i: (nchunk - 1 - i, 3)), wide, wide,
                  full((1, NS)), full((1, NS)), full((SSMW, NS)), full((SSMW, NS)),
                  full((NS, SSMW)), full((NS, SSMW)), full((1, SSMW)), full((SSMW, SSMW)), full((1, SSMW))],
        out_specs=[half, half, half, half, wide, wide, full((1, NS)), full((1, NS)),
                   full((1, SSMW)), full((1, SSMW))],
        out_shape=[jax.ShapeDtypeStruct((L, SSMW), f32)] + [jax.ShapeDtypeStruct((L, SSMW), bf16)] * 3
                  + [jax.ShapeDtypeStruct((L, NS), bf16)] * 2 + [jax.ShapeDtypeStruct((1, NS), f32)] * 2
                  + [jax.ShapeDtypeStruct((1, SSMW), f32)] * 2,
        scratch_shapes=[pltpu.VMEM((T, NS), f32), pltpu.VMEM((T, NS), f32),
                        pltpu.VMEM((1, NS), f32), pltpu.VMEM((1, NS), f32),
                        pltpu.VMEM((8, NS), f32), pltpu.VMEM((8, NS), f32)],
        compiler_params=_cp(("arbitrary",)),
    )(dout, p, s_re, s_im, abar_re, abar_im, bbr, bbi, ccr, cci, dsk, gw, gb)


def _block_diag(x):
    g, r, c = x.shape
    eye = jnp.eye(g, dtype=x.dtype)
    return (x[:, :, None, :] * eye[:, None, :, None]).reshape(g * r, g * c)


def _mod_fwd(c16, ada_w, ada_b_loc):
    nloc = ada_w.shape[2]

    def body(c_ref, w_ref, b_ref, o_ref):
        cv = c_ref[...]
        act = cv * _sigmoid(cv)
        o_ref[...] = _dot(act, w_ref[...]) + b_ref[...]

    return pl.pallas_call(
        body, name="mod_fwd", grid=(DEPTH,),
        in_specs=[pl.BlockSpec((16, D), lambda l: (0, 0)), pl.BlockSpec((None, D, nloc), lambda l: (l, 0, 0)),
                  pl.BlockSpec((None, 1, nloc), lambda l: (l, 0, 0))],
        out_specs=pl.BlockSpec((None, 16, nloc), lambda l: (l, 0, 0)),
        out_shape=jax.ShapeDtypeStruct((DEPTH, 16, nloc), f32),
        compiler_params=_cp(("parallel",)),
    )(c16, ada_w, ada_b_loc)


def _ada_grad(c16, dmod16):
    nloc = dmod16.shape[2]

    def body(c_ref, d_ref, o_ref):
        cv = c_ref[...]
        act = cv * _sigmoid(cv)
        o_ref[...] = _dot(act, d_ref[...], TN)

    return pl.pallas_call(
        body, name="ada_grad", grid=(DEPTH,),
        in_specs=[pl.BlockSpec((16, D), lambda l: (0, 0)), pl.BlockSpec((None, 16, nloc), lambda l: (l, 0, 0))],
        out_specs=pl.BlockSpec((None, D, nloc), lambda l: (l, 0, 0)),
        out_shape=jax.ShapeDtypeStruct((DEPTH, D, nloc), f32),
        compiler_params=_cp(("parallel",)),
    )(c16, dmod16)


def _as2d(a):
    return a.reshape(-1, a.shape[-1])


def _ew_rows(rows):
    for cand in (512, 256, 128, 64, 32, 16, 8):
        if rows % cand == 0:
            return cand
    return rows


def _cast_bf16(w):
    w2 = _as2d(w)
    rows, cols = w2.shape
    bm = _ew_rows(rows)

    def body(x_ref, o_ref):
        o_ref[...] = x_ref[...].astype(bf16)

    spec = pl.BlockSpec((bm, cols), lambda i: (i, 0))
    out = pl.pallas_call(body, name="cast_bf16", grid=(rows // bm,), in_specs=[spec], out_specs=spec,
                         out_shape=jax.ShapeDtypeStruct((rows, cols), bf16),
                         compiler_params=_cp(("parallel",)))(w2)
    return out.reshape(w.shape)


def _add2(a, b, out_dtype=f32):
    a2, b2 = _as2d(a), _as2d(b)
    rows, cols = a2.shape
    bm = _ew_rows(rows)

    def body(a_ref, b_ref, o_ref):
        o_ref[...] = (a_ref[...] + b_ref[...]).astype(o_ref.dtype)

    spec = pl.BlockSpec((bm, cols), lambda i: (i, 0))
    out = pl.pallas_call(body, name="add2", grid=(rows // bm,), in_specs=[spec, spec], out_specs=spec,
                         out_shape=jax.ShapeDtypeStruct((rows, cols), out_dtype),
                         compiler_params=_cp(("parallel",)))(a2, b2)
    return out.reshape(a.shape)


def _adamw(parts, w, m, v):
    n = parts.shape[0]
    w2, m2, v2 = _as2d(w), _as2d(m), _as2d(v)
    rows, cols = w2.shape
    p3 = parts.reshape(n, rows, cols)
    bm = _ew_rows(rows)
    if bm * cols * 4 > (1 << 20) and bm % 16 == 0:
        bm //= 2

    def body(p_ref, w_ref, m_ref, v_ref, g_ref, d_ref, nm_ref, nv_ref):
        g = p_ref[0]
        for k in range(1, n):
            g = g + p_ref[k]
        mn = ADAM_B1 * m_ref[...] + (1.0 - ADAM_B1) * g
        vn = ADAM_B2 * v_ref[...] + (1.0 - ADAM_B2) * (g * g)
        m_hat = mn / (1.0 - ADAM_B1 ** ADAM_STEP)
        v_hat = vn / (1.0 - ADAM_B2 ** ADAM_STEP)
        g_ref[...] = g
        d_ref[...] = -ADAM_LR * (m_hat / (jnp.sqrt(v_hat) + ADAM_EPS) + ADAM_WD * w_ref[...])
        nm_ref[...] = mn
        nv_ref[...] = vn

    spec = pl.BlockSpec((bm, cols), lambda i: (i, 0))
    outs = pl.pallas_call(
        body, name="adamw", grid=(rows // bm,),
        in_specs=[pl.BlockSpec((n, bm, cols), lambda i: (0, i, 0)), spec, spec, spec],
        out_specs=[spec] * 4, out_shape=[jax.ShapeDtypeStruct((rows, cols), f32)] * 4,
        compiler_params=_cp(("parallel",)),
    )(p3, w2, m2, v2)
    return tuple(o.reshape(w.shape) for o in outs)


ANY = pl.BlockSpec(memory_space=pl.ANY)


def _my_pos():
    return lax.axis_index("x"), lax.axis_index("y"), lax.axis_index("c")


def _allgather8(name, x):
    def body(x_ref, out_ref, send_sems, recv_sems):
        mx, my, mc = _my_pos()
        me, sibling = (mx, my, mc), (mx, my, 1 - mc)
        chips = [(1 - mx, my), (mx, 1 - my), (1 - mx, 1 - my)]

        def slot(px, py, pc):
            return out_ref.at[4 * px + 2 * py + pc]

        def copy(k, block, to, src=None):
            return pltpu.make_async_remote_copy(
                src_ref=slot(*block) if src is None else src, dst_ref=slot(*block),
                send_sem=send_sems.at[k], recv_sem=recv_sems.at[k], device_id=to, device_id_type=MESH)

        first = [copy(0, me, sibling, src=x_ref)]
        first += [copy(1 + j, me, (*chip, mc), src=x_ref) for j, chip in enumerate(chips)]
        for cp in first:
            cp.start()
        passed = [copy(4 + j, (*chip, mc), sibling) for j, chip in enumerate(chips)]
        for j, chip in enumerate(chips):
            copy(1 + j, (*chip, mc), me).wait_recv()
            passed[j].start()
        copy(0, sibling, me).wait_recv()
        for j, chip in enumerate(chips):
            copy(4 + j, (*chip, 1 - mc), me).wait_recv()
        for cp in first + passed:
            cp.wait_send()

    out = pl.pallas_call(
        body, name=name, in_specs=[ANY], out_specs=ANY,
        out_shape=jax.ShapeDtypeStruct((NDEV,) + x.shape, x.dtype),
        scratch_shapes=[pltpu.SemaphoreType.DMA((7,)), pltpu.SemaphoreType.DMA((7,))],
    )(x)
    dev = 4 * lax.axis_index("x") + 2 * lax.axis_index("y") + lax.axis_index("c")
    return lax.dynamic_update_index_in_dim(out, x, dev, 0)


HALF = DEPTH // 2


def _chip_gather(arrs):
    n = len(arrs)

    def body(*refs):
        ins, outs = refs[:n], refs[n:2 * n]
        send_sems, recv_sems = refs[2 * n:]
        mx, my, mc = _my_pos()
        chips = [(1 - mx, my), (mx, 1 - my), (1 - mx, 1 - my)]
        mine = 2 * mx + my
        my_half = pl.ds(HALF * mc, HALF)
        sib_half = pl.ds(HALF * (1 - mc), HALF)

        def ici(a, j, src_chip, half):
            px, py = chips[j]
            return pltpu.make_async_remote_copy(
                src_ref=ins[a].at[half], dst_ref=outs[a].at[src_chip, half],
                send_sem=send_sems.at[3 * a + j], recv_sem=recv_sems.at[3 * a + j],
                device_id=(px, py, mc), device_id_type=MESH)

        def d2d(a, j, half):
            px, py = chips[j]
            blk = outs[a].at[2 * px + py, half]
            return pltpu.make_async_remote_copy(
                src_ref=blk, dst_ref=blk,
                send_sem=send_sems.at[3 * n + 3 * a + j], recv_sem=recv_sems.at[3 * n + 3 * a + j],
                device_id=(mx, my, 1 - mc), device_id_type=MESH)

        sends = [ici(a, j, mine, my_half) for a in range(n) for j in range(3)]
        for cp in sends:
            cp.start()
        passed = []
        for a in range(n):
            for j, (px, py) in enumerate(chips):
                ici(a, j, 2 * px + py, my_half).wait_recv()
                fwd = d2d(a, j, my_half)
                fwd.start()
                passed.append(fwd)
        for a in range(n):
            for j in range(3):
                d2d(a, j, sib_half).wait_recv()
        for cp in sends + passed:
            cp.wait_send()

    outs = pl.pallas_call(
        body, name="chip_gather", in_specs=[ANY] * n, out_specs=[ANY] * n,
        out_shape=[jax.ShapeDtypeStruct((NCHIP,) + a.shape, a.dtype) for a in arrs],
        scratch_shapes=[pltpu.SemaphoreType.DMA((6 * n,)), pltpu.SemaphoreType.DMA((6 * n,))],
    )(*arrs)
    mine = 2 * lax.axis_index("x") + lax.axis_index("y")
    return [lax.dynamic_update_index_in_dim(o, a, mine, 0) for o, a in zip(outs, arrs)]


def _sibling_half_swap(arrs):
    n = len(arrs)

    def body(*refs):
        ins, outs = refs[:n], refs[n:2 * n]
        send_sems, recv_sems = refs[2 * n:]
        mx, my, mc = _my_pos()
        sib_half = pl.ds(HALF * (1 - mc), HALF)
        cps = [pltpu.make_async_remote_copy(
            src_ref=ins[a].at[j, sib_half], dst_ref=outs[a].at[j],
            send_sem=send_sems.at[NCHIP * a + j], recv_sem=recv_sems.at[NCHIP * a + j],
            device_id=(mx, my, 1 - mc), device_id_type=MESH) for a in range(n) for j in range(NCHIP)]
        for cp in cps:
            cp.start()
        for cp in cps:
            cp.wait()

    return pl.pallas_call(
        body, name="sibling_half_swap", in_specs=[ANY] * n, out_specs=[ANY] * n,
        out_shape=[jax.ShapeDtypeStruct((NCHIP, HALF) + a.shape[2:], a.dtype) for a in arrs],
        scratch_shapes=[pltpu.SemaphoreType.DMA((NCHIP * n,)), pltpu.SemaphoreType.DMA((NCHIP * n,))],
    )(*arrs)


def _sibling_merge_halves(arrs):
    n = len(arrs)

    def body(*refs):
        ins, outs = refs[:n], refs[n:2 * n]
        send_sems, recv_sems = refs[2 * n:]
        mx, my, mc = _my_pos()
        my_half = pl.ds(HALF * mc, HALF)
        cps = [pltpu.make_async_remote_copy(
            src_ref=ins[a], dst_ref=outs[a].at[my_half],
            send_sem=send_sems.at[a], recv_sem=recv_sems.at[a],
            device_id=(mx, my, 1 - mc), device_id_type=MESH) for a in range(n)]
        for cp in cps:
            cp.start()
        for cp in cps:
            cp.wait()

    outs = pl.pallas_call(
        body, name="sibling_merge_halves", in_specs=[ANY] * n, out_specs=[ANY] * n,
        out_shape=[jax.ShapeDtypeStruct((DEPTH,) + a.shape[1:], a.dtype) for a in arrs],
        scratch_shapes=[pltpu.SemaphoreType.DMA((n,)), pltpu.SemaphoreType.DMA((n,))],
    )(*arrs)
    start = HALF * lax.axis_index("c")
    return [lax.dynamic_update_slice_in_dim(o, a, start, 0) for o, a in zip(outs, arrs)]


def _sum_parts(parts):
    n = parts.shape[0]
    p3 = parts.reshape(n, -1, parts.shape[-1])
    rows, cols = p3.shape[1:]
    bm = _ew_rows(rows)
    if bm * cols * 4 > (1 << 20) and bm % 16 == 0:
        bm //= 2

    def body(p_ref, o_ref):
        g = p_ref[0].astype(f32)
        for k in range(1, n):
            g = g + p_ref[k].astype(f32)
        o_ref[...] = g

    out = pl.pallas_call(
        body, name="sum_parts", grid=(rows // bm,),
        in_specs=[pl.BlockSpec((n, bm, cols), lambda i: (0, i, 0))],
        out_specs=pl.BlockSpec((bm, cols), lambda i: (i, 0)),
        out_shape=jax.ShapeDtypeStruct((rows, cols), f32),
        compiler_params=_cp(("parallel",)),
    )(p3)
    return out.reshape(parts.shape[1:])


def _chip_scatter(arrs):
    n = len(arrs)

    def body(*refs):
        ins, outs = refs[:n], refs[n:2 * n]
        send_sems, recv_sems = refs[2 * n:]
        mx, my, mc = _my_pos()
        chips = [(1 - mx, my), (mx, 1 - my), (1 - mx, 1 - my)]
        mine = 2 * mx + my
        sends = []
        for a in range(n):
            for j, (px, py) in enumerate(chips):
                sends.append(pltpu.make_async_remote_copy(
                    src_ref=ins[a].at[2 * px + py], dst_ref=outs[a].at[mine],
                    send_sem=send_sems.at[3 * a + j], recv_sem=recv_sems.at[3 * a + j],
                    device_id=(px, py, mc), device_id_type=MESH))
        for cp in sends:
            cp.start()
        for a in range(n):
            for j, (px, py) in enumerate(chips):
                pltpu.make_async_remote_copy(
                    src_ref=ins[a].at[mine], dst_ref=outs[a].at[2 * px + py],
                    send_sem=send_sems.at[3 * a + j], recv_sem=recv_sems.at[3 * a + j],
                    device_id=(px, py, mc), device_id_type=MESH).wait_recv()
        for cp in sends:
            cp.wait_send()

    outs = pl.pallas_call(
        body, name="chip_scatter", in_specs=[ANY] * n, out_specs=[ANY] * n,
        out_shape=[jax.ShapeDtypeStruct(a.shape, a.dtype) for a in arrs],
        scratch_shapes=[pltpu.SemaphoreType.DMA((3 * n,)), pltpu.SemaphoreType.DMA((3 * n,))],
    )(*arrs)
    mine = 2 * lax.axis_index("x") + lax.axis_index("y")
    return [lax.dynamic_update_index_in_dim(o, lax.dynamic_index_in_dim(a, mine, 0, keepdims=False), mine, 0)
            for o, a in zip(outs, arrs)]


def _layer_fwd(x, res, res_gate, wl, mod):
    L = x.shape[0]
    bm = _row_block(L, 1024)
    nb = L // bm
    sh1, sc1, g1, sh2, sc2, g2 = [mod[k * D:(k + 1) * D][None, :] for k in range(NMOD)]
    sv = {}
    xin, h = _rmsmod_fwd(x, res, res_gate, wl["norm1_g"], sh1, sc1)
    if xin is None:
        xin = x
    p = _mm("mm_in", h, wl["w_in"], dims=NN, grid=(nb, 4, 1),
            a_block=(bm, D), a_map=lambda i, j, k: (i, 0),
            b_block=(None, D, 512), b_map=lambda i, j, k: (j, 0, 0),
            out_shape=(L, INW), o_block=(bm, 512), o_map=lambda i, j, k: (i, j))
    qkv = _qknorm_fwd(p, wl["gqk"])
    oa = _attn_fwd(qkv)
    os_, s_re, s_im = _s5_fwd(p, wl["abar_re"], wl["abar_im"], wl["bbr"], wl["bbi"], wl["ccr"], wl["cci"],
                              wl["dsk"], wl["gw"], wl["gb"])
    o = _outnorm_fwd(oa, os_, wl["attn_out_g"], wl["ssm_out_g"])
    proj = _mm("mm_out", o, wl["w_out"], dims=NN, grid=(nb, 2, 1),
               a_block=(bm, D), a_map=lambda i, j, k: (i, 0),
               b_block=(D, 512), b_map=lambda i, j, k: (0, j),
               out_shape=(L, D), o_block=(bm, 512), o_map=lambda i, j, k: (i, j))
    x1, h2 = _rmsmod_fwd(xin, proj, g1, wl["norm2_g"], sh2, sc2)
    up_pre = _mm("mm_up", h2, wl["w_up"], dims=NN, grid=(nb, 4, 1),
                 a_block=(bm, D), a_map=lambda i, j, k: (i, 0),
                 b_block=(None, D, 1408), b_map=lambda i, j, k: (j, 0, 0),
                 out_shape=(L, DUP), o_block=(bm, 1408), o_map=lambda i, j, k: (i, j))
    a = _convglu_fwd(up_pre, wl["conv_w"], wl["conv_b"])
    down = _mm("mm_down", a, wl["w_down"], dims=NN, grid=(nb, 2, 1),
               a_block=(bm, DFF), a_map=lambda i, j, k: (i, 0),
               b_block=(DFF, 512), b_map=lambda i, j, k: (0, j),
               out_shape=(L, D), o_block=(bm, 512), o_map=lambda i, j, k: (i, j))
    sv.update(xin=xin, h=h, p=p, qkv=qkv, oa=oa, os=os_, s_re=s_re, s_im=s_im, o=o, proj=proj,
              x1=x1, h2=h2, up_pre=up_pre, a=a, down=down,
              sc1=sc1, g1=g1, sc2=sc2, g2=g2)
    return x1, down, g2, sv


def _layer_bwd(dx_out, wl, sv):
    L = dx_out.shape[0]
    bm = _row_block(L, 1024)
    nb = L // bm
    bk = _row_block(L, 512)
    nk = L // bk
    g = {}
    dd, dg2 = _gate_bwd(dx_out, sv["down"], sv["g2"])
    da = _mm("mm_down_dx", dd, wl["w_down"], dims=NT, grid=(nb, 2, 1),
             a_block=(bm, D), a_map=lambda i, j, k: (i, 0),
             b_block=(1408, D), b_map=lambda i, j, k: (j, 0),
             out_shape=(L, DFF), o_block=(bm, 1408), o_map=lambda i, j, k: (i, j))
    g["w_down"] = _mm("mm_down_dw", sv["a"], dd, dims=TN, grid=(1, 2, nk),
                      a_block=(bk, DFF), a_map=lambda i, j, k: (k, 0),
                      b_block=(bk, 512), b_map=lambda i, j, k: (k, j),
                      out_shape=(DFF, D), o_block=(DFF, 512), o_map=lambda i, j, k: (0, j))
    dup, g["conv_w"], g["conv_b"] = _convglu_bwd(da, sv["up_pre"], wl["conv_w"], wl["conv_b"])
    dh2 = _mm("mm_up_dx", dup, wl["w_up"], dims=NT, grid=(nb, 1, 4),
              a_block=(None, bm, 1408), a_map=lambda i, j, k: (k // 2, i, k % 2),
              b_block=(None, D, 1408), b_map=lambda i, j, k: (k, 0, 0),
              out_shape=(L, D), o_block=(bm, D), o_map=lambda i, j, k: (i, 0))
    g["w_up"] = _mm("mm_up_dw", sv["h2"], dup, dims=TN, grid=(1, 4, nk),
                    a_block=(bk, D), a_map=lambda i, j, k: (k, 0),
                    b_block=(None, bk, 1408), b_map=lambda i, j, k: (j // 2, k, j % 2),
                    out_shape=(NCHIP, D, 1408), o_block=(None, D, 1408), o_map=lambda i, j, k: (j, 0, 0))
    dx1, dsh2, dsc2, g["norm2_g"] = _rmsmod_bwd(dh2, sv["x1"], wl["norm2_g"], sv["sc2"], dx_out)
    dproj, dg1 = _gate_bwd(dx1, sv["proj"], sv["g1"])
    do = _mm("mm_out_dx", dproj, wl["w_out"], dims=NT, grid=(nb, 2, 1),
             a_block=(bm, D), a_map=lambda i, j, k: (i, 0),
             b_block=(512, D), b_map=lambda i, j, k: (j, 0),
             out_shape=(L, D), o_block=(bm, 512), o_map=lambda i, j, k: (i, j))
    g["w_out"] = _mm("mm_out_dw", sv["o"], dproj, dims=TN, grid=(1, 2, nk),
                     a_block=(bk, D), a_map=lambda i, j, k: (k, 0),
                     b_block=(bk, 512), b_map=lambda i, j, k: (k, j),
                     out_shape=(D, D), o_block=(D, 512), o_map=lambda i, j, k: (0, j))
    doa, dos, g["attn_out_g"], g["ssm_out_g"] = _outnorm_bwd(do, sv["oa"], sv["os"],
                                                             wl["attn_out_g"], wl["ssm_out_g"])
    dqn, dkn, dv = _attn_bwd(sv["qkv"], sv["oa"], doa)
    (du, dy16, yg16, dgl16, lam_re, lam_im, d_abar_re, d_abar_im, g["ssm_d"], g["glu_b"]) = _s5_bwd(
        dos, sv["p"], sv["s_re"], sv["s_im"], wl["abar_re"], wl["abar_im"], wl["bbr"], wl["bbi"],
        wl["ccr"], wl["cci"], wl["dsk"], wl["gw"], wl["gb"])

    d_ccr = _diag_tn("s5_dcc", sv["s_re"], dy16, NP)
    d_cci = _diag_tn("s5_dcc", sv["s_im"], dy16, NP)
    d_bbre = _diag_tn("s5_dbb", lam_re, sv["p"], NP, b_blk0=12)
    d_bbim = _diag_tn("s5_dbb", lam_im, sv["p"], NP, b_blk0=12)
    d_gw = _diag_tn("s5_dgw", yg16, dgl16, NH16)
    g["ssm_c_re"] = d_ccr.reshape(NG, NP, NH16).transpose(0, 2, 1)
    g["ssm_c_im"] = -d_cci.reshape(NG, NP, NH16).transpose(0, 2, 1)
    g["glu_w"] = d_gw.reshape(NG, NH16, NH16)
    d_ldt, d_ar, d_ai, d_br, d_bi = _s5_disc_bwd(
        wl["ldt_col"], wl["ar_col"], wl["ai_col"], wl["br_mat"], wl["bi_mat"],
        d_abar_re.reshape(NS, 1), d_abar_im.reshape(NS, 1), d_bbre, d_bbim)
    g["ssm_log_dt"] = d_ldt[:, 0]
    g["ssm_a_re"] = d_ar.reshape(NG, NP)
    g["ssm_a_im"] = d_ai.reshape(NG, NP)
    g["ssm_b_re"] = d_br.reshape(NG, NP, NH16)
    g["ssm_b_im"] = d_bi.reshape(NG, NP, NH16)
    dp, dgqk = _qknorm_bwd(dqn, dkn, dv, du, sv["p"], wl["gqk"])
    g["q_norm_g"] = dgqk[0, :HD]
    g["k_norm_g"] = dgqk[0, HD:2 * HD]
    dh = _mm("mm_in_dx", dp, wl["w_in"], dims=NT, grid=(nb, 1, 4),
             a_block=(bm, 512), a_map=lambda i, j, k: (i, k),
             b_block=(None, D, 512), b_map=lambda i, j, k: (k, 0, 0),
             out_shape=(L, D), o_block=(bm, D), o_map=lambda i, j, k: (i, 0))
    g["w_in"] = _mm("mm_in_dw", sv["h"], dp, dims=TN, grid=(1, 4, nk),
                    a_block=(bk, D), a_map=lambda i, j, k: (k, 0),
                    b_block=(bk, 512), b_map=lambda i, j, k: (k, j),
                    out_shape=(NCHIP, D, 512), o_block=(None, D, 512), o_map=lambda i, j, k: (j, 0, 0))
    dx_in, dsh1, dsc1, g["norm1_g"] = _rmsmod_bwd(dh, sv["xin"], wl["norm1_g"], sv["sc1"], dx1)
    g["dmod"] = jnp.concatenate([dsh1, dsc1, dg1, dsh2, dsc2, dg2], axis=1)[0]
    return dx_in, g


def _prep_layer(l, big, small):
    wl = {}
    wl["w_in"] = big["w_in"][:, l]
    wl["w_up"] = big["w_up"][:, l]
    wl["w_out"] = big["w_out"][:, l].reshape(D, D)
    wl["w_down"] = big["w_down"][:, l].reshape(DFF, D)
    wl["conv_w"] = big["conv_w"][:, l].transpose(1, 0, 2).reshape(3, DUP)
    wl["conv_b"] = small["ffn_conv_b"][l][None, :]
    wl["norm1_g"] = small["norm1_g"][l][None, :]
    wl["norm2_g"] = small["norm2_g"][l][None, :]
    wl["attn_out_g"] = small["attn_out_g"][l][None, :]
    wl["ssm_out_g"] = small["ssm_out_g"][l][None, :]
    wl["gqk"] = jnp.concatenate([jnp.tile(small["q_norm_g"][l], 8), jnp.tile(small["k_norm_g"][l], 8)])[None, :]
    wl["ldt_col"] = jnp.repeat(small["ssm_log_dt"][l], NP)[:, None]
    wl["ar_col"] = small["ssm_a_re"][l].reshape(NS, 1)
    wl["ai_col"] = small["ssm_a_im"][l].reshape(NS, 1)
    wl["br_mat"] = small["ssm_b_re"][l].reshape(NS, NH16)
    wl["bi_mat"] = small["ssm_b_im"][l].reshape(NS, NH16)
    abar_re, abar_im, bb_re, bb_im = _s5_disc_fwd(wl["ldt_col"], wl["ar_col"], wl["ai_col"],
                                                  wl["br_mat"], wl["bi_mat"])
    wl["abar_re"] = abar_re.reshape(1, NS)
    wl["abar_im"] = abar_im.reshape(1, NS)
    wl["bbr"] = _block_diag(bb_re.reshape(NG, NP, NH16).transpose(0, 2, 1)).astype(bf16)
    wl["bbi"] = _block_diag(bb_im.reshape(NG, NP, NH16).transpose(0, 2, 1)).astype(bf16)
    wl["ccr"] = _block_diag(small["ssm_c_re"][l].transpose(0, 2, 1)).astype(bf16)
    wl["cci"] = _block_diag(small["ssm_c_im"][l].transpose(0, 2, 1)).astype(bf16)
    wl["gw"] = _block_diag(small["glu_w"][l]).astype(bf16)
    wl["dsk"] = small["ssm_d"][l].reshape(1, SSMW)
    wl["gb"] = small["glu_b"][l].reshape(1, SSMW)
    return wl


def _local_step(x, target, mods, big, small):
    wls = [_prep_layer(l, big, small) for l in range(DEPTH)]
    saved = []
    cur, res, res_gate = x, None, None
    for l in range(DEPTH):
        cur, res, res_gate, sv = _layer_fwd(cur, res, res_gate, wls[l], mods[l])
        saved.append(sv)
    loss, dx = _loss_fwd_bwd(cur, res, res_gate, target)
    grads = [None] * DEPTH
    for l in reversed(range(DEPTH)):
        dx, grads[l] = _layer_bwd(dx, wls[l], saved[l])
    return loss, dx, grads


BIG = ("w_in", "w_out", "ffn_w_up", "ffn_w_down")
SMALL = ("ada_b", "norm1_g", "q_norm_g", "k_norm_g", "ssm_a_re", "ssm_a_im", "ssm_log_dt",
         "ssm_b_re", "ssm_b_im", "ssm_c_re", "ssm_c_im", "ssm_d", "glu_w", "glu_b",
         "attn_out_g", "ssm_out_g", "norm2_g", "ffn_conv_b")
NAMES = ("ada_w", "ada_b", "norm1_g", "w_in", "q_norm_g", "k_norm_g", "ssm_a_re", "ssm_a_im",
         "ssm_log_dt", "ssm_b_re", "ssm_b_im", "ssm_c_re", "ssm_c_im", "ssm_d", "glu_w", "glu_b",
         "attn_out_g", "ssm_out_g", "w_out", "norm2_g", "ffn_w_up", "ffn_conv_w", "ffn_conv_b",
         "ffn_w_down")
PACK_COLS = 1024


def _pack(arrs):
    flat = jnp.concatenate([a.reshape(-1) for a in arrs])
    rows = -(-flat.shape[0] // PACK_COLS)
    rows = -(-rows // 8) * 8
    flat = jnp.pad(flat, (0, rows * PACK_COLS - flat.shape[0]))
    return flat.reshape(rows, PACK_COLS)


def _unpack(packed, shapes):
    flat = packed.reshape(-1)
    out, off = [], 0
    for s in shapes:
        n = math.prod(s)
        out.append(flat[off:off + n].reshape(s))
        off += n
    return out


def kernel(x, c, ada_w, ada_b, norm1_g, w_in, q_norm_g, k_norm_g, ssm_a_re, ssm_a_im, ssm_log_dt, ssm_b_re, ssm_b_im, ssm_c_re, ssm_c_im, ssm_d, glu_w, glu_b, attn_out_g, ssm_out_g, w_out, norm2_g, ffn_w_up, ffn_conv_w, ffn_conv_b, ffn_w_down, loss_target, m_ada_w, m_ada_b, m_norm1_g, m_w_in, m_q_norm_g, m_k_norm_g, m_ssm_a_re, m_ssm_a_im, m_ssm_log_dt, m_ssm_b_re, m_ssm_b_im, m_ssm_c_re, m_ssm_c_im, m_ssm_d, m_glu_w, m_glu_b, m_attn_out_g, m_ssm_out_g, m_w_out, m_norm2_g, m_ffn_w_up, m_ffn_conv_w, m_ffn_conv_b, m_ffn_w_down, v_ada_w, v_ada_b, v_norm1_g, v_w_in, v_q_norm_g, v_k_norm_g, v_ssm_a_re, v_ssm_a_im, v_ssm_log_dt, v_ssm_b_re, v_ssm_b_im, v_ssm_c_re, v_ssm_c_im, v_ssm_d, v_glu_w, v_glu_b, v_attn_out_g, v_ssm_out_g, v_w_out, v_norm2_g, v_ffn_w_up, v_ffn_conv_w, v_ffn_conv_b, v_ffn_w_down):
    env = dict(locals())
    w = {n: env[n] for n in NAMES}
    m = {n: env["m_" + n] for n in NAMES}
    v = {n: env["v_" + n] for n in NAMES}
    mx, my, mc = _my_pos()
    chip = 2 * mx + my
    dev = 4 * mx + 2 * my + mc
    xl = x[0]
    tl = loss_target[0]

    c_all = _allgather8("gather_c", jnp.pad(c, ((0, 7), (0, 0))))[:, 0, :]
    c16 = jnp.pad(c_all, ((0, 8), (0, 0)))
    nloc = NMOD * D // NCHIP
    ada_b_loc = lax.dynamic_slice(ada_b, (0, chip * nloc), (DEPTH, nloc))[:, None, :]
    mod_loc = _mod_fwd(c16, ada_w, ada_b_loc)
    mod_g = _allgather8("gather_mod", mod_loc.reshape(DEPTH * 16, nloc))
    mod_all = mod_g[0::2].reshape(NCHIP, DEPTH, 16, nloc).transpose(1, 2, 0, 3).reshape(DEPTH, 16, NMOD * D)
    mods = lax.dynamic_index_in_dim(mod_all, dev, axis=1, keepdims=False)

    gathered = _chip_gather([_cast_bf16(w["w_in"]), _cast_bf16(w["w_out"]), _cast_bf16(w["ffn_w_up"]),
                             _cast_bf16(w["ffn_w_down"]), w["ffn_conv_w"]])
    big = dict(zip(("w_in", "w_out", "w_up", "w_down", "conv_w"), gathered))
    small = {n: w[n] for n in SMALL}

    loss_loc, grad_x, grads = _local_step(xl, tl, mods, big, small)
    loss = lax.psum(loss_loc, ("x", "y", "c"))

    def stacked(name):
        return jnp.stack([grads[l][name] for l in range(DEPTH)], axis=1)

    loc = [stacked("w_in"),
           jnp.stack([grads[l]["w_out"].reshape(NCHIP, D // NCHIP, D) for l in range(DEPTH)], axis=1),
           stacked("w_up"),
           jnp.stack([grads[l]["w_down"].reshape(NCHIP, DFF // NCHIP, D) for l in range(DEPTH)], axis=1)]
    recv = _sibling_half_swap(loc)
    mine_half = [lax.dynamic_slice_in_dim(a, HALF * mc, HALF, axis=1) for a in loc]
    chip_sum = [_add2(a, b, bf16) for a, b in zip(mine_half, recv)]
    parts = _chip_scatter(chip_sum)
    g_full = _sibling_merge_halves([_sum_parts(p) for p in parts])
    outs = {}
    for name, g_big in zip(BIG, g_full):
        outs[name] = _adamw(g_big[None], w[name], m[name], v[name])

    small_names = SMALL[1:] + ("ffn_conv_w_full",)
    small_grads = []
    for n in SMALL[1:]:
        key = {"ffn_conv_b": "conv_b"}.get(n, n)
        small_grads.append(jnp.stack([grads[l][key].reshape(w[n].shape[1:]) for l in range(DEPTH)]))
    small_grads.append(jnp.stack([grads[l]["conv_w"] for l in range(DEPTH)]))
    dmod = jnp.stack([grads[l]["dmod"] for l in range(DEPTH)])
    packed = _pack([dmod] + small_grads)
    allp = _allgather8("gather_small", packed)
    shapes = [dmod.shape] + [a.shape for a in small_grads]
    dmod_all = allp[:, :DEPTH * NMOD, :].reshape(NDEV, DEPTH, NMOD * D)
    dmod_loc = lax.dynamic_slice(dmod_all, (0, 0, chip * nloc), (NDEV, DEPTH, nloc)).transpose(1, 0, 2)
    g_ada = _ada_grad(c16, jnp.pad(dmod_loc, ((0, 0), (0, 8), (0, 0))))
    outs["ada_w"] = _adamw(g_ada[None], w["ada_w"], m["ada_w"], v["ada_w"])
    conv_w_shape = (DEPTH, 3, DUP)
    w_small = _pack([w[n] for n in SMALL] + [jnp.zeros(conv_w_shape, f32)])
    m_small = _pack([m[n] for n in SMALL] + [jnp.zeros(conv_w_shape, f32)])
    v_small = _pack([v[n] for n in SMALL] + [jnp.ones(conv_w_shape, f32)])
    res_small = _adamw(allp, w_small, m_small, v_small)
    unpacked = [_unpack(r, shapes) for r in res_small]
    for i, n in enumerate(SMALL):
        outs[n] = tuple(unpacked[k][i] for k in range(4))
    g_conv_full = unpacked[0][len(SMALL)]
    ncw = DUP // NCHIP
    g_conv = lax.dynamic_slice(g_conv_full, (0, 0, chip * ncw), (DEPTH, 3, ncw))
    outs["ffn_conv_w"] = _adamw(g_conv[None], w["ffn_conv_w"], m["ffn_conv_w"], v["ffn_conv_w"])

    result = [loss, grad_x[None]]
    for k in range(4):
        result += [outs[n][k] for n in NAMES]
    return tuple(result)
```

```python
import functools
import math

import jax
import jax.numpy as jnp
from jax import lax
from jax.experimental import pallas as pl
from jax.experimental.pallas import tpu as pltpu

f32 = jnp.float32
bf16 = jnp.bfloat16
_MXU = jnp.bfloat16

D = 1024
ATT = 512
SSMW = 512
HD = 64
NG = 32
NP = 64
NH16 = 16
NS = NG * NP
INW = 2048
DFF = 2816
DUP = 2 * DFF
NMOD = 6
DEPTH = 4
EPS = 1e-6
NCHIP = 4
NDEV = 8

ADAM_LR = 0.001
ADAM_B1 = 0.9
ADAM_B2 = 0.999
ADAM_EPS = 1e-08
ADAM_WD = 0.01
ADAM_STEP = 10

VMEM_LIMIT = 56 * 1024 * 1024
MESH = pl.DeviceIdType.MESH

NN = (((1,), (0,)), ((), ()))
NT = (((1,), (1,)), ((), ()))
TN = (((0,), (0,)), ((), ()))


def _cp(sem=None):
    if sem is None:
        return pltpu.CompilerParams(vmem_limit_bytes=VMEM_LIMIT)
    return pltpu.CompilerParams(dimension_semantics=sem, vmem_limit_bytes=VMEM_LIMIT)


def _dot(a, b, dims=NN):
    return lax.dot_general(a.astype(_MXU), b.astype(_MXU), dims, preferred_element_type=f32)


def _dot_exact(x, m):
    hi = x.astype(bf16)
    r1 = x - hi.astype(f32)
    mid = r1.astype(bf16)
    lo = (r1 - mid.astype(f32)).astype(bf16)
    d = lambda p: lax.dot_general(p, m, NN, preferred_element_type=f32)
    return d(hi) + d(mid) + d(lo)


def _gelu(x):
    c = math.sqrt(2.0 / math.pi)
    return 0.5 * x * (1.0 + jnp.tanh(c * (x + 0.044715 * (x * x * x))))


def _gelu_grad(x):
    c = math.sqrt(2.0 / math.pi)
    t = jnp.tanh(c * (x + 0.044715 * (x * x * x)))
    return 0.5 * (1.0 + t) + 0.5 * x * (1.0 - t * t) * c * (1.0 + 3.0 * 0.044715 * (x * x))


def _sigmoid(x):
    return 1.0 / (1.0 + jnp.exp(-x))


def _mm(name, a, b, *, dims, grid, a_block, a_map, b_block, b_map, out_shape, o_block, o_map,
        out_dtype=f32):
    nk = grid[2]
    acc_shape = tuple(s for s in o_block if s is not None)

    def body(a_ref, b_ref, o_ref, acc_ref):
        k = pl.program_id(2)
        part = _dot(a_ref[...], b_ref[...], dims)
        if nk == 1:
            o_ref[...] = part.astype(o_ref.dtype)
        else:
            @pl.when(k == 0)
            def _():
                acc_ref[...] = part

            @pl.when(k > 0)
            def _():
                acc_ref[...] += part

            @pl.when(k == nk - 1)
            def _():
                o_ref[...] = acc_ref[...].astype(o_ref.dtype)

    return pl.pallas_call(
        body, name=name, grid=grid,
        in_specs=[pl.BlockSpec(a_block, a_map), pl.BlockSpec(b_block, b_map)],
        out_specs=pl.BlockSpec(o_block, o_map),
        out_shape=jax.ShapeDtypeStruct(out_shape, out_dtype),
        scratch_shapes=[pltpu.VMEM(acc_shape if nk > 1 else (8, 128), f32)],
        compiler_params=_cp(("parallel", "parallel", "arbitrary")),
    )(a, b)


def _row_block(L, want):
    return want if L % want == 0 else L


def _rmsmod_fwd(x, res, gate, g, sh, sc):
    L = x.shape[0]
    bm = _row_block(L, 256)
    with_res = res is not None

    def body(*refs):
        if with_res:
            x_ref, r_ref, gt_ref, g_ref, sh_ref, sc_ref, xo_ref, h_ref = refs
            xin = x_ref[...] + gt_ref[...] * r_ref[...]
            xo_ref[...] = xin
        else:
            x_ref, g_ref, sh_ref, sc_ref, h_ref = refs
            xin = x_ref[...]
        inv = lax.rsqrt(jnp.mean(xin * xin, axis=-1, keepdims=True) + EPS)
        xn = xin * inv * g_ref[...]
        h_ref[...] = (xn * (1.0 + sc_ref[...]) + sh_ref[...]).astype(h_ref.dtype)

    row = pl.BlockSpec((bm, D), lambda i: (i, 0))
    vec = pl.BlockSpec((1, D), lambda i: (0, 0))
    if with_res:
        return pl.pallas_call(
            body, name="rmsmod_res_fwd", grid=(L // bm,),
            in_specs=[row, row, vec, vec, vec, vec], out_specs=[row, row],
            out_shape=[jax.ShapeDtypeStruct((L, D), f32), jax.ShapeDtypeStruct((L, D), bf16)],
            compiler_params=_cp(("parallel",)),
        )(x, res, gate, g, sh, sc)
    h = pl.pallas_call(
        body, name="rmsmod_fwd", grid=(L // bm,),
        in_specs=[row, vec, vec, vec], out_specs=row,
        out_shape=jax.ShapeDtypeStruct((L, D), bf16),
        compiler_params=_cp(("parallel",)),
    )(x, g, sh, sc)
    return None, h


def _rmsmod_bwd(dh, x, g, sc, dres):
    L = x.shape[0]
    bm = _row_block(L, 256)

    def body(dh_ref, x_ref, g_ref, sc_ref, dr_ref, dx_ref, dsh_ref, dsc_ref, dg_ref):
        i = pl.program_id(0)
        xv = x_ref[...]
        dhv = dh_ref[...]
        inv = lax.rsqrt(jnp.mean(xv * xv, axis=-1, keepdims=True) + EPS)
        xh = xv * inv
        gv = g_ref[...]
        xn = xh * gv
        dxn = dhv * (1.0 + sc_ref[...])
        dxh = dxn * gv
        dx_ref[...] = inv * (dxh - xh * jnp.mean(dxh * xh, axis=-1, keepdims=True)) + dr_ref[...]
        p_sh = jnp.sum(dhv, axis=0, keepdims=True)
        p_sc = jnp.sum(dhv * xn, axis=0, keepdims=True)
        p_g = jnp.sum(dxn * xh, axis=0, keepdims=True)

        @pl.when(i == 0)
        def _():
            dsh_ref[...] = p_sh
            dsc_ref[...] = p_sc
            dg_ref[...] = p_g

        @pl.when(i > 0)
        def _():
            dsh_ref[...] += p_sh
            dsc_ref[...] += p_sc
            dg_ref[...] += p_g

    row = pl.BlockSpec((bm, D), lambda i: (i, 0))
    vec = pl.BlockSpec((1, D), lambda i: (0, 0))
    return pl.pallas_call(
        body, name="rmsmod_bwd", grid=(L // bm,),
        in_specs=[row, row, vec, vec, row], out_specs=[row, vec, vec, vec],
        out_shape=[jax.ShapeDtypeStruct((L, D), f32)] + [jax.ShapeDtypeStruct((1, D), f32)] * 3,
        compiler_params=_cp(("arbitrary",)),
    )(dh, x, g, sc, dres)


def _gate_bwd(dx, y, gate):
    L = dx.shape[0]
    bm = _row_block(L, 256)

    def body(dx_ref, y_ref, gt_ref, dy_ref, dgt_ref):
        i = pl.program_id(0)
        dxv = dx_ref[...]
        dy_ref[...] = (gt_ref[...] * dxv).astype(dy_ref.dtype)
        part = jnp.sum(dxv * y_ref[...], axis=0, keepdims=True)

        @pl.when(i == 0)
        def _():
            dgt_ref[...] = part

        @pl.when(i > 0)
        def _():
            dgt_ref[...] += part

    row = pl.BlockSpec((bm, D), lambda i: (i, 0))
    vec = pl.BlockSpec((1, D), lambda i: (0, 0))
    return pl.pallas_call(
        body, name="gate_bwd", grid=(L // bm,),
        in_specs=[row, row, vec], out_specs=[row, vec],
        out_shape=[jax.ShapeDtypeStruct((L, D), bf16), jax.ShapeDtypeStruct((1, D), f32)],
        compiler_params=_cp(("arbitrary",)),
    )(dx, y, gate)


def _loss_fwd_bwd(x1, down, gate, target):
    L = x1.shape[0]
    bm = _row_block(L, 256)
    nsteps = L // bm

    def body(x_ref, d_ref, gt_ref, t_ref, dy_ref, loss_ref, acc_ref):
        i = pl.program_id(0)
        diff = x_ref[...] + gt_ref[...] * d_ref[...] - t_ref[...]
        dy_ref[...] = diff * (1.0 / D)
        part = jnp.sum(diff * diff, axis=0, keepdims=True)

        @pl.when(i == 0)
        def _():
            acc_ref[...] = part

        @pl.when(i > 0)
        def _():
            acc_ref[...] += part

        @pl.when(i == nsteps - 1)
        def _():
            tot = jnp.sum(acc_ref[...], axis=1, keepdims=True) * (0.5 / D)
            loss_ref[...] = jnp.broadcast_to(tot, (8, 128))

    row = pl.BlockSpec((bm, D), lambda i: (i, 0))
    vec = pl.BlockSpec((1, D), lambda i: (0, 0))
    dy, loss = pl.pallas_call(
        body, name="loss_fwd_bwd", grid=(nsteps,),
        in_specs=[row, row, vec, row],
        out_specs=[row, pl.BlockSpec((8, 128), lambda i: (0, 0))],
        out_shape=[jax.ShapeDtypeStruct((L, D), f32), jax.ShapeDtypeStruct((8, 128), f32)],
        scratch_shapes=[pltpu.VMEM((1, D), f32)],
        compiler_params=_cp(("arbitrary",)),
    )(x1, down, gate, target)
    return loss[0, 0], dy


def _head_mean_matrix():
    r = lax.broadcasted_iota(jnp.int32, (128, 128), 0) // HD
    c = lax.broadcasted_iota(jnp.int32, (128, 128), 1) // HD
    return jnp.where(r == c, 1.0 / HD, 0.0).astype(bf16)


def _qknorm_fwd(p, gqk):
    L = p.shape[0]
    bm = _row_block(L, 256)

    def body(p_ref, g_ref, o_ref):
        e = _head_mean_matrix()
        for c in range(8):
            sl = slice(128 * c, 128 * (c + 1))
            xv = p_ref[:, sl]
            inv = lax.rsqrt(_dot_exact(xv * xv, e) + EPS)
            yv = xv * inv * g_ref[:, sl]
            if c < 4:
                yv = yv * (1.0 / math.sqrt(HD))
            o_ref[:, sl] = yv.astype(o_ref.dtype)
        o_ref[:, 1024:1536] = p_ref[:, 1024:1536].astype(o_ref.dtype)

    return pl.pallas_call(
        body, name="qknorm_fwd", grid=(L // bm,),
        in_specs=[pl.BlockSpec((bm, 1536), lambda i: (i, 0)), pl.BlockSpec((1, 1024), lambda i: (0, 0))],
        out_specs=pl.BlockSpec((bm, 1536), lambda i: (i, 0)),
        out_shape=jax.ShapeDtypeStruct((L, 1536), bf16),
        compiler_params=_cp(("parallel",)),
    )(p, gqk)


def _qknorm_bwd(dqn, dkn, dval, du, p, gqk):
    L = p.shape[0]
    bm = _row_block(L, 256)
    nsteps = L // bm

    def body(dq_ref, dk_ref, dv_ref, du_ref, p_ref, g_ref, o_ref, dg_ref, acc_ref):
        i = pl.program_id(0)
        e = _head_mean_matrix()
        o_ref[:, 1024:1536] = dv_ref[...].astype(o_ref.dtype)
        o_ref[:, 1536:2048] = du_ref[...].astype(o_ref.dtype)
        for c in range(8):
            sl = slice(128 * c, 128 * (c + 1))
            xv = p_ref[:, sl]
            d_ref = dq_ref if c < 4 else dk_ref
            dv = d_ref[:, 128 * (c % 4):128 * (c % 4 + 1)]
            gv = g_ref[:, sl]
            inv = lax.rsqrt(_dot_exact(xv * xv, e) + EPS)
            xh = xv * inv
            dxh = dv * gv
            o_ref[:, sl] = (inv * (dxh - xh * _dot_exact(dxh * xh, e))).astype(o_ref.dtype)
            part = jnp.sum(dv * xh, axis=0, keepdims=True)

            @pl.when(i == 0)
            def _():
                acc_ref[:, sl] = part

            @pl.when(i > 0)
            def _():
                acc_ref[:, sl] += part

        @pl.when(i == nsteps - 1)
        def _():
            r = lax.broadcasted_iota(jnp.int32, (1024, 128), 0)
            col = lax.broadcasted_iota(jnp.int32, (1024, 128), 1)
            fold = jnp.where(col == (r // 512) * HD + r % HD, 1.0, 0.0).astype(bf16)
            dg_ref[...] = _dot_exact(jnp.broadcast_to(acc_ref[...], (8, 1024)), fold)

    half = pl.BlockSpec((bm, 512), lambda i: (i, 0))
    return pl.pallas_call(
        body, name="qknorm_bwd", grid=(nsteps,),
        in_specs=[half, half, half, half, pl.BlockSpec((bm, 1024), lambda i: (i, 0)),
                  pl.BlockSpec((1, 1024), lambda i: (0, 0))],
        out_specs=[pl.BlockSpec((bm, INW), lambda i: (i, 0)), pl.BlockSpec((8, 128), lambda i: (0, 0))],
        out_shape=[jax.ShapeDtypeStruct((L, INW), bf16), jax.ShapeDtypeStruct((8, 128), f32)],
        scratch_shapes=[pltpu.VMEM((1, 1024), f32)],
        compiler_params=_cp(("arbitrary",)),
    )(dqn, dkn, dval, du, p, gqk)


def _outnorm_fwd(oa, os_, ga, gs):
    L = oa.shape[0]
    bm = _row_block(L, 256)

    def body(a_ref, s_ref, ga_ref, gs_ref, o_ref):
        for x_ref, g_ref, off in ((a_ref, ga_ref, 0), (s_ref, gs_ref, 512)):
            xv = x_ref[...]
            inv = lax.rsqrt(jnp.mean(xv * xv, axis=-1, keepdims=True) + EPS)
            o_ref[:, off:off + 512] = (xv * inv * g_ref[...]).astype(o_ref.dtype)

    half = pl.BlockSpec((bm, 512), lambda i: (i, 0))
    vec = pl.BlockSpec((1, 512), lambda i: (0, 0))
    return pl.pallas_call(
        body, name="outnorm_fwd", grid=(L // bm,),
        in_specs=[half, half, vec, vec], out_specs=pl.BlockSpec((bm, D), lambda i: (i, 0)),
        out_shape=jax.ShapeDtypeStruct((L, D), bf16),
        compiler_params=_cp(("parallel",)),
    )(oa, os_, ga, gs)


def _outnorm_bwd(do, oa, os_, ga, gs):
    L = oa.shape[0]
    bm = _row_block(L, 256)

    def body(do_ref, a_ref, s_ref, ga_ref, gs_ref, da_ref, ds_ref, dga_ref, dgs_ref):
        i = pl.program_id(0)
        for x_ref, g_ref, dx_ref, dg_ref, off in ((a_ref, ga_ref, da_ref, dga_ref, 0),
                                                  (s_ref, gs_ref, ds_ref, dgs_ref, 512)):
            xv = x_ref[...]
            dv = do_ref[:, off:off + 512]
            inv = lax.rsqrt(jnp.mean(xv * xv, axis=-1, keepdims=True) + EPS)
            xh = xv * inv
            dxh = dv * g_ref[...]
            dx_ref[...] = inv * (dxh - xh * jnp.mean(dxh * xh, axis=-1, keepdims=True))
            part = jnp.sum(dv * xh, axis=0, keepdims=True)

            @pl.when(i == 0)
            def _():
                dg_ref[...] = part

            @pl.when(i > 0)
            def _():
                dg_ref[...] += part

    half = pl.BlockSpec((bm, 512), lambda i: (i, 0))
    vec = pl.BlockSpec((1, 512), lambda i: (0, 0))
    return pl.pallas_call(
        body, name="outnorm_bwd", grid=(L // bm,),
        in_specs=[pl.BlockSpec((bm, D), lambda i: (i, 0)), half, half, vec, vec],
        out_specs=[half, half, vec, vec],
        out_shape=[jax.ShapeDtypeStruct((L, 512), f32)] * 2 + [jax.ShapeDtypeStruct((1, 512), f32)] * 2,
        compiler_params=_cp(("arbitrary",)),
    )(do, oa, os_, ga, gs)


NCB = DFF // 128


def _conv_rows(L):
    return 64 if L % 64 == 0 else 32


def _conv_chunk(x_ref, w_ref, b_ref, i, nch, R, L, ahead):
    base = pl.multiple_of(i * R, R)
    prev = x_ref[pl.ds(pl.multiple_of(jnp.maximum(base - 8, 0), 8), 8), :]
    parts = [jnp.where(i > 0, prev, 0.0), x_ref[pl.ds(base, R), :]]
    if ahead:
        nxt = x_ref[pl.ds(pl.multiple_of(jnp.minimum(base + R, L - 8), 8), 8), :]
        parts.append(jnp.where(i < nch - 1, nxt, 0.0))
    xe = jnp.concatenate(parts, axis=0)
    x1 = pltpu.roll(xe, 1, 0)
    x2 = pltpu.roll(xe, 2, 0)
    u = b_ref[...] + x2 * w_ref[0:1, :]
    u = u + x1 * w_ref[1:2, :]
    u = u + xe * w_ref[2:3, :]
    return u[8:], xe[8:], x1[8:], x2[8:]


def _fold8(x):
    out = x[0:8]
    for q in range(1, x.shape[0] // 8):
        out = out + x[8 * q:8 * (q + 1)]
    return out


def _convglu_fwd(up_pre, cw, cb):
    L = up_pre.shape[0]
    R = _conv_rows(L)
    nch = L // R

    def body(v_ref, g_ref, wv_ref, wg_ref, bv_ref, bg_ref, a_ref):
        def chunk(i, carry):
            val = _conv_chunk(v_ref, wv_ref, bv_ref, i, nch, R, L, False)[0]
            gate = _conv_chunk(g_ref, wg_ref, bg_ref, i, nch, R, L, False)[0]
            a_ref[pl.ds(pl.multiple_of(i * R, R), R), :] = (_gelu(gate) * val).astype(a_ref.dtype)
            return carry

        lax.fori_loop(0, nch, chunk, 0)

    col = lambda off: pl.BlockSpec((L, 128), lambda j: (0, j + off))
    wsp = lambda off: pl.BlockSpec((3, 128), lambda j: (0, j + off))
    bsp = lambda off: pl.BlockSpec((1, 128), lambda j: (0, j + off))
    return pl.pallas_call(
        body, name="convglu_fwd", grid=(NCB,),
        in_specs=[col(0), col(NCB), wsp(0), wsp(NCB), bsp(0), bsp(NCB)],
        out_specs=pl.BlockSpec((L, 128), lambda j: (0, j)),
        out_shape=jax.ShapeDtypeStruct((L, DFF), bf16),
        compiler_params=_cp(("parallel",)),
    )(up_pre, up_pre, cw, cw, cb, cb)


def _convglu_bwd(da, up_pre, cw, cb):
    L = up_pre.shape[0]
    R = _conv_rows(L)
    nch = L // R
    n = R + 8

    def body(da_ref, v_ref, g_ref, wv_ref, wg_ref, bv_ref, bg_ref, dx_ref, dw_ref, db_ref):
        def chunk(i, acc):
            base = pl.multiple_of(i * R, R)
            val, xv, xv1, xv2 = _conv_chunk(v_ref, wv_ref, bv_ref, i, nch, R, L, True)
            gate, xg, xg1, xg2 = _conv_chunk(g_ref, wg_ref, bg_ref, i, nch, R, L, True)
            nxt = da_ref[pl.ds(pl.multiple_of(jnp.minimum(base + R, L - 8), 8), 8), :]
            dav = jnp.concatenate([da_ref[pl.ds(base, R), :], jnp.where(i < nch - 1, nxt, 0.0)], axis=0)
            c = math.sqrt(2.0 / math.pi)
            t = jnp.tanh(c * (gate + 0.044715 * (gate * gate * gate)))
            d_val = dav * (0.5 * gate * (1.0 + t))
            d_gate = dav * val * (0.5 * (1.0 + t)
                                  + 0.5 * gate * (1.0 - t * t) * c * (1.0 + 3.0 * 0.044715 * (gate * gate)))
            new = []
            for k, (d, w_ref, xs) in enumerate(((d_val, wv_ref, (xv2, xv1, xv)), (d_gate, wg_ref, (xg2, xg1, xg)))):
                dx = d * w_ref[2:3, :] + pltpu.roll(d, n - 1, 0) * w_ref[1:2, :]
                dx = dx + pltpu.roll(d, n - 2, 0) * w_ref[0:1, :]
                dx_ref[k, pl.ds(base, R), :] = dx[:R].astype(dx_ref.dtype)
                dr = d[:R]
                new += [_fold8(dr * x[:R]) for x in xs] + [_fold8(dr)]
            return tuple(a + b for a, b in zip(acc, new))

        acc = lax.fori_loop(0, nch, chunk, (jnp.zeros((8, 128), f32),) * 8)
        for k in range(2):
            for r in range(3):
                dw_ref[k, r:r + 1, :] = jnp.sum(acc[4 * k + r], axis=0, keepdims=True)
            db_ref[k] = jnp.sum(acc[4 * k + 3], axis=0, keepdims=True)

    col = lambda off: pl.BlockSpec((L, 128), lambda j: (0, j + off))
    wsp = lambda off: pl.BlockSpec((3, 128), lambda j: (0, j + off))
    bsp = lambda off: pl.BlockSpec((1, 128), lambda j: (0, j + off))
    dx, dw, db = pl.pallas_call(
        body, name="convglu_bwd", grid=(NCB,),
        in_specs=[col(0), col(0), col(NCB), wsp(0), wsp(NCB), bsp(0), bsp(NCB)],
        out_specs=[pl.BlockSpec((2, L, 128), lambda j: (0, 0, j)), pl.BlockSpec((2, 3, 128), lambda j: (0, 0, j)),
                   pl.BlockSpec((2, 1, 128), lambda j: (0, 0, j))],
        out_shape=[jax.ShapeDtypeStruct((2, L, DFF), bf16), jax.ShapeDtypeStruct((2, 3, DFF), f32),
                   jax.ShapeDtypeStruct((2, 1, DFF), f32)],
        compiler_params=_cp(("parallel",)),
    )(da, up_pre, up_pre, cw, cw, cb, cb)
    return dx, dw.transpose(1, 0, 2).reshape(3, DUP), db.transpose(1, 0, 2).reshape(1, DUP)


def _attn_block(L):
    return 256 if L % 256 == 0 else 128


TAIL_DEAD = -110.0


def _dot_exact2(x, m):
    hi = x.astype(bf16)
    lo = (x - hi.astype(f32)).astype(bf16)
    return (lax.dot_general(hi, m, NN, preferred_element_type=f32)
            + lax.dot_general(lo, m, NN, preferred_element_type=f32))


def _sb_weights(z, mask, tri_gt, carry):
    l1p = jnp.log(1.0 + jnp.exp(-jnp.abs(z)))
    ls_pos = jnp.minimum(z, 0.0) - l1p
    lm = ls_pos - z
    if mask is not None:
        lm = jnp.where(mask, lm, 0.0)
    tail = _dot_exact2(lm, tri_gt) + carry
    w = jnp.exp(ls_pos + tail)
    if mask is not None:
        w = jnp.where(mask, w, 0.0)
    return w, ls_pos, lm


def _head_masks(B):
    lane = lax.broadcasted_iota(jnp.int32, (B, 128), 1)
    return lane < HD, lane >= HD


def _attn_fwd(qkv):
    L = qkv.shape[0]
    B = _attn_block(L)
    nq = L // B

    def body(q_ref, k_ref, v_ref, o_ref):
        qi = pl.program_id(1)
        heads = _head_masks(B)
        ti = lax.broadcasted_iota(jnp.int32, (B, B), 0)
        si = lax.broadcasted_iota(jnp.int32, (B, B), 1)
        tri_gt = jnp.where(ti > si, 1.0, 0.0).astype(bf16)
        diag = si < ti
        qv = q_ref[...]
        zero16 = jnp.zeros((), bf16)

        def tile(jb, mask, acc, carries):
            ks = pl.multiple_of(jb * B, B)
            kb = k_ref[pl.ds(ks, B), :]
            vb = v_ref[pl.ds(ks, B), :]
            out = []
            for in_head, carry in zip(heads, carries):
                kh = jnp.where(in_head, kb, zero16)
                vh = jnp.where(in_head, vb, zero16)
                z = lax.dot_general(qv, kh, NT, preferred_element_type=f32)
                w, _, lm = _sb_weights(z, mask, tri_gt, carry)
                acc = acc + lax.dot_general(w.astype(bf16), vh, NN, preferred_element_type=f32)
                out.append(carry + jnp.sum(lm, axis=1, keepdims=True))
            return acc, out

        zc = jnp.zeros((B, 1), f32)
        acc, (c0, c1) = tile(qi, diag, jnp.zeros((B, 128), f32), (zc, zc))

        def cond(st):
            return jnp.logical_and(st[0] <= qi, st[4] > TAIL_DEAD)

        def step(st):
            n, a, p0, p1, _ = st
            a, (p0, p1) = tile(qi - n, None, a, (p0, p1))
            return n + 1, a, p0, p1, jnp.maximum(jnp.max(p0), jnp.max(p1))

        st = lax.while_loop(cond, step, (jnp.int32(1), acc, c0, c1, jnp.maximum(jnp.max(c0), jnp.max(c1))))
        o_ref[...] = st[1]

    return pl.pallas_call(
        body, name="attn_fwd", grid=(4, nq),
        in_specs=[pl.BlockSpec((B, 128), lambda hp, i: (i, hp)),
                  pl.BlockSpec((L, 128), lambda hp, i: (0, 4 + hp)),
                  pl.BlockSpec((L, 128), lambda hp, i: (0, 8 + hp))],
        out_specs=pl.BlockSpec((B, 128), lambda hp, i: (i, hp)),
        out_shape=jax.ShapeDtypeStruct((L, ATT), f32),
        compiler_params=_cp(("parallel", "parallel")),
    )(qkv, qkv, qkv)


def _attn_bwd(qkv, o, do):
    L = qkv.shape[0]
    B = _attn_block(L)
    nq = L // B

    def body(q_ref, k_ref, v_ref, o_ref, do_ref, dq_ref, dk_ref, dv_ref):
        qi = pl.program_id(1)

        @pl.when(qi == 0)
        def _():
            dk_ref[...] = jnp.zeros_like(dk_ref)
            dv_ref[...] = jnp.zeros_like(dv_ref)

        heads = _head_masks(B)
        ti = lax.broadcasted_iota(jnp.int32, (B, B), 0)
        si = lax.broadcasted_iota(jnp.int32, (B, B), 1)
        tri_gt = jnp.where(ti > si, 1.0, 0.0).astype(bf16)
        tri_ge = jnp.where(ti >= si, 1.0, 0.0).astype(bf16)
        diag = si < ti
        zero16 = jnp.zeros((), bf16)
        qv = q_ref[...]
        do16 = do_ref[...].astype(bf16)
        dsum_lanes = do16.astype(f32) * o_ref[...]
        qhs = [jnp.where(m, qv, zero16) for m in heads]
        dohs = [jnp.where(m, do16, zero16) for m in heads]
        dsums = [jnp.sum(jnp.where(m, dsum_lanes, 0.0), axis=1, keepdims=True) for m in heads]

        def tile(jb, mask, dq_a, state):
            ks = pl.multiple_of(jb * B, B)
            kb = k_ref[pl.ds(ks, B), :]
            vb = v_ref[pl.ds(ks, B), :]
            dk_blk = jnp.zeros((B, 128), f32)
            dv_blk = jnp.zeros((B, 128), f32)
            out = []
            for in_head, qh, doh, dsum, (carry, suffix) in zip(heads, qhs, dohs, dsums, state):
                kh = jnp.where(in_head, kb, zero16)
                vh = jnp.where(in_head, vb, zero16)
                z = lax.dot_general(qv, kh, NT, preferred_element_type=f32)
                w, ls_pos, lm = _sb_weights(z, mask, tri_gt, carry)
                w16 = w.astype(bf16)
                dw = lax.dot_general(doh, vh, NT, preferred_element_type=f32)
                da = w16.astype(f32) * dw
                sig = jnp.exp(ls_pos)
                suf = _dot_exact2(da, tri_ge) + suffix
                dz = da * (1.0 - sig) - sig * (dsum - suf)
                if mask is not None:
                    dz = jnp.where(mask, dz, 0.0)
                dz16 = dz.astype(bf16)
                dq_a = dq_a + lax.dot_general(dz16, kh, NN, preferred_element_type=f32)
                dk_blk = dk_blk + lax.dot_general(dz16, qh, TN, preferred_element_type=f32)
                dv_blk = dv_blk + lax.dot_general(w16, doh, TN, preferred_element_type=f32)
                out.append((carry + jnp.sum(lm, axis=1, keepdims=True),
                            suffix + jnp.sum(da, axis=1, keepdims=True)))
            dk_ref[pl.ds(ks, B), :] += dk_blk
            dv_ref[pl.ds(ks, B), :] += dv_blk
            return dq_a, out

        def alive(state):
            return jnp.maximum(jnp.max(state[0][0]), jnp.max(state[1][0]))

        zc = jnp.zeros((B, 1), f32)
        dq_acc, state = tile(qi, diag, jnp.zeros((B, 128), f32), ((zc, zc), (zc, zc)))

        def cond(st):
            return jnp.logical_and(st[0] <= qi, st[6] > TAIL_DEAD)

        def step(st):
            n, a, c0, s0, c1, s1, _ = st
            a, new = tile(qi - n, None, a, ((c0, s0), (c1, s1)))
            return n + 1, a, new[0][0], new[0][1], new[1][0], new[1][1], alive(new)

        st = lax.while_loop(cond, step, (jnp.int32(1), dq_acc, state[0][0], state[0][1],
                                         state[1][0], state[1][1], alive(state)))
        dq_ref[...] = st[1] * (1.0 / math.sqrt(HD))

    blk = pl.BlockSpec((B, 128), lambda hp, i: (i, hp))
    dqkv = pl.pallas_call(
        body, name="attn_bwd", grid=(4, nq),
        in_specs=[blk,
                  pl.BlockSpec((L, 128), lambda hp, i: (0, 4 + hp)),
                  pl.BlockSpec((L, 128), lambda hp, i: (0, 8 + hp)),
                  blk, blk],
        out_specs=[blk, pl.BlockSpec((L, 128), lambda hp, i: (0, hp)),
                   pl.BlockSpec((L, 128), lambda hp, i: (0, hp))],
        out_shape=[jax.ShapeDtypeStruct((L, ATT), f32)] * 3,
        compiler_params=_cp(("parallel", "arbitrary")),
    )(qkv, qkv, qkv, o, do)
    return dqkv


def _s5_disc(ldt, ar, ai, br, bi):
    dt = jnp.exp(ldt)
    mag = jnp.exp(dt * ar)
    abar_re = mag * jnp.cos(dt * ai)
    abar_im = mag * jnp.sin(dt * ai)
    em_re = abar_re - 1.0
    em_im = abar_im
    den = ar * ar + ai * ai
    f_re = (em_re * ar + em_im * ai) / den
    f_im = (em_im * ar - em_re * ai) / den
    bb_re = f_re * br - f_im * bi
    bb_im = f_re * bi + f_im * br
    return abar_re, abar_im, bb_re, bb_im


def _s5_disc_fwd(ldt, ar, ai, br, bi):
    def body(ldt_ref, ar_ref, ai_ref, br_ref, bi_ref, o1, o2, o3, o4):
        outs = _s5_disc(ldt_ref[...], ar_ref[...], ai_ref[...], br_ref[...], bi_ref[...])
        for o_ref, v in zip((o1, o2, o3, o4), outs):
            o_ref[...] = v

    col = jax.ShapeDtypeStruct((NS, 1), f32)
    mat = jax.ShapeDtypeStruct((NS, NH16), f32)
    return pl.pallas_call(body, name="s5_disc_fwd", out_shape=[col, col, mat, mat],
                          compiler_params=_cp())(ldt, ar, ai, br, bi)


def _s5_disc_bwd(ldt, ar, ai, br, bi, d_are, d_aim, d_bbre, d_bbim):
    def body(ldt_ref, ar_ref, ai_ref, br_ref, bi_ref, c1, c2, c3, c4, o_ldt, o_ar, o_ai, o_br, o_bi):
        prim = (ldt_ref[...], ar_ref[...], ai_ref[...], br_ref[...], bi_ref[...])
        _, vjp = jax.vjp(_s5_disc, *prim)
        g_ldt, g_ar, g_ai, g_br, g_bi = vjp((c1[...], c2[...], c3[...], c4[...]))
        o_ar[...] = g_ar
        o_ai[...] = g_ai
        o_br[...] = g_br
        o_bi[...] = g_bi
        r = lax.broadcasted_iota(jnp.int32, (NG, NS), 0)
        c = lax.broadcasted_iota(jnp.int32, (NG, NS), 1)
        fold = jnp.where(c // NP == r, 1.0, 0.0).astype(bf16)
        hi = jnp.broadcast_to(g_ldt, (NS, 128))
        p1 = hi.astype(bf16)
        r1 = hi - p1.astype(f32)
        p2 = r1.astype(bf16)
        p3 = (r1 - p2.astype(f32)).astype(bf16)
        dd = lambda p: lax.dot_general(fold, p, NN, preferred_element_type=f32)
        o_ldt[...] = dd(p1) + dd(p2) + dd(p3)

    col = jax.ShapeDtypeStruct((NS, 1), f32)
    mat = jax.ShapeDtypeStruct((NS, NH16), f32)
    return pl.pallas_call(
        body, name="s5_disc_bwd",
        out_shape=[jax.ShapeDtypeStruct((NG, 128), f32), col, col, mat, mat],
        compiler_params=_cp())(ldt, ar, ai, br, bi, d_are, d_aim, d_bbre, d_bbim)


SCAN_LANES = 512


def _cmul(ar, ai, br, bi):
    return ar * br - ai * bi, ar * bi + ai * br


def _pow_tables(ar, ai):
    a2 = _cmul(ar, ai, ar, ai)
    a4 = _cmul(*a2, *a2)
    rows = [(ar, ai)]
    for _ in range(7):
        rows.append(_cmul(*rows[-1], ar, ai))
    return (ar, ai), a2, a4, rows


def _scan_rows(xr_ref, xi_ref, abar_re, abar_im, cr_ref, ci_ref, reverse, on_tile=None):
    T = xr_ref.shape[0]
    nt = T // 8
    row = lax.broadcasted_iota(jnp.int32, (8, SCAN_LANES), 0)
    for c in range(NS // SCAN_LANES):
        ls = slice(c * SCAN_LANES, (c + 1) * SCAN_LANES)
        ar = abar_re[:, ls]
        ai = abar_im[:, ls]
        if reverse:
            ai = -ai
        a1, a2, a4, prow = _pow_tables(ar, ai)
        if reverse:
            prow = prow[::-1]
        pr = jnp.concatenate([p[0] for p in prow], axis=0)
        pi = jnp.concatenate([p[1] for p in prow], axis=0)
        a8r, a8i = prow[0] if reverse else prow[7]
        edge = slice(0, 1) if reverse else slice(7, 8)
        steps = []
        for d, (dr, di) in ((1, a1), (2, a2), (4, a4)):
            keep = (row < 8 - d) if reverse else (row >= d)
            steps.append((8 - d if reverse else d, jnp.where(keep, dr, 0.0), jnp.where(keep, di, 0.0)))

        def local(n, _):
            rs = pl.ds(pl.multiple_of(n * 8, 8), 8)
            xr = xr_ref[rs, ls]
            xi = xi_ref[rs, ls]
            for shift, dr, di in steps:
                mr, mi = _cmul(dr, di, pltpu.roll(xr, shift, 0), pltpu.roll(xi, shift, 0))
                xr = xr + mr
                xi = xi + mi
            xr_ref[rs, ls] = xr
            xi_ref[rs, ls] = xi
            return 0

        lax.fori_loop(0, nt, local, 0, unroll=4)

        def chain(n, carry):
            cr, ci = carry
            it = (nt - 1 - n) if reverse else n
            rs = pl.ds(pl.multiple_of(it * 8, 8), 8)
            xr = xr_ref[rs, ls]
            xi = xi_ref[rs, ls]
            nr, ni = _cmul(a8r, a8i, cr, ci)
            nr = nr + xr[edge, :]
            ni = ni + xi[edge, :]
            mr, mi = _cmul(pr, pi, cr, ci)
            xr = xr + mr
            xi = xi + mi
            xr_ref[rs, ls] = xr
            xi_ref[rs, ls] = xi
            if on_tile is not None:
                nxt_r = jnp.where(row < 7, pltpu.roll(xr, 7, 0), cr)
                nxt_i = jnp.where(row < 7, pltpu.roll(xi, 7, 0), ci)
                on_tile(ls, rs, nxt_r, nxt_i)
            return nr, ni

        cr, ci = lax.fori_loop(0, nt, chain, (cr_ref[:, ls], ci_ref[:, ls]), unroll=2)
        cr_ref[:, ls] = cr
        ci_ref[:, ls] = ci


def _s5_chunk(L):
    return 256 if L % 256 == 0 else L


def _s5_gate(s_re, s_im, u, ccr_ref, cci_ref, dsk_ref, gw_ref, gb_ref):
    y = _dot(s_re, ccr_ref[...]) - _dot(s_im, cci_ref[...]) + dsk_ref[...] * u
    yg = _gelu(y)
    gate = _sigmoid(_dot(yg, gw_ref[...]) + gb_ref[...])
    return y, yg, gate


def _s5_fwd(p, abar_re, abar_im, bbr, bbi, ccr, cci, dsk, gw, gb):
    L = p.shape[0]
    T = _s5_chunk(L)

    def body(u_ref, are_ref, aim_ref, bbr_ref, bbi_ref, ccr_ref, cci_ref, dsk_ref, gw_ref, gb_ref,
             o_ref, sr_ref, si_ref, cr_ref, ci_ref):
        @pl.when(pl.program_id(0) == 0)
        def _():
            cr_ref[...] = jnp.zeros_like(cr_ref)
            ci_ref[...] = jnp.zeros_like(ci_ref)

        u = u_ref[...]
        sr_ref[...] = _dot(u, bbr_ref[...])
        si_ref[...] = _dot(u, bbi_ref[...])
        _scan_rows(sr_ref, si_ref, are_ref[...], aim_ref[...], cr_ref, ci_ref, reverse=False)
        _, yg, gate = _s5_gate(sr_ref[...], si_ref[...], u, ccr_ref, cci_ref, dsk_ref, gw_ref, gb_ref)
        o_ref[...] = yg * gate

    full = lambda shape: pl.BlockSpec(shape, lambda i: (0, 0))
    return pl.pallas_call(
        body, name="s5_fwd", grid=(L // T,),
        in_specs=[pl.BlockSpec((T, SSMW), lambda i: (i, 3)), full((1, NS)), full((1, NS)),
                  full((SSMW, NS)), full((SSMW, NS)), full((NS, SSMW)), full((NS, SSMW)),
                  full((1, SSMW)), full((SSMW, SSMW)), full((1, SSMW))],
        out_specs=[pl.BlockSpec((T, SSMW), lambda i: (i, 0)), pl.BlockSpec((T, NS), lambda i: (i, 0)),
                   pl.BlockSpec((T, NS), lambda i: (i, 0))],
        out_shape=[jax.ShapeDtypeStruct((L, SSMW), f32), jax.ShapeDtypeStruct((L, NS), f32),
                   jax.ShapeDtypeStruct((L, NS), f32)],
        scratch_shapes=[pltpu.VMEM((1, NS), f32), pltpu.VMEM((1, NS), f32)],
        compiler_params=_cp(("arbitrary",)),
    )(p, abar_re, abar_im, bbr, bbi, ccr, cci, dsk, gw, gb)


def _diag_fold(acc, rpg):
    aw = acc.shape[0]
    r = lax.broadcasted_iota(jnp.int32, (aw, 128), 0) // rpg
    c = lax.broadcasted_iota(jnp.int32, (aw, 128), 1) // NH16
    fr = lax.broadcasted_iota(jnp.int32, (128, NH16), 0) % NH16
    fc = lax.broadcasted_iota(jnp.int32, (128, NH16), 1)
    fold = jnp.where(fr == fc, 1.0, 0.0).astype(bf16)
    return _dot_exact(jnp.where(r == c, acc, 0.0), fold)


def _s5_bwd(dout, p, s_re, s_im, abar_re, abar_im, bbr, bbi, ccr, cci, dsk, gw, gb):
    L = p.shape[0]
    T = _s5_chunk(L)
    nchunk = L // T

    def body(do_ref, u_ref, sr_ref, si_ref, are_ref, aim_ref, bbr_ref, bbi_ref, ccr_ref, cci_ref,
             dsk_ref, gw_ref, gb_ref,
             du_ref, dare_ref, daim_ref, dd_ref, dgb_ref, dccr_ref, dcci_ref, dbbr_ref, dbbi_ref, dgw_ref,
             lr_ref, li_ref, cr_ref, ci_ref, accr_ref, acci_ref, wacc_ref, gacc_ref):
        i = pl.program_id(0)

        @pl.when(i == 0)
        def _():
            cr_ref[...] = jnp.zeros_like(cr_ref)
            ci_ref[...] = jnp.zeros_like(ci_ref)
            accr_ref[...] = jnp.zeros_like(accr_ref)
            acci_ref[...] = jnp.zeros_like(acci_ref)
            dd_ref[...] = jnp.zeros_like(dd_ref)
            dgb_ref[...] = jnp.zeros_like(dgb_ref)
            wacc_ref[...] = jnp.zeros_like(wacc_ref)
            gacc_ref[...] = jnp.zeros_like(gacc_ref)

        u = u_ref[...]
        dov = do_ref[...]
        y, yg, gate = _s5_gate(sr_ref[...], si_ref[...], u, ccr_ref, cci_ref, dsk_ref, gw_ref, gb_ref)
        dgl = dov * yg * gate * (1.0 - gate)
        dyg = dov * gate + _dot(dgl, gw_ref[...], NT)
        dy = dyg * _gelu_grad(y)
        u16, dy16, yg16, dgl16 = (t.astype(_MXU) for t in (u, dy, yg, dgl))
        for q in range(4):
            lanes = slice(128 * q, 128 * (q + 1))
            wide_l = slice(512 * q, 512 * (q + 1))
            wacc_ref[0, q] += _dot(sr_ref[:, wide_l], dy16[:, lanes], TN)
            wacc_ref[1, q] += _dot(si_ref[:, wide_l], dy16[:, lanes], TN)
            gacc_ref[q] += _dot(yg16[:, lanes], dgl16[:, lanes], TN)
        dd_ref[...] += jnp.sum(dy * u, axis=0, keepdims=True)
        dgb_ref[...] += jnp.sum(dgl, axis=0, keepdims=True)
        lr_ref[...] = _dot(dy, ccr_ref[...], NT)
        li_ref[...] = -_dot(dy, cci_ref[...], NT)

        def on_tile(ls, rs, nxt_r, nxt_i):
            s_r = sr_ref[rs, ls]
            s_i = si_ref[rs, ls]
            accr_ref[:, ls] += nxt_r * s_r + nxt_i * s_i
            acci_ref[:, ls] += nxt_i * s_r - nxt_r * s_i

        _scan_rows(lr_ref, li_ref, are_ref[...], aim_ref[...], cr_ref, ci_ref, reverse=True,
                   on_tile=on_tile)
        for q in range(4):
            lanes = slice(128 * q, 128 * (q + 1))
            wide_l = slice(512 * q, 512 * (q + 1))
            wacc_ref[2, q] += _dot(lr_ref[:, wide_l], u16[:, lanes], TN)
            wacc_ref[3, q] += _dot(li_ref[:, wide_l], u16[:, lanes], TN)
        du_ref[...] = (dy * dsk_ref[...] + _dot(lr_ref[...], bbr_ref[...], NT)
                       + _dot(li_ref[...], bbi_ref[...], NT))

        @pl.when(i == nchunk - 1)
        def _():
            dare_ref[...] = jnp.sum(accr_ref[...], axis=0, keepdims=True)
            daim_ref[...] = jnp.sum(acci_ref[...], axis=0, keepdims=True)
            for m, o_ref in enumerate((dccr_ref, dcci_ref, dbbr_ref, dbbi_ref)):
                for q in range(4):
                    o_ref[512 * q:512 * (q + 1), :] = _diag_fold(wacc_ref[m, q], NP)
            for q in range(4):
                dgw_ref[128 * q:128 * (q + 1), :] = _diag_fold(gacc_ref[q], NH16)

    rev = lambda i: (nchunk - 1 - i, 0)
    full = lambda shape: pl.BlockSpec(shape, lambda i: (0, 0))
    half = pl.BlockSpec((T, SSMW), rev)
    wide = pl.BlockSpec((T, NS), rev)
    return pl.pallas_call(
        body, name="s5_bwd", grid=(nchunk,),
        in_specs=[half, pl.BlockSpec((T, SSMW), lambda i: (nchunk - 1 - i, 3)), wide, wide,
                  full((1, NS)), full((1, NS)), full((SSMW, NS)), full((SSMW, NS)),
                  full((NS, SSMW)), full((NS, SSMW)), full((1, SSMW)), full((SSMW, SSMW)), full((1, SSMW))],
        out_specs=[half, full((1, NS)), full((1, NS)), full((1, SSMW)), full((1, SSMW)),
                   full((NS, NH16)), full((NS, NH16)), full((NS, NH16)), full((NS, NH16)),
                   full((SSMW, NH16))],
        out_shape=[jax.ShapeDtypeStruct((L, SSMW), f32)] + [jax.ShapeDtypeStruct((1, NS), f32)] * 2
                  + [jax.ShapeDtypeStruct((1, SSMW), f32)] * 2 + [jax.ShapeDtypeStruct((NS, NH16), f32)] * 4
                  + [jax.ShapeDtypeStruct((SSMW, NH16), f32)],
        scratch_shapes=[pltpu.VMEM((T, NS), f32), pltpu.VMEM((T, NS), f32),
                        pltpu.VMEM((1, NS), f32), pltpu.VMEM((1, NS), f32),
                        pltpu.VMEM((8, NS), f32), pltpu.VMEM((8, NS), f32),
                        pltpu.VMEM((4, 4, 512, 128), f32), pltpu.VMEM((4, 128, 128), f32)],
        compiler_params=_cp(("arbitrary",)),
    )(dout, p, s_re, s_im, abar_re, abar_im, bbr, bbi, ccr, cci, dsk, gw, gb)


def _block_diag(x):
    g, r, c = x.shape
    eye = jnp.eye(g, dtype=x.dtype)
    return (x[:, :, None, :] * eye[:, None, :, None]).reshape(g * r, g * c)


def _mod_fwd(c16, ada_w, ada_b_loc):
    nloc = ada_w.shape[2]

    def body(c_ref, w_ref, b_ref, o_ref):
        cv = c_ref[...]
        act = cv * _sigmoid(cv)
        o_ref[...] = _dot(act, w_ref[...]) + b_ref[...]

    return pl.pallas_call(
        body, name="mod_fwd", grid=(DEPTH,),
        in_specs=[pl.BlockSpec((16, D), lambda l: (0, 0)), pl.BlockSpec((None, D, nloc), lambda l: (l, 0, 0)),
                  pl.BlockSpec((None, 1, nloc), lambda l: (l, 0, 0))],
        out_specs=pl.BlockSpec((None, 16, nloc), lambda l: (l, 0, 0)),
        out_shape=jax.ShapeDtypeStruct((DEPTH, 16, nloc), f32),
        compiler_params=_cp(("parallel",)),
    )(c16, ada_w, ada_b_loc)


def _ada_grad(c16, dmod16):
    nloc = dmod16.shape[2]

    def body(c_ref, d_ref, o_ref):
        cv = c_ref[...]
        act = cv * _sigmoid(cv)
        o_ref[...] = _dot(act, d_ref[...], TN)

    return pl.pallas_call(
        body, name="ada_grad", grid=(DEPTH,),
        in_specs=[pl.BlockSpec((16, D), lambda l: (0, 0)), pl.BlockSpec((None, 16, nloc), lambda l: (l, 0, 0))],
        out_specs=pl.BlockSpec((None, D, nloc), lambda l: (l, 0, 0)),
        out_shape=jax.ShapeDtypeStruct((DEPTH, D, nloc), f32),
        compiler_params=_cp(("parallel",)),
    )(c16, dmod16)


def _as2d(a):
    return a.reshape(-1, a.shape[-1])


def _ew_rows(rows):
    for cand in (512, 256, 128, 64, 32, 16, 8):
        if rows % cand == 0:
            return cand
    return rows


def _cast_bf16(w):
    w2 = _as2d(w)
    rows, cols = w2.shape
    bm = _ew_rows(rows)

    def body(x_ref, o_ref):
        o_ref[...] = x_ref[...].astype(bf16)

    spec = pl.BlockSpec((bm, cols), lambda i: (i, 0))
    out = pl.pallas_call(body, name="cast_bf16", grid=(rows // bm,), in_specs=[spec], out_specs=spec,
                         out_shape=jax.ShapeDtypeStruct((rows, cols), bf16),
                         compiler_params=_cp(("parallel",)))(w2)
    return out.reshape(w.shape)


def _add2(a, b, out_dtype=f32):
    a2, b2 = _as2d(a), _as2d(b)
    rows, cols = a2.shape
    bm = _ew_rows(rows)

    def body(a_ref, b_ref, o_ref):
        o_ref[...] = (a_ref[...] + b_ref[...]).astype(o_ref.dtype)

    spec = pl.BlockSpec((bm, cols), lambda i: (i, 0))
    out = pl.pallas_call(body, name="add2", grid=(rows // bm,), in_specs=[spec, spec], out_specs=spec,
                         out_shape=jax.ShapeDtypeStruct((rows, cols), out_dtype),
                         compiler_params=_cp(("parallel",)))(a2, b2)
    return out.reshape(a.shape)


def _adamw(parts, w, m, v):
    n = parts.shape[0]
    w2, m2, v2 = _as2d(w), _as2d(m), _as2d(v)
    rows, cols = w2.shape
    p3 = parts.reshape(n, rows, cols)
    bm = _ew_rows(rows)
    if bm * cols * 4 > (1 << 20) and bm % 16 == 0:
        bm //= 2

    def body(p_ref, w_ref, m_ref, v_ref, g_ref, d_ref, nm_ref, nv_ref):
        g = p_ref[0]
        for k in range(1, n):
            g = g + p_ref[k]
        mn = ADAM_B1 * m_ref[...] + (1.0 - ADAM_B1) * g
        vn = ADAM_B2 * v_ref[...] + (1.0 - ADAM_B2) * (g * g)
        m_hat = mn / (1.0 - ADAM_B1 ** ADAM_STEP)
        v_hat = vn / (1.0 - ADAM_B2 ** ADAM_STEP)
        g_ref[...] = g
        d_ref[...] = -ADAM_LR * (m_hat / (jnp.sqrt(v_hat) + ADAM_EPS) + ADAM_WD * w_ref[...])
        nm_ref[...] = mn
        nv_ref[...] = vn

    spec = pl.BlockSpec((bm, cols), lambda i: (i, 0))
    outs = pl.pallas_call(
        body, name="adamw", grid=(rows // bm,),
        in_specs=[pl.BlockSpec((n, bm, cols), lambda i: (0, i, 0)), spec, spec, spec],
        out_specs=[spec] * 4, out_shape=[jax.ShapeDtypeStruct((rows, cols), f32)] * 4,
        compiler_params=_cp(("parallel",)),
    )(p3, w2, m2, v2)
    return tuple(o.reshape(w.shape) for o in outs)


ANY = pl.BlockSpec(memory_space=pl.ANY)


def _my_pos():
    return lax.axis_index("x"), lax.axis_index("y"), lax.axis_index("c")


def _allgather8(name, x):
    def body(x_ref, out_ref, send_sems, recv_sems):
        mx, my, mc = _my_pos()
        me, sibling = (mx, my, mc), (mx, my, 1 - mc)
        chips = [(1 - mx, my), (mx, 1 - my), (1 - mx, 1 - my)]

        def slot(px, py, pc):
            return out_ref.at[4 * px + 2 * py + pc]

        def copy(k, block, to, src=None):
            return pltpu.make_async_remote_copy(
                src_ref=slot(*block) if src is None else src, dst_ref=slot(*block),
                send_sem=send_sems.at[k], recv_sem=recv_sems.at[k], device_id=to, device_id_type=MESH)

        first = [copy(0, me, sibling, src=x_ref)]
        first += [copy(1 + j, me, (*chip, mc), src=x_ref) for j, chip in enumerate(chips)]
        for cp in first:
            cp.start()
        passed = [copy(4 + j, (*chip, mc), sibling) for j, chip in enumerate(chips)]
        for j, chip in enumerate(chips):
            copy(1 + j, (*chip, mc), me).wait_recv()
            passed[j].start()
        copy(0, sibling, me).wait_recv()
        for j, chip in enumerate(chips):
            copy(4 + j, (*chip, 1 - mc), me).wait_recv()
        for cp in first + passed:
            cp.wait_send()

    out = pl.pallas_call(
        body, name=name, in_specs=[ANY], out_specs=ANY,
        out_shape=jax.ShapeDtypeStruct((NDEV,) + x.shape, x.dtype),
        scratch_shapes=[pltpu.SemaphoreType.DMA((7,)), pltpu.SemaphoreType.DMA((7,))],
    )(x)
    dev = 4 * lax.axis_index("x") + 2 * lax.axis_index("y") + lax.axis_index("c")
    return lax.dynamic_update_index_in_dim(out, x, dev, 0)


HALF = DEPTH // 2


def _chip_gather(arrs):
    n = len(arrs)

    def body(*refs):
        ins, outs = refs[:n], refs[n:2 * n]
        send_sems, recv_sems = refs[2 * n:]
        mx, my, mc = _my_pos()
        chips = [(1 - mx, my), (mx, 1 - my), (1 - mx, 1 - my)]
        mine = 2 * mx + my
        my_half = pl.ds(HALF * mc, HALF)
        sib_half = pl.ds(HALF * (1 - mc), HALF)

        def ici(a, j, src_chip, half):
            px, py = chips[j]
            return pltpu.make_async_remote_copy(
                src_ref=ins[a].at[half], dst_ref=outs[a].at[src_chip, half],
                send_sem=send_sems.at[3 * a + j], recv_sem=recv_sems.at[3 * a + j],
                device_id=(px, py, mc), device_id_type=MESH)

        def d2d(a, j, half):
            px, py = chips[j]
            blk = outs[a].at[2 * px + py, half]
            return pltpu.make_async_remote_copy(
                src_ref=blk, dst_ref=blk,
                send_sem=send_sems.at[3 * n + 3 * a + j], recv_sem=recv_sems.at[3 * n + 3 * a + j],
                device_id=(mx, my, 1 - mc), device_id_type=MESH)

        sends = [ici(a, j, mine, my_half) for a in range(n) for j in range(3)]
        for cp in sends:
            cp.start()
        passed = []
        for a in range(n):
            for j, (px, py) in enumerate(chips):
                ici(a, j, 2 * px + py, my_half).wait_recv()
                fwd = d2d(a, j, my_half)
                fwd.start()
                passed.append(fwd)
        for a in range(n):
            for j in range(3):
                d2d(a, j, sib_half).wait_recv()
        for cp in sends + passed:
            cp.wait_send()

    outs = pl.pallas_call(
        body, name="chip_gather", in_specs=[ANY] * n, out_specs=[ANY] * n,
        out_shape=[jax.ShapeDtypeStruct((NCHIP,) + a.shape, a.dtype) for a in arrs],
        scratch_shapes=[pltpu.SemaphoreType.DMA((6 * n,)), pltpu.SemaphoreType.DMA((6 * n,))],
    )(*arrs)
    mine = 2 * lax.axis_index("x") + lax.axis_index("y")
    return [lax.dynamic_update_index_in_dim(o, a, mine, 0) for o, a in zip(outs, arrs)]


def _sibling_half_swap(arrs):
    n = len(arrs)

    def body(*refs):
        ins, outs = refs[:n], refs[n:2 * n]
        send_sems, recv_sems = refs[2 * n:]
        mx, my, mc = _my_pos()
        sib_half = pl.ds(HALF * (1 - mc), HALF)
        cps = [pltpu.make_async_remote_copy(
            src_ref=ins[a].at[j, sib_half], dst_ref=outs[a].at[j],
            send_sem=send_sems.at[NCHIP * a + j], recv_sem=recv_sems.at[NCHIP * a + j],
            device_id=(mx, my, 1 - mc), device_id_type=MESH) for a in range(n) for j in range(NCHIP)]
        for cp in cps:
            cp.start()
        for cp in cps:
            cp.wait()

    return pl.pallas_call(
        body, name="sibling_half_swap", in_specs=[ANY] * n, out_specs=[ANY] * n,
        out_shape=[jax.ShapeDtypeStruct((NCHIP, HALF) + a.shape[2:], a.dtype) for a in arrs],
        scratch_shapes=[pltpu.SemaphoreType.DMA((NCHIP * n,)), pltpu.SemaphoreType.DMA((NCHIP * n,))],
    )(*arrs)


def _sibling_merge_halves(arrs):
    n = len(arrs)

    def body(*refs):
        ins, outs = refs[:n], refs[n:2 * n]
        send_sems, recv_sems = refs[2 * n:]
        mx, my, mc = _my_pos()
        my_half = pl.ds(HALF * mc, HALF)
        cps = [pltpu.make_async_remote_copy(
            src_ref=ins[a], dst_ref=outs[a].at[my_half],
            send_sem=send_sems.at[a], recv_sem=recv_sems.at[a],
            device_id=(mx, my, 1 - mc), device_id_type=MESH) for a in range(n)]
        for cp in cps:
            cp.start()
        for cp in cps:
            cp.wait()

    outs = pl.pallas_call(
        body, name="sibling_merge_halves", in_specs=[ANY] * n, out_specs=[ANY] * n,
        out_shape=[jax.ShapeDtypeStruct((DEPTH,) + a.shape[1:], a.dtype) for a in arrs],
        scratch_shapes=[pltpu.SemaphoreType.DMA((n,)), pltpu.SemaphoreType.DMA((n,))],
    )(*arrs)
    start = HALF * lax.axis_index("c")
    return [lax.dynamic_update_slice_in_dim(o, a, start, 0) for o, a in zip(outs, arrs)]


def _sum_parts(parts):
    n = parts.shape[0]
    p3 = parts.reshape(n, -1, parts.shape[-1])
    rows, cols = p3.shape[1:]
    bm = _ew_rows(rows)
    if bm * cols * 4 > (1 << 20) and bm % 16 == 0:
        bm //= 2

    def body(p_ref, o_ref):
        g = p_ref[0].astype(f32)
        for k in range(1, n):
            g = g + p_ref[k].astype(f32)
        o_ref[...] = g

    out = pl.pallas_call(
        body, name="sum_parts", grid=(rows // bm,),
        in_specs=[pl.BlockSpec((n, bm, cols), lambda i: (0, i, 0))],
        out_specs=pl.BlockSpec((bm, cols), lambda i: (i, 0)),
        out_shape=jax.ShapeDtypeStruct((rows, cols), f32),
        compiler_params=_cp(("parallel",)),
    )(p3)
    return out.reshape(parts.shape[1:])


def _chip_scatter(arrs):
    n = len(arrs)

    def body(*refs):
        ins, outs = refs[:n], refs[n:2 * n]
        send_sems, recv_sems = refs[2 * n:]
        mx, my, mc = _my_pos()
        chips = [(1 - mx, my), (mx, 1 - my), (1 - mx, 1 - my)]
        mine = 2 * mx + my
        sends = []
        for a in range(n):
            for j, (px, py) in enumerate(chips):
                sends.append(pltpu.make_async_remote_copy(
                    src_ref=ins[a].at[2 * px + py], dst_ref=outs[a].at[mine],
                    send_sem=send_sems.at[3 * a + j], recv_sem=recv_sems.at[3 * a + j],
                    device_id=(px, py, mc), device_id_type=MESH))
        for cp in sends:
            cp.start()
        for a in range(n):
            for j, (px, py) in enumerate(chips):
                pltpu.make_async_remote_copy(
                    src_ref=ins[a].at[mine], dst_ref=outs[a].at[2 * px + py],
                    send_sem=send_sems.at[3 * a + j], recv_sem=recv_sems.at[3 * a + j],
                    device_id=(px, py, mc), device_id_type=MESH).wait_recv()
        for cp in sends:
            cp.wait_send()

    outs = pl.pallas_call(
        body, name="chip_scatter", in_specs=[ANY] * n, out_specs=[ANY] * n,
        out_shape=[jax.ShapeDtypeStruct(a.shape, a.dtype) for a in arrs],
        scratch_shapes=[pltpu.SemaphoreType.DMA((3 * n,)), pltpu.SemaphoreType.DMA((3 * n,))],
    )(*arrs)
    mine = 2 * lax.axis_index("x") + lax.axis_index("y")
    return [lax.dynamic_update_index_in_dim(o, lax.dynamic_index_in_dim(a, mine, 0, keepdims=False), mine, 0)
            for o, a in zip(outs, arrs)]


def _layer_fwd(x, res, res_gate, wl, mod):
    L = x.shape[0]
    bm = _row_block(L, 1024)
    nb = L // bm
    sh1, sc1, g1, sh2, sc2, g2 = [mod[k * D:(k + 1) * D][None, :] for k in range(NMOD)]
    sv = {}
    xin, h = _rmsmod_fwd(x, res, res_gate, wl["norm1_g"], sh1, sc1)
    if xin is None:
        xin = x
    p = _mm("mm_in", h, wl["w_in"], dims=NN, grid=(nb, 4, 1),
            a_block=(bm, D), a_map=lambda i, j, k: (i, 0),
            b_block=(None, D, 512), b_map=lambda i, j, k: (j, 0, 0),
            out_shape=(L, INW), o_block=(bm, 512), o_map=lambda i, j, k: (i, j))
    qkv = _qknorm_fwd(p, wl["gqk"])
    oa = _attn_fwd(qkv)
    os_, s_re, s_im = _s5_fwd(p, wl["abar_re"], wl["abar_im"], wl["bbr"], wl["bbi"], wl["ccr"], wl["cci"],
                              wl["dsk"], wl["gw"], wl["gb"])
    o = _outnorm_fwd(oa, os_, wl["attn_out_g"], wl["ssm_out_g"])
    proj = _mm("mm_out", o, wl["w_out"], dims=NN, grid=(nb, 2, 1),
               a_block=(bm, D), a_map=lambda i, j, k: (i, 0),
               b_block=(D, 512), b_map=lambda i, j, k: (0, j),
               out_shape=(L, D), o_block=(bm, 512), o_map=lambda i, j, k: (i, j))
    x1, h2 = _rmsmod_fwd(xin, proj, g1, wl["norm2_g"], sh2, sc2)
    up_pre = _mm("mm_up", h2, wl["w_up"], dims=NN, grid=(nb, 4, 1),
                 a_block=(bm, D), a_map=lambda i, j, k: (i, 0),
                 b_block=(None, D, 1408), b_map=lambda i, j, k: (j, 0, 0),
                 out_shape=(L, DUP), o_block=(bm, 1408), o_map=lambda i, j, k: (i, j))
    a = _convglu_fwd(up_pre, wl["conv_w"], wl["conv_b"])
    down = _mm("mm_down", a, wl["w_down"], dims=NN, grid=(nb, 2, 1),
               a_block=(bm, DFF), a_map=lambda i, j, k: (i, 0),
               b_block=(DFF, 512), b_map=lambda i, j, k: (0, j),
               out_shape=(L, D), o_block=(bm, 512), o_map=lambda i, j, k: (i, j))
    sv.update(xin=xin, h=h, p=p, qkv=qkv, oa=oa, os=os_, s_re=s_re, s_im=s_im, o=o, proj=proj,
              x1=x1, h2=h2, up_pre=up_pre, a=a, down=down,
              sc1=sc1, g1=g1, sc2=sc2, g2=g2)
    return x1, down, g2, sv


def _layer_bwd(dx_out, wl, sv):
    L = dx_out.shape[0]
    bm = _row_block(L, 1024)
    nb = L // bm
    bk = _row_block(L, 512)
    nk = L // bk
    g = {}
    dd, dg2 = _gate_bwd(dx_out, sv["down"], sv["g2"])
    da = _mm("mm_down_dx", dd, wl["w_down"], dims=NT, grid=(nb, 2, 1),
             a_block=(bm, D), a_map=lambda i, j, k: (i, 0),
             b_block=(1408, D), b_map=lambda i, j, k: (j, 0),
             out_shape=(L, DFF), o_block=(bm, 1408), o_map=lambda i, j, k: (i, j))
    g["w_down"] = _mm("mm_down_dw", sv["a"], dd, dims=TN, grid=(1, 2, nk),
                      a_block=(bk, DFF), a_map=lambda i, j, k: (k, 0),
                      b_block=(bk, 512), b_map=lambda i, j, k: (k, j),
                      out_shape=(DFF, D), o_block=(DFF, 512), o_map=lambda i, j, k: (0, j))
    dup, g["conv_w"], g["conv_b"] = _convglu_bwd(da, sv["up_pre"], wl["conv_w"], wl["conv_b"])
    dh2 = _mm("mm_up_dx", dup, wl["w_up"], dims=NT, grid=(nb, 1, 4),
              a_block=(None, bm, 1408), a_map=lambda i, j, k: (k // 2, i, k % 2),
              b_block=(None, D, 1408), b_map=lambda i, j, k: (k, 0, 0),
              out_shape=(L, D), o_block=(bm, D), o_map=lambda i, j, k: (i, 0))
    g["w_up"] = _mm("mm_up_dw", sv["h2"], dup, dims=TN, grid=(1, 4, nk),
                    a_block=(bk, D), a_map=lambda i, j, k: (k, 0),
                    b_block=(None, bk, 1408), b_map=lambda i, j, k: (j // 2, k, j % 2),
                    out_shape=(NCHIP, D, 1408), o_block=(None, D, 1408), o_map=lambda i, j, k: (j, 0, 0))
    dx1, dsh2, dsc2, g["norm2_g"] = _rmsmod_bwd(dh2, sv["x1"], wl["norm2_g"], sv["sc2"], dx_out)
    dproj, dg1 = _gate_bwd(dx1, sv["proj"], sv["g1"])
    do = _mm("mm_out_dx", dproj, wl["w_out"], dims=NT, grid=(nb, 2, 1),
             a_block=(bm, D), a_map=lambda i, j, k: (i, 0),
             b_block=(512, D), b_map=lambda i, j, k: (j, 0),
             out_shape=(L, D), o_block=(bm, 512), o_map=lambda i, j, k: (i, j))
    g["w_out"] = _mm("mm_out_dw", sv["o"], dproj, dims=TN, grid=(1, 2, nk),
                     a_block=(bk, D), a_map=lambda i, j, k: (k, 0),
                     b_block=(bk, 512), b_map=lambda i, j, k: (k, j),
                     out_shape=(D, D), o_block=(D, 512), o_map=lambda i, j, k: (0, j))
    doa, dos, g["attn_out_g"], g["ssm_out_g"] = _outnorm_bwd(do, sv["oa"], sv["os"],
                                                             wl["attn_out_g"], wl["ssm_out_g"])
    dqn, dkn, dv = _attn_bwd(sv["qkv"], sv["oa"], doa)
    (du, d_abar_re, d_abar_im, g["ssm_d"], g["glu_b"], d_ccr, d_cci, d_bbre, d_bbim, d_gw) = _s5_bwd(
        dos, sv["p"], sv["s_re"], sv["s_im"], wl["abar_re"], wl["abar_im"], wl["bbr"], wl["bbi"],
        wl["ccr"], wl["cci"], wl["dsk"], wl["gw"], wl["gb"])
    g["ssm_c_re"] = d_ccr.reshape(NG, NP, NH16).transpose(0, 2, 1)
    g["ssm_c_im"] = -d_cci.reshape(NG, NP, NH16).transpose(0, 2, 1)
    g["glu_w"] = d_gw.reshape(NG, NH16, NH16)
    d_ldt, d_ar, d_ai, d_br, d_bi = _s5_disc_bwd(
        wl["ldt_col"], wl["ar_col"], wl["ai_col"], wl["br_mat"], wl["bi_mat"],
        d_abar_re.reshape(NS, 1), d_abar_im.reshape(NS, 1), d_bbre, d_bbim)
    g["ssm_log_dt"] = d_ldt[:, 0]
    g["ssm_a_re"] = d_ar.reshape(NG, NP)
    g["ssm_a_im"] = d_ai.reshape(NG, NP)
    g["ssm_b_re"] = d_br.reshape(NG, NP, NH16)
    g["ssm_b_im"] = d_bi.reshape(NG, NP, NH16)
    dp, dgqk = _qknorm_bwd(dqn, dkn, dv, du, sv["p"], wl["gqk"])
    g["q_norm_g"] = dgqk[0, :HD]
    g["k_norm_g"] = dgqk[0, HD:2 * HD]
    dh = _mm("mm_in_dx", dp, wl["w_in"], dims=NT, grid=(nb, 1, 4),
             a_block=(bm, 512), a_map=lambda i, j, k: (i, k),
             b_block=(None, D, 512), b_map=lambda i, j, k: (k, 0, 0),
             out_shape=(L, D), o_block=(bm, D), o_map=lambda i, j, k: (i, 0))
    g["w_in"] = _mm("mm_in_dw", sv["h"], dp, dims=TN, grid=(1, 4, nk),
                    a_block=(bk, D), a_map=lambda i, j, k: (k, 0),
                    b_block=(bk, 512), b_map=lambda i, j, k: (k, j),
                    out_shape=(NCHIP, D, 512), o_block=(None, D, 512), o_map=lambda i, j, k: (j, 0, 0))
    dx_in, dsh1, dsc1, g["norm1_g"] = _rmsmod_bwd(dh, sv["xin"], wl["norm1_g"], sv["sc1"], dx1)
    g["dmod"] = jnp.concatenate([dsh1, dsc1, dg1, dsh2, dsc2, dg2], axis=1)[0]
    return dx_in, g


def _prep_layer(l, big, small):
    wl = {}
    wl["w_in"] = big["w_in"][:, l]
    wl["w_up"] = big["w_up"][:, l]
    wl["w_out"] = big["w_out"][:, l].reshape(D, D)
    wl["w_down"] = big["w_down"][:, l].reshape(DFF, D)
    wl["conv_w"] = big["conv_w"][:, l].transpose(1, 0, 2).reshape(3, DUP)
    wl["conv_b"] = small["ffn_conv_b"][l][None, :]
    wl["norm1_g"] = small["norm1_g"][l][None, :]
    wl["norm2_g"] = small["norm2_g"][l][None, :]
    wl["attn_out_g"] = small["attn_out_g"][l][None, :]
    wl["ssm_out_g"] = small["ssm_out_g"][l][None, :]
    wl["gqk"] = jnp.concatenate([jnp.tile(small["q_norm_g"][l], 8), jnp.tile(small["k_norm_g"][l], 8)])[None, :]
    wl["ldt_col"] = jnp.repeat(small["ssm_log_dt"][l], NP)[:, None]
    wl["ar_col"] = small["ssm_a_re"][l].reshape(NS, 1)
    wl["ai_col"] = small["ssm_a_im"][l].reshape(NS, 1)
    wl["br_mat"] = small["ssm_b_re"][l].reshape(NS, NH16)
    wl["bi_mat"] = small["ssm_b_im"][l].reshape(NS, NH16)
    abar_re, abar_im, bb_re, bb_im = _s5_disc_fwd(wl["ldt_col"], wl["ar_col"], wl["ai_col"],
                                                  wl["br_mat"], wl["bi_mat"])
    wl["abar_re"] = abar_re.reshape(1, NS)
    wl["abar_im"] = abar_im.reshape(1, NS)
    wl["bbr"] = _block_diag(bb_re.reshape(NG, NP, NH16).transpose(0, 2, 1)).astype(bf16)
    wl["bbi"] = _block_diag(bb_im.reshape(NG, NP, NH16).transpose(0, 2, 1)).astype(bf16)
    wl["ccr"] = _block_diag(small["ssm_c_re"][l].transpose(0, 2, 1)).astype(bf16)
    wl["cci"] = _block_diag(small["ssm_c_im"][l].transpose(0, 2, 1)).astype(bf16)
    wl["gw"] = _block_diag(small["glu_w"][l]).astype(bf16)
    wl["dsk"] = small["ssm_d"][l].reshape(1, SSMW)
    wl["gb"] = small["glu_b"][l].reshape(1, SSMW)
    return wl


def _local_step(x, target, mods, big, small):
    wls = [_prep_layer(l, big, small) for l in range(DEPTH)]
    saved = []
    cur, res, res_gate = x, None, None
    for l in range(DEPTH):
        cur, res, res_gate, sv = _layer_fwd(cur, res, res_gate, wls[l], mods[l])
        saved.append(sv)
    loss, dx = _loss_fwd_bwd(cur, res, res_gate, target)
    grads = [None] * DEPTH
    for l in reversed(range(DEPTH)):
        dx, grads[l] = _layer_bwd(dx, wls[l], saved[l])
    return loss, dx, grads


BIG = ("w_in", "w_out", "ffn_w_up", "ffn_w_down")
SMALL = ("ada_b", "norm1_g", "q_norm_g", "k_norm_g", "ssm_a_re", "ssm_a_im", "ssm_log_dt",
         "ssm_b_re", "ssm_b_im", "ssm_c_re", "ssm_c_im", "ssm_d", "glu_w", "glu_b",
         "attn_out_g", "ssm_out_g", "norm2_g", "ffn_conv_b")
NAMES = ("ada_w", "ada_b", "norm1_g", "w_in", "q_norm_g", "k_norm_g", "ssm_a_re", "ssm_a_im",
         "ssm_log_dt", "ssm_b_re", "ssm_b_im", "ssm_c_re", "ssm_c_im", "ssm_d", "glu_w", "glu_b",
         "attn_out_g", "ssm_out_g", "w_out", "norm2_g", "ffn_w_up", "ffn_conv_w", "ffn_conv_b",
         "ffn_w_down")
PACK_COLS = 1024


def _pack(arrs):
    flat = jnp.concatenate([a.reshape(-1) for a in arrs])
    rows = -(-flat.shape[0] // PACK_COLS)
    rows = -(-rows // 8) * 8
    flat = jnp.pad(flat, (0, rows * PACK_COLS - flat.shape[0]))
    return flat.reshape(rows, PACK_COLS)


def _unpack(packed, shapes):
    flat = packed.reshape(-1)
    out, off = [], 0
    for s in shapes:
        n = math.prod(s)
        out.append(flat[off:off + n].reshape(s))
        off += n
    return out


def kernel(x, c, ada_w, ada_b, norm1_g, w_in, q_norm_g, k_norm_g, ssm_a_re, ssm_a_im, ssm_log_dt, ssm_b_re, ssm_b_im, ssm_c_re, ssm_c_im, ssm_d, glu_w, glu_b, attn_out_g, ssm_out_g, w_out, norm2_g, ffn_w_up, ffn_conv_w, ffn_conv_b, ffn_w_down, loss_target, m_ada_w, m_ada_b, m_norm1_g, m_w_in, m_q_norm_g, m_k_norm_g, m_ssm_a_re, m_ssm_a_im, m_ssm_log_dt, m_ssm_b_re, m_ssm_b_im, m_ssm_c_re, m_ssm_c_im, m_ssm_d, m_glu_w, m_glu_b, m_attn_out_g, m_ssm_out_g, m_w_out, m_norm2_g, m_ffn_w_up, m_ffn_conv_w, m_ffn_conv_b, m_ffn_w_down, v_ada_w, v_ada_b, v_norm1_g, v_w_in, v_q_norm_g, v_k_norm_g, v_ssm_a_re, v_ssm_a_im, v_ssm_log_dt, v_ssm_b_re, v_ssm_b_im, v_ssm_c_re, v_ssm_c_im, v_ssm_d, v_glu_w, v_glu_b, v_attn_out_g, v_ssm_out_g, v_w_out, v_norm2_g, v_ffn_w_up, v_ffn_conv_w, v_ffn_conv_b, v_ffn_w_down):
    env = dict(locals())
    w = {n: env[n] for n in NAMES}
    m = {n: env["m_" + n] for n in NAMES}
    v = {n: env["v_" + n] for n in NAMES}
    mx, my, mc = _my_pos()
    chip = 2 * mx + my
    dev = 4 * mx + 2 * my + mc
    xl = x[0]
    tl = loss_target[0]

    c_all = _allgather8("gather_c", jnp.pad(c, ((0, 7), (0, 0))))[:, 0, :]
    c16 = jnp.pad(c_all, ((0, 8), (0, 0)))
    nloc = NMOD * D // NCHIP
    ada_b_loc = lax.dynamic_slice(ada_b, (0, chip * nloc), (DEPTH, nloc))[:, None, :]
    mod_loc = _mod_fwd(c16, ada_w, ada_b_loc)
    mod_g = _allgather8("gather_mod", mod_loc.reshape(DEPTH * 16, nloc))
    mod_all = mod_g[0::2].reshape(NCHIP, DEPTH, 16, nloc).transpose(1, 2, 0, 3).reshape(DEPTH, 16, NMOD * D)
    mods = lax.dynamic_index_in_dim(mod_all, dev, axis=1, keepdims=False)

    gathered = _chip_gather([_cast_bf16(w["w_in"]), _cast_bf16(w["w_out"]), _cast_bf16(w["ffn_w_up"]),
                             _cast_bf16(w["ffn_w_down"]), w["ffn_conv_w"]])
    big = dict(zip(("w_in", "w_out", "w_up", "w_down", "conv_w"), gathered))
    small = {n: w[n] for n in SMALL}

    loss_loc, grad_x, grads = _local_step(xl, tl, mods, big, small)
    loss = lax.psum(loss_loc, ("x", "y", "c"))

    def stacked(name):
        return jnp.stack([grads[l][name] for l in range(DEPTH)], axis=1)

    loc = [stacked("w_in"),
           jnp.stack([grads[l]["w_out"].reshape(NCHIP, D // NCHIP, D) for l in range(DEPTH)], axis=1),
           stacked("w_up"),
           jnp.stack([grads[l]["w_down"].reshape(NCHIP, DFF // NCHIP, D) for l in range(DEPTH)], axis=1)]
    recv = _sibling_half_swap(loc)
    mine_half = [lax.dynamic_slice_in_dim(a, HALF * mc, HALF, axis=1) for a in loc]
    chip_sum = [_add2(a, b, bf16) for a, b in zip(mine_half, recv)]
    parts = _chip_scatter(chip_sum)
    g_full = _sibling_merge_halves([_sum_parts(p) for p in parts])
    outs = {}
    for name, g_big in zip(BIG, g_full):
        outs[name] = _adamw(g_big[None], w[name], m[name], v[name])

    small_names = SMALL[1:] + ("ffn_conv_w_full",)
    small_grads = []
    for n in SMALL[1:]:
        key = {"ffn_conv_b": "conv_b"}.get(n, n)
        small_grads.append(jnp.stack([grads[l][key].reshape(w[n].shape[1:]) for l in range(DEPTH)]))
    small_grads.append(jnp.stack([grads[l]["conv_w"] for l in range(DEPTH)]))
    dmod = jnp.stack([grads[l]["dmod"] for l in range(DEPTH)])
    packed = _pack([dmod] + small_grads)
    allp = _allgather8("gather_small", packed)
    shapes = [dmod.shape] + [a.shape for a in small_grads]
    dmod_all = allp[:, :DEPTH * NMOD, :].reshape(NDEV, DEPTH, NMOD * D)
    dmod_loc = lax.dynamic_slice(dmod_all, (0, 0, chip * nloc), (NDEV, DEPTH, nloc)).transpose(1, 0, 2)
    g_ada = _ada_grad(c16, jnp.pad(dmod_loc, ((0, 0), (0, 8), (0, 0))))
    outs["ada_w"] = _adamw(g_ada[None], w["ada_w"], m["ada_w"], v["ada_w"])
    conv_w_shape = (DEPTH, 3, DUP)
    w_small = _pack([w[n] for n in SMALL] + [jnp.zeros(conv_w_shape, f32)])
    m_small = _pack([m[n] for n in SMALL] + [jnp.zeros(conv_w_shape, f32)])
    v_small = _pack([v[n] for n in SMALL] + [jnp.ones(conv_w_shape, f32)])
    res_small = _adamw(allp, w_small, m_small, v_small)
    unpacked = [_unpack(r, shapes) for r in res_small]
    for i, n in enumerate(SMALL):
        outs[n] = tuple(unpacked[k][i] for k in range(4))
    g_conv_full = unpacked[0][len(SMALL)]
    ncw = DUP // NCHIP
    g_conv = lax.dynamic_slice(g_conv_full, (0, 0, chip * ncw), (DEPTH, 3, ncw))
    outs["ffn_conv_w"] = _adamw(g_conv[None], w["ffn_conv_w"], m["ffn_conv_w"], v["ffn_conv_w"])

    result = [loss, grad_x[None]]
    for k in range(4):
        result += [outs[n][k] for n in NAMES]
    return tuple(result)
```

```python
import functools
import math

import jax
import jax.numpy as jnp
from jax import lax
from jax.experimental import pallas as pl
from jax.experimental.pallas import tpu as pltpu

f32 = jnp.float32
bf16 = jnp.bfloat16
_MXU = jnp.bfloat16

D = 1024
ATT = 512
SSMW = 512
HD = 64
NG = 32
NP = 64
NH16 = 16
NS = NG * NP
INW = 2048
DFF = 2816
DUP = 2 * DFF
NMOD = 6
DEPTH = 4
EPS = 1e-6
NCHIP = 4
NDEV = 8

ADAM_LR = 0.001
ADAM_B1 = 0.9
ADAM_B2 = 0.999
ADAM_EPS = 1e-08
ADAM_WD = 0.01
ADAM_STEP = 10

VMEM_LIMIT = 56 * 1024 * 1024
MESH = pl.DeviceIdType.MESH

NN = (((1,), (0,)), ((), ()))
NT = (((1,), (1,)), ((), ()))
TN = (((0,), (0,)), ((), ()))


def _cp(sem=None):
    if sem is None:
        return pltpu.CompilerParams(vmem_limit_bytes=VMEM_LIMIT)
    return pltpu.CompilerParams(dimension_semantics=sem, vmem_limit_bytes=VMEM_LIMIT)


def _dot(a, b, dims=NN):
    return lax.dot_general(a.astype(_MXU), b.astype(_MXU), dims, preferred_element_type=f32)


def _dot_exact(x, m):
    hi = x.astype(bf16)
    r1 = x - hi.astype(f32)
    mid = r1.astype(bf16)
    lo = (r1 - mid.astype(f32)).astype(bf16)
    d = lambda p: lax.dot_general(p, m, NN, preferred_element_type=f32)
    return d(hi) + d(mid) + d(lo)


def _gelu(x):
    c = math.sqrt(2.0 / math.pi)
    return 0.5 * x * (1.0 + jnp.tanh(c * (x + 0.044715 * (x * x * x))))


def _gelu_grad(x):
    c = math.sqrt(2.0 / math.pi)
    t = jnp.tanh(c * (x + 0.044715 * (x * x * x)))
    return 0.5 * (1.0 + t) + 0.5 * x * (1.0 - t * t) * c * (1.0 + 3.0 * 0.044715 * (x * x))


def _sigmoid(x):
    return 1.0 / (1.0 + jnp.exp(-x))


def _mm(name, a, b, *, dims, grid, a_block, a_map, b_block, b_map, out_shape, o_block, o_map,
        out_dtype=f32):
    nk = grid[2]
    acc_shape = tuple(s for s in o_block if s is not None)

    def body(a_ref, b_ref, o_ref, acc_ref):
        k = pl.program_id(2)
        part = _dot(a_ref[...], b_ref[...], dims)
        if nk == 1:
            o_ref[...] = part.astype(o_ref.dtype)
        else:
            @pl.when(k == 0)
            def _():
                acc_ref[...] = part

            @pl.when(k > 0)
            def _():
                acc_ref[...] += part

            @pl.when(k == nk - 1)
            def _():
                o_ref[...] = acc_ref[...].astype(o_ref.dtype)

    return pl.pallas_call(
        body, name=name, grid=grid,
        in_specs=[pl.BlockSpec(a_block, a_map), pl.BlockSpec(b_block, b_map)],
        out_specs=pl.BlockSpec(o_block, o_map),
        out_shape=jax.ShapeDtypeStruct(out_shape, out_dtype),
        scratch_shapes=[pltpu.VMEM(acc_shape if nk > 1 else (8, 128), f32)],
        compiler_params=_cp(("parallel", "parallel", "arbitrary")),
    )(a, b)


def _row_block(L, want):
    return want if L % want == 0 else L


def _rmsmod_fwd(x, res, gate, g, sh, sc):
    L = x.shape[0]
    bm = _row_block(L, 256)
    with_res = res is not None

    def body(*refs):
        if with_res:
            x_ref, r_ref, gt_ref, g_ref, sh_ref, sc_ref, xo_ref, h_ref = refs
            xin = x_ref[...] + gt_ref[...] * r_ref[...]
            xo_ref[...] = xin
        else:
            x_ref, g_ref, sh_ref, sc_ref, h_ref = refs
            xin = x_ref[...]
        inv = lax.rsqrt(jnp.mean(xin * xin, axis=-1, keepdims=True) + EPS)
        xn = xin * inv * g_ref[...]
        h_ref[...] = (xn * (1.0 + sc_ref[...]) + sh_ref[...]).astype(h_ref.dtype)

    row = pl.BlockSpec((bm, D), lambda i: (i, 0))
    vec = pl.BlockSpec((1, D), lambda i: (0, 0))
    if with_res:
        return pl.pallas_call(
            body, name="rmsmod_res_fwd", grid=(L // bm,),
            in_specs=[row, row, vec, vec, vec, vec], out_specs=[row, row],
            out_shape=[jax.ShapeDtypeStruct((L, D), f32), jax.ShapeDtypeStruct((L, D), bf16)],
            compiler_params=_cp(("parallel",)),
        )(x, res, gate, g, sh, sc)
    h = pl.pallas_call(
        body, name="rmsmod_fwd", grid=(L // bm,),
        in_specs=[row, vec, vec, vec], out_specs=row,
        out_shape=jax.ShapeDtypeStruct((L, D), bf16),
        compiler_params=_cp(("parallel",)),
    )(x, g, sh, sc)
    return None, h


def _rmsmod_bwd(dh, x, g, sc, dres):
    L = x.shape[0]
    bm = _row_block(L, 256)

    def body(dh_ref, x_ref, g_ref, sc_ref, dr_ref, dx_ref, dsh_ref, dsc_ref, dg_ref):
        i = pl.program_id(0)
        xv = x_ref[...]
        dhv = dh_ref[...]
        inv = lax.rsqrt(jnp.mean(xv * xv, axis=-1, keepdims=True) + EPS)
        xh = xv * inv
        gv = g_ref[...]
        xn = xh * gv
        dxn = dhv * (1.0 + sc_ref[...])
        dxh = dxn * gv
        dx_ref[...] = inv * (dxh - xh * jnp.mean(dxh * xh, axis=-1, keepdims=True)) + dr_ref[...]
        p_sh = jnp.sum(dhv, axis=0, keepdims=True)
        p_sc = jnp.sum(dhv * xn, axis=0, keepdims=True)
        p_g = jnp.sum(dxn * xh, axis=0, keepdims=True)

        @pl.when(i == 0)
        def _():
            dsh_ref[...] = p_sh
            dsc_ref[...] = p_sc
            dg_ref[...] = p_g

        @pl.when(i > 0)
        def _():
            dsh_ref[...] += p_sh
            dsc_ref[...] += p_sc
            dg_ref[...] += p_g

    row = pl.BlockSpec((bm, D), lambda i: (i, 0))
    vec = pl.BlockSpec((1, D), lambda i: (0, 0))
    return pl.pallas_call(
        body, name="rmsmod_bwd", grid=(L // bm,),
        in_specs=[row, row, vec, vec, row], out_specs=[row, vec, vec, vec],
        out_shape=[jax.ShapeDtypeStruct((L, D), f32)] + [jax.ShapeDtypeStruct((1, D), f32)] * 3,
        compiler_params=_cp(("arbitrary",)),
    )(dh, x, g, sc, dres)


def _gate_bwd(dx, y, gate):
    L = dx.shape[0]
    bm = _row_block(L, 256)

    def body(dx_ref, y_ref, gt_ref, dy_ref, dgt_ref):
        i = pl.program_id(0)
        dxv = dx_ref[...]
        dy_ref[...] = (gt_ref[...] * dxv).astype(dy_ref.dtype)
        part = jnp.sum(dxv * y_ref[...], axis=0, keepdims=True)

        @pl.when(i == 0)
        def _():
            dgt_ref[...] = part

        @pl.when(i > 0)
        def _():
            dgt_ref[...] += part

    row = pl.BlockSpec((bm, D), lambda i: (i, 0))
    vec = pl.BlockSpec((1, D), lambda i: (0, 0))
    return pl.pallas_call(
        body, name="gate_bwd", grid=(L // bm,),
        in_specs=[row, row, vec], out_specs=[row, vec],
        out_shape=[jax.ShapeDtypeStruct((L, D), bf16), jax.ShapeDtypeStruct((1, D), f32)],
        compiler_params=_cp(("arbitrary",)),
    )(dx, y, gate)


def _loss_fwd_bwd(x1, down, gate, target):
    L = x1.shape[0]
    bm = _row_block(L, 256)
    nsteps = L // bm

    def body(x_ref, d_ref, gt_ref, t_ref, dy_ref, loss_ref, acc_ref):
        i = pl.program_id(0)
        diff = x_ref[...] + gt_ref[...] * d_ref[...] - t_ref[...]
        dy_ref[...] = diff * (1.0 / D)
        part = jnp.sum(diff * diff, axis=0, keepdims=True)

        @pl.when(i == 0)
        def _():
            acc_ref[...] = part

        @pl.when(i > 0)
        def _():
            acc_ref[...] += part

        @pl.when(i == nsteps - 1)
        def _():
            tot = jnp.sum(acc_ref[...], axis=1, keepdims=True) * (0.5 / D)
            loss_ref[...] = jnp.broadcast_to(tot, (8, 128))

    row = pl.BlockSpec((bm, D), lambda i: (i, 0))
    vec = pl.BlockSpec((1, D), lambda i: (0, 0))
    dy, loss = pl.pallas_call(
        body, name="loss_fwd_bwd", grid=(nsteps,),
        in_specs=[row, row, vec, row],
        out_specs=[row, pl.BlockSpec((8, 128), lambda i: (0, 0))],
        out_shape=[jax.ShapeDtypeStruct((L, D), f32), jax.ShapeDtypeStruct((8, 128), f32)],
        scratch_shapes=[pltpu.VMEM((1, D), f32)],
        compiler_params=_cp(("arbitrary",)),
    )(x1, down, gate, target)
    return loss[0, 0], dy


def _head_mean_matrix():
    r = lax.broadcasted_iota(jnp.int32, (128, 128), 0) // HD
    c = lax.broadcasted_iota(jnp.int32, (128, 128), 1) // HD
    return jnp.where(r == c, 1.0 / HD, 0.0).astype(bf16)


def _qknorm_fwd(p, gqk):
    L = p.shape[0]
    bm = _row_block(L, 256)

    def body(p_ref, g_ref, o_ref):
        e = _head_mean_matrix()
        for c in range(8):
            sl = slice(128 * c, 128 * (c + 1))
            xv = p_ref[:, sl]
            inv = lax.rsqrt(_dot_exact(xv * xv, e) + EPS)
            yv = xv * inv * g_ref[:, sl]
            if c < 4:
                yv = yv * (1.0 / math.sqrt(HD))
            o_ref[:, sl] = yv.astype(o_ref.dtype)
        o_ref[:, 1024:1536] = p_ref[:, 1024:1536].astype(o_ref.dtype)

    return pl.pallas_call(
        body, name="qknorm_fwd", grid=(L // bm,),
        in_specs=[pl.BlockSpec((bm, 1536), lambda i: (i, 0)), pl.BlockSpec((1, 1024), lambda i: (0, 0))],
        out_specs=pl.BlockSpec((bm, 1536), lambda i: (i, 0)),
        out_shape=jax.ShapeDtypeStruct((L, 1536), bf16),
        compiler_params=_cp(("parallel",)),
    )(p, gqk)


def _qknorm_bwd(dqn, dkn, dval, du, p, gqk):
    L = p.shape[0]
    bm = _row_block(L, 256)
    nsteps = L // bm

    def body(dq_ref, dk_ref, dv_ref, du_ref, p_ref, g_ref, o_ref, dg_ref, acc_ref):
        i = pl.program_id(0)
        e = _head_mean_matrix()
        o_ref[:, 1024:1536] = dv_ref[...].astype(o_ref.dtype)
        o_ref[:, 1536:2048] = du_ref[...].astype(o_ref.dtype)
        for c in range(8):
            sl = slice(128 * c, 128 * (c + 1))
            xv = p_ref[:, sl]
            d_ref = dq_ref if c < 4 else dk_ref
            dv = d_ref[:, 128 * (c % 4):128 * (c % 4 + 1)]
            gv = g_ref[:, sl]
            inv = lax.rsqrt(_dot_exact(xv * xv, e) + EPS)
            xh = xv * inv
            dxh = dv * gv
            o_ref[:, sl] = (inv * (dxh - xh * _dot_exact(dxh * xh, e))).astype(o_ref.dtype)
            part = jnp.sum(dv * xh, axis=0, keepdims=True)

            @pl.when(i == 0)
            def _():
                acc_ref[:, sl] = part

            @pl.when(i > 0)
            def _():
                acc_ref[:, sl] += part

        @pl.when(i == nsteps - 1)
        def _():
            r = lax.broadcasted_iota(jnp.int32, (1024, 128), 0)
            col = lax.broadcasted_iota(jnp.int32, (1024, 128), 1)
            fold = jnp.where(col == (r // 512) * HD + r % HD, 1.0, 0.0).astype(bf16)
            dg_ref[...] = _dot_exact(jnp.broadcast_to(acc_ref[...], (8, 1024)), fold)

    half = pl.BlockSpec((bm, 512), lambda i: (i, 0))
    return pl.pallas_call(
        body, name="qknorm_bwd", grid=(nsteps,),
        in_specs=[half, half, half, half, pl.BlockSpec((bm, 1024), lambda i: (i, 0)),
                  pl.BlockSpec((1, 1024), lambda i: (0, 0))],
        out_specs=[pl.BlockSpec((bm, INW), lambda i: (i, 0)), pl.BlockSpec((8, 128), lambda i: (0, 0))],
        out_shape=[jax.ShapeDtypeStruct((L, INW), bf16), jax.ShapeDtypeStruct((8, 128), f32)],
        scratch_shapes=[pltpu.VMEM((1, 1024), f32)],
        compiler_params=_cp(("arbitrary",)),
    )(dqn, dkn, dval, du, p, gqk)


def _outnorm_fwd(oa, os_, ga, gs):
    L = oa.shape[0]
    bm = _row_block(L, 256)

    def body(a_ref, s_ref, ga_ref, gs_ref, o_ref):
        for x_ref, g_ref, off in ((a_ref, ga_ref, 0), (s_ref, gs_ref, 512)):
            xv = x_ref[...]
            inv = lax.rsqrt(jnp.mean(xv * xv, axis=-1, keepdims=True) + EPS)
            o_ref[:, off:off + 512] = (xv * inv * g_ref[...]).astype(o_ref.dtype)

    half = pl.BlockSpec((bm, 512), lambda i: (i, 0))
    vec = pl.BlockSpec((1, 512), lambda i: (0, 0))
    return pl.pallas_call(
        body, name="outnorm_fwd", grid=(L // bm,),
        in_specs=[half, half, vec, vec], out_specs=pl.BlockSpec((bm, D), lambda i: (i, 0)),
        out_shape=jax.ShapeDtypeStruct((L, D), bf16),
        compiler_params=_cp(("parallel",)),
    )(oa, os_, ga, gs)


def _outnorm_bwd(do, oa, os_, ga, gs):
    L = oa.shape[0]
    bm = _row_block(L, 256)

    def body(do_ref, a_ref, s_ref, ga_ref, gs_ref, da_ref, ds_ref, dga_ref, dgs_ref):
        i = pl.program_id(0)
        for x_ref, g_ref, dx_ref, dg_ref, off in ((a_ref, ga_ref, da_ref, dga_ref, 0),
                                                  (s_ref, gs_ref, ds_ref, dgs_ref, 512)):
            xv = x_ref[...]
            dv = do_ref[:, off:off + 512]
            inv = lax.rsqrt(jnp.mean(xv * xv, axis=-1, keepdims=True) + EPS)
            xh = xv * inv
            dxh = dv * g_ref[...]
            dx_ref[...] = inv * (dxh - xh * jnp.mean(dxh * xh, axis=-1, keepdims=True))
            part = jnp.sum(dv * xh, axis=0, keepdims=True)

            @pl.when(i == 0)
            def _():
                dg_ref[...] = part

            @pl.when(i > 0)
            def _():
                dg_ref[...] += part

    half = pl.BlockSpec((bm, 512), lambda i: (i, 0))
    vec = pl.BlockSpec((1, 512), lambda i: (0, 0))
    return pl.pallas_call(
        body, name="outnorm_bwd", grid=(L // bm,),
        in_specs=[pl.BlockSpec((bm, D), lambda i: (i, 0)), half, half, vec, vec],
        out_specs=[half, half, vec, vec],
        out_shape=[jax.ShapeDtypeStruct((L, 512), f32)] * 2 + [jax.ShapeDtypeStruct((1, 512), f32)] * 2,
        compiler_params=_cp(("arbitrary",)),
    )(do, oa, os_, ga, gs)


NCB = DFF // 128


def _conv_rows(L):
    return 64 if L % 64 == 0 else 32


def _conv_chunk(x_ref, w_ref, b_ref, i, nch, R, L, ahead):
    base = pl.multiple_of(i * R, R)
    prev = x_ref[pl.ds(pl.multiple_of(jnp.maximum(base - 8, 0), 8), 8), :]
    parts = [jnp.where(i > 0, prev, 0.0), x_ref[pl.ds(base, R), :]]
    if ahead:
        nxt = x_ref[pl.ds(pl.multiple_of(jnp.minimum(base + R, L - 8), 8), 8), :]
        parts.append(jnp.where(i < nch - 1, nxt, 0.0))
    xe = jnp.concatenate(parts, axis=0)
    x1 = pltpu.roll(xe, 1, 0)
    x2 = pltpu.roll(xe, 2, 0)
    u = b_ref[...] + x2 * w_ref[0:1, :]
    u = u + x1 * w_ref[1:2, :]
    u = u + xe * w_ref[2:3, :]
    return u[8:], xe[8:], x1[8:], x2[8:]


def _fold8(x):
    out = x[0:8]
    for q in range(1, x.shape[0] // 8):
        out = out + x[8 * q:8 * (q + 1)]
    return out


def _convglu_fwd(up_pre, cw, cb):
    L = up_pre.shape[0]

    def body(v_ref, g_ref, wv_ref, wg_ref, bv_ref, bg_ref, a_ref):
        row = lax.broadcasted_iota(jnp.int32, (L, 128), 0)

        def conv(x_ref, w_ref, b_ref):
            x = x_ref[...]
            out = b_ref[...] + jnp.where(row >= 2, pltpu.roll(x, 2, 0), 0.0) * w_ref[0:1, :]
            out = out + jnp.where(row >= 1, pltpu.roll(x, 1, 0), 0.0) * w_ref[1:2, :]
            return out + x * w_ref[2:3, :]

        val = conv(v_ref, wv_ref, bv_ref)
        a_ref[...] = (_gelu(conv(g_ref, wg_ref, bg_ref)) * val).astype(a_ref.dtype)

    col = lambda off: pl.BlockSpec((L, 128), lambda j: (0, j + off))
    wsp = lambda off: pl.BlockSpec((3, 128), lambda j: (0, j + off))
    bsp = lambda off: pl.BlockSpec((1, 128), lambda j: (0, j + off))
    return pl.pallas_call(
        body, name="convglu_fwd", grid=(NCB,),
        in_specs=[col(0), col(NCB), wsp(0), wsp(NCB), bsp(0), bsp(NCB)],
        out_specs=pl.BlockSpec((L, 128), lambda j: (0, j)),
        out_shape=jax.ShapeDtypeStruct((L, DFF), bf16),
        compiler_params=_cp(("parallel",)),
    )(up_pre, up_pre, cw, cw, cb, cb)


def _convglu_bwd(da, up_pre, cw, cb, comm=None):
    L = up_pre.shape[0]
    R = _conv_rows(L)
    nch = L // R
    n = R + 8

    def body(da_ref, v_ref, g_ref, wv_ref, wg_ref, bv_ref, bg_ref, dx_ref, dw_ref, db_ref):
        def chunk(i, acc):
            base = pl.multiple_of(i * R, R)
            val, xv, xv1, xv2 = _conv_chunk(v_ref, wv_ref, bv_ref, i, nch, R, L, True)
            gate, xg, xg1, xg2 = _conv_chunk(g_ref, wg_ref, bg_ref, i, nch, R, L, True)
            nxt = da_ref[pl.ds(pl.multiple_of(jnp.minimum(base + R, L - 8), 8), 8), :]
            dav = jnp.concatenate([da_ref[pl.ds(base, R), :], jnp.where(i < nch - 1, nxt, 0.0)], axis=0)
            c = math.sqrt(2.0 / math.pi)
            t = jnp.tanh(c * (gate + 0.044715 * (gate * gate * gate)))
            d_val = dav * (0.5 * gate * (1.0 + t))
            d_gate = dav * val * (0.5 * (1.0 + t)
                                  + 0.5 * gate * (1.0 - t * t) * c * (1.0 + 3.0 * 0.044715 * (gate * gate)))
            new = []
            for k, (d, w_ref, xs) in enumerate(((d_val, wv_ref, (xv2, xv1, xv)), (d_gate, wg_ref, (xg2, xg1, xg)))):
                dx = d * w_ref[2:3, :] + pltpu.roll(d, n - 1, 0) * w_ref[1:2, :]
                dx = dx + pltpu.roll(d, n - 2, 0) * w_ref[0:1, :]
                dx_ref[k, pl.ds(base, R), :] = dx[:R].astype(dx_ref.dtype)
                dr = d[:R]
                new += [_fold8(dr * x[:R]) for x in xs] + [_fold8(dr)]
            return tuple(a + b for a, b in zip(acc, new))

        acc = lax.fori_loop(0, nch, chunk, (jnp.zeros((8, 128), f32),) * 8)
        for k in range(2):
            for r in range(3):
                dw_ref[k, r:r + 1, :] = jnp.sum(acc[4 * k + r], axis=0, keepdims=True)
            db_ref[k] = jnp.sum(acc[4 * k + 3], axis=0, keepdims=True)

    col = lambda off: pl.BlockSpec((L, 128), lambda j: (0, j + off))
    wsp = lambda off: pl.BlockSpec((3, 128), lambda j: (0, j + off))
    bsp = lambda off: pl.BlockSpec((1, 128), lambda j: (0, j + off))
    (dx, dw, db), couts = _call(
        body, comm, name="convglu_bwd", grid=(NCB,),
        in_specs=[col(0), col(0), col(NCB), wsp(0), wsp(NCB), bsp(0), bsp(NCB)],
        out_specs=[pl.BlockSpec((2, L, 128), lambda j: (0, 0, j)), pl.BlockSpec((2, 3, 128), lambda j: (0, 0, j)),
                   pl.BlockSpec((2, 1, 128), lambda j: (0, 0, j))],
        out_shape=[jax.ShapeDtypeStruct((2, L, DFF), bf16), jax.ShapeDtypeStruct((2, 3, DFF), f32),
                   jax.ShapeDtypeStruct((2, 1, DFF), f32)], scratch_shapes=[],
        sem=("parallel",), args=(da, up_pre, up_pre, cw, cw, cb, cb))
    return dx, dw.transpose(1, 0, 2).reshape(3, DUP), db.transpose(1, 0, 2).reshape(1, DUP), couts


def _attn_block(L):
    return 256 if L % 256 == 0 else 128


TAIL_DEAD = -110.0


def _dot_exact2(x, m):
    hi = x.astype(bf16)
    lo = (x - hi.astype(f32)).astype(bf16)
    return (lax.dot_general(hi, m, NN, preferred_element_type=f32)
            + lax.dot_general(lo, m, NN, preferred_element_type=f32))


def _sb_weights(z, mask, tri_gt, carry):
    l1p = jnp.log(1.0 + jnp.exp(-jnp.abs(z)))
    ls_pos = jnp.minimum(z, 0.0) - l1p
    lm = ls_pos - z
    if mask is not None:
        lm = jnp.where(mask, lm, 0.0)
    tail = _dot_exact2(lm, tri_gt) + carry
    w = jnp.exp(ls_pos + tail)
    if mask is not None:
        w = jnp.where(mask, w, 0.0)
    return w, ls_pos, lm


def _head_masks(B):
    lane = lax.broadcasted_iota(jnp.int32, (B, 128), 1)
    return lane < HD, lane >= HD


def _attn_fwd(qkv, comm=None):
    L = qkv.shape[0]
    B = _attn_block(L)
    nq = L // B

    def body(q_ref, k_ref, v_ref, o_ref):
        qi = pl.program_id(1)
        heads = _head_masks(B)
        ti = lax.broadcasted_iota(jnp.int32, (B, B), 0)
        si = lax.broadcasted_iota(jnp.int32, (B, B), 1)
        tri_gt = jnp.where(ti > si, 1.0, 0.0).astype(bf16)
        diag = si < ti
        qv = q_ref[...]
        zero16 = jnp.zeros((), bf16)

        def tile(jb, mask, acc, carries):
            ks = pl.multiple_of(jb * B, B)
            kb = k_ref[pl.ds(ks, B), :]
            vb = v_ref[pl.ds(ks, B), :]
            out = []
            for in_head, carry in zip(heads, carries):
                kh = jnp.where(in_head, kb, zero16)
                vh = jnp.where(in_head, vb, zero16)
                z = lax.dot_general(qv, kh, NT, preferred_element_type=f32)
                w, _, lm = _sb_weights(z, mask, tri_gt, carry)
                acc = acc + lax.dot_general(w.astype(bf16), vh, NN, preferred_element_type=f32)
                out.append(carry + jnp.sum(lm, axis=1, keepdims=True))
            return acc, out

        zc = jnp.zeros((B, 1), f32)
        acc, (c0, c1) = tile(qi, diag, jnp.zeros((B, 128), f32), (zc, zc))

        def cond(st):
            return jnp.logical_and(st[0] <= qi, st[4] > TAIL_DEAD)

        def step(st):
            n, a, p0, p1, _ = st
            a, (p0, p1) = tile(qi - n, None, a, (p0, p1))
            return n + 1, a, p0, p1, jnp.maximum(jnp.max(p0), jnp.max(p1))

        st = lax.while_loop(cond, step, (jnp.int32(1), acc, c0, c1, jnp.maximum(jnp.max(c0), jnp.max(c1))))
        o_ref[...] = st[1]

    (o,), couts = _call(
        body, comm, name="attn_fwd", grid=(4, nq),
        in_specs=[pl.BlockSpec((B, 128), lambda hp, i: (i, hp)),
                  pl.BlockSpec((L, 128), lambda hp, i: (0, 4 + hp)),
                  pl.BlockSpec((L, 128), lambda hp, i: (0, 8 + hp))],
        out_specs=[pl.BlockSpec((B, 128), lambda hp, i: (i, hp))],
        out_shape=[jax.ShapeDtypeStruct((L, ATT), f32)], scratch_shapes=[],
        sem=("parallel", "parallel"), args=(qkv, qkv, qkv))
    return o, couts


def _attn_bwd(qkv, o, do, comm=None):
    L = qkv.shape[0]
    B = _attn_block(L)
    nq = L // B

    def body(q_ref, k_ref, v_ref, o_ref, do_ref, dq_ref, dk_ref, dv_ref):
        qi = pl.program_id(1)

        @pl.when(qi == 0)
        def _():
            dk_ref[...] = jnp.zeros_like(dk_ref)
            dv_ref[...] = jnp.zeros_like(dv_ref)

        heads = _head_masks(B)
        ti = lax.broadcasted_iota(jnp.int32, (B, B), 0)
        si = lax.broadcasted_iota(jnp.int32, (B, B), 1)
        tri_gt = jnp.where(ti > si, 1.0, 0.0).astype(bf16)
        tri_ge = jnp.where(ti >= si, 1.0, 0.0).astype(bf16)
        diag = si < ti
        zero16 = jnp.zeros((), bf16)
        qv = q_ref[...]
        do16 = do_ref[...].astype(bf16)
        dsum_lanes = do16.astype(f32) * o_ref[...]
        qhs = [jnp.where(m, qv, zero16) for m in heads]
        dohs = [jnp.where(m, do16, zero16) for m in heads]
        dsums = [jnp.sum(jnp.where(m, dsum_lanes, 0.0), axis=1, keepdims=True) for m in heads]

        def tile(jb, mask, dq_a, state):
            ks = pl.multiple_of(jb * B, B)
            kb = k_ref[pl.ds(ks, B), :]
            vb = v_ref[pl.ds(ks, B), :]
            dk_blk = jnp.zeros((B, 128), f32)
            dv_blk = jnp.zeros((B, 128), f32)
            out = []
            for in_head, qh, doh, dsum, (carry, suffix) in zip(heads, qhs, dohs, dsums, state):
                kh = jnp.where(in_head, kb, zero16)
                vh = jnp.where(in_head, vb, zero16)
                z = lax.dot_general(qv, kh, NT, preferred_element_type=f32)
                w, ls_pos, lm = _sb_weights(z, mask, tri_gt, carry)
                w16 = w.astype(bf16)
                dw = lax.dot_general(doh, vh, NT, preferred_element_type=f32)
                da = w16.astype(f32) * dw
                sig = jnp.exp(ls_pos)
                suf = _dot_exact2(da, tri_ge) + suffix
                dz = da * (1.0 - sig) - sig * (dsum - suf)
                if mask is not None:
                    dz = jnp.where(mask, dz, 0.0)
                dz16 = dz.astype(bf16)
                dq_a = dq_a + lax.dot_general(dz16, kh, NN, preferred_element_type=f32)
                dk_blk = dk_blk + lax.dot_general(dz16, qh, TN, preferred_element_type=f32)
                dv_blk = dv_blk + lax.dot_general(w16, doh, TN, preferred_element_type=f32)
                out.append((carry + jnp.sum(lm, axis=1, keepdims=True),
                            suffix + jnp.sum(da, axis=1, keepdims=True)))
            dk_ref[pl.ds(ks, B), :] += dk_blk
            dv_ref[pl.ds(ks, B), :] += dv_blk
            return dq_a, out

        def alive(state):
            return jnp.maximum(jnp.max(state[0][0]), jnp.max(state[1][0]))

        zc = jnp.zeros((B, 1), f32)
        dq_acc, state = tile(qi, diag, jnp.zeros((B, 128), f32), ((zc, zc), (zc, zc)))

        def cond(st):
            return jnp.logical_and(st[0] <= qi, st[6] > TAIL_DEAD)

        def step(st):
            n, a, c0, s0, c1, s1, _ = st
            a, new = tile(qi - n, None, a, ((c0, s0), (c1, s1)))
            return n + 1, a, new[0][0], new[0][1], new[1][0], new[1][1], alive(new)

        st = lax.while_loop(cond, step, (jnp.int32(1), dq_acc, state[0][0], state[0][1],
                                         state[1][0], state[1][1], alive(state)))
        dq_ref[...] = st[1] * (1.0 / math.sqrt(HD))

    blk = pl.BlockSpec((B, 128), lambda hp, i: (i, hp))
    return _call(
        body, comm, name="attn_bwd", grid=(4, nq),
        in_specs=[blk,
                  pl.BlockSpec((L, 128), lambda hp, i: (0, 4 + hp)),
                  pl.BlockSpec((L, 128), lambda hp, i: (0, 8 + hp)),
                  blk, blk],
        out_specs=[blk, pl.BlockSpec((L, 128), lambda hp, i: (0, hp)),
                   pl.BlockSpec((L, 128), lambda hp, i: (0, hp))],
        out_shape=[jax.ShapeDtypeStruct((L, ATT), f32)] * 3, scratch_shapes=[],
        sem=("parallel", "arbitrary"), args=(qkv, qkv, qkv, o, do))


def _s5_disc(ldt, ar, ai, br, bi):
    dt = jnp.exp(ldt)
    mag = jnp.exp(dt * ar)
    abar_re = mag * jnp.cos(dt * ai)
    abar_im = mag * jnp.sin(dt * ai)
    em_re = abar_re - 1.0
    em_im = abar_im
    den = ar * ar + ai * ai
    f_re = (em_re * ar + em_im * ai) / den
    f_im = (em_im * ar - em_re * ai) / den
    bb_re = f_re * br - f_im * bi
    bb_im = f_re * bi + f_im * br
    return abar_re, abar_im, bb_re, bb_im


def _s5_disc_fwd(ldt, ar, ai, br, bi):
    def body(ldt_ref, ar_ref, ai_ref, br_ref, bi_ref, o1, o2, o3, o4):
        outs = _s5_disc(ldt_ref[...], ar_ref[...], ai_ref[...], br_ref[...], bi_ref[...])
        for o_ref, v in zip((o1, o2, o3, o4), outs):
            o_ref[...] = v

    col = jax.ShapeDtypeStruct((NS, 1), f32)
    mat = jax.ShapeDtypeStruct((NS, NH16), f32)
    return pl.pallas_call(body, name="s5_disc_fwd", out_shape=[col, col, mat, mat],
                          compiler_params=_cp())(ldt, ar, ai, br, bi)


def _s5_disc_bwd(ldt, ar, ai, br, bi, d_are, d_aim, d_bbre, d_bbim):
    def body(ldt_ref, ar_ref, ai_ref, br_ref, bi_ref, c1, c2, c3, c4, o_ldt, o_ar, o_ai, o_br, o_bi):
        prim = (ldt_ref[...], ar_ref[...], ai_ref[...], br_ref[...], bi_ref[...])
        _, vjp = jax.vjp(_s5_disc, *prim)
        g_ldt, g_ar, g_ai, g_br, g_bi = vjp((c1[...], c2[...], c3[...], c4[...]))
        o_ar[...] = g_ar
        o_ai[...] = g_ai
        o_br[...] = g_br
        o_bi[...] = g_bi
        r = lax.broadcasted_iota(jnp.int32, (NG, NS), 0)
        c = lax.broadcasted_iota(jnp.int32, (NG, NS), 1)
        fold = jnp.where(c // NP == r, 1.0, 0.0).astype(bf16)
        hi = jnp.broadcast_to(g_ldt, (NS, 128))
        p1 = hi.astype(bf16)
        r1 = hi - p1.astype(f32)
        p2 = r1.astype(bf16)
        p3 = (r1 - p2.astype(f32)).astype(bf16)
        dd = lambda p: lax.dot_general(fold, p, NN, preferred_element_type=f32)
        o_ldt[...] = dd(p1) + dd(p2) + dd(p3)

    col = jax.ShapeDtypeStruct((NS, 1), f32)
    mat = jax.ShapeDtypeStruct((NS, NH16), f32)
    return pl.pallas_call(
        body, name="s5_disc_bwd",
        out_shape=[jax.ShapeDtypeStruct((NG, 128), f32), col, col, mat, mat],
        compiler_params=_cp())(ldt, ar, ai, br, bi, d_are, d_aim, d_bbre, d_bbim)


SCAN_LANES = 512


def _cmul(ar, ai, br, bi):
    return ar * br - ai * bi, ar * bi + ai * br


def _pow_tables(ar, ai):
    a2 = _cmul(ar, ai, ar, ai)
    a4 = _cmul(*a2, *a2)
    rows = [(ar, ai)]
    for _ in range(7):
        rows.append(_cmul(*rows[-1], ar, ai))
    return (ar, ai), a2, a4, rows


def _scan_rows(xr_ref, xi_ref, abar_re, abar_im, cr_ref, ci_ref, reverse, on_tile=None):
    T = xr_ref.shape[0]
    nt = T // 8
    row = lax.broadcasted_iota(jnp.int32, (8, SCAN_LANES), 0)
    for c in range(NS // SCAN_LANES):
        ls = slice(c * SCAN_LANES, (c + 1) * SCAN_LANES)
        ar = abar_re[:, ls]
        ai = abar_im[:, ls]
        if reverse:
            ai = -ai
        a1, a2, a4, prow = _pow_tables(ar, ai)
        if reverse:
            prow = prow[::-1]
        pr = jnp.concatenate([p[0] for p in prow], axis=0)
        pi = jnp.concatenate([p[1] for p in prow], axis=0)
        a8r, a8i = prow[0] if reverse else prow[7]
        edge = slice(0, 1) if reverse else slice(7, 8)
        steps = []
        for d, (dr, di) in ((1, a1), (2, a2), (4, a4)):
            keep = (row < 8 - d) if reverse else (row >= d)
            steps.append((8 - d if reverse else d, jnp.where(keep, dr, 0.0), jnp.where(keep, di, 0.0)))

        def local(n, _):
            rs = pl.ds(pl.multiple_of(n * 8, 8), 8)
            xr = xr_ref[rs, ls]
            xi = xi_ref[rs, ls]
            for shift, dr, di in steps:
                mr, mi = _cmul(dr, di, pltpu.roll(xr, shift, 0), pltpu.roll(xi, shift, 0))
                xr = xr + mr
                xi = xi + mi
            xr_ref[rs, ls] = xr
            xi_ref[rs, ls] = xi
            return 0

        lax.fori_loop(0, nt, local, 0, unroll=4)

        def chain(n, carry):
            cr, ci = carry
            it = (nt - 1 - n) if reverse else n
            rs = pl.ds(pl.multiple_of(it * 8, 8), 8)
            xr = xr_ref[rs, ls]
            xi = xi_ref[rs, ls]
            nr, ni = _cmul(a8r, a8i, cr, ci)
            nr = nr + xr[edge, :]
            ni = ni + xi[edge, :]
            mr, mi = _cmul(pr, pi, cr, ci)
            xr = xr + mr
            xi = xi + mi
            xr_ref[rs, ls] = xr
            xi_ref[rs, ls] = xi
            if on_tile is not None:
                nxt_r = jnp.where(row < 7, pltpu.roll(xr, 7, 0), cr)
                nxt_i = jnp.where(row < 7, pltpu.roll(xi, 7, 0), ci)
                on_tile(ls, rs, nxt_r, nxt_i)
            return nr, ni

        cr, ci = lax.fori_loop(0, nt, chain, (cr_ref[:, ls], ci_ref[:, ls]), unroll=2)
        cr_ref[:, ls] = cr
        ci_ref[:, ls] = ci


def _s5_chunk(L):
    return 256 if L % 256 == 0 else L


def _s5_gate(s_re, s_im, u, ccr_ref, cci_ref, dsk_ref, gw_ref, gb_ref):
    y = _dot(s_re, ccr_ref[...]) - _dot(s_im, cci_ref[...]) + dsk_ref[...] * u
    yg = _gelu(y)
    gate = _sigmoid(_dot(yg, gw_ref[...]) + gb_ref[...])
    return y, yg, gate


def _s5_fwd(p, abar_re, abar_im, bbr, bbi, ccr, cci, dsk, gw, gb):
    L = p.shape[0]
    T = _s5_chunk(L)

    def body(u_ref, are_ref, aim_ref, bbr_ref, bbi_ref, ccr_ref, cci_ref, dsk_ref, gw_ref, gb_ref,
             o_ref, sr_ref, si_ref, cr_ref, ci_ref):
        @pl.when(pl.program_id(0) == 0)
        def _():
            cr_ref[...] = jnp.zeros_like(cr_ref)
            ci_ref[...] = jnp.zeros_like(ci_ref)

        u = u_ref[...]
        sr_ref[...] = _dot(u, bbr_ref[...])
        si_ref[...] = _dot(u, bbi_ref[...])
        _scan_rows(sr_ref, si_ref, are_ref[...], aim_ref[...], cr_ref, ci_ref, reverse=False)
        _, yg, gate = _s5_gate(sr_ref[...], si_ref[...], u, ccr_ref, cci_ref, dsk_ref, gw_ref, gb_ref)
        o_ref[...] = yg * gate

    full = lambda shape: pl.BlockSpec(shape, lambda i: (0, 0))
    return pl.pallas_call(
        body, name="s5_fwd", grid=(L // T,),
        in_specs=[pl.BlockSpec((T, SSMW), lambda i: (i, 3)), full((1, NS)), full((1, NS)),
                  full((SSMW, NS)), full((SSMW, NS)), full((NS, SSMW)), full((NS, SSMW)),
                  full((1, SSMW)), full((SSMW, SSMW)), full((1, SSMW))],
        out_specs=[pl.BlockSpec((T, SSMW), lambda i: (i, 0)), pl.BlockSpec((T, NS), lambda i: (i, 0)),
                   pl.BlockSpec((T, NS), lambda i: (i, 0))],
        out_shape=[jax.ShapeDtypeStruct((L, SSMW), f32), jax.ShapeDtypeStruct((L, NS), f32),
                   jax.ShapeDtypeStruct((L, NS), f32)],
        scratch_shapes=[pltpu.VMEM((1, NS), f32), pltpu.VMEM((1, NS), f32)],
        compiler_params=_cp(("arbitrary",)),
    )(p, abar_re, abar_im, bbr, bbi, ccr, cci, dsk, gw, gb)


def _diag_fold(acc, rpg):
    aw = acc.shape[0]
    r = lax.broadcasted_iota(jnp.int32, (aw, 128), 0) // rpg
    c = lax.broadcasted_iota(jnp.int32, (aw, 128), 1) // NH16
    fr = lax.broadcasted_iota(jnp.int32, (128, NH16), 0) % NH16
    fc = lax.broadcasted_iota(jnp.int32, (128, NH16), 1)
    fold = jnp.where(fr == fc, 1.0, 0.0).astype(bf16)
    return _dot_exact(jnp.where(r == c, acc, 0.0), fold)


def _s5_bwd(dout, p, s_re, s_im, abar_re, abar_im, bbr, bbi, ccr, cci, dsk, gw, gb):
    L = p.shape[0]
    T = _s5_chunk(L)
    nchunk = L // T

    def body(do_ref, u_ref, sr_ref, si_ref, are_ref, aim_ref, bbr_ref, bbi_ref, ccr_ref, cci_ref,
             dsk_ref, gw_ref, gb_ref,
             du_ref, dare_ref, daim_ref, dd_ref, dgb_ref, dccr_ref, dcci_ref, dbbr_ref, dbbi_ref, dgw_ref,
             lr_ref, li_ref, cr_ref, ci_ref, accr_ref, acci_ref, wacc_ref, gacc_ref):
        i = pl.program_id(0)

        @pl.when(i == 0)
        def _():
            cr_ref[...] = jnp.zeros_like(cr_ref)
            ci_ref[...] = jnp.zeros_like(ci_ref)
            accr_ref[...] = jnp.zeros_like(accr_ref)
            acci_ref[...] = jnp.zeros_like(acci_ref)
            dd_ref[...] = jnp.zeros_like(dd_ref)
            dgb_ref[...] = jnp.zeros_like(dgb_ref)
            wacc_ref[...] = jnp.zeros_like(wacc_ref)
            gacc_ref[...] = jnp.zeros_like(gacc_ref)

        u = u_ref[...]
        dov = do_ref[...]
        y, yg, gate = _s5_gate(sr_ref[...], si_ref[...], u, ccr_ref, cci_ref, dsk_ref, gw_ref, gb_ref)
        dgl = dov * yg * gate * (1.0 - gate)
        dyg = dov * gate + _dot(dgl, gw_ref[...], NT)
        dy = dyg * _gelu_grad(y)
        u16, dy16, yg16, dgl16 = (t.astype(_MXU) for t in (u, dy, yg, dgl))
        for q in range(4):
            lanes = slice(128 * q, 128 * (q + 1))
            wide_l = slice(512 * q, 512 * (q + 1))
            wacc_ref[0, q] += _dot(sr_ref[:, wide_l], dy16[:, lanes], TN)
            wacc_ref[1, q] += _dot(si_ref[:, wide_l], dy16[:, lanes], TN)
            gacc_ref[q] += _dot(yg16[:, lanes], dgl16[:, lanes], TN)
        dd_ref[...] += jnp.sum(dy * u, axis=0, keepdims=True)
        dgb_ref[...] += jnp.sum(dgl, axis=0, keepdims=True)
        lr_ref[...] = _dot(dy, ccr_ref[...], NT)
        li_ref[...] = -_dot(dy, cci_ref[...], NT)

        def on_tile(ls, rs, nxt_r, nxt_i):
            s_r = sr_ref[rs, ls]
            s_i = si_ref[rs, ls]
            accr_ref[:, ls] += nxt_r * s_r + nxt_i * s_i
            acci_ref[:, ls] += nxt_i * s_r - nxt_r * s_i

        _scan_rows(lr_ref, li_ref, are_ref[...], aim_ref[...], cr_ref, ci_ref, reverse=True,
                   on_tile=on_tile)
        for q in range(4):
            lanes = slice(128 * q, 128 * (q + 1))
            wide_l = slice(512 * q, 512 * (q + 1))
            wacc_ref[2, q] += _dot(lr_ref[:, wide_l], u16[:, lanes], TN)
            wacc_ref[3, q] += _dot(li_ref[:, wide_l], u16[:, lanes], TN)
        du_ref[...] = (dy * dsk_ref[...] + _dot(lr_ref[...], bbr_ref[...], NT)
                       + _dot(li_ref[...], bbi_ref[...], NT))

        @pl.when(i == nchunk - 1)
        def _():
            dare_ref[...] = jnp.sum(accr_ref[...], axis=0, keepdims=True)
            daim_ref[...] = jnp.sum(acci_ref[...], axis=0, keepdims=True)
            for m, o_ref in enumerate((dccr_ref, dcci_ref, dbbr_ref, dbbi_ref)):
                for q in range(4):
                    o_ref[512 * q:512 * (q + 1), :] = _diag_fold(wacc_ref[m, q], NP)
            for q in range(4):
                dgw_ref[128 * q:128 * (q + 1), :] = _diag_fold(gacc_ref[q], NH16)

    rev = lambda i: (nchunk - 1 - i, 0)
    full = lambda shape: pl.BlockSpec(shape, lambda i: (0, 0))
    half = pl.BlockSpec((T, SSMW), rev)
    wide = pl.BlockSpec((T, NS), rev)
    return pl.pallas_call(
        body, name="s5_bwd", grid=(nchunk,),
        in_specs=[half, pl.BlockSpec((T, SSMW), lambda i: (nchunk - 1 - i, 3)), wide, wide,
                  full((1, NS)), full((1, NS)), full((SSMW, NS)), full((SSMW, NS)),
                  full((NS, SSMW)), full((NS, SSMW)), full((1, SSMW)), full((SSMW, SSMW)), full((1, SSMW))],
        out_specs=[half, full((1, NS)), full((1, NS)), full((1, SSMW)), full((1, SSMW)),
                   full((NS, NH16)), full((NS, NH16)), full((NS, NH16)), full((NS, NH16)),
                   full((SSMW, NH16))],
        out_shape=[jax.ShapeDtypeStruct((L, SSMW), f32)] + [jax.ShapeDtypeStruct((1, NS), f32)] * 2
                  + [jax.ShapeDtypeStruct((1, SSMW), f32)] * 2 + [jax.ShapeDtypeStruct((NS, NH16), f32)] * 4
                  + [jax.ShapeDtypeStruct((SSMW, NH16), f32)],
        scratch_shapes=[pltpu.VMEM((T, NS), f32), pltpu.VMEM((T, NS), f32),
                        pltpu.VMEM((1, NS), f32), pltpu.VMEM((1, NS), f32),
                        pltpu.VMEM((8, NS), f32), pltpu.VMEM((8, NS), f32),
                        pltpu.VMEM((4, 4, 512, 128), f32), pltpu.VMEM((4, 128, 128), f32)],
        compiler_params=_cp(("arbitrary",)),
    )(dout, p, s_re, s_im, abar_re, abar_im, bbr, bbi, ccr, cci, dsk, gw, gb)


def _block_diag(x):
    g, r, c = x.shape
    eye = jnp.eye(g, dtype=x.dtype)
    return (x[:, :, None, :] * eye[:, None, :, None]).reshape(g * r, g * c)


def _mod_fwd(c16, ada_w, ada_b_loc):
    nloc = ada_w.shape[2]

    def body(c_ref, w_ref, b_ref, o_ref):
        cv = c_ref[...]
        act = cv * _sigmoid(cv)
        o_ref[...] = _dot(act, w_ref[...]) + b_ref[...]

    return pl.pallas_call(
        body, name="mod_fwd", grid=(DEPTH,),
        in_specs=[pl.BlockSpec((16, D), lambda l: (0, 0)), pl.BlockSpec((None, D, nloc), lambda l: (l, 0, 0)),
                  pl.BlockSpec((None, 1, nloc), lambda l: (l, 0, 0))],
        out_specs=pl.BlockSpec((None, 16, nloc), lambda l: (l, 0, 0)),
        out_shape=jax.ShapeDtypeStruct((DEPTH, 16, nloc), f32),
        compiler_params=_cp(("parallel",)),
    )(c16, ada_w, ada_b_loc)


def _ada_grad(c16, dmod16):
    nloc = dmod16.shape[2]

    def body(c_ref, d_ref, o_ref):
        cv = c_ref[...]
        act = cv * _sigmoid(cv)
        o_ref[...] = _dot(act, d_ref[...], TN)

    return pl.pallas_call(
        body, name="ada_grad", grid=(DEPTH,),
        in_specs=[pl.BlockSpec((16, D), lambda l: (0, 0)), pl.BlockSpec((None, 16, nloc), lambda l: (l, 0, 0))],
        out_specs=pl.BlockSpec((None, D, nloc), lambda l: (l, 0, 0)),
        out_shape=jax.ShapeDtypeStruct((DEPTH, D, nloc), f32),
        compiler_params=_cp(("parallel",)),
    )(c16, dmod16)


def _as2d(a):
    return a.reshape(-1, a.shape[-1])


def _ew_rows(rows):
    for cand in (512, 256, 128, 64, 32, 16, 8):
        if rows % cand == 0:
            return cand
    return rows


def _cast_bf16(w):
    w2 = _as2d(w)
    rows, cols = w2.shape
    bm = _ew_rows(rows)

    def body(x_ref, o_ref):
        o_ref[...] = x_ref[...].astype(bf16)

    spec = pl.BlockSpec((bm, cols), lambda i: (i, 0))
    out = pl.pallas_call(body, name="cast_bf16", grid=(rows // bm,), in_specs=[spec], out_specs=spec,
                         out_shape=jax.ShapeDtypeStruct((rows, cols), bf16),
                         compiler_params=_cp(("parallel",)))(w2)
    return out.reshape(w.shape)


def _add2(a, b, out_dtype=f32):
    a2, b2 = _as2d(a), _as2d(b)
    rows, cols = a2.shape
    bm = _ew_rows(rows)

    def body(a_ref, b_ref, o_ref):
        o_ref[...] = (a_ref[...] + b_ref[...]).astype(o_ref.dtype)

    spec = pl.BlockSpec((bm, cols), lambda i: (i, 0))
    out = pl.pallas_call(body, name="add2", grid=(rows // bm,), in_specs=[spec, spec], out_specs=spec,
                         out_shape=jax.ShapeDtypeStruct((rows, cols), out_dtype),
                         compiler_params=_cp(("parallel",)))(a2, b2)
    return out.reshape(a.shape)


def _adamw(parts, w, m, v):
    n = parts.shape[0]
    w2, m2, v2 = _as2d(w), _as2d(m), _as2d(v)
    rows, cols = w2.shape
    p3 = parts.reshape(n, rows, cols)
    bm = _ew_rows(rows)
    if bm * cols * 4 > (1 << 20) and bm % 16 == 0:
        bm //= 2

    def body(p_ref, w_ref, m_ref, v_ref, g_ref, d_ref, nm_ref, nv_ref):
        g = p_ref[0]
        for k in range(1, n):
            g = g + p_ref[k]
        mn = ADAM_B1 * m_ref[...] + (1.0 - ADAM_B1) * g
        vn = ADAM_B2 * v_ref[...] + (1.0 - ADAM_B2) * (g * g)
        m_hat = mn / (1.0 - ADAM_B1 ** ADAM_STEP)
        v_hat = vn / (1.0 - ADAM_B2 ** ADAM_STEP)
        g_ref[...] = g
        d_ref[...] = -ADAM_LR * (m_hat / (jnp.sqrt(v_hat) + ADAM_EPS) + ADAM_WD * w_ref[...])
        nm_ref[...] = mn
        nv_ref[...] = vn

    spec = pl.BlockSpec((bm, cols), lambda i: (i, 0))
    outs = pl.pallas_call(
        body, name="adamw", grid=(rows // bm,),
        in_specs=[pl.BlockSpec((n, bm, cols), lambda i: (0, i, 0)), spec, spec, spec],
        out_specs=[spec] * 4, out_shape=[jax.ShapeDtypeStruct((rows, cols), f32)] * 4,
        compiler_params=_cp(("parallel",)),
    )(p3, w2, m2, v2)
    return tuple(o.reshape(w.shape) for o in outs)


ANY = pl.BlockSpec(memory_space=pl.ANY)


def _my_pos():
    return lax.axis_index("x"), lax.axis_index("y"), lax.axis_index("c")


def _allgather8(name, x):
    def body(x_ref, out_ref, send_sems, recv_sems):
        mx, my, mc = _my_pos()
        me, sibling = (mx, my, mc), (mx, my, 1 - mc)
        chips = [(1 - mx, my), (mx, 1 - my), (1 - mx, 1 - my)]

        def slot(px, py, pc):
            return out_ref.at[4 * px + 2 * py + pc]

        def copy(k, block, to, src=None):
            return pltpu.make_async_remote_copy(
                src_ref=slot(*block) if src is None else src, dst_ref=slot(*block),
                send_sem=send_sems.at[k], recv_sem=recv_sems.at[k], device_id=to, device_id_type=MESH)

        first = [copy(0, me, sibling, src=x_ref)]
        first += [copy(1 + j, me, (*chip, mc), src=x_ref) for j, chip in enumerate(chips)]
        for cp in first:
            cp.start()
        passed = [copy(4 + j, (*chip, mc), sibling) for j, chip in enumerate(chips)]
        for j, chip in enumerate(chips):
            copy(1 + j, (*chip, mc), me).wait_recv()
            passed[j].start()
        copy(0, sibling, me).wait_recv()
        for j, chip in enumerate(chips):
            copy(4 + j, (*chip, 1 - mc), me).wait_recv()
        for cp in first + passed:
            cp.wait_send()

    out = pl.pallas_call(
        body, name=name, in_specs=[ANY], out_specs=ANY,
        out_shape=jax.ShapeDtypeStruct((NDEV,) + x.shape, x.dtype),
        scratch_shapes=[pltpu.SemaphoreType.DMA((7,)), pltpu.SemaphoreType.DMA((7,))],
    )(x)
    dev = 4 * lax.axis_index("x") + 2 * lax.axis_index("y") + lax.axis_index("c")
    return lax.dynamic_update_index_in_dim(out, x, dev, 0)


HALF = DEPTH // 2


def _chip_gather(arrs):
    n = len(arrs)

    def body(*refs):
        ins, outs = refs[:n], refs[n:2 * n]
        send_sems, recv_sems = refs[2 * n:]
        mx, my, mc = _my_pos()
        chips = [(1 - mx, my), (mx, 1 - my), (1 - mx, 1 - my)]
        mine = 2 * mx + my
        my_half = pl.ds(HALF * mc, HALF)
        sib_half = pl.ds(HALF * (1 - mc), HALF)

        def ici(a, j, src_chip, half):
            px, py = chips[j]
            return pltpu.make_async_remote_copy(
                src_ref=ins[a].at[half], dst_ref=outs[a].at[src_chip, half],
                send_sem=send_sems.at[3 * a + j], recv_sem=recv_sems.at[3 * a + j],
                device_id=(px, py, mc), device_id_type=MESH)

        def d2d(a, j, half):
            px, py = chips[j]
            blk = outs[a].at[2 * px + py, half]
            return pltpu.make_async_remote_copy(
                src_ref=blk, dst_ref=blk,
                send_sem=send_sems.at[3 * n + 3 * a + j], recv_sem=recv_sems.at[3 * n + 3 * a + j],
                device_id=(mx, my, 1 - mc), device_id_type=MESH)

        sends = [ici(a, j, mine, my_half) for a in range(n) for j in range(3)]
        for cp in sends:
            cp.start()
        passed = []
        for a in range(n):
            for j, (px, py) in enumerate(chips):
                ici(a, j, 2 * px + py, my_half).wait_recv()
                fwd = d2d(a, j, my_half)
                fwd.start()
                passed.append(fwd)
        for a in range(n):
            for j in range(3):
                d2d(a, j, sib_half).wait_recv()
        for cp in sends + passed:
            cp.wait_send()

    outs = pl.pallas_call(
        body, name="chip_gather", in_specs=[ANY] * n, out_specs=[ANY] * n,
        out_shape=[jax.ShapeDtypeStruct((NCHIP,) + a.shape, a.dtype) for a in arrs],
        scratch_shapes=[pltpu.SemaphoreType.DMA((6 * n,)), pltpu.SemaphoreType.DMA((6 * n,))],
    )(*arrs)
    mine = 2 * lax.axis_index("x") + lax.axis_index("y")
    return [lax.dynamic_update_index_in_dim(o, a, mine, 0) for o, a in zip(outs, arrs)]


def _sibling_half_swap(arrs):
    n = len(arrs)

    def body(*refs):
        ins, outs = refs[:n], refs[n:2 * n]
        send_sems, recv_sems = refs[2 * n:]
        mx, my, mc = _my_pos()
        sib_half = pl.ds(HALF * (1 - mc), HALF)
        cps = [pltpu.make_async_remote_copy(
            src_ref=ins[a].at[j, sib_half], dst_ref=outs[a].at[j],
            send_sem=send_sems.at[NCHIP * a + j], recv_sem=recv_sems.at[NCHIP * a + j],
            device_id=(mx, my, 1 - mc), device_id_type=MESH) for a in range(n) for j in range(NCHIP)]
        for cp in cps:
            cp.start()
        for cp in cps:
            cp.wait()

    return pl.pallas_call(
        body, name="sibling_half_swap", in_specs=[ANY] * n, out_specs=[ANY] * n,
        out_shape=[jax.ShapeDtypeStruct((NCHIP, HALF) + a.shape[2:], a.dtype) for a in arrs],
        scratch_shapes=[pltpu.SemaphoreType.DMA((NCHIP * n,)), pltpu.SemaphoreType.DMA((NCHIP * n,))],
    )(*arrs)


def _sibling_merge_halves(arrs):
    n = len(arrs)

    def body(*refs):
        ins, outs = refs[:n], refs[n:2 * n]
        send_sems, recv_sems = refs[2 * n:]
        mx, my, mc = _my_pos()
        my_half = pl.ds(HALF * mc, HALF)
        cps = [pltpu.make_async_remote_copy(
            src_ref=ins[a], dst_ref=outs[a].at[my_half],
            send_sem=send_sems.at[a], recv_sem=recv_sems.at[a],
            device_id=(mx, my, 1 - mc), device_id_type=MESH) for a in range(n)]
        for cp in cps:
            cp.start()
        for cp in cps:
            cp.wait()

    outs = pl.pallas_call(
        body, name="sibling_merge_halves", in_specs=[ANY] * n, out_specs=[ANY] * n,
        out_shape=[jax.ShapeDtypeStruct((DEPTH,) + a.shape[1:], a.dtype) for a in arrs],
        scratch_shapes=[pltpu.SemaphoreType.DMA((n,)), pltpu.SemaphoreType.DMA((n,))],
    )(*arrs)
    start = HALF * lax.axis_index("c")
    return [lax.dynamic_update_slice_in_dim(o, a, start, 0) for o, a in zip(outs, arrs)]


def _sum_parts(parts):
    n = parts.shape[0]
    p3 = parts.reshape(n, -1, parts.shape[-1])
    rows, cols = p3.shape[1:]
    bm = _ew_rows(rows)
    if bm * cols * 4 > (1 << 20) and bm % 16 == 0:
        bm //= 2

    def body(p_ref, o_ref):
        g = p_ref[0].astype(f32)
        for k in range(1, n):
            g = g + p_ref[k].astype(f32)
        o_ref[...] = g

    out = pl.pallas_call(
        body, name="sum_parts", grid=(rows // bm,),
        in_specs=[pl.BlockSpec((n, bm, cols), lambda i: (0, i, 0))],
        out_specs=pl.BlockSpec((bm, cols), lambda i: (i, 0)),
        out_shape=jax.ShapeDtypeStruct((rows, cols), f32),
        compiler_params=_cp(("parallel",)),
    )(p3)
    return out.reshape(parts.shape[1:])


def _chip_scatter(arrs):
    n = len(arrs)

    def body(*refs):
        ins, outs = refs[:n], refs[n:2 * n]
        send_sems, recv_sems = refs[2 * n:]
        mx, my, mc = _my_pos()
        chips = [(1 - mx, my), (mx, 1 - my), (1 - mx, 1 - my)]
        mine = 2 * mx + my
        sends = []
        for a in range(n):
            for j, (px, py) in enumerate(chips):
                sends.append(pltpu.make_async_remote_copy(
                    src_ref=ins[a].at[2 * px + py], dst_ref=outs[a].at[mine],
                    send_sem=send_sems.at[3 * a + j], recv_sem=recv_sems.at[3 * a + j],
                    device_id=(px, py, mc), device_id_type=MESH))
        for cp in sends:
            cp.start()
        for a in range(n):
            for j, (px, py) in enumerate(chips):
                pltpu.make_async_remote_copy(
                    src_ref=ins[a].at[mine], dst_ref=outs[a].at[2 * px + py],
                    send_sem=send_sems.at[3 * a + j], recv_sem=recv_sems.at[3 * a + j],
                    device_id=(px, py, mc), device_id_type=MESH).wait_recv()
        for cp in sends:
            cp.wait_send()

    outs = pl.pallas_call(
        body, name="chip_scatter", in_specs=[ANY] * n, out_specs=[ANY] * n,
        out_shape=[jax.ShapeDtypeStruct(a.shape, a.dtype) for a in arrs],
        scratch_shapes=[pltpu.SemaphoreType.DMA((3 * n,)), pltpu.SemaphoreType.DMA((3 * n,))],
    )(*arrs)
    mine = 2 * lax.axis_index("x") + lax.axis_index("y")
    return [lax.dynamic_update_index_in_dim(o, lax.dynamic_index_in_dim(a, mine, 0, keepdims=False), mine, 0)
            for o, a in zip(outs, arrs)]


def _rc(src, dst, ss, rs, k, to):
    return pltpu.make_async_remote_copy(src_ref=src, dst_ref=dst, send_sem=ss.at[k], recv_sem=rs.at[k],
                                        device_id=to, device_id_type=MESH)


def _half_rows(rows, c):
    return pl.ds((rows // 2) * c, rows // 2)


def _gather_comm(shards):
    n = len(shards)

    def plan(ins, outs, ss, rs):
        mx, my, mc = _my_pos()
        chips = [(1 - mx, my), (mx, 1 - my), (1 - mx, 1 - my)]
        mine = 2 * mx + my
        sends, lands, fwds, fwd_lands = [], [], [], []
        for a in range(n):
            rows = ins[a].shape[0]
            mr, sr = _half_rows(rows, mc), _half_rows(rows, 1 - mc)
            for j, (px, py) in enumerate(chips):
                k, src_chip = 3 * a + j, 2 * px + py
                sends.append(_rc(ins[a].at[mr], outs[a].at[mine, mr], ss, rs, k, (px, py, mc)))
                lands.append(_rc(ins[a].at[mr], outs[a].at[src_chip, mr], ss, rs, k, (px, py, mc)))
                blk, sblk = outs[a].at[src_chip, mr], outs[a].at[src_chip, sr]
                fwds.append(_rc(blk, blk, ss, rs, 3 * n + k, (mx, my, 1 - mc)))
                fwd_lands.append(_rc(sblk, sblk, ss, rs, 3 * n + k, (mx, my, 1 - mc)))
        return sends, lands, fwds, fwd_lands

    def start(ins, outs, ss, rs):
        for cp in plan(ins, outs, ss, rs)[0]:
            cp.start()

    def finish(ins, outs, ss, rs):
        sends, lands, fwds, fwd_lands = plan(ins, outs, ss, rs)
        for land, fwd in zip(lands, fwds):
            land.wait_recv()
            fwd.start()
        for cp in fwd_lands:
            cp.wait_recv()
        for cp in sends + fwds:
            cp.wait_send()

    return dict(ins=list(shards), nsem=6 * n, start=start, finish=finish,
                out_shapes=[jax.ShapeDtypeStruct((NCHIP,) + a.shape, a.dtype) for a in shards])


def _gather_done(outs, shards):
    mine = 2 * lax.axis_index("x") + lax.axis_index("y")
    return [lax.dynamic_update_index_in_dim(o, a, mine, 0) for o, a in zip(outs, shards)]


def _swap_comm(grads):
    n = len(grads)

    def plan(ins, outs, ss, rs):
        mx, my, mc = _my_pos()
        return [_rc(ins[a].at[j, _half_rows(ins[a].shape[1], 1 - mc)], outs[a].at[j], ss, rs,
                    NCHIP * a + j, (mx, my, 1 - mc)) for a in range(n) for j in range(NCHIP)]

    def start(ins, outs, ss, rs):
        for cp in plan(ins, outs, ss, rs):
            cp.start()

    def finish(ins, outs, ss, rs):
        for cp in plan(ins, outs, ss, rs):
            cp.wait()

    return dict(ins=list(grads), nsem=NCHIP * n, start=start, finish=finish,
                out_shapes=[jax.ShapeDtypeStruct((NCHIP, a.shape[1] // 2) + a.shape[2:], a.dtype)
                            for a in grads])


def _scatter_comm(sums):
    n = len(sums)

    def plan(ins, outs, ss, rs):
        mx, my, mc = _my_pos()
        chips = [(1 - mx, my), (mx, 1 - my), (1 - mx, 1 - my)]
        mine = 2 * mx + my
        sends, lands = [], []
        for a in range(n):
            for j, (px, py) in enumerate(chips):
                sends.append(_rc(ins[a].at[2 * px + py], outs[a].at[mine], ss, rs, 3 * a + j, (px, py, mc)))
                lands.append(_rc(ins[a].at[mine], outs[a].at[2 * px + py], ss, rs, 3 * a + j, (px, py, mc)))
        return sends, lands

    def start(ins, outs, ss, rs):
        for cp in plan(ins, outs, ss, rs)[0]:
            cp.start()

    def finish(ins, outs, ss, rs):
        sends, lands = plan(ins, outs, ss, rs)
        for cp in lands:
            cp.wait_recv()
        for cp in sends:
            cp.wait_send()

    return dict(ins=list(sums), nsem=3 * n, start=start, finish=finish,
                out_shapes=[jax.ShapeDtypeStruct(a.shape, a.dtype) for a in sums])


def _scatter_done(outs, sums):
    mine = 2 * lax.axis_index("x") + lax.axis_index("y")
    return [lax.dynamic_update_index_in_dim(o, lax.dynamic_index_in_dim(a, mine, 0, keepdims=False), mine, 0)
            for o, a in zip(outs, sums)]


def _merge_comm(halves):
    n = len(halves)

    def plan(ins, outs, ss, rs):
        mx, my, mc = _my_pos()
        return [_rc(ins[a], outs[a].at[_half_rows(outs[a].shape[0], mc)], ss, rs, a, (mx, my, 1 - mc))
                for a in range(n)]

    def start(ins, outs, ss, rs):
        for cp in plan(ins, outs, ss, rs):
            cp.start()

    def finish(ins, outs, ss, rs):
        for cp in plan(ins, outs, ss, rs):
            cp.wait()

    return dict(ins=list(halves), nsem=n, start=start, finish=finish,
                out_shapes=[jax.ShapeDtypeStruct((2 * a.shape[0],) + a.shape[1:], a.dtype) for a in halves])


def _merge_done(outs, halves):
    mc = lax.axis_index("c")
    return [lax.dynamic_update_slice_in_dim(o, a, a.shape[0] * mc, 0) for o, a in zip(outs, halves)]


def _comm_call(name, comm):
    ni, no = len(comm["ins"]), len(comm["out_shapes"])

    def body(*refs):
        ins, outs, (ss, rs) = refs[:ni], refs[ni:ni + no], refs[ni + no:]
        comm["start"](ins, outs, ss, rs)
        comm["finish"](ins, outs, ss, rs)

    return pl.pallas_call(
        body, name=name, in_specs=[ANY] * ni, out_specs=[ANY] * no, out_shape=comm["out_shapes"],
        scratch_shapes=[pltpu.SemaphoreType.DMA((comm["nsem"],)), pltpu.SemaphoreType.DMA((comm["nsem"],))],
    )(*comm["ins"])


def _call(body, comm, *, name, grid, in_specs, out_specs, out_shape, scratch_shapes, sem, args):
    if comm is None:
        outs = pl.pallas_call(body, name=name, grid=grid, in_specs=in_specs, out_specs=out_specs,
                              out_shape=out_shape, scratch_shapes=scratch_shapes,
                              compiler_params=_cp(sem))(*args)
        return outs, None
    ni, no = len(in_specs), len(out_specs)
    ci, co = len(comm["ins"]), len(comm["out_shapes"])
    nscr = len(scratch_shapes)

    def carried(*refs):
        ins, cins = refs[:ni], refs[ni:ni + ci]
        outs, couts = refs[ni + ci:ni + ci + no], refs[ni + ci + no:ni + ci + no + co]
        scr = refs[ni + ci + no + co:]
        ss, rs = scr[nscr], scr[nscr + 1]
        pids = [pl.program_id(ax) for ax in range(len(grid))]
        first = functools.reduce(jnp.logical_and, [p == 0 for p in pids])
        last = functools.reduce(jnp.logical_and, [p == g - 1 for p, g in zip(pids, grid)])

        @pl.when(first)
        def _():
            comm["start"](cins, couts, ss, rs)

        body(*ins, *outs, *scr[:nscr])

        @pl.when(last)
        def _():
            comm["finish"](cins, couts, ss, rs)

    outs = pl.pallas_call(
        carried, name=name + "_x", grid=grid, in_specs=list(in_specs) + [ANY] * ci,
        out_specs=list(out_specs) + [ANY] * co, out_shape=list(out_shape) + comm["out_shapes"],
        scratch_shapes=list(scratch_shapes) + [pltpu.SemaphoreType.DMA((comm["nsem"],)),
                                               pltpu.SemaphoreType.DMA((comm["nsem"],))],
        compiler_params=_cp(("arbitrary",) * len(grid)),
    )(*args, *comm["ins"])
    return outs[:no], outs[no:]


def _layer_fwd(x, res, res_gate, wl, mod, comm=None):
    L = x.shape[0]
    bm = _row_block(L, 1024)
    nb = L // bm
    sh1, sc1, g1, sh2, sc2, g2 = [mod[k * D:(k + 1) * D][None, :] for k in range(NMOD)]
    sv = {}
    xin, h = _rmsmod_fwd(x, res, res_gate, wl["norm1_g"], sh1, sc1)
    if xin is None:
        xin = x
    p = _mm("mm_in", h, wl["w_in"], dims=NN, grid=(nb, 4, 1),
            a_block=(bm, D), a_map=lambda i, j, k: (i, 0),
            b_block=(None, D, 512), b_map=lambda i, j, k: (j, 0, 0),
            out_shape=(L, INW), o_block=(bm, 512), o_map=lambda i, j, k: (i, j))
    qkv = _qknorm_fwd(p, wl["gqk"])
    oa, couts = _attn_fwd(qkv, comm)
    os_, s_re, s_im = _s5_fwd(p, wl["abar_re"], wl["abar_im"], wl["bbr"], wl["bbi"], wl["ccr"], wl["cci"],
                              wl["dsk"], wl["gw"], wl["gb"])
    o = _outnorm_fwd(oa, os_, wl["attn_out_g"], wl["ssm_out_g"])
    proj = _mm("mm_out", o, wl["w_out"], dims=NN, grid=(nb, 2, 1),
               a_block=(bm, D), a_map=lambda i, j, k: (i, 0),
               b_block=(D, 512), b_map=lambda i, j, k: (0, j),
               out_shape=(L, D), o_block=(bm, 512), o_map=lambda i, j, k: (i, j))
    x1, h2 = _rmsmod_fwd(xin, proj, g1, wl["norm2_g"], sh2, sc2)
    up_pre = _mm("mm_up", h2, wl["w_up"], dims=NN, grid=(nb, 4, 1),
                 a_block=(bm, D), a_map=lambda i, j, k: (i, 0),
                 b_block=(None, D, 1408), b_map=lambda i, j, k: (j, 0, 0),
                 out_shape=(L, DUP), o_block=(bm, 1408), o_map=lambda i, j, k: (i, j))
    a = _convglu_fwd(up_pre, wl["conv_w"], wl["conv_b"])
    down = _mm("mm_down", a, wl["w_down"], dims=NN, grid=(nb, 2, 1),
               a_block=(bm, DFF), a_map=lambda i, j, k: (i, 0),
               b_block=(DFF, 512), b_map=lambda i, j, k: (0, j),
               out_shape=(L, D), o_block=(bm, 512), o_map=lambda i, j, k: (i, j))
    sv.update(xin=xin, h=h, p=p, qkv=qkv, oa=oa, os=os_, s_re=s_re, s_im=s_im, o=o, proj=proj,
              x1=x1, h2=h2, up_pre=up_pre, a=a, down=down,
              sc1=sc1, g1=g1, sc2=sc2, g2=g2)
    return x1, down, g2, sv, couts


def _big_grads(g):
    return [g["w_in"], g["w_out"].reshape(NCHIP, D // NCHIP, D), g["w_up"],
            g["w_down"].reshape(NCHIP, DFF // NCHIP, D)]


def _chip_sums(grads, recv):
    mc = lax.axis_index("c")
    mine = [lax.dynamic_slice_in_dim(a, (a.shape[1] // 2) * mc, a.shape[1] // 2, axis=1) for a in grads]
    return [_add2(a, b, bf16) for a, b in zip(mine, recv)]


def _reduced(parts, sums):
    halves = [_sum_parts(p) for p in _scatter_done(parts, sums)]
    return _merge_done(_comm_call("grad_merge", _merge_comm(halves)), halves)


def _layer_bwd(dx_out, wl, sv, prev=None):
    L = dx_out.shape[0]
    bm = _row_block(L, 1024)
    nb = L // bm
    bk = _row_block(L, 512)
    nk = L // bk
    g = {}
    dd, dg2 = _gate_bwd(dx_out, sv["down"], sv["g2"])
    da = _mm("mm_down_dx", dd, wl["w_down"], dims=NT, grid=(nb, 2, 1),
             a_block=(bm, D), a_map=lambda i, j, k: (i, 0),
             b_block=(1408, D), b_map=lambda i, j, k: (j, 0),
             out_shape=(L, DFF), o_block=(bm, 1408), o_map=lambda i, j, k: (i, j))
    g["w_down"] = _mm("mm_down_dw", sv["a"], dd, dims=TN, grid=(1, 2, nk),
                      a_block=(bk, DFF), a_map=lambda i, j, k: (k, 0),
                      b_block=(bk, 512), b_map=lambda i, j, k: (k, j),
                      out_shape=(DFF, D), o_block=(DFF, 512), o_map=lambda i, j, k: (0, j))
    dup, g["conv_w"], g["conv_b"], recv = _convglu_bwd(da, sv["up_pre"], wl["conv_w"], wl["conv_b"],
                                                       _swap_comm(prev) if prev else None)
    sums = _chip_sums(prev, recv) if prev else None
    dh2 = _mm("mm_up_dx", dup, wl["w_up"], dims=NT, grid=(nb, 1, 4),
              a_block=(None, bm, 1408), a_map=lambda i, j, k: (k // 2, i, k % 2),
              b_block=(None, D, 1408), b_map=lambda i, j, k: (k, 0, 0),
              out_shape=(L, D), o_block=(bm, D), o_map=lambda i, j, k: (i, 0))
    g["w_up"] = _mm("mm_up_dw", sv["h2"], dup, dims=TN, grid=(1, 4, nk),
                    a_block=(bk, D), a_map=lambda i, j, k: (k, 0),
                    b_block=(None, bk, 1408), b_map=lambda i, j, k: (j // 2, k, j % 2),
                    out_shape=(NCHIP, D, 1408), o_block=(None, D, 1408), o_map=lambda i, j, k: (j, 0, 0))
    dx1, dsh2, dsc2, g["norm2_g"] = _rmsmod_bwd(dh2, sv["x1"], wl["norm2_g"], sv["sc2"], dx_out)
    dproj, dg1 = _gate_bwd(dx1, sv["proj"], sv["g1"])
    do = _mm("mm_out_dx", dproj, wl["w_out"], dims=NT, grid=(nb, 2, 1),
             a_block=(bm, D), a_map=lambda i, j, k: (i, 0),
             b_block=(512, D), b_map=lambda i, j, k: (j, 0),
             out_shape=(L, D), o_block=(bm, 512), o_map=lambda i, j, k: (i, j))
    g["w_out"] = _mm("mm_out_dw", sv["o"], dproj, dims=TN, grid=(1, 2, nk),
                     a_block=(bk, D), a_map=lambda i, j, k: (k, 0),
                     b_block=(bk, 512), b_map=lambda i, j, k: (k, j),
                     out_shape=(D, D), o_block=(D, 512), o_map=lambda i, j, k: (0, j))
    doa, dos, g["attn_out_g"], g["ssm_out_g"] = _outnorm_bwd(do, sv["oa"], sv["os"],
                                                             wl["attn_out_g"], wl["ssm_out_g"])
    (dqn, dkn, dv), parts = _attn_bwd(sv["qkv"], sv["oa"], doa, _scatter_comm(sums) if prev else None)
    prev_reduced = _reduced(parts, sums) if prev else None
    (du, d_abar_re, d_abar_im, g["ssm_d"], g["glu_b"], d_ccr, d_cci, d_bbre, d_bbim, d_gw) = _s5_bwd(
        dos, sv["p"], sv["s_re"], sv["s_im"], wl["abar_re"], wl["abar_im"], wl["bbr"], wl["bbi"],
        wl["ccr"], wl["cci"], wl["dsk"], wl["gw"], wl["gb"])
    g["ssm_c_re"] = d_ccr.reshape(NG, NP, NH16).transpose(0, 2, 1)
    g["ssm_c_im"] = -d_cci.reshape(NG, NP, NH16).transpose(0, 2, 1)
    g["glu_w"] = d_gw.reshape(NG, NH16, NH16)
    d_ldt, d_ar, d_ai, d_br, d_bi = _s5_disc_bwd(
        wl["ldt_col"], wl["ar_col"], wl["ai_col"], wl["br_mat"], wl["bi_mat"],
        d_abar_re.reshape(NS, 1), d_abar_im.reshape(NS, 1), d_bbre, d_bbim)
    g["ssm_log_dt"] = d_ldt[:, 0]
    g["ssm_a_re"] = d_ar.reshape(NG, NP)
    g["ssm_a_im"] = d_ai.reshape(NG, NP)
    g["ssm_b_re"] = d_br.reshape(NG, NP, NH16)
    g["ssm_b_im"] = d_bi.reshape(NG, NP, NH16)
    dp, dgqk = _qknorm_bwd(dqn, dkn, dv, du, sv["p"], wl["gqk"])
    g["q_norm_g"] = dgqk[0, :HD]
    g["k_norm_g"] = dgqk[0, HD:2 * HD]
    dh = _mm("mm_in_dx", dp, wl["w_in"], dims=NT, grid=(nb, 1, 4),
             a_block=(bm, 512), a_map=lambda i, j, k: (i, k),
             b_block=(None, D, 512), b_map=lambda i, j, k: (k, 0, 0),
             out_shape=(L, D), o_block=(bm, D), o_map=lambda i, j, k: (i, 0))
    g["w_in"] = _mm("mm_in_dw", sv["h"], dp, dims=TN, grid=(1, 4, nk),
                    a_block=(bk, D), a_map=lambda i, j, k: (k, 0),
                    b_block=(bk, 512), b_map=lambda i, j, k: (k, j),
                    out_shape=(NCHIP, D, 512), o_block=(None, D, 512), o_map=lambda i, j, k: (j, 0, 0))
    dx_in, dsh1, dsc1, g["norm1_g"] = _rmsmod_bwd(dh, sv["xin"], wl["norm1_g"], sv["sc1"], dx1)
    g["dmod"] = jnp.concatenate([dsh1, dsc1, dg1, dsh2, dsc2, dg2], axis=1)[0]
    return dx_in, g, prev_reduced


def _prep_layer(l, conv_w, small):
    wl = {}
    wl["conv_w"] = conv_w[l]
    wl["conv_b"] = small["ffn_conv_b"][l][None, :]
    wl["norm1_g"] = small["norm1_g"][l][None, :]
    wl["norm2_g"] = small["norm2_g"][l][None, :]
    wl["attn_out_g"] = small["attn_out_g"][l][None, :]
    wl["ssm_out_g"] = small["ssm_out_g"][l][None, :]
    wl["gqk"] = jnp.concatenate([jnp.tile(small["q_norm_g"][l], 8), jnp.tile(small["k_norm_g"][l], 8)])[None, :]
    wl["ldt_col"] = jnp.repeat(small["ssm_log_dt"][l], NP)[:, None]
    wl["ar_col"] = small["ssm_a_re"][l].reshape(NS, 1)
    wl["ai_col"] = small["ssm_a_im"][l].reshape(NS, 1)
    wl["br_mat"] = small["ssm_b_re"][l].reshape(NS, NH16)
    wl["bi_mat"] = small["ssm_b_im"][l].reshape(NS, NH16)
    abar_re, abar_im, bb_re, bb_im = _s5_disc_fwd(wl["ldt_col"], wl["ar_col"], wl["ai_col"],
                                                  wl["br_mat"], wl["bi_mat"])
    wl["abar_re"] = abar_re.reshape(1, NS)
    wl["abar_im"] = abar_im.reshape(1, NS)
    wl["bbr"] = _block_diag(bb_re.reshape(NG, NP, NH16).transpose(0, 2, 1)).astype(bf16)
    wl["bbi"] = _block_diag(bb_im.reshape(NG, NP, NH16).transpose(0, 2, 1)).astype(bf16)
    wl["ccr"] = _block_diag(small["ssm_c_re"][l].transpose(0, 2, 1)).astype(bf16)
    wl["cci"] = _block_diag(small["ssm_c_im"][l].transpose(0, 2, 1)).astype(bf16)
    wl["gw"] = _block_diag(small["glu_w"][l]).astype(bf16)
    wl["dsk"] = small["ssm_d"][l].reshape(1, SSMW)
    wl["gb"] = small["glu_b"][l].reshape(1, SSMW)
    return wl


def _local_step(x, target, mods, shards, conv_w, small):
    def set_big(wl, gathered):
        w_in, w_out, w_up, w_down = gathered
        wl.update(w_in=w_in, w_up=w_up, w_out=w_out.reshape(D, D), w_down=w_down.reshape(DFF, D))

    def layer_shards(l):
        return [s[l] for s in shards]

    wls = [_prep_layer(l, conv_w, small) for l in range(DEPTH)]
    first = _gather_comm(layer_shards(0))
    set_big(wls[0], _gather_done(_comm_call("weight_gather", first), first["ins"]))
    saved = []
    cur, res, res_gate = x, None, None
    for l in range(DEPTH):
        nxt = _gather_comm(layer_shards(l + 1)) if l + 1 < DEPTH else None
        cur, res, res_gate, sv, couts = _layer_fwd(cur, res, res_gate, wls[l], mods[l], nxt)
        saved.append(sv)
        if nxt is not None:
            set_big(wls[l + 1], _gather_done(couts, nxt["ins"]))
    loss, dx = _loss_fwd_bwd(cur, res, res_gate, target)
    grads = [None] * DEPTH
    big = [None] * DEPTH
    prev = None
    for l in reversed(range(DEPTH)):
        dx, grads[l], done = _layer_bwd(dx, wls[l], saved[l], prev)
        if prev is not None:
            big[l + 1] = done
        prev = _big_grads(grads[l])
    recv = _comm_call("grad_swap", _swap_comm(prev))
    sums = _chip_sums(prev, recv)
    big[0] = _reduced(_comm_call("grad_scatter", _scatter_comm(sums)), sums)
    return loss, dx, grads, big


BIG = ("w_in", "w_out", "ffn_w_up", "ffn_w_down")
SMALL = ("ada_b", "norm1_g", "q_norm_g", "k_norm_g", "ssm_a_re", "ssm_a_im", "ssm_log_dt",
         "ssm_b_re", "ssm_b_im", "ssm_c_re", "ssm_c_im", "ssm_d", "glu_w", "glu_b",
         "attn_out_g", "ssm_out_g", "norm2_g", "ffn_conv_b")
NAMES = ("ada_w", "ada_b", "norm1_g", "w_in", "q_norm_g", "k_norm_g", "ssm_a_re", "ssm_a_im",
         "ssm_log_dt", "ssm_b_re", "ssm_b_im", "ssm_c_re", "ssm_c_im", "ssm_d", "glu_w", "glu_b",
         "attn_out_g", "ssm_out_g", "w_out", "norm2_g", "ffn_w_up", "ffn_conv_w", "ffn_conv_b",
         "ffn_w_down")
PACK_COLS = 1024


def _pack(arrs):
    flat = jnp.concatenate([a.reshape(-1) for a in arrs])
    rows = -(-flat.shape[0] // PACK_COLS)
    rows = -(-rows // 8) * 8
    flat = jnp.pad(flat, (0, rows * PACK_COLS - flat.shape[0]))
    return flat.reshape(rows, PACK_COLS)


def _unpack(packed, shapes):
    flat = packed.reshape(-1)
    out, off = [], 0
    for s in shapes:
        n = math.prod(s)
        out.append(flat[off:off + n].reshape(s))
        off += n
    return out


def kernel(x, c, ada_w, ada_b, norm1_g, w_in, q_norm_g, k_norm_g, ssm_a_re, ssm_a_im, ssm_log_dt, ssm_b_re, ssm_b_im, ssm_c_re, ssm_c_im, ssm_d, glu_w, glu_b, attn_out_g, ssm_out_g, w_out, norm2_g, ffn_w_up, ffn_conv_w, ffn_conv_b, ffn_w_down, loss_target, m_ada_w, m_ada_b, m_norm1_g, m_w_in, m_q_norm_g, m_k_norm_g, m_ssm_a_re, m_ssm_a_im, m_ssm_log_dt, m_ssm_b_re, m_ssm_b_im, m_ssm_c_re, m_ssm_c_im, m_ssm_d, m_glu_w, m_glu_b, m_attn_out_g, m_ssm_out_g, m_w_out, m_norm2_g, m_ffn_w_up, m_ffn_conv_w, m_ffn_conv_b, m_ffn_w_down, v_ada_w, v_ada_b, v_norm1_g, v_w_in, v_q_norm_g, v_k_norm_g, v_ssm_a_re, v_ssm_a_im, v_ssm_log_dt, v_ssm_b_re, v_ssm_b_im, v_ssm_c_re, v_ssm_c_im, v_ssm_d, v_glu_w, v_glu_b, v_attn_out_g, v_ssm_out_g, v_w_out, v_norm2_g, v_ffn_w_up, v_ffn_conv_w, v_ffn_conv_b, v_ffn_w_down):
    env = dict(locals())
    w = {n: env[n] for n in NAMES}
    m = {n: env["m_" + n] for n in NAMES}
    v = {n: env["v_" + n] for n in NAMES}
    mx, my, mc = _my_pos()
    chip = 2 * mx + my
    dev = 4 * mx + 2 * my + mc
    xl = x[0]
    tl = loss_target[0]

    c_all = _allgather8("gather_c", jnp.pad(c, ((0, 7), (0, 0))))[:, 0, :]
    c16 = jnp.pad(c_all, ((0, 8), (0, 0)))
    nloc = NMOD * D // NCHIP
    ada_b_loc = lax.dynamic_slice(ada_b, (0, chip * nloc), (DEPTH, nloc))[:, None, :]
    mod_loc = _mod_fwd(c16, ada_w, ada_b_loc)
    mod_g = _allgather8("gather_mod", mod_loc.reshape(DEPTH * 16, nloc))
    mod_all = mod_g[0::2].reshape(NCHIP, DEPTH, 16, nloc).transpose(1, 2, 0, 3).reshape(DEPTH, 16, NMOD * D)
    mods = lax.dynamic_index_in_dim(mod_all, dev, axis=1, keepdims=False)

    shards = [_cast_bf16(w[n]) for n in BIG]
    ncw = DUP // NCHIP
    conv_g = _allgather8("gather_conv", jnp.pad(w["ffn_conv_w"].reshape(DEPTH * 3, ncw), ((0, 4), (0, 0))))
    conv_w = conv_g[0::2, :DEPTH * 3].reshape(NCHIP, DEPTH, 3, ncw).transpose(1, 2, 0, 3).reshape(DEPTH, 3, DUP)
    small = {n: w[n] for n in SMALL}

    loss_loc, grad_x, grads, big = _local_step(xl, tl, mods, shards, conv_w, small)
    loss = lax.psum(loss_loc, ("x", "y", "c"))

    outs = {}
    for k, name in enumerate(BIG):
        g_big = jnp.stack([big[l][k] for l in range(DEPTH)])
        outs[name] = _adamw(g_big[None], w[name], m[name], v[name])

    small_names = SMALL[1:] + ("ffn_conv_w_full",)
    small_grads = []
    for n in SMALL[1:]:
        key = {"ffn_conv_b": "conv_b"}.get(n, n)
        small_grads.append(jnp.stack([grads[l][key].reshape(w[n].shape[1:]) for l in range(DEPTH)]))
    small_grads.append(jnp.stack([grads[l]["conv_w"] for l in range(DEPTH)]))
    dmod = jnp.stack([grads[l]["dmod"] for l in range(DEPTH)])
    packed = _pack([dmod] + small_grads)
    allp = _allgather8("gather_small", packed)
    shapes = [dmod.shape] + [a.shape for a in small_grads]
    dmod_all = allp[:, :DEPTH * NMOD, :].reshape(NDEV, DEPTH, NMOD * D)
    dmod_loc = lax.dynamic_slice(dmod_all, (0, 0, chip * nloc), (NDEV, DEPTH, nloc)).transpose(1, 0, 2)
    g_ada = _ada_grad(c16, jnp.pad(dmod_loc, ((0, 0), (0, 8), (0, 0))))
    outs["ada_w"] = _adamw(g_ada[None], w["ada_w"], m["ada_w"], v["ada_w"])
    conv_w_shape = (DEPTH, 3, DUP)
    w_small = _pack([w[n] for n in SMALL] + [jnp.zeros(conv_w_shape, f32)])
    m_small = _pack([m[n] for n in SMALL] + [jnp.zeros(conv_w_shape, f32)])
    v_small = _pack([v[n] for n in SMALL] + [jnp.ones(conv_w_shape, f32)])
    res_small = _adamw(allp, w_small, m_small, v_small)
    unpacked = [_unpack(r, shapes) for r in res_small]
    for i, n in enumerate(SMALL):
        outs[n] = tuple(unpacked[k][i] for k in range(4))
    g_conv_full = unpacked[0][len(SMALL)]
    ncw = DUP // NCHIP
    g_conv = lax.dynamic_slice(g_conv_full, (0, 0, chip * ncw), (DEPTH, 3, ncw))
    outs["ffn_conv_w"] = _adamw(g_conv[None], w["ffn_conv_w"], m["ffn_conv_w"], v["ffn_conv_w"])

    result = [loss, grad_x[None]]
    for k in range(4):
        result += [outs[n][k] for n in NAMES]
    return tuple(result)
```

```python
import functools
import math

import jax
import jax.numpy as jnp
from jax import lax
from jax.experimental import pallas as pl
from jax.experimental.pallas import tpu as pltpu

f32 = jnp.float32
bf16 = jnp.bfloat16
_MXU = jnp.bfloat16

D = 1024
ATT = 512
SSMW = 512
HD = 64
NG = 32
NP = 64
NH16 = 16
NS = NG * NP
INW = 2048
DFF = 2816
DUP = 2 * DFF
NMOD = 6
DEPTH = 4
EPS = 1e-6
NCHIP = 4
NDEV = 8

ADAM_LR = 0.001
ADAM_B1 = 0.9
ADAM_B2 = 0.999
ADAM_EPS = 1e-08
ADAM_WD = 0.01
ADAM_STEP = 10

VMEM_LIMIT = 56 * 1024 * 1024
MESH = pl.DeviceIdType.MESH

NN = (((1,), (0,)), ((), ()))
NT = (((1,), (1,)), ((), ()))
TN = (((0,), (0,)), ((), ()))


def _cp(sem=None):
    if sem is None:
        return pltpu.CompilerParams(vmem_limit_bytes=VMEM_LIMIT)
    return pltpu.CompilerParams(dimension_semantics=sem, vmem_limit_bytes=VMEM_LIMIT)


def _dot(a, b, dims=NN):
    return lax.dot_general(a.astype(_MXU), b.astype(_MXU), dims, preferred_element_type=f32)


def _dot_exact(x, m):
    hi = x.astype(bf16)
    r1 = x - hi.astype(f32)
    mid = r1.astype(bf16)
    lo = (r1 - mid.astype(f32)).astype(bf16)
    d = lambda p: lax.dot_general(p, m, NN, preferred_element_type=f32)
    return d(hi) + d(mid) + d(lo)


def _gelu(x):
    c = math.sqrt(2.0 / math.pi)
    return 0.5 * x * (1.0 + jnp.tanh(c * (x + 0.044715 * (x * x * x))))


def _gelu_grad(x):
    c = math.sqrt(2.0 / math.pi)
    t = jnp.tanh(c * (x + 0.044715 * (x * x * x)))
    return 0.5 * (1.0 + t) + 0.5 * x * (1.0 - t * t) * c * (1.0 + 3.0 * 0.044715 * (x * x))


def _sigmoid(x):
    return 1.0 / (1.0 + jnp.exp(-x))


def _mm(name, a, b, *, dims, grid, a_block, a_map, b_block, b_map, out_shape, o_block, o_map,
        out_dtype=f32):
    nk = grid[2]
    acc_shape = tuple(s for s in o_block if s is not None)

    def body(a_ref, b_ref, o_ref, acc_ref):
        k = pl.program_id(2)
        part = _dot(a_ref[...], b_ref[...], dims)
        if nk == 1:
            o_ref[...] = part.astype(o_ref.dtype)
        else:
            @pl.when(k == 0)
            def _():
                acc_ref[...] = part

            @pl.when(k > 0)
            def _():
                acc_ref[...] += part

            @pl.when(k == nk - 1)
            def _():
                o_ref[...] = acc_ref[...].astype(o_ref.dtype)

    return pl.pallas_call(
        body, name=name, grid=grid,
        in_specs=[pl.BlockSpec(a_block, a_map), pl.BlockSpec(b_block, b_map)],
        out_specs=pl.BlockSpec(o_block, o_map),
        out_shape=jax.ShapeDtypeStruct(out_shape, out_dtype),
        scratch_shapes=[pltpu.VMEM(acc_shape if nk > 1 else (8, 128), f32)],
        compiler_params=_cp(("parallel", "parallel", "arbitrary")),
    )(a, b)


def _row_block(L, want):
    return want if L % want == 0 else L


def _rmsmod_fwd(x, res, gate, g, sh, sc):
    L = x.shape[0]
    bm = _row_block(L, 256)
    with_res = res is not None

    def body(*refs):
        if with_res:
            x_ref, r_ref, gt_ref, g_ref, sh_ref, sc_ref, xo_ref, h_ref = refs
            xin = x_ref[...] + gt_ref[...] * r_ref[...]
            xo_ref[...] = xin
        else:
            x_ref, g_ref, sh_ref, sc_ref, h_ref = refs
            xin = x_ref[...]
        inv = lax.rsqrt(jnp.mean(xin * xin, axis=-1, keepdims=True) + EPS)
        xn = xin * inv * g_ref[...]
        h_ref[...] = (xn * (1.0 + sc_ref[...]) + sh_ref[...]).astype(h_ref.dtype)

    row = pl.BlockSpec((bm, D), lambda i: (i, 0))
    vec = pl.BlockSpec((1, D), lambda i: (0, 0))
    if with_res:
        return pl.pallas_call(
            body, name="rmsmod_res_fwd", grid=(L // bm,),
            in_specs=[row, row, vec, vec, vec, vec], out_specs=[row, row],
            out_shape=[jax.ShapeDtypeStruct((L, D), f32), jax.ShapeDtypeStruct((L, D), bf16)],
            compiler_params=_cp(("parallel",)),
        )(x, res, gate, g, sh, sc)
    h = pl.pallas_call(
        body, name="rmsmod_fwd", grid=(L // bm,),
        in_specs=[row, vec, vec, vec], out_specs=row,
        out_shape=jax.ShapeDtypeStruct((L, D), bf16),
        compiler_params=_cp(("parallel",)),
    )(x, g, sh, sc)
    return None, h


def _rmsmod_bwd(dh, x, g, sc, dres):
    L = x.shape[0]
    bm = _row_block(L, 256)

    def body(dh_ref, x_ref, g_ref, sc_ref, dr_ref, dx_ref, dsh_ref, dsc_ref, dg_ref):
        i = pl.program_id(0)
        xv = x_ref[...]
        dhv = dh_ref[...]
        inv = lax.rsqrt(jnp.mean(xv * xv, axis=-1, keepdims=True) + EPS)
        xh = xv * inv
        gv = g_ref[...]
        xn = xh * gv
        dxn = dhv * (1.0 + sc_ref[...])
        dxh = dxn * gv
        dx_ref[...] = inv * (dxh - xh * jnp.mean(dxh * xh, axis=-1, keepdims=True)) + dr_ref[...]
        p_sh = jnp.sum(dhv, axis=0, keepdims=True)
        p_sc = jnp.sum(dhv * xn, axis=0, keepdims=True)
        p_g = jnp.sum(dxn * xh, axis=0, keepdims=True)

        @pl.when(i == 0)
        def _():
            dsh_ref[...] = p_sh
            dsc_ref[...] = p_sc
            dg_ref[...] = p_g

        @pl.when(i > 0)
        def _():
            dsh_ref[...] += p_sh
            dsc_ref[...] += p_sc
            dg_ref[...] += p_g

    row = pl.BlockSpec((bm, D), lambda i: (i, 0))
    vec = pl.BlockSpec((1, D), lambda i: (0, 0))
    return pl.pallas_call(
        body, name="rmsmod_bwd", grid=(L // bm,),
        in_specs=[row, row, vec, vec, row], out_specs=[row, vec, vec, vec],
        out_shape=[jax.ShapeDtypeStruct((L, D), f32)] + [jax.ShapeDtypeStruct((1, D), f32)] * 3,
        compiler_params=_cp(("arbitrary",)),
    )(dh, x, g, sc, dres)


def _gate_bwd(dx, y, gate):
    L = dx.shape[0]
    bm = _row_block(L, 256)

    def body(dx_ref, y_ref, gt_ref, dy_ref, dgt_ref):
        i = pl.program_id(0)
        dxv = dx_ref[...]
        dy_ref[...] = (gt_ref[...] * dxv).astype(dy_ref.dtype)
        part = jnp.sum(dxv * y_ref[...], axis=0, keepdims=True)

        @pl.when(i == 0)
        def _():
            dgt_ref[...] = part

        @pl.when(i > 0)
        def _():
            dgt_ref[...] += part

    row = pl.BlockSpec((bm, D), lambda i: (i, 0))
    vec = pl.BlockSpec((1, D), lambda i: (0, 0))
    return pl.pallas_call(
        body, name="gate_bwd", grid=(L // bm,),
        in_specs=[row, row, vec], out_specs=[row, vec],
        out_shape=[jax.ShapeDtypeStruct((L, D), bf16), jax.ShapeDtypeStruct((1, D), f32)],
        compiler_params=_cp(("arbitrary",)),
    )(dx, y, gate)


def _loss_fwd_bwd(x1, down, gate, target):
    L = x1.shape[0]
    bm = _row_block(L, 256)
    nsteps = L // bm

    def body(x_ref, d_ref, gt_ref, t_ref, dy_ref, loss_ref, acc_ref):
        i = pl.program_id(0)
        diff = x_ref[...] + gt_ref[...] * d_ref[...] - t_ref[...]
        dy_ref[...] = diff * (1.0 / D)
        part = jnp.sum(diff * diff, axis=0, keepdims=True)

        @pl.when(i == 0)
        def _():
            acc_ref[...] = part

        @pl.when(i > 0)
        def _():
            acc_ref[...] += part

        @pl.when(i == nsteps - 1)
        def _():
            tot = jnp.sum(acc_ref[...], axis=1, keepdims=True) * (0.5 / D)
            loss_ref[...] = jnp.broadcast_to(tot, (8, 128))

    row = pl.BlockSpec((bm, D), lambda i: (i, 0))
    vec = pl.BlockSpec((1, D), lambda i: (0, 0))
    dy, loss = pl.pallas_call(
        body, name="loss_fwd_bwd", grid=(nsteps,),
        in_specs=[row, row, vec, row],
        out_specs=[row, pl.BlockSpec((8, 128), lambda i: (0, 0))],
        out_shape=[jax.ShapeDtypeStruct((L, D), f32), jax.ShapeDtypeStruct((8, 128), f32)],
        scratch_shapes=[pltpu.VMEM((1, D), f32)],
        compiler_params=_cp(("arbitrary",)),
    )(x1, down, gate, target)
    return loss[0, 0], dy


def _head_mean_matrix():
    r = lax.broadcasted_iota(jnp.int32, (128, 128), 0) // HD
    c = lax.broadcasted_iota(jnp.int32, (128, 128), 1) // HD
    return jnp.where(r == c, 1.0 / HD, 0.0).astype(bf16)


def _qknorm_fwd(p, gqk):
    L = p.shape[0]
    bm = _row_block(L, 256)

    def body(p_ref, g_ref, o_ref):
        e = _head_mean_matrix()
        for c in range(8):
            sl = slice(128 * c, 128 * (c + 1))
            xv = p_ref[:, sl]
            inv = lax.rsqrt(_dot_exact(xv * xv, e) + EPS)
            yv = xv * inv * g_ref[:, sl]
            if c < 4:
                yv = yv * (1.0 / math.sqrt(HD))
            o_ref[:, sl] = yv.astype(o_ref.dtype)
        o_ref[:, 1024:1536] = p_ref[:, 1024:1536].astype(o_ref.dtype)

    return pl.pallas_call(
        body, name="qknorm_fwd", grid=(L // bm,),
        in_specs=[pl.BlockSpec((bm, 1536), lambda i: (i, 0)), pl.BlockSpec((1, 1024), lambda i: (0, 0))],
        out_specs=pl.BlockSpec((bm, 1536), lambda i: (i, 0)),
        out_shape=jax.ShapeDtypeStruct((L, 1536), bf16),
        compiler_params=_cp(("parallel",)),
    )(p, gqk)


def _qknorm_bwd(dqn, dkn, dval, du, p, gqk):
    L = p.shape[0]
    bm = _row_block(L, 256)
    nsteps = L // bm

    def body(dq_ref, dk_ref, dv_ref, du_ref, p_ref, g_ref, o_ref, dg_ref, acc_ref):
        i = pl.program_id(0)
        e = _head_mean_matrix()
        o_ref[:, 1024:1536] = dv_ref[...].astype(o_ref.dtype)
        o_ref[:, 1536:2048] = du_ref[...].astype(o_ref.dtype)
        for c in range(8):
            sl = slice(128 * c, 128 * (c + 1))
            xv = p_ref[:, sl]
            d_ref = dq_ref if c < 4 else dk_ref
            dv = d_ref[:, 128 * (c % 4):128 * (c % 4 + 1)]
            gv = g_ref[:, sl]
            inv = lax.rsqrt(_dot_exact(xv * xv, e) + EPS)
            xh = xv * inv
            dxh = dv * gv
            o_ref[:, sl] = (inv * (dxh - xh * _dot_exact(dxh * xh, e))).astype(o_ref.dtype)
            part = jnp.sum(dv * xh, axis=0, keepdims=True)

            @pl.when(i == 0)
            def _():
                acc_ref[:, sl] = part

            @pl.when(i > 0)
            def _():
                acc_ref[:, sl] += part

        @pl.when(i == nsteps - 1)
        def _():
            r = lax.broadcasted_iota(jnp.int32, (1024, 128), 0)
            col = lax.broadcasted_iota(jnp.int32, (1024, 128), 1)
            fold = jnp.where(col == (r // 512) * HD + r % HD, 1.0, 0.0).astype(bf16)
            dg_ref[...] = _dot_exact(jnp.broadcast_to(acc_ref[...], (8, 1024)), fold)

    half = pl.BlockSpec((bm, 512), lambda i: (i, 0))
    return pl.pallas_call(
        body, name="qknorm_bwd", grid=(nsteps,),
        in_specs=[half, half, half, half, pl.BlockSpec((bm, 1024), lambda i: (i, 0)),
                  pl.BlockSpec((1, 1024), lambda i: (0, 0))],
        out_specs=[pl.BlockSpec((bm, INW), lambda i: (i, 0)), pl.BlockSpec((8, 128), lambda i: (0, 0))],
        out_shape=[jax.ShapeDtypeStruct((L, INW), bf16), jax.ShapeDtypeStruct((8, 128), f32)],
        scratch_shapes=[pltpu.VMEM((1, 1024), f32)],
        compiler_params=_cp(("arbitrary",)),
    )(dqn, dkn, dval, du, p, gqk)


def _outnorm_fwd(oa, os_, ga, gs):
    L = oa.shape[0]
    bm = _row_block(L, 256)

    def body(a_ref, s_ref, ga_ref, gs_ref, o_ref):
        for x_ref, g_ref, off in ((a_ref, ga_ref, 0), (s_ref, gs_ref, 512)):
            xv = x_ref[...]
            inv = lax.rsqrt(jnp.mean(xv * xv, axis=-1, keepdims=True) + EPS)
            o_ref[:, off:off + 512] = (xv * inv * g_ref[...]).astype(o_ref.dtype)

    half = pl.BlockSpec((bm, 512), lambda i: (i, 0))
    vec = pl.BlockSpec((1, 512), lambda i: (0, 0))
    return pl.pallas_call(
        body, name="outnorm_fwd", grid=(L // bm,),
        in_specs=[half, half, vec, vec], out_specs=pl.BlockSpec((bm, D), lambda i: (i, 0)),
        out_shape=jax.ShapeDtypeStruct((L, D), bf16),
        compiler_params=_cp(("parallel",)),
    )(oa, os_, ga, gs)


def _outnorm_bwd(do, oa, os_, ga, gs):
    L = oa.shape[0]
    bm = _row_block(L, 256)

    def body(do_ref, a_ref, s_ref, ga_ref, gs_ref, da_ref, ds_ref, dga_ref, dgs_ref):
        i = pl.program_id(0)
        for x_ref, g_ref, dx_ref, dg_ref, off in ((a_ref, ga_ref, da_ref, dga_ref, 0),
                                                  (s_ref, gs_ref, ds_ref, dgs_ref, 512)):
            xv = x_ref[...]
            dv = do_ref[:, off:off + 512]
            inv = lax.rsqrt(jnp.mean(xv * xv, axis=-1, keepdims=True) + EPS)
            xh = xv * inv
            dxh = dv * g_ref[...]
            dx_ref[...] = inv * (dxh - xh * jnp.mean(dxh * xh, axis=-1, keepdims=True))
            part = jnp.sum(dv * xh, axis=0, keepdims=True)

            @pl.when(i == 0)
            def _():
                dg_ref[...] = part

            @pl.when(i > 0)
            def _():
                dg_ref[...] += part

    half = pl.BlockSpec((bm, 512), lambda i: (i, 0))
    vec = pl.BlockSpec((1, 512), lambda i: (0, 0))
    return pl.pallas_call(
        body, name="outnorm_bwd", grid=(L // bm,),
        in_specs=[pl.BlockSpec((bm, D), lambda i: (i, 0)), half, half, vec, vec],
        out_specs=[half, half, vec, vec],
        out_shape=[jax.ShapeDtypeStruct((L, 512), f32)] * 2 + [jax.ShapeDtypeStruct((1, 512), f32)] * 2,
        compiler_params=_cp(("arbitrary",)),
    )(do, oa, os_, ga, gs)


NCB = DFF // 128


def _conv_rows(L):
    return 64 if L % 64 == 0 else 32


def _conv_chunk(x_ref, w_ref, b_ref, i, nch, R, L, ahead):
    base = pl.multiple_of(i * R, R)
    prev = x_ref[pl.ds(pl.multiple_of(jnp.maximum(base - 8, 0), 8), 8), :]
    parts = [jnp.where(i > 0, prev, 0.0), x_ref[pl.ds(base, R), :]]
    if ahead:
        nxt = x_ref[pl.ds(pl.multiple_of(jnp.minimum(base + R, L - 8), 8), 8), :]
        parts.append(jnp.where(i < nch - 1, nxt, 0.0))
    xe = jnp.concatenate(parts, axis=0)
    x1 = pltpu.roll(xe, 1, 0)
    x2 = pltpu.roll(xe, 2, 0)
    u = b_ref[...] + x2 * w_ref[0:1, :]
    u = u + x1 * w_ref[1:2, :]
    u = u + xe * w_ref[2:3, :]
    return u[8:], xe[8:], x1[8:], x2[8:]


def _fold8(x):
    out = x[0:8]
    for q in range(1, x.shape[0] // 8):
        out = out + x[8 * q:8 * (q + 1)]
    return out


def _convglu_fwd(up_pre, cw, cb):
    L = up_pre.shape[0]

    def body(v_ref, g_ref, wv_ref, wg_ref, bv_ref, bg_ref, a_ref):
        row = lax.broadcasted_iota(jnp.int32, (L, 128), 0)

        def conv(x_ref, w_ref, b_ref):
            x = x_ref[...]
            out = b_ref[...] + jnp.where(row >= 2, pltpu.roll(x, 2, 0), 0.0) * w_ref[0:1, :]
            out = out + jnp.where(row >= 1, pltpu.roll(x, 1, 0), 0.0) * w_ref[1:2, :]
            return out + x * w_ref[2:3, :]

        val = conv(v_ref, wv_ref, bv_ref)
        a_ref[...] = (_gelu(conv(g_ref, wg_ref, bg_ref)) * val).astype(a_ref.dtype)

    col = lambda off: pl.BlockSpec((L, 128), lambda j: (0, j + off))
    wsp = lambda off: pl.BlockSpec((3, 128), lambda j: (0, j + off))
    bsp = lambda off: pl.BlockSpec((1, 128), lambda j: (0, j + off))
    return pl.pallas_call(
        body, name="convglu_fwd", grid=(NCB,),
        in_specs=[col(0), col(NCB), wsp(0), wsp(NCB), bsp(0), bsp(NCB)],
        out_specs=pl.BlockSpec((L, 128), lambda j: (0, j)),
        out_shape=jax.ShapeDtypeStruct((L, DFF), bf16),
        compiler_params=_cp(("parallel",)),
    )(up_pre, up_pre, cw, cw, cb, cb)


def _convglu_bwd(da, up_pre, cw, cb, comm=None):
    L = up_pre.shape[0]
    R = _conv_rows(L)
    nch = L // R
    n = R + 8

    def body(da_ref, v_ref, g_ref, wv_ref, wg_ref, bv_ref, bg_ref, dx_ref, dw_ref, db_ref):
        def chunk(i, acc):
            base = pl.multiple_of(i * R, R)
            val, xv, xv1, xv2 = _conv_chunk(v_ref, wv_ref, bv_ref, i, nch, R, L, True)
            gate, xg, xg1, xg2 = _conv_chunk(g_ref, wg_ref, bg_ref, i, nch, R, L, True)
            nxt = da_ref[pl.ds(pl.multiple_of(jnp.minimum(base + R, L - 8), 8), 8), :]
            dav = jnp.concatenate([da_ref[pl.ds(base, R), :], jnp.where(i < nch - 1, nxt, 0.0)], axis=0)
            c = math.sqrt(2.0 / math.pi)
            t = jnp.tanh(c * (gate + 0.044715 * (gate * gate * gate)))
            d_val = dav * (0.5 * gate * (1.0 + t))
            d_gate = dav * val * (0.5 * (1.0 + t)
                                  + 0.5 * gate * (1.0 - t * t) * c * (1.0 + 3.0 * 0.044715 * (gate * gate)))
            new = []
            for k, (d, w_ref, xs) in enumerate(((d_val, wv_ref, (xv2, xv1, xv)), (d_gate, wg_ref, (xg2, xg1, xg)))):
                dx = d * w_ref[2:3, :] + pltpu.roll(d, n - 1, 0) * w_ref[1:2, :]
                dx = dx + pltpu.roll(d, n - 2, 0) * w_ref[0:1, :]
                dx_ref[k, pl.ds(base, R), :] = dx[:R].astype(dx_ref.dtype)
                dr = d[:R]
                new += [_fold8(dr * x[:R]) for x in xs] + [_fold8(dr)]
            return tuple(a + b for a, b in zip(acc, new))

        acc = lax.fori_loop(0, nch, chunk, (jnp.zeros((8, 128), f32),) * 8)
        for k in range(2):
            for r in range(3):
                dw_ref[k, r:r + 1, :] = jnp.sum(acc[4 * k + r], axis=0, keepdims=True)
            db_ref[k] = jnp.sum(acc[4 * k + 3], axis=0, keepdims=True)

    col = lambda off: pl.BlockSpec((L, 128), lambda j: (0, j + off))
    wsp = lambda off: pl.BlockSpec((3, 128), lambda j: (0, j + off))
    bsp = lambda off: pl.BlockSpec((1, 128), lambda j: (0, j + off))
    (dx, dw, db), couts = _call(
        body, comm, name="convglu_bwd", grid=(NCB,),
        in_specs=[col(0), col(0), col(NCB), wsp(0), wsp(NCB), bsp(0), bsp(NCB)],
        out_specs=[pl.BlockSpec((2, L, 128), lambda j: (0, 0, j)), pl.BlockSpec((2, 3, 128), lambda j: (0, 0, j)),
                   pl.BlockSpec((2, 1, 128), lambda j: (0, 0, j))],
        out_shape=[jax.ShapeDtypeStruct((2, L, DFF), bf16), jax.ShapeDtypeStruct((2, 3, DFF), f32),
                   jax.ShapeDtypeStruct((2, 1, DFF), f32)], scratch_shapes=[],
        sem=("parallel",), args=(da, up_pre, up_pre, cw, cw, cb, cb))
    return dx, dw.transpose(1, 0, 2).reshape(3, DUP), db.transpose(1, 0, 2).reshape(1, DUP), couts


def _attn_block(L):
    return 256 if L % 256 == 0 else 128


TAIL_DEAD = -110.0


def _dot_exact2(x, m):
    hi = x.astype(bf16)
    lo = (x - hi.astype(f32)).astype(bf16)
    return (lax.dot_general(hi, m, NN, preferred_element_type=f32)
            + lax.dot_general(lo, m, NN, preferred_element_type=f32))


def _sb_weights(z, mask, tri_gt, carry):
    l1p = jnp.log(1.0 + jnp.exp(-jnp.abs(z)))
    ls_pos = jnp.minimum(z, 0.0) - l1p
    lm = ls_pos - z
    if mask is not None:
        lm = jnp.where(mask, lm, 0.0)
    tail = _dot_exact2(lm, tri_gt) + carry
    w = jnp.exp(ls_pos + tail)
    if mask is not None:
        w = jnp.where(mask, w, 0.0)
    return w, ls_pos, lm


def _head_masks(B):
    lane = lax.broadcasted_iota(jnp.int32, (B, 128), 1)
    return lane < HD, lane >= HD


def _attn_fwd(qkv, comm=None):
    L = qkv.shape[0]
    B = _attn_block(L)
    nq = L // B

    def body(q_ref, k_ref, v_ref, o_ref):
        qi = pl.program_id(1)
        heads = _head_masks(B)
        ti = lax.broadcasted_iota(jnp.int32, (B, B), 0)
        si = lax.broadcasted_iota(jnp.int32, (B, B), 1)
        tri_gt = jnp.where(ti > si, 1.0, 0.0).astype(bf16)
        diag = si < ti
        qv = q_ref[...]
        zero16 = jnp.zeros((), bf16)

        def tile(jb, mask, acc, carries):
            ks = pl.multiple_of(jb * B, B)
            kb = k_ref[pl.ds(ks, B), :]
            vb = v_ref[pl.ds(ks, B), :]
            out = []
            for in_head, carry in zip(heads, carries):
                kh = jnp.where(in_head, kb, zero16)
                vh = jnp.where(in_head, vb, zero16)
                z = lax.dot_general(qv, kh, NT, preferred_element_type=f32)
                w, _, lm = _sb_weights(z, mask, tri_gt, carry)
                acc = acc + lax.dot_general(w.astype(bf16), vh, NN, preferred_element_type=f32)
                out.append(carry + jnp.sum(lm, axis=1, keepdims=True))
            return acc, out

        zc = jnp.zeros((B, 1), f32)
        acc, (c0, c1) = tile(qi, diag, jnp.zeros((B, 128), f32), (zc, zc))

        def cond(st):
            return jnp.logical_and(st[0] <= qi, st[4] > TAIL_DEAD)

        def step(st):
            n, a, p0, p1, _ = st
            a, (p0, p1) = tile(qi - n, None, a, (p0, p1))
            return n + 1, a, p0, p1, jnp.maximum(jnp.max(p0), jnp.max(p1))

        st = lax.while_loop(cond, step, (jnp.int32(1), acc, c0, c1, jnp.maximum(jnp.max(c0), jnp.max(c1))))
        o_ref[...] = st[1]

    (o,), couts = _call(
        body, comm, name="attn_fwd", grid=(4, nq),
        in_specs=[pl.BlockSpec((B, 128), lambda hp, i: (i, hp)),
                  pl.BlockSpec((L, 128), lambda hp, i: (0, 4 + hp)),
                  pl.BlockSpec((L, 128), lambda hp, i: (0, 8 + hp))],
        out_specs=[pl.BlockSpec((B, 128), lambda hp, i: (i, hp))],
        out_shape=[jax.ShapeDtypeStruct((L, ATT), f32)], scratch_shapes=[],
        sem=("parallel", "parallel"), args=(qkv, qkv, qkv))
    return o, couts


def _attn_bwd(qkv, o, do, comm=None):
    L = qkv.shape[0]
    B = _attn_block(L)
    nq = L // B

    def body(q_ref, k_ref, v_ref, o_ref, do_ref, dq_ref, dk_ref, dv_ref):
        qi = pl.program_id(1)

        @pl.when(qi == 0)
        def _():
            dk_ref[...] = jnp.zeros_like(dk_ref)
            dv_ref[...] = jnp.zeros_like(dv_ref)

        heads = _head_masks(B)
        ti = lax.broadcasted_iota(jnp.int32, (B, B), 0)
        si = lax.broadcasted_iota(jnp.int32, (B, B), 1)
        tri_gt = jnp.where(ti > si, 1.0, 0.0).astype(bf16)
        tri_ge = jnp.where(ti >= si, 1.0, 0.0).astype(bf16)
        diag = si < ti
        zero16 = jnp.zeros((), bf16)
        qv = q_ref[...]
        do16 = do_ref[...].astype(bf16)
        dsum_lanes = do16.astype(f32) * o_ref[...]
        qhs = [jnp.where(m, qv, zero16) for m in heads]
        dohs = [jnp.where(m, do16, zero16) for m in heads]
        dsums = [jnp.sum(jnp.where(m, dsum_lanes, 0.0), axis=1, keepdims=True) for m in heads]

        def tile(jb, mask, dq_a, state):
            ks = pl.multiple_of(jb * B, B)
            kb = k_ref[pl.ds(ks, B), :]
            vb = v_ref[pl.ds(ks, B), :]
            dk_blk = jnp.zeros((B, 128), f32)
            dv_blk = jnp.zeros((B, 128), f32)
            out = []
            for in_head, qh, doh, dsum, (carry, suffix) in zip(heads, qhs, dohs, dsums, state):
                kh = jnp.where(in_head, kb, zero16)
                vh = jnp.where(in_head, vb, zero16)
                z = lax.dot_general(qv, kh, NT, preferred_element_type=f32)
                w, ls_pos, lm = _sb_weights(z, mask, tri_gt, carry)
                w16 = w.astype(bf16)
                dw = lax.dot_general(doh, vh, NT, preferred_element_type=f32)
                da = w16.astype(f32) * dw
                sig = jnp.exp(ls_pos)
                suf = _dot_exact2(da, tri_ge) + suffix
                dz = da * (1.0 - sig) - sig * (dsum - suf)
                if mask is not None:
                    dz = jnp.where(mask, dz, 0.0)
                dz16 = dz.astype(bf16)
                dq_a = dq_a + lax.dot_general(dz16, kh, NN, preferred_element_type=f32)
                dk_blk = dk_blk + lax.dot_general(dz16, qh, TN, preferred_element_type=f32)
                dv_blk = dv_blk + lax.dot_general(w16, doh, TN, preferred_element_type=f32)
                out.append((carry + jnp.sum(lm, axis=1, keepdims=True),
                            suffix + jnp.sum(da, axis=1, keepdims=True)))
            dk_ref[pl.ds(ks, B), :] += dk_blk
            dv_ref[pl.ds(ks, B), :] += dv_blk
            return dq_a, out

        def alive(state):
            return jnp.maximum(jnp.max(state[0][0]), jnp.max(state[1][0]))

        zc = jnp.zeros((B, 1), f32)
        dq_acc, state = tile(qi, diag, jnp.zeros((B, 128), f32), ((zc, zc), (zc, zc)))

        def cond(st):
            return jnp.logical_and(st[0] <= qi, st[6] > TAIL_DEAD)

        def step(st):
            n, a, c0, s0, c1, s1, _ = st
            a, new = tile(qi - n, None, a, ((c0, s0), (c1, s1)))
            return n + 1, a, new[0][0], new[0][1], new[1][0], new[1][1], alive(new)

        st = lax.while_loop(cond, step, (jnp.int32(1), dq_acc, state[0][0], state[0][1],
                                         state[1][0], state[1][1], alive(state)))
        dq_ref[...] = st[1] * (1.0 / math.sqrt(HD))

    blk = pl.BlockSpec((B, 128), lambda hp, i: (i, hp))
    return _call(
        body, comm, name="attn_bwd", grid=(4, nq),
        in_specs=[blk,
                  pl.BlockSpec((L, 128), lambda hp, i: (0, 4 + hp)),
                  pl.BlockSpec((L, 128), lambda hp, i: (0, 8 + hp)),
                  blk, blk],
        out_specs=[blk, pl.BlockSpec((L, 128), lambda hp, i: (0, hp)),
                   pl.BlockSpec((L, 128), lambda hp, i: (0, hp))],
        out_shape=[jax.ShapeDtypeStruct((L, ATT), f32)] * 3, scratch_shapes=[],
        sem=("parallel", "arbitrary"), args=(qkv, qkv, qkv, o, do))


def _s5_disc(ldt, ar, ai, br, bi):
    dt = jnp.exp(ldt)
    mag = jnp.exp(dt * ar)
    abar_re = mag * jnp.cos(dt * ai)
    abar_im = mag * jnp.sin(dt * ai)
    em_re = abar_re - 1.0
    em_im = abar_im
    den = ar * ar + ai * ai
    f_re = (em_re * ar + em_im * ai) / den
    f_im = (em_im * ar - em_re * ai) / den
    bb_re = f_re * br - f_im * bi
    bb_im = f_re * bi + f_im * br
    return abar_re, abar_im, bb_re, bb_im


def _s5_disc_fwd(ldt, ar, ai, br, bi):
    def body(ldt_ref, ar_ref, ai_ref, br_ref, bi_ref, o1, o2, o3, o4):
        outs = _s5_disc(ldt_ref[...], ar_ref[...], ai_ref[...], br_ref[...], bi_ref[...])
        for o_ref, v in zip((o1, o2, o3, o4), outs):
            o_ref[...] = v

    col = jax.ShapeDtypeStruct((NS, 1), f32)
    mat = jax.ShapeDtypeStruct((NS, NH16), f32)
    return pl.pallas_call(body, name="s5_disc_fwd", out_shape=[col, col, mat, mat],
                          compiler_params=_cp())(ldt, ar, ai, br, bi)


def _s5_disc_bwd(ldt, ar, ai, br, bi, d_are, d_aim, d_bbre, d_bbim):
    def body(ldt_ref, ar_ref, ai_ref, br_ref, bi_ref, c1, c2, c3, c4, o_ldt, o_ar, o_ai, o_br, o_bi):
        prim = (ldt_ref[...], ar_ref[...], ai_ref[...], br_ref[...], bi_ref[...])
        _, vjp = jax.vjp(_s5_disc, *prim)
        g_ldt, g_ar, g_ai, g_br, g_bi = vjp((c1[...], c2[...], c3[...], c4[...]))
        o_ar[...] = g_ar
        o_ai[...] = g_ai
        o_br[...] = g_br
        o_bi[...] = g_bi
        r = lax.broadcasted_iota(jnp.int32, (NG, NS), 0)
        c = lax.broadcasted_iota(jnp.int32, (NG, NS), 1)
        fold = jnp.where(c // NP == r, 1.0, 0.0).astype(bf16)
        hi = jnp.broadcast_to(g_ldt, (NS, 128))
        p1 = hi.astype(bf16)
        r1 = hi - p1.astype(f32)
        p2 = r1.astype(bf16)
        p3 = (r1 - p2.astype(f32)).astype(bf16)
        dd = lambda p: lax.dot_general(fold, p, NN, preferred_element_type=f32)
        o_ldt[...] = dd(p1) + dd(p2) + dd(p3)

    col = jax.ShapeDtypeStruct((NS, 1), f32)
    mat = jax.ShapeDtypeStruct((NS, NH16), f32)
    return pl.pallas_call(
        body, name="s5_disc_bwd",
        out_shape=[jax.ShapeDtypeStruct((NG, 128), f32), col, col, mat, mat],
        compiler_params=_cp())(ldt, ar, ai, br, bi, d_are, d_aim, d_bbre, d_bbim)


SCAN_LANES = 512


def _cmul(ar, ai, br, bi):
    return ar * br - ai * bi, ar * bi + ai * br


def _pow_tables(ar, ai):
    a2 = _cmul(ar, ai, ar, ai)
    a4 = _cmul(*a2, *a2)
    rows = [(ar, ai)]
    for _ in range(7):
        rows.append(_cmul(*rows[-1], ar, ai))
    return (ar, ai), a2, a4, rows


def _scan_rows(xr_ref, xi_ref, abar_re, abar_im, cr_ref, ci_ref, reverse, on_tile=None):
    T = xr_ref.shape[0]
    nt = T // 8
    row = lax.broadcasted_iota(jnp.int32, (8, SCAN_LANES), 0)
    for c in range(NS // SCAN_LANES):
        ls = slice(c * SCAN_LANES, (c + 1) * SCAN_LANES)
        ar = abar_re[:, ls]
        ai = abar_im[:, ls]
        if reverse:
            ai = -ai
        a1, a2, a4, prow = _pow_tables(ar, ai)
        if reverse:
            prow = prow[::-1]
        pr = jnp.concatenate([p[0] for p in prow], axis=0)
        pi = jnp.concatenate([p[1] for p in prow], axis=0)
        a8r, a8i = prow[0] if reverse else prow[7]
        edge = slice(0, 1) if reverse else slice(7, 8)
        steps = []
        for d, (dr, di) in ((1, a1), (2, a2), (4, a4)):
            keep = (row < 8 - d) if reverse else (row >= d)
            steps.append((8 - d if reverse else d, jnp.where(keep, dr, 0.0), jnp.where(keep, di, 0.0)))

        def local(n, _):
            rs = pl.ds(pl.multiple_of(n * 8, 8), 8)
            xr = xr_ref[rs, ls]
            xi = xi_ref[rs, ls]
            for shift, dr, di in steps:
                mr, mi = _cmul(dr, di, pltpu.roll(xr, shift, 0), pltpu.roll(xi, shift, 0))
                xr = xr + mr
                xi = xi + mi
            xr_ref[rs, ls] = xr
            xi_ref[rs, ls] = xi
            return 0

        lax.fori_loop(0, nt, local, 0, unroll=4)

        def chain(n, carry):
            cr, ci = carry
            it = (nt - 1 - n) if reverse else n
            rs = pl.ds(pl.multiple_of(it * 8, 8), 8)
            xr = xr_ref[rs, ls]
            xi = xi_ref[rs, ls]
            nr, ni = _cmul(a8r, a8i, cr, ci)
            nr = nr + xr[edge, :]
            ni = ni + xi[edge, :]
            mr, mi = _cmul(pr, pi, cr, ci)
            xr = xr + mr
            xi = xi + mi
            xr_ref[rs, ls] = xr
            xi_ref[rs, ls] = xi
            if on_tile is not None:
                nxt_r = jnp.where(row < 7, pltpu.roll(xr, 7, 0), cr)
                nxt_i = jnp.where(row < 7, pltpu.roll(xi, 7, 0), ci)
                on_tile(ls, rs, nxt_r, nxt_i)
            return nr, ni

        cr, ci = lax.fori_loop(0, nt, chain, (cr_ref[:, ls], ci_ref[:, ls]), unroll=2)
        cr_ref[:, ls] = cr
        ci_ref[:, ls] = ci


def _s5_chunk(L):
    return 256 if L % 256 == 0 else L


def _s5_gate(s_re, s_im, u, ccr_ref, cci_ref, dsk_ref, gw_ref, gb_ref):
    y = _dot(s_re, ccr_ref[...]) - _dot(s_im, cci_ref[...]) + dsk_ref[...] * u
    yg = _gelu(y)
    gate = _sigmoid(_dot(yg, gw_ref[...]) + gb_ref[...])
    return y, yg, gate


def _s5_fwd(p, abar_re, abar_im, bbr, bbi, ccr, cci, dsk, gw, gb):
    L = p.shape[0]
    T = _s5_chunk(L)

    def body(u_ref, are_ref, aim_ref, bbr_ref, bbi_ref, ccr_ref, cci_ref, dsk_ref, gw_ref, gb_ref,
             o_ref, sr_ref, si_ref, cr_ref, ci_ref):
        @pl.when(pl.program_id(0) == 0)
        def _():
            cr_ref[...] = jnp.zeros_like(cr_ref)
            ci_ref[...] = jnp.zeros_like(ci_ref)

        u = u_ref[...]
        sr_ref[...] = _dot(u, bbr_ref[...])
        si_ref[...] = _dot(u, bbi_ref[...])
        _scan_rows(sr_ref, si_ref, are_ref[...], aim_ref[...], cr_ref, ci_ref, reverse=False)
        _, yg, gate = _s5_gate(sr_ref[...], si_ref[...], u, ccr_ref, cci_ref, dsk_ref, gw_ref, gb_ref)
        o_ref[...] = yg * gate

    full = lambda shape: pl.BlockSpec(shape, lambda i: (0, 0))
    return pl.pallas_call(
        body, name="s5_fwd", grid=(L // T,),
        in_specs=[pl.BlockSpec((T, SSMW), lambda i: (i, 3)), full((1, NS)), full((1, NS)),
                  full((SSMW, NS)), full((SSMW, NS)), full((NS, SSMW)), full((NS, SSMW)),
                  full((1, SSMW)), full((SSMW, SSMW)), full((1, SSMW))],
        out_specs=[pl.BlockSpec((T, SSMW), lambda i: (i, 0)), pl.BlockSpec((T, NS), lambda i: (i, 0)),
                   pl.BlockSpec((T, NS), lambda i: (i, 0))],
        out_shape=[jax.ShapeDtypeStruct((L, SSMW), f32), jax.ShapeDtypeStruct((L, NS), f32),
                   jax.ShapeDtypeStruct((L, NS), f32)],
        scratch_shapes=[pltpu.VMEM((1, NS), f32), pltpu.VMEM((1, NS), f32)],
        compiler_params=_cp(("arbitrary",)),
    )(p, abar_re, abar_im, bbr, bbi, ccr, cci, dsk, gw, gb)


def _diag_fold(acc, rpg):
    aw = acc.shape[0]
    r = lax.broadcasted_iota(jnp.int32, (aw, 128), 0) // rpg
    c = lax.broadcasted_iota(jnp.int32, (aw, 128), 1) // NH16
    fr = lax.broadcasted_iota(jnp.int32, (128, NH16), 0) % NH16
    fc = lax.broadcasted_iota(jnp.int32, (128, NH16), 1)
    fold = jnp.where(fr == fc, 1.0, 0.0).astype(bf16)
    return _dot_exact(jnp.where(r == c, acc, 0.0), fold)


def _s5_bwd(dout, p, s_re, s_im, abar_re, abar_im, bbr, bbi, ccr, cci, dsk, gw, gb):
    L = p.shape[0]
    T = _s5_chunk(L)
    nchunk = L // T

    def body(do_ref, u_ref, sr_ref, si_ref, are_ref, aim_ref, bbr_ref, bbi_ref, ccr_ref, cci_ref,
             dsk_ref, gw_ref, gb_ref,
             du_ref, dare_ref, daim_ref, dd_ref, dgb_ref, dccr_ref, dcci_ref, dbbr_ref, dbbi_ref, dgw_ref,
             lr_ref, li_ref, cr_ref, ci_ref, accr_ref, acci_ref, wacc_ref, gacc_ref):
        i = pl.program_id(0)

        @pl.when(i == 0)
        def _():
            cr_ref[...] = jnp.zeros_like(cr_ref)
            ci_ref[...] = jnp.zeros_like(ci_ref)
            accr_ref[...] = jnp.zeros_like(accr_ref)
            acci_ref[...] = jnp.zeros_like(acci_ref)
            dd_ref[...] = jnp.zeros_like(dd_ref)
            dgb_ref[...] = jnp.zeros_like(dgb_ref)
            wacc_ref[...] = jnp.zeros_like(wacc_ref)
            gacc_ref[...] = jnp.zeros_like(gacc_ref)

        u = u_ref[...]
        dov = do_ref[...]
        y, yg, gate = _s5_gate(sr_ref[...], si_ref[...], u, ccr_ref, cci_ref, dsk_ref, gw_ref, gb_ref)
        dgl = dov * yg * gate * (1.0 - gate)
        dyg = dov * gate + _dot(dgl, gw_ref[...], NT)
        dy = dyg * _gelu_grad(y)
        u16, dy16, yg16, dgl16 = (t.astype(_MXU) for t in (u, dy, yg, dgl))
        for q in range(4):
            lanes = slice(128 * q, 128 * (q + 1))
            wide_l = slice(512 * q, 512 * (q + 1))
            wacc_ref[0, q] += _dot(sr_ref[:, wide_l], dy16[:, lanes], TN)
            wacc_ref[1, q] += _dot(si_ref[:, wide_l], dy16[:, lanes], TN)
            gacc_ref[q] += _dot(yg16[:, lanes], dgl16[:, lanes], TN)
        dd_ref[...] += jnp.sum(dy * u, axis=0, keepdims=True)
        dgb_ref[...] += jnp.sum(dgl, axis=0, keepdims=True)
        lr_ref[...] = _dot(dy, ccr_ref[...], NT)
        li_ref[...] = -_dot(dy, cci_ref[...], NT)

        def on_tile(ls, rs, nxt_r, nxt_i):
            s_r = sr_ref[rs, ls]
            s_i = si_ref[rs, ls]
            accr_ref[:, ls] += nxt_r * s_r + nxt_i * s_i
            acci_ref[:, ls] += nxt_i * s_r - nxt_r * s_i

        _scan_rows(lr_ref, li_ref, are_ref[...], aim_ref[...], cr_ref, ci_ref, reverse=True,
                   on_tile=on_tile)
        for q in range(4):
            lanes = slice(128 * q, 128 * (q + 1))
            wide_l = slice(512 * q, 512 * (q + 1))
            wacc_ref[2, q] += _dot(lr_ref[:, wide_l], u16[:, lanes], TN)
            wacc_ref[3, q] += _dot(li_ref[:, wide_l], u16[:, lanes], TN)
        du_ref[...] = (dy * dsk_ref[...] + _dot(lr_ref[...], bbr_ref[...], NT)
                       + _dot(li_ref[...], bbi_ref[...], NT))

        @pl.when(i == nchunk - 1)
        def _():
            dare_ref[...] = jnp.sum(accr_ref[...], axis=0, keepdims=True)
            daim_ref[...] = jnp.sum(acci_ref[...], axis=0, keepdims=True)
            for m, o_ref in enumerate((dccr_ref, dcci_ref, dbbr_ref, dbbi_ref)):
                for q in range(4):
                    o_ref[512 * q:512 * (q + 1), :] = _diag_fold(wacc_ref[m, q], NP)
            for q in range(4):
                dgw_ref[128 * q:128 * (q + 1), :] = _diag_fold(gacc_ref[q], NH16)

    rev = lambda i: (nchunk - 1 - i, 0)
    full = lambda shape: pl.BlockSpec(shape, lambda i: (0, 0))
    half = pl.BlockSpec((T, SSMW), rev)
    wide = pl.BlockSpec((T, NS), rev)
    return pl.pallas_call(
        body, name="s5_bwd", grid=(nchunk,),
        in_specs=[half, pl.BlockSpec((T, SSMW), lambda i: (nchunk - 1 - i, 3)), wide, wide,
                  full((1, NS)), full((1, NS)), full((SSMW, NS)), full((SSMW, NS)),
                  full((NS, SSMW)), full((NS, SSMW)), full((1, SSMW)), full((SSMW, SSMW)), full((1, SSMW))],
        out_specs=[half, full((1, NS)), full((1, NS)), full((1, SSMW)), full((1, SSMW)),
                   full((NS, NH16)), full((NS, NH16)), full((NS, NH16)), full((NS, NH16)),
                   full((SSMW, NH16))],
        out_shape=[jax.ShapeDtypeStruct((L, SSMW), f32)] + [jax.ShapeDtypeStruct((1, NS), f32)] * 2
                  + [jax.ShapeDtypeStruct((1, SSMW), f32)] * 2 + [jax.ShapeDtypeStruct((NS, NH16), f32)] * 4
                  + [jax.ShapeDtypeStruct((SSMW, NH16), f32)],
        scratch_shapes=[pltpu.VMEM((T, NS), f32), pltpu.VMEM((T, NS), f32),
                        pltpu.VMEM((1, NS), f32), pltpu.VMEM((1, NS), f32),
                        pltpu.VMEM((8, NS), f32), pltpu.VMEM((8, NS), f32),
                        pltpu.VMEM((4, 4, 512, 128), f32), pltpu.VMEM((4, 128, 128), f32)],
        compiler_params=_cp(("arbitrary",)),
    )(dout, p, s_re, s_im, abar_re, abar_im, bbr, bbi, ccr, cci, dsk, gw, gb)


NTILE = NS // 128
TILE_GROUP = 4


def _seg_entry(lr, li, ar, ai, cr, ci, seg_rows, reverse, row):
    pr, pi = ar, ai
    for _ in range(seg_rows.bit_length() - 1):
        pr, pi = _cmul(pr, pi, pr, pi)
    if reverse:
        xr = jnp.where(row == 7, cr, pltpu.roll(lr, 7, 0))
        xi = jnp.where(row == 7, ci, pltpu.roll(li, 7, 0))
    else:
        xr = jnp.where(row == 0, cr, pltpu.roll(lr, 1, 0))
        xi = jnp.where(row == 0, ci, pltpu.roll(li, 1, 0))
    for d in (1, 2, 4):
        keep = (row < 8 - d) if reverse else (row >= d)
        shift = 8 - d if reverse else d
        mr, mi = _cmul(pr, pi, jnp.where(keep, pltpu.roll(xr, shift, 0), 0.0),
                       jnp.where(keep, pltpu.roll(xi, shift, 0), 0.0))
        xr, xi = xr + mr, xi + mi
        pr, pi = _cmul(pr, pi, pr, pi)
    return xr, xi


def _seg_scan(xr3, xi3, abar_re, abar_im, cr_ref, ci_ref, reverse, on_step=None):
    T = xr3.shape[1]
    seg = T // 8
    row = lax.broadcasted_iota(jnp.int32, (8, 128), 0)
    for g0 in range(0, NTILE, TILE_GROUP):
        tiles = list(range(g0, g0 + TILE_GROUP))
        lanes = [slice(128 * j, 128 * (j + 1)) for j in tiles]
        ar = [jnp.broadcast_to(abar_re[:, ls], (8, 128)) for ls in lanes]
        ai = [jnp.broadcast_to(-abar_im[:, ls] if reverse else abar_im[:, ls], (8, 128)) for ls in lanes]

        def sweep(store):
            def step(n, st):
                k = (seg - 1 - n) if reverse else n
                rows = pl.ds(k, 8, stride=seg)
                new = []
                for q, j in enumerate(tiles):
                    sr, si = st[2 * q], st[2 * q + 1]
                    if store and on_step is not None:
                        on_step(j, rows, sr, si)
                    nr = ar[q] * sr - ai[q] * si + xr3.at[j][rows, :]
                    ni = ar[q] * si + ai[q] * sr + xi3.at[j][rows, :]
                    if store:
                        xr3.at[j][rows, :] = nr
                        xi3.at[j][rows, :] = ni
                    new += [nr, ni]
                return tuple(new)
            return step

        zero = jnp.zeros((8, 128), f32)
        ends = lax.fori_loop(0, seg, sweep(False), (zero,) * (2 * TILE_GROUP))
        init = []
        for q, ls in enumerate(lanes):
            init += _seg_entry(ends[2 * q], ends[2 * q + 1], ar[q], ai[q], cr_ref[:, ls], ci_ref[:, ls],
                               seg, reverse, row)
        fin = lax.fori_loop(0, seg, sweep(True), tuple(init))
        edge = slice(0, 1) if reverse else slice(7, 8)
        for q, ls in enumerate(lanes):
            cr_ref[:, ls] = fin[2 * q][edge, :]
            ci_ref[:, ls] = fin[2 * q + 1][edge, :]


def _coarse(i):
    return slice(128 * i, 128 * (i + 1)), slice(512 * i, 512 * (i + 1))


def _tiles_cat(x3, i):
    return jnp.concatenate([x3[4 * i + q] for q in range(4)], axis=1)


def _s5_gate2(sr3, si3, u, ccr_ref, cci_ref, dsk_ref, gw_ref, gb_ref):
    ys = []
    for i in range(4):
        ch, st = _coarse(i)
        ys.append(_dot(_tiles_cat(sr3, i), ccr_ref[st, ch]) - _dot(_tiles_cat(si3, i), cci_ref[st, ch]))
    y = jnp.concatenate(ys, axis=1) + dsk_ref[...] * u
    yg = _gelu(y)
    gl = jnp.concatenate([_dot(yg[:, _coarse(i)[0]], gw_ref[_coarse(i)[0], _coarse(i)[0]]) for i in range(4)],
                         axis=1)
    return y, yg, _sigmoid(gl + gb_ref[...])


def _s5_fwd2(p, abar_re, abar_im, bbr, bbi, ccr, cci, dsk, gw, gb):
    L = p.shape[0]
    T = _s5_chunk(L)

    def body(u_ref, are_ref, aim_ref, bbr_ref, bbi_ref, ccr_ref, cci_ref, dsk_ref, gw_ref, gb_ref,
             o_ref, sr3, si3, cr_ref, ci_ref):
        @pl.when(pl.program_id(0) == 0)
        def _():
            cr_ref[...] = jnp.zeros_like(cr_ref)
            ci_ref[...] = jnp.zeros_like(ci_ref)

        u = u_ref[...]
        for i in range(4):
            ch, st = _coarse(i)
            br = _dot(u[:, ch], bbr_ref[ch, st])
            bi = _dot(u[:, ch], bbi_ref[ch, st])
            for q in range(4):
                sr3[4 * i + q] = br[:, 128 * q:128 * (q + 1)]
                si3[4 * i + q] = bi[:, 128 * q:128 * (q + 1)]
        _seg_scan(sr3, si3, are_ref[...], aim_ref[...], cr_ref, ci_ref, reverse=False)
        _, yg, gate = _s5_gate2(sr3, si3, u, ccr_ref, cci_ref, dsk_ref, gw_ref, gb_ref)
        o_ref[...] = yg * gate

    full = lambda shape: pl.BlockSpec(shape, lambda i: (0, 0))
    st3 = pl.BlockSpec((NTILE, T, 128), lambda i: (0, i, 0))
    return pl.pallas_call(
        body, name="s5_fwd", grid=(L // T,),
        in_specs=[pl.BlockSpec((T, SSMW), lambda i: (i, 3)), full((1, NS)), full((1, NS)),
                  full((SSMW, NS)), full((SSMW, NS)), full((NS, SSMW)), full((NS, SSMW)),
                  full((1, SSMW)), full((SSMW, SSMW)), full((1, SSMW))],
        out_specs=[pl.BlockSpec((T, SSMW), lambda i: (i, 0)), st3, st3],
        out_shape=[jax.ShapeDtypeStruct((L, SSMW), f32), jax.ShapeDtypeStruct((NTILE, L, 128), f32),
                   jax.ShapeDtypeStruct((NTILE, L, 128), f32)],
        scratch_shapes=[pltpu.VMEM((1, NS), f32), pltpu.VMEM((1, NS), f32)],
        compiler_params=_cp(("arbitrary",)),
    )(p, abar_re, abar_im, bbr, bbi, ccr, cci, dsk, gw, gb)


def _s5_bwd2(dout, p, s_re, s_im, abar_re, abar_im, bbr, bbi, ccr, cci, dsk, gw, gb):
    L = p.shape[0]
    T = _s5_chunk(L)
    nchunk = L // T

    def body(do_ref, u_ref, sr3, si3, are_ref, aim_ref, bbr_ref, bbi_ref, ccr_ref, cci_ref,
             dsk_ref, gw_ref, gb_ref,
             du_ref, dare_ref, daim_ref, dd_ref, dgb_ref, dccr_ref, dcci_ref, dbbr_ref, dbbi_ref, dgw_ref,
             lr3, li3, cr_ref, ci_ref, accr3, acci3, wacc_ref, gacc_ref):
        i = pl.program_id(0)

        @pl.when(i == 0)
        def _():
            for ref in (cr_ref, ci_ref, accr3, acci3, dd_ref, dgb_ref, wacc_ref, gacc_ref):
                ref[...] = jnp.zeros_like(ref)

        u = u_ref[...]
        dov = do_ref[...]
        y, yg, gate = _s5_gate2(sr3, si3, u, ccr_ref, cci_ref, dsk_ref, gw_ref, gb_ref)
        dgl = dov * yg * gate * (1.0 - gate)
        dyg = dov * gate + jnp.concatenate(
            [_dot(dgl[:, _coarse(q)[0]], gw_ref[_coarse(q)[0], _coarse(q)[0]], NT) for q in range(4)], axis=1)
        dy = dyg * _gelu_grad(y)
        u16, dy16, yg16, dgl16 = (t.astype(_MXU) for t in (u, dy, yg, dgl))
        dd_ref[...] += jnp.sum(dy * u, axis=0, keepdims=True)
        dgb_ref[...] += jnp.sum(dgl, axis=0, keepdims=True)
        for q in range(4):
            ch, st = _coarse(q)
            wacc_ref[0, q] += _dot(_tiles_cat(sr3, q), dy16[:, ch], TN)
            wacc_ref[1, q] += _dot(_tiles_cat(si3, q), dy16[:, ch], TN)
            gacc_ref[q] += _dot(yg16[:, ch], dgl16[:, ch], TN)
            gr = _dot(dy16[:, ch], ccr_ref[st, ch], NT)
            gi = -_dot(dy16[:, ch], cci_ref[st, ch], NT)
            for t in range(4):
                lr3[4 * q + t] = gr[:, 128 * t:128 * (t + 1)]
                li3[4 * q + t] = gi[:, 128 * t:128 * (t + 1)]

        def on_step(j, rows, nxt_r, nxt_i):
            s_r = sr3.at[j][rows, :]
            s_i = si3.at[j][rows, :]
            accr3[j] += nxt_r * s_r + nxt_i * s_i
            acci3[j] += nxt_i * s_r - nxt_r * s_i

        _seg_scan(lr3, li3, are_ref[...], aim_ref[...], cr_ref, ci_ref, reverse=True, on_step=on_step)
        dus = []
        for q in range(4):
            ch, st = _coarse(q)
            lam_r, lam_i = _tiles_cat(lr3, q), _tiles_cat(li3, q)
            wacc_ref[2, q] += _dot(lam_r, u16[:, ch], TN)
            wacc_ref[3, q] += _dot(lam_i, u16[:, ch], TN)
            dus.append(_dot(lam_r, bbr_ref[ch, st], NT) + _dot(lam_i, bbi_ref[ch, st], NT))
        du_ref[...] = dy * dsk_ref[...] + jnp.concatenate(dus, axis=1)

        @pl.when(i == nchunk - 1)
        def _():
            for j in range(NTILE):
                ls = slice(128 * j, 128 * (j + 1))
                dare_ref[:, ls] = jnp.sum(accr3[j], axis=0, keepdims=True)
                daim_ref[:, ls] = jnp.sum(acci3[j], axis=0, keepdims=True)
            for m, o_ref in enumerate((dccr_ref, dcci_ref, dbbr_ref, dbbi_ref)):
                for q in range(4):
                    o_ref[512 * q:512 * (q + 1), :] = _diag_fold(wacc_ref[m, q], NP)
            for q in range(4):
                dgw_ref[128 * q:128 * (q + 1), :] = _diag_fold(gacc_ref[q], NH16)

    rev = lambda i: (nchunk - 1 - i, 0)
    full = lambda shape: pl.BlockSpec(shape, lambda i: (0, 0))
    half = pl.BlockSpec((T, SSMW), rev)
    st3 = pl.BlockSpec((NTILE, T, 128), lambda i: (0, nchunk - 1 - i, 0))
    return pl.pallas_call(
        body, name="s5_bwd", grid=(nchunk,),
        in_specs=[half, pl.BlockSpec((T, SSMW), lambda i: (nchunk - 1 - i, 3)), st3, st3,
                  full((1, NS)), full((1, NS)), full((SSMW, NS)), full((SSMW, NS)),
                  full((NS, SSMW)), full((NS, SSMW)), full((1, SSMW)), full((SSMW, SSMW)), full((1, SSMW))],
        out_specs=[half, full((1, NS)), full((1, NS)), full((1, SSMW)), full((1, SSMW)),
                   full((NS, NH16)), full((NS, NH16)), full((NS, NH16)), full((NS, NH16)),
                   full((SSMW, NH16))],
        out_shape=[jax.ShapeDtypeStruct((L, SSMW), f32)] + [jax.ShapeDtypeStruct((1, NS), f32)] * 2
                  + [jax.ShapeDtypeStruct((1, SSMW), f32)] * 2 + [jax.ShapeDtypeStruct((NS, NH16), f32)] * 4
                  + [jax.ShapeDtypeStruct((SSMW, NH16), f32)],
        scratch_shapes=[pltpu.VMEM((NTILE, T, 128), f32), pltpu.VMEM((NTILE, T, 128), f32),
                        pltpu.VMEM((1, NS), f32), pltpu.VMEM((1, NS), f32),
                        pltpu.VMEM((NTILE, 8, 128), f32), pltpu.VMEM((NTILE, 8, 128), f32),
                        pltpu.VMEM((4, 4, 512, 128), f32), pltpu.VMEM((4, 128, 128), f32)],
        compiler_params=_cp(("arbitrary",)),
    )(dout, p, s_re, s_im, abar_re, abar_im, bbr, bbi, ccr, cci, dsk, gw, gb)


def _block_diag(x):
    g, r, c = x.shape
    eye = jnp.eye(g, dtype=x.dtype)
    return (x[:, :, None, :] * eye[:, None, :, None]).reshape(g * r, g * c)


def _mod_fwd(c16, ada_w, ada_b_loc):
    nloc = ada_w.shape[2]

    def body(c_ref, w_ref, b_ref, o_ref):
        cv = c_ref[...]
        act = cv * _sigmoid(cv)
        o_ref[...] = _dot(act, w_ref[...]) + b_ref[...]

    return pl.pallas_call(
        body, name="mod_fwd", grid=(DEPTH,),
        in_specs=[pl.BlockSpec((16, D), lambda l: (0, 0)), pl.BlockSpec((None, D, nloc), lambda l: (l, 0, 0)),
                  pl.BlockSpec((None, 1, nloc), lambda l: (l, 0, 0))],
        out_specs=pl.BlockSpec((None, 16, nloc), lambda l: (l, 0, 0)),
        out_shape=jax.ShapeDtypeStruct((DEPTH, 16, nloc), f32),
        compiler_params=_cp(("parallel",)),
    )(c16, ada_w, ada_b_loc)


def _ada_grad(c16, dmod16):
    nloc = dmod16.shape[2]

    def body(c_ref, d_ref, o_ref):
        cv = c_ref[...]
        act = cv * _sigmoid(cv)
        o_ref[...] = _dot(act, d_ref[...], TN)

    return pl.pallas_call(
        body, name="ada_grad", grid=(DEPTH,),
        in_specs=[pl.BlockSpec((16, D), lambda l: (0, 0)), pl.BlockSpec((None, 16, nloc), lambda l: (l, 0, 0))],
        out_specs=pl.BlockSpec((None, D, nloc), lambda l: (l, 0, 0)),
        out_shape=jax.ShapeDtypeStruct((DEPTH, D, nloc), f32),
        compiler_params=_cp(("parallel",)),
    )(c16, dmod16)


def _as2d(a):
    return a.reshape(-1, a.shape[-1])


def _ew_rows(rows):
    for cand in (512, 256, 128, 64, 32, 16, 8):
        if rows % cand == 0:
            return cand
    return rows


def _cast_bf16(w):
    w2 = _as2d(w)
    rows, cols = w2.shape
    bm = _ew_rows(rows)

    def body(x_ref, o_ref):
        o_ref[...] = x_ref[...].astype(bf16)

    spec = pl.BlockSpec((bm, cols), lambda i: (i, 0))
    out = pl.pallas_call(body, name="cast_bf16", grid=(rows // bm,), in_specs=[spec], out_specs=spec,
                         out_shape=jax.ShapeDtypeStruct((rows, cols), bf16),
                         compiler_params=_cp(("parallel",)))(w2)
    return out.reshape(w.shape)


def _add2(a, b, out_dtype=f32):
    a2, b2 = _as2d(a), _as2d(b)
    rows, cols = a2.shape
    bm = _ew_rows(rows)

    def body(a_ref, b_ref, o_ref):
        o_ref[...] = (a_ref[...] + b_ref[...]).astype(o_ref.dtype)

    spec = pl.BlockSpec((bm, cols), lambda i: (i, 0))
    out = pl.pallas_call(body, name="add2", grid=(rows // bm,), in_specs=[spec, spec], out_specs=spec,
                         out_shape=jax.ShapeDtypeStruct((rows, cols), out_dtype),
                         compiler_params=_cp(("parallel",)))(a2, b2)
    return out.reshape(a.shape)


def _adamw(parts, w, m, v):
    n = parts.shape[0]
    w2, m2, v2 = _as2d(w), _as2d(m), _as2d(v)
    rows, cols = w2.shape
    p3 = parts.reshape(n, rows, cols)
    bm = _ew_rows(rows)
    if bm * cols * 4 > (1 << 20) and bm % 16 == 0:
        bm //= 2

    def body(p_ref, w_ref, m_ref, v_ref, g_ref, d_ref, nm_ref, nv_ref):
        g = p_ref[0]
        for k in range(1, n):
            g = g + p_ref[k]
        mn = ADAM_B1 * m_ref[...] + (1.0 - ADAM_B1) * g
        vn = ADAM_B2 * v_ref[...] + (1.0 - ADAM_B2) * (g * g)
        m_hat = mn / (1.0 - ADAM_B1 ** ADAM_STEP)
        v_hat = vn / (1.0 - ADAM_B2 ** ADAM_STEP)
        g_ref[...] = g
        d_ref[...] = -ADAM_LR * (m_hat / (jnp.sqrt(v_hat) + ADAM_EPS) + ADAM_WD * w_ref[...])
        nm_ref[...] = mn
        nv_ref[...] = vn

    spec = pl.BlockSpec((bm, cols), lambda i: (i, 0))
    outs = pl.pallas_call(
        body, name="adamw", grid=(rows // bm,),
        in_specs=[pl.BlockSpec((n, bm, cols), lambda i: (0, i, 0)), spec, spec, spec],
        out_specs=[spec] * 4, out_shape=[jax.ShapeDtypeStruct((rows, cols), f32)] * 4,
        compiler_params=_cp(("parallel",)),
    )(p3, w2, m2, v2)
    return tuple(o.reshape(w.shape) for o in outs)


ANY = pl.BlockSpec(memory_space=pl.ANY)


def _my_pos():
    return lax.axis_index("x"), lax.axis_index("y"), lax.axis_index("c")


def _allgather8(name, x):
    def body(x_ref, out_ref, send_sems, recv_sems):
        mx, my, mc = _my_pos()
        me, sibling = (mx, my, mc), (mx, my, 1 - mc)
        chips = [(1 - mx, my), (mx, 1 - my), (1 - mx, 1 - my)]

        def slot(px, py, pc):
            return out_ref.at[4 * px + 2 * py + pc]

        def copy(k, block, to, src=None):
            return pltpu.make_async_remote_copy(
                src_ref=slot(*block) if src is None else src, dst_ref=slot(*block),
                send_sem=send_sems.at[k], recv_sem=recv_sems.at[k], device_id=to, device_id_type=MESH)

        first = [copy(0, me, sibling, src=x_ref)]
        first += [copy(1 + j, me, (*chip, mc), src=x_ref) for j, chip in enumerate(chips)]
        for cp in first:
            cp.start()
        passed = [copy(4 + j, (*chip, mc), sibling) for j, chip in enumerate(chips)]
        for j, chip in enumerate(chips):
            copy(1 + j, (*chip, mc), me).wait_recv()
            passed[j].start()
        copy(0, sibling, me).wait_recv()
        for j, chip in enumerate(chips):
            copy(4 + j, (*chip, 1 - mc), me).wait_recv()
        for cp in first + passed:
            cp.wait_send()

    out = pl.pallas_call(
        body, name=name, in_specs=[ANY], out_specs=ANY,
        out_shape=jax.ShapeDtypeStruct((NDEV,) + x.shape, x.dtype),
        scratch_shapes=[pltpu.SemaphoreType.DMA((7,)), pltpu.SemaphoreType.DMA((7,))],
    )(x)
    dev = 4 * lax.axis_index("x") + 2 * lax.axis_index("y") + lax.axis_index("c")
    return lax.dynamic_update_index_in_dim(out, x, dev, 0)


HALF = DEPTH // 2


def _chip_gather(arrs):
    n = len(arrs)

    def body(*refs):
        ins, outs = refs[:n], refs[n:2 * n]
        send_sems, recv_sems = refs[2 * n:]
        mx, my, mc = _my_pos()
        chips = [(1 - mx, my), (mx, 1 - my), (1 - mx, 1 - my)]
        mine = 2 * mx + my
        my_half = pl.ds(HALF * mc, HALF)
        sib_half = pl.ds(HALF * (1 - mc), HALF)

        def ici(a, j, src_chip, half):
            px, py = chips[j]
            return pltpu.make_async_remote_copy(
                src_ref=ins[a].at[half], dst_ref=outs[a].at[src_chip, half],
                send_sem=send_sems.at[3 * a + j], recv_sem=recv_sems.at[3 * a + j],
                device_id=(px, py, mc), device_id_type=MESH)

        def d2d(a, j, half):
            px, py = chips[j]
            blk = outs[a].at[2 * px + py, half]
            return pltpu.make_async_remote_copy(
                src_ref=blk, dst_ref=blk,
                send_sem=send_sems.at[3 * n + 3 * a + j], recv_sem=recv_sems.at[3 * n + 3 * a + j],
                device_id=(mx, my, 1 - mc), device_id_type=MESH)

        sends = [ici(a, j, mine, my_half) for a in range(n) for j in range(3)]
        for cp in sends:
            cp.start()
        passed = []
        for a in range(n):
            for j, (px, py) in enumerate(chips):
                ici(a, j, 2 * px + py, my_half).wait_recv()
                fwd = d2d(a, j, my_half)
                fwd.start()
                passed.append(fwd)
        for a in range(n):
            for j in range(3):
                d2d(a, j, sib_half).wait_recv()
        for cp in sends + passed:
            cp.wait_send()

    outs = pl.pallas_call(
        body, name="chip_gather", in_specs=[ANY] * n, out_specs=[ANY] * n,
        out_shape=[jax.ShapeDtypeStruct((NCHIP,) + a.shape, a.dtype) for a in arrs],
        scratch_shapes=[pltpu.SemaphoreType.DMA((6 * n,)), pltpu.SemaphoreType.DMA((6 * n,))],
    )(*arrs)
    mine = 2 * lax.axis_index("x") + lax.axis_index("y")
    return [lax.dynamic_update_index_in_dim(o, a, mine, 0) for o, a in zip(outs, arrs)]


def _sibling_half_swap(arrs):
    n = len(arrs)

    def body(*refs):
        ins, outs = refs[:n], refs[n:2 * n]
        send_sems, recv_sems = refs[2 * n:]
        mx, my, mc = _my_pos()
        sib_half = pl.ds(HALF * (1 - mc), HALF)
        cps = [pltpu.make_async_remote_copy(
            src_ref=ins[a].at[j, sib_half], dst_ref=outs[a].at[j],
            send_sem=send_sems.at[NCHIP * a + j], recv_sem=recv_sems.at[NCHIP * a + j],
            device_id=(mx, my, 1 - mc), device_id_type=MESH) for a in range(n) for j in range(NCHIP)]
        for cp in cps:
            cp.start()
        for cp in cps:
            cp.wait()

    return pl.pallas_call(
        body, name="sibling_half_swap", in_specs=[ANY] * n, out_specs=[ANY] * n,
        out_shape=[jax.ShapeDtypeStruct((NCHIP, HALF) + a.shape[2:], a.dtype) for a in arrs],
        scratch_shapes=[pltpu.SemaphoreType.DMA((NCHIP * n,)), pltpu.SemaphoreType.DMA((NCHIP * n,))],
    )(*arrs)


def _sibling_merge_halves(arrs):
    n = len(arrs)

    def body(*refs):
        ins, outs = refs[:n], refs[n:2 * n]
        send_sems, recv_sems = refs[2 * n:]
        mx, my, mc = _my_pos()
        my_half = pl.ds(HALF * mc, HALF)
        cps = [pltpu.make_async_remote_copy(
            src_ref=ins[a], dst_ref=outs[a].at[my_half],
            send_sem=send_sems.at[a], recv_sem=recv_sems.at[a],
            device_id=(mx, my, 1 - mc), device_id_type=MESH) for a in range(n)]
        for cp in cps:
            cp.start()
        for cp in cps:
            cp.wait()

    outs = pl.pallas_call(
        body, name="sibling_merge_halves", in_specs=[ANY] * n, out_specs=[ANY] * n,
        out_shape=[jax.ShapeDtypeStruct((DEPTH,) + a.shape[1:], a.dtype) for a in arrs],
        scratch_shapes=[pltpu.SemaphoreType.DMA((n,)), pltpu.SemaphoreType.DMA((n,))],
    )(*arrs)
    start = HALF * lax.axis_index("c")
    return [lax.dynamic_update_slice_in_dim(o, a, start, 0) for o, a in zip(outs, arrs)]


def _sum_parts(parts):
    n = parts.shape[0]
    p3 = parts.reshape(n, -1, parts.shape[-1])
    rows, cols = p3.shape[1:]
    bm = _ew_rows(rows)
    if bm * cols * 4 > (1 << 20) and bm % 16 == 0:
        bm //= 2

    def body(p_ref, o_ref):
        g = p_ref[0].astype(f32)
        for k in range(1, n):
            g = g + p_ref[k].astype(f32)
        o_ref[...] = g

    out = pl.pallas_call(
        body, name="sum_parts", grid=(rows // bm,),
        in_specs=[pl.BlockSpec((n, bm, cols), lambda i: (0, i, 0))],
        out_specs=pl.BlockSpec((bm, cols), lambda i: (i, 0)),
        out_shape=jax.ShapeDtypeStruct((rows, cols), f32),
        compiler_params=_cp(("parallel",)),
    )(p3)
    return out.reshape(parts.shape[1:])


def _chip_scatter(arrs):
    n = len(arrs)

    def body(*refs):
        ins, outs = refs[:n], refs[n:2 * n]
        send_sems, recv_sems = refs[2 * n:]
        mx, my, mc = _my_pos()
        chips = [(1 - mx, my), (mx, 1 - my), (1 - mx, 1 - my)]
        mine = 2 * mx + my
        sends = []
        for a in range(n):
            for j, (px, py) in enumerate(chips):
                sends.append(pltpu.make_async_remote_copy(
                    src_ref=ins[a].at[2 * px + py], dst_ref=outs[a].at[mine],
                    send_sem=send_sems.at[3 * a + j], recv_sem=recv_sems.at[3 * a + j],
                    device_id=(px, py, mc), device_id_type=MESH))
        for cp in sends:
            cp.start()
        for a in range(n):
            for j, (px, py) in enumerate(chips):
                pltpu.make_async_remote_copy(
                    src_ref=ins[a].at[mine], dst_ref=outs[a].at[2 * px + py],
                    send_sem=send_sems.at[3 * a + j], recv_sem=recv_sems.at[3 * a + j],
                    device_id=(px, py, mc), device_id_type=MESH).wait_recv()
        for cp in sends:
            cp.wait_send()

    outs = pl.pallas_call(
        body, name="chip_scatter", in_specs=[ANY] * n, out_specs=[ANY] * n,
        out_shape=[jax.ShapeDtypeStruct(a.shape, a.dtype) for a in arrs],
        scratch_shapes=[pltpu.SemaphoreType.DMA((3 * n,)), pltpu.SemaphoreType.DMA((3 * n,))],
    )(*arrs)
    mine = 2 * lax.axis_index("x") + lax.axis_index("y")
    return [lax.dynamic_update_index_in_dim(o, lax.dynamic_index_in_dim(a, mine, 0, keepdims=False), mine, 0)
            for o, a in zip(outs, arrs)]


def _rc(src, dst, ss, rs, k, to):
    return pltpu.make_async_remote_copy(src_ref=src, dst_ref=dst, send_sem=ss.at[k], recv_sem=rs.at[k],
                                        device_id=to, device_id_type=MESH)


def _half_rows(rows, c):
    return pl.ds((rows // 2) * c, rows // 2)


def _gather_comm(shards):
    n = len(shards)

    def plan(ins, outs, ss, rs):
        mx, my, mc = _my_pos()
        chips = [(1 - mx, my), (mx, 1 - my), (1 - mx, 1 - my)]
        mine = 2 * mx + my
        sends, lands, fwds, fwd_lands = [], [], [], []
        for a in range(n):
            rows = ins[a].shape[0]
            mr, sr = _half_rows(rows, mc), _half_rows(rows, 1 - mc)
            for j, (px, py) in enumerate(chips):
                k, src_chip = 3 * a + j, 2 * px + py
                sends.append(_rc(ins[a].at[mr], outs[a].at[mine, mr], ss, rs, k, (px, py, mc)))
                lands.append(_rc(ins[a].at[mr], outs[a].at[src_chip, mr], ss, rs, k, (px, py, mc)))
                blk, sblk = outs[a].at[src_chip, mr], outs[a].at[src_chip, sr]
                fwds.append(_rc(blk, blk, ss, rs, 3 * n + k, (mx, my, 1 - mc)))
                fwd_lands.append(_rc(sblk, sblk, ss, rs, 3 * n + k, (mx, my, 1 - mc)))
        return sends, lands, fwds, fwd_lands

    def start(ins, outs, ss, rs):
        for cp in plan(ins, outs, ss, rs)[0]:
            cp.start()

    def finish(ins, outs, ss, rs):
        sends, lands, fwds, fwd_lands = plan(ins, outs, ss, rs)
        for land, fwd in zip(lands, fwds):
            land.wait_recv()
            fwd.start()
        for cp in fwd_lands:
            cp.wait_recv()
        for cp in sends + fwds:
            cp.wait_send()

    return dict(ins=list(shards), nsem=6 * n, start=start, finish=finish,
                out_shapes=[jax.ShapeDtypeStruct((NCHIP,) + a.shape, a.dtype) for a in shards])


def _gather_done(outs, shards):
    mine = 2 * lax.axis_index("x") + lax.axis_index("y")
    return [lax.dynamic_update_index_in_dim(o, a, mine, 0) for o, a in zip(outs, shards)]


def _swap_comm(grads):
    n = len(grads)

    def plan(ins, outs, ss, rs):
        mx, my, mc = _my_pos()
        return [_rc(ins[a].at[j, _half_rows(ins[a].shape[1], 1 - mc)], outs[a].at[j], ss, rs,
                    NCHIP * a + j, (mx, my, 1 - mc)) for a in range(n) for j in range(NCHIP)]

    def start(ins, outs, ss, rs):
        for cp in plan(ins, outs, ss, rs):
            cp.start()

    def finish(ins, outs, ss, rs):
        for cp in plan(ins, outs, ss, rs):
            cp.wait()

    return dict(ins=list(grads), nsem=NCHIP * n, start=start, finish=finish,
                out_shapes=[jax.ShapeDtypeStruct((NCHIP, a.shape[1] // 2) + a.shape[2:], a.dtype)
                            for a in grads])


def _scatter_comm(sums):
    n = len(sums)

    def plan(ins, outs, ss, rs):
        mx, my, mc = _my_pos()
        chips = [(1 - mx, my), (mx, 1 - my), (1 - mx, 1 - my)]
        mine = 2 * mx + my
        sends, lands = [], []
        for a in range(n):
            for j, (px, py) in enumerate(chips):
                sends.append(_rc(ins[a].at[2 * px + py], outs[a].at[mine], ss, rs, 3 * a + j, (px, py, mc)))
                lands.append(_rc(ins[a].at[mine], outs[a].at[2 * px + py], ss, rs, 3 * a + j, (px, py, mc)))
        return sends, lands

    def start(ins, outs, ss, rs):
        for cp in plan(ins, outs, ss, rs)[0]:
            cp.start()

    def finish(ins, outs, ss, rs):
        sends, lands = plan(ins, outs, ss, rs)
        for cp in lands:
            cp.wait_recv()
        for cp in sends:
            cp.wait_send()

    return dict(ins=list(sums), nsem=3 * n, start=start, finish=finish,
                out_shapes=[jax.ShapeDtypeStruct(a.shape, a.dtype) for a in sums])


def _scatter_done(outs, sums):
    mine = 2 * lax.axis_index("x") + lax.axis_index("y")
    return [lax.dynamic_update_index_in_dim(o, lax.dynamic_index_in_dim(a, mine, 0, keepdims=False), mine, 0)
            for o, a in zip(outs, sums)]


def _merge_comm(halves):
    n = len(halves)

    def plan(ins, outs, ss, rs):
        mx, my, mc = _my_pos()
        return [_rc(ins[a], outs[a].at[_half_rows(outs[a].shape[0], mc)], ss, rs, a, (mx, my, 1 - mc))
                for a in range(n)]

    def start(ins, outs, ss, rs):
        for cp in plan(ins, outs, ss, rs):
            cp.start()

    def finish(ins, outs, ss, rs):
        for cp in plan(ins, outs, ss, rs):
            cp.wait()

    return dict(ins=list(halves), nsem=n, start=start, finish=finish,
                out_shapes=[jax.ShapeDtypeStruct((2 * a.shape[0],) + a.shape[1:], a.dtype) for a in halves])


def _merge_done(outs, halves):
    mc = lax.axis_index("c")
    return [lax.dynamic_update_slice_in_dim(o, a, a.shape[0] * mc, 0) for o, a in zip(outs, halves)]


def _comm_call(name, comm):
    ni, no = len(comm["ins"]), len(comm["out_shapes"])

    def body(*refs):
        ins, outs, (ss, rs) = refs[:ni], refs[ni:ni + no], refs[ni + no:]
        comm["start"](ins, outs, ss, rs)
        comm["finish"](ins, outs, ss, rs)

    return pl.pallas_call(
        body, name=name, in_specs=[ANY] * ni, out_specs=[ANY] * no, out_shape=comm["out_shapes"],
        scratch_shapes=[pltpu.SemaphoreType.DMA((comm["nsem"],)), pltpu.SemaphoreType.DMA((comm["nsem"],))],
    )(*comm["ins"])


def _call(body, comm, *, name, grid, in_specs, out_specs, out_shape, scratch_shapes, sem, args):
    if comm is None:
        outs = pl.pallas_call(body, name=name, grid=grid, in_specs=in_specs, out_specs=out_specs,
                              out_shape=out_shape, scratch_shapes=scratch_shapes,
                              compiler_params=_cp(sem))(*args)
        return outs, None
    ni, no = len(in_specs), len(out_specs)
    ci, co = len(comm["ins"]), len(comm["out_shapes"])
    nscr = len(scratch_shapes)

    def carried(*refs):
        ins, cins = refs[:ni], refs[ni:ni + ci]
        outs, couts = refs[ni + ci:ni + ci + no], refs[ni + ci + no:ni + ci + no + co]
        scr = refs[ni + ci + no + co:]
        ss, rs = scr[nscr], scr[nscr + 1]
        pids = [pl.program_id(ax) for ax in range(len(grid))]
        first = functools.reduce(jnp.logical_and, [p == 0 for p in pids])
        last = functools.reduce(jnp.logical_and, [p == g - 1 for p, g in zip(pids, grid)])

        @pl.when(first)
        def _():
            comm["start"](cins, couts, ss, rs)

        body(*ins, *outs, *scr[:nscr])

        @pl.when(last)
        def _():
            comm["finish"](cins, couts, ss, rs)

    outs = pl.pallas_call(
        carried, name=name + "_x", grid=grid, in_specs=list(in_specs) + [ANY] * ci,
        out_specs=list(out_specs) + [ANY] * co, out_shape=list(out_shape) + comm["out_shapes"],
        scratch_shapes=list(scratch_shapes) + [pltpu.SemaphoreType.DMA((comm["nsem"],)),
                                               pltpu.SemaphoreType.DMA((comm["nsem"],))],
        compiler_params=_cp(("arbitrary",) * len(grid)),
    )(*args, *comm["ins"])
    return outs[:no], outs[no:]


def _layer_fwd(x, res, res_gate, wl, mod, comm=None):
    L = x.shape[0]
    bm = _row_block(L, 1024)
    nb = L // bm
    sh1, sc1, g1, sh2, sc2, g2 = [mod[k * D:(k + 1) * D][None, :] for k in range(NMOD)]
    sv = {}
    xin, h = _rmsmod_fwd(x, res, res_gate, wl["norm1_g"], sh1, sc1)
    if xin is None:
        xin = x
    p = _mm("mm_in", h, wl["w_in"], dims=NN, grid=(nb, 4, 1),
            a_block=(bm, D), a_map=lambda i, j, k: (i, 0),
            b_block=(None, D, 512), b_map=lambda i, j, k: (j, 0, 0),
            out_shape=(L, INW), o_block=(bm, 512), o_map=lambda i, j, k: (i, j))
    qkv = _qknorm_fwd(p, wl["gqk"])
    oa, couts = _attn_fwd(qkv, comm)
    os_, s_re, s_im = _s5_fwd2(p, wl["abar_re"], wl["abar_im"], wl["bbr"], wl["bbi"], wl["ccr"], wl["cci"],
                              wl["dsk"], wl["gw"], wl["gb"])
    o = _outnorm_fwd(oa, os_, wl["attn_out_g"], wl["ssm_out_g"])
    proj = _mm("mm_out", o, wl["w_out"], dims=NN, grid=(nb, 2, 1),
               a_block=(bm, D), a_map=lambda i, j, k: (i, 0),
               b_block=(D, 512), b_map=lambda i, j, k: (0, j),
               out_shape=(L, D), o_block=(bm, 512), o_map=lambda i, j, k: (i, j))
    x1, h2 = _rmsmod_fwd(xin, proj, g1, wl["norm2_g"], sh2, sc2)
    up_pre = _mm("mm_up", h2, wl["w_up"], dims=NN, grid=(nb, 4, 1),
                 a_block=(bm, D), a_map=lambda i, j, k: (i, 0),
                 b_block=(None, D, 1408), b_map=lambda i, j, k: (j, 0, 0),
                 out_shape=(L, DUP), o_block=(bm, 1408), o_map=lambda i, j, k: (i, j))
    a = _convglu_fwd(up_pre, wl["conv_w"], wl["conv_b"])
    down = _mm("mm_down", a, wl["w_down"], dims=NN, grid=(nb, 2, 1),
               a_block=(bm, DFF), a_map=lambda i, j, k: (i, 0),
               b_block=(DFF, 512), b_map=lambda i, j, k: (0, j),
               out_shape=(L, D), o_block=(bm, 512), o_map=lambda i, j, k: (i, j))
    sv.update(xin=xin, h=h, p=p, qkv=qkv, oa=oa, os=os_, s_re=s_re, s_im=s_im, o=o, proj=proj,
              x1=x1, h2=h2, up_pre=up_pre, a=a, down=down,
              sc1=sc1, g1=g1, sc2=sc2, g2=g2)
    return x1, down, g2, sv, couts


def _big_grads(g):
    return [g["w_in"], g["w_out"].reshape(NCHIP, D // NCHIP, D), g["w_up"],
            g["w_down"].reshape(NCHIP, DFF // NCHIP, D)]


def _chip_sums(grads, recv):
    mc = lax.axis_index("c")
    mine = [lax.dynamic_slice_in_dim(a, (a.shape[1] // 2) * mc, a.shape[1] // 2, axis=1) for a in grads]
    return [_add2(a, b, bf16) for a, b in zip(mine, recv)]


def _reduced(parts, sums):
    halves = [_sum_parts(p) for p in _scatter_done(parts, sums)]
    return _merge_done(_comm_call("grad_merge", _merge_comm(halves)), halves)


def _layer_bwd(dx_out, wl, sv, prev=None):
    L = dx_out.shape[0]
    bm = _row_block(L, 1024)
    nb = L // bm
    bk = _row_block(L, 512)
    nk = L // bk
    g = {}
    dd, dg2 = _gate_bwd(dx_out, sv["down"], sv["g2"])
    da = _mm("mm_down_dx", dd, wl["w_down"], dims=NT, grid=(nb, 2, 1),
             a_block=(bm, D), a_map=lambda i, j, k: (i, 0),
             b_block=(1408, D), b_map=lambda i, j, k: (j, 0),
             out_shape=(L, DFF), o_block=(bm, 1408), o_map=lambda i, j, k: (i, j))
    g["w_down"] = _mm("mm_down_dw", sv["a"], dd, dims=TN, grid=(1, 2, nk),
                      a_block=(bk, DFF), a_map=lambda i, j, k: (k, 0),
                      b_block=(bk, 512), b_map=lambda i, j, k: (k, j),
                      out_shape=(DFF, D), o_block=(DFF, 512), o_map=lambda i, j, k: (0, j))
    dup, g["conv_w"], g["conv_b"], recv = _convglu_bwd(da, sv["up_pre"], wl["conv_w"], wl["conv_b"],
                                                       _swap_comm(prev) if prev else None)
    sums = _chip_sums(prev, recv) if prev else None
    dh2 = _mm("mm_up_dx", dup, wl["w_up"], dims=NT, grid=(nb, 1, 4),
              a_block=(None, bm, 1408), a_map=lambda i, j, k: (k // 2, i, k % 2),
              b_block=(None, D, 1408), b_map=lambda i, j, k: (k, 0, 0),
              out_shape=(L, D), o_block=(bm, D), o_map=lambda i, j, k: (i, 0))
    g["w_up"] = _mm("mm_up_dw", sv["h2"], dup, dims=TN, grid=(1, 4, nk),
                    a_block=(bk, D), a_map=lambda i, j, k: (k, 0),
                    b_block=(None, bk, 1408), b_map=lambda i, j, k: (j // 2, k, j % 2),
                    out_shape=(NCHIP, D, 1408), o_block=(None, D, 1408), o_map=lambda i, j, k: (j, 0, 0))
    dx1, dsh2, dsc2, g["norm2_g"] = _rmsmod_bwd(dh2, sv["x1"], wl["norm2_g"], sv["sc2"], dx_out)
    dproj, dg1 = _gate_bwd(dx1, sv["proj"], sv["g1"])
    do = _mm("mm_out_dx", dproj, wl["w_out"], dims=NT, grid=(nb, 2, 1),
             a_block=(bm, D), a_map=lambda i, j, k: (i, 0),
             b_block=(512, D), b_map=lambda i, j, k: (j, 0),
             out_shape=(L, D), o_block=(bm, 512), o_map=lambda i, j, k: (i, j))
    g["w_out"] = _mm("mm_out_dw", sv["o"], dproj, dims=TN, grid=(1, 2, nk),
                     a_block=(bk, D), a_map=lambda i, j, k: (k, 0),
                     b_block=(bk, 512), b_map=lambda i, j, k: (k, j),
                     out_shape=(D, D), o_block=(D, 512), o_map=lambda i, j, k: (0, j))
    doa, dos, g["attn_out_g"], g["ssm_out_g"] = _outnorm_bwd(do, sv["oa"], sv["os"],
                                                             wl["attn_out_g"], wl["ssm_out_g"])
    (dqn, dkn, dv), parts = _attn_bwd(sv["qkv"], sv["oa"], doa, _scatter_comm(sums) if prev else None)
    prev_reduced = _reduced(parts, sums) if prev else None
    (du, d_abar_re, d_abar_im, g["ssm_d"], g["glu_b"], d_ccr, d_cci, d_bbre, d_bbim, d_gw) = _s5_bwd2(
        dos, sv["p"], sv["s_re"], sv["s_im"], wl["abar_re"], wl["abar_im"], wl["bbr"], wl["bbi"],
        wl["ccr"], wl["cci"], wl["dsk"], wl["gw"], wl["gb"])
    g["ssm_c_re"] = d_ccr.reshape(NG, NP, NH16).transpose(0, 2, 1)
    g["ssm_c_im"] = -d_cci.reshape(NG, NP, NH16).transpose(0, 2, 1)
    g["glu_w"] = d_gw.reshape(NG, NH16, NH16)
    d_ldt, d_ar, d_ai, d_br, d_bi = _s5_disc_bwd(
        wl["ldt_col"], wl["ar_col"], wl["ai_col"], wl["br_mat"], wl["bi_mat"],
        d_abar_re.reshape(NS, 1), d_abar_im.reshape(NS, 1), d_bbre, d_bbim)
    g["ssm_log_dt"] = d_ldt[:, 0]
    g["ssm_a_re"] = d_ar.reshape(NG, NP)
    g["ssm_a_im"] = d_ai.reshape(NG, NP)
    g["ssm_b_re"] = d_br.reshape(NG, NP, NH16)
    g["ssm_b_im"] = d_bi.reshape(NG, NP, NH16)
    dp, dgqk = _qknorm_bwd(dqn, dkn, dv, du, sv["p"], wl["gqk"])
    g["q_norm_g"] = dgqk[0, :HD]
    g["k_norm_g"] = dgqk[0, HD:2 * HD]
    dh = _mm("mm_in_dx", dp, wl["w_in"], dims=NT, grid=(nb, 1, 4),
             a_block=(bm, 512), a_map=lambda i, j, k: (i, k),
             b_block=(None, D, 512), b_map=lambda i, j, k: (k, 0, 0),
             out_shape=(L, D), o_block=(bm, D), o_map=lambda i, j, k: (i, 0))
    g["w_in"] = _mm("mm_in_dw", sv["h"], dp, dims=TN, grid=(1, 4, nk),
                    a_block=(bk, D), a_map=lambda i, j, k: (k, 0),
                    b_block=(bk, 512), b_map=lambda i, j, k: (k, j),
                    out_shape=(NCHIP, D, 512), o_block=(None, D, 512), o_map=lambda i, j, k: (j, 0, 0))
    dx_in, dsh1, dsc1, g["norm1_g"] = _rmsmod_bwd(dh, sv["xin"], wl["norm1_g"], sv["sc1"], dx1)
    g["dmod"] = jnp.concatenate([dsh1, dsc1, dg1, dsh2, dsc2, dg2], axis=1)[0]
    return dx_in, g, prev_reduced


def _prep_layer(l, conv_w, small):
    wl = {}
    wl["conv_w"] = conv_w[l]
    wl["conv_b"] = small["ffn_conv_b"][l][None, :]
    wl["norm1_g"] = small["norm1_g"][l][None, :]
    wl["norm2_g"] = small["norm2_g"][l][None, :]
    wl["attn_out_g"] = small["attn_out_g"][l][None, :]
    wl["ssm_out_g"] = small["ssm_out_g"][l][None, :]
    wl["gqk"] = jnp.concatenate([jnp.tile(small["q_norm_g"][l], 8), jnp.tile(small["k_norm_g"][l], 8)])[None, :]
    wl["ldt_col"] = jnp.repeat(small["ssm_log_dt"][l], NP)[:, None]
    wl["ar_col"] = small["ssm_a_re"][l].reshape(NS, 1)
    wl["ai_col"] = small["ssm_a_im"][l].reshape(NS, 1)
    wl["br_mat"] = small["ssm_b_re"][l].reshape(NS, NH16)
    wl["bi_mat"] = small["ssm_b_im"][l].reshape(NS, NH16)
    abar_re, abar_im, bb_re, bb_im = _s5_disc_fwd(wl["ldt_col"], wl["ar_col"], wl["ai_col"],
                                                  wl["br_mat"], wl["bi_mat"])
    wl["abar_re"] = abar_re.reshape(1, NS)
    wl["abar_im"] = abar_im.reshape(1, NS)
    wl["bbr"] = _block_diag(bb_re.reshape(NG, NP, NH16).transpose(0, 2, 1)).astype(bf16)
    wl["bbi"] = _block_diag(bb_im.reshape(NG, NP, NH16).transpose(0, 2, 1)).astype(bf16)
    wl["ccr"] = _block_diag(small["ssm_c_re"][l].transpose(0, 2, 1)).astype(bf16)
    wl["cci"] = _block_diag(small["ssm_c_im"][l].transpose(0, 2, 1)).astype(bf16)
    wl["gw"] = _block_diag(small["glu_w"][l]).astype(bf16)
    wl["dsk"] = small["ssm_d"][l].reshape(1, SSMW)
    wl["gb"] = small["glu_b"][l].reshape(1, SSMW)
    return wl


def _local_step(x, target, mods, shards, conv_w, small):
    def set_big(wl, gathered):
        w_in, w_out, w_up, w_down = gathered
        wl.update(w_in=w_in, w_up=w_up, w_out=w_out.reshape(D, D), w_down=w_down.reshape(DFF, D))

    def layer_shards(l):
        return [s[l] for s in shards]

    wls = [_prep_layer(l, conv_w, small) for l in range(DEPTH)]
    first = _gather_comm(layer_shards(0))
    set_big(wls[0], _gather_done(_comm_call("weight_gather", first), first["ins"]))
    saved = []
    cur, res, res_gate = x, None, None
    for l in range(DEPTH):
        nxt = _gather_comm(layer_shards(l + 1)) if l + 1 < DEPTH else None
        cur, res, res_gate, sv, couts = _layer_fwd(cur, res, res_gate, wls[l], mods[l], nxt)
        saved.append(sv)
        if nxt is not None:
            set_big(wls[l + 1], _gather_done(couts, nxt["ins"]))
    loss, dx = _loss_fwd_bwd(cur, res, res_gate, target)
    grads = [None] * DEPTH
    big = [None] * DEPTH
    prev = None
    for l in reversed(range(DEPTH)):
        dx, grads[l], done = _layer_bwd(dx, wls[l], saved[l], prev)
        if prev is not None:
            big[l + 1] = done
        prev = _big_grads(grads[l])
    recv = _comm_call("grad_swap", _swap_comm(prev))
    sums = _chip_sums(prev, recv)
    big[0] = _reduced(_comm_call("grad_scatter", _scatter_comm(sums)), sums)
    return loss, dx, grads, big


BIG = ("w_in", "w_out", "ffn_w_up", "ffn_w_down")
SMALL = ("ada_b", "norm1_g", "q_norm_g", "k_norm_g", "ssm_a_re", "ssm_a_im", "ssm_log_dt",
         "ssm_b_re", "ssm_b_im", "ssm_c_re", "ssm_c_im", "ssm_d", "glu_w", "glu_b",
         "attn_out_g", "ssm_out_g", "norm2_g", "ffn_conv_b")
NAMES = ("ada_w", "ada_b", "norm1_g", "w_in", "q_norm_g", "k_norm_g", "ssm_a_re", "ssm_a_im",
         "ssm_log_dt", "ssm_b_re", "ssm_b_im", "ssm_c_re", "ssm_c_im", "ssm_d", "glu_w", "glu_b",
         "attn_out_g", "ssm_out_g", "w_out", "norm2_g", "ffn_w_up", "ffn_conv_w", "ffn_conv_b",
         "ffn_w_down")
PACK_COLS = 1024


def _pack(arrs):
    flat = jnp.concatenate([a.reshape(-1) for a in arrs])
    rows = -(-flat.shape[0] // PACK_COLS)
    rows = -(-rows // 8) * 8
    flat = jnp.pad(flat, (0, rows * PACK_COLS - flat.shape[0]))
    return flat.reshape(rows, PACK_COLS)


def _unpack(packed, shapes):
    flat = packed.reshape(-1)
    out, off = [], 0
    for s in shapes:
        n = math.prod(s)
        out.append(flat[off:off + n].reshape(s))
        off += n
    return out


def kernel(x, c, ada_w, ada_b, norm1_g, w_in, q_norm_g, k_norm_g, ssm_a_re, ssm_a_im, ssm_log_dt, ssm_b_re, ssm_b_im, ssm_c_re, ssm_c_im, ssm_d, glu_w, glu_b, attn_out_g, ssm_out_g, w_out, norm2_g, ffn_w_up, ffn_conv_w, ffn_conv_b, ffn_w_down, loss_target, m_ada_w, m_ada_b, m_norm1_g, m_w_in, m_q_norm_g, m_k_norm_g, m_ssm_a_re, m_ssm_a_im, m_ssm_log_dt, m_ssm_b_re, m_ssm_b_im, m_ssm_c_re, m_ssm_c_im, m_ssm_d, m_glu_w, m_glu_b, m_attn_out_g, m_ssm_out_g, m_w_out, m_norm2_g, m_ffn_w_up, m_ffn_conv_w, m_ffn_conv_b, m_ffn_w_down, v_ada_w, v_ada_b, v_norm1_g, v_w_in, v_q_norm_g, v_k_norm_g, v_ssm_a_re, v_ssm_a_im, v_ssm_log_dt, v_ssm_b_re, v_ssm_b_im, v_ssm_c_re, v_ssm_c_im, v_ssm_d, v_glu_w, v_glu_b, v_attn_out_g, v_ssm_out_g, v_w_out, v_norm2_g, v_ffn_w_up, v_ffn_conv_w, v_ffn_conv_b, v_ffn_w_down):
    env = dict(locals())
    w = {n: env[n] for n in NAMES}
    m = {n: env["m_" + n] for n in NAMES}
    v = {n: env["v_" + n] for n in NAMES}
    mx, my, mc = _my_pos()
    chip = 2 * mx + my
    dev = 4 * mx + 2 * my + mc
    xl = x[0]
    tl = loss_target[0]

    c_all = _allgather8("gather_c", jnp.pad(c, ((0, 7), (0, 0))))[:, 0, :]
    c16 = jnp.pad(c_all, ((0, 8), (0, 0)))
    nloc = NMOD * D // NCHIP
    ada_b_loc = lax.dynamic_slice(ada_b, (0, chip * nloc), (DEPTH, nloc))[:, None, :]
    mod_loc = _mod_fwd(c16, ada_w, ada_b_loc)
    mod_g = _allgather8("gather_mod", mod_loc.reshape(DEPTH * 16, nloc))
    mod_all = mod_g[0::2].reshape(NCHIP, DEPTH, 16, nloc).transpose(1, 2, 0, 3).reshape(DEPTH, 16, NMOD * D)
    mods = lax.dynamic_index_in_dim(mod_all, dev, axis=1, keepdims=False)

    shards = [_cast_bf16(w[n]) for n in BIG]
    ncw = DUP // NCHIP
    conv_g = _allgather8("gather_conv", jnp.pad(w["ffn_conv_w"].reshape(DEPTH * 3, ncw), ((0, 4), (0, 0))))
    conv_w = conv_g[0::2, :DEPTH * 3].reshape(NCHIP, DEPTH, 3, ncw).transpose(1, 2, 0, 3).reshape(DEPTH, 3, DUP)
    small = {n: w[n] for n in SMALL}

    loss_loc, grad_x, grads, big = _local_step(xl, tl, mods, shards, conv_w, small)
    loss = lax.psum(loss_loc, ("x", "y", "c"))

    outs = {}
    for k, name in enumerate(BIG):
        g_big = jnp.stack([big[l][k] for l in range(DEPTH)])
        outs[name] = _adamw(g_big[None], w[name], m[name], v[name])

    small_names = SMALL[1:] + ("ffn_conv_w_full",)
    small_grads = []
    for n in SMALL[1:]:
        key = {"ffn_conv_b": "conv_b"}.get(n, n)
        small_grads.append(jnp.stack([grads[l][key].reshape(w[n].shape[1:]) for l in range(DEPTH)]))
    small_grads.append(jnp.stack([grads[l]["conv_w"] for l in range(DEPTH)]))
    dmod = jnp.stack([grads[l]["dmod"] for l in range(DEPTH)])
    packed = _pack([dmod] + small_grads)
    allp = _allgather8("gather_small", packed)
    shapes = [dmod.shape] + [a.shape for a in small_grads]
    dmod_all = allp[:, :DEPTH * NMOD, :].reshape(NDEV, DEPTH, NMOD * D)
    dmod_loc = lax.dynamic_slice(dmod_all, (0, 0, chip * nloc), (NDEV, DEPTH, nloc)).transpose(1, 0, 2)
    g_ada = _ada_grad(c16, jnp.pad(dmod_loc, ((0, 0), (0, 8), (0, 0))))
    outs["ada_w"] = _adamw(g_ada[None], w["ada_w"], m["ada_w"], v["ada_w"])
    conv_w_shape = (DEPTH, 3, DUP)
    w_small = _pack([w[n] for n in SMALL] + [jnp.zeros(conv_w_shape, f32)])
    m_small = _pack([m[n] for n in SMALL] + [jnp.zeros(conv_w_shape, f32)])
    v_small = _pack([v[n] for n in SMALL] + [jnp.ones(conv_w_shape, f32)])
    res_small = _adamw(allp, w_small, m_small, v_small)
    unpacked = [_unpack(r, shapes) for r in res_small]
    for i, n in enumerate(SMALL):
        outs[n] = tuple(unpacked[k][i] for k in range(4))
    g_conv_full = unpacked[0][len(SMALL)]
    ncw = DUP // NCHIP
    g_conv = lax.dynamic_slice(g_conv_full, (0, 0, chip * ncw), (DEPTH, 3, ncw))
    outs["ffn_conv_w"] = _adamw(g_conv[None], w["ffn_conv_w"], m["ffn_conv_w"], v["ffn_conv_w"])

    result = [loss, grad_x[None]]
    for k in range(4):
        result += [outs[n][k] for n in NAMES]
    return tuple(result)
```

```python
import functools
import math

import jax
import jax.numpy as jnp
from jax import lax
from jax.experimental import pallas as pl
from jax.experimental.pallas import tpu as pltpu

f32 = jnp.float32
bf16 = jnp.bfloat16
_MXU = jnp.bfloat16

D = 1024
ATT = 512
SSMW = 512
HD = 64
NG = 32
NP = 64
NH16 = 16
NS = NG * NP
INW = 2048
DFF = 2816
DUP = 2 * DFF
NMOD = 6
DEPTH = 4
EPS = 1e-6
NCHIP = 4
NDEV = 8

ADAM_LR = 0.001
ADAM_B1 = 0.9
ADAM_B2 = 0.999
ADAM_EPS = 1e-08
ADAM_WD = 0.01
ADAM_STEP = 10

VMEM_LIMIT = 56 * 1024 * 1024
MESH = pl.DeviceIdType.MESH

NN = (((1,), (0,)), ((), ()))
NT = (((1,), (1,)), ((), ()))
TN = (((0,), (0,)), ((), ()))


def _cp(sem=None):
    if sem is None:
        return pltpu.CompilerParams(vmem_limit_bytes=VMEM_LIMIT)
    return pltpu.CompilerParams(dimension_semantics=sem, vmem_limit_bytes=VMEM_LIMIT)


def _dot(a, b, dims=NN):
    return lax.dot_general(a.astype(_MXU), b.astype(_MXU), dims, preferred_element_type=f32)


def _dot_exact(x, m):
    hi = x.astype(bf16)
    r1 = x - hi.astype(f32)
    mid = r1.astype(bf16)
    lo = (r1 - mid.astype(f32)).astype(bf16)
    d = lambda p: lax.dot_general(p, m, NN, preferred_element_type=f32)
    return d(hi) + d(mid) + d(lo)


def _gelu(x):
    c = math.sqrt(2.0 / math.pi)
    return 0.5 * x * (1.0 + jnp.tanh(c * (x + 0.044715 * (x * x * x))))


def _gelu_grad(x):
    c = math.sqrt(2.0 / math.pi)
    t = jnp.tanh(c * (x + 0.044715 * (x * x * x)))
    return 0.5 * (1.0 + t) + 0.5 * x * (1.0 - t * t) * c * (1.0 + 3.0 * 0.044715 * (x * x))


def _sigmoid(x):
    return 1.0 / (1.0 + jnp.exp(-x))


def _mm(name, a, b, *, dims, grid, a_block, a_map, b_block, b_map, out_shape, o_block, o_map,
        out_dtype=f32):
    nk = grid[2]
    acc_shape = tuple(s for s in o_block if s is not None)

    def body(a_ref, b_ref, o_ref, acc_ref):
        k = pl.program_id(2)
        part = _dot(a_ref[...], b_ref[...], dims)
        if nk == 1:
            o_ref[...] = part.astype(o_ref.dtype)
        else:
            @pl.when(k == 0)
            def _():
                acc_ref[...] = part

            @pl.when(k > 0)
            def _():
                acc_ref[...] += part

            @pl.when(k == nk - 1)
            def _():
                o_ref[...] = acc_ref[...].astype(o_ref.dtype)

    return pl.pallas_call(
        body, name=name, grid=grid,
        in_specs=[pl.BlockSpec(a_block, a_map), pl.BlockSpec(b_block, b_map)],
        out_specs=pl.BlockSpec(o_block, o_map),
        out_shape=jax.ShapeDtypeStruct(out_shape, out_dtype),
        scratch_shapes=[pltpu.VMEM(acc_shape if nk > 1 else (8, 128), f32)],
        compiler_params=_cp(("parallel", "parallel", "arbitrary")),
    )(a, b)


def _row_block(L, want):
    return want if L % want == 0 else L


def _rmsmod_fwd(x, res, gate, g, sh, sc):
    L = x.shape[0]
    bm = _row_block(L, 256)
    with_res = res is not None

    def body(*refs):
        if with_res:
            x_ref, r_ref, gt_ref, g_ref, sh_ref, sc_ref, xo_ref, h_ref = refs
            xin = x_ref[...] + gt_ref[...] * r_ref[...]
            xo_ref[...] = xin
        else:
            x_ref, g_ref, sh_ref, sc_ref, h_ref = refs
            xin = x_ref[...]
        inv = lax.rsqrt(jnp.mean(xin * xin, axis=-1, keepdims=True) + EPS)
        xn = xin * inv * g_ref[...]
        h_ref[...] = (xn * (1.0 + sc_ref[...]) + sh_ref[...]).astype(h_ref.dtype)

    row = pl.BlockSpec((bm, D), lambda i: (i, 0))
    vec = pl.BlockSpec((1, D), lambda i: (0, 0))
    if with_res:
        return pl.pallas_call(
            body, name="rmsmod_res_fwd", grid=(L // bm,),
            in_specs=[row, row, vec, vec, vec, vec], out_specs=[row, row],
            out_shape=[jax.ShapeDtypeStruct((L, D), f32), jax.ShapeDtypeStruct((L, D), bf16)],
            compiler_params=_cp(("parallel",)),
        )(x, res, gate, g, sh, sc)
    h = pl.pallas_call(
        body, name="rmsmod_fwd", grid=(L // bm,),
        in_specs=[row, vec, vec, vec], out_specs=row,
        out_shape=jax.ShapeDtypeStruct((L, D), bf16),
        compiler_params=_cp(("parallel",)),
    )(x, g, sh, sc)
    return None, h


def _rmsmod_bwd(dh, x, g, sc, dres):
    L = x.shape[0]
    bm = _row_block(L, 256)

    def body(dh_ref, x_ref, g_ref, sc_ref, dr_ref, dx_ref, dsh_ref, dsc_ref, dg_ref):
        i = pl.program_id(0)
        xv = x_ref[...]
        dhv = dh_ref[...]
        inv = lax.rsqrt(jnp.mean(xv * xv, axis=-1, keepdims=True) + EPS)
        xh = xv * inv
        gv = g_ref[...]
        xn = xh * gv
        dxn = dhv * (1.0 + sc_ref[...])
        dxh = dxn * gv
        dx_ref[...] = inv * (dxh - xh * jnp.mean(dxh * xh, axis=-1, keepdims=True)) + dr_ref[...]
        p_sh = jnp.sum(dhv, axis=0, keepdims=True)
        p_sc = jnp.sum(dhv * xn, axis=0, keepdims=True)
        p_g = jnp.sum(dxn * xh, axis=0, keepdims=True)

        @pl.when(i == 0)
        def _():
            dsh_ref[...] = p_sh
            dsc_ref[...] = p_sc
            dg_ref[...] = p_g

        @pl.when(i > 0)
        def _():
            dsh_ref[...] += p_sh
            dsc_ref[...] += p_sc
            dg_ref[...] += p_g

    row = pl.BlockSpec((bm, D), lambda i: (i, 0))
    vec = pl.BlockSpec((1, D), lambda i: (0, 0))
    return pl.pallas_call(
        body, name="rmsmod_bwd", grid=(L // bm,),
        in_specs=[row, row, vec, vec, row], out_specs=[row, vec, vec, vec],
        out_shape=[jax.ShapeDtypeStruct((L, D), f32)] + [jax.ShapeDtypeStruct((1, D), f32)] * 3,
        compiler_params=_cp(("arbitrary",)),
    )(dh, x, g, sc, dres)


def _gate_bwd(dx, y, gate):
    L = dx.shape[0]
    bm = _row_block(L, 256)

    def body(dx_ref, y_ref, gt_ref, dy_ref, dgt_ref):
        i = pl.program_id(0)
        dxv = dx_ref[...]
        dy_ref[...] = (gt_ref[...] * dxv).astype(dy_ref.dtype)
        part = jnp.sum(dxv * y_ref[...], axis=0, keepdims=True)

        @pl.when(i == 0)
        def _():
            dgt_ref[...] = part

        @pl.when(i > 0)
        def _():
            dgt_ref[...] += part

    row = pl.BlockSpec((bm, D), lambda i: (i, 0))
    vec = pl.BlockSpec((1, D), lambda i: (0, 0))
    return pl.pallas_call(
        body, name="gate_bwd", grid=(L // bm,),
        in_specs=[row, row, vec], out_specs=[row, vec],
        out_shape=[jax.ShapeDtypeStruct((L, D), bf16), jax.ShapeDtypeStruct((1, D), f32)],
        compiler_params=_cp(("arbitrary",)),
    )(dx, y, gate)


def _loss_fwd_bwd(x1, down, gate, target):
    L = x1.shape[0]
    bm = _row_block(L, 256)
    nsteps = L // bm

    def body(x_ref, d_ref, gt_ref, t_ref, dy_ref, loss_ref, acc_ref):
        i = pl.program_id(0)
        diff = x_ref[...] + gt_ref[...] * d_ref[...] - t_ref[...]
        dy_ref[...] = diff * (1.0 / D)
        part = jnp.sum(diff * diff, axis=0, keepdims=True)

        @pl.when(i == 0)
        def _():
            acc_ref[...] = part

        @pl.when(i > 0)
        def _():
            acc_ref[...] += part

        @pl.when(i == nsteps - 1)
        def _():
            tot = jnp.sum(acc_ref[...], axis=1, keepdims=True) * (0.5 / D)
            loss_ref[...] = jnp.broadcast_to(tot, (8, 128))

    row = pl.BlockSpec((bm, D), lambda i: (i, 0))
    vec = pl.BlockSpec((1, D), lambda i: (0, 0))
    dy, loss = pl.pallas_call(
        body, name="loss_fwd_bwd", grid=(nsteps,),
        in_specs=[row, row, vec, row],
        out_specs=[row, pl.BlockSpec((8, 128), lambda i: (0, 0))],
        out_shape=[jax.ShapeDtypeStruct((L, D), f32), jax.ShapeDtypeStruct((8, 128), f32)],
        scratch_shapes=[pltpu.VMEM((1, D), f32)],
        compiler_params=_cp(("arbitrary",)),
    )(x1, down, gate, target)
    return loss[0, 0], dy


def _head_mean_matrix():
    r = lax.broadcasted_iota(jnp.int32, (128, 128), 0) // HD
    c = lax.broadcasted_iota(jnp.int32, (128, 128), 1) // HD
    return jnp.where(r == c, 1.0 / HD, 0.0).astype(bf16)


def _qknorm_fwd(p, gqk):
    L = p.shape[0]
    bm = _row_block(L, 256)

    def body(p_ref, g_ref, o_ref):
        e = _head_mean_matrix()
        for c in range(8):
            sl = slice(128 * c, 128 * (c + 1))
            xv = p_ref[:, sl]
            inv = lax.rsqrt(_dot_exact(xv * xv, e) + EPS)
            yv = xv * inv * g_ref[:, sl]
            if c < 4:
                yv = yv * (1.0 / math.sqrt(HD))
            o_ref[:, sl] = yv.astype(o_ref.dtype)
        o_ref[:, 1024:1536] = p_ref[:, 1024:1536].astype(o_ref.dtype)

    return pl.pallas_call(
        body, name="qknorm_fwd", grid=(L // bm,),
        in_specs=[pl.BlockSpec((bm, 1536), lambda i: (i, 0)), pl.BlockSpec((1, 1024), lambda i: (0, 0))],
        out_specs=pl.BlockSpec((bm, 1536), lambda i: (i, 0)),
        out_shape=jax.ShapeDtypeStruct((L, 1536), bf16),
        compiler_params=_cp(("parallel",)),
    )(p, gqk)


def _qknorm_bwd(dqn, dkn, dval, du, p, gqk):
    L = p.shape[0]
    bm = _row_block(L, 256)
    nsteps = L // bm

    def body(dq_ref, dk_ref, dv_ref, du_ref, p_ref, g_ref, o_ref, dg_ref, acc_ref):
        i = pl.program_id(0)
        e = _head_mean_matrix()
        o_ref[:, 1024:1536] = dv_ref[...].astype(o_ref.dtype)
        o_ref[:, 1536:2048] = du_ref[...].astype(o_ref.dtype)
        for c in range(8):
            sl = slice(128 * c, 128 * (c + 1))
            xv = p_ref[:, sl]
            d_ref = dq_ref if c < 4 else dk_ref
            dv = d_ref[:, 128 * (c % 4):128 * (c % 4 + 1)]
            gv = g_ref[:, sl]
            inv = lax.rsqrt(_dot_exact(xv * xv, e) + EPS)
            xh = xv * inv
            dxh = dv * gv
            o_ref[:, sl] = (inv * (dxh - xh * _dot_exact(dxh * xh, e))).astype(o_ref.dtype)
            part = jnp.sum(dv * xh, axis=0, keepdims=True)

            @pl.when(i == 0)
            def _():
                acc_ref[:, sl] = part

            @pl.when(i > 0)
            def _():
                acc_ref[:, sl] += part

        @pl.when(i == nsteps - 1)
        def _():
            r = lax.broadcasted_iota(jnp.int32, (1024, 128), 0)
            col = lax.broadcasted_iota(jnp.int32, (1024, 128), 1)
            fold = jnp.where(col == (r // 512) * HD + r % HD, 1.0, 0.0).astype(bf16)
            dg_ref[...] = _dot_exact(jnp.broadcast_to(acc_ref[...], (8, 1024)), fold)

    half = pl.BlockSpec((bm, 512), lambda i: (i, 0))
    return pl.pallas_call(
        body, name="qknorm_bwd", grid=(nsteps,),
        in_specs=[half, half, half, half, pl.BlockSpec((bm, 1024), lambda i: (i, 0)),
                  pl.BlockSpec((1, 1024), lambda i: (0, 0))],
        out_specs=[pl.BlockSpec((bm, INW), lambda i: (i, 0)), pl.BlockSpec((8, 128), lambda i: (0, 0))],
        out_shape=[jax.ShapeDtypeStruct((L, INW), bf16), jax.ShapeDtypeStruct((8, 128), f32)],
        scratch_shapes=[pltpu.VMEM((1, 1024), f32)],
        compiler_params=_cp(("arbitrary",)),
    )(dqn, dkn, dval, du, p, gqk)


def _outnorm_fwd(oa, os_, ga, gs):
    L = oa.shape[0]
    bm = _row_block(L, 256)

    def body(a_ref, s_ref, ga_ref, gs_ref, o_ref):
        for x_ref, g_ref, off in ((a_ref, ga_ref, 0), (s_ref, gs_ref, 512)):
            xv = x_ref[...]
            inv = lax.rsqrt(jnp.mean(xv * xv, axis=-1, keepdims=True) + EPS)
            o_ref[:, off:off + 512] = (xv * inv * g_ref[...]).astype(o_ref.dtype)

    half = pl.BlockSpec((bm, 512), lambda i: (i, 0))
    vec = pl.BlockSpec((1, 512), lambda i: (0, 0))
    return pl.pallas_call(
        body, name="outnorm_fwd", grid=(L // bm,),
        in_specs=[half, half, vec, vec], out_specs=pl.BlockSpec((bm, D), lambda i: (i, 0)),
        out_shape=jax.ShapeDtypeStruct((L, D), bf16),
        compiler_params=_cp(("parallel",)),
    )(oa, os_, ga, gs)


def _outnorm_bwd(do, oa, os_, ga, gs):
    L = oa.shape[0]
    bm = _row_block(L, 256)

    def body(do_ref, a_ref, s_ref, ga_ref, gs_ref, da_ref, ds_ref, dga_ref, dgs_ref):
        i = pl.program_id(0)
        for x_ref, g_ref, dx_ref, dg_ref, off in ((a_ref, ga_ref, da_ref, dga_ref, 0),
                                                  (s_ref, gs_ref, ds_ref, dgs_ref, 512)):
            xv = x_ref[...]
            dv = do_ref[:, off:off + 512]
            inv = lax.rsqrt(jnp.mean(xv * xv, axis=-1, keepdims=True) + EPS)
            xh = xv * inv
            dxh = dv * g_ref[...]
            dx_ref[...] = inv * (dxh - xh * jnp.mean(dxh * xh, axis=-1, keepdims=True))
            part = jnp.sum(dv * xh, axis=0, keepdims=True)

            @pl.when(i == 0)
            def _():
                dg_ref[...] = part

            @pl.when(i > 0)
            def _():
                dg_ref[...] += part

    half = pl.BlockSpec((bm, 512), lambda i: (i, 0))
    vec = pl.BlockSpec((1, 512), lambda i: (0, 0))
    return pl.pallas_call(
        body, name="outnorm_bwd", grid=(L // bm,),
        in_specs=[pl.BlockSpec((bm, D), lambda i: (i, 0)), half, half, vec, vec],
        out_specs=[half, half, vec, vec],
        out_shape=[jax.ShapeDtypeStruct((L, 512), f32)] * 2 + [jax.ShapeDtypeStruct((1, 512), f32)] * 2,
        compiler_params=_cp(("arbitrary",)),
    )(do, oa, os_, ga, gs)


NCB = DFF // 128


def _conv_rows(L):
    return 64 if L % 64 == 0 else 32


def _conv_chunk(x_ref, w_ref, b_ref, i, nch, R, L, ahead):
    base = pl.multiple_of(i * R, R)
    prev = x_ref[pl.ds(pl.multiple_of(jnp.maximum(base - 8, 0), 8), 8), :]
    parts = [jnp.where(i > 0, prev, 0.0), x_ref[pl.ds(base, R), :]]
    if ahead:
        nxt = x_ref[pl.ds(pl.multiple_of(jnp.minimum(base + R, L - 8), 8), 8), :]
        parts.append(jnp.where(i < nch - 1, nxt, 0.0))
    xe = jnp.concatenate(parts, axis=0)
    x1 = pltpu.roll(xe, 1, 0)
    x2 = pltpu.roll(xe, 2, 0)
    u = b_ref[...] + x2 * w_ref[0:1, :]
    u = u + x1 * w_ref[1:2, :]
    u = u + xe * w_ref[2:3, :]
    return u[8:], xe[8:], x1[8:], x2[8:]


def _fold8(x):
    out = x[0:8]
    for q in range(1, x.shape[0] // 8):
        out = out + x[8 * q:8 * (q + 1)]
    return out


def _convglu_fwd(up_pre, cw, cb):
    L = up_pre.shape[0]

    def body(v_ref, g_ref, wv_ref, wg_ref, bv_ref, bg_ref, a_ref):
        row = lax.broadcasted_iota(jnp.int32, (L, 128), 0)

        def conv(x_ref, w_ref, b_ref):
            x = x_ref[...]
            out = b_ref[...] + jnp.where(row >= 2, pltpu.roll(x, 2, 0), 0.0) * w_ref[0:1, :]
            out = out + jnp.where(row >= 1, pltpu.roll(x, 1, 0), 0.0) * w_ref[1:2, :]
            return out + x * w_ref[2:3, :]

        val = conv(v_ref, wv_ref, bv_ref)
        a_ref[...] = (_gelu(conv(g_ref, wg_ref, bg_ref)) * val).astype(a_ref.dtype)

    col = lambda off: pl.BlockSpec((L, 128), lambda j: (0, j + off))
    wsp = lambda off: pl.BlockSpec((3, 128), lambda j: (0, j + off))
    bsp = lambda off: pl.BlockSpec((1, 128), lambda j: (0, j + off))
    return pl.pallas_call(
        body, name="convglu_fwd", grid=(NCB,),
        in_specs=[col(0), col(NCB), wsp(0), wsp(NCB), bsp(0), bsp(NCB)],
        out_specs=pl.BlockSpec((L, 128), lambda j: (0, j)),
        out_shape=jax.ShapeDtypeStruct((L, DFF), bf16),
        compiler_params=_cp(("parallel",)),
    )(up_pre, up_pre, cw, cw, cb, cb)


def _convglu_bwd(da, up_pre, cw, cb, comm=None):
    L = up_pre.shape[0]
    R = _conv_rows(L)
    nch = L // R
    n = R + 8

    def body(da_ref, v_ref, g_ref, wv_ref, wg_ref, bv_ref, bg_ref, dx_ref, dw_ref, db_ref):
        def chunk(i, acc):
            base = pl.multiple_of(i * R, R)
            val, xv, xv1, xv2 = _conv_chunk(v_ref, wv_ref, bv_ref, i, nch, R, L, True)
            gate, xg, xg1, xg2 = _conv_chunk(g_ref, wg_ref, bg_ref, i, nch, R, L, True)
            nxt = da_ref[pl.ds(pl.multiple_of(jnp.minimum(base + R, L - 8), 8), 8), :]
            dav = jnp.concatenate([da_ref[pl.ds(base, R), :], jnp.where(i < nch - 1, nxt, 0.0)], axis=0)
            c = math.sqrt(2.0 / math.pi)
            t = jnp.tanh(c * (gate + 0.044715 * (gate * gate * gate)))
            d_val = dav * (0.5 * gate * (1.0 + t))
            d_gate = dav * val * (0.5 * (1.0 + t)
                                  + 0.5 * gate * (1.0 - t * t) * c * (1.0 + 3.0 * 0.044715 * (gate * gate)))
            new = []
            for k, (d, w_ref, xs) in enumerate(((d_val, wv_ref, (xv2, xv1, xv)), (d_gate, wg_ref, (xg2, xg1, xg)))):
                dx = d * w_ref[2:3, :] + pltpu.roll(d, n - 1, 0) * w_ref[1:2, :]
                dx = dx + pltpu.roll(d, n - 2, 0) * w_ref[0:1, :]
                dx_ref[k, pl.ds(base, R), :] = dx[:R].astype(dx_ref.dtype)
                dr = d[:R]
                new += [_fold8(dr * x[:R]) for x in xs] + [_fold8(dr)]
            return tuple(a + b for a, b in zip(acc, new))

        acc = lax.fori_loop(0, nch, chunk, (jnp.zeros((8, 128), f32),) * 8)
        for k in range(2):
            for r in range(3):
                dw_ref[k, r:r + 1, :] = jnp.sum(acc[4 * k + r], axis=0, keepdims=True)
            db_ref[k] = jnp.sum(acc[4 * k + 3], axis=0, keepdims=True)

    col = lambda off: pl.BlockSpec((L, 128), lambda j: (0, j + off))
    wsp = lambda off: pl.BlockSpec((3, 128), lambda j: (0, j + off))
    bsp = lambda off: pl.BlockSpec((1, 128), lambda j: (0, j + off))
    (dx, dw, db), couts = _call(
        body, comm, name="convglu_bwd", grid=(NCB,),
        in_specs=[col(0), col(0), col(NCB), wsp(0), wsp(NCB), bsp(0), bsp(NCB)],
        out_specs=[pl.BlockSpec((2, L, 128), lambda j: (0, 0, j)), pl.BlockSpec((2, 3, 128), lambda j: (0, 0, j)),
                   pl.BlockSpec((2, 1, 128), lambda j: (0, 0, j))],
        out_shape=[jax.ShapeDtypeStruct((2, L, DFF), bf16), jax.ShapeDtypeStruct((2, 3, DFF), f32),
                   jax.ShapeDtypeStruct((2, 1, DFF), f32)], scratch_shapes=[],
        sem=("parallel",), args=(da, up_pre, up_pre, cw, cw, cb, cb))
    return dx, dw.transpose(1, 0, 2).reshape(3, DUP), db.transpose(1, 0, 2).reshape(1, DUP), couts


def _attn_block(L):
    return 256 if L % 256 == 0 else 128


TAIL_DEAD = -110.0


def _dot_exact2(x, m):
    hi = x.astype(bf16)
    lo = (x - hi.astype(f32)).astype(bf16)
    return (lax.dot_general(hi, m, NN, preferred_element_type=f32)
            + lax.dot_general(lo, m, NN, preferred_element_type=f32))


def _sb_weights(z, mask, tri_gt, carry):
    l1p = jnp.log(1.0 + jnp.exp(-jnp.abs(z)))
    ls_pos = jnp.minimum(z, 0.0) - l1p
    lm = ls_pos - z
    if mask is not None:
        lm = jnp.where(mask, lm, 0.0)
    tail = _dot_exact2(lm, tri_gt) + carry
    w = jnp.exp(ls_pos + tail)
    if mask is not None:
        w = jnp.where(mask, w, 0.0)
    return w, ls_pos, lm


def _head_masks(B):
    lane = lax.broadcasted_iota(jnp.int32, (B, 128), 1)
    return lane < HD, lane >= HD


def _attn_fwd(qkv, comm=None):
    L = qkv.shape[0]
    B = _attn_block(L)
    nq = L // B

    def body(q_ref, k_ref, v_ref, o_ref):
        qi = pl.program_id(1)
        heads = _head_masks(B)
        ti = lax.broadcasted_iota(jnp.int32, (B, B), 0)
        si = lax.broadcasted_iota(jnp.int32, (B, B), 1)
        tri_gt = jnp.where(ti > si, 1.0, 0.0).astype(bf16)
        diag = si < ti
        qv = q_ref[...]
        zero16 = jnp.zeros((), bf16)

        def tile(jb, mask, acc, carries):
            ks = pl.multiple_of(jb * B, B)
            kb = k_ref[pl.ds(ks, B), :]
            vb = v_ref[pl.ds(ks, B), :]
            out = []
            for in_head, carry in zip(heads, carries):
                kh = jnp.where(in_head, kb, zero16)
                vh = jnp.where(in_head, vb, zero16)
                z = lax.dot_general(qv, kh, NT, preferred_element_type=f32)
                w, _, lm = _sb_weights(z, mask, tri_gt, carry)
                acc = acc + lax.dot_general(w.astype(bf16), vh, NN, preferred_element_type=f32)
                out.append(carry + jnp.sum(lm, axis=1, keepdims=True))
            return acc, out

        zc = jnp.zeros((B, 1), f32)
        acc, (c0, c1) = tile(qi, diag, jnp.zeros((B, 128), f32), (zc, zc))

        def cond(st):
            return jnp.logical_and(st[0] <= qi, st[4] > TAIL_DEAD)

        def step(st):
            n, a, p0, p1, _ = st
            a, (p0, p1) = tile(qi - n, None, a, (p0, p1))
            return n + 1, a, p0, p1, jnp.maximum(jnp.max(p0), jnp.max(p1))

        st = lax.while_loop(cond, step, (jnp.int32(1), acc, c0, c1, jnp.maximum(jnp.max(c0), jnp.max(c1))))
        o_ref[...] = st[1]

    (o,), couts = _call(
        body, comm, name="attn_fwd", grid=(4, nq),
        in_specs=[pl.BlockSpec((B, 128), lambda hp, i: (i, hp)),
                  pl.BlockSpec((L, 128), lambda hp, i: (0, 4 + hp)),
                  pl.BlockSpec((L, 128), lambda hp, i: (0, 8 + hp))],
        out_specs=[pl.BlockSpec((B, 128), lambda hp, i: (i, hp))],
        out_shape=[jax.ShapeDtypeStruct((L, ATT), f32)], scratch_shapes=[],
        sem=("parallel", "parallel"), args=(qkv, qkv, qkv))
    return o, couts


def _attn_bwd(qkv, o, do, comm=None):
    L = qkv.shape[0]
    B = _attn_block(L)
    nq = L // B

    def body(q_ref, k_ref, v_ref, o_ref, do_ref, dq_ref, dk_ref, dv_ref):
        qi = pl.program_id(1)

        @pl.when(qi == 0)
        def _():
            dk_ref[...] = jnp.zeros_like(dk_ref)
            dv_ref[...] = jnp.zeros_like(dv_ref)

        heads = _head_masks(B)
        ti = lax.broadcasted_iota(jnp.int32, (B, B), 0)
        si = lax.broadcasted_iota(jnp.int32, (B, B), 1)
        tri_gt = jnp.where(ti > si, 1.0, 0.0).astype(bf16)
        tri_ge = jnp.where(ti >= si, 1.0, 0.0).astype(bf16)
        diag = si < ti
        zero16 = jnp.zeros((), bf16)
        qv = q_ref[...]
        do16 = do_ref[...].astype(bf16)
        dsum_lanes = do16.astype(f32) * o_ref[...]
        qhs = [jnp.where(m, qv, zero16) for m in heads]
        dohs = [jnp.where(m, do16, zero16) for m in heads]
        dsums = [jnp.sum(jnp.where(m, dsum_lanes, 0.0), axis=1, keepdims=True) for m in heads]

        def tile(jb, mask, dq_a, state):
            ks = pl.multiple_of(jb * B, B)
            kb = k_ref[pl.ds(ks, B), :]
            vb = v_ref[pl.ds(ks, B), :]
            dk_blk = jnp.zeros((B, 128), f32)
            dv_blk = jnp.zeros((B, 128), f32)
            out = []
            for in_head, qh, doh, dsum, (carry, suffix) in zip(heads, qhs, dohs, dsums, state):
                kh = jnp.where(in_head, kb, zero16)
                vh = jnp.where(in_head, vb, zero16)
                z = lax.dot_general(qv, kh, NT, preferred_element_type=f32)
                w, ls_pos, lm = _sb_weights(z, mask, tri_gt, carry)
                w16 = w.astype(bf16)
                dw = lax.dot_general(doh, vh, NT, preferred_element_type=f32)
                da = w16.astype(f32) * dw
                sig = jnp.exp(ls_pos)
                suf = _dot_exact2(da, tri_ge) + suffix
                dz = da * (1.0 - sig) - sig * (dsum - suf)
                if mask is not None:
                    dz = jnp.where(mask, dz, 0.0)
                dz16 = dz.astype(bf16)
                dq_a = dq_a + lax.dot_general(dz16, kh, NN, preferred_element_type=f32)
                dk_blk = dk_blk + lax.dot_general(dz16, qh, TN, preferred_element_type=f32)
                dv_blk = dv_blk + lax.dot_general(w16, doh, TN, preferred_element_type=f32)
                out.append((carry + jnp.sum(lm, axis=1, keepdims=True),
                            suffix + jnp.sum(da, axis=1, keepdims=True)))
            dk_ref[pl.ds(ks, B), :] += dk_blk
            dv_ref[pl.ds(ks, B), :] += dv_blk
            return dq_a, out

        def alive(state):
            return jnp.maximum(jnp.max(state[0][0]), jnp.max(state[1][0]))

        zc = jnp.zeros((B, 1), f32)
        dq_acc, state = tile(qi, diag, jnp.zeros((B, 128), f32), ((zc, zc), (zc, zc)))

        def cond(st):
            return jnp.logical_and(st[0] <= qi, st[6] > TAIL_DEAD)

        def step(st):
            n, a, c0, s0, c1, s1, _ = st
            a, new = tile(qi - n, None, a, ((c0, s0), (c1, s1)))
            return n + 1, a, new[0][0], new[0][1], new[1][0], new[1][1], alive(new)

        st = lax.while_loop(cond, step, (jnp.int32(1), dq_acc, state[0][0], state[0][1],
                                         state[1][0], state[1][1], alive(state)))
        dq_ref[...] = st[1] * (1.0 / math.sqrt(HD))

    blk = pl.BlockSpec((B, 128), lambda hp, i: (i, hp))
    return _call(
        body, comm, name="attn_bwd", grid=(4, nq),
        in_specs=[blk,
                  pl.BlockSpec((L, 128), lambda hp, i: (0, 4 + hp)),
                  pl.BlockSpec((L, 128), lambda hp, i: (0, 8 + hp)),
                  blk, blk],
        out_specs=[blk, pl.BlockSpec((L, 128), lambda hp, i: (0, hp)),
                   pl.BlockSpec((L, 128), lambda hp, i: (0, hp))],
        out_shape=[jax.ShapeDtypeStruct((L, ATT), f32)] * 3, scratch_shapes=[],
        sem=("parallel", "arbitrary"), args=(qkv, qkv, qkv, o, do))


def _s5_disc(ldt, ar, ai, br, bi):
    dt = jnp.exp(ldt)
    mag = jnp.exp(dt * ar)
    abar_re = mag * jnp.cos(dt * ai)
    abar_im = mag * jnp.sin(dt * ai)
    em_re = abar_re - 1.0
    em_im = abar_im
    den = ar * ar + ai * ai
    f_re = (em_re * ar + em_im * ai) / den
    f_im = (em_im * ar - em_re * ai) / den
    bb_re = f_re * br - f_im * bi
    bb_im = f_re * bi + f_im * br
    return abar_re, abar_im, bb_re, bb_im


def _s5_disc_fwd(ldt, ar, ai, br, bi):
    def body(ldt_ref, ar_ref, ai_ref, br_ref, bi_ref, o1, o2, o3, o4):
        outs = _s5_disc(ldt_ref[...], ar_ref[...], ai_ref[...], br_ref[...], bi_ref[...])
        for o_ref, v in zip((o1, o2, o3, o4), outs):
            o_ref[...] = v

    col = jax.ShapeDtypeStruct((NS, 1), f32)
    mat = jax.ShapeDtypeStruct((NS, NH16), f32)
    return pl.pallas_call(body, name="s5_disc_fwd", out_shape=[col, col, mat, mat],
                          compiler_params=_cp())(ldt, ar, ai, br, bi)


def _s5_disc_bwd(ldt, ar, ai, br, bi, d_are, d_aim, d_bbre, d_bbim):
    def body(ldt_ref, ar_ref, ai_ref, br_ref, bi_ref, c1, c2, c3, c4, o_ldt, o_ar, o_ai, o_br, o_bi):
        prim = (ldt_ref[...], ar_ref[...], ai_ref[...], br_ref[...], bi_ref[...])
        _, vjp = jax.vjp(_s5_disc, *prim)
        g_ldt, g_ar, g_ai, g_br, g_bi = vjp((c1[...], c2[...], c3[...], c4[...]))
        o_ar[...] = g_ar
        o_ai[...] = g_ai
        o_br[...] = g_br
        o_bi[...] = g_bi
        r = lax.broadcasted_iota(jnp.int32, (NG, NS), 0)
        c = lax.broadcasted_iota(jnp.int32, (NG, NS), 1)
        fold = jnp.where(c // NP == r, 1.0, 0.0).astype(bf16)
        hi = jnp.broadcast_to(g_ldt, (NS, 128))
        p1 = hi.astype(bf16)
        r1 = hi - p1.astype(f32)
        p2 = r1.astype(bf16)
        p3 = (r1 - p2.astype(f32)).astype(bf16)
        dd = lambda p: lax.dot_general(fold, p, NN, preferred_element_type=f32)
        o_ldt[...] = dd(p1) + dd(p2) + dd(p3)

    col = jax.ShapeDtypeStruct((NS, 1), f32)
    mat = jax.ShapeDtypeStruct((NS, NH16), f32)
    return pl.pallas_call(
        body, name="s5_disc_bwd",
        out_shape=[jax.ShapeDtypeStruct((NG, 128), f32), col, col, mat, mat],
        compiler_params=_cp())(ldt, ar, ai, br, bi, d_are, d_aim, d_bbre, d_bbim)


SCAN_LANES = 512


def _cmul(ar, ai, br, bi):
    return ar * br - ai * bi, ar * bi + ai * br


def _pow_tables(ar, ai):
    a2 = _cmul(ar, ai, ar, ai)
    a4 = _cmul(*a2, *a2)
    rows = [(ar, ai)]
    for _ in range(7):
        rows.append(_cmul(*rows[-1], ar, ai))
    return (ar, ai), a2, a4, rows


def _scan_rows(xr_ref, xi_ref, abar_re, abar_im, cr_ref, ci_ref, reverse, on_tile=None):
    T = xr_ref.shape[0]
    nt = T // 8
    row = lax.broadcasted_iota(jnp.int32, (8, SCAN_LANES), 0)
    for c in range(NS // SCAN_LANES):
        ls = slice(c * SCAN_LANES, (c + 1) * SCAN_LANES)
        ar = abar_re[:, ls]
        ai = abar_im[:, ls]
        if reverse:
            ai = -ai
        a1, a2, a4, prow = _pow_tables(ar, ai)
        if reverse:
            prow = prow[::-1]
        pr = jnp.concatenate([p[0] for p in prow], axis=0)
        pi = jnp.concatenate([p[1] for p in prow], axis=0)
        a8r, a8i = prow[0] if reverse else prow[7]
        edge = slice(0, 1) if reverse else slice(7, 8)
        steps = []
        for d, (dr, di) in ((1, a1), (2, a2), (4, a4)):
            keep = (row < 8 - d) if reverse else (row >= d)
            steps.append((8 - d if reverse else d, jnp.where(keep, dr, 0.0), jnp.where(keep, di, 0.0)))

        def local(n, _):
            rs = pl.ds(pl.multiple_of(n * 8, 8), 8)
            xr = xr_ref[rs, ls]
            xi = xi_ref[rs, ls]
            for shift, dr, di in steps:
                mr, mi = _cmul(dr, di, pltpu.roll(xr, shift, 0), pltpu.roll(xi, shift, 0))
                xr = xr + mr
                xi = xi + mi
            xr_ref[rs, ls] = xr
            xi_ref[rs, ls] = xi
            return 0

        lax.fori_loop(0, nt, local, 0, unroll=4)

        def chain(n, carry):
            cr, ci = carry
            it = (nt - 1 - n) if reverse else n
            rs = pl.ds(pl.multiple_of(it * 8, 8), 8)
            xr = xr_ref[rs, ls]
            xi = xi_ref[rs, ls]
            nr, ni = _cmul(a8r, a8i, cr, ci)
            nr = nr + xr[edge, :]
            ni = ni + xi[edge, :]
            mr, mi = _cmul(pr, pi, cr, ci)
            xr = xr + mr
            xi = xi + mi
            xr_ref[rs, ls] = xr
            xi_ref[rs, ls] = xi
            if on_tile is not None:
                nxt_r = jnp.where(row < 7, pltpu.roll(xr, 7, 0), cr)
                nxt_i = jnp.where(row < 7, pltpu.roll(xi, 7, 0), ci)
                on_tile(ls, rs, nxt_r, nxt_i)
            return nr, ni

        cr, ci = lax.fori_loop(0, nt, chain, (cr_ref[:, ls], ci_ref[:, ls]), unroll=2)
        cr_ref[:, ls] = cr
        ci_ref[:, ls] = ci


def _s5_chunk(L):
    return 256 if L % 256 == 0 else L


def _s5_gate(s_re, s_im, u, ccr_ref, cci_ref, dsk_ref, gw_ref, gb_ref):
    y = _dot(s_re, ccr_ref[...]) - _dot(s_im, cci_ref[...]) + dsk_ref[...] * u
    yg = _gelu(y)
    gate = _sigmoid(_dot(yg, gw_ref[...]) + gb_ref[...])
    return y, yg, gate


def _s5_fwd(p, abar_re, abar_im, bbr, bbi, ccr, cci, dsk, gw, gb):
    L = p.shape[0]
    T = _s5_chunk(L)

    def body(u_ref, are_ref, aim_ref, bbr_ref, bbi_ref, ccr_ref, cci_ref, dsk_ref, gw_ref, gb_ref,
             o_ref, sr_ref, si_ref, cr_ref, ci_ref):
        @pl.when(pl.program_id(0) == 0)
        def _():
            cr_ref[...] = jnp.zeros_like(cr_ref)
            ci_ref[...] = jnp.zeros_like(ci_ref)

        u = u_ref[...]
        sr_ref[...] = _dot(u, bbr_ref[...])
        si_ref[...] = _dot(u, bbi_ref[...])
        _scan_rows(sr_ref, si_ref, are_ref[...], aim_ref[...], cr_ref, ci_ref, reverse=False)
        _, yg, gate = _s5_gate(sr_ref[...], si_ref[...], u, ccr_ref, cci_ref, dsk_ref, gw_ref, gb_ref)
        o_ref[...] = yg * gate

    full = lambda shape: pl.BlockSpec(shape, lambda i: (0, 0))
    return pl.pallas_call(
        body, name="s5_fwd", grid=(L // T,),
        in_specs=[pl.BlockSpec((T, SSMW), lambda i: (i, 3)), full((1, NS)), full((1, NS)),
                  full((SSMW, NS)), full((SSMW, NS)), full((NS, SSMW)), full((NS, SSMW)),
                  full((1, SSMW)), full((SSMW, SSMW)), full((1, SSMW))],
        out_specs=[pl.BlockSpec((T, SSMW), lambda i: (i, 0)), pl.BlockSpec((T, NS), lambda i: (i, 0)),
                   pl.BlockSpec((T, NS), lambda i: (i, 0))],
        out_shape=[jax.ShapeDtypeStruct((L, SSMW), f32), jax.ShapeDtypeStruct((L, NS), f32),
                   jax.ShapeDtypeStruct((L, NS), f32)],
        scratch_shapes=[pltpu.VMEM((1, NS), f32), pltpu.VMEM((1, NS), f32)],
        compiler_params=_cp(("arbitrary",)),
    )(p, abar_re, abar_im, bbr, bbi, ccr, cci, dsk, gw, gb)


def _diag_fold(acc, rpg):
    aw = acc.shape[0]
    r = lax.broadcasted_iota(jnp.int32, (aw, 128), 0) // rpg
    c = lax.broadcasted_iota(jnp.int32, (aw, 128), 1) // NH16
    fr = lax.broadcasted_iota(jnp.int32, (128, NH16), 0) % NH16
    fc = lax.broadcasted_iota(jnp.int32, (128, NH16), 1)
    fold = jnp.where(fr == fc, 1.0, 0.0).astype(bf16)
    return _dot_exact(jnp.where(r == c, acc, 0.0), fold)


def _s5_bwd(dout, p, s_re, s_im, abar_re, abar_im, bbr, bbi, ccr, cci, dsk, gw, gb):
    L = p.shape[0]
    T = _s5_chunk(L)
    nchunk = L // T

    def body(do_ref, u_ref, sr_ref, si_ref, are_ref, aim_ref, bbr_ref, bbi_ref, ccr_ref, cci_ref,
             dsk_ref, gw_ref, gb_ref,
             du_ref, dare_ref, daim_ref, dd_ref, dgb_ref, dccr_ref, dcci_ref, dbbr_ref, dbbi_ref, dgw_ref,
             lr_ref, li_ref, cr_ref, ci_ref, accr_ref, acci_ref, wacc_ref, gacc_ref):
        i = pl.program_id(0)

        @pl.when(i == 0)
        def _():
            cr_ref[...] = jnp.zeros_like(cr_ref)
            ci_ref[...] = jnp.zeros_like(ci_ref)
            accr_ref[...] = jnp.zeros_like(accr_ref)
            acci_ref[...] = jnp.zeros_like(acci_ref)
            dd_ref[...] = jnp.zeros_like(dd_ref)
            dgb_ref[...] = jnp.zeros_like(dgb_ref)
            wacc_ref[...] = jnp.zeros_like(wacc_ref)
            gacc_ref[...] = jnp.zeros_like(gacc_ref)

        u = u_ref[...]
        dov = do_ref[...]
        y, yg, gate = _s5_gate(sr_ref[...], si_ref[...], u, ccr_ref, cci_ref, dsk_ref, gw_ref, gb_ref)
        dgl = dov * yg * gate * (1.0 - gate)
        dyg = dov * gate + _dot(dgl, gw_ref[...], NT)
        dy = dyg * _gelu_grad(y)
        u16, dy16, yg16, dgl16 = (t.astype(_MXU) for t in (u, dy, yg, dgl))
        for q in range(4):
            lanes = slice(128 * q, 128 * (q + 1))
            wide_l = slice(512 * q, 512 * (q + 1))
            wacc_ref[0, q] += _dot(sr_ref[:, wide_l], dy16[:, lanes], TN)
            wacc_ref[1, q] += _dot(si_ref[:, wide_l], dy16[:, lanes], TN)
            gacc_ref[q] += _dot(yg16[:, lanes], dgl16[:, lanes], TN)
        dd_ref[...] += jnp.sum(dy * u, axis=0, keepdims=True)
        dgb_ref[...] += jnp.sum(dgl, axis=0, keepdims=True)
        lr_ref[...] = _dot(dy, ccr_ref[...], NT)
        li_ref[...] = -_dot(dy, cci_ref[...], NT)

        def on_tile(ls, rs, nxt_r, nxt_i):
            s_r = sr_ref[rs, ls]
            s_i = si_ref[rs, ls]
            accr_ref[:, ls] += nxt_r * s_r + nxt_i * s_i
            acci_ref[:, ls] += nxt_i * s_r - nxt_r * s_i

        _scan_rows(lr_ref, li_ref, are_ref[...], aim_ref[...], cr_ref, ci_ref, reverse=True,
                   on_tile=on_tile)
        for q in range(4):
            lanes = slice(128 * q, 128 * (q + 1))
            wide_l = slice(512 * q, 512 * (q + 1))
            wacc_ref[2, q] += _dot(lr_ref[:, wide_l], u16[:, lanes], TN)
            wacc_ref[3, q] += _dot(li_ref[:, wide_l], u16[:, lanes], TN)
        du_ref[...] = (dy * dsk_ref[...] + _dot(lr_ref[...], bbr_ref[...], NT)
                       + _dot(li_ref[...], bbi_ref[...], NT))

        @pl.when(i == nchunk - 1)
        def _():
            dare_ref[...] = jnp.sum(accr_ref[...], axis=0, keepdims=True)
            daim_ref[...] = jnp.sum(acci_ref[...], axis=0, keepdims=True)
            for m, o_ref in enumerate((dccr_ref, dcci_ref, dbbr_ref, dbbi_ref)):
                for q in range(4):
                    o_ref[512 * q:512 * (q + 1), :] = _diag_fold(wacc_ref[m, q], NP)
            for q in range(4):
                dgw_ref[128 * q:128 * (q + 1), :] = _diag_fold(gacc_ref[q], NH16)

    rev = lambda i: (nchunk - 1 - i, 0)
    full = lambda shape: pl.BlockSpec(shape, lambda i: (0, 0))
    half = pl.BlockSpec((T, SSMW), rev)
    wide = pl.BlockSpec((T, NS), rev)
    return pl.pallas_call(
        body, name="s5_bwd", grid=(nchunk,),
        in_specs=[half, pl.BlockSpec((T, SSMW), lambda i: (nchunk - 1 - i, 3)), wide, wide,
                  full((1, NS)), full((1, NS)), full((SSMW, NS)), full((SSMW, NS)),
                  full((NS, SSMW)), full((NS, SSMW)), full((1, SSMW)), full((SSMW, SSMW)), full((1, SSMW))],
        out_specs=[half, full((1, NS)), full((1, NS)), full((1, SSMW)), full((1, SSMW)),
                   full((NS, NH16)), full((NS, NH16)), full((NS, NH16)), full((NS, NH16)),
                   full((SSMW, NH16))],
        out_shape=[jax.ShapeDtypeStruct((L, SSMW), f32)] + [jax.ShapeDtypeStruct((1, NS), f32)] * 2
                  + [jax.ShapeDtypeStruct((1, SSMW), f32)] * 2 + [jax.ShapeDtypeStruct((NS, NH16), f32)] * 4
                  + [jax.ShapeDtypeStruct((SSMW, NH16), f32)],
        scratch_shapes=[pltpu.VMEM((T, NS), f32), pltpu.VMEM((T, NS), f32),
                        pltpu.VMEM((1, NS), f32), pltpu.VMEM((1, NS), f32),
                        pltpu.VMEM((8, NS), f32), pltpu.VMEM((8, NS), f32),
                        pltpu.VMEM((4, 4, 512, 128), f32), pltpu.VMEM((4, 128, 128), f32)],
        compiler_params=_cp(("arbitrary",)),
    )(dout, p, s_re, s_im, abar_re, abar_im, bbr, bbi, ccr, cci, dsk, gw, gb)


NTILE = NS // 128
TILE_GROUP = 4


def _seg_entry(lr, li, ar, ai, cr, ci, seg_rows, reverse, row):
    pr, pi = ar, ai
    for _ in range(seg_rows.bit_length() - 1):
        pr, pi = _cmul(pr, pi, pr, pi)
    if reverse:
        xr = jnp.where(row == 7, cr, pltpu.roll(lr, 7, 0))
        xi = jnp.where(row == 7, ci, pltpu.roll(li, 7, 0))
    else:
        xr = jnp.where(row == 0, cr, pltpu.roll(lr, 1, 0))
        xi = jnp.where(row == 0, ci, pltpu.roll(li, 1, 0))
    for d in (1, 2, 4):
        keep = (row < 8 - d) if reverse else (row >= d)
        shift = 8 - d if reverse else d
        mr, mi = _cmul(pr, pi, jnp.where(keep, pltpu.roll(xr, shift, 0), 0.0),
                       jnp.where(keep, pltpu.roll(xi, shift, 0), 0.0))
        xr, xi = xr + mr, xi + mi
        pr, pi = _cmul(pr, pi, pr, pi)
    return xr, xi


def _seg_scan(xr3, xi3, abar_re, abar_im, cr_ref, ci_ref, reverse, on_step=None):
    pitch = xr3.shape[1] // 8
    seg = pitch - 8
    row = lax.broadcasted_iota(jnp.int32, (8, 128), 0)
    for g0 in range(0, NTILE, TILE_GROUP):
        tiles = list(range(g0, g0 + TILE_GROUP))
        lanes = [slice(128 * j, 128 * (j + 1)) for j in tiles]
        ar = [jnp.broadcast_to(abar_re[:, ls], (8, 128)) for ls in lanes]
        ai = [jnp.broadcast_to(-abar_im[:, ls] if reverse else abar_im[:, ls], (8, 128)) for ls in lanes]

        def sweep(store):
            def step(n, st):
                k = (seg - 1 - n) if reverse else n
                rows = pl.ds(k, 8, stride=pitch)
                new = []
                for q, j in enumerate(tiles):
                    sr, si = st[2 * q], st[2 * q + 1]
                    if store and on_step is not None:
                        on_step(j, rows, sr, si)
                    nr = ar[q] * sr - ai[q] * si + xr3.at[j][rows, :]
                    ni = ar[q] * si + ai[q] * sr + xi3.at[j][rows, :]
                    if store:
                        xr3.at[j][rows, :] = nr
                        xi3.at[j][rows, :] = ni
                    new += [nr, ni]
                return tuple(new)
            return step

        zero = jnp.zeros((8, 128), f32)
        ends = lax.fori_loop(0, seg, sweep(False), (zero,) * (2 * TILE_GROUP))
        init = []
        for q, ls in enumerate(lanes):
            init += _seg_entry(ends[2 * q], ends[2 * q + 1], ar[q], ai[q], cr_ref[:, ls], ci_ref[:, ls],
                               seg, reverse, row)
        fin = lax.fori_loop(0, seg, sweep(True), tuple(init))
        edge = slice(0, 1) if reverse else slice(7, 8)
        for q, ls in enumerate(lanes):
            cr_ref[:, ls] = fin[2 * q][edge, :]
            ci_ref[:, ls] = fin[2 * q + 1][edge, :]


def _coarse(i):
    return slice(128 * i, 128 * (i + 1)), slice(512 * i, 512 * (i + 1))


def _seg_pitch(T):
    return T // 8 + 8


def _put_rows(x3, j, val):
    pitch = x3.shape[1] // 8
    seg = pitch - 8
    for s in range(8):
        x3[j, pitch * s:pitch * s + seg, :] = val[seg * s:seg * (s + 1), :]


def _get_rows(x3, j):
    pitch = x3.shape[1] // 8
    seg = pitch - 8
    return jnp.concatenate([x3[j, pitch * s:pitch * s + seg, :] for s in range(8)], axis=0)


def _tiles_cat(x3, i):
    return jnp.concatenate([_get_rows(x3, 4 * i + q) for q in range(4)], axis=1)


def _s5_gate2(sr3, si3, u, ccr_ref, cci_ref, dsk_ref, gw_ref, gb_ref):
    ys = []
    for i in range(4):
        ch, st = _coarse(i)
        ys.append(_dot(_tiles_cat(sr3, i), ccr_ref[st, ch]) - _dot(_tiles_cat(si3, i), cci_ref[st, ch]))
    y = jnp.concatenate(ys, axis=1) + dsk_ref[...] * u
    yg = _gelu(y)
    gl = jnp.concatenate([_dot(yg[:, _coarse(i)[0]], gw_ref[_coarse(i)[0], _coarse(i)[0]]) for i in range(4)],
                         axis=1)
    return y, yg, _sigmoid(gl + gb_ref[...])


def _s5_fwd2(p, abar_re, abar_im, bbr, bbi, ccr, cci, dsk, gw, gb):
    L = p.shape[0]
    T = _s5_chunk(L)

    def body(u_ref, are_ref, aim_ref, bbr_ref, bbi_ref, ccr_ref, cci_ref, dsk_ref, gw_ref, gb_ref,
             o_ref, sr3, si3, cr_ref, ci_ref):
        @pl.when(pl.program_id(0) == 0)
        def _():
            cr_ref[...] = jnp.zeros_like(cr_ref)
            ci_ref[...] = jnp.zeros_like(ci_ref)

        u = u_ref[...]
        for i in range(4):
            ch, st = _coarse(i)
            br = _dot(u[:, ch], bbr_ref[ch, st])
            bi = _dot(u[:, ch], bbi_ref[ch, st])
            for q in range(4):
                _put_rows(sr3, 4 * i + q, br[:, 128 * q:128 * (q + 1)])
                _put_rows(si3, 4 * i + q, bi[:, 128 * q:128 * (q + 1)])
        _seg_scan(sr3, si3, are_ref[...], aim_ref[...], cr_ref, ci_ref, reverse=False)
        _, yg, gate = _s5_gate2(sr3, si3, u, ccr_ref, cci_ref, dsk_ref, gw_ref, gb_ref)
        o_ref[...] = yg * gate

    full = lambda shape: pl.BlockSpec(shape, lambda i: (0, 0))
    rows = 8 * _seg_pitch(T)
    st3 = pl.BlockSpec((NTILE, rows, 128), lambda i: (0, i, 0))
    return pl.pallas_call(
        body, name="s5_fwd", grid=(L // T,),
        in_specs=[pl.BlockSpec((T, SSMW), lambda i: (i, 3)), full((1, NS)), full((1, NS)),
                  full((SSMW, NS)), full((SSMW, NS)), full((NS, SSMW)), full((NS, SSMW)),
                  full((1, SSMW)), full((SSMW, SSMW)), full((1, SSMW))],
        out_specs=[pl.BlockSpec((T, SSMW), lambda i: (i, 0)), st3, st3],
        out_shape=[jax.ShapeDtypeStruct((L, SSMW), f32)]
                  + [jax.ShapeDtypeStruct((NTILE, (L // T) * rows, 128), f32)] * 2,
        scratch_shapes=[pltpu.VMEM((1, NS), f32), pltpu.VMEM((1, NS), f32)],
        compiler_params=_cp(("arbitrary",)),
    )(p, abar_re, abar_im, bbr, bbi, ccr, cci, dsk, gw, gb)


def _s5_bwd2(dout, p, s_re, s_im, abar_re, abar_im, bbr, bbi, ccr, cci, dsk, gw, gb):
    L = p.shape[0]
    T = _s5_chunk(L)
    nchunk = L // T

    def body(do_ref, u_ref, sr3, si3, are_ref, aim_ref, bbr_ref, bbi_ref, ccr_ref, cci_ref,
             dsk_ref, gw_ref, gb_ref,
             du_ref, dare_ref, daim_ref, dd_ref, dgb_ref, dccr_ref, dcci_ref, dbbr_ref, dbbi_ref, dgw_ref,
             lr3, li3, cr_ref, ci_ref, accr3, acci3, wacc_ref, gacc_ref):
        i = pl.program_id(0)

        @pl.when(i == 0)
        def _():
            for ref in (cr_ref, ci_ref, accr3, acci3, dd_ref, dgb_ref, wacc_ref, gacc_ref):
                ref[...] = jnp.zeros_like(ref)

        u = u_ref[...]
        dov = do_ref[...]
        y, yg, gate = _s5_gate2(sr3, si3, u, ccr_ref, cci_ref, dsk_ref, gw_ref, gb_ref)
        dgl = dov * yg * gate * (1.0 - gate)
        dyg = dov * gate + jnp.concatenate(
            [_dot(dgl[:, _coarse(q)[0]], gw_ref[_coarse(q)[0], _coarse(q)[0]], NT) for q in range(4)], axis=1)
        dy = dyg * _gelu_grad(y)
        u16, dy16, yg16, dgl16 = (t.astype(_MXU) for t in (u, dy, yg, dgl))
        dd_ref[...] += jnp.sum(dy * u, axis=0, keepdims=True)
        dgb_ref[...] += jnp.sum(dgl, axis=0, keepdims=True)
        for q in range(4):
            ch, st = _coarse(q)
            wacc_ref[0, q] += _dot(_tiles_cat(sr3, q), dy16[:, ch], TN)
            wacc_ref[1, q] += _dot(_tiles_cat(si3, q), dy16[:, ch], TN)
            gacc_ref[q] += _dot(yg16[:, ch], dgl16[:, ch], TN)
            gr = _dot(dy16[:, ch], ccr_ref[st, ch], NT)
            gi = -_dot(dy16[:, ch], cci_ref[st, ch], NT)
            for t in range(4):
                _put_rows(lr3, 4 * q + t, gr[:, 128 * t:128 * (t + 1)])
                _put_rows(li3, 4 * q + t, gi[:, 128 * t:128 * (t + 1)])

        def on_step(j, rows, nxt_r, nxt_i):
            s_r = sr3.at[j][rows, :]
            s_i = si3.at[j][rows, :]
            accr3[j] += nxt_r * s_r + nxt_i * s_i
            acci3[j] += nxt_i * s_r - nxt_r * s_i

        _seg_scan(lr3, li3, are_ref[...], aim_ref[...], cr_ref, ci_ref, reverse=True, on_step=on_step)
        dus = []
        for q in range(4):
            ch, st = _coarse(q)
            lam_r, lam_i = _tiles_cat(lr3, q), _tiles_cat(li3, q)
            wacc_ref[2, q] += _dot(lam_r, u16[:, ch], TN)
            wacc_ref[3, q] += _dot(lam_i, u16[:, ch], TN)
            dus.append(_dot(lam_r, bbr_ref[ch, st], NT) + _dot(lam_i, bbi_ref[ch, st], NT))
        du_ref[...] = dy * dsk_ref[...] + jnp.concatenate(dus, axis=1)

        @pl.when(i == nchunk - 1)
        def _():
            for j in range(NTILE):
                ls = slice(128 * j, 128 * (j + 1))
                dare_ref[:, ls] = jnp.sum(accr3[j], axis=0, keepdims=True)
                daim_ref[:, ls] = jnp.sum(acci3[j], axis=0, keepdims=True)
            for m, o_ref in enumerate((dccr_ref, dcci_ref, dbbr_ref, dbbi_ref)):
                for q in range(4):
                    o_ref[512 * q:512 * (q + 1), :] = _diag_fold(wacc_ref[m, q], NP)
            for q in range(4):
                dgw_ref[128 * q:128 * (q + 1), :] = _diag_fold(gacc_ref[q], NH16)

    rev = lambda i: (nchunk - 1 - i, 0)
    full = lambda shape: pl.BlockSpec(shape, lambda i: (0, 0))
    half = pl.BlockSpec((T, SSMW), rev)
    rows = 8 * _seg_pitch(T)
    st3 = pl.BlockSpec((NTILE, rows, 128), lambda i: (0, nchunk - 1 - i, 0))
    return pl.pallas_call(
        body, name="s5_bwd", grid=(nchunk,),
        in_specs=[half, pl.BlockSpec((T, SSMW), lambda i: (nchunk - 1 - i, 3)), st3, st3,
                  full((1, NS)), full((1, NS)), full((SSMW, NS)), full((SSMW, NS)),
                  full((NS, SSMW)), full((NS, SSMW)), full((1, SSMW)), full((SSMW, SSMW)), full((1, SSMW))],
        out_specs=[half, full((1, NS)), full((1, NS)), full((1, SSMW)), full((1, SSMW)),
                   full((NS, NH16)), full((NS, NH16)), full((NS, NH16)), full((NS, NH16)),
                   full((SSMW, NH16))],
        out_shape=[jax.ShapeDtypeStruct((L, SSMW), f32)] + [jax.ShapeDtypeStruct((1, NS), f32)] * 2
                  + [jax.ShapeDtypeStruct((1, SSMW), f32)] * 2 + [jax.ShapeDtypeStruct((NS, NH16), f32)] * 4
                  + [jax.ShapeDtypeStruct((SSMW, NH16), f32)],
        scratch_shapes=[pltpu.VMEM((NTILE, rows, 128), f32), pltpu.VMEM((NTILE, rows, 128), f32),
                        pltpu.VMEM((1, NS), f32), pltpu.VMEM((1, NS), f32),
                        pltpu.VMEM((NTILE, 8, 128), f32), pltpu.VMEM((NTILE, 8, 128), f32),
                        pltpu.VMEM((4, 4, 512, 128), f32), pltpu.VMEM((4, 128, 128), f32)],
        compiler_params=_cp(("arbitrary",)),
    )(dout, p, s_re, s_im, abar_re, abar_im, bbr, bbi, ccr, cci, dsk, gw, gb)


def _block_diag(x):
    g, r, c = x.shape
    eye = jnp.eye(g, dtype=x.dtype)
    return (x[:, :, None, :] * eye[:, None, :, None]).reshape(g * r, g * c)


def _mod_fwd(c16, ada_w, ada_b_loc):
    nloc = ada_w.shape[2]

    def body(c_ref, w_ref, b_ref, o_ref):
        cv = c_ref[...]
        act = cv * _sigmoid(cv)
        o_ref[...] = _dot(act, w_ref[...]) + b_ref[...]

    return pl.pallas_call(
        body, name="mod_fwd", grid=(DEPTH,),
        in_specs=[pl.BlockSpec((16, D), lambda l: (0, 0)), pl.BlockSpec((None, D, nloc), lambda l: (l, 0, 0)),
                  pl.BlockSpec((None, 1, nloc), lambda l: (l, 0, 0))],
        out_specs=pl.BlockSpec((None, 16, nloc), lambda l: (l, 0, 0)),
        out_shape=jax.ShapeDtypeStruct((DEPTH, 16, nloc), f32),
        compiler_params=_cp(("parallel",)),
    )(c16, ada_w, ada_b_loc)


def _ada_grad(c16, dmod16):
    nloc = dmod16.shape[2]

    def body(c_ref, d_ref, o_ref):
        cv = c_ref[...]
        act = cv * _sigmoid(cv)
        o_ref[...] = _dot(act, d_ref[...], TN)

    return pl.pallas_call(
        body, name="ada_grad", grid=(DEPTH,),
        in_specs=[pl.BlockSpec((16, D), lambda l: (0, 0)), pl.BlockSpec((None, 16, nloc), lambda l: (l, 0, 0))],
        out_specs=pl.BlockSpec((None, D, nloc), lambda l: (l, 0, 0)),
        out_shape=jax.ShapeDtypeStruct((DEPTH, D, nloc), f32),
        compiler_params=_cp(("parallel",)),
    )(c16, dmod16)


def _as2d(a):
    return a.reshape(-1, a.shape[-1])


def _ew_rows(rows):
    for cand in (512, 256, 128, 64, 32, 16, 8):
        if rows % cand == 0:
            return cand
    return rows


def _cast_bf16(w):
    w2 = _as2d(w)
    rows, cols = w2.shape
    bm = _ew_rows(rows)

    def body(x_ref, o_ref):
        o_ref[...] = x_ref[...].astype(bf16)

    spec = pl.BlockSpec((bm, cols), lambda i: (i, 0))
    out = pl.pallas_call(body, name="cast_bf16", grid=(rows // bm,), in_specs=[spec], out_specs=spec,
                         out_shape=jax.ShapeDtypeStruct((rows, cols), bf16),
                         compiler_params=_cp(("parallel",)))(w2)
    return out.reshape(w.shape)


def _add2(a, b, out_dtype=f32):
    a2, b2 = _as2d(a), _as2d(b)
    rows, cols = a2.shape
    bm = _ew_rows(rows)

    def body(a_ref, b_ref, o_ref):
        o_ref[...] = (a_ref[...] + b_ref[...]).astype(o_ref.dtype)

    spec = pl.BlockSpec((bm, cols), lambda i: (i, 0))
    out = pl.pallas_call(body, name="add2", grid=(rows // bm,), in_specs=[spec, spec], out_specs=spec,
                         out_shape=jax.ShapeDtypeStruct((rows, cols), out_dtype),
                         compiler_params=_cp(("parallel",)))(a2, b2)
    return out.reshape(a.shape)


def _adamw(parts, w, m, v):
    n = parts.shape[0]
    w2, m2, v2 = _as2d(w), _as2d(m), _as2d(v)
    rows, cols = w2.shape
    p3 = parts.reshape(n, rows, cols)
    bm = _ew_rows(rows)
    if bm * cols * 4 > (1 << 20) and bm % 16 == 0:
        bm //= 2

    def body(p_ref, w_ref, m_ref, v_ref, g_ref, d_ref, nm_ref, nv_ref):
        g = p_ref[0]
        for k in range(1, n):
            g = g + p_ref[k]
        mn = ADAM_B1 * m_ref[...] + (1.0 - ADAM_B1) * g
        vn = ADAM_B2 * v_ref[...] + (1.0 - ADAM_B2) * (g * g)
        m_hat = mn / (1.0 - ADAM_B1 ** ADAM_STEP)
        v_hat = vn / (1.0 - ADAM_B2 ** ADAM_STEP)
        g_ref[...] = g
        d_ref[...] = -ADAM_LR * (m_hat / (jnp.sqrt(v_hat) + ADAM_EPS) + ADAM_WD * w_ref[...])
        nm_ref[...] = mn
        nv_ref[...] = vn

    spec = pl.BlockSpec((bm, cols), lambda i: (i, 0))
    outs = pl.pallas_call(
        body, name="adamw", grid=(rows // bm,),
        in_specs=[pl.BlockSpec((n, bm, cols), lambda i: (0, i, 0)), spec, spec, spec],
        out_specs=[spec] * 4, out_shape=[jax.ShapeDtypeStruct((rows, cols), f32)] * 4,
        compiler_params=_cp(("parallel",)),
    )(p3, w2, m2, v2)
    return tuple(o.reshape(w.shape) for o in outs)


ANY = pl.BlockSpec(memory_space=pl.ANY)


def _my_pos():
    return lax.axis_index("x"), lax.axis_index("y"), lax.axis_index("c")


def _allgather8(name, x):
    def body(x_ref, out_ref, send_sems, recv_sems):
        mx, my, mc = _my_pos()
        me, sibling = (mx, my, mc), (mx, my, 1 - mc)
        chips = [(1 - mx, my), (mx, 1 - my), (1 - mx, 1 - my)]

        def slot(px, py, pc):
            return out_ref.at[4 * px + 2 * py + pc]

        def copy(k, block, to, src=None):
            return pltpu.make_async_remote_copy(
                src_ref=slot(*block) if src is None else src, dst_ref=slot(*block),
                send_sem=send_sems.at[k], recv_sem=recv_sems.at[k], device_id=to, device_id_type=MESH)

        first = [copy(0, me, sibling, src=x_ref)]
        first += [copy(1 + j, me, (*chip, mc), src=x_ref) for j, chip in enumerate(chips)]
        for cp in first:
            cp.start()
        passed = [copy(4 + j, (*chip, mc), sibling) for j, chip in enumerate(chips)]
        for j, chip in enumerate(chips):
            copy(1 + j, (*chip, mc), me).wait_recv()
            passed[j].start()
        copy(0, sibling, me).wait_recv()
        for j, chip in enumerate(chips):
            copy(4 + j, (*chip, 1 - mc), me).wait_recv()
        for cp in first + passed:
            cp.wait_send()

    out = pl.pallas_call(
        body, name=name, in_specs=[ANY], out_specs=ANY,
        out_shape=jax.ShapeDtypeStruct((NDEV,) + x.shape, x.dtype),
        scratch_shapes=[pltpu.SemaphoreType.DMA((7,)), pltpu.SemaphoreType.DMA((7,))],
    )(x)
    dev = 4 * lax.axis_index("x") + 2 * lax.axis_index("y") + lax.axis_index("c")
    return lax.dynamic_update_index_in_dim(out, x, dev, 0)


HALF = DEPTH // 2


def _chip_gather(arrs):
    n = len(arrs)

    def body(*refs):
        ins, outs = refs[:n], refs[n:2 * n]
        send_sems, recv_sems = refs[2 * n:]
        mx, my, mc = _my_pos()
        chips = [(1 - mx, my), (mx, 1 - my), (1 - mx, 1 - my)]
        mine = 2 * mx + my
        my_half = pl.ds(HALF * mc, HALF)
        sib_half = pl.ds(HALF * (1 - mc), HALF)

        def ici(a, j, src_chip, half):
            px, py = chips[j]
            return pltpu.make_async_remote_copy(
                src_ref=ins[a].at[half], dst_ref=outs[a].at[src_chip, half],
                send_sem=send_sems.at[3 * a + j], recv_sem=recv_sems.at[3 * a + j],
                device_id=(px, py, mc), device_id_type=MESH)

        def d2d(a, j, half):
            px, py = chips[j]
            blk = outs[a].at[2 * px + py, half]
            return pltpu.make_async_remote_copy(
                src_ref=blk, dst_ref=blk,
                send_sem=send_sems.at[3 * n + 3 * a + j], recv_sem=recv_sems.at[3 * n + 3 * a + j],
                device_id=(mx, my, 1 - mc), device_id_type=MESH)

        sends = [ici(a, j, mine, my_half) for a in range(n) for j in range(3)]
        for cp in sends:
            cp.start()
        passed = []
        for a in range(n):
            for j, (px, py) in enumerate(chips):
                ici(a, j, 2 * px + py, my_half).wait_recv()
                fwd = d2d(a, j, my_half)
                fwd.start()
                passed.append(fwd)
        for a in range(n):
            for j in range(3):
                d2d(a, j, sib_half).wait_recv()
        for cp in sends + passed:
            cp.wait_send()

    outs = pl.pallas_call(
        body, name="chip_gather", in_specs=[ANY] * n, out_specs=[ANY] * n,
        out_shape=[jax.ShapeDtypeStruct((NCHIP,) + a.shape, a.dtype) for a in arrs],
        scratch_shapes=[pltpu.SemaphoreType.DMA((6 * n,)), pltpu.SemaphoreType.DMA((6 * n,))],
    )(*arrs)
    mine = 2 * lax.axis_index("x") + lax.axis_index("y")
    return [lax.dynamic_update_index_in_dim(o, a, mine, 0) for o, a in zip(outs, arrs)]


def _sibling_half_swap(arrs):
    n = len(arrs)

    def body(*refs):
        ins, outs = refs[:n], refs[n:2 * n]
        send_sems, recv_sems = refs[2 * n:]
        mx, my, mc = _my_pos()
        sib_half = pl.ds(HALF * (1 - mc), HALF)
        cps = [pltpu.make_async_remote_copy(
            src_ref=ins[a].at[j, sib_half], dst_ref=outs[a].at[j],
            send_sem=send_sems.at[NCHIP * a + j], recv_sem=recv_sems.at[NCHIP * a + j],
            device_id=(mx, my, 1 - mc), device_id_type=MESH) for a in range(n) for j in range(NCHIP)]
        for cp in cps:
            cp.start()
        for cp in cps:
            cp.wait()

    return pl.pallas_call(
        body, name="sibling_half_swap", in_specs=[ANY] * n, out_specs=[ANY] * n,
        out_shape=[jax.ShapeDtypeStruct((NCHIP, HALF) + a.shape[2:], a.dtype) for a in arrs],
        scratch_shapes=[pltpu.SemaphoreType.DMA((NCHIP * n,)), pltpu.SemaphoreType.DMA((NCHIP * n,))],
    )(*arrs)


def _sibling_merge_halves(arrs):
    n = len(arrs)

    def body(*refs):
        ins, outs = refs[:n], refs[n:2 * n]
        send_sems, recv_sems = refs[2 * n:]
        mx, my, mc = _my_pos()
        my_half = pl.ds(HALF * mc, HALF)
        cps = [pltpu.make_async_remote_copy(
            src_ref=ins[a], dst_ref=outs[a].at[my_half],
            send_sem=send_sems.at[a], recv_sem=recv_sems.at[a],
            device_id=(mx, my, 1 - mc), device_id_type=MESH) for a in range(n)]
        for cp in cps:
            cp.start()
        for cp in cps:
            cp.wait()

    outs = pl.pallas_call(
        body, name="sibling_merge_halves", in_specs=[ANY] * n, out_specs=[ANY] * n,
        out_shape=[jax.ShapeDtypeStruct((DEPTH,) + a.shape[1:], a.dtype) for a in arrs],
        scratch_shapes=[pltpu.SemaphoreType.DMA((n,)), pltpu.SemaphoreType.DMA((n,))],
    )(*arrs)
    start = HALF * lax.axis_index("c")
    return [lax.dynamic_update_slice_in_dim(o, a, start, 0) for o, a in zip(outs, arrs)]


def _sum_parts(parts):
    n = parts.shape[0]
    p3 = parts.reshape(n, -1, parts.shape[-1])
    rows, cols = p3.shape[1:]
    bm = _ew_rows(rows)
    if bm * cols * 4 > (1 << 20) and bm % 16 == 0:
        bm //= 2

    def body(p_ref, o_ref):
        g = p_ref[0].astype(f32)
        for k in range(1, n):
            g = g + p_ref[k].astype(f32)
        o_ref[...] = g

    out = pl.pallas_call(
        body, name="sum_parts", grid=(rows // bm,),
        in_specs=[pl.BlockSpec((n, bm, cols), lambda i: (0, i, 0))],
        out_specs=pl.BlockSpec((bm, cols), lambda i: (i, 0)),
        out_shape=jax.ShapeDtypeStruct((rows, cols), f32),
        compiler_params=_cp(("parallel",)),
    )(p3)
    return out.reshape(parts.shape[1:])


def _chip_scatter(arrs):
    n = len(arrs)

    def body(*refs):
        ins, outs = refs[:n], refs[n:2 * n]
        send_sems, recv_sems = refs[2 * n:]
        mx, my, mc = _my_pos()
        chips = [(1 - mx, my), (mx, 1 - my), (1 - mx, 1 - my)]
        mine = 2 * mx + my
        sends = []
        for a in range(n):
            for j, (px, py) in enumerate(chips):
                sends.append(pltpu.make_async_remote_copy(
                    src_ref=ins[a].at[2 * px + py], dst_ref=outs[a].at[mine],
                    send_sem=send_sems.at[3 * a + j], recv_sem=recv_sems.at[3 * a + j],
                    device_id=(px, py, mc), device_id_type=MESH))
        for cp in sends:
            cp.start()
        for a in range(n):
            for j, (px, py) in enumerate(chips):
                pltpu.make_async_remote_copy(
                    src_ref=ins[a].at[mine], dst_ref=outs[a].at[2 * px + py],
                    send_sem=send_sems.at[3 * a + j], recv_sem=recv_sems.at[3 * a + j],
                    device_id=(px, py, mc), device_id_type=MESH).wait_recv()
        for cp in sends:
            cp.wait_send()

    outs = pl.pallas_call(
        body, name="chip_scatter", in_specs=[ANY] * n, out_specs=[ANY] * n,
        out_shape=[jax.ShapeDtypeStruct(a.shape, a.dtype) for a in arrs],
        scratch_shapes=[pltpu.SemaphoreType.DMA((3 * n,)), pltpu.SemaphoreType.DMA((3 * n,))],
    )(*arrs)
    mine = 2 * lax.axis_index("x") + lax.axis_index("y")
    return [lax.dynamic_update_index_in_dim(o, lax.dynamic_index_in_dim(a, mine, 0, keepdims=False), mine, 0)
            for o, a in zip(outs, arrs)]


def _rc(src, dst, ss, rs, k, to):
    return pltpu.make_async_remote_copy(src_ref=src, dst_ref=dst, send_sem=ss.at[k], recv_sem=rs.at[k],
                                        device_id=to, device_id_type=MESH)


def _half_rows(rows, c):
    return pl.ds((rows // 2) * c, rows // 2)


def _gather_comm(shards):
    n = len(shards)

    def plan(ins, outs, ss, rs):
        mx, my, mc = _my_pos()
        chips = [(1 - mx, my), (mx, 1 - my), (1 - mx, 1 - my)]
        mine = 2 * mx + my
        sends, lands, fwds, fwd_lands = [], [], [], []
        for a in range(n):
            rows = ins[a].shape[0]
            mr, sr = _half_rows(rows, mc), _half_rows(rows, 1 - mc)
            for j, (px, py) in enumerate(chips):
                k, src_chip = 3 * a + j, 2 * px + py
                sends.append(_rc(ins[a].at[mr], outs[a].at[mine, mr], ss, rs, k, (px, py, mc)))
                lands.append(_rc(ins[a].at[mr], outs[a].at[src_chip, mr], ss, rs, k, (px, py, mc)))
                blk, sblk = outs[a].at[src_chip, mr], outs[a].at[src_chip, sr]
                fwds.append(_rc(blk, blk, ss, rs, 3 * n + k, (mx, my, 1 - mc)))
                fwd_lands.append(_rc(sblk, sblk, ss, rs, 3 * n + k, (mx, my, 1 - mc)))
        return sends, lands, fwds, fwd_lands

    def start(ins, outs, ss, rs):
        for cp in plan(ins, outs, ss, rs)[0]:
            cp.start()

    def finish(ins, outs, ss, rs):
        sends, lands, fwds, fwd_lands = plan(ins, outs, ss, rs)
        for land, fwd in zip(lands, fwds):
            land.wait_recv()
            fwd.start()
        for cp in fwd_lands:
            cp.wait_recv()
        for cp in sends + fwds:
            cp.wait_send()

    return dict(ins=list(shards), nsem=6 * n, start=start, finish=finish,
                out_shapes=[jax.ShapeDtypeStruct((NCHIP,) + a.shape, a.dtype) for a in shards])


def _gather_done(outs, shards):
    mine = 2 * lax.axis_index("x") + lax.axis_index("y")
    return [lax.dynamic_update_index_in_dim(o, a, mine, 0) for o, a in zip(outs, shards)]


def _swap_comm(grads):
    n = len(grads)

    def plan(ins, outs, ss, rs):
        mx, my, mc = _my_pos()
        return [_rc(ins[a].at[j, _half_rows(ins[a].shape[1], 1 - mc)], outs[a].at[j], ss, rs,
                    NCHIP * a + j, (mx, my, 1 - mc)) for a in range(n) for j in range(NCHIP)]

    def start(ins, outs, ss, rs):
        for cp in plan(ins, outs, ss, rs):
            cp.start()

    def finish(ins, outs, ss, rs):
        for cp in plan(ins, outs, ss, rs):
            cp.wait()

    return dict(ins=list(grads), nsem=NCHIP * n, start=start, finish=finish,
                out_shapes=[jax.ShapeDtypeStruct((NCHIP, a.shape[1] // 2) + a.shape[2:], a.dtype)
                            for a in grads])


def _scatter_comm(sums):
    n = len(sums)

    def plan(ins, outs, ss, rs):
        mx, my, mc = _my_pos()
        chips = [(1 - mx, my), (mx, 1 - my), (1 - mx, 1 - my)]
        mine = 2 * mx + my
        sends, lands = [], []
        for a in range(n):
            for j, (px, py) in enumerate(chips):
                sends.append(_rc(ins[a].at[2 * px + py], outs[a].at[mine], ss, rs, 3 * a + j, (px, py, mc)))
                lands.append(_rc(ins[a].at[mine], outs[a].at[2 * px + py], ss, rs, 3 * a + j, (px, py, mc)))
        return sends, lands

    def start(ins, outs, ss, rs):
        for cp in plan(ins, outs, ss, rs)[0]:
            cp.start()

    def finish(ins, outs, ss, rs):
        sends, lands = plan(ins, outs, ss, rs)
        for cp in lands:
            cp.wait_recv()
        for cp in sends:
            cp.wait_send()

    return dict(ins=list(sums), nsem=3 * n, start=start, finish=finish,
                out_shapes=[jax.ShapeDtypeStruct(a.shape, a.dtype) for a in sums])


def _scatter_done(outs, sums):
    mine = 2 * lax.axis_index("x") + lax.axis_index("y")
    return [lax.dynamic_update_index_in_dim(o, lax.dynamic_index_in_dim(a, mine, 0, keepdims=False), mine, 0)
            for o, a in zip(outs, sums)]


def _merge_comm(halves):
    n = len(halves)

    def plan(ins, outs, ss, rs):
        mx, my, mc = _my_pos()
        return [_rc(ins[a], outs[a].at[_half_rows(outs[a].shape[0], mc)], ss, rs, a, (mx, my, 1 - mc))
                for a in range(n)]

    def start(ins, outs, ss, rs):
        for cp in plan(ins, outs, ss, rs):
            cp.start()

    def finish(ins, outs, ss, rs):
        for cp in plan(ins, outs, ss, rs):
            cp.wait()

    return dict(ins=list(halves), nsem=n, start=start, finish=finish,
                out_shapes=[jax.ShapeDtypeStruct((2 * a.shape[0],) + a.shape[1:], a.dtype) for a in halves])


def _merge_done(outs, halves):
    mc = lax.axis_index("c")
    return [lax.dynamic_update_slice_in_dim(o, a, a.shape[0] * mc, 0) for o, a in zip(outs, halves)]


def _comm_call(name, comm):
    ni, no = len(comm["ins"]), len(comm["out_shapes"])

    def body(*refs):
        ins, outs, (ss, rs) = refs[:ni], refs[ni:ni + no], refs[ni + no:]
        comm["start"](ins, outs, ss, rs)
        comm["finish"](ins, outs, ss, rs)

    return pl.pallas_call(
        body, name=name, in_specs=[ANY] * ni, out_specs=[ANY] * no, out_shape=comm["out_shapes"],
        scratch_shapes=[pltpu.SemaphoreType.DMA((comm["nsem"],)), pltpu.SemaphoreType.DMA((comm["nsem"],))],
    )(*comm["ins"])


def _call(body, comm, *, name, grid, in_specs, out_specs, out_shape, scratch_shapes, sem, args):
    if comm is None:
        outs = pl.pallas_call(body, name=name, grid=grid, in_specs=in_specs, out_specs=out_specs,
                              out_shape=out_shape, scratch_shapes=scratch_shapes,
                              compiler_params=_cp(sem))(*args)
        return outs, None
    ni, no = len(in_specs), len(out_specs)
    ci, co = len(comm["ins"]), len(comm["out_shapes"])
    nscr = len(scratch_shapes)

    def carried(*refs):
        ins, cins = refs[:ni], refs[ni:ni + ci]
        outs, couts = refs[ni + ci:ni + ci + no], refs[ni + ci + no:ni + ci + no + co]
        scr = refs[ni + ci + no + co:]
        ss, rs = scr[nscr], scr[nscr + 1]
        pids = [pl.program_id(ax) for ax in range(len(grid))]
        first = functools.reduce(jnp.logical_and, [p == 0 for p in pids])
        last = functools.reduce(jnp.logical_and, [p == g - 1 for p, g in zip(pids, grid)])

        @pl.when(first)
        def _():
            comm["start"](cins, couts, ss, rs)

        body(*ins, *outs, *scr[:nscr])

        @pl.when(last)
        def _():
            comm["finish"](cins, couts, ss, rs)

    outs = pl.pallas_call(
        carried, name=name + "_x", grid=grid, in_specs=list(in_specs) + [ANY] * ci,
        out_specs=list(out_specs) + [ANY] * co, out_shape=list(out_shape) + comm["out_shapes"],
        scratch_shapes=list(scratch_shapes) + [pltpu.SemaphoreType.DMA((comm["nsem"],)),
                                               pltpu.SemaphoreType.DMA((comm["nsem"],))],
        compiler_params=_cp(("arbitrary",) * len(grid)),
    )(*args, *comm["ins"])
    return outs[:no], outs[no:]


def _layer_fwd(x, res, res_gate, wl, mod, comm=None):
    L = x.shape[0]
    bm = _row_block(L, 1024)
    nb = L // bm
    sh1, sc1, g1, sh2, sc2, g2 = [mod[k * D:(k + 1) * D][None, :] for k in range(NMOD)]
    sv = {}
    xin, h = _rmsmod_fwd(x, res, res_gate, wl["norm1_g"], sh1, sc1)
    if xin is None:
        xin = x
    p = _mm("mm_in", h, wl["w_in"], dims=NN, grid=(nb, 4, 1),
            a_block=(bm, D), a_map=lambda i, j, k: (i, 0),
            b_block=(None, D, 512), b_map=lambda i, j, k: (j, 0, 0),
            out_shape=(L, INW), o_block=(bm, 512), o_map=lambda i, j, k: (i, j))
    qkv = _qknorm_fwd(p, wl["gqk"])
    oa, couts = _attn_fwd(qkv, comm)
    os_, s_re, s_im = _s5_fwd2(p, wl["abar_re"], wl["abar_im"], wl["bbr"], wl["bbi"], wl["ccr"], wl["cci"],
                              wl["dsk"], wl["gw"], wl["gb"])
    o = _outnorm_fwd(oa, os_, wl["attn_out_g"], wl["ssm_out_g"])
    proj = _mm("mm_out", o, wl["w_out"], dims=NN, grid=(nb, 2, 1),
               a_block=(bm, D), a_map=lambda i, j, k: (i, 0),
               b_block=(D, 512), b_map=lambda i, j, k: (0, j),
               out_shape=(L, D), o_block=(bm, 512), o_map=lambda i, j, k: (i, j))
    x1, h2 = _rmsmod_fwd(xin, proj, g1, wl["norm2_g"], sh2, sc2)
    up_pre = _mm("mm_up", h2, wl["w_up"], dims=NN, grid=(nb, 4, 1),
                 a_block=(bm, D), a_map=lambda i, j, k: (i, 0),
                 b_block=(None, D, 1408), b_map=lambda i, j, k: (j, 0, 0),
                 out_shape=(L, DUP), o_block=(bm, 1408), o_map=lambda i, j, k: (i, j))
    a = _convglu_fwd(up_pre, wl["conv_w"], wl["conv_b"])
    down = _mm("mm_down", a, wl["w_down"], dims=NN, grid=(nb, 2, 1),
               a_block=(bm, DFF), a_map=lambda i, j, k: (i, 0),
               b_block=(DFF, 512), b_map=lambda i, j, k: (0, j),
               out_shape=(L, D), o_block=(bm, 512), o_map=lambda i, j, k: (i, j))
    sv.update(xin=xin, h=h, p=p, qkv=qkv, oa=oa, os=os_, s_re=s_re, s_im=s_im, o=o, proj=proj,
              x1=x1, h2=h2, up_pre=up_pre, a=a, down=down,
              sc1=sc1, g1=g1, sc2=sc2, g2=g2)
    return x1, down, g2, sv, couts


def _big_grads(g):
    return [g["w_in"], g["w_out"].reshape(NCHIP, D // NCHIP, D), g["w_up"],
            g["w_down"].reshape(NCHIP, DFF // NCHIP, D)]


def _chip_sums(grads, recv):
    mc = lax.axis_index("c")
    mine = [lax.dynamic_slice_in_dim(a, (a.shape[1] // 2) * mc, a.shape[1] // 2, axis=1) for a in grads]
    return [_add2(a, b, bf16) for a, b in zip(mine, recv)]


def _reduced(parts, sums):
    halves = [_sum_parts(p) for p in _scatter_done(parts, sums)]
    return _merge_done(_comm_call("grad_merge", _merge_comm(halves)), halves)


def _layer_bwd(dx_out, wl, sv, prev=None):
    L = dx_out.shape[0]
    bm = _row_block(L, 1024)
    nb = L // bm
    bk = _row_block(L, 512)
    nk = L // bk
    g = {}
    dd, dg2 = _gate_bwd(dx_out, sv["down"], sv["g2"])
    da = _mm("mm_down_dx", dd, wl["w_down"], dims=NT, grid=(nb, 2, 1),
             a_block=(bm, D), a_map=lambda i, j, k: (i, 0),
             b_block=(1408, D), b_map=lambda i, j, k: (j, 0),
             out_shape=(L, DFF), o_block=(bm, 1408), o_map=lambda i, j, k: (i, j))
    g["w_down"] = _mm("mm_down_dw", sv["a"], dd, dims=TN, grid=(1, 2, nk),
                      a_block=(bk, DFF), a_map=lambda i, j, k: (k, 0),
                      b_block=(bk, 512), b_map=lambda i, j, k: (k, j),
                      out_shape=(DFF, D), o_block=(DFF, 512), o_map=lambda i, j, k: (0, j))
    dup, g["conv_w"], g["conv_b"], recv = _convglu_bwd(da, sv["up_pre"], wl["conv_w"], wl["conv_b"],
                                                       _swap_comm(prev) if prev else None)
    sums = _chip_sums(prev, recv) if prev else None
    dh2 = _mm("mm_up_dx", dup, wl["w_up"], dims=NT, grid=(nb, 1, 4),
              a_block=(None, bm, 1408), a_map=lambda i, j, k: (k // 2, i, k % 2),
              b_block=(None, D, 1408), b_map=lambda i, j, k: (k, 0, 0),
              out_shape=(L, D), o_block=(bm, D), o_map=lambda i, j, k: (i, 0))
    g["w_up"] = _mm("mm_up_dw", sv["h2"], dup, dims=TN, grid=(1, 4, nk),
                    a_block=(bk, D), a_map=lambda i, j, k: (k, 0),
                    b_block=(None, bk, 1408), b_map=lambda i, j, k: (j // 2, k, j % 2),
                    out_shape=(NCHIP, D, 1408), o_block=(None, D, 1408), o_map=lambda i, j, k: (j, 0, 0))
    dx1, dsh2, dsc2, g["norm2_g"] = _rmsmod_bwd(dh2, sv["x1"], wl["norm2_g"], sv["sc2"], dx_out)
    dproj, dg1 = _gate_bwd(dx1, sv["proj"], sv["g1"])
    do = _mm("mm_out_dx", dproj, wl["w_out"], dims=NT, grid=(nb, 2, 1),
             a_block=(bm, D), a_map=lambda i, j, k: (i, 0),
             b_block=(512, D), b_map=lambda i, j, k: (j, 0),
             out_shape=(L, D), o_block=(bm, 512), o_map=lambda i, j, k: (i, j))
    g["w_out"] = _mm("mm_out_dw", sv["o"], dproj, dims=TN, grid=(1, 2, nk),
                     a_block=(bk, D), a_map=lambda i, j, k: (k, 0),
                     b_block=(bk, 512), b_map=lambda i, j, k: (k, j),
                     out_shape=(D, D), o_block=(D, 512), o_map=lambda i, j, k: (0, j))
    doa, dos, g["attn_out_g"], g["ssm_out_g"] = _outnorm_bwd(do, sv["oa"], sv["os"],
                                                             wl["attn_out_g"], wl["ssm_out_g"])
    (dqn, dkn, dv), parts = _attn_bwd(sv["qkv"], sv["oa"], doa, _scatter_comm(sums) if prev else None)
    prev_reduced = _reduced(parts, sums) if prev else None
    (du, d_abar_re, d_abar_im, g["ssm_d"], g["glu_b"], d_ccr, d_cci, d_bbre, d_bbim, d_gw) = _s5_bwd2(
        dos, sv["p"], sv["s_re"], sv["s_im"], wl["abar_re"], wl["abar_im"], wl["bbr"], wl["bbi"],
        wl["ccr"], wl["cci"], wl["dsk"], wl["gw"], wl["gb"])
    g["ssm_c_re"] = d_ccr.reshape(NG, NP, NH16).transpose(0, 2, 1)
    g["ssm_c_im"] = -d_cci.reshape(NG, NP, NH16).transpose(0, 2, 1)
    g["glu_w"] = d_gw.reshape(NG, NH16, NH16)
    d_ldt, d_ar, d_ai, d_br, d_bi = _s5_disc_bwd(
        wl["ldt_col"], wl["ar_col"], wl["ai_col"], wl["br_mat"], wl["bi_mat"],
        d_abar_re.reshape(NS, 1), d_abar_im.reshape(NS, 1), d_bbre, d_bbim)
    g["ssm_log_dt"] = d_ldt[:, 0]
    g["ssm_a_re"] = d_ar.reshape(NG, NP)
    g["ssm_a_im"] = d_ai.reshape(NG, NP)
    g["ssm_b_re"] = d_br.reshape(NG, NP, NH16)
    g["ssm_b_im"] = d_bi.reshape(NG, NP, NH16)
    dp, dgqk = _qknorm_bwd(dqn, dkn, dv, du, sv["p"], wl["gqk"])
    g["q_norm_g"] = dgqk[0, :HD]
    g["k_norm_g"] = dgqk[0, HD:2 * HD]
    dh = _mm("mm_in_dx", dp, wl["w_in"], dims=NT, grid=(nb, 1, 4),
             a_block=(bm, 512), a_map=lambda i, j, k: (i, k),
             b_block=(None, D, 512), b_map=lambda i, j, k: (k, 0, 0),
             out_shape=(L, D), o_block=(bm, D), o_map=lambda i, j, k: (i, 0))
    g["w_in"] = _mm("mm_in_dw", sv["h"], dp, dims=TN, grid=(1, 4, nk),
                    a_block=(bk, D), a_map=lambda i, j, k: (k, 0),
                    b_block=(bk, 512), b_map=lambda i, j, k: (k, j),
                    out_shape=(NCHIP, D, 512), o_block=(None, D, 512), o_map=lambda i, j, k: (j, 0, 0))
    dx_in, dsh1, dsc1, g["norm1_g"] = _rmsmod_bwd(dh, sv["xin"], wl["norm1_g"], sv["sc1"], dx1)
    g["dmod"] = jnp.concatenate([dsh1, dsc1, dg1, dsh2, dsc2, dg2], axis=1)[0]
    return dx_in, g, prev_reduced


def _prep_layer(l, conv_w, small):
    wl = {}
    wl["conv_w"] = conv_w[l]
    wl["conv_b"] = small["ffn_conv_b"][l][None, :]
    wl["norm1_g"] = small["norm1_g"][l][None, :]
    wl["norm2_g"] = small["norm2_g"][l][None, :]
    wl["attn_out_g"] = small["attn_out_g"][l][None, :]
    wl["ssm_out_g"] = small["ssm_out_g"][l][None, :]
    wl["gqk"] = jnp.concatenate([jnp.tile(small["q_norm_g"][l], 8), jnp.tile(small["k_norm_g"][l], 8)])[None, :]
    wl["ldt_col"] = jnp.repeat(small["ssm_log_dt"][l], NP)[:, None]
    wl["ar_col"] = small["ssm_a_re"][l].reshape(NS, 1)
    wl["ai_col"] = small["ssm_a_im"][l].reshape(NS, 1)
    wl["br_mat"] = small["ssm_b_re"][l].reshape(NS, NH16)
    wl["bi_mat"] = small["ssm_b_im"][l].reshape(NS, NH16)
    abar_re, abar_im, bb_re, bb_im = _s5_disc_fwd(wl["ldt_col"], wl["ar_col"], wl["ai_col"],
                                                  wl["br_mat"], wl["bi_mat"])
    wl["abar_re"] = abar_re.reshape(1, NS)
    wl["abar_im"] = abar_im.reshape(1, NS)
    wl["bbr"] = _block_diag(bb_re.reshape(NG, NP, NH16).transpose(0, 2, 1)).astype(bf16)
    wl["bbi"] = _block_diag(bb_im.reshape(NG, NP, NH16).transpose(0, 2, 1)).astype(bf16)
    wl["ccr"] = _block_diag(small["ssm_c_re"][l].transpose(0, 2, 1)).astype(bf16)
    wl["cci"] = _block_diag(small["ssm_c_im"][l].transpose(0, 2, 1)).astype(bf16)
    wl["gw"] = _block_diag(small["glu_w"][l]).astype(bf16)
    wl["dsk"] = small["ssm_d"][l].reshape(1, SSMW)
    wl["gb"] = small["glu_b"][l].reshape(1, SSMW)
    return wl


def _local_step(x, target, mods, shards, conv_w, small):
    def set_big(wl, gathered):
        w_in, w_out, w_up, w_down = gathered
        wl.update(w_in=w_in, w_up=w_up, w_out=w_out.reshape(D, D), w_down=w_down.reshape(DFF, D))

    def layer_shards(l):
        return [s[l] for s in shards]

    wls = [_prep_layer(l, conv_w, small) for l in range(DEPTH)]
    first = _gather_comm(layer_shards(0))
    set_big(wls[0], _gather_done(_comm_call("weight_gather", first), first["ins"]))
    saved = []
    cur, res, res_gate = x, None, None
    for l in range(DEPTH):
        nxt = _gather_comm(layer_shards(l + 1)) if l + 1 < DEPTH else None
        cur, res, res_gate, sv, couts = _layer_fwd(cur, res, res_gate, wls[l], mods[l], nxt)
        saved.append(sv)
        if nxt is not None:
            set_big(wls[l + 1], _gather_done(couts, nxt["ins"]))
    loss, dx = _loss_fwd_bwd(cur, res, res_gate, target)
    grads = [None] * DEPTH
    big = [None] * DEPTH
    prev = None
    for l in reversed(range(DEPTH)):
        dx, grads[l], done = _layer_bwd(dx, wls[l], saved[l], prev)
        if prev is not None:
            big[l + 1] = done
        prev = _big_grads(grads[l])
    recv = _comm_call("grad_swap", _swap_comm(prev))
    sums = _chip_sums(prev, recv)
    big[0] = _reduced(_comm_call("grad_scatter", _scatter_comm(sums)), sums)
    return loss, dx, grads, big


BIG = ("w_in", "w_out", "ffn_w_up", "ffn_w_down")
SMALL = ("ada_b", "norm1_g", "q_norm_g", "k_norm_g", "ssm_a_re", "ssm_a_im", "ssm_log_dt",
         "ssm_b_re", "ssm_b_im", "ssm_c_re", "ssm_c_im", "ssm_d", "glu_w", "glu_b",
         "attn_out_g", "ssm_out_g", "norm2_g", "ffn_conv_b")
NAMES = ("ada_w", "ada_b", "norm1_g", "w_in", "q_norm_g", "k_norm_g", "ssm_a_re", "ssm_a_im",
         "ssm_log_dt", "ssm_b_re", "ssm_b_im", "ssm_c_re", "ssm_c_im", "ssm_d", "glu_w", "glu_b",
         "attn_out_g", "ssm_out_g", "w_out", "norm2_g", "ffn_w_up", "ffn_conv_w", "ffn_conv_b",
         "ffn_w_down")
PACK_COLS = 1024


def _pack(arrs):
    flat = jnp.concatenate([a.reshape(-1) for a in arrs])
    rows = -(-flat.shape[0] // PACK_COLS)
    rows = -(-rows // 8) * 8
    flat = jnp.pad(flat, (0, rows * PACK_COLS - flat.shape[0]))
    return flat.reshape(rows, PACK_COLS)


def _unpack(packed, shapes):
    flat = packed.reshape(-1)
    out, off = [], 0
    for s in shapes:
        n = math.prod(s)
        out.append(flat[off:off + n].reshape(s))
        off += n
    return out


def kernel(x, c, ada_w, ada_b, norm1_g, w_in, q_norm_g, k_norm_g, ssm_a_re, ssm_a_im, ssm_log_dt, ssm_b_re, ssm_b_im, ssm_c_re, ssm_c_im, ssm_d, glu_w, glu_b, attn_out_g, ssm_out_g, w_out, norm2_g, ffn_w_up, ffn_conv_w, ffn_conv_b, ffn_w_down, loss_target, m_ada_w, m_ada_b, m_norm1_g, m_w_in, m_q_norm_g, m_k_norm_g, m_ssm_a_re, m_ssm_a_im, m_ssm_log_dt, m_ssm_b_re, m_ssm_b_im, m_ssm_c_re, m_ssm_c_im, m_ssm_d, m_glu_w, m_glu_b, m_attn_out_g, m_ssm_out_g, m_w_out, m_norm2_g, m_ffn_w_up, m_ffn_conv_w, m_ffn_conv_b, m_ffn_w_down, v_ada_w, v_ada_b, v_norm1_g, v_w_in, v_q_norm_g, v_k_norm_g, v_ssm_a_re, v_ssm_a_im, v_ssm_log_dt, v_ssm_b_re, v_ssm_b_im, v_ssm_c_re, v_ssm_c_im, v_ssm_d, v_glu_w, v_glu_b, v_attn_out_g, v_ssm_out_g, v_w_out, v_norm2_g, v_ffn_w_up, v_ffn_conv_w, v_ffn_conv_b, v_ffn_w_down):
    env = dict(locals())
    w = {n: env[n] for n in NAMES}
    m = {n: env["m_" + n] for n in NAMES}
    v = {n: env["v_" + n] for n in NAMES}
    mx, my, mc = _my_pos()
    chip = 2 * mx + my
    dev = 4 * mx + 2 * my + mc
    xl = x[0]
    tl = loss_target[0]

    c_all = _allgather8("gather_c", jnp.pad(c, ((0, 7), (0, 0))))[:, 0, :]
    c16 = jnp.pad(c_all, ((0, 8), (0, 0)))
    nloc = NMOD * D // NCHIP
    ada_b_loc = lax.dynamic_slice(ada_b, (0, chip * nloc), (DEPTH, nloc))[:, None, :]
    mod_loc = _mod_fwd(c16, ada_w, ada_b_loc)
    mod_g = _allgather8("gather_mod", mod_loc.reshape(DEPTH * 16, nloc))
    mod_all = mod_g[0::2].reshape(NCHIP, DEPTH, 16, nloc).transpose(1, 2, 0, 3).reshape(DEPTH, 16, NMOD * D)
    mods = lax.dynamic_index_in_dim(mod_all, dev, axis=1, keepdims=False)

    shards = [_cast_bf16(w[n]) for n in BIG]
    ncw = DUP // NCHIP
    conv_g = _allgather8("gather_conv", jnp.pad(w["ffn_conv_w"].reshape(DEPTH * 3, ncw), ((0, 4), (0, 0))))
    conv_w = conv_g[0::2, :DEPTH * 3].reshape(NCHIP, DEPTH, 3, ncw).transpose(1, 2, 0, 3).reshape(DEPTH, 3, DUP)
    small = {n: w[n] for n in SMALL}

    loss_loc, grad_x, grads, big = _local_step(xl, tl, mods, shards, conv_w, small)
    loss = lax.psum(loss_loc, ("x", "y", "c"))

    outs = {}
    for k, name in enumerate(BIG):
        g_big = jnp.stack([big[l][k] for l in range(DEPTH)])
        outs[name] = _adamw(g_big[None], w[name], m[name], v[name])

    small_names = SMALL[1:] + ("ffn_conv_w_full",)
    small_grads = []
    for n in SMALL[1:]:
        key = {"ffn_conv_b": "conv_b"}.get(n, n)
        small_grads.append(jnp.stack([grads[l][key].reshape(w[n].shape[1:]) for l in range(DEPTH)]))
    small_grads.append(jnp.stack([grads[l]["conv_w"] for l in range(DEPTH)]))
    dmod = jnp.stack([grads[l]["dmod"] for l in range(DEPTH)])
    packed = _pack([dmod] + small_grads)
    allp = _allgather8("gather_small", packed)
    shapes = [dmod.shape] + [a.shape for a in small_grads]
    dmod_all = allp[:, :DEPTH * NMOD, :].reshape(NDEV, DEPTH, NMOD * D)
    dmod_loc = lax.dynamic_slice(dmod_all, (0, 0, chip * nloc), (NDEV, DEPTH, nloc)).transpose(1, 0, 2)
    g_ada = _ada_grad(c16, jnp.pad(dmod_loc, ((0, 0), (0, 8), (0, 0))))
    outs["ada_w"] = _adamw(g_ada[None], w["ada_w"], m["ada_w"], v["ada_w"])
    conv_w_shape = (DEPTH, 3, DUP)
    w_small = _pack([w[n] for n in SMALL] + [jnp.zeros(conv_w_shape, f32)])
    m_small = _pack([m[n] for n in SMALL] + [jnp.zeros(conv_w_shape, f32)])
    v_small = _pack([v[n] for n in SMALL] + [jnp.ones(conv_w_shape, f32)])
    res_small = _adamw(allp, w_small, m_small, v_small)
    unpacked = [_unpack(r, shapes) for r in res_small]
    for i, n in enumerate(SMALL):
        outs[n] = tuple(unpacked[k][i] for k in range(4))
    g_conv_full = unpacked[0][len(SMALL)]
    ncw = DUP // NCHIP
    g_conv = lax.dynamic_slice(g_conv_full, (0, 0, chip * ncw), (DEPTH, 3, ncw))
    outs["ffn_conv_w"] = _adamw(g_conv[None], w["ffn_conv_w"], m["ffn_conv_w"], v["ffn_conv_w"])

    result = [loss, grad_x[None]]
    for k in range(4):
        result += [outs[n][k] for n in NAMES]
    return tuple(result)
```

```python
import functools
import math

import jax
import jax.numpy as jnp
from jax import lax
from jax.experimental import pallas as pl
from jax.experimental.pallas import tpu as pltpu

f32 = jnp.float32
bf16 = jnp.bfloat16
_MXU = jnp.bfloat16

D = 1024
ATT = 512
SSMW = 512
HD = 64
NG = 32
NP = 64
NH16 = 16
NS = NG * NP
INW = 2048
DFF = 2816
DUP = 2 * DFF
NMOD = 6
DEPTH = 4
EPS = 1e-6
NCHIP = 4
NDEV = 8

ADAM_LR = 0.001
ADAM_B1 = 0.9
ADAM_B2 = 0.999
ADAM_EPS = 1e-08
ADAM_WD = 0.01
ADAM_STEP = 10

VMEM_LIMIT = 56 * 1024 * 1024
MESH = pl.DeviceIdType.MESH

NN = (((1,), (0,)), ((), ()))
NT = (((1,), (1,)), ((), ()))
TN = (((0,), (0,)), ((), ()))


def _cp(sem=None):
    if sem is None:
        return pltpu.CompilerParams(vmem_limit_bytes=VMEM_LIMIT)
    return pltpu.CompilerParams(dimension_semantics=sem, vmem_limit_bytes=VMEM_LIMIT)


def _dot(a, b, dims=NN):
    return lax.dot_general(a.astype(_MXU), b.astype(_MXU), dims, preferred_element_type=f32)


def _dot_exact(x, m):
    hi = x.astype(bf16)
    r1 = x - hi.astype(f32)
    mid = r1.astype(bf16)
    lo = (r1 - mid.astype(f32)).astype(bf16)
    d = lambda p: lax.dot_general(p, m, NN, preferred_element_type=f32)
    return d(hi) + d(mid) + d(lo)


def _gelu(x):
    c = math.sqrt(2.0 / math.pi)
    return 0.5 * x * (1.0 + jnp.tanh(c * (x + 0.044715 * (x * x * x))))


def _gelu_grad(x):
    c = math.sqrt(2.0 / math.pi)
    t = jnp.tanh(c * (x + 0.044715 * (x * x * x)))
    return 0.5 * (1.0 + t) + 0.5 * x * (1.0 - t * t) * c * (1.0 + 3.0 * 0.044715 * (x * x))


def _sigmoid(x):
    return 1.0 / (1.0 + jnp.exp(-x))


def _mm(name, a, b, *, dims, grid, a_block, a_map, b_block, b_map, out_shape, o_block, o_map,
        out_dtype=f32):
    nk = grid[2]
    acc_shape = tuple(s for s in o_block if s is not None)

    def body(a_ref, b_ref, o_ref, acc_ref):
        k = pl.program_id(2)
        part = _dot(a_ref[...], b_ref[...], dims)
        if nk == 1:
            o_ref[...] = part.astype(o_ref.dtype)
        else:
            @pl.when(k == 0)
            def _():
                acc_ref[...] = part

            @pl.when(k > 0)
            def _():
                acc_ref[...] += part

            @pl.when(k == nk - 1)
            def _():
                o_ref[...] = acc_ref[...].astype(o_ref.dtype)

    return pl.pallas_call(
        body, name=name, grid=grid,
        in_specs=[pl.BlockSpec(a_block, a_map), pl.BlockSpec(b_block, b_map)],
        out_specs=pl.BlockSpec(o_block, o_map),
        out_shape=jax.ShapeDtypeStruct(out_shape, out_dtype),
        scratch_shapes=[pltpu.VMEM(acc_shape if nk > 1 else (8, 128), f32)],
        compiler_params=_cp(("parallel", "parallel", "arbitrary")),
    )(a, b)


def _row_block(L, want):
    return want if L % want == 0 else L


def _rmsmod_fwd(x, res, gate, g, sh, sc):
    L = x.shape[0]
    bm = _row_block(L, 256)
    with_res = res is not None

    def body(*refs):
        if with_res:
            x_ref, r_ref, gt_ref, g_ref, sh_ref, sc_ref, xo_ref, h_ref = refs
            xin = x_ref[...] + gt_ref[...] * r_ref[...]
            xo_ref[...] = xin
        else:
            x_ref, g_ref, sh_ref, sc_ref, h_ref = refs
            xin = x_ref[...]
        inv = lax.rsqrt(jnp.mean(xin * xin, axis=-1, keepdims=True) + EPS)
        xn = xin * inv * g_ref[...]
        h_ref[...] = (xn * (1.0 + sc_ref[...]) + sh_ref[...]).astype(h_ref.dtype)

    row = pl.BlockSpec((bm, D), lambda i: (i, 0))
    vec = pl.BlockSpec((1, D), lambda i: (0, 0))
    if with_res:
        return pl.pallas_call(
            body, name="rmsmod_res_fwd", grid=(L // bm,),
            in_specs=[row, row, vec, vec, vec, vec], out_specs=[row, row],
            out_shape=[jax.ShapeDtypeStruct((L, D), f32), jax.ShapeDtypeStruct((L, D), bf16)],
            compiler_params=_cp(("parallel",)),
        )(x, res, gate, g, sh, sc)
    h = pl.pallas_call(
        body, name="rmsmod_fwd", grid=(L // bm,),
        in_specs=[row, vec, vec, vec], out_specs=row,
        out_shape=jax.ShapeDtypeStruct((L, D), bf16),
        compiler_params=_cp(("parallel",)),
    )(x, g, sh, sc)
    return None, h


def _rmsmod_bwd(dh, x, g, sc, dres):
    L = x.shape[0]
    bm = _row_block(L, 256)

    def body(dh_ref, x_ref, g_ref, sc_ref, dr_ref, dx_ref, dsh_ref, dsc_ref, dg_ref):
        i = pl.program_id(0)
        xv = x_ref[...]
        dhv = dh_ref[...]
        inv = lax.rsqrt(jnp.mean(xv * xv, axis=-1, keepdims=True) + EPS)
        xh = xv * inv
        gv = g_ref[...]
        xn = xh * gv
        dxn = dhv * (1.0 + sc_ref[...])
        dxh = dxn * gv
        dx_ref[...] = inv * (dxh - xh * jnp.mean(dxh * xh, axis=-1, keepdims=True)) + dr_ref[...]
        p_sh = jnp.sum(dhv, axis=0, keepdims=True)
        p_sc = jnp.sum(dhv * xn, axis=0, keepdims=True)
        p_g = jnp.sum(dxn * xh, axis=0, keepdims=True)

        @pl.when(i == 0)
        def _():
            dsh_ref[...] = p_sh
            dsc_ref[...] = p_sc
            dg_ref[...] = p_g

        @pl.when(i > 0)
        def _():
            dsh_ref[...] += p_sh
            dsc_ref[...] += p_sc
            dg_ref[...] += p_g

    row = pl.BlockSpec((bm, D), lambda i: (i, 0))
    vec = pl.BlockSpec((1, D), lambda i: (0, 0))
    return pl.pallas_call(
        body, name="rmsmod_bwd", grid=(L // bm,),
        in_specs=[row, row, vec, vec, row], out_specs=[row, vec, vec, vec],
        out_shape=[jax.ShapeDtypeStruct((L, D), f32)] + [jax.ShapeDtypeStruct((1, D), f32)] * 3,
        compiler_params=_cp(("arbitrary",)),
    )(dh, x, g, sc, dres)


def _gate_bwd(dx, y, gate):
    L = dx.shape[0]
    bm = _row_block(L, 256)

    def body(dx_ref, y_ref, gt_ref, dy_ref, dgt_ref):
        i = pl.program_id(0)
        dxv = dx_ref[...]
        dy_ref[...] = (gt_ref[...] * dxv).astype(dy_ref.dtype)
        part = jnp.sum(dxv * y_ref[...], axis=0, keepdims=True)

        @pl.when(i == 0)
        def _():
            dgt_ref[...] = part

        @pl.when(i > 0)
        def _():
            dgt_ref[...] += part

    row = pl.BlockSpec((bm, D), lambda i: (i, 0))
    vec = pl.BlockSpec((1, D), lambda i: (0, 0))
    return pl.pallas_call(
        body, name="gate_bwd", grid=(L // bm,),
        in_specs=[row, row, vec], out_specs=[row, vec],
        out_shape=[jax.ShapeDtypeStruct((L, D), bf16), jax.ShapeDtypeStruct((1, D), f32)],
        compiler_params=_cp(("arbitrary",)),
    )(dx, y, gate)


def _loss_fwd_bwd(x1, down, gate, target):
    L = x1.shape[0]
    bm = _row_block(L, 256)
    nsteps = L // bm

    def body(x_ref, d_ref, gt_ref, t_ref, dy_ref, loss_ref, acc_ref):
        i = pl.program_id(0)
        diff = x_ref[...] + gt_ref[...] * d_ref[...] - t_ref[...]
        dy_ref[...] = diff * (1.0 / D)
        part = jnp.sum(diff * diff, axis=0, keepdims=True)

        @pl.when(i == 0)
        def _():
            acc_ref[...] = part

        @pl.when(i > 0)
        def _():
            acc_ref[...] += part

        @pl.when(i == nsteps - 1)
        def _():
            tot = jnp.sum(acc_ref[...], axis=1, keepdims=True) * (0.5 / D)
            loss_ref[...] = jnp.broadcast_to(tot, (8, 128))

    row = pl.BlockSpec((bm, D), lambda i: (i, 0))
    vec = pl.BlockSpec((1, D), lambda i: (0, 0))
    dy, loss = pl.pallas_call(
        body, name="loss_fwd_bwd", grid=(nsteps,),
        in_specs=[row, row, vec, row],
        out_specs=[row, pl.BlockSpec((8, 128), lambda i: (0, 0))],
        out_shape=[jax.ShapeDtypeStruct((L, D), f32), jax.ShapeDtypeStruct((8, 128), f32)],
        scratch_shapes=[pltpu.VMEM((1, D), f32)],
        compiler_params=_cp(("arbitrary",)),
    )(x1, down, gate, target)
    return loss[0, 0], dy


def _head_mean_matrix():
    r = lax.broadcasted_iota(jnp.int32, (128, 128), 0) // HD
    c = lax.broadcasted_iota(jnp.int32, (128, 128), 1) // HD
    return jnp.where(r == c, 1.0 / HD, 0.0).astype(bf16)


def _qknorm_fwd(p, gqk):
    L = p.shape[0]
    bm = _row_block(L, 256)

    def body(p_ref, g_ref, o_ref):
        e = _head_mean_matrix()
        for c in range(8):
            sl = slice(128 * c, 128 * (c + 1))
            xv = p_ref[:, sl]
            inv = lax.rsqrt(_dot_exact(xv * xv, e) + EPS)
            yv = xv * inv * g_ref[:, sl]
            if c < 4:
                yv = yv * (1.0 / math.sqrt(HD))
            o_ref[:, sl] = yv.astype(o_ref.dtype)
        o_ref[:, 1024:1536] = p_ref[:, 1024:1536].astype(o_ref.dtype)

    return pl.pallas_call(
        body, name="qknorm_fwd", grid=(L // bm,),
        in_specs=[pl.BlockSpec((bm, 1536), lambda i: (i, 0)), pl.BlockSpec((1, 1024), lambda i: (0, 0))],
        out_specs=pl.BlockSpec((bm, 1536), lambda i: (i, 0)),
        out_shape=jax.ShapeDtypeStruct((L, 1536), bf16),
        compiler_params=_cp(("parallel",)),
    )(p, gqk)


def _qknorm_bwd(dqn, dkn, dval, du, p, gqk):
    L = p.shape[0]
    bm = _row_block(L, 256)
    nsteps = L // bm

    def body(dq_ref, dk_ref, dv_ref, du_ref, p_ref, g_ref, o_ref, dg_ref, acc_ref):
        i = pl.program_id(0)
        e = _head_mean_matrix()
        o_ref[:, 1024:1536] = dv_ref[...].astype(o_ref.dtype)
        o_ref[:, 1536:2048] = du_ref[...].astype(o_ref.dtype)
        for c in range(8):
            sl = slice(128 * c, 128 * (c + 1))
            xv = p_ref[:, sl]
            d_ref = dq_ref if c < 4 else dk_ref
            dv = d_ref[:, 128 * (c % 4):128 * (c % 4 + 1)]
            gv = g_ref[:, sl]
            inv = lax.rsqrt(_dot_exact(xv * xv, e) + EPS)
            xh = xv * inv
            dxh = dv * gv
            o_ref[:, sl] = (inv * (dxh - xh * _dot_exact(dxh * xh, e))).astype(o_ref.dtype)
            part = jnp.sum(dv * xh, axis=0, keepdims=True)

            @pl.when(i == 0)
            def _():
                acc_ref[:, sl] = part

            @pl.when(i > 0)
            def _():
                acc_ref[:, sl] += part

        @pl.when(i == nsteps - 1)
        def _():
            r = lax.broadcasted_iota(jnp.int32, (1024, 128), 0)
            col = lax.broadcasted_iota(jnp.int32, (1024, 128), 1)
            fold = jnp.where(col == (r // 512) * HD + r % HD, 1.0, 0.0).astype(bf16)
            dg_ref[...] = _dot_exact(jnp.broadcast_to(acc_ref[...], (8, 1024)), fold)

    half = pl.BlockSpec((bm, 512), lambda i: (i, 0))
    return pl.pallas_call(
        body, name="qknorm_bwd", grid=(nsteps,),
        in_specs=[half, half, half, half, pl.BlockSpec((bm, 1024), lambda i: (i, 0)),
                  pl.BlockSpec((1, 1024), lambda i: (0, 0))],
        out_specs=[pl.BlockSpec((bm, INW), lambda i: (i, 0)), pl.BlockSpec((8, 128), lambda i: (0, 0))],
        out_shape=[jax.ShapeDtypeStruct((L, INW), bf16), jax.ShapeDtypeStruct((8, 128), f32)],
        scratch_shapes=[pltpu.VMEM((1, 1024), f32)],
        compiler_params=_cp(("arbitrary",)),
    )(dqn, dkn, dval, du, p, gqk)


def _outnorm_fwd(oa, os_, ga, gs):
    L = oa.shape[0]
    bm = _row_block(L, 256)

    def body(a_ref, s_ref, ga_ref, gs_ref, o_ref):
        for x_ref, g_ref, off in ((a_ref, ga_ref, 0), (s_ref, gs_ref, 512)):
            xv = x_ref[...]
            inv = lax.rsqrt(jnp.mean(xv * xv, axis=-1, keepdims=True) + EPS)
            o_ref[:, off:off + 512] = (xv * inv * g_ref[...]).astype(o_ref.dtype)

    half = pl.BlockSpec((bm, 512), lambda i: (i, 0))
    vec = pl.BlockSpec((1, 512), lambda i: (0, 0))
    return pl.pallas_call(
        body, name="outnorm_fwd", grid=(L // bm,),
        in_specs=[half, half, vec, vec], out_specs=pl.BlockSpec((bm, D), lambda i: (i, 0)),
        out_shape=jax.ShapeDtypeStruct((L, D), bf16),
        compiler_params=_cp(("parallel",)),
    )(oa, os_, ga, gs)


def _outnorm_bwd(do, oa, os_, ga, gs):
    L = oa.shape[0]
    bm = _row_block(L, 256)

    def body(do_ref, a_ref, s_ref, ga_ref, gs_ref, da_ref, ds_ref, dga_ref, dgs_ref):
        i = pl.program_id(0)
        for x_ref, g_ref, dx_ref, dg_ref, off in ((a_ref, ga_ref, da_ref, dga_ref, 0),
                                                  (s_ref, gs_ref, ds_ref, dgs_ref, 512)):
            xv = x_ref[...]
            dv = do_ref[:, off:off + 512]
            inv = lax.rsqrt(jnp.mean(xv * xv, axis=-1, keepdims=True) + EPS)
            xh = xv * inv
            dxh = dv * g_ref[...]
            dx_ref[...] = inv * (dxh - xh * jnp.mean(dxh * xh, axis=-1, keepdims=True))
            part = jnp.sum(dv * xh, axis=0, keepdims=True)

            @pl.when(i == 0)
            def _():
                dg_ref[...] = part

            @pl.when(i > 0)
            def _():
                dg_ref[...] += part

    half = pl.BlockSpec((bm, 512), lambda i: (i, 0))
    vec = pl.BlockSpec((1, 512), lambda i: (0, 0))
    return pl.pallas_call(
        body, name="outnorm_bwd", grid=(L // bm,),
        in_specs=[pl.BlockSpec((bm, D), lambda i: (i, 0)), half, half, vec, vec],
        out_specs=[half, half, vec, vec],
        out_shape=[jax.ShapeDtypeStruct((L, 512), f32)] * 2 + [jax.ShapeDtypeStruct((1, 512), f32)] * 2,
        compiler_params=_cp(("arbitrary",)),
    )(do, oa, os_, ga, gs)


NCB = DFF // 128


def _conv_rows(L):
    return 64 if L % 64 == 0 else 32


def _conv_chunk(x_ref, w_ref, b_ref, i, nch, R, L, ahead):
    base = pl.multiple_of(i * R, R)
    prev = x_ref[pl.ds(pl.multiple_of(jnp.maximum(base - 8, 0), 8), 8), :]
    parts = [jnp.where(i > 0, prev, 0.0), x_ref[pl.ds(base, R), :]]
    if ahead:
        nxt = x_ref[pl.ds(pl.multiple_of(jnp.minimum(base + R, L - 8), 8), 8), :]
        parts.append(jnp.where(i < nch - 1, nxt, 0.0))
    xe = jnp.concatenate(parts, axis=0)
    x1 = pltpu.roll(xe, 1, 0)
    x2 = pltpu.roll(xe, 2, 0)
    u = b_ref[...] + x2 * w_ref[0:1, :]
    u = u + x1 * w_ref[1:2, :]
    u = u + xe * w_ref[2:3, :]
    return u[8:], xe[8:], x1[8:], x2[8:]


def _fold8(x):
    out = x[0:8]
    for q in range(1, x.shape[0] // 8):
        out = out + x[8 * q:8 * (q + 1)]
    return out


def _convglu_fwd(up_pre, cw, cb):
    L = up_pre.shape[0]

    def body(v_ref, g_ref, wv_ref, wg_ref, bv_ref, bg_ref, a_ref):
        row = lax.broadcasted_iota(jnp.int32, (L, 128), 0)

        def conv(x_ref, w_ref, b_ref):
            x = x_ref[...]
            out = b_ref[...] + jnp.where(row >= 2, pltpu.roll(x, 2, 0), 0.0) * w_ref[0:1, :]
            out = out + jnp.where(row >= 1, pltpu.roll(x, 1, 0), 0.0) * w_ref[1:2, :]
            return out + x * w_ref[2:3, :]

        val = conv(v_ref, wv_ref, bv_ref)
        a_ref[...] = (_gelu(conv(g_ref, wg_ref, bg_ref)) * val).astype(a_ref.dtype)

    col = lambda off: pl.BlockSpec((L, 128), lambda j: (0, j + off))
    wsp = lambda off: pl.BlockSpec((3, 128), lambda j: (0, j + off))
    bsp = lambda off: pl.BlockSpec((1, 128), lambda j: (0, j + off))
    return pl.pallas_call(
        body, name="convglu_fwd", grid=(NCB,),
        in_specs=[col(0), col(NCB), wsp(0), wsp(NCB), bsp(0), bsp(NCB)],
        out_specs=pl.BlockSpec((L, 128), lambda j: (0, j)),
        out_shape=jax.ShapeDtypeStruct((L, DFF), bf16),
        compiler_params=_cp(("parallel",)),
    )(up_pre, up_pre, cw, cw, cb, cb)


def _convglu_bwd(da, up_pre, cw, cb, comm=None):
    L = up_pre.shape[0]
    R = _conv_rows(L)
    nch = L // R
    n = R + 8

    def body(da_ref, v_ref, g_ref, wv_ref, wg_ref, bv_ref, bg_ref, dx_ref, dw_ref, db_ref):
        def chunk(i, acc):
            base = pl.multiple_of(i * R, R)
            val, xv, xv1, xv2 = _conv_chunk(v_ref, wv_ref, bv_ref, i, nch, R, L, True)
            gate, xg, xg1, xg2 = _conv_chunk(g_ref, wg_ref, bg_ref, i, nch, R, L, True)
            nxt = da_ref[pl.ds(pl.multiple_of(jnp.minimum(base + R, L - 8), 8), 8), :]
            dav = jnp.concatenate([da_ref[pl.ds(base, R), :], jnp.where(i < nch - 1, nxt, 0.0)], axis=0)
            c = math.sqrt(2.0 / math.pi)
            t = jnp.tanh(c * (gate + 0.044715 * (gate * gate * gate)))
            d_val = dav * (0.5 * gate * (1.0 + t))
            d_gate = dav * val * (0.5 * (1.0 + t)
                                  + 0.5 * gate * (1.0 - t * t) * c * (1.0 + 3.0 * 0.044715 * (gate * gate)))
            new = []
            for k, (d, w_ref, xs) in enumerate(((d_val, wv_ref, (xv2, xv1, xv)), (d_gate, wg_ref, (xg2, xg1, xg)))):
                dx = d * w_ref[2:3, :] + pltpu.roll(d, n - 1, 0) * w_ref[1:2, :]
                dx = dx + pltpu.roll(d, n - 2, 0) * w_ref[0:1, :]
                dx_ref[k, pl.ds(base, R), :] = dx[:R].astype(dx_ref.dtype)
                dr = d[:R]
                new += [_fold8(dr * x[:R]) for x in xs] + [_fold8(dr)]
            return tuple(a + b for a, b in zip(acc, new))

        acc = lax.fori_loop(0, nch, chunk, (jnp.zeros((8, 128), f32),) * 8)
        for k in range(2):
            for r in range(3):
                dw_ref[k, r:r + 1, :] = jnp.sum(acc[4 * k + r], axis=0, keepdims=True)
            db_ref[k] = jnp.sum(acc[4 * k + 3], axis=0, keepdims=True)

    col = lambda off: pl.BlockSpec((L, 128), lambda j: (0, j + off))
    wsp = lambda off: pl.BlockSpec((3, 128), lambda j: (0, j + off))
    bsp = lambda off: pl.BlockSpec((1, 128), lambda j: (0, j + off))
    (dx, dw, db), couts = _call(
        body, comm, name="convglu_bwd", grid=(NCB,),
        in_specs=[col(0), col(0), col(NCB), wsp(0), wsp(NCB), bsp(0), bsp(NCB)],
        out_specs=[pl.BlockSpec((2, L, 128), lambda j: (0, 0, j)), pl.BlockSpec((2, 3, 128), lambda j: (0, 0, j)),
                   pl.BlockSpec((2, 1, 128), lambda j: (0, 0, j))],
        out_shape=[jax.ShapeDtypeStruct((2, L, DFF), bf16), jax.ShapeDtypeStruct((2, 3, DFF), f32),
                   jax.ShapeDtypeStruct((2, 1, DFF), f32)], scratch_shapes=[],
        sem=("parallel",), args=(da, up_pre, up_pre, cw, cw, cb, cb))
    return dx, dw.transpose(1, 0, 2).reshape(3, DUP), db.transpose(1, 0, 2).reshape(1, DUP), couts


def _attn_block(L):
    return 256 if L % 256 == 0 else 128


TAIL_DEAD = -88.0


def _dot_exact2(x, m):
    hi = x.astype(bf16)
    lo = (x - hi.astype(f32)).astype(bf16)
    return (lax.dot_general(hi, m, NN, preferred_element_type=f32)
            + lax.dot_general(lo, m, NN, preferred_element_type=f32))


def _sb_weights(z, mask, tri_gt, carry):
    l1p = jnp.log(1.0 + jnp.exp(-jnp.abs(z)))
    ls_pos = jnp.minimum(z, 0.0) - l1p
    lm = ls_pos - z
    if mask is not None:
        lm = jnp.where(mask, lm, 0.0)
    tail = _dot_exact2(lm, tri_gt) + carry
    w = jnp.exp(ls_pos + tail)
    if mask is not None:
        w = jnp.where(mask, w, 0.0)
    return w, ls_pos, lm


KB = 128


def _packed_consts():
    row = lax.broadcasted_iota(jnp.int32, (KB, 2 * KB), 0)
    col = lax.broadcasted_iota(jnp.int32, (KB, 2 * KB), 1)
    r2 = lax.broadcasted_iota(jnp.int32, (2 * KB, 2 * KB), 0)
    c2 = lax.broadcasted_iota(jnp.int32, (2 * KB, 2 * KB), 1)
    same = (r2 < KB) == (c2 < KB)
    rk, ck = r2 & (KB - 1), c2 & (KB - 1)
    return dict(
        first=col < KB,
        diag=(col & (KB - 1)) < row,
        lane_h0=lax.broadcasted_iota(jnp.int32, (KB, 128), 1) < HD,
        tri_gt=jnp.where(same & (rk > ck), 1.0, 0.0).astype(bf16),
        tri_ge=jnp.where(same & (rk >= ck), 1.0, 0.0).astype(bf16))


def _packed_kv(k_ref, v_ref, jb, pk):
    ks = pl.multiple_of(jb * KB, KB)
    zero16 = jnp.zeros((), bf16)
    out = []
    for ref in (k_ref, v_ref):
        blk = ref[pl.ds(ks, KB), :]
        out.append(jnp.concatenate([jnp.where(pk["lane_h0"], blk, zero16),
                                    jnp.where(pk["lane_h0"], zero16, blk)], axis=0))
    return out


def _attn_fwd(qkv, comm=None):
    L = qkv.shape[0]
    B = _attn_block(L)
    nq = L // B

    def body(q_ref, k_ref, v_ref, o_ref):
        qi = pl.program_id(1)
        pk = _packed_consts()
        for sub in range(B // KB):
            qv = q_ref[KB * sub:KB * (sub + 1), :]
            jd = qi * (B // KB) + sub

            def tile(jb, mask, acc, c0, c1):
                kcat, vcat = _packed_kv(k_ref, v_ref, jb, pk)
                z = lax.dot_general(qv, kcat, NT, preferred_element_type=f32)
                w, _, lm = _sb_weights(z, mask, pk["tri_gt"], jnp.where(pk["first"], c0, c1))
                acc = acc + lax.dot_general(w.astype(bf16), vcat, NN, preferred_element_type=f32)
                return (acc, c0 + jnp.sum(lm[:, :KB], axis=1, keepdims=True),
                        c1 + jnp.sum(lm[:, KB:], axis=1, keepdims=True))

            zc = jnp.zeros((KB, 1), f32)
            acc, c0, c1 = tile(jd, pk["diag"], jnp.zeros((KB, 128), f32), zc, zc)

            def cond(st):
                return jnp.logical_and(st[0] <= jd, st[4] > TAIL_DEAD)

            def step(st):
                n, a, p0, p1, _ = st
                a, p0, p1 = tile(jd - n, None, a, p0, p1)
                return n + 1, a, p0, p1, jnp.maximum(jnp.max(p0), jnp.max(p1))

            st = lax.while_loop(cond, step, (jnp.int32(1), acc, c0, c1, jnp.maximum(jnp.max(c0), jnp.max(c1))))
            o_ref[KB * sub:KB * (sub + 1), :] = st[1]

    (o,), couts = _call(
        body, comm, name="attn_fwd", grid=(4, nq),
        in_specs=[pl.BlockSpec((B, 128), lambda hp, i: (i, hp)),
                  pl.BlockSpec((L, 128), lambda hp, i: (0, 4 + hp)),
                  pl.BlockSpec((L, 128), lambda hp, i: (0, 8 + hp))],
        out_specs=[pl.BlockSpec((B, 128), lambda hp, i: (i, hp))],
        out_shape=[jax.ShapeDtypeStruct((L, ATT), f32)], scratch_shapes=[],
        sem=("parallel", "parallel"), args=(qkv, qkv, qkv))
    return o, couts


def _attn_bwd(qkv, o, do, comm=None):
    L = qkv.shape[0]
    B = _attn_block(L)
    nq = L // B

    def body(q_ref, k_ref, v_ref, o_ref, do_ref, dq_ref, dk_ref, dv_ref):
        qi = pl.program_id(1)

        @pl.when(qi == 0)
        def _():
            dk_ref[...] = jnp.zeros_like(dk_ref)
            dv_ref[...] = jnp.zeros_like(dv_ref)

        pk = _packed_consts()
        first, lane_h0 = pk["first"], pk["lane_h0"]
        for sub in range(B // KB):
            rs = slice(KB * sub, KB * (sub + 1))
            qv = q_ref[rs, :]
            jd = qi * (B // KB) + sub
            do16 = do_ref[rs, :].astype(bf16)
            dsum_lanes = do16.astype(f32) * o_ref[rs, :]
            dsum = jnp.where(first, jnp.sum(jnp.where(lane_h0, dsum_lanes, 0.0), axis=1, keepdims=True),
                             jnp.sum(jnp.where(lane_h0, 0.0, dsum_lanes), axis=1, keepdims=True))

            def tile(jb, mask, dq_a, c0, s0, c1, s1):
                ks = pl.multiple_of(jb * KB, KB)
                kcat, vcat = _packed_kv(k_ref, v_ref, jb, pk)
                z = lax.dot_general(qv, kcat, NT, preferred_element_type=f32)
                w, ls_pos, lm = _sb_weights(z, mask, pk["tri_gt"], jnp.where(first, c0, c1))
                w16 = w.astype(bf16)
                dw = lax.dot_general(do16, vcat, NT, preferred_element_type=f32)
                da = w16.astype(f32) * dw
                sig = jnp.exp(ls_pos)
                suf = _dot_exact2(da, pk["tri_ge"]) + jnp.where(first, s0, s1)
                dz = da * (1.0 - sig) - sig * (dsum - suf)
                if mask is not None:
                    dz = jnp.where(mask, dz, 0.0)
                dz16 = dz.astype(bf16)
                dq_a = dq_a + lax.dot_general(dz16, kcat, NN, preferred_element_type=f32)
                rk = lax.dot_general(dz16, qv, TN, preferred_element_type=f32)
                rv = lax.dot_general(w16, do16, TN, preferred_element_type=f32)
                lane0 = lane_h0[:KB]
                dk_ref[pl.ds(ks, KB), :] += jnp.where(lane0, rk[:KB], rk[KB:])
                dv_ref[pl.ds(ks, KB), :] += jnp.where(lane0, rv[:KB], rv[KB:])
                return (dq_a, c0 + jnp.sum(lm[:, :KB], axis=1, keepdims=True),
                        s0 + jnp.sum(da[:, :KB], axis=1, keepdims=True),
                        c1 + jnp.sum(lm[:, KB:], axis=1, keepdims=True),
                        s1 + jnp.sum(da[:, KB:], axis=1, keepdims=True))

            def alive(c0, c1):
                return jnp.maximum(jnp.max(c0), jnp.max(c1))

            zc = jnp.zeros((KB, 1), f32)
            st0 = tile(jd, pk["diag"], jnp.zeros((KB, 128), f32), zc, zc, zc, zc)

            def cond(st):
                return jnp.logical_and(st[0] <= jd, st[6] > TAIL_DEAD)

            def step(st):
                n, a, c0, s0, c1, s1, _ = st
                a, c0, s0, c1, s1 = tile(jd - n, None, a, c0, s0, c1, s1)
                return n + 1, a, c0, s0, c1, s1, alive(c0, c1)

            st = lax.while_loop(cond, step, (jnp.int32(1), *st0, alive(st0[1], st0[3])))
            dq_ref[rs, :] = st[1] * (1.0 / math.sqrt(HD))

    blk = pl.BlockSpec((B, 128), lambda hp, i: (i, hp))
    return _call(
        body, comm, name="attn_bwd", grid=(4, nq),
        in_specs=[blk,
                  pl.BlockSpec((L, 128), lambda hp, i: (0, 4 + hp)),
                  pl.BlockSpec((L, 128), lambda hp, i: (0, 8 + hp)),
                  blk, blk],
        out_specs=[blk, pl.BlockSpec((L, 128), lambda hp, i: (0, hp)),
                   pl.BlockSpec((L, 128), lambda hp, i: (0, hp))],
        out_shape=[jax.ShapeDtypeStruct((L, ATT), f32)] * 3, scratch_shapes=[],
        sem=("parallel", "arbitrary"), args=(qkv, qkv, qkv, o, do))


def _s5_disc(ldt, ar, ai, br, bi):
    dt = jnp.exp(ldt)
    mag = jnp.exp(dt * ar)
    abar_re = mag * jnp.cos(dt * ai)
    abar_im = mag * jnp.sin(dt * ai)
    em_re = abar_re - 1.0
    em_im = abar_im
    den = ar * ar + ai * ai
    f_re = (em_re * ar + em_im * ai) / den
    f_im = (em_im * ar - em_re * ai) / den
    bb_re = f_re * br - f_im * bi
    bb_im = f_re * bi + f_im * br
    return abar_re, abar_im, bb_re, bb_im


def _s5_disc_fwd(ldt, ar, ai, br, bi):
    def body(ldt_ref, ar_ref, ai_ref, br_ref, bi_ref, o1, o2, o3, o4):
        outs = _s5_disc(ldt_ref[...], ar_ref[...], ai_ref[...], br_ref[...], bi_ref[...])
        for o_ref, v in zip((o1, o2, o3, o4), outs):
            o_ref[...] = v

    col = jax.ShapeDtypeStruct((NS, 1), f32)
    mat = jax.ShapeDtypeStruct((NS, NH16), f32)
    return pl.pallas_call(body, name="s5_disc_fwd", out_shape=[col, col, mat, mat],
                          compiler_params=_cp())(ldt, ar, ai, br, bi)


def _s5_disc_bwd(ldt, ar, ai, br, bi, d_are, d_aim, d_bbre, d_bbim):
    def body(ldt_ref, ar_ref, ai_ref, br_ref, bi_ref, c1, c2, c3, c4, o_ldt, o_ar, o_ai, o_br, o_bi):
        prim = (ldt_ref[...], ar_ref[...], ai_ref[...], br_ref[...], bi_ref[...])
        _, vjp = jax.vjp(_s5_disc, *prim)
        g_ldt, g_ar, g_ai, g_br, g_bi = vjp((c1[...], c2[...], c3[...], c4[...]))
        o_ar[...] = g_ar
        o_ai[...] = g_ai
        o_br[...] = g_br
        o_bi[...] = g_bi
        r = lax.broadcasted_iota(jnp.int32, (NG, NS), 0)
        c = lax.broadcasted_iota(jnp.int32, (NG, NS), 1)
        fold = jnp.where(c // NP == r, 1.0, 0.0).astype(bf16)
        hi = jnp.broadcast_to(g_ldt, (NS, 128))
        p1 = hi.astype(bf16)
        r1 = hi - p1.astype(f32)
        p2 = r1.astype(bf16)
        p3 = (r1 - p2.astype(f32)).astype(bf16)
        dd = lambda p: lax.dot_general(fold, p, NN, preferred_element_type=f32)
        o_ldt[...] = dd(p1) + dd(p2) + dd(p3)

    col = jax.ShapeDtypeStruct((NS, 1), f32)
    mat = jax.ShapeDtypeStruct((NS, NH16), f32)
    return pl.pallas_call(
        body, name="s5_disc_bwd",
        out_shape=[jax.ShapeDtypeStruct((NG, 128), f32), col, col, mat, mat],
        compiler_params=_cp())(ldt, ar, ai, br, bi, d_are, d_aim, d_bbre, d_bbim)


SCAN_LANES = 512


def _cmul(ar, ai, br, bi):
    return ar * br - ai * bi, ar * bi + ai * br


def _pow_tables(ar, ai):
    a2 = _cmul(ar, ai, ar, ai)
    a4 = _cmul(*a2, *a2)
    rows = [(ar, ai)]
    for _ in range(7):
        rows.append(_cmul(*rows[-1], ar, ai))
    return (ar, ai), a2, a4, rows


def _scan_rows(xr_ref, xi_ref, abar_re, abar_im, cr_ref, ci_ref, reverse, on_tile=None):
    T = xr_ref.shape[0]
    nt = T // 8
    row = lax.broadcasted_iota(jnp.int32, (8, SCAN_LANES), 0)
    for c in range(NS // SCAN_LANES):
        ls = slice(c * SCAN_LANES, (c + 1) * SCAN_LANES)
        ar = abar_re[:, ls]
        ai = abar_im[:, ls]
        if reverse:
            ai = -ai
        a1, a2, a4, prow = _pow_tables(ar, ai)
        if reverse:
            prow = prow[::-1]
        pr = jnp.concatenate([p[0] for p in prow], axis=0)
        pi = jnp.concatenate([p[1] for p in prow], axis=0)
        a8r, a8i = prow[0] if reverse else prow[7]
        edge = slice(0, 1) if reverse else slice(7, 8)
        steps = []
        for d, (dr, di) in ((1, a1), (2, a2), (4, a4)):
            keep = (row < 8 - d) if reverse else (row >= d)
            steps.append((8 - d if reverse else d, jnp.where(keep, dr, 0.0), jnp.where(keep, di, 0.0)))

        def local(n, _):
            rs = pl.ds(pl.multiple_of(n * 8, 8), 8)
            xr = xr_ref[rs, ls]
            xi = xi_ref[rs, ls]
            for shift, dr, di in steps:
                mr, mi = _cmul(dr, di, pltpu.roll(xr, shift, 0), pltpu.roll(xi, shift, 0))
                xr = xr + mr
                xi = xi + mi
            xr_ref[rs, ls] = xr
            xi_ref[rs, ls] = xi
            return 0

        lax.fori_loop(0, nt, local, 0, unroll=4)

        def chain(n, carry):
            cr, ci = carry
            it = (nt - 1 - n) if reverse else n
            rs = pl.ds(pl.multiple_of(it * 8, 8), 8)
            xr = xr_ref[rs, ls]
            xi = xi_ref[rs, ls]
            nr, ni = _cmul(a8r, a8i, cr, ci)
            nr = nr + xr[edge, :]
            ni = ni + xi[edge, :]
            mr, mi = _cmul(pr, pi, cr, ci)
            xr = xr + mr
            xi = xi + mi
            xr_ref[rs, ls] = xr
            xi_ref[rs, ls] = xi
            if on_tile is not None:
                nxt_r = jnp.where(row < 7, pltpu.roll(xr, 7, 0), cr)
                nxt_i = jnp.where(row < 7, pltpu.roll(xi, 7, 0), ci)
                on_tile(ls, rs, nxt_r, nxt_i)
            return nr, ni

        cr, ci = lax.fori_loop(0, nt, chain, (cr_ref[:, ls], ci_ref[:, ls]), unroll=2)
        cr_ref[:, ls] = cr
        ci_ref[:, ls] = ci


def _s5_chunk(L):
    return 256 if L % 256 == 0 else L


def _s5_gate(s_re, s_im, u, ccr_ref, cci_ref, dsk_ref, gw_ref, gb_ref):
    y = _dot(s_re, ccr_ref[...]) - _dot(s_im, cci_ref[...]) + dsk_ref[...] * u
    yg = _gelu(y)
    gate = _sigmoid(_dot(yg, gw_ref[...]) + gb_ref[...])
    return y, yg, gate


def _s5_fwd(p, abar_re, abar_im, bbr, bbi, ccr, cci, dsk, gw, gb):
    L = p.shape[0]
    T = _s5_chunk(L)

    def body(u_ref, are_ref, aim_ref, bbr_ref, bbi_ref, ccr_ref, cci_ref, dsk_ref, gw_ref, gb_ref,
             o_ref, sr_ref, si_ref, cr_ref, ci_ref):
        @pl.when(pl.program_id(0) == 0)
        def _():
            cr_ref[...] = jnp.zeros_like(cr_ref)
            ci_ref[...] = jnp.zeros_like(ci_ref)

        u = u_ref[...]
        sr_ref[...] = _dot(u, bbr_ref[...])
        si_ref[...] = _dot(u, bbi_ref[...])
        _scan_rows(sr_ref, si_ref, are_ref[...], aim_ref[...], cr_ref, ci_ref, reverse=False)
        _, yg, gate = _s5_gate(sr_ref[...], si_ref[...], u, ccr_ref, cci_ref, dsk_ref, gw_ref, gb_ref)
        o_ref[...] = yg * gate

    full = lambda shape: pl.BlockSpec(shape, lambda i: (0, 0))
    return pl.pallas_call(
        body, name="s5_fwd", grid=(L // T,),
        in_specs=[pl.BlockSpec((T, SSMW), lambda i: (i, 3)), full((1, NS)), full((1, NS)),
                  full((SSMW, NS)), full((SSMW, NS)), full((NS, SSMW)), full((NS, SSMW)),
                  full((1, SSMW)), full((SSMW, SSMW)), full((1, SSMW))],
        out_specs=[pl.BlockSpec((T, SSMW), lambda i: (i, 0)), pl.BlockSpec((T, NS), lambda i: (i, 0)),
                   pl.BlockSpec((T, NS), lambda i: (i, 0))],
        out_shape=[jax.ShapeDtypeStruct((L, SSMW), f32), jax.ShapeDtypeStruct((L, NS), f32),
                   jax.ShapeDtypeStruct((L, NS), f32)],
        scratch_shapes=[pltpu.VMEM((1, NS), f32), pltpu.VMEM((1, NS), f32)],
        compiler_params=_cp(("arbitrary",)),
    )(p, abar_re, abar_im, bbr, bbi, ccr, cci, dsk, gw, gb)


def _diag_fold(acc, rpg):
    aw = acc.shape[0]
    r = lax.broadcasted_iota(jnp.int32, (aw, 128), 0) // rpg
    c = lax.broadcasted_iota(jnp.int32, (aw, 128), 1) // NH16
    fr = lax.broadcasted_iota(jnp.int32, (128, NH16), 0) % NH16
    fc = lax.broadcasted_iota(jnp.int32, (128, NH16), 1)
    fold = jnp.where(fr == fc, 1.0, 0.0).astype(bf16)
    return _dot_exact(jnp.where(r == c, acc, 0.0), fold)


def _s5_bwd(dout, p, s_re, s_im, abar_re, abar_im, bbr, bbi, ccr, cci, dsk, gw, gb):
    L = p.shape[0]
    T = _s5_chunk(L)
    nchunk = L // T

    def body(do_ref, u_ref, sr_ref, si_ref, are_ref, aim_ref, bbr_ref, bbi_ref, ccr_ref, cci_ref,
             dsk_ref, gw_ref, gb_ref,
             du_ref, dare_ref, daim_ref, dd_ref, dgb_ref, dccr_ref, dcci_ref, dbbr_ref, dbbi_ref, dgw_ref,
             lr_ref, li_ref, cr_ref, ci_ref, accr_ref, acci_ref, wacc_ref, gacc_ref):
        i = pl.program_id(0)

        @pl.when(i == 0)
        def _():
            cr_ref[...] = jnp.zeros_like(cr_ref)
            ci_ref[...] = jnp.zeros_like(ci_ref)
            accr_ref[...] = jnp.zeros_like(accr_ref)
            acci_ref[...] = jnp.zeros_like(acci_ref)
            dd_ref[...] = jnp.zeros_like(dd_ref)
            dgb_ref[...] = jnp.zeros_like(dgb_ref)
            wacc_ref[...] = jnp.zeros_like(wacc_ref)
            gacc_ref[...] = jnp.zeros_like(gacc_ref)

        u = u_ref[...]
        dov = do_ref[...]
        y, yg, gate = _s5_gate(sr_ref[...], si_ref[...], u, ccr_ref, cci_ref, dsk_ref, gw_ref, gb_ref)
        dgl = dov * yg * gate * (1.0 - gate)
        dyg = dov * gate + _dot(dgl, gw_ref[...], NT)
        dy = dyg * _gelu_grad(y)
        u16, dy16, yg16, dgl16 = (t.astype(_MXU) for t in (u, dy, yg, dgl))
        for q in range(4):
            lanes = slice(128 * q, 128 * (q + 1))
            wide_l = slice(512 * q, 512 * (q + 1))
            wacc_ref[0, q] += _dot(sr_ref[:, wide_l], dy16[:, lanes], TN)
            wacc_ref[1, q] += _dot(si_ref[:, wide_l], dy16[:, lanes], TN)
            gacc_ref[q] += _dot(yg16[:, lanes], dgl16[:, lanes], TN)
        dd_ref[...] += jnp.sum(dy * u, axis=0, keepdims=True)
        dgb_ref[...] += jnp.sum(dgl, axis=0, keepdims=True)
        lr_ref[...] = _dot(dy, ccr_ref[...], NT)
        li_ref[...] = -_dot(dy, cci_ref[...], NT)

        def on_tile(ls, rs, nxt_r, nxt_i):
            s_r = sr_ref[rs, ls]
            s_i = si_ref[rs, ls]
            accr_ref[:, ls] += nxt_r * s_r + nxt_i * s_i
            acci_ref[:, ls] += nxt_i * s_r - nxt_r * s_i

        _scan_rows(lr_ref, li_ref, are_ref[...], aim_ref[...], cr_ref, ci_ref, reverse=True,
                   on_tile=on_tile)
        for q in range(4):
            lanes = slice(128 * q, 128 * (q + 1))
            wide_l = slice(512 * q, 512 * (q + 1))
            wacc_ref[2, q] += _dot(lr_ref[:, wide_l], u16[:, lanes], TN)
            wacc_ref[3, q] += _dot(li_ref[:, wide_l], u16[:, lanes], TN)
        du_ref[...] = (dy * dsk_ref[...] + _dot(lr_ref[...], bbr_ref[...], NT)
                       + _dot(li_ref[...], bbi_ref[...], NT))

        @pl.when(i == nchunk - 1)
        def _():
            dare_ref[...] = jnp.sum(accr_ref[...], axis=0, keepdims=True)
            daim_ref[...] = jnp.sum(acci_ref[...], axis=0, keepdims=True)
            for m, o_ref in enumerate((dccr_ref, dcci_ref, dbbr_ref, dbbi_ref)):
                for q in range(4):
                    o_ref[512 * q:512 * (q + 1), :] = _diag_fold(wacc_ref[m, q], NP)
            for q in range(4):
                dgw_ref[128 * q:128 * (q + 1), :] = _diag_fold(gacc_ref[q], NH16)

    rev = lambda i: (nchunk - 1 - i, 0)
    full = lambda shape: pl.BlockSpec(shape, lambda i: (0, 0))
    half = pl.BlockSpec((T, SSMW), rev)
    wide = pl.BlockSpec((T, NS), rev)
    return pl.pallas_call(
        body, name="s5_bwd", grid=(nchunk,),
        in_specs=[half, pl.BlockSpec((T, SSMW), lambda i: (nchunk - 1 - i, 3)), wide, wide,
                  full((1, NS)), full((1, NS)), full((SSMW, NS)), full((SSMW, NS)),
                  full((NS, SSMW)), full((NS, SSMW)), full((1, SSMW)), full((SSMW, SSMW)), full((1, SSMW))],
        out_specs=[half, full((1, NS)), full((1, NS)), full((1, SSMW)), full((1, SSMW)),
                   full((NS, NH16)), full((NS, NH16)), full((NS, NH16)), full((NS, NH16)),
                   full((SSMW, NH16))],
        out_shape=[jax.ShapeDtypeStruct((L, SSMW), f32)] + [jax.ShapeDtypeStruct((1, NS), f32)] * 2
                  + [jax.ShapeDtypeStruct((1, SSMW), f32)] * 2 + [jax.ShapeDtypeStruct((NS, NH16), f32)] * 4
                  + [jax.ShapeDtypeStruct((SSMW, NH16), f32)],
        scratch_shapes=[pltpu.VMEM((T, NS), f32), pltpu.VMEM((T, NS), f32),
                        pltpu.VMEM((1, NS), f32), pltpu.VMEM((1, NS), f32),
                        pltpu.VMEM((8, NS), f32), pltpu.VMEM((8, NS), f32),
                        pltpu.VMEM((4, 4, 512, 128), f32), pltpu.VMEM((4, 128, 128), f32)],
        compiler_params=_cp(("arbitrary",)),
    )(dout, p, s_re, s_im, abar_re, abar_im, bbr, bbi, ccr, cci, dsk, gw, gb)


NTILE = NS // 128
TILE_GROUP = 4


def _seg_entry(lr, li, ar, ai, cr, ci, seg_rows, reverse, row):
    pr, pi = ar, ai
    for _ in range(seg_rows.bit_length() - 1):
        pr, pi = _cmul(pr, pi, pr, pi)
    if reverse:
        xr = jnp.where(row == 7, cr, pltpu.roll(lr, 7, 0))
        xi = jnp.where(row == 7, ci, pltpu.roll(li, 7, 0))
    else:
        xr = jnp.where(row == 0, cr, pltpu.roll(lr, 1, 0))
        xi = jnp.where(row == 0, ci, pltpu.roll(li, 1, 0))
    for d in (1, 2, 4):
        keep = (row < 8 - d) if reverse else (row >= d)
        shift = 8 - d if reverse else d
        mr, mi = _cmul(pr, pi, jnp.where(keep, pltpu.roll(xr, shift, 0), 0.0),
                       jnp.where(keep, pltpu.roll(xi, shift, 0), 0.0))
        xr, xi = xr + mr, xi + mi
        pr, pi = _cmul(pr, pi, pr, pi)
    return xr, xi


def _seg_scan(xr3, xi3, abar_re, abar_im, cr_ref, ci_ref, reverse, on_step=None):
    pitch = xr3.shape[1] // 8
    seg = pitch - 8
    row = lax.broadcasted_iota(jnp.int32, (8, 128), 0)
    for g0 in range(0, NTILE, TILE_GROUP):
        tiles = list(range(g0, g0 + TILE_GROUP))
        lanes = [slice(128 * j, 128 * (j + 1)) for j in tiles]
        ar = [jnp.broadcast_to(abar_re[:, ls], (8, 128)) for ls in lanes]
        ai = [jnp.broadcast_to(-abar_im[:, ls] if reverse else abar_im[:, ls], (8, 128)) for ls in lanes]

        def sweep(store):
            def step(n, st):
                k = (seg - 1 - n) if reverse else n
                rows = pl.ds(k, 8, stride=pitch)
                new = []
                for q, j in enumerate(tiles):
                    sr, si = st[2 * q], st[2 * q + 1]
                    if store and on_step is not None:
                        on_step(j, rows, sr, si)
                    nr = ar[q] * sr - ai[q] * si + xr3.at[j][rows, :]
                    ni = ar[q] * si + ai[q] * sr + xi3.at[j][rows, :]
                    if store:
                        xr3.at[j][rows, :] = nr
                        xi3.at[j][rows, :] = ni
                    new += [nr, ni]
                return tuple(new)
            return step

        zero = jnp.zeros((8, 128), f32)
        ends = lax.fori_loop(0, seg, sweep(False), (zero,) * (2 * TILE_GROUP))
        init = []
        for q, ls in enumerate(lanes):
            init += _seg_entry(ends[2 * q], ends[2 * q + 1], ar[q], ai[q], cr_ref[:, ls], ci_ref[:, ls],
                               seg, reverse, row)
        fin = lax.fori_loop(0, seg, sweep(True), tuple(init))
        edge = slice(0, 1) if reverse else slice(7, 8)
        for q, ls in enumerate(lanes):
            cr_ref[:, ls] = fin[2 * q][edge, :]
            ci_ref[:, ls] = fin[2 * q + 1][edge, :]


def _coarse(i):
    return slice(128 * i, 128 * (i + 1)), slice(512 * i, 512 * (i + 1))


def _seg_pitch(T):
    return T // 8 + 8


def _put_rows(x3, j, val):
    pitch = x3.shape[1] // 8
    seg = pitch - 8
    for s in range(8):
        x3[j, pitch * s:pitch * s + seg, :] = val[seg * s:seg * (s + 1), :]


def _get_rows(x3, j):
    pitch = x3.shape[1] // 8
    seg = pitch - 8
    return jnp.concatenate([x3[j, pitch * s:pitch * s + seg, :] for s in range(8)], axis=0)


def _tiles_cat(x3, i):
    return jnp.concatenate([_get_rows(x3, 4 * i + q) for q in range(4)], axis=1)


def _s5_gate2(sr3, si3, u, ccr_ref, cci_ref, dsk_ref, gw_ref, gb_ref):
    ys = []
    for i in range(4):
        ch, st = _coarse(i)
        ys.append(_dot(_tiles_cat(sr3, i), ccr_ref[st, ch]) - _dot(_tiles_cat(si3, i), cci_ref[st, ch]))
    y = jnp.concatenate(ys, axis=1) + dsk_ref[...] * u
    yg = _gelu(y)
    gl = jnp.concatenate([_dot(yg[:, _coarse(i)[0]], gw_ref[_coarse(i)[0], _coarse(i)[0]]) for i in range(4)],
                         axis=1)
    return y, yg, _sigmoid(gl + gb_ref[...])


def _s5_fwd2(p, abar_re, abar_im, bbr, bbi, ccr, cci, dsk, gw, gb):
    L = p.shape[0]
    T = _s5_chunk(L)

    def body(u_ref, are_ref, aim_ref, bbr_ref, bbi_ref, ccr_ref, cci_ref, dsk_ref, gw_ref, gb_ref,
             o_ref, sr3, si3, cr_ref, ci_ref):
        @pl.when(pl.program_id(0) == 0)
        def _():
            cr_ref[...] = jnp.zeros_like(cr_ref)
            ci_ref[...] = jnp.zeros_like(ci_ref)

        u = u_ref[...]
        for i in range(4):
            ch, st = _coarse(i)
            br = _dot(u[:, ch], bbr_ref[ch, st])
            bi = _dot(u[:, ch], bbi_ref[ch, st])
            for q in range(4):
                _put_rows(sr3, 4 * i + q, br[:, 128 * q:128 * (q + 1)])
                _put_rows(si3, 4 * i + q, bi[:, 128 * q:128 * (q + 1)])
        _seg_scan(sr3, si3, are_ref[...], aim_ref[...], cr_ref, ci_ref, reverse=False)
        _, yg, gate = _s5_gate2(sr3, si3, u, ccr_ref, cci_ref, dsk_ref, gw_ref, gb_ref)
        o_ref[...] = yg * gate

    full = lambda shape: pl.BlockSpec(shape, lambda i: (0, 0))
    rows = 8 * _seg_pitch(T)
    st3 = pl.BlockSpec((NTILE, rows, 128), lambda i: (0, i, 0))
    return pl.pallas_call(
        body, name="s5_fwd", grid=(L // T,),
        in_specs=[pl.BlockSpec((T, SSMW), lambda i: (i, 3)), full((1, NS)), full((1, NS)),
                  full((SSMW, NS)), full((SSMW, NS)), full((NS, SSMW)), full((NS, SSMW)),
                  full((1, SSMW)), full((SSMW, SSMW)), full((1, SSMW))],
        out_specs=[pl.BlockSpec((T, SSMW), lambda i: (i, 0)), st3, st3],
        out_shape=[jax.ShapeDtypeStruct((L, SSMW), f32)]
                  + [jax.ShapeDtypeStruct((NTILE, (L // T) * rows, 128), f32)] * 2,
        scratch_shapes=[pltpu.VMEM((1, NS), f32), pltpu.VMEM((1, NS), f32)],
        compiler_params=_cp(("arbitrary",)),
    )(p, abar_re, abar_im, bbr, bbi, ccr, cci, dsk, gw, gb)


def _s5_bwd2(dout, p, s_re, s_im, abar_re, abar_im, bbr, bbi, ccr, cci, dsk, gw, gb):
    L = p.shape[0]
    T = _s5_chunk(L)
    nchunk = L // T

    def body(do_ref, u_ref, sr3, si3, are_ref, aim_ref, bbr_ref, bbi_ref, ccr_ref, cci_ref,
             dsk_ref, gw_ref, gb_ref,
             du_ref, dare_ref, daim_ref, dd_ref, dgb_ref, dccr_ref, dcci_ref, dbbr_ref, dbbi_ref, dgw_ref,
             lr3, li3, cr_ref, ci_ref, accr3, acci3, wacc_ref, gacc_ref):
        i = pl.program_id(0)

        @pl.when(i == 0)
        def _():
            for ref in (cr_ref, ci_ref, accr3, acci3, dd_ref, dgb_ref, wacc_ref, gacc_ref):
                ref[...] = jnp.zeros_like(ref)

        u = u_ref[...]
        dov = do_ref[...]
        y, yg, gate = _s5_gate2(sr3, si3, u, ccr_ref, cci_ref, dsk_ref, gw_ref, gb_ref)
        dgl = dov * yg * gate * (1.0 - gate)
        dyg = dov * gate + jnp.concatenate(
            [_dot(dgl[:, _coarse(q)[0]], gw_ref[_coarse(q)[0], _coarse(q)[0]], NT) for q in range(4)], axis=1)
        dy = dyg * _gelu_grad(y)
        u16, dy16, yg16, dgl16 = (t.astype(_MXU) for t in (u, dy, yg, dgl))
        dd_ref[...] += jnp.sum(dy * u, axis=0, keepdims=True)
        dgb_ref[...] += jnp.sum(dgl, axis=0, keepdims=True)
        for q in range(4):
            ch, st = _coarse(q)
            wacc_ref[0, q] += _dot(_tiles_cat(sr3, q), dy16[:, ch], TN)
            wacc_ref[1, q] += _dot(_tiles_cat(si3, q), dy16[:, ch], TN)
            gacc_ref[q] += _dot(yg16[:, ch], dgl16[:, ch], TN)
            gr = _dot(dy16[:, ch], ccr_ref[st, ch], NT)
            gi = -_dot(dy16[:, ch], cci_ref[st, ch], NT)
            for t in range(4):
                _put_rows(lr3, 4 * q + t, gr[:, 128 * t:128 * (t + 1)])
                _put_rows(li3, 4 * q + t, gi[:, 128 * t:128 * (t + 1)])

        def on_step(j, rows, nxt_r, nxt_i):
            s_r = sr3.at[j][rows, :]
            s_i = si3.at[j][rows, :]
            accr3[j] += nxt_r * s_r + nxt_i * s_i
            acci3[j] += nxt_i * s_r - nxt_r * s_i

        _seg_scan(lr3, li3, are_ref[...], aim_ref[...], cr_ref, ci_ref, reverse=True, on_step=on_step)
        dus = []
        for q in range(4):
            ch, st = _coarse(q)
            lam_r, lam_i = _tiles_cat(lr3, q), _tiles_cat(li3, q)
            wacc_ref[2, q] += _dot(lam_r, u16[:, ch], TN)
            wacc_ref[3, q] += _dot(lam_i, u16[:, ch], TN)
            dus.append(_dot(lam_r, bbr_ref[ch, st], NT) + _dot(lam_i, bbi_ref[ch, st], NT))
        du_ref[...] = dy * dsk_ref[...] + jnp.concatenate(dus, axis=1)

        @pl.when(i == nchunk - 1)
        def _():
            for j in range(NTILE):
                ls = slice(128 * j, 128 * (j + 1))
                dare_ref[:, ls] = jnp.sum(accr3[j], axis=0, keepdims=True)
                daim_ref[:, ls] = jnp.sum(acci3[j], axis=0, keepdims=True)
            for m, o_ref in enumerate((dccr_ref, dcci_ref, dbbr_ref, dbbi_ref)):
                for q in range(4):
                    o_ref[512 * q:512 * (q + 1), :] = _diag_fold(wacc_ref[m, q], NP)
            for q in range(4):
                dgw_ref[128 * q:128 * (q + 1), :] = _diag_fold(gacc_ref[q], NH16)

    rev = lambda i: (nchunk - 1 - i, 0)
    full = lambda shape: pl.BlockSpec(shape, lambda i: (0, 0))
    half = pl.BlockSpec((T, SSMW), rev)
    rows = 8 * _seg_pitch(T)
    st3 = pl.BlockSpec((NTILE, rows, 128), lambda i: (0, nchunk - 1 - i, 0))
    return pl.pallas_call(
        body, name="s5_bwd", grid=(nchunk,),
        in_specs=[half, pl.BlockSpec((T, SSMW), lambda i: (nchunk - 1 - i, 3)), st3, st3,
                  full((1, NS)), full((1, NS)), full((SSMW, NS)), full((SSMW, NS)),
                  full((NS, SSMW)), full((NS, SSMW)), full((1, SSMW)), full((SSMW, SSMW)), full((1, SSMW))],
        out_specs=[half, full((1, NS)), full((1, NS)), full((1, SSMW)), full((1, SSMW)),
                   full((NS, NH16)), full((NS, NH16)), full((NS, NH16)), full((NS, NH16)),
                   full((SSMW, NH16))],
        out_shape=[jax.ShapeDtypeStruct((L, SSMW), f32)] + [jax.ShapeDtypeStruct((1, NS), f32)] * 2
                  + [jax.ShapeDtypeStruct((1, SSMW), f32)] * 2 + [jax.ShapeDtypeStruct((NS, NH16), f32)] * 4
                  + [jax.ShapeDtypeStruct((SSMW, NH16), f32)],
        scratch_shapes=[pltpu.VMEM((NTILE, rows, 128), f32), pltpu.VMEM((NTILE, rows, 128), f32),
                        pltpu.VMEM((1, NS), f32), pltpu.VMEM((1, NS), f32),
                        pltpu.VMEM((NTILE, 8, 128), f32), pltpu.VMEM((NTILE, 8, 128), f32),
                        pltpu.VMEM((4, 4, 512, 128), f32), pltpu.VMEM((4, 128, 128), f32)],
        compiler_params=_cp(("arbitrary",)),
    )(dout, p, s_re, s_im, abar_re, abar_im, bbr, bbi, ccr, cci, dsk, gw, gb)


def _block_diag(x):
    g, r, c = x.shape
    eye = jnp.eye(g, dtype=x.dtype)
    return (x[:, :, None, :] * eye[:, None, :, None]).reshape(g * r, g * c)


def _mod_fwd(c16, ada_w, ada_b_loc):
    nloc = ada_w.shape[2]

    def body(c_ref, w_ref, b_ref, o_ref):
        cv = c_ref[...]
        act = cv * _sigmoid(cv)
        o_ref[...] = _dot(act, w_ref[...]) + b_ref[...]

    return pl.pallas_call(
        body, name="mod_fwd", grid=(DEPTH,),
        in_specs=[pl.BlockSpec((16, D), lambda l: (0, 0)), pl.BlockSpec((None, D, nloc), lambda l: (l, 0, 0)),
                  pl.BlockSpec((None, 1, nloc), lambda l: (l, 0, 0))],
        out_specs=pl.BlockSpec((None, 16, nloc), lambda l: (l, 0, 0)),
        out_shape=jax.ShapeDtypeStruct((DEPTH, 16, nloc), f32),
        compiler_params=_cp(("parallel",)),
    )(c16, ada_w, ada_b_loc)


def _ada_grad(c16, dmod16):
    nloc = dmod16.shape[2]

    def body(c_ref, d_ref, o_ref):
        cv = c_ref[...]
        act = cv * _sigmoid(cv)
        o_ref[...] = _dot(act, d_ref[...], TN)

    return pl.pallas_call(
        body, name="ada_grad", grid=(DEPTH,),
        in_specs=[pl.BlockSpec((16, D), lambda l: (0, 0)), pl.BlockSpec((None, 16, nloc), lambda l: (l, 0, 0))],
        out_specs=pl.BlockSpec((None, D, nloc), lambda l: (l, 0, 0)),
        out_shape=jax.ShapeDtypeStruct((DEPTH, D, nloc), f32),
        compiler_params=_cp(("parallel",)),
    )(c16, dmod16)


def _as2d(a):
    return a.reshape(-1, a.shape[-1])


def _ew_rows(rows):
    for cand in (512, 256, 128, 64, 32, 16, 8):
        if rows % cand == 0:
            return cand
    return rows


def _cast_bf16(w):
    w2 = _as2d(w)
    rows, cols = w2.shape
    bm = _ew_rows(rows)

    def body(x_ref, o_ref):
        o_ref[...] = x_ref[...].astype(bf16)

    spec = pl.BlockSpec((bm, cols), lambda i: (i, 0))
    out = pl.pallas_call(body, name="cast_bf16", grid=(rows // bm,), in_specs=[spec], out_specs=spec,
                         out_shape=jax.ShapeDtypeStruct((rows, cols), bf16),
                         compiler_params=_cp(("parallel",)))(w2)
    return out.reshape(w.shape)


def _add2(a, b, out_dtype=f32):
    a2, b2 = _as2d(a), _as2d(b)
    rows, cols = a2.shape
    bm = _ew_rows(rows)

    def body(a_ref, b_ref, o_ref):
        o_ref[...] = (a_ref[...] + b_ref[...]).astype(o_ref.dtype)

    spec = pl.BlockSpec((bm, cols), lambda i: (i, 0))
    out = pl.pallas_call(body, name="add2", grid=(rows // bm,), in_specs=[spec, spec], out_specs=spec,
                         out_shape=jax.ShapeDtypeStruct((rows, cols), out_dtype),
                         compiler_params=_cp(("parallel",)))(a2, b2)
    return out.reshape(a.shape)


def _adamw(parts, w, m, v):
    n = parts.shape[0]
    w2, m2, v2 = _as2d(w), _as2d(m), _as2d(v)
    rows, cols = w2.shape
    p3 = parts.reshape(n, rows, cols)
    bm = _ew_rows(rows)
    if bm * cols * 4 > (1 << 20) and bm % 16 == 0:
        bm //= 2

    def body(p_ref, w_ref, m_ref, v_ref, g_ref, d_ref, nm_ref, nv_ref):
        g = p_ref[0]
        for k in range(1, n):
            g = g + p_ref[k]
        mn = ADAM_B1 * m_ref[...] + (1.0 - ADAM_B1) * g
        vn = ADAM_B2 * v_ref[...] + (1.0 - ADAM_B2) * (g * g)
        m_hat = mn / (1.0 - ADAM_B1 ** ADAM_STEP)
        v_hat = vn / (1.0 - ADAM_B2 ** ADAM_STEP)
        g_ref[...] = g
        d_ref[...] = -ADAM_LR * (m_hat / (jnp.sqrt(v_hat) + ADAM_EPS) + ADAM_WD * w_ref[...])
        nm_ref[...] = mn
        nv_ref[...] = vn

    spec = pl.BlockSpec((bm, cols), lambda i: (i, 0))
    outs = pl.pallas_call(
        body, name="adamw", grid=(rows // bm,),
        in_specs=[pl.BlockSpec((n, bm, cols), lambda i: (0, i, 0)), spec, spec, spec],
        out_specs=[spec] * 4, out_shape=[jax.ShapeDtypeStruct((rows, cols), f32)] * 4,
        compiler_params=_cp(("parallel",)),
    )(p3, w2, m2, v2)
    return tuple(o.reshape(w.shape) for o in outs)


ANY = pl.BlockSpec(memory_space=pl.ANY)


def _my_pos():
    return lax.axis_index("x"), lax.axis_index("y"), lax.axis_index("c")


def _allgather8(name, x):
    def body(x_ref, out_ref, send_sems, recv_sems):
        mx, my, mc = _my_pos()
        me, sibling = (mx, my, mc), (mx, my, 1 - mc)
        chips = [(1 - mx, my), (mx, 1 - my), (1 - mx, 1 - my)]

        def slot(px, py, pc):
            return out_ref.at[4 * px + 2 * py + pc]

        def copy(k, block, to, src=None):
            return pltpu.make_async_remote_copy(
                src_ref=slot(*block) if src is None else src, dst_ref=slot(*block),
                send_sem=send_sems.at[k], recv_sem=recv_sems.at[k], device_id=to, device_id_type=MESH)

        first = [copy(0, me, sibling, src=x_ref)]
        first += [copy(1 + j, me, (*chip, mc), src=x_ref) for j, chip in enumerate(chips)]
        for cp in first:
            cp.start()
        passed = [copy(4 + j, (*chip, mc), sibling) for j, chip in enumerate(chips)]
        for j, chip in enumerate(chips):
            copy(1 + j, (*chip, mc), me).wait_recv()
            passed[j].start()
        copy(0, sibling, me).wait_recv()
        for j, chip in enumerate(chips):
            copy(4 + j, (*chip, 1 - mc), me).wait_recv()
        for cp in first + passed:
            cp.wait_send()

    out = pl.pallas_call(
        body, name=name, in_specs=[ANY], out_specs=ANY,
        out_shape=jax.ShapeDtypeStruct((NDEV,) + x.shape, x.dtype),
        scratch_shapes=[pltpu.SemaphoreType.DMA((7,)), pltpu.SemaphoreType.DMA((7,))],
    )(x)
    dev = 4 * lax.axis_index("x") + 2 * lax.axis_index("y") + lax.axis_index("c")
    return lax.dynamic_update_index_in_dim(out, x, dev, 0)


HALF = DEPTH // 2


def _chip_gather(arrs):
    n = len(arrs)

    def body(*refs):
        ins, outs = refs[:n], refs[n:2 * n]
        send_sems, recv_sems = refs[2 * n:]
        mx, my, mc = _my_pos()
        chips = [(1 - mx, my), (mx, 1 - my), (1 - mx, 1 - my)]
        mine = 2 * mx + my
        my_half = pl.ds(HALF * mc, HALF)
        sib_half = pl.ds(HALF * (1 - mc), HALF)

        def ici(a, j, src_chip, half):
            px, py = chips[j]
            return pltpu.make_async_remote_copy(
                src_ref=ins[a].at[half], dst_ref=outs[a].at[src_chip, half],
                send_sem=send_sems.at[3 * a + j], recv_sem=recv_sems.at[3 * a + j],
                device_id=(px, py, mc), device_id_type=MESH)

        def d2d(a, j, half):
            px, py = chips[j]
            blk = outs[a].at[2 * px + py, half]
            return pltpu.make_async_remote_copy(
                src_ref=blk, dst_ref=blk,
                send_sem=send_sems.at[3 * n + 3 * a + j], recv_sem=recv_sems.at[3 * n + 3 * a + j],
                device_id=(mx, my, 1 - mc), device_id_type=MESH)

        sends = [ici(a, j, mine, my_half) for a in range(n) for j in range(3)]
        for cp in sends:
            cp.start()
        passed = []
        for a in range(n):
            for j, (px, py) in enumerate(chips):
                ici(a, j, 2 * px + py, my_half).wait_recv()
                fwd = d2d(a, j, my_half)
                fwd.start()
                passed.append(fwd)
        for a in range(n):
            for j in range(3):
                d2d(a, j, sib_half).wait_recv()
        for cp in sends + passed:
            cp.wait_send()

    outs = pl.pallas_call(
        body, name="chip_gather", in_specs=[ANY] * n, out_specs=[ANY] * n,
        out_shape=[jax.ShapeDtypeStruct((NCHIP,) + a.shape, a.dtype) for a in arrs],
        scratch_shapes=[pltpu.SemaphoreType.DMA((6 * n,)), pltpu.SemaphoreType.DMA((6 * n,))],
    )(*arrs)
    mine = 2 * lax.axis_index("x") + lax.axis_index("y")
    return [lax.dynamic_update_index_in_dim(o, a, mine, 0) for o, a in zip(outs, arrs)]


def _sibling_half_swap(arrs):
    n = len(arrs)

    def body(*refs):
        ins, outs = refs[:n], refs[n:2 * n]
        send_sems, recv_sems = refs[2 * n:]
        mx, my, mc = _my_pos()
        sib_half = pl.ds(HALF * (1 - mc), HALF)
        cps = [pltpu.make_async_remote_copy(
            src_ref=ins[a].at[j, sib_half], dst_ref=outs[a].at[j],
            send_sem=send_sems.at[NCHIP * a + j], recv_sem=recv_sems.at[NCHIP * a + j],
            device_id=(mx, my, 1 - mc), device_id_type=MESH) for a in range(n) for j in range(NCHIP)]
        for cp in cps:
            cp.start()
        for cp in cps:
            cp.wait()

    return pl.pallas_call(
        body, name="sibling_half_swap", in_specs=[ANY] * n, out_specs=[ANY] * n,
        out_shape=[jax.ShapeDtypeStruct((NCHIP, HALF) + a.shape[2:], a.dtype) for a in arrs],
        scratch_shapes=[pltpu.SemaphoreType.DMA((NCHIP * n,)), pltpu.SemaphoreType.DMA((NCHIP * n,))],
    )(*arrs)


def _sibling_merge_halves(arrs):
    n = len(arrs)

    def body(*refs):
        ins, outs = refs[:n], refs[n:2 * n]
        send_sems, recv_sems = refs[2 * n:]
        mx, my, mc = _my_pos()
        my_half = pl.ds(HALF * mc, HALF)
        cps = [pltpu.make_async_remote_copy(
            src_ref=ins[a], dst_ref=outs[a].at[my_half],
            send_sem=send_sems.at[a], recv_sem=recv_sems.at[a],
            device_id=(mx, my, 1 - mc), device_id_type=MESH) for a in range(n)]
        for cp in cps:
            cp.start()
        for cp in cps:
            cp.wait()

    outs = pl.pallas_call(
        body, name="sibling_merge_halves", in_specs=[ANY] * n, out_specs=[ANY] * n,
        out_shape=[jax.ShapeDtypeStruct((DEPTH,) + a.shape[1:], a.dtype) for a in arrs],
        scratch_shapes=[pltpu.SemaphoreType.DMA((n,)), pltpu.SemaphoreType.DMA((n,))],
    )(*arrs)
    start = HALF * lax.axis_index("c")
    return [lax.dynamic_update_slice_in_dim(o, a, start, 0) for o, a in zip(outs, arrs)]


def _sum_parts(parts):
    n = parts.shape[0]
    p3 = parts.reshape(n, -1, parts.shape[-1])
    rows, cols = p3.shape[1:]
    bm = _ew_rows(rows)
    if bm * cols * 4 > (1 << 20) and bm % 16 == 0:
        bm //= 2

    def body(p_ref, o_ref):
        g = p_ref[0].astype(f32)
        for k in range(1, n):
            g = g + p_ref[k].astype(f32)
        o_ref[...] = g

    out = pl.pallas_call(
        body, name="sum_parts", grid=(rows // bm,),
        in_specs=[pl.BlockSpec((n, bm, cols), lambda i: (0, i, 0))],
        out_specs=pl.BlockSpec((bm, cols), lambda i: (i, 0)),
        out_shape=jax.ShapeDtypeStruct((rows, cols), f32),
        compiler_params=_cp(("parallel",)),
    )(p3)
    return out.reshape(parts.shape[1:])


def _chip_scatter(arrs):
    n = len(arrs)

    def body(*refs):
        ins, outs = refs[:n], refs[n:2 * n]
        send_sems, recv_sems = refs[2 * n:]
        mx, my, mc = _my_pos()
        chips = [(1 - mx, my), (mx, 1 - my), (1 - mx, 1 - my)]
        mine = 2 * mx + my
        sends = []
        for a in range(n):
            for j, (px, py) in enumerate(chips):
                sends.append(pltpu.make_async_remote_copy(
                    src_ref=ins[a].at[2 * px + py], dst_ref=outs[a].at[mine],
                    send_sem=send_sems.at[3 * a + j], recv_sem=recv_sems.at[3 * a + j],
                    device_id=(px, py, mc), device_id_type=MESH))
        for cp in sends:
            cp.start()
        for a in range(n):
            for j, (px, py) in enumerate(chips):
                pltpu.make_async_remote_copy(
                    src_ref=ins[a].at[mine], dst_ref=outs[a].at[2 * px + py],
                    send_sem=send_sems.at[3 * a + j], recv_sem=recv_sems.at[3 * a + j],
                    device_id=(px, py, mc), device_id_type=MESH).wait_recv()
        for cp in sends:
            cp.wait_send()

    outs = pl.pallas_call(
        body, name="chip_scatter", in_specs=[ANY] * n, out_specs=[ANY] * n,
        out_shape=[jax.ShapeDtypeStruct(a.shape, a.dtype) for a in arrs],
        scratch_shapes=[pltpu.SemaphoreType.DMA((3 * n,)), pltpu.SemaphoreType.DMA((3 * n,))],
    )(*arrs)
    mine = 2 * lax.axis_index("x") + lax.axis_index("y")
    return [lax.dynamic_update_index_in_dim(o, lax.dynamic_index_in_dim(a, mine, 0, keepdims=False), mine, 0)
            for o, a in zip(outs, arrs)]


def _rc(src, dst, ss, rs, k, to):
    return pltpu.make_async_remote_copy(src_ref=src, dst_ref=dst, send_sem=ss.at[k], recv_sem=rs.at[k],
                                        device_id=to, device_id_type=MESH)


def _half_rows(rows, c):
    return pl.ds((rows // 2) * c, rows // 2)


def _gather_comm(shards):
    n = len(shards)

    def plan(ins, outs, ss, rs):
        mx, my, mc = _my_pos()
        chips = [(1 - mx, my), (mx, 1 - my), (1 - mx, 1 - my)]
        mine = 2 * mx + my
        sends, lands, fwds, fwd_lands = [], [], [], []
        for a in range(n):
            rows = ins[a].shape[0]
            mr, sr = _half_rows(rows, mc), _half_rows(rows, 1 - mc)
            for j, (px, py) in enumerate(chips):
                k, src_chip = 3 * a + j, 2 * px + py
                sends.append(_rc(ins[a].at[mr], outs[a].at[mine, mr], ss, rs, k, (px, py, mc)))
                lands.append(_rc(ins[a].at[mr], outs[a].at[src_chip, mr], ss, rs, k, (px, py, mc)))
                blk, sblk = outs[a].at[src_chip, mr], outs[a].at[src_chip, sr]
                fwds.append(_rc(blk, blk, ss, rs, 3 * n + k, (mx, my, 1 - mc)))
                fwd_lands.append(_rc(sblk, sblk, ss, rs, 3 * n + k, (mx, my, 1 - mc)))
        return sends, lands, fwds, fwd_lands

    def start(ins, outs, ss, rs):
        for cp in plan(ins, outs, ss, rs)[0]:
            cp.start()

    def finish(ins, outs, ss, rs):
        sends, lands, fwds, fwd_lands = plan(ins, outs, ss, rs)
        for land, fwd in zip(lands, fwds):
            land.wait_recv()
            fwd.start()
        for cp in fwd_lands:
            cp.wait_recv()
        for cp in sends + fwds:
            cp.wait_send()

    return dict(ins=list(shards), nsem=6 * n, start=start, finish=finish,
                out_shapes=[jax.ShapeDtypeStruct((NCHIP,) + a.shape, a.dtype) for a in shards])


def _gather_done(outs, shards):
    mine = 2 * lax.axis_index("x") + lax.axis_index("y")
    return [lax.dynamic_update_index_in_dim(o, a, mine, 0) for o, a in zip(outs, shards)]


def _swap_comm(grads):
    n = len(grads)

    def plan(ins, outs, ss, rs):
        mx, my, mc = _my_pos()
        return [_rc(ins[a].at[j, _half_rows(ins[a].shape[1], 1 - mc)], outs[a].at[j], ss, rs,
                    NCHIP * a + j, (mx, my, 1 - mc)) for a in range(n) for j in range(NCHIP)]

    def start(ins, outs, ss, rs):
        for cp in plan(ins, outs, ss, rs):
            cp.start()

    def finish(ins, outs, ss, rs):
        for cp in plan(ins, outs, ss, rs):
            cp.wait()

    return dict(ins=list(grads), nsem=NCHIP * n, start=start, finish=finish,
                out_shapes=[jax.ShapeDtypeStruct((NCHIP, a.shape[1] // 2) + a.shape[2:], a.dtype)
                            for a in grads])


def _scatter_comm(sums):
    n = len(sums)

    def plan(ins, outs, ss, rs):
        mx, my, mc = _my_pos()
        chips = [(1 - mx, my), (mx, 1 - my), (1 - mx, 1 - my)]
        mine = 2 * mx + my
        sends, lands = [], []
        for a in range(n):
            for j, (px, py) in enumerate(chips):
                sends.append(_rc(ins[a].at[2 * px + py], outs[a].at[mine], ss, rs, 3 * a + j, (px, py, mc)))
                lands.append(_rc(ins[a].at[mine], outs[a].at[2 * px + py], ss, rs, 3 * a + j, (px, py, mc)))
        return sends, lands

    def start(ins, outs, ss, rs):
        for cp in plan(ins, outs, ss, rs)[0]:
            cp.start()

    def finish(ins, outs, ss, rs):
        sends, lands = plan(ins, outs, ss, rs)
        for cp in lands:
            cp.wait_recv()
        for cp in sends:
            cp.wait_send()

    return dict(ins=list(sums), nsem=3 * n, start=start, finish=finish,
                out_shapes=[jax.ShapeDtypeStruct(a.shape, a.dtype) for a in sums])


def _scatter_done(outs, sums):
    mine = 2 * lax.axis_index("x") + lax.axis_index("y")
    return [lax.dynamic_update_index_in_dim(o, lax.dynamic_index_in_dim(a, mine, 0, keepdims=False), mine, 0)
            for o, a in zip(outs, sums)]


def _merge_comm(halves):
    n = len(halves)

    def plan(ins, outs, ss, rs):
        mx, my, mc = _my_pos()
        return [_rc(ins[a], outs[a].at[_half_rows(outs[a].shape[0], mc)], ss, rs, a, (mx, my, 1 - mc))
                for a in range(n)]

    def start(ins, outs, ss, rs):
        for cp in plan(ins, outs, ss, rs):
            cp.start()

    def finish(ins, outs, ss, rs):
        for cp in plan(ins, outs, ss, rs):
            cp.wait()

    return dict(ins=list(halves), nsem=n, start=start, finish=finish,
                out_shapes=[jax.ShapeDtypeStruct((2 * a.shape[0],) + a.shape[1:], a.dtype) for a in halves])


def _merge_done(outs, halves):
    mc = lax.axis_index("c")
    return [lax.dynamic_update_slice_in_dim(o, a, a.shape[0] * mc, 0) for o, a in zip(outs, halves)]


def _comm_call(name, comm):
    ni, no = len(comm["ins"]), len(comm["out_shapes"])

    def body(*refs):
        ins, outs, (ss, rs) = refs[:ni], refs[ni:ni + no], refs[ni + no:]
        comm["start"](ins, outs, ss, rs)
        comm["finish"](ins, outs, ss, rs)

    return pl.pallas_call(
        body, name=name, in_specs=[ANY] * ni, out_specs=[ANY] * no, out_shape=comm["out_shapes"],
        scratch_shapes=[pltpu.SemaphoreType.DMA((comm["nsem"],)), pltpu.SemaphoreType.DMA((comm["nsem"],))],
    )(*comm["ins"])


def _call(body, comm, *, name, grid, in_specs, out_specs, out_shape, scratch_shapes, sem, args):
    if comm is None:
        outs = pl.pallas_call(body, name=name, grid=grid, in_specs=in_specs, out_specs=out_specs,
                              out_shape=out_shape, scratch_shapes=scratch_shapes,
                              compiler_params=_cp(sem))(*args)
        return outs, None
    ni, no = len(in_specs), len(out_specs)
    ci, co = len(comm["ins"]), len(comm["out_shapes"])
    nscr = len(scratch_shapes)

    def carried(*refs):
        ins, cins = refs[:ni], refs[ni:ni + ci]
        outs, couts = refs[ni + ci:ni + ci + no], refs[ni + ci + no:ni + ci + no + co]
        scr = refs[ni + ci + no + co:]
        ss, rs = scr[nscr], scr[nscr + 1]
        pids = [pl.program_id(ax) for ax in range(len(grid))]
        first = functools.reduce(jnp.logical_and, [p == 0 for p in pids])
        last = functools.reduce(jnp.logical_and, [p == g - 1 for p, g in zip(pids, grid)])

        @pl.when(first)
        def _():
            comm["start"](cins, couts, ss, rs)

        body(*ins, *outs, *scr[:nscr])

        @pl.when(last)
        def _():
            comm["finish"](cins, couts, ss, rs)

    outs = pl.pallas_call(
        carried, name=name + "_x", grid=grid, in_specs=list(in_specs) + [ANY] * ci,
        out_specs=list(out_specs) + [ANY] * co, out_shape=list(out_shape) + comm["out_shapes"],
        scratch_shapes=list(scratch_shapes) + [pltpu.SemaphoreType.DMA((comm["nsem"],)),
                                               pltpu.SemaphoreType.DMA((comm["nsem"],))],
        compiler_params=_cp(("arbitrary",) * len(grid)),
    )(*args, *comm["ins"])
    return outs[:no], outs[no:]


def _layer_fwd(x, res, res_gate, wl, mod, comm=None):
    L = x.shape[0]
    bm = _row_block(L, 1024)
    nb = L // bm
    sh1, sc1, g1, sh2, sc2, g2 = [mod[k * D:(k + 1) * D][None, :] for k in range(NMOD)]
    sv = {}
    xin, h = _rmsmod_fwd(x, res, res_gate, wl["norm1_g"], sh1, sc1)
    if xin is None:
        xin = x
    p = _mm("mm_in", h, wl["w_in"], dims=NN, grid=(nb, 4, 1),
            a_block=(bm, D), a_map=lambda i, j, k: (i, 0),
            b_block=(None, D, 512), b_map=lambda i, j, k: (j, 0, 0),
            out_shape=(L, INW), o_block=(bm, 512), o_map=lambda i, j, k: (i, j))
    qkv = _qknorm_fwd(p, wl["gqk"])
    oa, couts = _attn_fwd(qkv, comm)
    os_, s_re, s_im = _s5_fwd2(p, wl["abar_re"], wl["abar_im"], wl["bbr"], wl["bbi"], wl["ccr"], wl["cci"],
                              wl["dsk"], wl["gw"], wl["gb"])
    o = _outnorm_fwd(oa, os_, wl["attn_out_g"], wl["ssm_out_g"])
    proj = _mm("mm_out", o, wl["w_out"], dims=NN, grid=(nb, 2, 1),
               a_block=(bm, D), a_map=lambda i, j, k: (i, 0),
               b_block=(D, 512), b_map=lambda i, j, k: (0, j),
               out_shape=(L, D), o_block=(bm, 512), o_map=lambda i, j, k: (i, j))
    x1, h2 = _rmsmod_fwd(xin, proj, g1, wl["norm2_g"], sh2, sc2)
    up_pre = _mm("mm_up", h2, wl["w_up"], dims=NN, grid=(nb, 4, 1),
                 a_block=(bm, D), a_map=lambda i, j, k: (i, 0),
                 b_block=(None, D, 1408), b_map=lambda i, j, k: (j, 0, 0),
                 out_shape=(L, DUP), o_block=(bm, 1408), o_map=lambda i, j, k: (i, j))
    a = _convglu_fwd(up_pre, wl["conv_w"], wl["conv_b"])
    down = _mm("mm_down", a, wl["w_down"], dims=NN, grid=(nb, 2, 1),
               a_block=(bm, DFF), a_map=lambda i, j, k: (i, 0),
               b_block=(DFF, 512), b_map=lambda i, j, k: (0, j),
               out_shape=(L, D), o_block=(bm, 512), o_map=lambda i, j, k: (i, j))
    sv.update(xin=xin, h=h, p=p, qkv=qkv, oa=oa, os=os_, s_re=s_re, s_im=s_im, o=o, proj=proj,
              x1=x1, h2=h2, up_pre=up_pre, a=a, down=down,
              sc1=sc1, g1=g1, sc2=sc2, g2=g2)
    return x1, down, g2, sv, couts


def _big_grads(g):
    return [g["w_in"], g["w_out"].reshape(NCHIP, D // NCHIP, D), g["w_up"],
            g["w_down"].reshape(NCHIP, DFF // NCHIP, D)]


def _chip_sums(grads, recv):
    mc = lax.axis_index("c")
    mine = [lax.dynamic_slice_in_dim(a, (a.shape[1] // 2) * mc, a.shape[1] // 2, axis=1) for a in grads]
    return [_add2(a, b, bf16) for a, b in zip(mine, recv)]


def _reduced(parts, sums):
    halves = [_sum_parts(p) for p in _scatter_done(parts, sums)]
    return _merge_done(_comm_call("grad_merge", _merge_comm(halves)), halves)


def _layer_bwd(dx_out, wl, sv, prev=None):
    L = dx_out.shape[0]
    bm = _row_block(L, 1024)
    nb = L // bm
    bk = _row_block(L, 512)
    nk = L // bk
    g = {}
    dd, dg2 = _gate_bwd(dx_out, sv["down"], sv["g2"])
    da = _mm("mm_down_dx", dd, wl["w_down"], dims=NT, grid=(nb, 2, 1),
             a_block=(bm, D), a_map=lambda i, j, k: (i, 0),
             b_block=(1408, D), b_map=lambda i, j, k: (j, 0),
             out_shape=(L, DFF), o_block=(bm, 1408), o_map=lambda i, j, k: (i, j))
    g["w_down"] = _mm("mm_down_dw", sv["a"], dd, dims=TN, grid=(1, 2, nk),
                      a_block=(bk, DFF), a_map=lambda i, j, k: (k, 0),
                      b_block=(bk, 512), b_map=lambda i, j, k: (k, j),
                      out_shape=(DFF, D), o_block=(DFF, 512), o_map=lambda i, j, k: (0, j))
    dup, g["conv_w"], g["conv_b"], recv = _convglu_bwd(da, sv["up_pre"], wl["conv_w"], wl["conv_b"],
                                                       _swap_comm(prev) if prev else None)
    sums = _chip_sums(prev, recv) if prev else None
    dh2 = _mm("mm_up_dx", dup, wl["w_up"], dims=NT, grid=(nb, 1, 4),
              a_block=(None, bm, 1408), a_map=lambda i, j, k: (k // 2, i, k % 2),
              b_block=(None, D, 1408), b_map=lambda i, j, k: (k, 0, 0),
              out_shape=(L, D), o_block=(bm, D), o_map=lambda i, j, k: (i, 0))
    g["w_up"] = _mm("mm_up_dw", sv["h2"], dup, dims=TN, grid=(1, 4, nk),
                    a_block=(bk, D), a_map=lambda i, j, k: (k, 0),
                    b_block=(None, bk, 1408), b_map=lambda i, j, k: (j // 2, k, j % 2),
                    out_shape=(NCHIP, D, 1408), o_block=(None, D, 1408), o_map=lambda i, j, k: (j, 0, 0))
    dx1, dsh2, dsc2, g["norm2_g"] = _rmsmod_bwd(dh2, sv["x1"], wl["norm2_g"], sv["sc2"], dx_out)
    dproj, dg1 = _gate_bwd(dx1, sv["proj"], sv["g1"])
    do = _mm("mm_out_dx", dproj, wl["w_out"], dims=NT, grid=(nb, 2, 1),
             a_block=(bm, D), a_map=lambda i, j, k: (i, 0),
             b_block=(512, D), b_map=lambda i, j, k: (j, 0),
             out_shape=(L, D), o_block=(bm, 512), o_map=lambda i, j, k: (i, j))
    g["w_out"] = _mm("mm_out_dw", sv["o"], dproj, dims=TN, grid=(1, 2, nk),
                     a_block=(bk, D), a_map=lambda i, j, k: (k, 0),
                     b_block=(bk, 512), b_map=lambda i, j, k: (k, j),
                     out_shape=(D, D), o_block=(D, 512), o_map=lambda i, j, k: (0, j))
    doa, dos, g["attn_out_g"], g["ssm_out_g"] = _outnorm_bwd(do, sv["oa"], sv["os"],
                                                             wl["attn_out_g"], wl["ssm_out_g"])
    (dqn, dkn, dv), parts = _attn_bwd(sv["qkv"], sv["oa"], doa, _scatter_comm(sums) if prev else None)
    prev_reduced = _reduced(parts, sums) if prev else None
    (du, d_abar_re, d_abar_im, g["ssm_d"], g["glu_b"], d_ccr, d_cci, d_bbre, d_bbim, d_gw) = _s5_bwd2(
        dos, sv["p"], sv["s_re"], sv["s_im"], wl["abar_re"], wl["abar_im"], wl["bbr"], wl["bbi"],
        wl["ccr"], wl["cci"], wl["dsk"], wl["gw"], wl["gb"])
    g["ssm_c_re"] = d_ccr.reshape(NG, NP, NH16).transpose(0, 2, 1)
    g["ssm_c_im"] = -d_cci.reshape(NG, NP, NH16).transpose(0, 2, 1)
    g["glu_w"] = d_gw.reshape(NG, NH16, NH16)
    d_ldt, d_ar, d_ai, d_br, d_bi = _s5_disc_bwd(
        wl["ldt_col"], wl["ar_col"], wl["ai_col"], wl["br_mat"], wl["bi_mat"],
        d_abar_re.reshape(NS, 1), d_abar_im.reshape(NS, 1), d_bbre, d_bbim)
    g["ssm_log_dt"] = d_ldt[:, 0]
    g["ssm_a_re"] = d_ar.reshape(NG, NP)
    g["ssm_a_im"] = d_ai.reshape(NG, NP)
    g["ssm_b_re"] = d_br.reshape(NG, NP, NH16)
    g["ssm_b_im"] = d_bi.reshape(NG, NP, NH16)
    dp, dgqk = _qknorm_bwd(dqn, dkn, dv, du, sv["p"], wl["gqk"])
    g["q_norm_g"] = dgqk[0, :HD]
    g["k_norm_g"] = dgqk[0, HD:2 * HD]
    dh = _mm("mm_in_dx", dp, wl["w_in"], dims=NT, grid=(nb, 1, 4),
             a_block=(bm, 512), a_map=lambda i, j, k: (i, k),
             b_block=(None, D, 512), b_map=lambda i, j, k: (k, 0, 0),
             out_shape=(L, D), o_block=(bm, D), o_map=lambda i, j, k: (i, 0))
    g["w_in"] = _mm("mm_in_dw", sv["h"], dp, dims=TN, grid=(1, 4, nk),
                    a_block=(bk, D), a_map=lambda i, j, k: (k, 0),
                    b_block=(bk, 512), b_map=lambda i, j, k: (k, j),
                    out_shape=(NCHIP, D, 512), o_block=(None, D, 512), o_map=lambda i, j, k: (j, 0, 0))
    dx_in, dsh1, dsc1, g["norm1_g"] = _rmsmod_bwd(dh, sv["xin"], wl["norm1_g"], sv["sc1"], dx1)
    g["dmod"] = jnp.concatenate([dsh1, dsc1, dg1, dsh2, dsc2, dg2], axis=1)[0]
    return dx_in, g, prev_reduced


def _prep_layer(l, conv_w, small):
    wl = {}
    wl["conv_w"] = conv_w[l]
    wl["conv_b"] = small["ffn_conv_b"][l][None, :]
    wl["norm1_g"] = small["norm1_g"][l][None, :]
    wl["norm2_g"] = small["norm2_g"][l][None, :]
    wl["attn_out_g"] = small["attn_out_g"][l][None, :]
    wl["ssm_out_g"] = small["ssm_out_g"][l][None, :]
    wl["gqk"] = jnp.concatenate([jnp.tile(small["q_norm_g"][l], 8), jnp.tile(small["k_norm_g"][l], 8)])[None, :]
    wl["ldt_col"] = jnp.repeat(small["ssm_log_dt"][l], NP)[:, None]
    wl["ar_col"] = small["ssm_a_re"][l].reshape(NS, 1)
    wl["ai_col"] = small["ssm_a_im"][l].reshape(NS, 1)
    wl["br_mat"] = small["ssm_b_re"][l].reshape(NS, NH16)
    wl["bi_mat"] = small["ssm_b_im"][l].reshape(NS, NH16)
    abar_re, abar_im, bb_re, bb_im = _s5_disc_fwd(wl["ldt_col"], wl["ar_col"], wl["ai_col"],
                                                  wl["br_mat"], wl["bi_mat"])
    wl["abar_re"] = abar_re.reshape(1, NS)
    wl["abar_im"] = abar_im.reshape(1, NS)
    wl["bbr"] = _block_diag(bb_re.reshape(NG, NP, NH16).transpose(0, 2, 1)).astype(bf16)
    wl["bbi"] = _block_diag(bb_im.reshape(NG, NP, NH16).transpose(0, 2, 1)).astype(bf16)
    wl["ccr"] = _block_diag(small["ssm_c_re"][l].transpose(0, 2, 1)).astype(bf16)
    wl["cci"] = _block_diag(small["ssm_c_im"][l].transpose(0, 2, 1)).astype(bf16)
    wl["gw"] = _block_diag(small["glu_w"][l]).astype(bf16)
    wl["dsk"] = small["ssm_d"][l].reshape(1, SSMW)
    wl["gb"] = small["glu_b"][l].reshape(1, SSMW)
    return wl


def _local_step(x, target, mods, shards, conv_w, small):
    def set_big(wl, gathered):
        w_in, w_out, w_up, w_down = gathered
        wl.update(w_in=w_in, w_up=w_up, w_out=w_out.reshape(D, D), w_down=w_down.reshape(DFF, D))

    def layer_shards(l):
        return [s[l] for s in shards]

    wls = [_prep_layer(l, conv_w, small) for l in range(DEPTH)]
    first = _gather_comm(layer_shards(0))
    set_big(wls[0], _gather_done(_comm_call("weight_gather", first), first["ins"]))
    saved = []
    cur, res, res_gate = x, None, None
    for l in range(DEPTH):
        nxt = _gather_comm(layer_shards(l + 1)) if l + 1 < DEPTH else None
        cur, res, res_gate, sv, couts = _layer_fwd(cur, res, res_gate, wls[l], mods[l], nxt)
        saved.append(sv)
        if nxt is not None:
            set_big(wls[l + 1], _gather_done(couts, nxt["ins"]))
    loss, dx = _loss_fwd_bwd(cur, res, res_gate, target)
    grads = [None] * DEPTH
    big = [None] * DEPTH
    prev = None
    for l in reversed(range(DEPTH)):
        dx, grads[l], done = _layer_bwd(dx, wls[l], saved[l], prev)
        if prev is not None:
            big[l + 1] = done
        prev = _big_grads(grads[l])
    recv = _comm_call("grad_swap", _swap_comm(prev))
    sums = _chip_sums(prev, recv)
    big[0] = _reduced(_comm_call("grad_scatter", _scatter_comm(sums)), sums)
    return loss, dx, grads, big


BIG = ("w_in", "w_out", "ffn_w_up", "ffn_w_down")
SMALL = ("ada_b", "norm1_g", "q_norm_g", "k_norm_g", "ssm_a_re", "ssm_a_im", "ssm_log_dt",
         "ssm_b_re", "ssm_b_im", "ssm_c_re", "ssm_c_im", "ssm_d", "glu_w", "glu_b",
         "attn_out_g", "ssm_out_g", "norm2_g", "ffn_conv_b")
NAMES = ("ada_w", "ada_b", "norm1_g", "w_in", "q_norm_g", "k_norm_g", "ssm_a_re", "ssm_a_im",
         "ssm_log_dt", "ssm_b_re", "ssm_b_im", "ssm_c_re", "ssm_c_im", "ssm_d", "glu_w", "glu_b",
         "attn_out_g", "ssm_out_g", "w_out", "norm2_g", "ffn_w_up", "ffn_conv_w", "ffn_conv_b",
         "ffn_w_down")
PACK_COLS = 1024


def _pack(arrs):
    flat = jnp.concatenate([a.reshape(-1) for a in arrs])
    rows = -(-flat.shape[0] // PACK_COLS)
    rows = -(-rows // 8) * 8
    flat = jnp.pad(flat, (0, rows * PACK_COLS - flat.shape[0]))
    return flat.reshape(rows, PACK_COLS)


def _unpack(packed, shapes):
    flat = packed.reshape(-1)
    out, off = [], 0
    for s in shapes:
        n = math.prod(s)
        out.append(flat[off:off + n].reshape(s))
        off += n
    return out


def kernel(x, c, ada_w, ada_b, norm1_g, w_in, q_norm_g, k_norm_g, ssm_a_re, ssm_a_im, ssm_log_dt, ssm_b_re, ssm_b_im, ssm_c_re, ssm_c_im, ssm_d, glu_w, glu_b, attn_out_g, ssm_out_g, w_out, norm2_g, ffn_w_up, ffn_conv_w, ffn_conv_b, ffn_w_down, loss_target, m_ada_w, m_ada_b, m_norm1_g, m_w_in, m_q_norm_g, m_k_norm_g, m_ssm_a_re, m_ssm_a_im, m_ssm_log_dt, m_ssm_b_re, m_ssm_b_im, m_ssm_c_re, m_ssm_c_im, m_ssm_d, m_glu_w, m_glu_b, m_attn_out_g, m_ssm_out_g, m_w_out, m_norm2_g, m_ffn_w_up, m_ffn_conv_w, m_ffn_conv_b, m_ffn_w_down, v_ada_w, v_ada_b, v_norm1_g, v_w_in, v_q_norm_g, v_k_norm_g, v_ssm_a_re, v_ssm_a_im, v_ssm_log_dt, v_ssm_b_re, v_ssm_b_im, v_ssm_c_re, v_ssm_c_im, v_ssm_d, v_glu_w, v_glu_b, v_attn_out_g, v_ssm_out_g, v_w_out, v_norm2_g, v_ffn_w_up, v_ffn_conv_w, v_ffn_conv_b, v_ffn_w_down):
    env = dict(locals())
    w = {n: env[n] for n in NAMES}
    m = {n: env["m_" + n] for n in NAMES}
    v = {n: env["v_" + n] for n in NAMES}
    mx, my, mc = _my_pos()
    chip = 2 * mx + my
    dev = 4 * mx + 2 * my + mc
    xl = x[0]
    tl = loss_target[0]

    c_all = _allgather8("gather_c", jnp.pad(c, ((0, 7), (0, 0))))[:, 0, :]
    c16 = jnp.pad(c_all, ((0, 8), (0, 0)))
    nloc = NMOD * D // NCHIP
    ada_b_loc = lax.dynamic_slice(ada_b, (0, chip * nloc), (DEPTH, nloc))[:, None, :]
    mod_loc = _mod_fwd(c16, ada_w, ada_b_loc)
    mod_g = _allgather8("gather_mod", mod_loc.reshape(DEPTH * 16, nloc))
    mod_all = mod_g[0::2].reshape(NCHIP, DEPTH, 16, nloc).transpose(1, 2, 0, 3).reshape(DEPTH, 16, NMOD * D)
    mods = lax.dynamic_index_in_dim(mod_all, dev, axis=1, keepdims=False)

    shards = [_cast_bf16(w[n]) for n in BIG]
    ncw = DUP // NCHIP
    conv_g = _allgather8("gather_conv", jnp.pad(w["ffn_conv_w"].reshape(DEPTH * 3, ncw), ((0, 4), (0, 0))))
    conv_w = conv_g[0::2, :DEPTH * 3].reshape(NCHIP, DEPTH, 3, ncw).transpose(1, 2, 0, 3).reshape(DEPTH, 3, DUP)
    small = {n: w[n] for n in SMALL}

    loss_loc, grad_x, grads, big = _local_step(xl, tl, mods, shards, conv_w, small)
    loss = lax.psum(loss_loc, ("x", "y", "c"))

    outs = {}
    for k, name in enumerate(BIG):
        g_big = jnp.stack([big[l][k] for l in range(DEPTH)])
        outs[name] = _adamw(g_big[None], w[name], m[name], v[name])

    small_names = SMALL[1:] + ("ffn_conv_w_full",)
    small_grads = []
    for n in SMALL[1:]:
        key = {"ffn_conv_b": "conv_b"}.get(n, n)
        small_grads.append(jnp.stack([grads[l][key].reshape(w[n].shape[1:]) for l in range(DEPTH)]))
    small_grads.append(jnp.stack([grads[l]["conv_w"] for l in range(DEPTH)]))
    dmod = jnp.stack([grads[l]["dmod"] for l in range(DEPTH)])
    packed = _pack([dmod] + small_grads)
    allp = _allgather8("gather_small", packed)
    shapes = [dmod.shape] + [a.shape for a in small_grads]
    dmod_all = allp[:, :DEPTH * NMOD, :].reshape(NDEV, DEPTH, NMOD * D)
    dmod_loc = lax.dynamic_slice(dmod_all, (0, 0, chip * nloc), (NDEV, DEPTH, nloc)).transpose(1, 0, 2)
    g_ada = _ada_grad(c16, jnp.pad(dmod_loc, ((0, 0), (0, 8), (0, 0))))
    outs["ada_w"] = _adamw(g_ada[None], w["ada_w"], m["ada_w"], v["ada_w"])
    conv_w_shape = (DEPTH, 3, DUP)
    w_small = _pack([w[n] for n in SMALL] + [jnp.zeros(conv_w_shape, f32)])
    m_small = _pack([m[n] for n in SMALL] + [jnp.zeros(conv_w_shape, f32)])
    v_small = _pack([v[n] for n in SMALL] + [jnp.ones(conv_w_shape, f32)])
    res_small = _adamw(allp, w_small, m_small, v_small)
    unpacked = [_unpack(r, shapes) for r in res_small]
    for i, n in enumerate(SMALL):
        outs[n] = tuple(unpacked[k][i] for k in range(4))
    g_conv_full = unpacked[0][len(SMALL)]
    ncw = DUP // NCHIP
    g_conv = lax.dynamic_slice(g_conv_full, (0, 0, chip * ncw), (DEPTH, 3, ncw))
    outs["ffn_conv_w"] = _adamw(g_conv[None], w["ffn_conv_w"], m["ffn_conv_w"], v["ffn_conv_w"])

    result = [loss, grad_x[None]]
    for k in range(4):
        result += [outs[n][k] for n in NAMES]
    return tuple(result)
```

```python
import functools
import math

import jax
import jax.numpy as jnp
from jax import lax
from jax.experimental import pallas as pl
from jax.experimental.pallas import tpu as pltpu

f32 = jnp.float32
bf16 = jnp.bfloat16
_MXU = jnp.bfloat16

D = 1024
ATT = 512
SSMW = 512
HD = 64
NG = 32
NP = 64
NH16 = 16
NS = NG * NP
INW = 2048
DFF = 2816
DUP = 2 * DFF
NMOD = 6
DEPTH = 4
EPS = 1e-6
NCHIP = 4
NDEV = 8

ADAM_LR = 0.001
ADAM_B1 = 0.9
ADAM_B2 = 0.999
ADAM_EPS = 1e-08
ADAM_WD = 0.01
ADAM_STEP = 10

VMEM_LIMIT = 56 * 1024 * 1024
MESH = pl.DeviceIdType.MESH

NN = (((1,), (0,)), ((), ()))
NT = (((1,), (1,)), ((), ()))
TN = (((0,), (0,)), ((), ()))


def _cp(sem=None):
    if sem is None:
        return pltpu.CompilerParams(vmem_limit_bytes=VMEM_LIMIT)
    return pltpu.CompilerParams(dimension_semantics=sem, vmem_limit_bytes=VMEM_LIMIT)


def _dot(a, b, dims=NN):
    return lax.dot_general(a.astype(_MXU), b.astype(_MXU), dims, preferred_element_type=f32)


def _dot_exact(x, m):
    hi = x.astype(bf16)
    r1 = x - hi.astype(f32)
    mid = r1.astype(bf16)
    lo = (r1 - mid.astype(f32)).astype(bf16)
    d = lambda p: lax.dot_general(p, m, NN, preferred_element_type=f32)
    return d(hi) + d(mid) + d(lo)


def _gelu(x):
    c = math.sqrt(2.0 / math.pi)
    return 0.5 * x * (1.0 + jnp.tanh(c * (x + 0.044715 * (x * x * x))))


def _gelu_grad(x):
    c = math.sqrt(2.0 / math.pi)
    t = jnp.tanh(c * (x + 0.044715 * (x * x * x)))
    return 0.5 * (1.0 + t) + 0.5 * x * (1.0 - t * t) * c * (1.0 + 3.0 * 0.044715 * (x * x))


def _sigmoid(x):
    return 1.0 / (1.0 + jnp.exp(-x))


def _mm(name, a, b, *, dims, grid, a_block, a_map, b_block, b_map, out_shape, o_block, o_map,
        out_dtype=f32):
    nk = grid[2]
    acc_shape = tuple(s for s in o_block if s is not None)

    def body(a_ref, b_ref, o_ref, acc_ref):
        k = pl.program_id(2)
        part = _dot(a_ref[...], b_ref[...], dims)
        if nk == 1:
            o_ref[...] = part.astype(o_ref.dtype)
        else:
            @pl.when(k == 0)
            def _():
                acc_ref[...] = part

            @pl.when(k > 0)
            def _():
                acc_ref[...] += part

            @pl.when(k == nk - 1)
            def _():
                o_ref[...] = acc_ref[...].astype(o_ref.dtype)

    return pl.pallas_call(
        body, name=name, grid=grid,
        in_specs=[pl.BlockSpec(a_block, a_map), pl.BlockSpec(b_block, b_map)],
        out_specs=pl.BlockSpec(o_block, o_map),
        out_shape=jax.ShapeDtypeStruct(out_shape, out_dtype),
        scratch_shapes=[pltpu.VMEM(acc_shape if nk > 1 else (8, 128), f32)],
        compiler_params=_cp(("parallel", "parallel", "arbitrary")),
    )(a, b)


def _row_block(L, want):
    return want if L % want == 0 else L


def _rmsmod_fwd(x, res, gate, g, sh, sc):
    L = x.shape[0]
    bm = _row_block(L, 256)
    with_res = res is not None

    def body(*refs):
        if with_res:
            x_ref, r_ref, gt_ref, g_ref, sh_ref, sc_ref, xo_ref, h_ref = refs
            xin = x_ref[...] + gt_ref[...] * r_ref[...]
            xo_ref[...] = xin
        else:
            x_ref, g_ref, sh_ref, sc_ref, h_ref = refs
            xin = x_ref[...]
        inv = lax.rsqrt(jnp.mean(xin * xin, axis=-1, keepdims=True) + EPS)
        xn = xin * inv * g_ref[...]
        h_ref[...] = (xn * (1.0 + sc_ref[...]) + sh_ref[...]).astype(h_ref.dtype)

    row = pl.BlockSpec((bm, D), lambda i: (i, 0))
    vec = pl.BlockSpec((1, D), lambda i: (0, 0))
    if with_res:
        return pl.pallas_call(
            body, name="rmsmod_res_fwd", grid=(L // bm,),
            in_specs=[row, row, vec, vec, vec, vec], out_specs=[row, row],
            out_shape=[jax.ShapeDtypeStruct((L, D), f32), jax.ShapeDtypeStruct((L, D), bf16)],
            compiler_params=_cp(("parallel",)),
        )(x, res, gate, g, sh, sc)
    h = pl.pallas_call(
        body, name="rmsmod_fwd", grid=(L // bm,),
        in_specs=[row, vec, vec, vec], out_specs=row,
        out_shape=jax.ShapeDtypeStruct((L, D), bf16),
        compiler_params=_cp(("parallel",)),
    )(x, g, sh, sc)
    return None, h


def _rmsmod_bwd(dh, x, g, sc, dres):
    L = x.shape[0]
    bm = _row_block(L, 256)

    def body(dh_ref, x_ref, g_ref, sc_ref, dr_ref, dx_ref, dsh_ref, dsc_ref, dg_ref):
        i = pl.program_id(0)
        xv = x_ref[...]
        dhv = dh_ref[...]
        inv = lax.rsqrt(jnp.mean(xv * xv, axis=-1, keepdims=True) + EPS)
        xh = xv * inv
        gv = g_ref[...]
        xn = xh * gv
        dxn = dhv * (1.0 + sc_ref[...])
        dxh = dxn * gv
        dx_ref[...] = inv * (dxh - xh * jnp.mean(dxh * xh, axis=-1, keepdims=True)) + dr_ref[...]
        p_sh = jnp.sum(dhv, axis=0, keepdims=True)
        p_sc = jnp.sum(dhv * xn, axis=0, keepdims=True)
        p_g = jnp.sum(dxn * xh, axis=0, keepdims=True)

        @pl.when(i == 0)
        def _():
            dsh_ref[...] = p_sh
            dsc_ref[...] = p_sc
            dg_ref[...] = p_g

        @pl.when(i > 0)
        def _():
            dsh_ref[...] += p_sh
            dsc_ref[...] += p_sc
            dg_ref[...] += p_g

    row = pl.BlockSpec((bm, D), lambda i: (i, 0))
    vec = pl.BlockSpec((1, D), lambda i: (0, 0))
    return pl.pallas_call(
        body, name="rmsmod_bwd", grid=(L // bm,),
        in_specs=[row, row, vec, vec, row], out_specs=[row, vec, vec, vec],
        out_shape=[jax.ShapeDtypeStruct((L, D), f32)] + [jax.ShapeDtypeStruct((1, D), f32)] * 3,
        compiler_params=_cp(("arbitrary",)),
    )(dh, x, g, sc, dres)


def _gate_bwd(dx, y, gate):
    L = dx.shape[0]
    bm = _row_block(L, 256)

    def body(dx_ref, y_ref, gt_ref, dy_ref, dgt_ref):
        i = pl.program_id(0)
        dxv = dx_ref[...]
        dy_ref[...] = (gt_ref[...] * dxv).astype(dy_ref.dtype)
        part = jnp.sum(dxv * y_ref[...], axis=0, keepdims=True)

        @pl.when(i == 0)
        def _():
            dgt_ref[...] = part

        @pl.when(i > 0)
        def _():
            dgt_ref[...] += part

    row = pl.BlockSpec((bm, D), lambda i: (i, 0))
    vec = pl.BlockSpec((1, D), lambda i: (0, 0))
    return pl.pallas_call(
        body, name="gate_bwd", grid=(L // bm,),
        in_specs=[row, row, vec], out_specs=[row, vec],
        out_shape=[jax.ShapeDtypeStruct((L, D), bf16), jax.ShapeDtypeStruct((1, D), f32)],
        compiler_params=_cp(("arbitrary",)),
    )(dx, y, gate)


def _loss_fwd_bwd(x1, down, gate, target):
    L = x1.shape[0]
    bm = _row_block(L, 256)
    nsteps = L // bm

    def body(x_ref, d_ref, gt_ref, t_ref, dy_ref, loss_ref, acc_ref):
        i = pl.program_id(0)
        diff = x_ref[...] + gt_ref[...] * d_ref[...] - t_ref[...]
        dy_ref[...] = diff * (1.0 / D)
        part = jnp.sum(diff * diff, axis=0, keepdims=True)

        @pl.when(i == 0)
        def _():
            acc_ref[...] = part

        @pl.when(i > 0)
        def _():
            acc_ref[...] += part

        @pl.when(i == nsteps - 1)
        def _():
            tot = jnp.sum(acc_ref[...], axis=1, keepdims=True) * (0.5 / D)
            loss_ref[...] = jnp.broadcast_to(tot, (8, 128))

    row = pl.BlockSpec((bm, D), lambda i: (i, 0))
    vec = pl.BlockSpec((1, D), lambda i: (0, 0))
    dy, loss = pl.pallas_call(
        body, name="loss_fwd_bwd", grid=(nsteps,),
        in_specs=[row, row, vec, row],
        out_specs=[row, pl.BlockSpec((8, 128), lambda i: (0, 0))],
        out_shape=[jax.ShapeDtypeStruct((L, D), f32), jax.ShapeDtypeStruct((8, 128), f32)],
        scratch_shapes=[pltpu.VMEM((1, D), f32)],
        compiler_params=_cp(("arbitrary",)),
    )(x1, down, gate, target)
    return loss[0, 0], dy


def _head_mean_matrix():
    r = lax.broadcasted_iota(jnp.int32, (128, 128), 0) // HD
    c = lax.broadcasted_iota(jnp.int32, (128, 128), 1) // HD
    return jnp.where(r == c, 1.0 / HD, 0.0).astype(bf16)


def _qknorm_fwd(p, gqk):
    L = p.shape[0]
    bm = _row_block(L, 256)

    def body(p_ref, g_ref, o_ref):
        e = _head_mean_matrix()
        for c in range(8):
            sl = slice(128 * c, 128 * (c + 1))
            xv = p_ref[:, sl]
            inv = lax.rsqrt(_dot_exact(xv * xv, e) + EPS)
            yv = xv * inv * g_ref[:, sl]
            if c < 4:
                yv = yv * (1.0 / math.sqrt(HD))
            o_ref[:, sl] = yv.astype(o_ref.dtype)
        o_ref[:, 1024:1536] = p_ref[:, 1024:1536].astype(o_ref.dtype)

    return pl.pallas_call(
        body, name="qknorm_fwd", grid=(L // bm,),
        in_specs=[pl.BlockSpec((bm, 1536), lambda i: (i, 0)), pl.BlockSpec((1, 1024), lambda i: (0, 0))],
        out_specs=pl.BlockSpec((bm, 1536), lambda i: (i, 0)),
        out_shape=jax.ShapeDtypeStruct((L, 1536), bf16),
        compiler_params=_cp(("parallel",)),
    )(p, gqk)


def _qknorm_bwd(dqn, dkn, dval, du, p, gqk):
    L = p.shape[0]
    bm = _row_block(L, 256)
    nsteps = L // bm

    def body(dq_ref, dk_ref, dv_ref, du_ref, p_ref, g_ref, o_ref, dg_ref, acc_ref):
        i = pl.program_id(0)
        e = _head_mean_matrix()
        o_ref[:, 1024:1536] = dv_ref[...].astype(o_ref.dtype)
        o_ref[:, 1536:2048] = du_ref[...].astype(o_ref.dtype)
        for c in range(8):
            sl = slice(128 * c, 128 * (c + 1))
            xv = p_ref[:, sl]
            d_ref = dq_ref if c < 4 else dk_ref
            dv = d_ref[:, 128 * (c % 4):128 * (c % 4 + 1)]
            gv = g_ref[:, sl]
            inv = lax.rsqrt(_dot_exact(xv * xv, e) + EPS)
            xh = xv * inv
            dxh = dv * gv
            o_ref[:, sl] = (inv * (dxh - xh * _dot_exact(dxh * xh, e))).astype(o_ref.dtype)
            part = jnp.sum(dv * xh, axis=0, keepdims=True)

            @pl.when(i == 0)
            def _():
                acc_ref[:, sl] = part

            @pl.when(i > 0)
            def _():
                acc_ref[:, sl] += part

        @pl.when(i == nsteps - 1)
        def _():
            r = lax.broadcasted_iota(jnp.int32, (1024, 128), 0)
            col = lax.broadcasted_iota(jnp.int32, (1024, 128), 1)
            fold = jnp.where(col == (r // 512) * HD + r % HD, 1.0, 0.0).astype(bf16)
            dg_ref[...] = _dot_exact(jnp.broadcast_to(acc_ref[...], (8, 1024)), fold)

    half = pl.BlockSpec((bm, 512), lambda i: (i, 0))
    return pl.pallas_call(
        body, name="qknorm_bwd", grid=(nsteps,),
        in_specs=[half, half, half, half, pl.BlockSpec((bm, 1024), lambda i: (i, 0)),
                  pl.BlockSpec((1, 1024), lambda i: (0, 0))],
        out_specs=[pl.BlockSpec((bm, INW), lambda i: (i, 0)), pl.BlockSpec((8, 128), lambda i: (0, 0))],
        out_shape=[jax.ShapeDtypeStruct((L, INW), bf16), jax.ShapeDtypeStruct((8, 128), f32)],
        scratch_shapes=[pltpu.VMEM((1, 1024), f32)],
        compiler_params=_cp(("arbitrary",)),
    )(dqn, dkn, dval, du, p, gqk)


def _outnorm_fwd(oa, os_, ga, gs):
    L = oa.shape[0]
    bm = _row_block(L, 256)

    def body(a_ref, s_ref, ga_ref, gs_ref, o_ref):
        for x_ref, g_ref, off in ((a_ref, ga_ref, 0), (s_ref, gs_ref, 512)):
            xv = x_ref[...]
            inv = lax.rsqrt(jnp.mean(xv * xv, axis=-1, keepdims=True) + EPS)
            o_ref[:, off:off + 512] = (xv * inv * g_ref[...]).astype(o_ref.dtype)

    half = pl.BlockSpec((bm, 512), lambda i: (i, 0))
    vec = pl.BlockSpec((1, 512), lambda i: (0, 0))
    return pl.pallas_call(
        body, name="outnorm_fwd", grid=(L // bm,),
        in_specs=[half, half, vec, vec], out_specs=pl.BlockSpec((bm, D), lambda i: (i, 0)),
        out_shape=jax.ShapeDtypeStruct((L, D), bf16),
        compiler_params=_cp(("parallel",)),
    )(oa, os_, ga, gs)


def _outnorm_bwd(do, oa, os_, ga, gs):
    L = oa.shape[0]
    bm = _row_block(L, 256)

    def body(do_ref, a_ref, s_ref, ga_ref, gs_ref, da_ref, ds_ref, dga_ref, dgs_ref):
        i = pl.program_id(0)
        for x_ref, g_ref, dx_ref, dg_ref, off in ((a_ref, ga_ref, da_ref, dga_ref, 0),
                                                  (s_ref, gs_ref, ds_ref, dgs_ref, 512)):
            xv = x_ref[...]
            dv = do_ref[:, off:off + 512]
            inv = lax.rsqrt(jnp.mean(xv * xv, axis=-1, keepdims=True) + EPS)
            xh = xv * inv
            dxh = dv * g_ref[...]
            dx_ref[...] = inv * (dxh - xh * jnp.mean(dxh * xh, axis=-1, keepdims=True))
            part = jnp.sum(dv * xh, axis=0, keepdims=True)

            @pl.when(i == 0)
            def _():
                dg_ref[...] = part

            @pl.when(i > 0)
            def _():
                dg_ref[...] += part

    half = pl.BlockSpec((bm, 512), lambda i: (i, 0))
    vec = pl.BlockSpec((1, 512), lambda i: (0, 0))
    return pl.pallas_call(
        body, name="outnorm_bwd", grid=(L // bm,),
        in_specs=[pl.BlockSpec((bm, D), lambda i: (i, 0)), half, half, vec, vec],
        out_specs=[half, half, vec, vec],
        out_shape=[jax.ShapeDtypeStruct((L, 512), f32)] * 2 + [jax.ShapeDtypeStruct((1, 512), f32)] * 2,
        compiler_params=_cp(("arbitrary",)),
    )(do, oa, os_, ga, gs)


NCB = DFF // 128


def _conv_rows(L):
    return 64 if L % 64 == 0 else 32


def _conv_chunk(x_ref, w_ref, b_ref, i, nch, R, L, ahead):
    base = pl.multiple_of(i * R, R)
    prev = x_ref[pl.ds(pl.multiple_of(jnp.maximum(base - 8, 0), 8), 8), :]
    parts = [jnp.where(i > 0, prev, 0.0), x_ref[pl.ds(base, R), :]]
    if ahead:
        nxt = x_ref[pl.ds(pl.multiple_of(jnp.minimum(base + R, L - 8), 8), 8), :]
        parts.append(jnp.where(i < nch - 1, nxt, 0.0))
    xe = jnp.concatenate(parts, axis=0)
    x1 = pltpu.roll(xe, 1, 0)
    x2 = pltpu.roll(xe, 2, 0)
    u = b_ref[...] + x2 * w_ref[0:1, :]
    u = u + x1 * w_ref[1:2, :]
    u = u + xe * w_ref[2:3, :]
    return u[8:], xe[8:], x1[8:], x2[8:]


def _fold8(x):
    out = x[0:8]
    for q in range(1, x.shape[0] // 8):
        out = out + x[8 * q:8 * (q + 1)]
    return out


def _convglu_fwd(up_pre, cw, cb):
    L = up_pre.shape[0]

    def body(v_ref, g_ref, wv_ref, wg_ref, bv_ref, bg_ref, a_ref):
        row = lax.broadcasted_iota(jnp.int32, (L, 128), 0)

        def conv(x_ref, w_ref, b_ref):
            x = x_ref[...]
            out = b_ref[...] + jnp.where(row >= 2, pltpu.roll(x, 2, 0), 0.0) * w_ref[0:1, :]
            out = out + jnp.where(row >= 1, pltpu.roll(x, 1, 0), 0.0) * w_ref[1:2, :]
            return out + x * w_ref[2:3, :]

        val = conv(v_ref, wv_ref, bv_ref)
        a_ref[...] = (_gelu(conv(g_ref, wg_ref, bg_ref)) * val).astype(a_ref.dtype)

    col = lambda off: pl.BlockSpec((L, 128), lambda j: (0, j + off))
    wsp = lambda off: pl.BlockSpec((3, 128), lambda j: (0, j + off))
    bsp = lambda off: pl.BlockSpec((1, 128), lambda j: (0, j + off))
    return pl.pallas_call(
        body, name="convglu_fwd", grid=(NCB,),
        in_specs=[col(0), col(NCB), wsp(0), wsp(NCB), bsp(0), bsp(NCB)],
        out_specs=pl.BlockSpec((L, 128), lambda j: (0, j)),
        out_shape=jax.ShapeDtypeStruct((L, DFF), bf16),
        compiler_params=_cp(("parallel",)),
    )(up_pre, up_pre, cw, cw, cb, cb)


def _convglu_bwd(da, up_pre, cw, cb, comm=None):
    L = up_pre.shape[0]
    R = _conv_rows(L)
    nch = L // R
    n = R + 8

    def body(da_ref, v_ref, g_ref, wv_ref, wg_ref, bv_ref, bg_ref, dx_ref, dw_ref, db_ref):
        def chunk(i, acc):
            base = pl.multiple_of(i * R, R)
            val, xv, xv1, xv2 = _conv_chunk(v_ref, wv_ref, bv_ref, i, nch, R, L, True)
            gate, xg, xg1, xg2 = _conv_chunk(g_ref, wg_ref, bg_ref, i, nch, R, L, True)
            nxt = da_ref[pl.ds(pl.multiple_of(jnp.minimum(base + R, L - 8), 8), 8), :]
            dav = jnp.concatenate([da_ref[pl.ds(base, R), :], jnp.where(i < nch - 1, nxt, 0.0)], axis=0)
            c = math.sqrt(2.0 / math.pi)
            t = jnp.tanh(c * (gate + 0.044715 * (gate * gate * gate)))
            d_val = dav * (0.5 * gate * (1.0 + t))
            d_gate = dav * val * (0.5 * (1.0 + t)
                                  + 0.5 * gate * (1.0 - t * t) * c * (1.0 + 3.0 * 0.044715 * (gate * gate)))
            new = []
            for k, (d, w_ref, xs) in enumerate(((d_val, wv_ref, (xv2, xv1, xv)), (d_gate, wg_ref, (xg2, xg1, xg)))):
                dx = d * w_ref[2:3, :] + pltpu.roll(d, n - 1, 0) * w_ref[1:2, :]
                dx = dx + pltpu.roll(d, n - 2, 0) * w_ref[0:1, :]
                dx_ref[k, pl.ds(base, R), :] = dx[:R].astype(dx_ref.dtype)
                dr = d[:R]
                new += [_fold8(dr * x[:R]) for x in xs] + [_fold8(dr)]
            return tuple(a + b for a, b in zip(acc, new))

        acc = lax.fori_loop(0, nch, chunk, (jnp.zeros((8, 128), f32),) * 8)
        for k in range(2):
            for r in range(3):
                dw_ref[k, r:r + 1, :] = jnp.sum(acc[4 * k + r], axis=0, keepdims=True)
            db_ref[k] = jnp.sum(acc[4 * k + 3], axis=0, keepdims=True)

    col = lambda off: pl.BlockSpec((L, 128), lambda j: (0, j + off))
    wsp = lambda off: pl.BlockSpec((3, 128), lambda j: (0, j + off))
    bsp = lambda off: pl.BlockSpec((1, 128), lambda j: (0, j + off))
    (dx, dw, db), couts = _call(
        body, comm, name="convglu_bwd", grid=(NCB,),
        in_specs=[col(0), col(0), col(NCB), wsp(0), wsp(NCB), bsp(0), bsp(NCB)],
        out_specs=[pl.BlockSpec((2, L, 128), lambda j: (0, 0, j)), pl.BlockSpec((2, 3, 128), lambda j: (0, 0, j)),
                   pl.BlockSpec((2, 1, 128), lambda j: (0, 0, j))],
        out_shape=[jax.ShapeDtypeStruct((2, L, DFF), bf16), jax.ShapeDtypeStruct((2, 3, DFF), f32),
                   jax.ShapeDtypeStruct((2, 1, DFF), f32)], scratch_shapes=[],
        sem=("parallel",), args=(da, up_pre, up_pre, cw, cw, cb, cb))
    return dx, dw.transpose(1, 0, 2).reshape(3, DUP), db.transpose(1, 0, 2).reshape(1, DUP), couts


def _attn_block(L):
    return 256 if L % 256 == 0 else 128


TAIL_DEAD = -110.0


def _dot_exact2(x, m):
    hi = x.astype(bf16)
    lo = (x - hi.astype(f32)).astype(bf16)
    return (lax.dot_general(hi, m, NN, preferred_element_type=f32)
            + lax.dot_general(lo, m, NN, preferred_element_type=f32))


def _sb_weights(z, mask, tri_gt, carry):
    l1p = jnp.log(1.0 + jnp.exp(-jnp.abs(z)))
    ls_pos = jnp.minimum(z, 0.0) - l1p
    lm = ls_pos - z
    if mask is not None:
        lm = jnp.where(mask, lm, 0.0)
    tail = _dot_exact2(lm, tri_gt) + carry
    w = jnp.exp(ls_pos + tail)
    if mask is not None:
        w = jnp.where(mask, w, 0.0)
    return w, ls_pos, lm


def _head_masks(B):
    lane = lax.broadcasted_iota(jnp.int32, (B, 128), 1)
    return lane < HD, lane >= HD


def _attn_fwd(qkv, comm=None):
    L = qkv.shape[0]
    B = _attn_block(L)
    nq = L // B

    def body(q_ref, k_ref, v_ref, o_ref):
        qi = pl.program_id(1)
        heads = _head_masks(B)
        ti = lax.broadcasted_iota(jnp.int32, (B, B), 0)
        si = lax.broadcasted_iota(jnp.int32, (B, B), 1)
        tri_gt = jnp.where(ti > si, 1.0, 0.0).astype(bf16)
        diag = si < ti
        qv = q_ref[...]
        zero16 = jnp.zeros((), bf16)

        def tile(jb, mask, acc, carries):
            ks = pl.multiple_of(jb * B, B)
            kb = k_ref[pl.ds(ks, B), :]
            vb = v_ref[pl.ds(ks, B), :]
            out = []
            for in_head, carry in zip(heads, carries):
                kh = jnp.where(in_head, kb, zero16)
                vh = jnp.where(in_head, vb, zero16)
                z = lax.dot_general(qv, kh, NT, preferred_element_type=f32)
                w, _, lm = _sb_weights(z, mask, tri_gt, carry)
                acc = acc + lax.dot_general(w.astype(bf16), vh, NN, preferred_element_type=f32)
                out.append(carry + jnp.sum(lm, axis=1, keepdims=True))
            return acc, out

        zc = jnp.zeros((B, 1), f32)
        acc, (c0, c1) = tile(qi, diag, jnp.zeros((B, 128), f32), (zc, zc))

        def cond(st):
            return jnp.logical_and(st[0] <= qi, st[4] > TAIL_DEAD)

        def step(st):
            n, a, p0, p1, _ = st
            a, (p0, p1) = tile(qi - n, None, a, (p0, p1))
            return n + 1, a, p0, p1, jnp.maximum(jnp.max(p0), jnp.max(p1))

        st = lax.while_loop(cond, step, (jnp.int32(1), acc, c0, c1, jnp.maximum(jnp.max(c0), jnp.max(c1))))
        o_ref[...] = st[1]

    (o,), couts = _call(
        body, comm, name="attn_fwd", grid=(4, nq),
        in_specs=[pl.BlockSpec((B, 128), lambda hp, i: (i, hp)),
                  pl.BlockSpec((L, 128), lambda hp, i: (0, 4 + hp)),
                  pl.BlockSpec((L, 128), lambda hp, i: (0, 8 + hp))],
        out_specs=[pl.BlockSpec((B, 128), lambda hp, i: (i, hp))],
        out_shape=[jax.ShapeDtypeStruct((L, ATT), f32)], scratch_shapes=[],
        sem=("parallel", "parallel"), args=(qkv, qkv, qkv))
    return o, couts


def _attn_bwd(qkv, o, do, comm=None):
    L = qkv.shape[0]
    B = _attn_block(L)
    nq = L // B

    def body(q_ref, k_ref, v_ref, o_ref, do_ref, dq_ref, dk_ref, dv_ref):
        qi = pl.program_id(1)

        @pl.when(qi == 0)
        def _():
            dk_ref[...] = jnp.zeros_like(dk_ref)
            dv_ref[...] = jnp.zeros_like(dv_ref)

        heads = _head_masks(B)
        ti = lax.broadcasted_iota(jnp.int32, (B, B), 0)
        si = lax.broadcasted_iota(jnp.int32, (B, B), 1)
        tri_gt = jnp.where(ti > si, 1.0, 0.0).astype(bf16)
        tri_ge = jnp.where(ti >= si, 1.0, 0.0).astype(bf16)
        diag = si < ti
        zero16 = jnp.zeros((), bf16)
        qv = q_ref[...]
        do16 = do_ref[...].astype(bf16)
        dsum_lanes = do16.astype(f32) * o_ref[...]
        qhs = [jnp.where(m, qv, zero16) for m in heads]
        dohs = [jnp.where(m, do16, zero16) for m in heads]
        dsums = [jnp.sum(jnp.where(m, dsum_lanes, 0.0), axis=1, keepdims=True) for m in heads]

        def tile(jb, mask, dq_a, state):
            ks = pl.multiple_of(jb * B, B)
            kb = k_ref[pl.ds(ks, B), :]
            vb = v_ref[pl.ds(ks, B), :]
            dk_blk = jnp.zeros((B, 128), f32)
            dv_blk = jnp.zeros((B, 128), f32)
            out = []
            for in_head, qh, doh, dsum, (carry, suffix) in zip(heads, qhs, dohs, dsums, state):
                kh = jnp.where(in_head, kb, zero16)
                vh = jnp.where(in_head, vb, zero16)
                z = lax.dot_general(qv, kh, NT, preferred_element_type=f32)
                w, ls_pos, lm = _sb_weights(z, mask, tri_gt, carry)
                w16 = w.astype(bf16)
                dw = lax.dot_general(doh, vh, NT, preferred_element_type=f32)
                da = w16.astype(f32) * dw
                sig = jnp.exp(ls_pos)
                suf = _dot_exact2(da, tri_ge) + suffix
                dz = da * (1.0 - sig) - sig * (dsum - suf)
                if mask is not None:
                    dz = jnp.where(mask, dz, 0.0)
                dz16 = dz.astype(bf16)
                dq_a = dq_a + lax.dot_general(dz16, kh, NN, preferred_element_type=f32)
                dk_blk = dk_blk + lax.dot_general(dz16, qh, TN, preferred_element_type=f32)
                dv_blk = dv_blk + lax.dot_general(w16, doh, TN, preferred_element_type=f32)
                out.append((carry + jnp.sum(lm, axis=1, keepdims=True),
                            suffix + jnp.sum(da, axis=1, keepdims=True)))
            dk_ref[pl.ds(ks, B), :] += dk_blk
            dv_ref[pl.ds(ks, B), :] += dv_blk
            return dq_a, out

        def alive(state):
            return jnp.maximum(jnp.max(state[0][0]), jnp.max(state[1][0]))

        zc = jnp.zeros((B, 1), f32)
        dq_acc, state = tile(qi, diag, jnp.zeros((B, 128), f32), ((zc, zc), (zc, zc)))

        def cond(st):
            return jnp.logical_and(st[0] <= qi, st[6] > TAIL_DEAD)

        def step(st):
            n, a, c0, s0, c1, s1, _ = st
            a, new = tile(qi - n, None, a, ((c0, s0), (c1, s1)))
            return n + 1, a, new[0][0], new[0][1], new[1][0], new[1][1], alive(new)

        st = lax.while_loop(cond, step, (jnp.int32(1), dq_acc, state[0][0], state[0][1],
                                         state[1][0], state[1][1], alive(state)))
        dq_ref[...] = st[1] * (1.0 / math.sqrt(HD))

    blk = pl.BlockSpec((B, 128), lambda hp, i: (i, hp))
    return _call(
        body, comm, name="attn_bwd", grid=(4, nq),
        in_specs=[blk,
                  pl.BlockSpec((L, 128), lambda hp, i: (0, 4 + hp)),
                  pl.BlockSpec((L, 128), lambda hp, i: (0, 8 + hp)),
                  blk, blk],
        out_specs=[blk, pl.BlockSpec((L, 128), lambda hp, i: (0, hp)),
                   pl.BlockSpec((L, 128), lambda hp, i: (0, hp))],
        out_shape=[jax.ShapeDtypeStruct((L, ATT), f32)] * 3, scratch_shapes=[],
        sem=("parallel", "arbitrary"), args=(qkv, qkv, qkv, o, do))


def _s5_disc(ldt, ar, ai, br, bi):
    dt = jnp.exp(ldt)
    mag = jnp.exp(dt * ar)
    abar_re = mag * jnp.cos(dt * ai)
    abar_im = mag * jnp.sin(dt * ai)
    em_re = abar_re - 1.0
    em_im = abar_im
    den = ar * ar + ai * ai
    f_re = (em_re * ar + em_im * ai) / den
    f_im = (em_im * ar - em_re * ai) / den
    bb_re = f_re * br - f_im * bi
    bb_im = f_re * bi + f_im * br
    return abar_re, abar_im, bb_re, bb_im


def _s5_disc_fwd(ldt, ar, ai, br, bi):
    def body(ldt_ref, ar_ref, ai_ref, br_ref, bi_ref, o1, o2, o3, o4):
        outs = _s5_disc(ldt_ref[...], ar_ref[...], ai_ref[...], br_ref[...], bi_ref[...])
        for o_ref, v in zip((o1, o2, o3, o4), outs):
            o_ref[...] = v

    col = jax.ShapeDtypeStruct((NS, 1), f32)
    mat = jax.ShapeDtypeStruct((NS, NH16), f32)
    return pl.pallas_call(body, name="s5_disc_fwd", out_shape=[col, col, mat, mat],
                          compiler_params=_cp())(ldt, ar, ai, br, bi)


def _s5_disc_bwd(ldt, ar, ai, br, bi, d_are, d_aim, d_bbre, d_bbim):
    def body(ldt_ref, ar_ref, ai_ref, br_ref, bi_ref, c1, c2, c3, c4, o_ldt, o_ar, o_ai, o_br, o_bi):
        prim = (ldt_ref[...], ar_ref[...], ai_ref[...], br_ref[...], bi_ref[...])
        _, vjp = jax.vjp(_s5_disc, *prim)
        g_ldt, g_ar, g_ai, g_br, g_bi = vjp((c1[...], c2[...], c3[...], c4[...]))
        o_ar[...] = g_ar
        o_ai[...] = g_ai
        o_br[...] = g_br
        o_bi[...] = g_bi
        r = lax.broadcasted_iota(jnp.int32, (NG, NS), 0)
        c = lax.broadcasted_iota(jnp.int32, (NG, NS), 1)
        fold = jnp.where(c // NP == r, 1.0, 0.0).astype(bf16)
        hi = jnp.broadcast_to(g_ldt, (NS, 128))
        p1 = hi.astype(bf16)
        r1 = hi - p1.astype(f32)
        p2 = r1.astype(bf16)
        p3 = (r1 - p2.astype(f32)).astype(bf16)
        dd = lambda p: lax.dot_general(fold, p, NN, preferred_element_type=f32)
        o_ldt[...] = dd(p1) + dd(p2) + dd(p3)

    col = jax.ShapeDtypeStruct((NS, 1), f32)
    mat = jax.ShapeDtypeStruct((NS, NH16), f32)
    return pl.pallas_call(
        body, name="s5_disc_bwd",
        out_shape=[jax.ShapeDtypeStruct((NG, 128), f32), col, col, mat, mat],
        compiler_params=_cp())(ldt, ar, ai, br, bi, d_are, d_aim, d_bbre, d_bbim)


def _cmul(ar, ai, br, bi):
    return ar * br - ai * bi, ar * bi + ai * br


def _s5_chunk(L):
    return 256 if L % 256 == 0 else L


def _diag_fold(acc, rpg):
    aw = acc.shape[0]
    r = lax.broadcasted_iota(jnp.int32, (aw, 128), 0) // rpg
    c = lax.broadcasted_iota(jnp.int32, (aw, 128), 1) // NH16
    fr = lax.broadcasted_iota(jnp.int32, (128, NH16), 0) % NH16
    fc = lax.broadcasted_iota(jnp.int32, (128, NH16), 1)
    fold = jnp.where(fr == fc, 1.0, 0.0).astype(bf16)
    return _dot_exact(jnp.where(r == c, acc, 0.0), fold)


def _diag_fold_t(acc):
    r = lax.broadcasted_iota(jnp.int32, (128, 8 * NP), 0) // NH16
    c = lax.broadcasted_iota(jnp.int32, (128, 8 * NP), 1) // NP
    fr = lax.broadcasted_iota(jnp.int32, (128, NH16), 0) % NH16
    fc = lax.broadcasted_iota(jnp.int32, (128, NH16), 1)
    fold = jnp.where(fr == fc, 1.0, 0.0).astype(bf16)
    x = jnp.where(r == c, acc, 0.0)
    hi = x.astype(bf16)
    r1 = x - hi.astype(f32)
    mid = r1.astype(bf16)
    lo = (r1 - mid.astype(f32)).astype(bf16)
    d = lambda piece: lax.dot_general(piece, fold, TN, preferred_element_type=f32)
    return d(hi) + d(mid) + d(lo)


NTILE = NS // 128
TILE_GROUP = 4


def _seg_entry(lr, li, ar, ai, cr, ci, seg_rows, reverse, row):
    pr, pi = ar, ai
    for _ in range(seg_rows.bit_length() - 1):
        pr, pi = _cmul(pr, pi, pr, pi)
    if reverse:
        xr = jnp.where(row == 7, cr, pltpu.roll(lr, 7, 0))
        xi = jnp.where(row == 7, ci, pltpu.roll(li, 7, 0))
    else:
        xr = jnp.where(row == 0, cr, pltpu.roll(lr, 1, 0))
        xi = jnp.where(row == 0, ci, pltpu.roll(li, 1, 0))
    for d in (1, 2, 4):
        keep = (row < 8 - d) if reverse else (row >= d)
        shift = 8 - d if reverse else d
        mr, mi = _cmul(pr, pi, jnp.where(keep, pltpu.roll(xr, shift, 0), 0.0),
                       jnp.where(keep, pltpu.roll(xi, shift, 0), 0.0))
        xr, xi = xr + mr, xi + mi
        pr, pi = _cmul(pr, pi, pr, pi)
    return xr, xi


def _seg_scan(xr3, xi3, abar_re, abar_im, cr_ref, ci_ref, reverse, on_step=None):
    pitch = xr3.shape[1] // 8
    seg = pitch - 8
    row = lax.broadcasted_iota(jnp.int32, (8, 128), 0)
    for g0 in range(0, NTILE, TILE_GROUP):
        tiles = list(range(g0, g0 + TILE_GROUP))
        lanes = [slice(128 * j, 128 * (j + 1)) for j in tiles]
        ar = [jnp.broadcast_to(abar_re[:, ls], (8, 128)) for ls in lanes]
        ai = [jnp.broadcast_to(-abar_im[:, ls] if reverse else abar_im[:, ls], (8, 128)) for ls in lanes]

        def sweep(store):
            def step(n, st):
                k = (seg - 1 - n) if reverse else n
                rows = pl.ds(k, 8, stride=pitch)
                new = []
                for q, j in enumerate(tiles):
                    sr, si = st[2 * q], st[2 * q + 1]
                    if store and on_step is not None:
                        on_step(j, rows, sr, si)
                    nr = ar[q] * sr - ai[q] * si + xr3.at[j][rows, :]
                    ni = ar[q] * si + ai[q] * sr + xi3.at[j][rows, :]
                    if store:
                        xr3.at[j][rows, :] = nr
                        xi3.at[j][rows, :] = ni
                    new += [nr, ni]
                return tuple(new)
            return step

        zero = jnp.zeros((8, 128), f32)
        ends = lax.fori_loop(0, seg, sweep(False), (zero,) * (2 * TILE_GROUP))
        init = []
        for q, ls in enumerate(lanes):
            init += _seg_entry(ends[2 * q], ends[2 * q + 1], ar[q], ai[q], cr_ref[:, ls], ci_ref[:, ls],
                               seg, reverse, row)
        fin = lax.fori_loop(0, seg, sweep(True), tuple(init))
        edge = slice(0, 1) if reverse else slice(7, 8)
        for q, ls in enumerate(lanes):
            cr_ref[:, ls] = fin[2 * q][edge, :]
            ci_ref[:, ls] = fin[2 * q + 1][edge, :]


def _coarse(i):
    return slice(128 * i, 128 * (i + 1)), slice(512 * i, 512 * (i + 1))


def _seg_pitch(T):
    return T // 8 + 8


def _put_rows(x3, j, val):
    pitch = x3.shape[1] // 8
    seg = pitch - 8
    for s in range(8):
        x3[j, pitch * s:pitch * s + seg, :] = val[seg * s:seg * (s + 1), :]


def _get_rows(x3, j):
    pitch = x3.shape[1] // 8
    seg = pitch - 8
    return jnp.concatenate([x3[j, pitch * s:pitch * s + seg, :] for s in range(8)], axis=0)


def _tiles_cat(x3, i):
    return jnp.concatenate([_get_rows(x3, 4 * i + q) for q in range(4)], axis=1)


def _s5_gate2(sr3, si3, u, ccr_ref, cci_ref, dsk_ref, gw_ref, gb_ref):
    ys = []
    for i in range(4):
        ch, st = _coarse(i)
        ys.append(_dot(_tiles_cat(sr3, i), ccr_ref[st, ch]) - _dot(_tiles_cat(si3, i), cci_ref[st, ch]))
    y = jnp.concatenate(ys, axis=1) + dsk_ref[...] * u
    yg = _gelu(y)
    gl = jnp.concatenate([_dot(yg[:, _coarse(i)[0]], gw_ref[_coarse(i)[0], _coarse(i)[0]]) for i in range(4)],
                         axis=1)
    return y, yg, _sigmoid(gl + gb_ref[...])


def _s5_fwd2(p, abar_re, abar_im, bbr, bbi, ccr, cci, dsk, gw, gb):
    L = p.shape[0]
    T = _s5_chunk(L)

    def body(u_ref, are_ref, aim_ref, bbr_ref, bbi_ref, ccr_ref, cci_ref, dsk_ref, gw_ref, gb_ref,
             o_ref, sr3, si3, cr_ref, ci_ref):
        @pl.when(pl.program_id(0) == 0)
        def _():
            cr_ref[...] = jnp.zeros_like(cr_ref)
            ci_ref[...] = jnp.zeros_like(ci_ref)

        u = u_ref[...]
        for i in range(4):
            ch, st = _coarse(i)
            br = _dot(u[:, ch], bbr_ref[ch, st])
            bi = _dot(u[:, ch], bbi_ref[ch, st])
            for q in range(4):
                _put_rows(sr3, 4 * i + q, br[:, 128 * q:128 * (q + 1)])
                _put_rows(si3, 4 * i + q, bi[:, 128 * q:128 * (q + 1)])
        _seg_scan(sr3, si3, are_ref[...], aim_ref[...], cr_ref, ci_ref, reverse=False)
        _, yg, gate = _s5_gate2(sr3, si3, u, ccr_ref, cci_ref, dsk_ref, gw_ref, gb_ref)
        o_ref[...] = yg * gate

    full = lambda shape: pl.BlockSpec(shape, lambda i: (0, 0))
    rows = 8 * _seg_pitch(T)
    st3 = pl.BlockSpec((NTILE, rows, 128), lambda i: (0, i, 0))
    return pl.pallas_call(
        body, name="s5_fwd", grid=(L // T,),
        in_specs=[pl.BlockSpec((T, SSMW), lambda i: (i, 3)), full((1, NS)), full((1, NS)),
                  full((SSMW, NS)), full((SSMW, NS)), full((NS, SSMW)), full((NS, SSMW)),
                  full((1, SSMW)), full((SSMW, SSMW)), full((1, SSMW))],
        out_specs=[pl.BlockSpec((T, SSMW), lambda i: (i, 0)), st3, st3],
        out_shape=[jax.ShapeDtypeStruct((L, SSMW), f32)]
                  + [jax.ShapeDtypeStruct((NTILE, (L // T) * rows, 128), f32)] * 2,
        scratch_shapes=[pltpu.VMEM((1, NS), f32), pltpu.VMEM((1, NS), f32)],
        compiler_params=_cp(("arbitrary",)),
    )(p, abar_re, abar_im, bbr, bbi, ccr, cci, dsk, gw, gb)


def _s5_bwd2(dout, p, s_re, s_im, abar_re, abar_im, bbr, bbi, ccr, cci, dsk, gw, gb):
    L = p.shape[0]
    T = _s5_chunk(L)
    nchunk = L // T

    def body(do_ref, u_ref, sr3, si3, are_ref, aim_ref, bbr_ref, bbi_ref, ccr_ref, cci_ref,
             dsk_ref, gw_ref, gb_ref,
             du_ref, dare_ref, daim_ref, dd_ref, dgb_ref, dccr_ref, dcci_ref, dbbr_ref, dbbi_ref, dgw_ref,
             lr3, li3, cr_ref, ci_ref, accr3, acci3, wacc_ref, gacc_ref):
        i = pl.program_id(0)

        @pl.when(i == 0)
        def _():
            for ref in (cr_ref, ci_ref, accr3, acci3, dd_ref, dgb_ref, wacc_ref, gacc_ref):
                ref[...] = jnp.zeros_like(ref)

        u = u_ref[...]
        dov = do_ref[...]
        y, yg, gate = _s5_gate2(sr3, si3, u, ccr_ref, cci_ref, dsk_ref, gw_ref, gb_ref)
        dgl = dov * yg * gate * (1.0 - gate)
        dyg = dov * gate + jnp.concatenate(
            [_dot(dgl[:, _coarse(q)[0]], gw_ref[_coarse(q)[0], _coarse(q)[0]], NT) for q in range(4)], axis=1)
        dy = dyg * _gelu_grad(y)
        u16, dy16, yg16, dgl16 = (t.astype(_MXU) for t in (u, dy, yg, dgl))
        dd_ref[...] += jnp.sum(dy * u, axis=0, keepdims=True)
        dgb_ref[...] += jnp.sum(dgl, axis=0, keepdims=True)
        for q in range(4):
            ch, st = _coarse(q)
            wacc_ref[0, q] += _dot(dy16[:, ch], _tiles_cat(sr3, q), TN)
            wacc_ref[1, q] += _dot(dy16[:, ch], _tiles_cat(si3, q), TN)
            gacc_ref[q] += _dot(yg16[:, ch], dgl16[:, ch], TN)
            gr = _dot(dy16[:, ch], ccr_ref[st, ch], NT)
            gi = -_dot(dy16[:, ch], cci_ref[st, ch], NT)
            for t in range(4):
                _put_rows(lr3, 4 * q + t, gr[:, 128 * t:128 * (t + 1)])
                _put_rows(li3, 4 * q + t, gi[:, 128 * t:128 * (t + 1)])

        def on_step(j, rows, nxt_r, nxt_i):
            s_r = sr3.at[j][rows, :]
            s_i = si3.at[j][rows, :]
            accr3[j] += nxt_r * s_r + nxt_i * s_i
            acci3[j] += nxt_i * s_r - nxt_r * s_i

        _seg_scan(lr3, li3, are_ref[...], aim_ref[...], cr_ref, ci_ref, reverse=True, on_step=on_step)
        dus = []
        for q in range(4):
            ch, st = _coarse(q)
            lam_r, lam_i = _tiles_cat(lr3, q), _tiles_cat(li3, q)
            wacc_ref[2, q] += _dot(u16[:, ch], lam_r, TN)
            wacc_ref[3, q] += _dot(u16[:, ch], lam_i, TN)
            dus.append(_dot(lam_r, bbr_ref[ch, st], NT) + _dot(lam_i, bbi_ref[ch, st], NT))
        du_ref[...] = dy * dsk_ref[...] + jnp.concatenate(dus, axis=1)

        @pl.when(i == nchunk - 1)
        def _():
            for j in range(NTILE):
                ls = slice(128 * j, 128 * (j + 1))
                dare_ref[:, ls] = jnp.sum(accr3[j], axis=0, keepdims=True)
                daim_ref[:, ls] = jnp.sum(acci3[j], axis=0, keepdims=True)
            for m, o_ref in enumerate((dccr_ref, dcci_ref, dbbr_ref, dbbi_ref)):
                for q in range(4):
                    o_ref[512 * q:512 * (q + 1), :] = _diag_fold_t(wacc_ref[m, q])
            for q in range(4):
                dgw_ref[128 * q:128 * (q + 1), :] = _diag_fold(gacc_ref[q], NH16)

    rev = lambda i: (nchunk - 1 - i, 0)
    full = lambda shape: pl.BlockSpec(shape, lambda i: (0, 0))
    half = pl.BlockSpec((T, SSMW), rev)
    rows = 8 * _seg_pitch(T)
    st3 = pl.BlockSpec((NTILE, rows, 128), lambda i: (0, nchunk - 1 - i, 0))
    return pl.pallas_call(
        body, name="s5_bwd", grid=(nchunk,),
        in_specs=[half, pl.BlockSpec((T, SSMW), lambda i: (nchunk - 1 - i, 3)), st3, st3,
                  full((1, NS)), full((1, NS)), full((SSMW, NS)), full((SSMW, NS)),
                  full((NS, SSMW)), full((NS, SSMW)), full((1, SSMW)), full((SSMW, SSMW)), full((1, SSMW))],
        out_specs=[half, full((1, NS)), full((1, NS)), full((1, SSMW)), full((1, SSMW)),
                   full((NS, NH16)), full((NS, NH16)), full((NS, NH16)), full((NS, NH16)),
                   full((SSMW, NH16))],
        out_shape=[jax.ShapeDtypeStruct((L, SSMW), f32)] + [jax.ShapeDtypeStruct((1, NS), f32)] * 2
                  + [jax.ShapeDtypeStruct((1, SSMW), f32)] * 2 + [jax.ShapeDtypeStruct((NS, NH16), f32)] * 4
                  + [jax.ShapeDtypeStruct((SSMW, NH16), f32)],
        scratch_shapes=[pltpu.VMEM((NTILE, rows, 128), f32), pltpu.VMEM((NTILE, rows, 128), f32),
                        pltpu.VMEM((1, NS), f32), pltpu.VMEM((1, NS), f32),
                        pltpu.VMEM((NTILE, 8, 128), f32), pltpu.VMEM((NTILE, 8, 128), f32),
                        pltpu.VMEM((4, 4, 128, 512), f32), pltpu.VMEM((4, 128, 128), f32)],
        compiler_params=_cp(("arbitrary",)),
    )(dout, p, s_re, s_im, abar_re, abar_im, bbr, bbi, ccr, cci, dsk, gw, gb)


def _block_diag(x):
    g, r, c = x.shape
    eye = jnp.eye(g, dtype=x.dtype)
    return (x[:, :, None, :] * eye[:, None, :, None]).reshape(g * r, g * c)


def _mod_fwd(c16, ada_w, ada_b_loc):
    nloc = ada_w.shape[2]

    def body(c_ref, w_ref, b_ref, o_ref):
        cv = c_ref[...]
        act = cv * _sigmoid(cv)
        o_ref[...] = _dot(act, w_ref[...]) + b_ref[...]

    return pl.pallas_call(
        body, name="mod_fwd", grid=(DEPTH,),
        in_specs=[pl.BlockSpec((16, D), lambda l: (0, 0)), pl.BlockSpec((None, D, nloc), lambda l: (l, 0, 0)),
                  pl.BlockSpec((None, 1, nloc), lambda l: (l, 0, 0))],
        out_specs=pl.BlockSpec((None, 16, nloc), lambda l: (l, 0, 0)),
        out_shape=jax.ShapeDtypeStruct((DEPTH, 16, nloc), f32),
        compiler_params=_cp(("parallel",)),
    )(c16, ada_w, ada_b_loc)


def _ada_grad(c16, dmod16):
    nloc = dmod16.shape[2]

    def body(c_ref, d_ref, o_ref):
        cv = c_ref[...]
        act = cv * _sigmoid(cv)
        o_ref[...] = _dot(act, d_ref[...], TN)

    return pl.pallas_call(
        body, name="ada_grad", grid=(DEPTH,),
        in_specs=[pl.BlockSpec((16, D), lambda l: (0, 0)), pl.BlockSpec((None, 16, nloc), lambda l: (l, 0, 0))],
        out_specs=pl.BlockSpec((None, D, nloc), lambda l: (l, 0, 0)),
        out_shape=jax.ShapeDtypeStruct((DEPTH, D, nloc), f32),
        compiler_params=_cp(("parallel",)),
    )(c16, dmod16)


def _as2d(a):
    return a.reshape(-1, a.shape[-1])


def _ew_rows(rows):
    for cand in (512, 256, 128, 64, 32, 16, 8):
        if rows % cand == 0:
            return cand
    return rows


def _cast_bf16(w):
    w2 = _as2d(w)
    rows, cols = w2.shape
    bm = _ew_rows(rows)

    def body(x_ref, o_ref):
        o_ref[...] = x_ref[...].astype(bf16)

    spec = pl.BlockSpec((bm, cols), lambda i: (i, 0))
    out = pl.pallas_call(body, name="cast_bf16", grid=(rows // bm,), in_specs=[spec], out_specs=spec,
                         out_shape=jax.ShapeDtypeStruct((rows, cols), bf16),
                         compiler_params=_cp(("parallel",)))(w2)
    return out.reshape(w.shape)


def _add2(a, b, out_dtype=f32):
    a2, b2 = _as2d(a), _as2d(b)
    rows, cols = a2.shape
    bm = _ew_rows(rows)

    def body(a_ref, b_ref, o_ref):
        o_ref[...] = (a_ref[...] + b_ref[...]).astype(o_ref.dtype)

    spec = pl.BlockSpec((bm, cols), lambda i: (i, 0))
    out = pl.pallas_call(body, name="add2", grid=(rows // bm,), in_specs=[spec, spec], out_specs=spec,
                         out_shape=jax.ShapeDtypeStruct((rows, cols), out_dtype),
                         compiler_params=_cp(("parallel",)))(a2, b2)
    return out.reshape(a.shape)


def _adamw(parts, w, m, v):
    n = parts.shape[0]
    w2, m2, v2 = _as2d(w), _as2d(m), _as2d(v)
    rows, cols = w2.shape
    p3 = parts.reshape(n, rows, cols)
    bm = _ew_rows(rows)
    if bm * cols * 4 > (1 << 20) and bm % 16 == 0:
        bm //= 2

    def body(p_ref, w_ref, m_ref, v_ref, g_ref, d_ref, nm_ref, nv_ref):
        g = p_ref[0]
        for k in range(1, n):
            g = g + p_ref[k]
        mn = ADAM_B1 * m_ref[...] + (1.0 - ADAM_B1) * g
        vn = ADAM_B2 * v_ref[...] + (1.0 - ADAM_B2) * (g * g)
        m_hat = mn / (1.0 - ADAM_B1 ** ADAM_STEP)
        v_hat = vn / (1.0 - ADAM_B2 ** ADAM_STEP)
        g_ref[...] = g
        d_ref[...] = -ADAM_LR * (m_hat / (jnp.sqrt(v_hat) + ADAM_EPS) + ADAM_WD * w_ref[...])
        nm_ref[...] = mn
        nv_ref[...] = vn

    spec = pl.BlockSpec((bm, cols), lambda i: (i, 0))
    outs = pl.pallas_call(
        body, name="adamw", grid=(rows // bm,),
        in_specs=[pl.BlockSpec((n, bm, cols), lambda i: (0, i, 0)), spec, spec, spec],
        out_specs=[spec] * 4, out_shape=[jax.ShapeDtypeStruct((rows, cols), f32)] * 4,
        compiler_params=_cp(("parallel",)),
    )(p3, w2, m2, v2)
    return tuple(o.reshape(w.shape) for o in outs)


ANY = pl.BlockSpec(memory_space=pl.ANY)


def _my_pos():
    return lax.axis_index("x"), lax.axis_index("y"), lax.axis_index("c")


def _allgather8(name, x):
    def body(x_ref, out_ref, send_sems, recv_sems):
        mx, my, mc = _my_pos()
        me, sibling = (mx, my, mc), (mx, my, 1 - mc)
        chips = [(1 - mx, my), (mx, 1 - my), (1 - mx, 1 - my)]

        def slot(px, py, pc):
            return out_ref.at[4 * px + 2 * py + pc]

        def copy(k, block, to, src=None):
            return pltpu.make_async_remote_copy(
                src_ref=slot(*block) if src is None else src, dst_ref=slot(*block),
                send_sem=send_sems.at[k], recv_sem=recv_sems.at[k], device_id=to, device_id_type=MESH)

        first = [copy(0, me, sibling, src=x_ref)]
        first += [copy(1 + j, me, (*chip, mc), src=x_ref) for j, chip in enumerate(chips)]
        for cp in first:
            cp.start()
        passed = [copy(4 + j, (*chip, mc), sibling) for j, chip in enumerate(chips)]
        for j, chip in enumerate(chips):
            copy(1 + j, (*chip, mc), me).wait_recv()
            passed[j].start()
        copy(0, sibling, me).wait_recv()
        for j, chip in enumerate(chips):
            copy(4 + j, (*chip, 1 - mc), me).wait_recv()
        for cp in first + passed:
            cp.wait_send()

    out = pl.pallas_call(
        body, name=name, in_specs=[ANY], out_specs=ANY,
        out_shape=jax.ShapeDtypeStruct((NDEV,) + x.shape, x.dtype),
        scratch_shapes=[pltpu.SemaphoreType.DMA((7,)), pltpu.SemaphoreType.DMA((7,))],
    )(x)
    dev = 4 * lax.axis_index("x") + 2 * lax.axis_index("y") + lax.axis_index("c")
    return lax.dynamic_update_index_in_dim(out, x, dev, 0)


def _sum_parts(parts):
    n = parts.shape[0]
    p3 = parts.reshape(n, -1, parts.shape[-1])
    rows, cols = p3.shape[1:]
    bm = _ew_rows(rows)
    if bm * cols * 4 > (1 << 20) and bm % 16 == 0:
        bm //= 2

    def body(p_ref, o_ref):
        g = p_ref[0].astype(f32)
        for k in range(1, n):
            g = g + p_ref[k].astype(f32)
        o_ref[...] = g

    out = pl.pallas_call(
        body, name="sum_parts", grid=(rows // bm,),
        in_specs=[pl.BlockSpec((n, bm, cols), lambda i: (0, i, 0))],
        out_specs=pl.BlockSpec((bm, cols), lambda i: (i, 0)),
        out_shape=jax.ShapeDtypeStruct((rows, cols), f32),
        compiler_params=_cp(("parallel",)),
    )(p3)
    return out.reshape(parts.shape[1:])


def _rc(src, dst, ss, rs, k, to):
    return pltpu.make_async_remote_copy(src_ref=src, dst_ref=dst, send_sem=ss.at[k], recv_sem=rs.at[k],
                                        device_id=to, device_id_type=MESH)


def _half_rows(rows, c):
    return pl.ds((rows // 2) * c, rows // 2)


def _gather_comm(shards):
    n = len(shards)

    def plan(ins, outs, ss, rs):
        mx, my, mc = _my_pos()
        chips = [(1 - mx, my), (mx, 1 - my), (1 - mx, 1 - my)]
        mine = 2 * mx + my
        sends, lands, fwds, fwd_lands = [], [], [], []
        for a in range(n):
            rows = ins[a].shape[0]
            mr, sr = _half_rows(rows, mc), _half_rows(rows, 1 - mc)
            for j, (px, py) in enumerate(chips):
                k, src_chip = 3 * a + j, 2 * px + py
                sends.append(_rc(ins[a].at[mr], outs[a].at[mine, mr], ss, rs, k, (px, py, mc)))
                lands.append(_rc(ins[a].at[mr], outs[a].at[src_chip, mr], ss, rs, k, (px, py, mc)))
                blk, sblk = outs[a].at[src_chip, mr], outs[a].at[src_chip, sr]
                fwds.append(_rc(blk, blk, ss, rs, 3 * n + k, (mx, my, 1 - mc)))
                fwd_lands.append(_rc(sblk, sblk, ss, rs, 3 * n + k, (mx, my, 1 - mc)))
        return sends, lands, fwds, fwd_lands

    def start(ins, outs, ss, rs):
        for cp in plan(ins, outs, ss, rs)[0]:
            cp.start()

    def finish(ins, outs, ss, rs):
        sends, lands, fwds, fwd_lands = plan(ins, outs, ss, rs)
        for land, fwd in zip(lands, fwds):
            land.wait_recv()
            fwd.start()
        for cp in fwd_lands:
            cp.wait_recv()
        for cp in sends + fwds:
            cp.wait_send()

    return dict(ins=list(shards), nsem=6 * n, start=start, finish=finish,
                out_shapes=[jax.ShapeDtypeStruct((NCHIP,) + a.shape, a.dtype) for a in shards])


def _gather_done(outs, shards):
    mine = 2 * lax.axis_index("x") + lax.axis_index("y")
    return [lax.dynamic_update_index_in_dim(o, a, mine, 0) for o, a in zip(outs, shards)]


def _swap_comm(grads):
    n = len(grads)

    def plan(ins, outs, ss, rs):
        mx, my, mc = _my_pos()
        return [_rc(ins[a].at[j, _half_rows(ins[a].shape[1], 1 - mc)], outs[a].at[j], ss, rs,
                    NCHIP * a + j, (mx, my, 1 - mc)) for a in range(n) for j in range(NCHIP)]

    def start(ins, outs, ss, rs):
        for cp in plan(ins, outs, ss, rs):
            cp.start()

    def finish(ins, outs, ss, rs):
        for cp in plan(ins, outs, ss, rs):
            cp.wait()

    return dict(ins=list(grads), nsem=NCHIP * n, start=start, finish=finish,
                out_shapes=[jax.ShapeDtypeStruct((NCHIP, a.shape[1] // 2) + a.shape[2:], a.dtype)
                            for a in grads])


def _scatter_comm(sums):
    n = len(sums)

    def plan(ins, outs, ss, rs):
        mx, my, mc = _my_pos()
        chips = [(1 - mx, my), (mx, 1 - my), (1 - mx, 1 - my)]
        mine = 2 * mx + my
        sends, lands = [], []
        for a in range(n):
            for j, (px, py) in enumerate(chips):
                sends.append(_rc(ins[a].at[2 * px + py], outs[a].at[mine], ss, rs, 3 * a + j, (px, py, mc)))
                lands.append(_rc(ins[a].at[mine], outs[a].at[2 * px + py], ss, rs, 3 * a + j, (px, py, mc)))
        return sends, lands

    def start(ins, outs, ss, rs):
        for cp in plan(ins, outs, ss, rs)[0]:
            cp.start()

    def finish(ins, outs, ss, rs):
        sends, lands = plan(ins, outs, ss, rs)
        for cp in lands:
            cp.wait_recv()
        for cp in sends:
            cp.wait_send()

    return dict(ins=list(sums), nsem=3 * n, start=start, finish=finish,
                out_shapes=[jax.ShapeDtypeStruct(a.shape, a.dtype) for a in sums])


def _scatter_done(outs, sums):
    mine = 2 * lax.axis_index("x") + lax.axis_index("y")
    return [lax.dynamic_update_index_in_dim(o, lax.dynamic_index_in_dim(a, mine, 0, keepdims=False), mine, 0)
            for o, a in zip(outs, sums)]


def _merge_comm(halves):
    n = len(halves)

    def plan(ins, outs, ss, rs):
        mx, my, mc = _my_pos()
        return [_rc(ins[a], outs[a].at[_half_rows(outs[a].shape[0], mc)], ss, rs, a, (mx, my, 1 - mc))
                for a in range(n)]

    def start(ins, outs, ss, rs):
        for cp in plan(ins, outs, ss, rs):
            cp.start()

    def finish(ins, outs, ss, rs):
        for cp in plan(ins, outs, ss, rs):
            cp.wait()

    return dict(ins=list(halves), nsem=n, start=start, finish=finish,
                out_shapes=[jax.ShapeDtypeStruct((2 * a.shape[0],) + a.shape[1:], a.dtype) for a in halves])


def _merge_done(outs, halves):
    mc = lax.axis_index("c")
    return [lax.dynamic_update_slice_in_dim(o, a, a.shape[0] * mc, 0) for o, a in zip(outs, halves)]


def _comm_call(name, comm):
    ni, no = len(comm["ins"]), len(comm["out_shapes"])

    def body(*refs):
        ins, outs, (ss, rs) = refs[:ni], refs[ni:ni + no], refs[ni + no:]
        comm["start"](ins, outs, ss, rs)
        comm["finish"](ins, outs, ss, rs)

    return pl.pallas_call(
        body, name=name, in_specs=[ANY] * ni, out_specs=[ANY] * no, out_shape=comm["out_shapes"],
        scratch_shapes=[pltpu.SemaphoreType.DMA((comm["nsem"],)), pltpu.SemaphoreType.DMA((comm["nsem"],))],
    )(*comm["ins"])


def _call(body, comm, *, name, grid, in_specs, out_specs, out_shape, scratch_shapes, sem, args):
    if comm is None:
        outs = pl.pallas_call(body, name=name, grid=grid, in_specs=in_specs, out_specs=out_specs,
                              out_shape=out_shape, scratch_shapes=scratch_shapes,
                              compiler_params=_cp(sem))(*args)
        return outs, None
    ni, no = len(in_specs), len(out_specs)
    ci, co = len(comm["ins"]), len(comm["out_shapes"])
    nscr = len(scratch_shapes)

    def carried(*refs):
        ins, cins = refs[:ni], refs[ni:ni + ci]
        outs, couts = refs[ni + ci:ni + ci + no], refs[ni + ci + no:ni + ci + no + co]
        scr = refs[ni + ci + no + co:]
        ss, rs = scr[nscr], scr[nscr + 1]
        pids = [pl.program_id(ax) for ax in range(len(grid))]
        first = functools.reduce(jnp.logical_and, [p == 0 for p in pids])
        last = functools.reduce(jnp.logical_and, [p == g - 1 for p, g in zip(pids, grid)])

        @pl.when(first)
        def _():
            comm["start"](cins, couts, ss, rs)

        body(*ins, *outs, *scr[:nscr])

        @pl.when(last)
        def _():
            comm["finish"](cins, couts, ss, rs)

    outs = pl.pallas_call(
        carried, name=name + "_x", grid=grid, in_specs=list(in_specs) + [ANY] * ci,
        out_specs=list(out_specs) + [ANY] * co, out_shape=list(out_shape) + comm["out_shapes"],
        scratch_shapes=list(scratch_shapes) + [pltpu.SemaphoreType.DMA((comm["nsem"],)),
                                               pltpu.SemaphoreType.DMA((comm["nsem"],))],
        compiler_params=_cp(("arbitrary",) * len(grid)),
    )(*args, *comm["ins"])
    return outs[:no], outs[no:]


def _layer_fwd(x, res, res_gate, wl, mod, comm=None):
    L = x.shape[0]
    bm = _row_block(L, 1024)
    nb = L // bm
    sh1, sc1, g1, sh2, sc2, g2 = [mod[k * D:(k + 1) * D][None, :] for k in range(NMOD)]
    sv = {}
    xin, h = _rmsmod_fwd(x, res, res_gate, wl["norm1_g"], sh1, sc1)
    if xin is None:
        xin = x
    p = _mm("mm_in", h, wl["w_in"], dims=NN, grid=(nb, 4, 1),
            a_block=(bm, D), a_map=lambda i, j, k: (i, 0),
            b_block=(None, D, 512), b_map=lambda i, j, k: (j, 0, 0),
            out_shape=(L, INW), o_block=(bm, 512), o_map=lambda i, j, k: (i, j))
    qkv = _qknorm_fwd(p, wl["gqk"])
    oa, couts = _attn_fwd(qkv, comm)
    os_, s_re, s_im = _s5_fwd2(p, wl["abar_re"], wl["abar_im"], wl["bbr"], wl["bbi"], wl["ccr"], wl["cci"],
                              wl["dsk"], wl["gw"], wl["gb"])
    o = _outnorm_fwd(oa, os_, wl["attn_out_g"], wl["ssm_out_g"])
    proj = _mm("mm_out", o, wl["w_out"], dims=NN, grid=(nb, 2, 1),
               a_block=(bm, D), a_map=lambda i, j, k: (i, 0),
               b_block=(D, 512), b_map=lambda i, j, k: (0, j),
               out_shape=(L, D), o_block=(bm, 512), o_map=lambda i, j, k: (i, j))
    x1, h2 = _rmsmod_fwd(xin, proj, g1, wl["norm2_g"], sh2, sc2)
    up_pre = _mm("mm_up", h2, wl["w_up"], dims=NN, grid=(nb, 4, 1),
                 a_block=(bm, D), a_map=lambda i, j, k: (i, 0),
                 b_block=(None, D, 1408), b_map=lambda i, j, k: (j, 0, 0),
                 out_shape=(L, DUP), o_block=(bm, 1408), o_map=lambda i, j, k: (i, j))
    a = _convglu_fwd(up_pre, wl["conv_w"], wl["conv_b"])
    down = _mm("mm_down", a, wl["w_down"], dims=NN, grid=(nb, 2, 1),
               a_block=(bm, DFF), a_map=lambda i, j, k: (i, 0),
               b_block=(DFF, 512), b_map=lambda i, j, k: (0, j),
               out_shape=(L, D), o_block=(bm, 512), o_map=lambda i, j, k: (i, j))
    sv.update(xin=xin, h=h, p=p, qkv=qkv, oa=oa, os=os_, s_re=s_re, s_im=s_im, o=o, proj=proj,
              x1=x1, h2=h2, up_pre=up_pre, a=a, down=down,
              sc1=sc1, g1=g1, sc2=sc2, g2=g2)
    return x1, down, g2, sv, couts


def _big_grads(g):
    return [g["w_in"], g["w_out"].reshape(NCHIP, D // NCHIP, D), g["w_up"],
            g["w_down"].reshape(NCHIP, DFF // NCHIP, D)]


def _chip_sums(grads, recv):
    mc = lax.axis_index("c")
    mine = [lax.dynamic_slice_in_dim(a, (a.shape[1] // 2) * mc, a.shape[1] // 2, axis=1) for a in grads]
    return [_add2(a, b, bf16) for a, b in zip(mine, recv)]


def _reduced(parts, sums):
    halves = [_sum_parts(p) for p in _scatter_done(parts, sums)]
    return _merge_done(_comm_call("grad_merge", _merge_comm(halves)), halves)


def _layer_bwd(dx_out, wl, sv, prev=None):
    L = dx_out.shape[0]
    bm = _row_block(L, 1024)
    nb = L // bm
    bk = _row_block(L, 512)
    nk = L // bk
    g = {}
    dd, dg2 = _gate_bwd(dx_out, sv["down"], sv["g2"])
    da = _mm("mm_down_dx", dd, wl["w_down"], dims=NT, grid=(nb, 2, 1),
             a_block=(bm, D), a_map=lambda i, j, k: (i, 0),
             b_block=(1408, D), b_map=lambda i, j, k: (j, 0),
             out_shape=(L, DFF), o_block=(bm, 1408), o_map=lambda i, j, k: (i, j))
    g["w_down"] = _mm("mm_down_dw", sv["a"], dd, dims=TN, grid=(1, 2, nk),
                      a_block=(bk, DFF), a_map=lambda i, j, k: (k, 0),
                      b_block=(bk, 512), b_map=lambda i, j, k: (k, j),
                      out_shape=(DFF, D), o_block=(DFF, 512), o_map=lambda i, j, k: (0, j))
    dup, g["conv_w"], g["conv_b"], recv = _convglu_bwd(da, sv["up_pre"], wl["conv_w"], wl["conv_b"],
                                                       _swap_comm(prev) if prev else None)
    sums = _chip_sums(prev, recv) if prev else None
    dh2 = _mm("mm_up_dx", dup, wl["w_up"], dims=NT, grid=(nb, 1, 4),
              a_block=(None, bm, 1408), a_map=lambda i, j, k: (k // 2, i, k % 2),
              b_block=(None, D, 1408), b_map=lambda i, j, k: (k, 0, 0),
              out_shape=(L, D), o_block=(bm, D), o_map=lambda i, j, k: (i, 0))
    g["w_up"] = _mm("mm_up_dw", sv["h2"], dup, dims=TN, grid=(1, 4, nk),
                    a_block=(bk, D), a_map=lambda i, j, k: (k, 0),
                    b_block=(None, bk, 1408), b_map=lambda i, j, k: (j // 2, k, j % 2),
                    out_shape=(NCHIP, D, 1408), o_block=(None, D, 1408), o_map=lambda i, j, k: (j, 0, 0))
    dx1, dsh2, dsc2, g["norm2_g"] = _rmsmod_bwd(dh2, sv["x1"], wl["norm2_g"], sv["sc2"], dx_out)
    dproj, dg1 = _gate_bwd(dx1, sv["proj"], sv["g1"])
    do = _mm("mm_out_dx", dproj, wl["w_out"], dims=NT, grid=(nb, 2, 1),
             a_block=(bm, D), a_map=lambda i, j, k: (i, 0),
             b_block=(512, D), b_map=lambda i, j, k: (j, 0),
             out_shape=(L, D), o_block=(bm, 512), o_map=lambda i, j, k: (i, j))
    g["w_out"] = _mm("mm_out_dw", sv["o"], dproj, dims=TN, grid=(1, 2, nk),
                     a_block=(bk, D), a_map=lambda i, j, k: (k, 0),
                     b_block=(bk, 512), b_map=lambda i, j, k: (k, j),
                     out_shape=(D, D), o_block=(D, 512), o_map=lambda i, j, k: (0, j))
    doa, dos, g["attn_out_g"], g["ssm_out_g"] = _outnorm_bwd(do, sv["oa"], sv["os"],
                                                             wl["attn_out_g"], wl["ssm_out_g"])
    (dqn, dkn, dv), parts = _attn_bwd(sv["qkv"], sv["oa"], doa, _scatter_comm(sums) if prev else None)
    prev_reduced = _reduced(parts, sums) if prev else None
    (du, d_abar_re, d_abar_im, g["ssm_d"], g["glu_b"], d_ccr, d_cci, d_bbre, d_bbim, d_gw) = _s5_bwd2(
        dos, sv["p"], sv["s_re"], sv["s_im"], wl["abar_re"], wl["abar_im"], wl["bbr"], wl["bbi"],
        wl["ccr"], wl["cci"], wl["dsk"], wl["gw"], wl["gb"])
    g["ssm_c_re"] = d_ccr.reshape(NG, NP, NH16).transpose(0, 2, 1)
    g["ssm_c_im"] = -d_cci.reshape(NG, NP, NH16).transpose(0, 2, 1)
    g["glu_w"] = d_gw.reshape(NG, NH16, NH16)
    d_ldt, d_ar, d_ai, d_br, d_bi = _s5_disc_bwd(
        wl["ldt_col"], wl["ar_col"], wl["ai_col"], wl["br_mat"], wl["bi_mat"],
        d_abar_re.reshape(NS, 1), d_abar_im.reshape(NS, 1), d_bbre, d_bbim)
    g["ssm_log_dt"] = d_ldt[:, 0]
    g["ssm_a_re"] = d_ar.reshape(NG, NP)
    g["ssm_a_im"] = d_ai.reshape(NG, NP)
    g["ssm_b_re"] = d_br.reshape(NG, NP, NH16)
    g["ssm_b_im"] = d_bi.reshape(NG, NP, NH16)
    dp, dgqk = _qknorm_bwd(dqn, dkn, dv, du, sv["p"], wl["gqk"])
    g["q_norm_g"] = dgqk[0, :HD]
    g["k_norm_g"] = dgqk[0, HD:2 * HD]
    dh = _mm("mm_in_dx", dp, wl["w_in"], dims=NT, grid=(nb, 1, 4),
             a_block=(bm, 512), a_map=lambda i, j, k: (i, k),
             b_block=(None, D, 512), b_map=lambda i, j, k: (k, 0, 0),
             out_shape=(L, D), o_block=(bm, D), o_map=lambda i, j, k: (i, 0))
    g["w_in"] = _mm("mm_in_dw", sv["h"], dp, dims=TN, grid=(1, 4, nk),
                    a_block=(bk, D), a_map=lambda i, j, k: (k, 0),
                    b_block=(bk, 512), b_map=lambda i, j, k: (k, j),
                    out_shape=(NCHIP, D, 512), o_block=(None, D, 512), o_map=lambda i, j, k: (j, 0, 0))
    dx_in, dsh1, dsc1, g["norm1_g"] = _rmsmod_bwd(dh, sv["xin"], wl["norm1_g"], sv["sc1"], dx1)
    g["dmod"] = jnp.concatenate([dsh1, dsc1, dg1, dsh2, dsc2, dg2], axis=1)[0]
    return dx_in, g, prev_reduced


def _prep_layer(l, conv_w, small):
    wl = {}
    wl["conv_w"] = conv_w[l]
    wl["conv_b"] = small["ffn_conv_b"][l][None, :]
    wl["norm1_g"] = small["norm1_g"][l][None, :]
    wl["norm2_g"] = small["norm2_g"][l][None, :]
    wl["attn_out_g"] = small["attn_out_g"][l][None, :]
    wl["ssm_out_g"] = small["ssm_out_g"][l][None, :]
    wl["gqk"] = jnp.concatenate([jnp.tile(small["q_norm_g"][l], 8), jnp.tile(small["k_norm_g"][l], 8)])[None, :]
    wl["ldt_col"] = jnp.repeat(small["ssm_log_dt"][l], NP)[:, None]
    wl["ar_col"] = small["ssm_a_re"][l].reshape(NS, 1)
    wl["ai_col"] = small["ssm_a_im"][l].reshape(NS, 1)
    wl["br_mat"] = small["ssm_b_re"][l].reshape(NS, NH16)
    wl["bi_mat"] = small["ssm_b_im"][l].reshape(NS, NH16)
    abar_re, abar_im, bb_re, bb_im = _s5_disc_fwd(wl["ldt_col"], wl["ar_col"], wl["ai_col"],
                                                  wl["br_mat"], wl["bi_mat"])
    wl["abar_re"] = abar_re.reshape(1, NS)
    wl["abar_im"] = abar_im.reshape(1, NS)
    wl["bbr"] = _block_diag(bb_re.astype(bf16).reshape(NG, NP, NH16).transpose(0, 2, 1))
    wl["bbi"] = _block_diag(bb_im.astype(bf16).reshape(NG, NP, NH16).transpose(0, 2, 1))
    wl["ccr"] = _block_diag(small["ssm_c_re"][l].astype(bf16).transpose(0, 2, 1))
    wl["cci"] = _block_diag(small["ssm_c_im"][l].astype(bf16).transpose(0, 2, 1))
    wl["gw"] = _block_diag(small["glu_w"][l].astype(bf16))
    wl["dsk"] = small["ssm_d"][l].reshape(1, SSMW)
    wl["gb"] = small["glu_b"][l].reshape(1, SSMW)
    return wl


def _local_step(x, target, mods, shards, conv_w, small):
    def set_big(wl, gathered):
        w_in, w_out, w_up, w_down = gathered
        wl.update(w_in=w_in, w_up=w_up, w_out=w_out.reshape(D, D), w_down=w_down.reshape(DFF, D))

    def layer_shards(l):
        return [s[l] for s in shards]

    wls = [_prep_layer(l, conv_w, small) for l in range(DEPTH)]
    first = _gather_comm(layer_shards(0))
    set_big(wls[0], _gather_done(_comm_call("weight_gather", first), first["ins"]))
    saved = []
    cur, res, res_gate = x, None, None
    for l in range(DEPTH):
        nxt = _gather_comm(layer_shards(l + 1)) if l + 1 < DEPTH else None
        cur, res, res_gate, sv, couts = _layer_fwd(cur, res, res_gate, wls[l], mods[l], nxt)
        saved.append(sv)
        if nxt is not None:
            set_big(wls[l + 1], _gather_done(couts, nxt["ins"]))
    loss, dx = _loss_fwd_bwd(cur, res, res_gate, target)
    grads = [None] * DEPTH
    big = [None] * DEPTH
    prev = None
    for l in reversed(range(DEPTH)):
        dx, grads[l], done = _layer_bwd(dx, wls[l], saved[l], prev)
        if prev is not None:
            big[l + 1] = done
        prev = _big_grads(grads[l])
    recv = _comm_call("grad_swap", _swap_comm(prev))
    sums = _chip_sums(prev, recv)
    big[0] = _reduced(_comm_call("grad_scatter", _scatter_comm(sums)), sums)
    return loss, dx, grads, big


BIG = ("w_in", "w_out", "ffn_w_up", "ffn_w_down")
SMALL = ("ada_b", "norm1_g", "q_norm_g", "k_norm_g", "ssm_a_re", "ssm_a_im", "ssm_log_dt",
         "ssm_b_re", "ssm_b_im", "ssm_c_re", "ssm_c_im", "ssm_d", "glu_w", "glu_b",
         "attn_out_g", "ssm_out_g", "norm2_g", "ffn_conv_b")
NAMES = ("ada_w", "ada_b", "norm1_g", "w_in", "q_norm_g", "k_norm_g", "ssm_a_re", "ssm_a_im",
         "ssm_log_dt", "ssm_b_re", "ssm_b_im", "ssm_c_re", "ssm_c_im", "ssm_d", "glu_w", "glu_b",
         "attn_out_g", "ssm_out_g", "w_out", "norm2_g", "ffn_w_up", "ffn_conv_w", "ffn_conv_b",
         "ffn_w_down")
PACK_COLS = 1024


def _pack(arrs):
    flat = jnp.concatenate([a.reshape(-1) for a in arrs])
    rows = -(-flat.shape[0] // PACK_COLS)
    rows = -(-rows // 8) * 8
    flat = jnp.pad(flat, (0, rows * PACK_COLS - flat.shape[0]))
    return flat.reshape(rows, PACK_COLS)


def _unpack(packed, shapes):
    flat = packed.reshape(-1)
    out, off = [], 0
    for s in shapes:
        n = math.prod(s)
        out.append(flat[off:off + n].reshape(s))
        off += n
    return out


def kernel(x, c, ada_w, ada_b, norm1_g, w_in, q_norm_g, k_norm_g, ssm_a_re, ssm_a_im, ssm_log_dt, ssm_b_re, ssm_b_im, ssm_c_re, ssm_c_im, ssm_d, glu_w, glu_b, attn_out_g, ssm_out_g, w_out, norm2_g, ffn_w_up, ffn_conv_w, ffn_conv_b, ffn_w_down, loss_target, m_ada_w, m_ada_b, m_norm1_g, m_w_in, m_q_norm_g, m_k_norm_g, m_ssm_a_re, m_ssm_a_im, m_ssm_log_dt, m_ssm_b_re, m_ssm_b_im, m_ssm_c_re, m_ssm_c_im, m_ssm_d, m_glu_w, m_glu_b, m_attn_out_g, m_ssm_out_g, m_w_out, m_norm2_g, m_ffn_w_up, m_ffn_conv_w, m_ffn_conv_b, m_ffn_w_down, v_ada_w, v_ada_b, v_norm1_g, v_w_in, v_q_norm_g, v_k_norm_g, v_ssm_a_re, v_ssm_a_im, v_ssm_log_dt, v_ssm_b_re, v_ssm_b_im, v_ssm_c_re, v_ssm_c_im, v_ssm_d, v_glu_w, v_glu_b, v_attn_out_g, v_ssm_out_g, v_w_out, v_norm2_g, v_ffn_w_up, v_ffn_conv_w, v_ffn_conv_b, v_ffn_w_down):
    env = dict(locals())
    w = {n: env[n] for n in NAMES}
    m = {n: env["m_" + n] for n in NAMES}
    v = {n: env["v_" + n] for n in NAMES}
    mx, my, mc = _my_pos()
    chip = 2 * mx + my
    dev = 4 * mx + 2 * my + mc
    xl = x[0]
    tl = loss_target[0]

    c_all = _allgather8("gather_c", jnp.pad(c, ((0, 7), (0, 0))))[:, 0, :]
    c16 = jnp.pad(c_all, ((0, 8), (0, 0)))
    nloc = NMOD * D // NCHIP
    ada_b_loc = lax.dynamic_slice(ada_b, (0, chip * nloc), (DEPTH, nloc))[:, None, :]
    mod_loc = _mod_fwd(c16, ada_w, ada_b_loc)
    mod_g = _allgather8("gather_mod", mod_loc.reshape(DEPTH * 16, nloc))
    mod_all = mod_g[0::2].reshape(NCHIP, DEPTH, 16, nloc).transpose(1, 2, 0, 3).reshape(DEPTH, 16, NMOD * D)
    mods = lax.dynamic_index_in_dim(mod_all, dev, axis=1, keepdims=False)

    shards = [_cast_bf16(w[n]) for n in BIG]
    ncw = DUP // NCHIP
    conv_g = _allgather8("gather_conv", jnp.pad(w["ffn_conv_w"].reshape(DEPTH * 3, ncw), ((0, 4), (0, 0))))
    conv_w = conv_g[0::2, :DEPTH * 3].reshape(NCHIP, DEPTH, 3, ncw).transpose(1, 2, 0, 3).reshape(DEPTH, 3, DUP)
    small = {n: w[n] for n in SMALL}

    loss_loc, grad_x, grads, big = _local_step(xl, tl, mods, shards, conv_w, small)
    loss = lax.psum(loss_loc, ("x", "y", "c"))

    outs = {}
    for k, name in enumerate(BIG):
        g_big = jnp.stack([big[l][k] for l in range(DEPTH)])
        outs[name] = _adamw(g_big[None], w[name], m[name], v[name])

    small_names = SMALL[1:] + ("ffn_conv_w_full",)
    small_grads = []
    for n in SMALL[1:]:
        key = {"ffn_conv_b": "conv_b"}.get(n, n)
        small_grads.append(jnp.stack([grads[l][key].reshape(w[n].shape[1:]) for l in range(DEPTH)]))
    small_grads.append(jnp.stack([grads[l]["conv_w"] for l in range(DEPTH)]))
    dmod = jnp.stack([grads[l]["dmod"] for l in range(DEPTH)])
    packed = _pack([dmod] + small_grads)
    allp = _allgather8("gather_small", packed)
    shapes = [dmod.shape] + [a.shape for a in small_grads]
    dmod_all = allp[:, :DEPTH * NMOD, :].reshape(NDEV, DEPTH, NMOD * D)
    dmod_loc = lax.dynamic_slice(dmod_all, (0, 0, chip * nloc), (NDEV, DEPTH, nloc)).transpose(1, 0, 2)
    g_ada = _ada_grad(c16, jnp.pad(dmod_loc, ((0, 0), (0, 8), (0, 0))))
    outs["ada_w"] = _adamw(g_ada[None], w["ada_w"], m["ada_w"], v["ada_w"])
    conv_w_shape = (DEPTH, 3, DUP)
    w_small = _pack([w[n] for n in SMALL] + [jnp.zeros(conv_w_shape, f32)])
    m_small = _pack([m[n] for n in SMALL] + [jnp.zeros(conv_w_shape, f32)])
    v_small = _pack([v[n] for n in SMALL] + [jnp.ones(conv_w_shape, f32)])
    res_small = _adamw(allp, w_small, m_small, v_small)
    unpacked = [_unpack(r, shapes) for r in res_small]
    for i, n in enumerate(SMALL):
        outs[n] = tuple(unpacked[k][i] for k in range(4))
    g_conv_full = unpacked[0][len(SMALL)]
    ncw = DUP // NCHIP
    g_conv = lax.dynamic_slice(g_conv_full, (0, 0, chip * ncw), (DEPTH, 3, ncw))
    outs["ffn_conv_w"] = _adamw(g_conv[None], w["ffn_conv_w"], m["ffn_conv_w"], v["ffn_conv_w"])

    result = [loss, grad_x[None]]
    for k in range(4):
        result += [outs[n][k] for n in NAMES]
    return tuple(result)
```

```python
import functools
import math

import jax
import jax.numpy as jnp
from jax import lax
from jax.experimental import pallas as pl
from jax.experimental.pallas import tpu as pltpu

f32 = jnp.float32
bf16 = jnp.bfloat16
_MXU = jnp.bfloat16

D = 1024
ATT = 512
SSMW = 512
HD = 64
NG = 32
NP = 64
NH16 = 16
NS = NG * NP
INW = 2048
DFF = 2816
DUP = 2 * DFF
NMOD = 6
DEPTH = 4
EPS = 1e-6
NCHIP = 4
NDEV = 8

ADAM_LR = 0.001
ADAM_B1 = 0.9
ADAM_B2 = 0.999
ADAM_EPS = 1e-08
ADAM_WD = 0.01
ADAM_STEP = 10

VMEM_LIMIT = 56 * 1024 * 1024
MESH = pl.DeviceIdType.MESH

NN = (((1,), (0,)), ((), ()))
NT = (((1,), (1,)), ((), ()))
TN = (((0,), (0,)), ((), ()))


def _cp(sem=None):
    if sem is None:
        return pltpu.CompilerParams(vmem_limit_bytes=VMEM_LIMIT)
    return pltpu.CompilerParams(dimension_semantics=sem, vmem_limit_bytes=VMEM_LIMIT)


def _dot(a, b, dims=NN):
    return lax.dot_general(a.astype(_MXU), b.astype(_MXU), dims, preferred_element_type=f32)


def _dot_exact(x, m):
    hi = x.astype(bf16)
    r1 = x - hi.astype(f32)
    mid = r1.astype(bf16)
    lo = (r1 - mid.astype(f32)).astype(bf16)
    d = lambda p: lax.dot_general(p, m, NN, preferred_element_type=f32)
    return d(hi) + d(mid) + d(lo)


def _gelu(x):
    c = math.sqrt(2.0 / math.pi)
    return 0.5 * x * (1.0 + jnp.tanh(c * (x + 0.044715 * (x * x * x))))


def _gelu_grad(x):
    c = math.sqrt(2.0 / math.pi)
    t = jnp.tanh(c * (x + 0.044715 * (x * x * x)))
    return 0.5 * (1.0 + t) + 0.5 * x * (1.0 - t * t) * c * (1.0 + 3.0 * 0.044715 * (x * x))


def _sigmoid(x):
    return 1.0 / (1.0 + jnp.exp(-x))


def _mm(name, a, b, *, dims, grid, a_block, a_map, b_block, b_map, out_shape, o_block, o_map,
        out_dtype=f32):
    nk = grid[2]
    acc_shape = tuple(s for s in o_block if s is not None)

    def body(a_ref, b_ref, o_ref, acc_ref):
        k = pl.program_id(2)
        part = _dot(a_ref[...], b_ref[...], dims)
        if nk == 1:
            o_ref[...] = part.astype(o_ref.dtype)
        else:
            @pl.when(k == 0)
            def _():
                acc_ref[...] = part

            @pl.when(k > 0)
            def _():
                acc_ref[...] += part

            @pl.when(k == nk - 1)
            def _():
                o_ref[...] = acc_ref[...].astype(o_ref.dtype)

    return pl.pallas_call(
        body, name=name, grid=grid,
        in_specs=[pl.BlockSpec(a_block, a_map), pl.BlockSpec(b_block, b_map)],
        out_specs=pl.BlockSpec(o_block, o_map),
        out_shape=jax.ShapeDtypeStruct(out_shape, out_dtype),
        scratch_shapes=[pltpu.VMEM(acc_shape if nk > 1 else (8, 128), f32)],
        compiler_params=_cp(("parallel", "parallel", "arbitrary")),
    )(a, b)


def _row_block(L, want):
    return want if L % want == 0 else L


def _rmsmod_fwd(x, res, gate, g, sh, sc):
    L = x.shape[0]
    bm = _row_block(L, 256)
    with_res = res is not None

    def body(*refs):
        if with_res:
            x_ref, r_ref, gt_ref, g_ref, sh_ref, sc_ref, xo_ref, h_ref = refs
            xin = x_ref[...] + gt_ref[...] * r_ref[...]
            xo_ref[...] = xin
        else:
            x_ref, g_ref, sh_ref, sc_ref, h_ref = refs
            xin = x_ref[...]
        inv = lax.rsqrt(jnp.mean(xin * xin, axis=-1, keepdims=True) + EPS)
        xn = xin * inv * g_ref[...]
        h_ref[...] = (xn * (1.0 + sc_ref[...]) + sh_ref[...]).astype(h_ref.dtype)

    row = pl.BlockSpec((bm, D), lambda i: (i, 0))
    vec = pl.BlockSpec((1, D), lambda i: (0, 0))
    if with_res:
        return pl.pallas_call(
            body, name="rmsmod_res_fwd", grid=(L // bm,),
            in_specs=[row, row, vec, vec, vec, vec], out_specs=[row, row],
            out_shape=[jax.ShapeDtypeStruct((L, D), f32), jax.ShapeDtypeStruct((L, D), bf16)],
            compiler_params=_cp(("parallel",)),
        )(x, res, gate, g, sh, sc)
    h = pl.pallas_call(
        body, name="rmsmod_fwd", grid=(L // bm,),
        in_specs=[row, vec, vec, vec], out_specs=row,
        out_shape=jax.ShapeDtypeStruct((L, D), bf16),
        compiler_params=_cp(("parallel",)),
    )(x, g, sh, sc)
    return None, h


def _rmsmod_bwd(dh, x, g, sc, dres):
    L = x.shape[0]
    bm = _row_block(L, 256)

    def body(dh_ref, x_ref, g_ref, sc_ref, dr_ref, dx_ref, dsh_ref, dsc_ref, dg_ref):
        i = pl.program_id(0)
        xv = x_ref[...]
        dhv = dh_ref[...]
        inv = lax.rsqrt(jnp.mean(xv * xv, axis=-1, keepdims=True) + EPS)
        xh = xv * inv
        gv = g_ref[...]
        xn = xh * gv
        dxn = dhv * (1.0 + sc_ref[...])
        dxh = dxn * gv
        dx_ref[...] = inv * (dxh - xh * jnp.mean(dxh * xh, axis=-1, keepdims=True)) + dr_ref[...]
        p_sh = jnp.sum(dhv, axis=0, keepdims=True)
        p_sc = jnp.sum(dhv * xn, axis=0, keepdims=True)
        p_g = jnp.sum(dxn * xh, axis=0, keepdims=True)

        @pl.when(i == 0)
        def _():
            dsh_ref[...] = p_sh
            dsc_ref[...] = p_sc
            dg_ref[...] = p_g

        @pl.when(i > 0)
        def _():
            dsh_ref[...] += p_sh
            dsc_ref[...] += p_sc
            dg_ref[...] += p_g

    row = pl.BlockSpec((bm, D), lambda i: (i, 0))
    vec = pl.BlockSpec((1, D), lambda i: (0, 0))
    return pl.pallas_call(
        body, name="rmsmod_bwd", grid=(L // bm,),
        in_specs=[row, row, vec, vec, row], out_specs=[row, vec, vec, vec],
        out_shape=[jax.ShapeDtypeStruct((L, D), f32)] + [jax.ShapeDtypeStruct((1, D), f32)] * 3,
        compiler_params=_cp(("arbitrary",)),
    )(dh, x, g, sc, dres)


def _gate_bwd(dx, y, gate):
    L = dx.shape[0]
    bm = _row_block(L, 256)

    def body(dx_ref, y_ref, gt_ref, dy_ref, dgt_ref):
        i = pl.program_id(0)
        dxv = dx_ref[...]
        dy_ref[...] = (gt_ref[...] * dxv).astype(dy_ref.dtype)
        part = jnp.sum(dxv * y_ref[...], axis=0, keepdims=True)

        @pl.when(i == 0)
        def _():
            dgt_ref[...] = part

        @pl.when(i > 0)
        def _():
            dgt_ref[...] += part

    row = pl.BlockSpec((bm, D), lambda i: (i, 0))
    vec = pl.BlockSpec((1, D), lambda i: (0, 0))
    return pl.pallas_call(
        body, name="gate_bwd", grid=(L // bm,),
        in_specs=[row, row, vec], out_specs=[row, vec],
        out_shape=[jax.ShapeDtypeStruct((L, D), bf16), jax.ShapeDtypeStruct((1, D), f32)],
        compiler_params=_cp(("arbitrary",)),
    )(dx, y, gate)


def _loss_fwd_bwd(x1, down, gate, target):
    L = x1.shape[0]
    bm = _row_block(L, 256)
    nsteps = L // bm

    def body(x_ref, d_ref, gt_ref, t_ref, dy_ref, loss_ref, acc_ref):
        i = pl.program_id(0)
        diff = x_ref[...] + gt_ref[...] * d_ref[...] - t_ref[...]
        dy_ref[...] = diff * (1.0 / D)
        part = jnp.sum(diff * diff, axis=0, keepdims=True)

        @pl.when(i == 0)
        def _():
            acc_ref[...] = part

        @pl.when(i > 0)
        def _():
            acc_ref[...] += part

        @pl.when(i == nsteps - 1)
        def _():
            tot = jnp.sum(acc_ref[...], axis=1, keepdims=True) * (0.5 / D)
            loss_ref[...] = jnp.broadcast_to(tot, (8, 128))

    row = pl.BlockSpec((bm, D), lambda i: (i, 0))
    vec = pl.BlockSpec((1, D), lambda i: (0, 0))
    dy, loss = pl.pallas_call(
        body, name="loss_fwd_bwd", grid=(nsteps,),
        in_specs=[row, row, vec, row],
        out_specs=[row, pl.BlockSpec((8, 128), lambda i: (0, 0))],
        out_shape=[jax.ShapeDtypeStruct((L, D), f32), jax.ShapeDtypeStruct((8, 128), f32)],
        scratch_shapes=[pltpu.VMEM((1, D), f32)],
        compiler_params=_cp(("arbitrary",)),
    )(x1, down, gate, target)
    return loss[0, 0], dy


def _head_mean_matrix():
    r = lax.broadcasted_iota(jnp.int32, (128, 128), 0) // HD
    c = lax.broadcasted_iota(jnp.int32, (128, 128), 1) // HD
    return jnp.where(r == c, 1.0 / HD, 0.0).astype(bf16)


def _qknorm_fwd(p, gqk):
    L = p.shape[0]
    bm = _row_block(L, 256)

    def body(p_ref, g_ref, o_ref):
        e = _head_mean_matrix()
        for c in range(8):
            sl = slice(128 * c, 128 * (c + 1))
            xv = p_ref[:, sl]
            inv = lax.rsqrt(_dot_exact(xv * xv, e) + EPS)
            yv = xv * inv * g_ref[:, sl]
            if c < 4:
                yv = yv * (1.0 / math.sqrt(HD))
            o_ref[:, sl] = yv.astype(o_ref.dtype)
        o_ref[:, 1024:1536] = p_ref[:, 1024:1536].astype(o_ref.dtype)

    return pl.pallas_call(
        body, name="qknorm_fwd", grid=(L // bm,),
        in_specs=[pl.BlockSpec((bm, 1536), lambda i: (i, 0)), pl.BlockSpec((1, 1024), lambda i: (0, 0))],
        out_specs=pl.BlockSpec((bm, 1536), lambda i: (i, 0)),
        out_shape=jax.ShapeDtypeStruct((L, 1536), bf16),
        compiler_params=_cp(("parallel",)),
    )(p, gqk)


def _qknorm_bwd(dqn, dkn, dval, du, p, gqk):
    L = p.shape[0]
    bm = _row_block(L, 256)
    nsteps = L // bm

    def body(dq_ref, dk_ref, dv_ref, du_ref, p_ref, g_ref, o_ref, dg_ref, acc_ref):
        i = pl.program_id(0)
        e = _head_mean_matrix()
        o_ref[:, 1024:1536] = dv_ref[...].astype(o_ref.dtype)
        o_ref[:, 1536:2048] = du_ref[...].astype(o_ref.dtype)
        for c in range(8):
            sl = slice(128 * c, 128 * (c + 1))
            xv = p_ref[:, sl]
            d_ref = dq_ref if c < 4 else dk_ref
            dv = d_ref[:, 128 * (c % 4):128 * (c % 4 + 1)]
            gv = g_ref[:, sl]
            inv = lax.rsqrt(_dot_exact(xv * xv, e) + EPS)
            xh = xv * inv
            dxh = dv * gv
            o_ref[:, sl] = (inv * (dxh - xh * _dot_exact(dxh * xh, e))).astype(o_ref.dtype)
            part = jnp.sum(dv * xh, axis=0, keepdims=True)

            @pl.when(i == 0)
            def _():
                acc_ref[:, sl] = part

            @pl.when(i > 0)
            def _():
                acc_ref[:, sl] += part

        @pl.when(i == nsteps - 1)
        def _():
            r = lax.broadcasted_iota(jnp.int32, (1024, 128), 0)
            col = lax.broadcasted_iota(jnp.int32, (1024, 128), 1)
            fold = jnp.where(col == (r // 512) * HD + r % HD, 1.0, 0.0).astype(bf16)
            dg_ref[...] = _dot_exact(jnp.broadcast_to(acc_ref[...], (8, 1024)), fold)

    half = pl.BlockSpec((bm, 512), lambda i: (i, 0))
    return pl.pallas_call(
        body, name="qknorm_bwd", grid=(nsteps,),
        in_specs=[half, half, half, half, pl.BlockSpec((bm, 1024), lambda i: (i, 0)),
                  pl.BlockSpec((1, 1024), lambda i: (0, 0))],
        out_specs=[pl.BlockSpec((bm, INW), lambda i: (i, 0)), pl.BlockSpec((8, 128), lambda i: (0, 0))],
        out_shape=[jax.ShapeDtypeStruct((L, INW), bf16), jax.ShapeDtypeStruct((8, 128), f32)],
        scratch_shapes=[pltpu.VMEM((1, 1024), f32)],
        compiler_params=_cp(("arbitrary",)),
    )(dqn, dkn, dval, du, p, gqk)


def _outnorm_fwd(oa, os_, ga, gs):
    L = oa.shape[0]
    bm = _row_block(L, 256)

    def body(a_ref, s_ref, ga_ref, gs_ref, o_ref):
        for x_ref, g_ref, off in ((a_ref, ga_ref, 0), (s_ref, gs_ref, 512)):
            xv = x_ref[...]
            inv = lax.rsqrt(jnp.mean(xv * xv, axis=-1, keepdims=True) + EPS)
            o_ref[:, off:off + 512] = (xv * inv * g_ref[...]).astype(o_ref.dtype)

    half = pl.BlockSpec((bm, 512), lambda i: (i, 0))
    vec = pl.BlockSpec((1, 512), lambda i: (0, 0))
    return pl.pallas_call(
        body, name="outnorm_fwd", grid=(L // bm,),
        in_specs=[half, half, vec, vec], out_specs=pl.BlockSpec((bm, D), lambda i: (i, 0)),
        out_shape=jax.ShapeDtypeStruct((L, D), bf16),
        compiler_params=_cp(("parallel",)),
    )(oa, os_, ga, gs)


def _outnorm_bwd(do, oa, os_, ga, gs):
    L = oa.shape[0]
    bm = _row_block(L, 256)

    def body(do_ref, a_ref, s_ref, ga_ref, gs_ref, da_ref, ds_ref, dga_ref, dgs_ref):
        i = pl.program_id(0)
        for x_ref, g_ref, dx_ref, dg_ref, off in ((a_ref, ga_ref, da_ref, dga_ref, 0),
                                                  (s_ref, gs_ref, ds_ref, dgs_ref, 512)):
            xv = x_ref[...]
            dv = do_ref[:, off:off + 512]
            inv = lax.rsqrt(jnp.mean(xv * xv, axis=-1, keepdims=True) + EPS)
            xh = xv * inv
            dxh = dv * g_ref[...]
            dx_ref[...] = inv * (dxh - xh * jnp.mean(dxh * xh, axis=-1, keepdims=True))
            part = jnp.sum(dv * xh, axis=0, keepdims=True)

            @pl.when(i == 0)
            def _():
                dg_ref[...] = part

            @pl.when(i > 0)
            def _():
                dg_ref[...] += part

    half = pl.BlockSpec((bm, 512), lambda i: (i, 0))
    vec = pl.BlockSpec((1, 512), lambda i: (0, 0))
    return pl.pallas_call(
        body, name="outnorm_bwd", grid=(L // bm,),
        in_specs=[pl.BlockSpec((bm, D), lambda i: (i, 0)), half, half, vec, vec],
        out_specs=[half, half, vec, vec],
        out_shape=[jax.ShapeDtypeStruct((L, 512), f32)] * 2 + [jax.ShapeDtypeStruct((1, 512), f32)] * 2,
        compiler_params=_cp(("arbitrary",)),
    )(do, oa, os_, ga, gs)


NCB = DFF // 128


def _conv_rows(L):
    return 64 if L % 64 == 0 else 32


def _conv_chunk(x_ref, w_ref, b_ref, i, nch, R, L, ahead):
    base = pl.multiple_of(i * R, R)
    prev = x_ref[pl.ds(pl.multiple_of(jnp.maximum(base - 8, 0), 8), 8), :]
    parts = [jnp.where(i > 0, prev, 0.0), x_ref[pl.ds(base, R), :]]
    if ahead:
        nxt = x_ref[pl.ds(pl.multiple_of(jnp.minimum(base + R, L - 8), 8), 8), :]
        parts.append(jnp.where(i < nch - 1, nxt, 0.0))
    xe = jnp.concatenate(parts, axis=0)
    x1 = pltpu.roll(xe, 1, 0)
    x2 = pltpu.roll(xe, 2, 0)
    u = b_ref[...] + x2 * w_ref[0:1, :]
    u = u + x1 * w_ref[1:2, :]
    u = u + xe * w_ref[2:3, :]
    return u[8:], xe[8:], x1[8:], x2[8:]


def _fold8(x):
    out = x[0:8]
    for q in range(1, x.shape[0] // 8):
        out = out + x[8 * q:8 * (q + 1)]
    return out


def _convglu_fwd(up_pre, cw, cb):
    L = up_pre.shape[0]

    def body(v_ref, g_ref, wv_ref, wg_ref, bv_ref, bg_ref, a_ref):
        row = lax.broadcasted_iota(jnp.int32, (L, 128), 0)

        def conv(x_ref, w_ref, b_ref):
            x = x_ref[...]
            out = b_ref[...] + jnp.where(row >= 2, pltpu.roll(x, 2, 0), 0.0) * w_ref[0:1, :]
            out = out + jnp.where(row >= 1, pltpu.roll(x, 1, 0), 0.0) * w_ref[1:2, :]
            return out + x * w_ref[2:3, :]

        val = conv(v_ref, wv_ref, bv_ref)
        a_ref[...] = (_gelu(conv(g_ref, wg_ref, bg_ref)) * val).astype(a_ref.dtype)

    col = lambda off: pl.BlockSpec((L, 128), lambda j: (0, j + off))
    wsp = lambda off: pl.BlockSpec((3, 128), lambda j: (0, j + off))
    bsp = lambda off: pl.BlockSpec((1, 128), lambda j: (0, j + off))
    return pl.pallas_call(
        body, name="convglu_fwd", grid=(NCB,),
        in_specs=[col(0), col(NCB), wsp(0), wsp(NCB), bsp(0), bsp(NCB)],
        out_specs=pl.BlockSpec((L, 128), lambda j: (0, j)),
        out_shape=jax.ShapeDtypeStruct((L, DFF), bf16),
        compiler_params=_cp(("parallel",)),
    )(up_pre, up_pre, cw, cw, cb, cb)


def _convglu_bwd(da, up_pre, cw, cb, comm=None):
    L = up_pre.shape[0]
    R = _conv_rows(L)
    nch = L // R
    n = R + 8

    def body(da_ref, v_ref, g_ref, wv_ref, wg_ref, bv_ref, bg_ref, dx_ref, dw_ref, db_ref):
        def chunk(i, acc):
            base = pl.multiple_of(i * R, R)
            val, xv, xv1, xv2 = _conv_chunk(v_ref, wv_ref, bv_ref, i, nch, R, L, True)
            gate, xg, xg1, xg2 = _conv_chunk(g_ref, wg_ref, bg_ref, i, nch, R, L, True)
            nxt = da_ref[pl.ds(pl.multiple_of(jnp.minimum(base + R, L - 8), 8), 8), :]
            dav = jnp.concatenate([da_ref[pl.ds(base, R), :], jnp.where(i < nch - 1, nxt, 0.0)], axis=0)
            c = math.sqrt(2.0 / math.pi)
            t = jnp.tanh(c * (gate + 0.044715 * (gate * gate * gate)))
            d_val = dav * (0.5 * gate * (1.0 + t))
            d_gate = dav * val * (0.5 * (1.0 + t)
                                  + 0.5 * gate * (1.0 - t * t) * c * (1.0 + 3.0 * 0.044715 * (gate * gate)))
            new = []
            for k, (d, w_ref, xs) in enumerate(((d_val, wv_ref, (xv2, xv1, xv)), (d_gate, wg_ref, (xg2, xg1, xg)))):
                dx = d * w_ref[2:3, :] + pltpu.roll(d, n - 1, 0) * w_ref[1:2, :]
                dx = dx + pltpu.roll(d, n - 2, 0) * w_ref[0:1, :]
                dx_ref[k, pl.ds(base, R), :] = dx[:R].astype(dx_ref.dtype)
                dr = d[:R]
                new += [_fold8(dr * x[:R]) for x in xs] + [_fold8(dr)]
            return tuple(a + b for a, b in zip(acc, new))

        acc = lax.fori_loop(0, nch, chunk, (jnp.zeros((8, 128), f32),) * 8)
        for k in range(2):
            for r in range(3):
                dw_ref[k, r:r + 1, :] = jnp.sum(acc[4 * k + r], axis=0, keepdims=True)
            db_ref[k] = jnp.sum(acc[4 * k + 3], axis=0, keepdims=True)

    col = lambda off: pl.BlockSpec((L, 128), lambda j: (0, j + off))
    wsp = lambda off: pl.BlockSpec((3, 128), lambda j: (0, j + off))
    bsp = lambda off: pl.BlockSpec((1, 128), lambda j: (0, j + off))
    (dx, dw, db), couts = _call(
        body, comm, name="convglu_bwd", grid=(NCB,),
        in_specs=[col(0), col(0), col(NCB), wsp(0), wsp(NCB), bsp(0), bsp(NCB)],
        out_specs=[pl.BlockSpec((2, L, 128), lambda j: (0, 0, j)), pl.BlockSpec((2, 3, 128), lambda j: (0, 0, j)),
                   pl.BlockSpec((2, 1, 128), lambda j: (0, 0, j))],
        out_shape=[jax.ShapeDtypeStruct((2, L, DFF), bf16), jax.ShapeDtypeStruct((2, 3, DFF), f32),
                   jax.ShapeDtypeStruct((2, 1, DFF), f32)], scratch_shapes=[],
        sem=("parallel",), args=(da, up_pre, up_pre, cw, cw, cb, cb))
    return dx, dw.transpose(1, 0, 2).reshape(3, DUP), db.transpose(1, 0, 2).reshape(1, DUP), couts


def _attn_block(L):
    return 256 if L % 256 == 0 else 128


TAIL_DEAD = -110.0


def _dot_exact2(x, m):
    hi = x.astype(bf16)
    lo = (x - hi.astype(f32)).astype(bf16)
    return (lax.dot_general(hi, m, NN, preferred_element_type=f32)
            + lax.dot_general(lo, m, NN, preferred_element_type=f32))


def _sb_weights(z, mask, tri_gt, carry):
    l1p = jnp.log(1.0 + jnp.exp(-jnp.abs(z)))
    ls_pos = jnp.minimum(z, 0.0) - l1p
    lm = ls_pos - z
    if mask is not None:
        lm = jnp.where(mask, lm, 0.0)
    tail = _dot_exact2(lm, tri_gt) + carry
    w = jnp.exp(ls_pos + tail)
    if mask is not None:
        w = jnp.where(mask, w, 0.0)
    return w, ls_pos, lm


def _head_masks(B):
    lane = lax.broadcasted_iota(jnp.int32, (B, 128), 1)
    return lane < HD, lane >= HD


def _attn_fwd(qkv, comm=None):
    L = qkv.shape[0]
    B = _attn_block(L)
    nq = L // B

    def body(q_ref, k_ref, v_ref, o_ref):
        qi = pl.program_id(1)
        heads = _head_masks(B)
        ti = lax.broadcasted_iota(jnp.int32, (B, B), 0)
        si = lax.broadcasted_iota(jnp.int32, (B, B), 1)
        tri_gt = jnp.where(ti > si, 1.0, 0.0).astype(bf16)
        diag = si < ti
        qv = q_ref[...]
        zero16 = jnp.zeros((), bf16)

        def tile(jb, mask, acc, carries):
            ks = pl.multiple_of(jb * B, B)
            kb = k_ref[pl.ds(ks, B), :]
            vb = v_ref[pl.ds(ks, B), :]
            out = []
            for in_head, carry in zip(heads, carries):
                kh = jnp.where(in_head, kb, zero16)
                vh = jnp.where(in_head, vb, zero16)
                z = lax.dot_general(qv, kh, NT, preferred_element_type=f32)
                w, _, lm = _sb_weights(z, mask, tri_gt, carry)
                acc = acc + lax.dot_general(w.astype(bf16), vh, NN, preferred_element_type=f32)
                out.append(carry + jnp.sum(lm, axis=1, keepdims=True))
            return acc, out

        zc = jnp.zeros((B, 1), f32)
        acc, (c0, c1) = tile(qi, diag, jnp.zeros((B, 128), f32), (zc, zc))

        def cond(st):
            return jnp.logical_and(st[0] <= qi, st[4] > TAIL_DEAD)

        def step(st):
            n, a, p0, p1, _ = st
            a, (p0, p1) = tile(qi - n, None, a, (p0, p1))
            return n + 1, a, p0, p1, jnp.maximum(jnp.max(p0), jnp.max(p1))

        st = lax.while_loop(cond, step, (jnp.int32(1), acc, c0, c1, jnp.maximum(jnp.max(c0), jnp.max(c1))))
        o_ref[...] = st[1]

    (o,), couts = _call(
        body, comm, name="attn_fwd", grid=(4, nq),
        in_specs=[pl.BlockSpec((B, 128), lambda hp, i: (i, hp)),
                  pl.BlockSpec((L, 128), lambda hp, i: (0, 4 + hp)),
                  pl.BlockSpec((L, 128), lambda hp, i: (0, 8 + hp))],
        out_specs=[pl.BlockSpec((B, 128), lambda hp, i: (i, hp))],
        out_shape=[jax.ShapeDtypeStruct((L, ATT), f32)], scratch_shapes=[],
        sem=("parallel", "parallel"), args=(qkv, qkv, qkv))
    return o, couts


def _attn_bwd(qkv, o, do, comm=None):
    L = qkv.shape[0]
    B = _attn_block(L)
    nq = L // B

    def body(q_ref, k_ref, v_ref, o_ref, do_ref, dq_ref, dk_ref, dv_ref):
        qi = pl.program_id(1)

        @pl.when(qi == 0)
        def _():
            dk_ref[...] = jnp.zeros_like(dk_ref)
            dv_ref[...] = jnp.zeros_like(dv_ref)

        heads = _head_masks(B)
        ti = lax.broadcasted_iota(jnp.int32, (B, B), 0)
        si = lax.broadcasted_iota(jnp.int32, (B, B), 1)
        tri_gt = jnp.where(ti > si, 1.0, 0.0).astype(bf16)
        tri_ge = jnp.where(ti >= si, 1.0, 0.0).astype(bf16)
        diag = si < ti
        zero16 = jnp.zeros((), bf16)
        qv = q_ref[...]
        do16 = do_ref[...].astype(bf16)
        dsum_lanes = do16.astype(f32) * o_ref[...]
        qhs = [jnp.where(m, qv, zero16) for m in heads]
        dohs = [jnp.where(m, do16, zero16) for m in heads]
        dsums = [jnp.sum(jnp.where(m, dsum_lanes, 0.0), axis=1, keepdims=True) for m in heads]

        def tile(jb, mask, dq_a, state):
            ks = pl.multiple_of(jb * B, B)
            kb = k_ref[pl.ds(ks, B), :]
            vb = v_ref[pl.ds(ks, B), :]
            dk_blk = jnp.zeros((B, 128), f32)
            dv_blk = jnp.zeros((B, 128), f32)
            out = []
            for in_head, qh, doh, dsum, (carry, suffix) in zip(heads, qhs, dohs, dsums, state):
                kh = jnp.where(in_head, kb, zero16)
                vh = jnp.where(in_head, vb, zero16)
                z = lax.dot_general(qv, kh, NT, preferred_element_type=f32)
                w, ls_pos, lm = _sb_weights(z, mask, tri_gt, carry)
                w16 = w.astype(bf16)
                dw = lax.dot_general(doh, vh, NT, preferred_element_type=f32)
                da = w16.astype(f32) * dw
                sig = jnp.exp(ls_pos)
                suf = _dot_exact2(da, tri_ge) + suffix
                dz = da * (1.0 - sig) - sig * (dsum - suf)
                if mask is not None:
                    dz = jnp.where(mask, dz, 0.0)
                dz16 = dz.astype(bf16)
                dq_a = dq_a + lax.dot_general(dz16, kh, NN, preferred_element_type=f32)
                dk_blk = dk_blk + lax.dot_general(dz16, qh, TN, preferred_element_type=f32)
                dv_blk = dv_blk + lax.dot_general(w16, doh, TN, preferred_element_type=f32)
                out.append((carry + jnp.sum(lm, axis=1, keepdims=True),
                            suffix + jnp.sum(da, axis=1, keepdims=True)))
            dk_ref[pl.ds(ks, B), :] += dk_blk
            dv_ref[pl.ds(ks, B), :] += dv_blk
            return dq_a, out

        def alive(state):
            return jnp.maximum(jnp.max(state[0][0]), jnp.max(state[1][0]))

        zc = jnp.zeros((B, 1), f32)
        dq_acc, state = tile(qi, diag, jnp.zeros((B, 128), f32), ((zc, zc), (zc, zc)))

        def cond(st):
            return jnp.logical_and(st[0] <= qi, st[6] > TAIL_DEAD)

        def step(st):
            n, a, c0, s0, c1, s1, _ = st
            a, new = tile(qi - n, None, a, ((c0, s0), (c1, s1)))
            return n + 1, a, new[0][0], new[0][1], new[1][0], new[1][1], alive(new)

        st = lax.while_loop(cond, step, (jnp.int32(1), dq_acc, state[0][0], state[0][1],
                                         state[1][0], state[1][1], alive(state)))
        dq_ref[...] = st[1] * (1.0 / math.sqrt(HD))

    blk = pl.BlockSpec((B, 128), lambda hp, i: (i, hp))
    return _call(
        body, comm, name="attn_bwd", grid=(4, nq),
        in_specs=[blk,
                  pl.BlockSpec((L, 128), lambda hp, i: (0, 4 + hp)),
                  pl.BlockSpec((L, 128), lambda hp, i: (0, 8 + hp)),
                  blk, blk],
        out_specs=[blk, pl.BlockSpec((L, 128), lambda hp, i: (0, hp)),
                   pl.BlockSpec((L, 128), lambda hp, i: (0, hp))],
        out_shape=[jax.ShapeDtypeStruct((L, ATT), f32)] * 3, scratch_shapes=[],
        sem=("parallel", "arbitrary"), args=(qkv, qkv, qkv, o, do))


def _s5_disc(ldt, ar, ai, br, bi):
    dt = jnp.exp(ldt)
    mag = jnp.exp(dt * ar)
    abar_re = mag * jnp.cos(dt * ai)
    abar_im = mag * jnp.sin(dt * ai)
    em_re = abar_re - 1.0
    em_im = abar_im
    den = ar * ar + ai * ai
    f_re = (em_re * ar + em_im * ai) / den
    f_im = (em_im * ar - em_re * ai) / den
    bb_re = f_re * br - f_im * bi
    bb_im = f_re * bi + f_im * br
    return abar_re, abar_im, bb_re, bb_im


def _s5_disc_fwd(ldt, ar, ai, br, bi):
    def body(ldt_ref, ar_ref, ai_ref, br_ref, bi_ref, o1, o2, o3, o4):
        outs = _s5_disc(ldt_ref[...], ar_ref[...], ai_ref[...], br_ref[...], bi_ref[...])
        for o_ref, v in zip((o1, o2, o3, o4), outs):
            o_ref[...] = v

    col = jax.ShapeDtypeStruct((NS, 1), f32)
    mat = jax.ShapeDtypeStruct((NS, NH16), f32)
    return pl.pallas_call(body, name="s5_disc_fwd", out_shape=[col, col, mat, mat],
                          compiler_params=_cp())(ldt, ar, ai, br, bi)


def _s5_disc_bwd(ldt, ar, ai, br, bi, d_are, d_aim, d_bbre, d_bbim):
    def body(ldt_ref, ar_ref, ai_ref, br_ref, bi_ref, c1, c2, c3, c4, o_ldt, o_ar, o_ai, o_br, o_bi):
        prim = (ldt_ref[...], ar_ref[...], ai_ref[...], br_ref[...], bi_ref[...])
        _, vjp = jax.vjp(_s5_disc, *prim)
        g_ldt, g_ar, g_ai, g_br, g_bi = vjp((c1[...], c2[...], c3[...], c4[...]))
        o_ar[...] = g_ar
        o_ai[...] = g_ai
        o_br[...] = g_br
        o_bi[...] = g_bi
        r = lax.broadcasted_iota(jnp.int32, (NG, NS), 0)
        c = lax.broadcasted_iota(jnp.int32, (NG, NS), 1)
        fold = jnp.where(c // NP == r, 1.0, 0.0).astype(bf16)
        hi = jnp.broadcast_to(g_ldt, (NS, 128))
        p1 = hi.astype(bf16)
        r1 = hi - p1.astype(f32)
        p2 = r1.astype(bf16)
        p3 = (r1 - p2.astype(f32)).astype(bf16)
        dd = lambda p: lax.dot_general(fold, p, NN, preferred_element_type=f32)
        o_ldt[...] = dd(p1) + dd(p2) + dd(p3)

    col = jax.ShapeDtypeStruct((NS, 1), f32)
    mat = jax.ShapeDtypeStruct((NS, NH16), f32)
    return pl.pallas_call(
        body, name="s5_disc_bwd",
        out_shape=[jax.ShapeDtypeStruct((NG, 128), f32), col, col, mat, mat],
        compiler_params=_cp())(ldt, ar, ai, br, bi, d_are, d_aim, d_bbre, d_bbim)


def _cmul(ar, ai, br, bi):
    return ar * br - ai * bi, ar * bi + ai * br


def _s5_chunk(L):
    return 256 if L % 256 == 0 else L


def _diag_fold(acc, rpg):
    aw = acc.shape[0]
    r = lax.broadcasted_iota(jnp.int32, (aw, 128), 0) // rpg
    c = lax.broadcasted_iota(jnp.int32, (aw, 128), 1) // NH16
    fr = lax.broadcasted_iota(jnp.int32, (128, NH16), 0) % NH16
    fc = lax.broadcasted_iota(jnp.int32, (128, NH16), 1)
    fold = jnp.where(fr == fc, 1.0, 0.0).astype(bf16)
    return _dot_exact(jnp.where(r == c, acc, 0.0), fold)


def _diag_fold_t(acc):
    r = lax.broadcasted_iota(jnp.int32, (128, 8 * NP), 0) // NH16
    c = lax.broadcasted_iota(jnp.int32, (128, 8 * NP), 1) // NP
    fr = lax.broadcasted_iota(jnp.int32, (128, NH16), 0) % NH16
    fc = lax.broadcasted_iota(jnp.int32, (128, NH16), 1)
    fold = jnp.where(fr == fc, 1.0, 0.0).astype(bf16)
    x = jnp.where(r == c, acc, 0.0)
    hi = x.astype(bf16)
    r1 = x - hi.astype(f32)
    mid = r1.astype(bf16)
    lo = (r1 - mid.astype(f32)).astype(bf16)
    d = lambda piece: lax.dot_general(piece, fold, TN, preferred_element_type=f32)
    return d(hi) + d(mid) + d(lo)


NTILE = NS // 128
TILE_GROUP = 4


def _seg_entry(lr, li, ar, ai, cr, ci, seg_rows, reverse, row):
    pr, pi = ar, ai
    for _ in range(seg_rows.bit_length() - 1):
        pr, pi = _cmul(pr, pi, pr, pi)
    if reverse:
        xr = jnp.where(row == 7, cr, pltpu.roll(lr, 7, 0))
        xi = jnp.where(row == 7, ci, pltpu.roll(li, 7, 0))
    else:
        xr = jnp.where(row == 0, cr, pltpu.roll(lr, 1, 0))
        xi = jnp.where(row == 0, ci, pltpu.roll(li, 1, 0))
    for d in (1, 2, 4):
        keep = (row < 8 - d) if reverse else (row >= d)
        shift = 8 - d if reverse else d
        mr, mi = _cmul(pr, pi, jnp.where(keep, pltpu.roll(xr, shift, 0), 0.0),
                       jnp.where(keep, pltpu.roll(xi, shift, 0), 0.0))
        xr, xi = xr + mr, xi + mi
        pr, pi = _cmul(pr, pi, pr, pi)
    return xr, xi


def _seg_scan(xr3, xi3, abar_re, abar_im, cr_ref, ci_ref, reverse, on_step=None):
    pitch = xr3.shape[1] // 8
    seg = pitch - 8
    row = lax.broadcasted_iota(jnp.int32, (8, 128), 0)
    for g0 in range(0, NTILE, TILE_GROUP):
        tiles = list(range(g0, g0 + TILE_GROUP))
        lanes = [slice(128 * j, 128 * (j + 1)) for j in tiles]
        ar = [jnp.broadcast_to(abar_re[:, ls], (8, 128)) for ls in lanes]
        ai = [jnp.broadcast_to(-abar_im[:, ls] if reverse else abar_im[:, ls], (8, 128)) for ls in lanes]

        def sweep(store):
            def step(n, st):
                k = (seg - 1 - n) if reverse else n
                rows = pl.ds(k, 8, stride=pitch)
                new = []
                for q, j in enumerate(tiles):
                    sr, si = st[2 * q], st[2 * q + 1]
                    if store and on_step is not None:
                        on_step(j, rows, sr, si)
                    nr = ar[q] * sr - ai[q] * si + xr3.at[j][rows, :]
                    ni = ar[q] * si + ai[q] * sr + xi3.at[j][rows, :]
                    if store:
                        xr3.at[j][rows, :] = nr
                        xi3.at[j][rows, :] = ni
                    new += [nr, ni]
                return tuple(new)
            return step

        zero = jnp.zeros((8, 128), f32)
        ends = lax.fori_loop(0, seg, sweep(False), (zero,) * (2 * TILE_GROUP))
        init = []
        for q, ls in enumerate(lanes):
            init += _seg_entry(ends[2 * q], ends[2 * q + 1], ar[q], ai[q], cr_ref[:, ls], ci_ref[:, ls],
                               seg, reverse, row)
        fin = lax.fori_loop(0, seg, sweep(True), tuple(init))
        edge = slice(0, 1) if reverse else slice(7, 8)
        for q, ls in enumerate(lanes):
            cr_ref[:, ls] = fin[2 * q][edge, :]
            ci_ref[:, ls] = fin[2 * q + 1][edge, :]


def _coarse(i):
    return slice(128 * i, 128 * (i + 1)), slice(512 * i, 512 * (i + 1))


def _seg_pitch(T):
    return T // 8 + 8


def _put_rows(x3, j, val):
    pitch = x3.shape[1] // 8
    seg = pitch - 8
    for s in range(8):
        x3[j, pitch * s:pitch * s + seg, :] = val[seg * s:seg * (s + 1), :]


def _get_rows(x3, j):
    pitch = x3.shape[1] // 8
    seg = pitch - 8
    return jnp.concatenate([x3[j, pitch * s:pitch * s + seg, :] for s in range(8)], axis=0)


def _tiles_cat(x3, i):
    return jnp.concatenate([_get_rows(x3, 4 * i + q) for q in range(4)], axis=1)


def _s5_gate2(sr3, si3, u, ccr_ref, cci_ref, dsk_ref, gw_ref, gb_ref):
    ys = []
    for i in range(4):
        ch, st = _coarse(i)
        ys.append(_dot(_tiles_cat(sr3, i), ccr_ref[st, ch]) - _dot(_tiles_cat(si3, i), cci_ref[st, ch]))
    y = jnp.concatenate(ys, axis=1) + dsk_ref[...] * u
    yg = _gelu(y)
    gl = jnp.concatenate([_dot(yg[:, _coarse(i)[0]], gw_ref[_coarse(i)[0], _coarse(i)[0]]) for i in range(4)],
                         axis=1)
    return y, yg, _sigmoid(gl + gb_ref[...])


def _s5_fwd2(p, abar_re, abar_im, bbr, bbi, ccr, cci, dsk, gw, gb):
    L = p.shape[0]
    T = _s5_chunk(L)

    def body(u_ref, are_ref, aim_ref, bbr_ref, bbi_ref, ccr_ref, cci_ref, dsk_ref, gw_ref, gb_ref,
             o_ref, sr3, si3, cr_ref, ci_ref):
        @pl.when(pl.program_id(0) == 0)
        def _():
            cr_ref[...] = jnp.zeros_like(cr_ref)
            ci_ref[...] = jnp.zeros_like(ci_ref)

        u = u_ref[...]
        for i in range(4):
            ch, st = _coarse(i)
            br = _dot(u[:, ch], bbr_ref[ch, st])
            bi = _dot(u[:, ch], bbi_ref[ch, st])
            for q in range(4):
                _put_rows(sr3, 4 * i + q, br[:, 128 * q:128 * (q + 1)])
                _put_rows(si3, 4 * i + q, bi[:, 128 * q:128 * (q + 1)])
        _seg_scan(sr3, si3, are_ref[...], aim_ref[...], cr_ref, ci_ref, reverse=False)
        _, yg, gate = _s5_gate2(sr3, si3, u, ccr_ref, cci_ref, dsk_ref, gw_ref, gb_ref)
        o_ref[...] = yg * gate

    full = lambda shape: pl.BlockSpec(shape, lambda i: (0, 0))
    rows = 8 * _seg_pitch(T)
    st3 = pl.BlockSpec((NTILE, rows, 128), lambda i: (0, i, 0))
    return pl.pallas_call(
        body, name="s5_fwd", grid=(L // T,),
        in_specs=[pl.BlockSpec((T, SSMW), lambda i: (i, 3)), full((1, NS)), full((1, NS)),
                  full((SSMW, NS)), full((SSMW, NS)), full((NS, SSMW)), full((NS, SSMW)),
                  full((1, SSMW)), full((SSMW, SSMW)), full((1, SSMW))],
        out_specs=[pl.BlockSpec((T, SSMW), lambda i: (i, 0)), st3, st3],
        out_shape=[jax.ShapeDtypeStruct((L, SSMW), f32)]
                  + [jax.ShapeDtypeStruct((NTILE, (L // T) * rows, 128), f32)] * 2,
        scratch_shapes=[pltpu.VMEM((1, NS), f32), pltpu.VMEM((1, NS), f32)],
        compiler_params=_cp(("arbitrary",)),
    )(p, abar_re, abar_im, bbr, bbi, ccr, cci, dsk, gw, gb)


def _s5_bwd2(dout, p, s_re, s_im, abar_re, abar_im, bbr, bbi, ccr, cci, dsk, gw, gb):
    L = p.shape[0]
    T = _s5_chunk(L)
    nchunk = L // T

    def body(do_ref, u_ref, sr3, si3, are_ref, aim_ref, bbr_ref, bbi_ref, ccr_ref, cci_ref,
             dsk_ref, gw_ref, gb_ref,
             du_ref, dare_ref, daim_ref, dd_ref, dgb_ref, dccr_ref, dcci_ref, dbbr_ref, dbbi_ref, dgw_ref,
             lr3, li3, cr_ref, ci_ref, accr3, acci3, wacc_ref, gacc_ref):
        i = pl.program_id(0)

        @pl.when(i == 0)
        def _():
            for ref in (cr_ref, ci_ref, accr3, acci3, dd_ref, dgb_ref, wacc_ref, gacc_ref):
                ref[...] = jnp.zeros_like(ref)

        u = u_ref[...]
        dov = do_ref[...]
        y, yg, gate = _s5_gate2(sr3, si3, u, ccr_ref, cci_ref, dsk_ref, gw_ref, gb_ref)
        dgl = dov * yg * gate * (1.0 - gate)
        dyg = dov * gate + jnp.concatenate(
            [_dot(dgl[:, _coarse(q)[0]], gw_ref[_coarse(q)[0], _coarse(q)[0]], NT) for q in range(4)], axis=1)
        dy = dyg * _gelu_grad(y)
        u16, dy16, yg16, dgl16 = (t.astype(_MXU) for t in (u, dy, yg, dgl))
        dd_ref[...] += jnp.sum(dy * u, axis=0, keepdims=True)
        dgb_ref[...] += jnp.sum(dgl, axis=0, keepdims=True)
        for q in range(4):
            ch, st = _coarse(q)
            wacc_ref[0, q] += _dot(dy16[:, ch], _tiles_cat(sr3, q), TN)
            wacc_ref[1, q] += _dot(dy16[:, ch], _tiles_cat(si3, q), TN)
            gacc_ref[q] += _dot(yg16[:, ch], dgl16[:, ch], TN)
            gr = _dot(dy16[:, ch], ccr_ref[st, ch], NT)
            gi = -_dot(dy16[:, ch], cci_ref[st, ch], NT)
            for t in range(4):
                _put_rows(lr3, 4 * q + t, gr[:, 128 * t:128 * (t + 1)])
                _put_rows(li3, 4 * q + t, gi[:, 128 * t:128 * (t + 1)])

        def on_step(j, rows, nxt_r, nxt_i):
            s_r = sr3.at[j][rows, :]
            s_i = si3.at[j][rows, :]
            accr3[j] += nxt_r * s_r + nxt_i * s_i
            acci3[j] += nxt_i * s_r - nxt_r * s_i

        _seg_scan(lr3, li3, are_ref[...], aim_ref[...], cr_ref, ci_ref, reverse=True, on_step=on_step)
        dus = []
        for q in range(4):
            ch, st = _coarse(q)
            lam_r, lam_i = _tiles_cat(lr3, q), _tiles_cat(li3, q)
            wacc_ref[2, q] += _dot(u16[:, ch], lam_r, TN)
            wacc_ref[3, q] += _dot(u16[:, ch], lam_i, TN)
            dus.append(_dot(lam_r, bbr_ref[ch, st], NT) + _dot(lam_i, bbi_ref[ch, st], NT))
        du_ref[...] = dy * dsk_ref[...] + jnp.concatenate(dus, axis=1)

        @pl.when(i == nchunk - 1)
        def _():
            for j in range(NTILE):
                ls = slice(128 * j, 128 * (j + 1))
                dare_ref[:, ls] = jnp.sum(accr3[j], axis=0, keepdims=True)
                daim_ref[:, ls] = jnp.sum(acci3[j], axis=0, keepdims=True)
            for m, o_ref in enumerate((dccr_ref, dcci_ref, dbbr_ref, dbbi_ref)):
                for q in range(4):
                    o_ref[512 * q:512 * (q + 1), :] = _diag_fold_t(wacc_ref[m, q])
            for q in range(4):
                dgw_ref[128 * q:128 * (q + 1), :] = _diag_fold(gacc_ref[q], NH16)

    rev = lambda i: (nchunk - 1 - i, 0)
    full = lambda shape: pl.BlockSpec(shape, lambda i: (0, 0))
    half = pl.BlockSpec((T, SSMW), rev)
    rows = 8 * _seg_pitch(T)
    st3 = pl.BlockSpec((NTILE, rows, 128), lambda i: (0, nchunk - 1 - i, 0))
    return pl.pallas_call(
        body, name="s5_bwd", grid=(nchunk,),
        in_specs=[half, pl.BlockSpec((T, SSMW), lambda i: (nchunk - 1 - i, 3)), st3, st3,
                  full((1, NS)), full((1, NS)), full((SSMW, NS)), full((SSMW, NS)),
                  full((NS, SSMW)), full((NS, SSMW)), full((1, SSMW)), full((SSMW, SSMW)), full((1, SSMW))],
        out_specs=[half, full((1, NS)), full((1, NS)), full((1, SSMW)), full((1, SSMW)),
                   full((NS, NH16)), full((NS, NH16)), full((NS, NH16)), full((NS, NH16)),
                   full((SSMW, NH16))],
        out_shape=[jax.ShapeDtypeStruct((L, SSMW), f32)] + [jax.ShapeDtypeStruct((1, NS), f32)] * 2
                  + [jax.ShapeDtypeStruct((1, SSMW), f32)] * 2 + [jax.ShapeDtypeStruct((NS, NH16), f32)] * 4
                  + [jax.ShapeDtypeStruct((SSMW, NH16), f32)],
        scratch_shapes=[pltpu.VMEM((NTILE, rows, 128), f32), pltpu.VMEM((NTILE, rows, 128), f32),
                        pltpu.VMEM((1, NS), f32), pltpu.VMEM((1, NS), f32),
                        pltpu.VMEM((NTILE, 8, 128), f32), pltpu.VMEM((NTILE, 8, 128), f32),
                        pltpu.VMEM((4, 4, 128, 512), f32), pltpu.VMEM((4, 128, 128), f32)],
        compiler_params=_cp(("arbitrary",)),
    )(dout, p, s_re, s_im, abar_re, abar_im, bbr, bbi, ccr, cci, dsk, gw, gb)


def _block_diag(x):
    g, r, c = x.shape
    eye = jnp.eye(g, dtype=x.dtype)
    return (x[:, :, None, :] * eye[:, None, :, None]).reshape(g * r, g * c)


def _mod_fwd(c16, ada_w, ada_b_loc):
    nloc = ada_w.shape[2]

    def body(c_ref, w_ref, b_ref, o_ref):
        cv = c_ref[...]
        act = cv * _sigmoid(cv)
        o_ref[...] = _dot(act, w_ref[...]) + b_ref[...]

    return pl.pallas_call(
        body, name="mod_fwd", grid=(DEPTH,),
        in_specs=[pl.BlockSpec((16, D), lambda l: (0, 0)), pl.BlockSpec((None, D, nloc), lambda l: (l, 0, 0)),
                  pl.BlockSpec((None, 1, nloc), lambda l: (l, 0, 0))],
        out_specs=pl.BlockSpec((None, 16, nloc), lambda l: (l, 0, 0)),
        out_shape=jax.ShapeDtypeStruct((DEPTH, 16, nloc), f32),
        compiler_params=_cp(("parallel",)),
    )(c16, ada_w, ada_b_loc)


def _ada_grad(c16, dmod16):
    nloc = dmod16.shape[2]

    def body(c_ref, d_ref, o_ref):
        cv = c_ref[...]
        act = cv * _sigmoid(cv)
        o_ref[...] = _dot(act, d_ref[...], TN)

    return pl.pallas_call(
        body, name="ada_grad", grid=(DEPTH,),
        in_specs=[pl.BlockSpec((16, D), lambda l: (0, 0)), pl.BlockSpec((None, 16, nloc), lambda l: (l, 0, 0))],
        out_specs=pl.BlockSpec((None, D, nloc), lambda l: (l, 0, 0)),
        out_shape=jax.ShapeDtypeStruct((DEPTH, D, nloc), f32),
        compiler_params=_cp(("parallel",)),
    )(c16, dmod16)


def _as2d(a):
    return a.reshape(-1, a.shape[-1])


def _ew_rows(rows):
    for cand in (512, 256, 128, 64, 32, 16, 8):
        if rows % cand == 0:
            return cand
    return rows


def _cast_bf16(w):
    w2 = _as2d(w)
    rows, cols = w2.shape
    bm = _ew_rows(rows)

    def body(x_ref, o_ref):
        o_ref[...] = x_ref[...].astype(bf16)

    spec = pl.BlockSpec((bm, cols), lambda i: (i, 0))
    out = pl.pallas_call(body, name="cast_bf16", grid=(rows // bm,), in_specs=[spec], out_specs=spec,
                         out_shape=jax.ShapeDtypeStruct((rows, cols), bf16),
                         compiler_params=_cp(("parallel",)))(w2)
    return out.reshape(w.shape)


def _adamw(parts, w, m, v):
    n = parts.shape[0]
    w2, m2, v2 = _as2d(w), _as2d(m), _as2d(v)
    rows, cols = w2.shape
    p3 = parts.reshape(n, rows, cols)
    bm = _ew_rows(rows)
    if bm * cols * 4 > (1 << 20) and bm % 16 == 0:
        bm //= 2

    def body(p_ref, w_ref, m_ref, v_ref, g_ref, d_ref, nm_ref, nv_ref):
        g = p_ref[0]
        for k in range(1, n):
            g = g + p_ref[k]
        mn = ADAM_B1 * m_ref[...] + (1.0 - ADAM_B1) * g
        vn = ADAM_B2 * v_ref[...] + (1.0 - ADAM_B2) * (g * g)
        m_hat = mn / (1.0 - ADAM_B1 ** ADAM_STEP)
        v_hat = vn / (1.0 - ADAM_B2 ** ADAM_STEP)
        g_ref[...] = g
        d_ref[...] = -ADAM_LR * (m_hat / (jnp.sqrt(v_hat) + ADAM_EPS) + ADAM_WD * w_ref[...])
        nm_ref[...] = mn
        nv_ref[...] = vn

    spec = pl.BlockSpec((bm, cols), lambda i: (i, 0))
    outs = pl.pallas_call(
        body, name="adamw", grid=(rows // bm,),
        in_specs=[pl.BlockSpec((n, bm, cols), lambda i: (0, i, 0)), spec, spec, spec],
        out_specs=[spec] * 4, out_shape=[jax.ShapeDtypeStruct((rows, cols), f32)] * 4,
        compiler_params=_cp(("parallel",)),
    )(p3, w2, m2, v2)
    return tuple(o.reshape(w.shape) for o in outs)


ANY = pl.BlockSpec(memory_space=pl.ANY)


def _my_pos():
    return lax.axis_index("x"), lax.axis_index("y"), lax.axis_index("c")


def _allgather8(name, x):
    def body(x_ref, out_ref, send_sems, recv_sems):
        mx, my, mc = _my_pos()
        me, sibling = (mx, my, mc), (mx, my, 1 - mc)
        chips = [(1 - mx, my), (mx, 1 - my), (1 - mx, 1 - my)]

        def slot(px, py, pc):
            return out_ref.at[4 * px + 2 * py + pc]

        def copy(k, block, to, src=None):
            return pltpu.make_async_remote_copy(
                src_ref=slot(*block) if src is None else src, dst_ref=slot(*block),
                send_sem=send_sems.at[k], recv_sem=recv_sems.at[k], device_id=to, device_id_type=MESH)

        first = [copy(0, me, sibling, src=x_ref)]
        first += [copy(1 + j, me, (*chip, mc), src=x_ref) for j, chip in enumerate(chips)]
        for cp in first:
            cp.start()
        passed = [copy(4 + j, (*chip, mc), sibling) for j, chip in enumerate(chips)]
        for j, chip in enumerate(chips):
            copy(1 + j, (*chip, mc), me).wait_recv()
            passed[j].start()
        copy(0, sibling, me).wait_recv()
        for j, chip in enumerate(chips):
            copy(4 + j, (*chip, 1 - mc), me).wait_recv()
        for cp in first + passed:
            cp.wait_send()

    out = pl.pallas_call(
        body, name=name, in_specs=[ANY], out_specs=ANY,
        out_shape=jax.ShapeDtypeStruct((NDEV,) + x.shape, x.dtype),
        scratch_shapes=[pltpu.SemaphoreType.DMA((7,)), pltpu.SemaphoreType.DMA((7,))],
    )(x)
    dev = 4 * lax.axis_index("x") + 2 * lax.axis_index("y") + lax.axis_index("c")
    return lax.dynamic_update_index_in_dim(out, x, dev, 0)


def _sum_parts(parts):
    n = parts.shape[0]
    p3 = parts.reshape(n, -1, parts.shape[-1])
    rows, cols = p3.shape[1:]
    bm = _ew_rows(rows)
    if bm * cols * 4 > (1 << 20) and bm % 16 == 0:
        bm //= 2

    def body(p_ref, o_ref):
        g = p_ref[0].astype(f32)
        for k in range(1, n):
            g = g + p_ref[k].astype(f32)
        o_ref[...] = g

    out = pl.pallas_call(
        body, name="sum_parts", grid=(rows // bm,),
        in_specs=[pl.BlockSpec((n, bm, cols), lambda i: (0, i, 0))],
        out_specs=pl.BlockSpec((bm, cols), lambda i: (i, 0)),
        out_shape=jax.ShapeDtypeStruct((rows, cols), f32),
        compiler_params=_cp(("parallel",)),
    )(p3)
    return out.reshape(parts.shape[1:])


def _rc(src, dst, ss, rs, k, to):
    return pltpu.make_async_remote_copy(src_ref=src, dst_ref=dst, send_sem=ss.at[k], recv_sem=rs.at[k],
                                        device_id=to, device_id_type=MESH)


def _half_rows(rows, c):
    return pl.ds((rows // 2) * c, rows // 2)


def _gather_comm(shards, l):
    n = len(shards)

    def plan(ins, outs, ss, rs):
        mx, my, mc = _my_pos()
        chips = [(1 - mx, my), (mx, 1 - my), (1 - mx, 1 - my)]
        mine = 2 * mx + my
        sibling = (mx, my, 1 - mc)
        sends, lands, fwds, fwd_lands, own = [], [], [], [], []
        for a in range(n):
            src = ins[a].at[l]
            rows = ins[a].shape[1]
            mr, sr = _half_rows(rows, mc), _half_rows(rows, 1 - mc)
            for j, (px, py) in enumerate(chips):
                k, src_chip = 3 * a + j, 2 * px + py
                sends.append(_rc(src.at[mr], outs[a].at[mine, mr], ss, rs, k, (px, py, mc)))
                lands.append(_rc(src.at[mr], outs[a].at[src_chip, mr], ss, rs, k, (px, py, mc)))
                blk, sblk = outs[a].at[src_chip, mr], outs[a].at[src_chip, sr]
                fwds.append(_rc(blk, blk, ss, rs, 3 * n + k, sibling))
                fwd_lands.append(_rc(sblk, sblk, ss, rs, 3 * n + k, sibling))
            own.append(_rc(src, outs[a].at[mine], ss, rs, 6 * n + a, sibling))
        return sends, lands, fwds, fwd_lands, own

    def start(ins, outs, ss, rs):
        sends, _, _, _, own = plan(ins, outs, ss, rs)
        for cp in sends + own:
            cp.start()

    def finish(ins, outs, ss, rs):
        sends, lands, fwds, fwd_lands, own = plan(ins, outs, ss, rs)
        for land, fwd in zip(lands, fwds):
            land.wait_recv()
            fwd.start()
        for cp in fwd_lands:
            cp.wait_recv()
        for cp in sends + fwds:
            cp.wait_send()
        for cp in own:
            cp.wait()

    return dict(ins=list(shards), nsem=7 * n, start=start, finish=finish,
                out_shapes=[jax.ShapeDtypeStruct((NCHIP,) + a.shape[1:], a.dtype) for a in shards])


def _swap_comm(grads):
    n = len(grads)

    def plan(ins, outs, ss, rs):
        mx, my, mc = _my_pos()
        return [_rc(ins[a].at[j, _half_rows(ins[a].shape[1], 1 - mc)], outs[a].at[j], ss, rs,
                    NCHIP * a + j, (mx, my, 1 - mc)) for a in range(n) for j in range(NCHIP)]

    def start(ins, outs, ss, rs):
        for cp in plan(ins, outs, ss, rs):
            cp.start()

    def finish(ins, outs, ss, rs):
        for cp in plan(ins, outs, ss, rs):
            cp.wait()

    return dict(ins=list(grads), nsem=NCHIP * n, start=start, finish=finish,
                out_shapes=[jax.ShapeDtypeStruct((NCHIP, a.shape[1] // 2) + a.shape[2:], a.dtype)
                            for a in grads])


def _scatter_comm(sums):
    n = len(sums)

    def plan(ins, outs, ss, rs):
        mx, my, mc = _my_pos()
        chips = [(1 - mx, my), (mx, 1 - my), (1 - mx, 1 - my)]
        mine = 2 * mx + my
        sends, lands = [], []
        for a in range(n):
            for j, (px, py) in enumerate(chips):
                sends.append(_rc(ins[a].at[2 * px + py], outs[a].at[mine], ss, rs, 3 * a + j, (px, py, mc)))
                lands.append(_rc(ins[a].at[mine], outs[a].at[2 * px + py], ss, rs, 3 * a + j, (px, py, mc)))
        return sends, lands

    def start(ins, outs, ss, rs):
        for cp in plan(ins, outs, ss, rs)[0]:
            cp.start()

    def finish(ins, outs, ss, rs):
        sends, lands = plan(ins, outs, ss, rs)
        for cp in lands:
            cp.wait_recv()
        for cp in sends:
            cp.wait_send()

    return dict(ins=list(sums), nsem=3 * n, start=start, finish=finish,
                out_shapes=[jax.ShapeDtypeStruct(a.shape, a.dtype) for a in sums])


def _scatter_done(outs, sums):
    mine = 2 * lax.axis_index("x") + lax.axis_index("y")
    return [lax.dynamic_update_index_in_dim(o, lax.dynamic_index_in_dim(a, mine, 0, keepdims=False), mine, 0)
            for o, a in zip(outs, sums)]


def _merge_comm(halves):
    n = len(halves)

    def plan(ins, outs, ss, rs):
        mx, my, mc = _my_pos()
        return [_rc(ins[a], outs[a].at[_half_rows(outs[a].shape[0], mc)], ss, rs, a, (mx, my, 1 - mc))
                for a in range(n)]

    def start(ins, outs, ss, rs):
        for cp in plan(ins, outs, ss, rs):
            cp.start()

    def finish(ins, outs, ss, rs):
        for cp in plan(ins, outs, ss, rs):
            cp.wait()

    return dict(ins=list(halves), nsem=n, start=start, finish=finish,
                out_shapes=[jax.ShapeDtypeStruct((2 * a.shape[0],) + a.shape[1:], a.dtype) for a in halves])


def _merge_done(outs, halves):
    mc = lax.axis_index("c")
    return [lax.dynamic_update_slice_in_dim(o, a, a.shape[0] * mc, 0) for o, a in zip(outs, halves)]


def _comm_call(name, comm):
    ni, no = len(comm["ins"]), len(comm["out_shapes"])

    def body(*refs):
        ins, outs, (ss, rs) = refs[:ni], refs[ni:ni + no], refs[ni + no:]
        comm["start"](ins, outs, ss, rs)
        comm["finish"](ins, outs, ss, rs)

    return pl.pallas_call(
        body, name=name, in_specs=[ANY] * ni, out_specs=[ANY] * no, out_shape=comm["out_shapes"],
        scratch_shapes=[pltpu.SemaphoreType.DMA((comm["nsem"],)), pltpu.SemaphoreType.DMA((comm["nsem"],))],
    )(*comm["ins"])


def _call(body, comm, *, name, grid, in_specs, out_specs, out_shape, scratch_shapes, sem, args):
    if comm is None:
        outs = pl.pallas_call(body, name=name, grid=grid, in_specs=in_specs, out_specs=out_specs,
                              out_shape=out_shape, scratch_shapes=scratch_shapes,
                              compiler_params=_cp(sem))(*args)
        return outs, None
    ni, no = len(in_specs), len(out_specs)
    ci, co = len(comm["ins"]), len(comm["out_shapes"])
    nscr = len(scratch_shapes)

    def carried(*refs):
        ins, cins = refs[:ni], refs[ni:ni + ci]
        outs, couts = refs[ni + ci:ni + ci + no], refs[ni + ci + no:ni + ci + no + co]
        scr = refs[ni + ci + no + co:]
        ss, rs = scr[nscr], scr[nscr + 1]
        pids = [pl.program_id(ax) for ax in range(len(grid))]
        first = functools.reduce(jnp.logical_and, [p == 0 for p in pids])
        last = functools.reduce(jnp.logical_and, [p == g - 1 for p, g in zip(pids, grid)])

        @pl.when(first)
        def _():
            comm["start"](cins, couts, ss, rs)

        body(*ins, *outs, *scr[:nscr])

        @pl.when(last)
        def _():
            comm["finish"](cins, couts, ss, rs)

    outs = pl.pallas_call(
        carried, name=name + "_x", grid=grid, in_specs=list(in_specs) + [ANY] * ci,
        out_specs=list(out_specs) + [ANY] * co, out_shape=list(out_shape) + comm["out_shapes"],
        scratch_shapes=list(scratch_shapes) + [pltpu.SemaphoreType.DMA((comm["nsem"],)),
                                               pltpu.SemaphoreType.DMA((comm["nsem"],))],
        compiler_params=_cp(("arbitrary",) * len(grid)),
    )(*args, *comm["ins"])
    return outs[:no], outs[no:]


def _layer_fwd(x, res, res_gate, wl, mod, comm=None):
    L = x.shape[0]
    bm = _row_block(L, 1024)
    nb = L // bm
    sh1, sc1, g1, sh2, sc2, g2 = [mod[k * D:(k + 1) * D][None, :] for k in range(NMOD)]
    sv = {}
    xin, h = _rmsmod_fwd(x, res, res_gate, wl["norm1_g"], sh1, sc1)
    if xin is None:
        xin = x
    p = _mm("mm_in", h, wl["w_in"], dims=NN, grid=(nb, 4, 1),
            a_block=(bm, D), a_map=lambda i, j, k: (i, 0),
            b_block=(None, D, 512), b_map=lambda i, j, k: (j, 0, 0),
            out_shape=(L, INW), o_block=(bm, 512), o_map=lambda i, j, k: (i, j))
    qkv = _qknorm_fwd(p, wl["gqk"])
    oa, couts = _attn_fwd(qkv, comm)
    os_, s_re, s_im = _s5_fwd2(p, wl["abar_re"], wl["abar_im"], wl["bbr"], wl["bbi"], wl["ccr"], wl["cci"],
                              wl["dsk"], wl["gw"], wl["gb"])
    o = _outnorm_fwd(oa, os_, wl["attn_out_g"], wl["ssm_out_g"])
    proj = _mm("mm_out", o, wl["w_out"], dims=NN, grid=(nb, 2, 1),
               a_block=(bm, D), a_map=lambda i, j, k: (i, 0),
               b_block=(D, 512), b_map=lambda i, j, k: (0, j),
               out_shape=(L, D), o_block=(bm, 512), o_map=lambda i, j, k: (i, j))
    x1, h2 = _rmsmod_fwd(xin, proj, g1, wl["norm2_g"], sh2, sc2)
    up_pre = _mm("mm_up", h2, wl["w_up"], dims=NN, grid=(nb, 4, 1),
                 a_block=(bm, D), a_map=lambda i, j, k: (i, 0),
                 b_block=(None, D, 1408), b_map=lambda i, j, k: (j, 0, 0),
                 out_shape=(L, DUP), o_block=(bm, 1408), o_map=lambda i, j, k: (i, j))
    a = _convglu_fwd(up_pre, wl["conv_w"], wl["conv_b"])
    down = _mm("mm_down", a, wl["w_down"], dims=NN, grid=(nb, 2, 1),
               a_block=(bm, DFF), a_map=lambda i, j, k: (i, 0),
               b_block=(DFF, 512), b_map=lambda i, j, k: (0, j),
               out_shape=(L, D), o_block=(bm, 512), o_map=lambda i, j, k: (i, j))
    sv.update(xin=xin, h=h, p=p, qkv=qkv, oa=oa, os=os_, s_re=s_re, s_im=s_im, o=o, proj=proj,
              x1=x1, h2=h2, up_pre=up_pre, a=a, down=down,
              sc1=sc1, g1=g1, sc2=sc2, g2=g2)
    return x1, down, g2, sv, couts


def _big_grads(g):
    return [g["w_in"], g["w_out"].reshape(NCHIP, D // NCHIP, D), g["w_up"],
            g["w_down"].reshape(NCHIP, DFF // NCHIP, D)]


def _chip_sums(grads, recv):
    core = lax.axis_index("c").astype(jnp.int32).reshape((1,))

    def half_sum(a, b):
        _, half, cols = b.shape
        bm = half if half * cols * 4 <= (3 << 19) else half // 2
        nrb = half // bm

        def body(core_ref, a_ref, b_ref, o_ref):
            o_ref[...] = (a_ref[...] + b_ref[...]).astype(o_ref.dtype)

        blk = pl.BlockSpec((None, bm, cols), lambda j, i, core_ref: (j, i, 0))
        return pl.pallas_call(
            body, name="chip_sum",
            grid_spec=pltpu.PrefetchScalarGridSpec(
                num_scalar_prefetch=1, grid=(NCHIP, nrb),
                in_specs=[pl.BlockSpec((None, bm, cols), lambda j, i, core_ref: (j, core_ref[0] * nrb + i, 0)),
                          blk],
                out_specs=blk),
            out_shape=jax.ShapeDtypeStruct(b.shape, bf16),
            compiler_params=_cp(("parallel", "parallel")),
        )(core, a, b)

    return [half_sum(a, b) for a, b in zip(grads, recv)]


def _reduced(parts, sums):
    halves = [_sum_parts(p) for p in _scatter_done(parts, sums)]
    return _merge_done(_comm_call("grad_merge", _merge_comm(halves)), halves)


def _layer_bwd(dx_out, wl, sv, prev=None):
    L = dx_out.shape[0]
    bm = _row_block(L, 1024)
    nb = L // bm
    bk = _row_block(L, 512)
    nk = L // bk
    g = {}
    dd, dg2 = _gate_bwd(dx_out, sv["down"], sv["g2"])
    da = _mm("mm_down_dx", dd, wl["w_down"], dims=NT, grid=(nb, 2, 1),
             a_block=(bm, D), a_map=lambda i, j, k: (i, 0),
             b_block=(1408, D), b_map=lambda i, j, k: (j, 0),
             out_shape=(L, DFF), o_block=(bm, 1408), o_map=lambda i, j, k: (i, j))
    g["w_down"] = _mm("mm_down_dw", sv["a"], dd, dims=TN, grid=(1, 2, nk),
                      a_block=(bk, DFF), a_map=lambda i, j, k: (k, 0),
                      b_block=(bk, 512), b_map=lambda i, j, k: (k, j),
                      out_shape=(DFF, D), o_block=(DFF, 512), o_map=lambda i, j, k: (0, j))
    dup, g["conv_w"], g["conv_b"], recv = _convglu_bwd(da, sv["up_pre"], wl["conv_w"], wl["conv_b"],
                                                       _swap_comm(prev) if prev else None)
    sums = _chip_sums(prev, recv) if prev else None
    dh2 = _mm("mm_up_dx", dup, wl["w_up"], dims=NT, grid=(nb, 1, 4),
              a_block=(None, bm, 1408), a_map=lambda i, j, k: (k // 2, i, k % 2),
              b_block=(None, D, 1408), b_map=lambda i, j, k: (k, 0, 0),
              out_shape=(L, D), o_block=(bm, D), o_map=lambda i, j, k: (i, 0))
    g["w_up"] = _mm("mm_up_dw", sv["h2"], dup, dims=TN, grid=(1, 4, nk),
                    a_block=(bk, D), a_map=lambda i, j, k: (k, 0),
                    b_block=(None, bk, 1408), b_map=lambda i, j, k: (j // 2, k, j % 2),
                    out_shape=(NCHIP, D, 1408), o_block=(None, D, 1408), o_map=lambda i, j, k: (j, 0, 0))
    dx1, dsh2, dsc2, g["norm2_g"] = _rmsmod_bwd(dh2, sv["x1"], wl["norm2_g"], sv["sc2"], dx_out)
    dproj, dg1 = _gate_bwd(dx1, sv["proj"], sv["g1"])
    do = _mm("mm_out_dx", dproj, wl["w_out"], dims=NT, grid=(nb, 2, 1),
             a_block=(bm, D), a_map=lambda i, j, k: (i, 0),
             b_block=(512, D), b_map=lambda i, j, k: (j, 0),
             out_shape=(L, D), o_block=(bm, 512), o_map=lambda i, j, k: (i, j))
    g["w_out"] = _mm("mm_out_dw", sv["o"], dproj, dims=TN, grid=(1, 2, nk),
                     a_block=(bk, D), a_map=lambda i, j, k: (k, 0),
                     b_block=(bk, 512), b_map=lambda i, j, k: (k, j),
                     out_shape=(D, D), o_block=(D, 512), o_map=lambda i, j, k: (0, j))
    doa, dos, g["attn_out_g"], g["ssm_out_g"] = _outnorm_bwd(do, sv["oa"], sv["os"],
                                                             wl["attn_out_g"], wl["ssm_out_g"])
    (dqn, dkn, dv), parts = _attn_bwd(sv["qkv"], sv["oa"], doa, _scatter_comm(sums) if prev else None)
    prev_reduced = _reduced(parts, sums) if prev else None
    (du, d_abar_re, d_abar_im, g["ssm_d"], g["glu_b"], d_ccr, d_cci, d_bbre, d_bbim, d_gw) = _s5_bwd2(
        dos, sv["p"], sv["s_re"], sv["s_im"], wl["abar_re"], wl["abar_im"], wl["bbr"], wl["bbi"],
        wl["ccr"], wl["cci"], wl["dsk"], wl["gw"], wl["gb"])
    g["ssm_c_re"] = d_ccr.reshape(NG, NP, NH16).transpose(0, 2, 1)
    g["ssm_c_im"] = -d_cci.reshape(NG, NP, NH16).transpose(0, 2, 1)
    g["glu_w"] = d_gw.reshape(NG, NH16, NH16)
    d_ldt, d_ar, d_ai, d_br, d_bi = _s5_disc_bwd(
        wl["ldt_col"], wl["ar_col"], wl["ai_col"], wl["br_mat"], wl["bi_mat"],
        d_abar_re.reshape(NS, 1), d_abar_im.reshape(NS, 1), d_bbre, d_bbim)
    g["ssm_log_dt"] = d_ldt[:, 0]
    g["ssm_a_re"] = d_ar.reshape(NG, NP)
    g["ssm_a_im"] = d_ai.reshape(NG, NP)
    g["ssm_b_re"] = d_br.reshape(NG, NP, NH16)
    g["ssm_b_im"] = d_bi.reshape(NG, NP, NH16)
    dp, dgqk = _qknorm_bwd(dqn, dkn, dv, du, sv["p"], wl["gqk"])
    g["q_norm_g"] = dgqk[0, :HD]
    g["k_norm_g"] = dgqk[0, HD:2 * HD]
    dh = _mm("mm_in_dx", dp, wl["w_in"], dims=NT, grid=(nb, 1, 4),
             a_block=(bm, 512), a_map=lambda i, j, k: (i, k),
             b_block=(None, D, 512), b_map=lambda i, j, k: (k, 0, 0),
             out_shape=(L, D), o_block=(bm, D), o_map=lambda i, j, k: (i, 0))
    g["w_in"] = _mm("mm_in_dw", sv["h"], dp, dims=TN, grid=(1, 4, nk),
                    a_block=(bk, D), a_map=lambda i, j, k: (k, 0),
                    b_block=(bk, 512), b_map=lambda i, j, k: (k, j),
                    out_shape=(NCHIP, D, 512), o_block=(None, D, 512), o_map=lambda i, j, k: (j, 0, 0))
    dx_in, dsh1, dsc1, g["norm1_g"] = _rmsmod_bwd(dh, sv["xin"], wl["norm1_g"], sv["sc1"], dx1)
    g["dmod"] = jnp.concatenate([dsh1, dsc1, dg1, dsh2, dsc2, dg2], axis=1)[0]
    return dx_in, g, prev_reduced


def _prep_layer(l, conv_w, small):
    wl = {}
    wl["conv_w"] = conv_w[l]
    wl["conv_b"] = small["ffn_conv_b"][l][None, :]
    wl["norm1_g"] = small["norm1_g"][l][None, :]
    wl["norm2_g"] = small["norm2_g"][l][None, :]
    wl["attn_out_g"] = small["attn_out_g"][l][None, :]
    wl["ssm_out_g"] = small["ssm_out_g"][l][None, :]
    wl["gqk"] = jnp.concatenate([jnp.tile(small["q_norm_g"][l], 8), jnp.tile(small["k_norm_g"][l], 8)])[None, :]
    wl["ldt_col"] = jnp.repeat(small["ssm_log_dt"][l], NP)[:, None]
    wl["ar_col"] = small["ssm_a_re"][l].reshape(NS, 1)
    wl["ai_col"] = small["ssm_a_im"][l].reshape(NS, 1)
    wl["br_mat"] = small["ssm_b_re"][l].reshape(NS, NH16)
    wl["bi_mat"] = small["ssm_b_im"][l].reshape(NS, NH16)
    abar_re, abar_im, bb_re, bb_im = _s5_disc_fwd(wl["ldt_col"], wl["ar_col"], wl["ai_col"],
                                                  wl["br_mat"], wl["bi_mat"])
    wl["abar_re"] = abar_re.reshape(1, NS)
    wl["abar_im"] = abar_im.reshape(1, NS)
    wl["bbr"] = _block_diag(bb_re.astype(bf16).reshape(NG, NP, NH16).transpose(0, 2, 1))
    wl["bbi"] = _block_diag(bb_im.astype(bf16).reshape(NG, NP, NH16).transpose(0, 2, 1))
    wl["ccr"] = _block_diag(small["ssm_c_re"][l].astype(bf16).transpose(0, 2, 1))
    wl["cci"] = _block_diag(small["ssm_c_im"][l].astype(bf16).transpose(0, 2, 1))
    wl["gw"] = _block_diag(small["glu_w"][l].astype(bf16))
    wl["dsk"] = small["ssm_d"][l].reshape(1, SSMW)
    wl["gb"] = small["glu_b"][l].reshape(1, SSMW)
    return wl


def _local_step(x, target, mods, shards, conv_w, small):
    def set_big(wl, gathered):
        w_in, w_out, w_up, w_down = gathered
        wl.update(w_in=w_in, w_up=w_up, w_out=w_out.reshape(D, D), w_down=w_down.reshape(DFF, D))

    wls = [_prep_layer(l, conv_w, small) for l in range(DEPTH)]
    set_big(wls[0], _comm_call("weight_gather", _gather_comm(shards, 0)))
    saved = []
    cur, res, res_gate = x, None, None
    for l in range(DEPTH):
        nxt = _gather_comm(shards, l + 1) if l + 1 < DEPTH else None
        cur, res, res_gate, sv, couts = _layer_fwd(cur, res, res_gate, wls[l], mods[l], nxt)
        saved.append(sv)
        if nxt is not None:
            set_big(wls[l + 1], couts)
    loss, dx = _loss_fwd_bwd(cur, res, res_gate, target)
    grads = [None] * DEPTH
    big = [None] * DEPTH
    prev = None
    for l in reversed(range(DEPTH)):
        dx, grads[l], done = _layer_bwd(dx, wls[l], saved[l], prev)
        if prev is not None:
            big[l + 1] = done
        prev = _big_grads(grads[l])
    recv = _comm_call("grad_swap", _swap_comm(prev))
    sums = _chip_sums(prev, recv)
    big[0] = _reduced(_comm_call("grad_scatter", _scatter_comm(sums)), sums)
    return loss, dx, grads, big


BIG = ("w_in", "w_out", "ffn_w_up", "ffn_w_down")
SMALL = ("ada_b", "norm1_g", "q_norm_g", "k_norm_g", "ssm_a_re", "ssm_a_im", "ssm_log_dt",
         "ssm_b_re", "ssm_b_im", "ssm_c_re", "ssm_c_im", "ssm_d", "glu_w", "glu_b",
         "attn_out_g", "ssm_out_g", "norm2_g", "ffn_conv_b")
NAMES = ("ada_w", "ada_b", "norm1_g", "w_in", "q_norm_g", "k_norm_g", "ssm_a_re", "ssm_a_im",
         "ssm_log_dt", "ssm_b_re", "ssm_b_im", "ssm_c_re", "ssm_c_im", "ssm_d", "glu_w", "glu_b",
         "attn_out_g", "ssm_out_g", "w_out", "norm2_g", "ffn_w_up", "ffn_conv_w", "ffn_conv_b",
         "ffn_w_down")
PACK_COLS = 1024


def _pack(arrs):
    flat = jnp.concatenate([a.reshape(-1) for a in arrs])
    rows = -(-flat.shape[0] // PACK_COLS)
    rows = -(-rows // 8) * 8
    flat = jnp.pad(flat, (0, rows * PACK_COLS - flat.shape[0]))
    return flat.reshape(rows, PACK_COLS)


def _unpack(packed, shapes):
    flat = packed.reshape(-1)
    out, off = [], 0
    for s in shapes:
        n = math.prod(s)
        out.append(flat[off:off + n].reshape(s))
        off += n
    return out


def kernel(x, c, ada_w, ada_b, norm1_g, w_in, q_norm_g, k_norm_g, ssm_a_re, ssm_a_im, ssm_log_dt, ssm_b_re, ssm_b_im, ssm_c_re, ssm_c_im, ssm_d, glu_w, glu_b, attn_out_g, ssm_out_g, w_out, norm2_g, ffn_w_up, ffn_conv_w, ffn_conv_b, ffn_w_down, loss_target, m_ada_w, m_ada_b, m_norm1_g, m_w_in, m_q_norm_g, m_k_norm_g, m_ssm_a_re, m_ssm_a_im, m_ssm_log_dt, m_ssm_b_re, m_ssm_b_im, m_ssm_c_re, m_ssm_c_im, m_ssm_d, m_glu_w, m_glu_b, m_attn_out_g, m_ssm_out_g, m_w_out, m_norm2_g, m_ffn_w_up, m_ffn_conv_w, m_ffn_conv_b, m_ffn_w_down, v_ada_w, v_ada_b, v_norm1_g, v_w_in, v_q_norm_g, v_k_norm_g, v_ssm_a_re, v_ssm_a_im, v_ssm_log_dt, v_ssm_b_re, v_ssm_b_im, v_ssm_c_re, v_ssm_c_im, v_ssm_d, v_glu_w, v_glu_b, v_attn_out_g, v_ssm_out_g, v_w_out, v_norm2_g, v_ffn_w_up, v_ffn_conv_w, v_ffn_conv_b, v_ffn_w_down):
    env = dict(locals())
    w = {n: env[n] for n in NAMES}
    m = {n: env["m_" + n] for n in NAMES}
    v = {n: env["v_" + n] for n in NAMES}
    mx, my, mc = _my_pos()
    chip = 2 * mx + my
    dev = 4 * mx + 2 * my + mc
    xl = x[0]
    tl = loss_target[0]

    c_all = _allgather8("gather_c", jnp.pad(c, ((0, 7), (0, 0))))[:, 0, :]
    c16 = jnp.pad(c_all, ((0, 8), (0, 0)))
    nloc = NMOD * D // NCHIP
    ada_b_loc = lax.dynamic_slice(ada_b, (0, chip * nloc), (DEPTH, nloc))[:, None, :]
    mod_loc = _mod_fwd(c16, ada_w, ada_b_loc)
    mod_g = _allgather8("gather_mod", mod_loc.reshape(DEPTH * 16, nloc))
    mod_all = mod_g[0::2].reshape(NCHIP, DEPTH, 16, nloc).transpose(1, 2, 0, 3).reshape(DEPTH, 16, NMOD * D)
    mods = lax.dynamic_index_in_dim(mod_all, dev, axis=1, keepdims=False)

    shards = [_cast_bf16(w[n]) for n in BIG]
    ncw = DUP // NCHIP
    conv_g = _allgather8("gather_conv", jnp.pad(w["ffn_conv_w"].reshape(DEPTH * 3, ncw), ((0, 4), (0, 0))))
    conv_w = conv_g[0::2, :DEPTH * 3].reshape(NCHIP, DEPTH, 3, ncw).transpose(1, 2, 0, 3).reshape(DEPTH, 3, DUP)
    small = {n: w[n] for n in SMALL}

    loss_loc, grad_x, grads, big = _local_step(xl, tl, mods, shards, conv_w, small)
    loss = lax.psum(loss_loc, ("x", "y", "c"))

    outs = {}
    for k, name in enumerate(BIG):
        g_big = jnp.stack([big[l][k] for l in range(DEPTH)])
        outs[name] = _adamw(g_big[None], w[name], m[name], v[name])

    small_names = SMALL[1:] + ("ffn_conv_w_full",)
    small_grads = []
    for n in SMALL[1:]:
        key = {"ffn_conv_b": "conv_b"}.get(n, n)
        small_grads.append(jnp.stack([grads[l][key].reshape(w[n].shape[1:]) for l in range(DEPTH)]))
    small_grads.append(jnp.stack([grads[l]["conv_w"] for l in range(DEPTH)]))
    dmod = jnp.stack([grads[l]["dmod"] for l in range(DEPTH)])
    packed = _pack([dmod] + small_grads)
    allp = _allgather8("gather_small", packed)
    shapes = [dmod.shape] + [a.shape for a in small_grads]
    dmod_all = allp[:, :DEPTH * NMOD, :].reshape(NDEV, DEPTH, NMOD * D)
    dmod_loc = lax.dynamic_slice(dmod_all, (0, 0, chip * nloc), (NDEV, DEPTH, nloc)).transpose(1, 0, 2)
    g_ada = _ada_grad(c16, jnp.pad(dmod_loc, ((0, 0), (0, 8), (0, 0))))
    outs["ada_w"] = _adamw(g_ada[None], w["ada_w"], m["ada_w"], v["ada_w"])
    conv_w_shape = (DEPTH, 3, DUP)
    w_small = _pack([w[n] for n in SMALL] + [jnp.zeros(conv_w_shape, f32)])
    m_small = _pack([m[n] for n in SMALL] + [jnp.zeros(conv_w_shape, f32)])
    v_small = _pack([v[n] for n in SMALL] + [jnp.ones(conv_w_shape, f32)])
    res_small = _adamw(allp, w_small, m_small, v_small)
    unpacked = [_unpack(r, shapes) for r in res_small]
    for i, n in enumerate(SMALL):
        outs[n] = tuple(unpacked[k][i] for k in range(4))
    g_conv_full = unpacked[0][len(SMALL)]
    ncw = DUP // NCHIP
    g_conv = lax.dynamic_slice(g_conv_full, (0, 0, chip * ncw), (DEPTH, 3, ncw))
    outs["ffn_conv_w"] = _adamw(g_conv[None], w["ffn_conv_w"], m["ffn_conv_w"], v["ffn_conv_w"])

    result = [loss, grad_x[None]]
    for k in range(4):
        result += [outs[n][k] for n in NAMES]
    return tuple(result)
```

```python
import functools
import math

import jax
import jax.numpy as jnp
from jax import lax
from jax.experimental import pallas as pl
from jax.experimental.pallas import tpu as pltpu

f32 = jnp.float32
bf16 = jnp.bfloat16
_MXU = jnp.bfloat16

D = 1024
ATT = 512
SSMW = 512
HD = 64
NG = 32
NP = 64
NH16 = 16
NS = NG * NP
INW = 2048
DFF = 2816
DUP = 2 * DFF
NMOD = 6
DEPTH = 4
EPS = 1e-6
NCHIP = 4
NDEV = 8

ADAM_LR = 0.001
ADAM_B1 = 0.9
ADAM_B2 = 0.999
ADAM_EPS = 1e-08
ADAM_WD = 0.01
ADAM_STEP = 10

VMEM_LIMIT = 56 * 1024 * 1024
MESH = pl.DeviceIdType.MESH

NN = (((1,), (0,)), ((), ()))
NT = (((1,), (1,)), ((), ()))
TN = (((0,), (0,)), ((), ()))


def _cp(sem=None):
    if sem is None:
        return pltpu.CompilerParams(vmem_limit_bytes=VMEM_LIMIT)
    return pltpu.CompilerParams(dimension_semantics=sem, vmem_limit_bytes=VMEM_LIMIT)


def _dot(a, b, dims=NN):
    return lax.dot_general(a.astype(_MXU), b.astype(_MXU), dims, preferred_element_type=f32)


def _dot_exact(x, m):
    hi = x.astype(bf16)
    r1 = x - hi.astype(f32)
    mid = r1.astype(bf16)
    lo = (r1 - mid.astype(f32)).astype(bf16)
    d = lambda p: lax.dot_general(p, m, NN, preferred_element_type=f32)
    return d(hi) + d(mid) + d(lo)


def _gelu(x):
    c = math.sqrt(2.0 / math.pi)
    return 0.5 * x * (1.0 + jnp.tanh(c * (x + 0.044715 * (x * x * x))))


def _gelu_grad(x):
    c = math.sqrt(2.0 / math.pi)
    t = jnp.tanh(c * (x + 0.044715 * (x * x * x)))
    return 0.5 * (1.0 + t) + 0.5 * x * (1.0 - t * t) * c * (1.0 + 3.0 * 0.044715 * (x * x))


def _sigmoid(x):
    return 1.0 / (1.0 + jnp.exp(-x))


def _mm(name, a, b, *, dims, grid, a_block, a_map, b_block, b_map, out_shape, o_block, o_map,
        out_dtype=f32):
    nk = grid[2]
    acc_shape = tuple(s for s in o_block if s is not None)

    def body(a_ref, b_ref, o_ref, acc_ref):
        k = pl.program_id(2)
        part = _dot(a_ref[...], b_ref[...], dims)
        if nk == 1:
            o_ref[...] = part.astype(o_ref.dtype)
        else:
            @pl.when(k == 0)
            def _():
                acc_ref[...] = part

            @pl.when(k > 0)
            def _():
                acc_ref[...] += part

            @pl.when(k == nk - 1)
            def _():
                o_ref[...] = acc_ref[...].astype(o_ref.dtype)

    return pl.pallas_call(
        body, name=name, grid=grid,
        in_specs=[pl.BlockSpec(a_block, a_map), pl.BlockSpec(b_block, b_map)],
        out_specs=pl.BlockSpec(o_block, o_map),
        out_shape=jax.ShapeDtypeStruct(out_shape, out_dtype),
        scratch_shapes=[pltpu.VMEM(acc_shape if nk > 1 else (8, 128), f32)],
        compiler_params=_cp(("parallel", "parallel", "arbitrary")),
    )(a, b)


def _row_block(L, want):
    return want if L % want == 0 else L


def _rmsmod_fwd(x, res, gate, g, sh, sc):
    L = x.shape[0]
    bm = _row_block(L, 256)
    with_res = res is not None

    def body(*refs):
        if with_res:
            x_ref, r_ref, gt_ref, g_ref, sh_ref, sc_ref, xo_ref, h_ref = refs
            xin = x_ref[...] + gt_ref[...] * r_ref[...]
            xo_ref[...] = xin
        else:
            x_ref, g_ref, sh_ref, sc_ref, h_ref = refs
            xin = x_ref[...]
        inv = lax.rsqrt(jnp.mean(xin * xin, axis=-1, keepdims=True) + EPS)
        xn = xin * inv * g_ref[...]
        h_ref[...] = (xn * (1.0 + sc_ref[...]) + sh_ref[...]).astype(h_ref.dtype)

    row = pl.BlockSpec((bm, D), lambda i: (i, 0))
    vec = pl.BlockSpec((1, D), lambda i: (0, 0))
    if with_res:
        return pl.pallas_call(
            body, name="rmsmod_res_fwd", grid=(L // bm,),
            in_specs=[row, row, vec, vec, vec, vec], out_specs=[row, row],
            out_shape=[jax.ShapeDtypeStruct((L, D), f32), jax.ShapeDtypeStruct((L, D), bf16)],
            compiler_params=_cp(("parallel",)),
        )(x, res, gate, g, sh, sc)
    h = pl.pallas_call(
        body, name="rmsmod_fwd", grid=(L // bm,),
        in_specs=[row, vec, vec, vec], out_specs=row,
        out_shape=jax.ShapeDtypeStruct((L, D), bf16),
        compiler_params=_cp(("parallel",)),
    )(x, g, sh, sc)
    return None, h


def _rmsmod_bwd(dh, x, g, sc, dres):
    L = x.shape[0]
    bm = _row_block(L, 256)

    def body(dh_ref, x_ref, g_ref, sc_ref, dr_ref, dx_ref, dsh_ref, dsc_ref, dg_ref):
        i = pl.program_id(0)
        xv = x_ref[...]
        dhv = dh_ref[...]
        inv = lax.rsqrt(jnp.mean(xv * xv, axis=-1, keepdims=True) + EPS)
        xh = xv * inv
        gv = g_ref[...]
        xn = xh * gv
        dxn = dhv * (1.0 + sc_ref[...])
        dxh = dxn * gv
        dx_ref[...] = inv * (dxh - xh * jnp.mean(dxh * xh, axis=-1, keepdims=True)) + dr_ref[...]
        p_sh = jnp.sum(dhv, axis=0, keepdims=True)
        p_sc = jnp.sum(dhv * xn, axis=0, keepdims=True)
        p_g = jnp.sum(dxn * xh, axis=0, keepdims=True)

        @pl.when(i == 0)
        def _():
            dsh_ref[...] = p_sh
            dsc_ref[...] = p_sc
            dg_ref[...] = p_g

        @pl.when(i > 0)
        def _():
            dsh_ref[...] += p_sh
            dsc_ref[...] += p_sc
            dg_ref[...] += p_g

    row = pl.BlockSpec((bm, D), lambda i: (i, 0))
    vec = pl.BlockSpec((1, D), lambda i: (0, 0))
    return pl.pallas_call(
        body, name="rmsmod_bwd", grid=(L // bm,),
        in_specs=[row, row, vec, vec, row], out_specs=[row, vec, vec, vec],
        out_shape=[jax.ShapeDtypeStruct((L, D), f32)] + [jax.ShapeDtypeStruct((1, D), f32)] * 3,
        compiler_params=_cp(("arbitrary",)),
    )(dh, x, g, sc, dres)


def _gate_bwd(dx, y, gate):
    L = dx.shape[0]
    bm = _row_block(L, 256)

    def body(dx_ref, y_ref, gt_ref, dy_ref, dgt_ref):
        i = pl.program_id(0)
        dxv = dx_ref[...]
        dy_ref[...] = (gt_ref[...] * dxv).astype(dy_ref.dtype)
        part = jnp.sum(dxv * y_ref[...], axis=0, keepdims=True)

        @pl.when(i == 0)
        def _():
            dgt_ref[...] = part

        @pl.when(i > 0)
        def _():
            dgt_ref[...] += part

    row = pl.BlockSpec((bm, D), lambda i: (i, 0))
    vec = pl.BlockSpec((1, D), lambda i: (0, 0))
    return pl.pallas_call(
        body, name="gate_bwd", grid=(L // bm,),
        in_specs=[row, row, vec], out_specs=[row, vec],
        out_shape=[jax.ShapeDtypeStruct((L, D), bf16), jax.ShapeDtypeStruct((1, D), f32)],
        compiler_params=_cp(("arbitrary",)),
    )(dx, y, gate)


def _loss_fwd_bwd(x1, down, gate, target):
    L = x1.shape[0]
    bm = _row_block(L, 256)
    nsteps = L // bm

    def body(x_ref, d_ref, gt_ref, t_ref, dy_ref, loss_ref, acc_ref):
        i = pl.program_id(0)
        diff = x_ref[...] + gt_ref[...] * d_ref[...] - t_ref[...]
        dy_ref[...] = diff * (1.0 / D)
        part = jnp.sum(diff * diff, axis=0, keepdims=True)

        @pl.when(i == 0)
        def _():
            acc_ref[...] = part

        @pl.when(i > 0)
        def _():
            acc_ref[...] += part

        @pl.when(i == nsteps - 1)
        def _():
            tot = jnp.sum(acc_ref[...], axis=1, keepdims=True) * (0.5 / D)
            loss_ref[...] = jnp.broadcast_to(tot, (8, 128))

    row = pl.BlockSpec((bm, D), lambda i: (i, 0))
    vec = pl.BlockSpec((1, D), lambda i: (0, 0))
    dy, loss = pl.pallas_call(
        body, name="loss_fwd_bwd", grid=(nsteps,),
        in_specs=[row, row, vec, row],
        out_specs=[row, pl.BlockSpec((8, 128), lambda i: (0, 0))],
        out_shape=[jax.ShapeDtypeStruct((L, D), f32), jax.ShapeDtypeStruct((8, 128), f32)],
        scratch_shapes=[pltpu.VMEM((1, D), f32)],
        compiler_params=_cp(("arbitrary",)),
    )(x1, down, gate, target)
    return loss[0, 0], dy


def _head_mean_matrix():
    r = lax.broadcasted_iota(jnp.int32, (128, 128), 0) // HD
    c = lax.broadcasted_iota(jnp.int32, (128, 128), 1) // HD
    return jnp.where(r == c, 1.0 / HD, 0.0).astype(bf16)


def _qknorm_fwd(p, gqk):
    L = p.shape[0]
    bm = _row_block(L, 256)

    def body(p_ref, g_ref, o_ref):
        e = _head_mean_matrix()
        for c in range(8):
            sl = slice(128 * c, 128 * (c + 1))
            xv = p_ref[:, sl]
            inv = lax.rsqrt(_dot_exact2(xv * xv, e) + EPS)
            yv = xv * inv * g_ref[:, sl]
            if c < 4:
                yv = yv * (1.0 / math.sqrt(HD))
            o_ref[:, sl] = yv.astype(o_ref.dtype)
        o_ref[:, 1024:1536] = p_ref[:, 1024:1536].astype(o_ref.dtype)

    return pl.pallas_call(
        body, name="qknorm_fwd", grid=(L // bm,),
        in_specs=[pl.BlockSpec((bm, 1536), lambda i: (i, 0)), pl.BlockSpec((1, 1024), lambda i: (0, 0))],
        out_specs=pl.BlockSpec((bm, 1536), lambda i: (i, 0)),
        out_shape=jax.ShapeDtypeStruct((L, 1536), bf16),
        compiler_params=_cp(("parallel",)),
    )(p, gqk)


def _qknorm_bwd(dqn, dkn, dval, du, p, gqk):
    L = p.shape[0]
    bm = _row_block(L, 256)
    nsteps = L // bm

    def body(dq_ref, dk_ref, dv_ref, du_ref, p_ref, g_ref, o_ref, dg_ref, acc_ref):
        i = pl.program_id(0)
        e = _head_mean_matrix()
        o_ref[:, 1024:1536] = dv_ref[...].astype(o_ref.dtype)
        o_ref[:, 1536:2048] = du_ref[...].astype(o_ref.dtype)
        for c in range(8):
            sl = slice(128 * c, 128 * (c + 1))
            xv = p_ref[:, sl]
            d_ref = dq_ref if c < 4 else dk_ref
            dv = d_ref[:, 128 * (c % 4):128 * (c % 4 + 1)]
            gv = g_ref[:, sl]
            inv = lax.rsqrt(_dot_exact2(xv * xv, e) + EPS)
            xh = xv * inv
            dxh = dv * gv
            o_ref[:, sl] = (inv * (dxh - xh * _dot_exact2(dxh * xh, e))).astype(o_ref.dtype)
            part = jnp.sum(dv * xh, axis=0, keepdims=True)

            @pl.when(i == 0)
            def _():
                acc_ref[:, sl] = part

            @pl.when(i > 0)
            def _():
                acc_ref[:, sl] += part

        @pl.when(i == nsteps - 1)
        def _():
            r = lax.broadcasted_iota(jnp.int32, (1024, 128), 0)
            col = lax.broadcasted_iota(jnp.int32, (1024, 128), 1)
            fold = jnp.where(col == (r // 512) * HD + r % HD, 1.0, 0.0).astype(bf16)
            dg_ref[...] = _dot_exact(jnp.broadcast_to(acc_ref[...], (8, 1024)), fold)

    half = pl.BlockSpec((bm, 512), lambda i: (i, 0))
    return pl.pallas_call(
        body, name="qknorm_bwd", grid=(nsteps,),
        in_specs=[half, half, half, half, pl.BlockSpec((bm, 1024), lambda i: (i, 0)),
                  pl.BlockSpec((1, 1024), lambda i: (0, 0))],
        out_specs=[pl.BlockSpec((bm, INW), lambda i: (i, 0)), pl.BlockSpec((8, 128), lambda i: (0, 0))],
        out_shape=[jax.ShapeDtypeStruct((L, INW), bf16), jax.ShapeDtypeStruct((8, 128), f32)],
        scratch_shapes=[pltpu.VMEM((1, 1024), f32)],
        compiler_params=_cp(("arbitrary",)),
    )(dqn, dkn, dval, du, p, gqk)


def _outnorm_fwd(oa, os_, ga, gs):
    L = oa.shape[0]
    bm = _row_block(L, 256)

    def body(a_ref, s_ref, ga_ref, gs_ref, o_ref):
        for x_ref, g_ref, off in ((a_ref, ga_ref, 0), (s_ref, gs_ref, 512)):
            xv = x_ref[...]
            inv = lax.rsqrt(jnp.mean(xv * xv, axis=-1, keepdims=True) + EPS)
            o_ref[:, off:off + 512] = (xv * inv * g_ref[...]).astype(o_ref.dtype)

    half = pl.BlockSpec((bm, 512), lambda i: (i, 0))
    vec = pl.BlockSpec((1, 512), lambda i: (0, 0))
    return pl.pallas_call(
        body, name="outnorm_fwd", grid=(L // bm,),
        in_specs=[half, half, vec, vec], out_specs=pl.BlockSpec((bm, D), lambda i: (i, 0)),
        out_shape=jax.ShapeDtypeStruct((L, D), bf16),
        compiler_params=_cp(("parallel",)),
    )(oa, os_, ga, gs)


def _outnorm_bwd(do, oa, os_, ga, gs):
    L = oa.shape[0]
    bm = _row_block(L, 256)

    def body(do_ref, a_ref, s_ref, ga_ref, gs_ref, da_ref, ds_ref, dga_ref, dgs_ref):
        i = pl.program_id(0)
        for x_ref, g_ref, dx_ref, dg_ref, off in ((a_ref, ga_ref, da_ref, dga_ref, 0),
                                                  (s_ref, gs_ref, ds_ref, dgs_ref, 512)):
            xv = x_ref[...]
            dv = do_ref[:, off:off + 512]
            inv = lax.rsqrt(jnp.mean(xv * xv, axis=-1, keepdims=True) + EPS)
            xh = xv * inv
            dxh = dv * g_ref[...]
            dx_ref[...] = inv * (dxh - xh * jnp.mean(dxh * xh, axis=-1, keepdims=True))
            part = jnp.sum(dv * xh, axis=0, keepdims=True)

            @pl.when(i == 0)
            def _():
                dg_ref[...] = part

            @pl.when(i > 0)
            def _():
                dg_ref[...] += part

    half = pl.BlockSpec((bm, 512), lambda i: (i, 0))
    vec = pl.BlockSpec((1, 512), lambda i: (0, 0))
    return pl.pallas_call(
        body, name="outnorm_bwd", grid=(L // bm,),
        in_specs=[pl.BlockSpec((bm, D), lambda i: (i, 0)), half, half, vec, vec],
        out_specs=[half, half, vec, vec],
        out_shape=[jax.ShapeDtypeStruct((L, 512), f32)] * 2 + [jax.ShapeDtypeStruct((1, 512), f32)] * 2,
        compiler_params=_cp(("arbitrary",)),
    )(do, oa, os_, ga, gs)


NCB = DFF // 128


def _conv_rows(L):
    return 64 if L % 64 == 0 else 32


def _conv_chunk(x_ref, w_ref, b_ref, i, nch, R, L, ahead):
    base = pl.multiple_of(i * R, R)
    prev = x_ref[pl.ds(pl.multiple_of(jnp.maximum(base - 8, 0), 8), 8), :]
    parts = [jnp.where(i > 0, prev, 0.0), x_ref[pl.ds(base, R), :]]
    if ahead:
        nxt = x_ref[pl.ds(pl.multiple_of(jnp.minimum(base + R, L - 8), 8), 8), :]
        parts.append(jnp.where(i < nch - 1, nxt, 0.0))
    xe = jnp.concatenate(parts, axis=0)
    x1 = pltpu.roll(xe, 1, 0)
    x2 = pltpu.roll(xe, 2, 0)
    u = b_ref[...] + x2 * w_ref[0:1, :]
    u = u + x1 * w_ref[1:2, :]
    u = u + xe * w_ref[2:3, :]
    return u[8:], xe[8:], x1[8:], x2[8:]


def _fold8(x):
    out = x[0:8]
    for q in range(1, x.shape[0] // 8):
        out = out + x[8 * q:8 * (q + 1)]
    return out


def _convglu_fwd(up_pre, cw, cb):
    L = up_pre.shape[0]

    def body(v_ref, g_ref, wv_ref, wg_ref, bv_ref, bg_ref, a_ref):
        row = lax.broadcasted_iota(jnp.int32, (L, 128), 0)

        def conv(x_ref, w_ref, b_ref):
            x = x_ref[...]
            out = b_ref[...] + jnp.where(row >= 2, pltpu.roll(x, 2, 0), 0.0) * w_ref[0:1, :]
            out = out + jnp.where(row >= 1, pltpu.roll(x, 1, 0), 0.0) * w_ref[1:2, :]
            return out + x * w_ref[2:3, :]

        val = conv(v_ref, wv_ref, bv_ref)
        a_ref[...] = (_gelu(conv(g_ref, wg_ref, bg_ref)) * val).astype(a_ref.dtype)

    col = lambda off: pl.BlockSpec((L, 128), lambda j: (0, j + off))
    wsp = lambda off: pl.BlockSpec((3, 128), lambda j: (0, j + off))
    bsp = lambda off: pl.BlockSpec((1, 128), lambda j: (0, j + off))
    return pl.pallas_call(
        body, name="convglu_fwd", grid=(NCB,),
        in_specs=[col(0), col(NCB), wsp(0), wsp(NCB), bsp(0), bsp(NCB)],
        out_specs=pl.BlockSpec((L, 128), lambda j: (0, j)),
        out_shape=jax.ShapeDtypeStruct((L, DFF), bf16),
        compiler_params=_cp(("parallel",)),
    )(up_pre, up_pre, cw, cw, cb, cb)


def _convglu_bwd(da, up_pre, cw, cb, comm=None):
    L = up_pre.shape[0]
    R = _conv_rows(L)
    nch = L // R
    n = R + 8

    def body(da_ref, v_ref, g_ref, wv_ref, wg_ref, bv_ref, bg_ref, dx_ref, dw_ref, db_ref):
        def chunk(i, acc):
            base = pl.multiple_of(i * R, R)
            val, xv, xv1, xv2 = _conv_chunk(v_ref, wv_ref, bv_ref, i, nch, R, L, True)
            gate, xg, xg1, xg2 = _conv_chunk(g_ref, wg_ref, bg_ref, i, nch, R, L, True)
            nxt = da_ref[pl.ds(pl.multiple_of(jnp.minimum(base + R, L - 8), 8), 8), :]
            dav = jnp.concatenate([da_ref[pl.ds(base, R), :], jnp.where(i < nch - 1, nxt, 0.0)], axis=0)
            c = math.sqrt(2.0 / math.pi)
            t = jnp.tanh(c * (gate + 0.044715 * (gate * gate * gate)))
            d_val = dav * (0.5 * gate * (1.0 + t))
            d_gate = dav * val * (0.5 * (1.0 + t)
                                  + 0.5 * gate * (1.0 - t * t) * c * (1.0 + 3.0 * 0.044715 * (gate * gate)))
            new = []
            for k, (d, w_ref, xs) in enumerate(((d_val, wv_ref, (xv2, xv1, xv)), (d_gate, wg_ref, (xg2, xg1, xg)))):
                dx = d * w_ref[2:3, :] + pltpu.roll(d, n - 1, 0) * w_ref[1:2, :]
                dx = dx + pltpu.roll(d, n - 2, 0) * w_ref[0:1, :]
                dx_ref[k, pl.ds(base, R), :] = dx[:R].astype(dx_ref.dtype)
                dr = d[:R]
                new += [_fold8(dr * x[:R]) for x in xs] + [_fold8(dr)]
            return tuple(a + b for a, b in zip(acc, new))

        acc = lax.fori_loop(0, nch, chunk, (jnp.zeros((8, 128), f32),) * 8)
        for k in range(2):
            for r in range(3):
                dw_ref[k, r:r + 1, :] = jnp.sum(acc[4 * k + r], axis=0, keepdims=True)
            db_ref[k] = jnp.sum(acc[4 * k + 3], axis=0, keepdims=True)

    col = lambda off: pl.BlockSpec((L, 128), lambda j: (0, j + off))
    wsp = lambda off: pl.BlockSpec((3, 128), lambda j: (0, j + off))
    bsp = lambda off: pl.BlockSpec((1, 128), lambda j: (0, j + off))
    (dx, dw, db), couts = _call(
        body, comm, name="convglu_bwd", grid=(NCB,),
        in_specs=[col(0), col(0), col(NCB), wsp(0), wsp(NCB), bsp(0), bsp(NCB)],
        out_specs=[pl.BlockSpec((2, L, 128), lambda j: (0, 0, j)), pl.BlockSpec((2, 3, 128), lambda j: (0, 0, j)),
                   pl.BlockSpec((2, 1, 128), lambda j: (0, 0, j))],
        out_shape=[jax.ShapeDtypeStruct((2, L, DFF), bf16), jax.ShapeDtypeStruct((2, 3, DFF), f32),
                   jax.ShapeDtypeStruct((2, 1, DFF), f32)], scratch_shapes=[],
        sem=("parallel",), args=(da, up_pre, up_pre, cw, cw, cb, cb))
    return dx, dw.transpose(1, 0, 2).reshape(3, DUP), db.transpose(1, 0, 2).reshape(1, DUP), couts


def _attn_block(L):
    return 256 if L % 256 == 0 else 128


TAIL_DEAD = -110.0


def _dot_exact2(x, m):
    hi = x.astype(bf16)
    lo = (x - hi.astype(f32)).astype(bf16)
    return (lax.dot_general(hi, m, NN, preferred_element_type=f32)
            + lax.dot_general(lo, m, NN, preferred_element_type=f32))


def _sb_weights(z, mask, tri_gt, carry):
    l1p = jnp.log(1.0 + jnp.exp(-jnp.abs(z)))
    ls_pos = jnp.minimum(z, 0.0) - l1p
    lm = ls_pos - z
    if mask is not None:
        lm = jnp.where(mask, lm, 0.0)
    tail = _dot_exact2(lm, tri_gt) + carry
    w = jnp.exp(ls_pos + tail)
    if mask is not None:
        w = jnp.where(mask, w, 0.0)
    return w, ls_pos, lm


def _head_masks(B):
    lane = lax.broadcasted_iota(jnp.int32, (B, 128), 1)
    return lane < HD, lane >= HD


def _attn_fwd(qkv, comm=None):
    L = qkv.shape[0]
    B = _attn_block(L)
    nq = L // B

    def body(q_ref, k_ref, v_ref, o_ref):
        qi = pl.program_id(1)
        heads = _head_masks(B)
        ti = lax.broadcasted_iota(jnp.int32, (B, B), 0)
        si = lax.broadcasted_iota(jnp.int32, (B, B), 1)
        tri_gt = jnp.where(ti > si, 1.0, 0.0).astype(bf16)
        diag = si < ti
        qv = q_ref[...]
        zero16 = jnp.zeros((), bf16)

        def tile(jb, mask, acc, carries):
            ks = pl.multiple_of(jb * B, B)
            kb = k_ref[pl.ds(ks, B), :]
            vb = v_ref[pl.ds(ks, B), :]
            out = []
            for in_head, carry in zip(heads, carries):
                kh = jnp.where(in_head, kb, zero16)
                vh = jnp.where(in_head, vb, zero16)
                z = lax.dot_general(qv, kh, NT, preferred_element_type=f32)
                w, _, lm = _sb_weights(z, mask, tri_gt, carry)
                acc = acc + lax.dot_general(w.astype(bf16), vh, NN, preferred_element_type=f32)
                out.append(carry + jnp.sum(lm, axis=1, keepdims=True))
            return acc, out

        zc = jnp.zeros((B, 1), f32)
        acc, (c0, c1) = tile(qi, diag, jnp.zeros((B, 128), f32), (zc, zc))

        def cond(st):
            return jnp.logical_and(st[0] <= qi, st[4] > TAIL_DEAD)

        def step(st):
            n, a, p0, p1, _ = st
            a, (p0, p1) = tile(qi - n, None, a, (p0, p1))
            return n + 1, a, p0, p1, jnp.maximum(jnp.max(p0), jnp.max(p1))

        st = lax.while_loop(cond, step, (jnp.int32(1), acc, c0, c1, jnp.maximum(jnp.max(c0), jnp.max(c1))))
        o_ref[...] = st[1]

    (o,), couts = _call(
        body, comm, name="attn_fwd", grid=(4, nq),
        in_specs=[pl.BlockSpec((B, 128), lambda hp, i: (i, hp)),
                  pl.BlockSpec((L, 128), lambda hp, i: (0, 4 + hp)),
                  pl.BlockSpec((L, 128), lambda hp, i: (0, 8 + hp))],
        out_specs=[pl.BlockSpec((B, 128), lambda hp, i: (i, hp))],
        out_shape=[jax.ShapeDtypeStruct((L, ATT), f32)], scratch_shapes=[],
        sem=("parallel", "parallel"), args=(qkv, qkv, qkv))
    return o, couts


def _attn_bwd(qkv, o, do, comm=None):
    L = qkv.shape[0]
    B = _attn_block(L)
    nq = L // B

    def body(q_ref, k_ref, v_ref, o_ref, do_ref, dq_ref, dk_ref, dv_ref):
        qi = pl.program_id(1)

        @pl.when(qi == 0)
        def _():
            dk_ref[...] = jnp.zeros_like(dk_ref)
            dv_ref[...] = jnp.zeros_like(dv_ref)

        heads = _head_masks(B)
        ti = lax.broadcasted_iota(jnp.int32, (B, B), 0)
        si = lax.broadcasted_iota(jnp.int32, (B, B), 1)
        tri_gt = jnp.where(ti > si, 1.0, 0.0).astype(bf16)
        tri_ge = jnp.where(ti >= si, 1.0, 0.0).astype(bf16)
        diag = si < ti
        zero16 = jnp.zeros((), bf16)
        qv = q_ref[...]
        do16 = do_ref[...].astype(bf16)
        dsum_lanes = do16.astype(f32) * o_ref[...]
        qhs = [jnp.where(m, qv, zero16) for m in heads]
        dohs = [jnp.where(m, do16, zero16) for m in heads]
        dsums = [jnp.sum(jnp.where(m, dsum_lanes, 0.0), axis=1, keepdims=True) for m in heads]

        def tile(jb, mask, dq_a, state):
            ks = pl.multiple_of(jb * B, B)
            kb = k_ref[pl.ds(ks, B), :]
            vb = v_ref[pl.ds(ks, B), :]
            dk_blk = jnp.zeros((B, 128), f32)
            dv_blk = jnp.zeros((B, 128), f32)
            out = []
            for in_head, qh, doh, dsum, (carry, suffix) in zip(heads, qhs, dohs, dsums, state):
                kh = jnp.where(in_head, kb, zero16)
                vh = jnp.where(in_head, vb, zero16)
                z = lax.dot_general(qv, kh, NT, preferred_element_type=f32)
                w, ls_pos, lm = _sb_weights(z, mask, tri_gt, carry)
                w16 = w.astype(bf16)
                dw = lax.dot_general(doh, vh, NT, preferred_element_type=f32)
                da = w16.astype(f32) * dw
                sig = jnp.exp(ls_pos)
                suf = _dot_exact2(da, tri_ge) + suffix
                dz = da * (1.0 - sig) - sig * (dsum - suf)
                if mask is not None:
                    dz = jnp.where(mask, dz, 0.0)
                dz16 = dz.astype(bf16)
                dq_a = dq_a + lax.dot_general(dz16, kh, NN, preferred_element_type=f32)
                dk_blk = dk_blk + lax.dot_general(dz16, qh, TN, preferred_element_type=f32)
                dv_blk = dv_blk + lax.dot_general(w16, doh, TN, preferred_element_type=f32)
                out.append((carry + jnp.sum(lm, axis=1, keepdims=True),
                            suffix + jnp.sum(da, axis=1, keepdims=True)))
            dk_ref[pl.ds(ks, B), :] += dk_blk
            dv_ref[pl.ds(ks, B), :] += dv_blk
            return dq_a, out

        def alive(state):
            return jnp.maximum(jnp.max(state[0][0]), jnp.max(state[1][0]))

        zc = jnp.zeros((B, 1), f32)
        dq_acc, state = tile(qi, diag, jnp.zeros((B, 128), f32), ((zc, zc), (zc, zc)))

        def cond(st):
            return jnp.logical_and(st[0] <= qi, st[6] > TAIL_DEAD)

        def step(st):
            n, a, c0, s0, c1, s1, _ = st
            a, new = tile(qi - n, None, a, ((c0, s0), (c1, s1)))
            return n + 1, a, new[0][0], new[0][1], new[1][0], new[1][1], alive(new)

        st = lax.while_loop(cond, step, (jnp.int32(1), dq_acc, state[0][0], state[0][1],
                                         state[1][0], state[1][1], alive(state)))
        dq_ref[...] = st[1] * (1.0 / math.sqrt(HD))

    blk = pl.BlockSpec((B, 128), lambda hp, i: (i, hp))
    return _call(
        body, comm, name="attn_bwd", grid=(4, nq),
        in_specs=[blk,
                  pl.BlockSpec((L, 128), lambda hp, i: (0, 4 + hp)),
                  pl.BlockSpec((L, 128), lambda hp, i: (0, 8 + hp)),
                  blk, blk],
        out_specs=[blk, pl.BlockSpec((L, 128), lambda hp, i: (0, hp)),
                   pl.BlockSpec((L, 128), lambda hp, i: (0, hp))],
        out_shape=[jax.ShapeDtypeStruct((L, ATT), f32)] * 3, scratch_shapes=[],
        sem=("parallel", "arbitrary"), args=(qkv, qkv, qkv, o, do))


def _s5_disc(ldt, ar, ai, br, bi):
    dt = jnp.exp(ldt)
    mag = jnp.exp(dt * ar)
    abar_re = mag * jnp.cos(dt * ai)
    abar_im = mag * jnp.sin(dt * ai)
    em_re = abar_re - 1.0
    em_im = abar_im
    den = ar * ar + ai * ai
    f_re = (em_re * ar + em_im * ai) / den
    f_im = (em_im * ar - em_re * ai) / den
    bb_re = f_re * br - f_im * bi
    bb_im = f_re * bi + f_im * br
    return abar_re, abar_im, bb_re, bb_im


def _s5_disc_fwd(ldt, ar, ai, br, bi):
    def body(ldt_ref, ar_ref, ai_ref, br_ref, bi_ref, o1, o2, o3, o4):
        outs = _s5_disc(ldt_ref[...], ar_ref[...], ai_ref[...], br_ref[...], bi_ref[...])
        for o_ref, v in zip((o1, o2, o3, o4), outs):
            o_ref[...] = v

    col = jax.ShapeDtypeStruct((NS, 1), f32)
    mat = jax.ShapeDtypeStruct((NS, NH16), f32)
    return pl.pallas_call(body, name="s5_disc_fwd", out_shape=[col, col, mat, mat],
                          compiler_params=_cp())(ldt, ar, ai, br, bi)


def _s5_disc_bwd(ldt, ar, ai, br, bi, d_are, d_aim, d_bbre, d_bbim):
    def body(ldt_ref, ar_ref, ai_ref, br_ref, bi_ref, c1, c2, c3, c4, o_ldt, o_ar, o_ai, o_br, o_bi):
        prim = (ldt_ref[...], ar_ref[...], ai_ref[...], br_ref[...], bi_ref[...])
        _, vjp = jax.vjp(_s5_disc, *prim)
        g_ldt, g_ar, g_ai, g_br, g_bi = vjp((c1[...], c2[...], c3[...], c4[...]))
        o_ar[...] = g_ar
        o_ai[...] = g_ai
        o_br[...] = g_br
        o_bi[...] = g_bi
        r = lax.broadcasted_iota(jnp.int32, (NG, NS), 0)
        c = lax.broadcasted_iota(jnp.int32, (NG, NS), 1)
        fold = jnp.where(c // NP == r, 1.0, 0.0).astype(bf16)
        hi = jnp.broadcast_to(g_ldt, (NS, 128))
        p1 = hi.astype(bf16)
        r1 = hi - p1.astype(f32)
        p2 = r1.astype(bf16)
        p3 = (r1 - p2.astype(f32)).astype(bf16)
        dd = lambda p: lax.dot_general(fold, p, NN, preferred_element_type=f32)
        o_ldt[...] = dd(p1) + dd(p2) + dd(p3)

    col = jax.ShapeDtypeStruct((NS, 1), f32)
    mat = jax.ShapeDtypeStruct((NS, NH16), f32)
    return pl.pallas_call(
        body, name="s5_disc_bwd",
        out_shape=[jax.ShapeDtypeStruct((NG, 128), f32), col, col, mat, mat],
        compiler_params=_cp())(ldt, ar, ai, br, bi, d_are, d_aim, d_bbre, d_bbim)


def _cmul(ar, ai, br, bi):
    return ar * br - ai * bi, ar * bi + ai * br


def _s5_chunk(L):
    return 256 if L % 256 == 0 else L


def _diag_fold(acc, rpg):
    aw = acc.shape[0]
    r = lax.broadcasted_iota(jnp.int32, (aw, 128), 0) // rpg
    c = lax.broadcasted_iota(jnp.int32, (aw, 128), 1) // NH16
    fr = lax.broadcasted_iota(jnp.int32, (128, NH16), 0) % NH16
    fc = lax.broadcasted_iota(jnp.int32, (128, NH16), 1)
    fold = jnp.where(fr == fc, 1.0, 0.0).astype(bf16)
    return _dot_exact(jnp.where(r == c, acc, 0.0), fold)


def _diag_fold_t(acc):
    r = lax.broadcasted_iota(jnp.int32, (128, 8 * NP), 0) // NH16
    c = lax.broadcasted_iota(jnp.int32, (128, 8 * NP), 1) // NP
    fr = lax.broadcasted_iota(jnp.int32, (128, NH16), 0) % NH16
    fc = lax.broadcasted_iota(jnp.int32, (128, NH16), 1)
    fold = jnp.where(fr == fc, 1.0, 0.0).astype(bf16)
    x = jnp.where(r == c, acc, 0.0)
    hi = x.astype(bf16)
    r1 = x - hi.astype(f32)
    mid = r1.astype(bf16)
    lo = (r1 - mid.astype(f32)).astype(bf16)
    d = lambda piece: lax.dot_general(piece, fold, TN, preferred_element_type=f32)
    return d(hi) + d(mid) + d(lo)


NTILE = NS // 128
TILE_GROUP = 4


def _seg_entry(lr, li, ar, ai, cr, ci, seg_rows, reverse, row):
    pr, pi = ar, ai
    for _ in range(seg_rows.bit_length() - 1):
        pr, pi = _cmul(pr, pi, pr, pi)
    if reverse:
        xr = jnp.where(row == 7, cr, pltpu.roll(lr, 7, 0))
        xi = jnp.where(row == 7, ci, pltpu.roll(li, 7, 0))
    else:
        xr = jnp.where(row == 0, cr, pltpu.roll(lr, 1, 0))
        xi = jnp.where(row == 0, ci, pltpu.roll(li, 1, 0))
    for d in (1, 2, 4):
        keep = (row < 8 - d) if reverse else (row >= d)
        shift = 8 - d if reverse else d
        mr, mi = _cmul(pr, pi, jnp.where(keep, pltpu.roll(xr, shift, 0), 0.0),
                       jnp.where(keep, pltpu.roll(xi, shift, 0), 0.0))
        xr, xi = xr + mr, xi + mi
        pr, pi = _cmul(pr, pi, pr, pi)
    return xr, xi


def _seg_scan(xr3, xi3, abar_re, abar_im, cr_ref, ci_ref, reverse, on_step=None):
    pitch = xr3.shape[1] // 8
    seg = pitch - 8
    row = lax.broadcasted_iota(jnp.int32, (8, 128), 0)
    for g0 in range(0, NTILE, TILE_GROUP):
        tiles = list(range(g0, g0 + TILE_GROUP))
        lanes = [slice(128 * j, 128 * (j + 1)) for j in tiles]
        ar = [jnp.broadcast_to(abar_re[:, ls], (8, 128)) for ls in lanes]
        ai = [jnp.broadcast_to(-abar_im[:, ls] if reverse else abar_im[:, ls], (8, 128)) for ls in lanes]

        def sweep(store):
            def step(n, st):
                k = (seg - 1 - n) if reverse else n
                rows = pl.ds(k, 8, stride=pitch)
                new = []
                for q, j in enumerate(tiles):
                    sr, si = st[2 * q], st[2 * q + 1]
                    if store and on_step is not None:
                        on_step(j, rows, sr, si)
                    nr = ar[q] * sr - ai[q] * si + xr3.at[j][rows, :]
                    ni = ar[q] * si + ai[q] * sr + xi3.at[j][rows, :]
                    if store:
                        xr3.at[j][rows, :] = nr
                        xi3.at[j][rows, :] = ni
                    new += [nr, ni]
                return tuple(new)
            return step

        zero = jnp.zeros((8, 128), f32)
        ends = lax.fori_loop(0, seg, sweep(False), (zero,) * (2 * TILE_GROUP))
        init = []
        for q, ls in enumerate(lanes):
            init += _seg_entry(ends[2 * q], ends[2 * q + 1], ar[q], ai[q], cr_ref[:, ls], ci_ref[:, ls],
                               seg, reverse, row)
        fin = lax.fori_loop(0, seg, sweep(True), tuple(init))
        edge = slice(0, 1) if reverse else slice(7, 8)
        for q, ls in enumerate(lanes):
            cr_ref[:, ls] = fin[2 * q][edge, :]
            ci_ref[:, ls] = fin[2 * q + 1][edge, :]


def _coarse(i):
    return slice(128 * i, 128 * (i + 1)), slice(512 * i, 512 * (i + 1))


def _seg_pitch(T):
    return T // 8 + 8


def _put_rows(x3, j, val):
    pitch = x3.shape[1] // 8
    seg = pitch - 8
    for s in range(8):
        x3[j, pitch * s:pitch * s + seg, :] = val[seg * s:seg * (s + 1), :]


def _get_rows(x3, j):
    pitch = x3.shape[1] // 8
    seg = pitch - 8
    return jnp.concatenate([x3[j, pitch * s:pitch * s + seg, :] for s in range(8)], axis=0)


def _tiles_cat(x3, i):
    return jnp.concatenate([_get_rows(x3, 4 * i + q) for q in range(4)], axis=1)


def _s5_gate2(sr3, si3, u, ccr_ref, cci_ref, dsk_ref, gw_ref, gb_ref):
    ys = []
    for i in range(4):
        ch, st = _coarse(i)
        ys.append(_dot(_tiles_cat(sr3, i), ccr_ref[st, ch]) - _dot(_tiles_cat(si3, i), cci_ref[st, ch]))
    y = jnp.concatenate(ys, axis=1) + dsk_ref[...] * u
    yg = _gelu(y)
    gl = jnp.concatenate([_dot(yg[:, _coarse(i)[0]], gw_ref[_coarse(i)[0], _coarse(i)[0]]) for i in range(4)],
                         axis=1)
    return y, yg, _sigmoid(gl + gb_ref[...])


def _s5_fwd2(p, abar_re, abar_im, bbr, bbi, ccr, cci, dsk, gw, gb):
    L = p.shape[0]
    T = _s5_chunk(L)

    def body(u_ref, are_ref, aim_ref, bbr_ref, bbi_ref, ccr_ref, cci_ref, dsk_ref, gw_ref, gb_ref,
             o_ref, sr3, si3, cr_ref, ci_ref):
        @pl.when(pl.program_id(0) == 0)
        def _():
            cr_ref[...] = jnp.zeros_like(cr_ref)
            ci_ref[...] = jnp.zeros_like(ci_ref)

        u = u_ref[...]
        for i in range(4):
            ch, st = _coarse(i)
            br = _dot(u[:, ch], bbr_ref[ch, st])
            bi = _dot(u[:, ch], bbi_ref[ch, st])
            for q in range(4):
                _put_rows(sr3, 4 * i + q, br[:, 128 * q:128 * (q + 1)])
                _put_rows(si3, 4 * i + q, bi[:, 128 * q:128 * (q + 1)])
        _seg_scan(sr3, si3, are_ref[...], aim_ref[...], cr_ref, ci_ref, reverse=False)
        _, yg, gate = _s5_gate2(sr3, si3, u, ccr_ref, cci_ref, dsk_ref, gw_ref, gb_ref)
        o_ref[...] = yg * gate

    full = lambda shape: pl.BlockSpec(shape, lambda i: (0, 0))
    rows = 8 * _seg_pitch(T)
    st3 = pl.BlockSpec((NTILE, rows, 128), lambda i: (0, i, 0))
    return pl.pallas_call(
        body, name="s5_fwd", grid=(L // T,),
        in_specs=[pl.BlockSpec((T, SSMW), lambda i: (i, 3)), full((1, NS)), full((1, NS)),
                  full((SSMW, NS)), full((SSMW, NS)), full((NS, SSMW)), full((NS, SSMW)),
                  full((1, SSMW)), full((SSMW, SSMW)), full((1, SSMW))],
        out_specs=[pl.BlockSpec((T, SSMW), lambda i: (i, 0)), st3, st3],
        out_shape=[jax.ShapeDtypeStruct((L, SSMW), f32)]
                  + [jax.ShapeDtypeStruct((NTILE, (L // T) * rows, 128), f32)] * 2,
        scratch_shapes=[pltpu.VMEM((1, NS), f32), pltpu.VMEM((1, NS), f32)],
        compiler_params=_cp(("arbitrary",)),
    )(p, abar_re, abar_im, bbr, bbi, ccr, cci, dsk, gw, gb)


def _s5_bwd2(dout, p, s_re, s_im, abar_re, abar_im, bbr, bbi, ccr, cci, dsk, gw, gb):
    L = p.shape[0]
    T = _s5_chunk(L)
    nchunk = L // T

    def body(do_ref, u_ref, sr3, si3, are_ref, aim_ref, bbr_ref, bbi_ref, ccr_ref, cci_ref,
             dsk_ref, gw_ref, gb_ref,
             du_ref, dare_ref, daim_ref, dd_ref, dgb_ref, dccr_ref, dcci_ref, dbbr_ref, dbbi_ref, dgw_ref,
             lr3, li3, cr_ref, ci_ref, accr3, acci3, wacc_ref, gacc_ref):
        i = pl.program_id(0)

        @pl.when(i == 0)
        def _():
            for ref in (cr_ref, ci_ref, accr3, acci3, dd_ref, dgb_ref, wacc_ref, gacc_ref):
                ref[...] = jnp.zeros_like(ref)

        u = u_ref[...]
        dov = do_ref[...]
        y, yg, gate = _s5_gate2(sr3, si3, u, ccr_ref, cci_ref, dsk_ref, gw_ref, gb_ref)
        dgl = dov * yg * gate * (1.0 - gate)
        dyg = dov * gate + jnp.concatenate(
            [_dot(dgl[:, _coarse(q)[0]], gw_ref[_coarse(q)[0], _coarse(q)[0]], NT) for q in range(4)], axis=1)
        dy = dyg * _gelu_grad(y)
        u16, dy16, yg16, dgl16 = (t.astype(_MXU) for t in (u, dy, yg, dgl))
        dd_ref[...] += jnp.sum(dy * u, axis=0, keepdims=True)
        dgb_ref[...] += jnp.sum(dgl, axis=0, keepdims=True)
        for q in range(4):
            ch, st = _coarse(q)
            wacc_ref[0, q] += _dot(dy16[:, ch], _tiles_cat(sr3, q), TN)
            wacc_ref[1, q] += _dot(dy16[:, ch], _tiles_cat(si3, q), TN)
            gacc_ref[q] += _dot(yg16[:, ch], dgl16[:, ch], TN)
            gr = _dot(dy16[:, ch], ccr_ref[st, ch], NT)
            gi = -_dot(dy16[:, ch], cci_ref[st, ch], NT)
            for t in range(4):
                _put_rows(lr3, 4 * q + t, gr[:, 128 * t:128 * (t + 1)])
                _put_rows(li3, 4 * q + t, gi[:, 128 * t:128 * (t + 1)])

        def on_step(j, rows, nxt_r, nxt_i):
            s_r = sr3.at[j][rows, :]
            s_i = si3.at[j][rows, :]
            accr3[j] += nxt_r * s_r + nxt_i * s_i
            acci3[j] += nxt_i * s_r - nxt_r * s_i

        _seg_scan(lr3, li3, are_ref[...], aim_ref[...], cr_ref, ci_ref, reverse=True, on_step=on_step)
        dus = []
        for q in range(4):
            ch, st = _coarse(q)
            lam_r, lam_i = _tiles_cat(lr3, q), _tiles_cat(li3, q)
            wacc_ref[2, q] += _dot(u16[:, ch], lam_r, TN)
            wacc_ref[3, q] += _dot(u16[:, ch], lam_i, TN)
            dus.append(_dot(lam_r, bbr_ref[ch, st], NT) + _dot(lam_i, bbi_ref[ch, st], NT))
        du_ref[...] = dy * dsk_ref[...] + jnp.concatenate(dus, axis=1)

        @pl.when(i == nchunk - 1)
        def _():
            for j in range(NTILE):
                ls = slice(128 * j, 128 * (j + 1))
                dare_ref[:, ls] = jnp.sum(accr3[j], axis=0, keepdims=True)
                daim_ref[:, ls] = jnp.sum(acci3[j], axis=0, keepdims=True)
            for m, o_ref in enumerate((dccr_ref, dcci_ref, dbbr_ref, dbbi_ref)):
                for q in range(4):
                    o_ref[512 * q:512 * (q + 1), :] = _diag_fold_t(wacc_ref[m, q])
            for q in range(4):
                dgw_ref[128 * q:128 * (q + 1), :] = _diag_fold(gacc_ref[q], NH16)

    rev = lambda i: (nchunk - 1 - i, 0)
    full = lambda shape: pl.BlockSpec(shape, lambda i: (0, 0))
    half = pl.BlockSpec((T, SSMW), rev)
    rows = 8 * _seg_pitch(T)
    st3 = pl.BlockSpec((NTILE, rows, 128), lambda i: (0, nchunk - 1 - i, 0))
    return pl.pallas_call(
        body, name="s5_bwd", grid=(nchunk,),
        in_specs=[half, pl.BlockSpec((T, SSMW), lambda i: (nchunk - 1 - i, 3)), st3, st3,
                  full((1, NS)), full((1, NS)), full((SSMW, NS)), full((SSMW, NS)),
                  full((NS, SSMW)), full((NS, SSMW)), full((1, SSMW)), full((SSMW, SSMW)), full((1, SSMW))],
        out_specs=[half, full((1, NS)), full((1, NS)), full((1, SSMW)), full((1, SSMW)),
                   full((NS, NH16)), full((NS, NH16)), full((NS, NH16)), full((NS, NH16)),
                   full((SSMW, NH16))],
        out_shape=[jax.ShapeDtypeStruct((L, SSMW), f32)] + [jax.ShapeDtypeStruct((1, NS), f32)] * 2
                  + [jax.ShapeDtypeStruct((1, SSMW), f32)] * 2 + [jax.ShapeDtypeStruct((NS, NH16), f32)] * 4
                  + [jax.ShapeDtypeStruct((SSMW, NH16), f32)],
        scratch_shapes=[pltpu.VMEM((NTILE, rows, 128), f32), pltpu.VMEM((NTILE, rows, 128), f32),
                        pltpu.VMEM((1, NS), f32), pltpu.VMEM((1, NS), f32),
                        pltpu.VMEM((NTILE, 8, 128), f32), pltpu.VMEM((NTILE, 8, 128), f32),
                        pltpu.VMEM((4, 4, 128, 512), f32), pltpu.VMEM((4, 128, 128), f32)],
        compiler_params=_cp(("arbitrary",)),
    )(dout, p, s_re, s_im, abar_re, abar_im, bbr, bbi, ccr, cci, dsk, gw, gb)


def _block_diag(x):
    g, r, c = x.shape
    eye = jnp.eye(g, dtype=x.dtype)
    return (x[:, :, None, :] * eye[:, None, :, None]).reshape(g * r, g * c)


def _mod_fwd(c16, ada_w, ada_b_loc):
    nloc = ada_w.shape[2]

    def body(c_ref, w_ref, b_ref, o_ref):
        cv = c_ref[...]
        act = cv * _sigmoid(cv)
        o_ref[...] = _dot(act, w_ref[...]) + b_ref[...]

    return pl.pallas_call(
        body, name="mod_fwd", grid=(DEPTH,),
        in_specs=[pl.BlockSpec((16, D), lambda l: (0, 0)), pl.BlockSpec((None, D, nloc), lambda l: (l, 0, 0)),
                  pl.BlockSpec((None, 1, nloc), lambda l: (l, 0, 0))],
        out_specs=pl.BlockSpec((None, 16, nloc), lambda l: (l, 0, 0)),
        out_shape=jax.ShapeDtypeStruct((DEPTH, 16, nloc), f32),
        compiler_params=_cp(("parallel",)),
    )(c16, ada_w, ada_b_loc)


def _ada_grad(c16, dmod16):
    nloc = dmod16.shape[2]

    def body(c_ref, d_ref, o_ref):
        cv = c_ref[...]
        act = cv * _sigmoid(cv)
        o_ref[...] = _dot(act, d_ref[...], TN)

    return pl.pallas_call(
        body, name="ada_grad", grid=(DEPTH,),
        in_specs=[pl.BlockSpec((16, D), lambda l: (0, 0)), pl.BlockSpec((None, 16, nloc), lambda l: (l, 0, 0))],
        out_specs=pl.BlockSpec((None, D, nloc), lambda l: (l, 0, 0)),
        out_shape=jax.ShapeDtypeStruct((DEPTH, D, nloc), f32),
        compiler_params=_cp(("parallel",)),
    )(c16, dmod16)


def _as2d(a):
    return a.reshape(-1, a.shape[-1])


def _ew_rows(rows):
    for cand in (512, 256, 128, 64, 32, 16, 8):
        if rows % cand == 0:
            return cand
    return rows


def _cast_bf16(w):
    w2 = _as2d(w)
    rows, cols = w2.shape
    bm = _ew_rows(rows)

    def body(x_ref, o_ref):
        o_ref[...] = x_ref[...].astype(bf16)

    spec = pl.BlockSpec((bm, cols), lambda i: (i, 0))
    out = pl.pallas_call(body, name="cast_bf16", grid=(rows // bm,), in_specs=[spec], out_specs=spec,
                         out_shape=jax.ShapeDtypeStruct((rows, cols), bf16),
                         compiler_params=_cp(("parallel",)))(w2)
    return out.reshape(w.shape)


def _adamw(parts, w, m, v):
    n = parts.shape[0]
    w2, m2, v2 = _as2d(w), _as2d(m), _as2d(v)
    rows, cols = w2.shape
    p3 = parts.reshape(n, rows, cols)
    bm = _ew_rows(rows)
    if bm * cols * 4 > (1 << 20) and bm % 16 == 0:
        bm //= 2

    def body(p_ref, w_ref, m_ref, v_ref, g_ref, d_ref, nm_ref, nv_ref):
        g = p_ref[0]
        for k in range(1, n):
            g = g + p_ref[k]
        mn = ADAM_B1 * m_ref[...] + (1.0 - ADAM_B1) * g
        vn = ADAM_B2 * v_ref[...] + (1.0 - ADAM_B2) * (g * g)
        m_hat = mn / (1.0 - ADAM_B1 ** ADAM_STEP)
        v_hat = vn / (1.0 - ADAM_B2 ** ADAM_STEP)
        g_ref[...] = g
        d_ref[...] = -ADAM_LR * (m_hat / (jnp.sqrt(v_hat) + ADAM_EPS) + ADAM_WD * w_ref[...])
        nm_ref[...] = mn
        nv_ref[...] = vn

    spec = pl.BlockSpec((bm, cols), lambda i: (i, 0))
    outs = pl.pallas_call(
        body, name="adamw", grid=(rows // bm,),
        in_specs=[pl.BlockSpec((n, bm, cols), lambda i: (0, i, 0)), spec, spec, spec],
        out_specs=[spec] * 4, out_shape=[jax.ShapeDtypeStruct((rows, cols), f32)] * 4,
        compiler_params=_cp(("parallel",)),
    )(p3, w2, m2, v2)
    return tuple(o.reshape(w.shape) for o in outs)


ANY = pl.BlockSpec(memory_space=pl.ANY)


def _my_pos():
    return lax.axis_index("x"), lax.axis_index("y"), lax.axis_index("c")


def _allgather8(name, x):
    def body(x_ref, out_ref, send_sems, recv_sems):
        mx, my, mc = _my_pos()
        me, sibling = (mx, my, mc), (mx, my, 1 - mc)
        chips = [(1 - mx, my), (mx, 1 - my), (1 - mx, 1 - my)]

        def slot(px, py, pc):
            return out_ref.at[4 * px + 2 * py + pc]

        def copy(k, block, to, src=None):
            return pltpu.make_async_remote_copy(
                src_ref=slot(*block) if src is None else src, dst_ref=slot(*block),
                send_sem=send_sems.at[k], recv_sem=recv_sems.at[k], device_id=to, device_id_type=MESH)

        first = [copy(0, me, sibling, src=x_ref)]
        first += [copy(1 + j, me, (*chip, mc), src=x_ref) for j, chip in enumerate(chips)]
        for cp in first:
            cp.start()
        passed = [copy(4 + j, (*chip, mc), sibling) for j, chip in enumerate(chips)]
        for j, chip in enumerate(chips):
            copy(1 + j, (*chip, mc), me).wait_recv()
            passed[j].start()
        copy(0, sibling, me).wait_recv()
        for j, chip in enumerate(chips):
            copy(4 + j, (*chip, 1 - mc), me).wait_recv()
        for cp in first + passed:
            cp.wait_send()

    out = pl.pallas_call(
        body, name=name, in_specs=[ANY], out_specs=ANY,
        out_shape=jax.ShapeDtypeStruct((NDEV,) + x.shape, x.dtype),
        scratch_shapes=[pltpu.SemaphoreType.DMA((7,)), pltpu.SemaphoreType.DMA((7,))],
    )(x)
    dev = 4 * lax.axis_index("x") + 2 * lax.axis_index("y") + lax.axis_index("c")
    return lax.dynamic_update_index_in_dim(out, x, dev, 0)


def _sum_parts(parts):
    n = parts.shape[0]
    p3 = parts.reshape(n, -1, parts.shape[-1])
    rows, cols = p3.shape[1:]
    bm = _ew_rows(rows)
    if bm * cols * 4 > (1 << 20) and bm % 16 == 0:
        bm //= 2

    def body(p_ref, o_ref):
        g = p_ref[0].astype(f32)
        for k in range(1, n):
            g = g + p_ref[k].astype(f32)
        o_ref[...] = g

    out = pl.pallas_call(
        body, name="sum_parts", grid=(rows // bm,),
        in_specs=[pl.BlockSpec((n, bm, cols), lambda i: (0, i, 0))],
        out_specs=pl.BlockSpec((bm, cols), lambda i: (i, 0)),
        out_shape=jax.ShapeDtypeStruct((rows, cols), f32),
        compiler_params=_cp(("parallel",)),
    )(p3)
    return out.reshape(parts.shape[1:])


def _rc(src, dst, ss, rs, k, to):
    return pltpu.make_async_remote_copy(src_ref=src, dst_ref=dst, send_sem=ss.at[k], recv_sem=rs.at[k],
                                        device_id=to, device_id_type=MESH)


def _half_rows(rows, c):
    return pl.ds((rows // 2) * c, rows // 2)


def _gather_comm(shards, l):
    n = len(shards)

    def plan(ins, outs, ss, rs):
        mx, my, mc = _my_pos()
        chips = [(1 - mx, my), (mx, 1 - my), (1 - mx, 1 - my)]
        mine = 2 * mx + my
        sibling = (mx, my, 1 - mc)
        sends, lands, fwds, fwd_lands, own = [], [], [], [], []
        for a in range(n):
            src = ins[a].at[l]
            rows = ins[a].shape[1]
            mr, sr = _half_rows(rows, mc), _half_rows(rows, 1 - mc)
            for j, (px, py) in enumerate(chips):
                k, src_chip = 3 * a + j, 2 * px + py
                sends.append(_rc(src.at[mr], outs[a].at[mine, mr], ss, rs, k, (px, py, mc)))
                lands.append(_rc(src.at[mr], outs[a].at[src_chip, mr], ss, rs, k, (px, py, mc)))
                blk, sblk = outs[a].at[src_chip, mr], outs[a].at[src_chip, sr]
                fwds.append(_rc(blk, blk, ss, rs, 3 * n + k, sibling))
                fwd_lands.append(_rc(sblk, sblk, ss, rs, 3 * n + k, sibling))
            own.append(_rc(src, outs[a].at[mine], ss, rs, 6 * n + a, sibling))
        return sends, lands, fwds, fwd_lands, own

    def start(ins, outs, ss, rs):
        sends, _, _, _, own = plan(ins, outs, ss, rs)
        for cp in sends + own:
            cp.start()

    def finish(ins, outs, ss, rs):
        sends, lands, fwds, fwd_lands, own = plan(ins, outs, ss, rs)
        for land, fwd in zip(lands, fwds):
            land.wait_recv()
            fwd.start()
        for cp in fwd_lands:
            cp.wait_recv()
        for cp in sends + fwds:
            cp.wait_send()
        for cp in own:
            cp.wait()

    return dict(ins=list(shards), nsem=7 * n, start=start, finish=finish,
                out_shapes=[jax.ShapeDtypeStruct((NCHIP,) + a.shape[1:], a.dtype) for a in shards])


def _swap_comm(grads):
    n = len(grads)

    def plan(ins, outs, ss, rs):
        mx, my, mc = _my_pos()
        return [_rc(ins[a].at[j, _half_rows(ins[a].shape[1], 1 - mc)], outs[a].at[j], ss, rs,
                    NCHIP * a + j, (mx, my, 1 - mc)) for a in range(n) for j in range(NCHIP)]

    def start(ins, outs, ss, rs):
        for cp in plan(ins, outs, ss, rs):
            cp.start()

    def finish(ins, outs, ss, rs):
        for cp in plan(ins, outs, ss, rs):
            cp.wait()

    return dict(ins=list(grads), nsem=NCHIP * n, start=start, finish=finish,
                out_shapes=[jax.ShapeDtypeStruct((NCHIP, a.shape[1] // 2) + a.shape[2:], a.dtype)
                            for a in grads])


def _scatter_comm(sums):
    n = len(sums)

    def plan(ins, outs, ss, rs):
        mx, my, mc = _my_pos()
        chips = [(1 - mx, my), (mx, 1 - my), (1 - mx, 1 - my)]
        mine = 2 * mx + my
        sends, lands = [], []
        for a in range(n):
            for j, (px, py) in enumerate(chips):
                sends.append(_rc(ins[a].at[2 * px + py], outs[a].at[mine], ss, rs, 3 * a + j, (px, py, mc)))
                lands.append(_rc(ins[a].at[mine], outs[a].at[2 * px + py], ss, rs, 3 * a + j, (px, py, mc)))
        return sends, lands

    def start(ins, outs, ss, rs):
        for cp in plan(ins, outs, ss, rs)[0]:
            cp.start()

    def finish(ins, outs, ss, rs):
        sends, lands = plan(ins, outs, ss, rs)
        for cp in lands:
            cp.wait_recv()
        for cp in sends:
            cp.wait_send()

    return dict(ins=list(sums), nsem=3 * n, start=start, finish=finish,
                out_shapes=[jax.ShapeDtypeStruct(a.shape, a.dtype) for a in sums])


def _scatter_done(outs, sums):
    mine = 2 * lax.axis_index("x") + lax.axis_index("y")
    return [lax.dynamic_update_index_in_dim(o, lax.dynamic_index_in_dim(a, mine, 0, keepdims=False), mine, 0)
            for o, a in zip(outs, sums)]


def _merge_comm(halves):
    n = len(halves)

    def plan(ins, outs, ss, rs):
        mx, my, mc = _my_pos()
        return [_rc(ins[a], outs[a].at[_half_rows(outs[a].shape[0], mc)], ss, rs, a, (mx, my, 1 - mc))
                for a in range(n)]

    def start(ins, outs, ss, rs):
        for cp in plan(ins, outs, ss, rs):
            cp.start()

    def finish(ins, outs, ss, rs):
        for cp in plan(ins, outs, ss, rs):
            cp.wait()

    return dict(ins=list(halves), nsem=n, start=start, finish=finish,
                out_shapes=[jax.ShapeDtypeStruct((2 * a.shape[0],) + a.shape[1:], a.dtype) for a in halves])


def _merge_done(outs, halves):
    mc = lax.axis_index("c")
    return [lax.dynamic_update_slice_in_dim(o, a, a.shape[0] * mc, 0) for o, a in zip(outs, halves)]


def _comm_call(name, comm):
    ni, no = len(comm["ins"]), len(comm["out_shapes"])

    def body(*refs):
        ins, outs, (ss, rs) = refs[:ni], refs[ni:ni + no], refs[ni + no:]
        comm["start"](ins, outs, ss, rs)
        comm["finish"](ins, outs, ss, rs)

    return pl.pallas_call(
        body, name=name, in_specs=[ANY] * ni, out_specs=[ANY] * no, out_shape=comm["out_shapes"],
        scratch_shapes=[pltpu.SemaphoreType.DMA((comm["nsem"],)), pltpu.SemaphoreType.DMA((comm["nsem"],))],
    )(*comm["ins"])


def _call(body, comm, *, name, grid, in_specs, out_specs, out_shape, scratch_shapes, sem, args):
    if comm is None:
        outs = pl.pallas_call(body, name=name, grid=grid, in_specs=in_specs, out_specs=out_specs,
                              out_shape=out_shape, scratch_shapes=scratch_shapes,
                              compiler_params=_cp(sem))(*args)
        return outs, None
    ni, no = len(in_specs), len(out_specs)
    ci, co = len(comm["ins"]), len(comm["out_shapes"])
    nscr = len(scratch_shapes)

    def carried(*refs):
        ins, cins = refs[:ni], refs[ni:ni + ci]
        outs, couts = refs[ni + ci:ni + ci + no], refs[ni + ci + no:ni + ci + no + co]
        scr = refs[ni + ci + no + co:]
        ss, rs = scr[nscr], scr[nscr + 1]
        pids = [pl.program_id(ax) for ax in range(len(grid))]
        first = functools.reduce(jnp.logical_and, [p == 0 for p in pids])
        last = functools.reduce(jnp.logical_and, [p == g - 1 for p, g in zip(pids, grid)])

        @pl.when(first)
        def _():
            comm["start"](cins, couts, ss, rs)

        body(*ins, *outs, *scr[:nscr])

        @pl.when(last)
        def _():
            comm["finish"](cins, couts, ss, rs)

    outs = pl.pallas_call(
        carried, name=name + "_x", grid=grid, in_specs=list(in_specs) + [ANY] * ci,
        out_specs=list(out_specs) + [ANY] * co, out_shape=list(out_shape) + comm["out_shapes"],
        scratch_shapes=list(scratch_shapes) + [pltpu.SemaphoreType.DMA((comm["nsem"],)),
                                               pltpu.SemaphoreType.DMA((comm["nsem"],))],
        compiler_params=_cp(("arbitrary",) * len(grid)),
    )(*args, *comm["ins"])
    return outs[:no], outs[no:]


def _layer_fwd(x, res, res_gate, wl, mod, comm=None):
    L = x.shape[0]
    bm = _row_block(L, 1024)
    nb = L // bm
    sh1, sc1, g1, sh2, sc2, g2 = [mod[k * D:(k + 1) * D][None, :] for k in range(NMOD)]
    sv = {}
    xin, h = _rmsmod_fwd(x, res, res_gate, wl["norm1_g"], sh1, sc1)
    if xin is None:
        xin = x
    p = _mm("mm_in", h, wl["w_in"], dims=NN, grid=(nb, 4, 1),
            a_block=(bm, D), a_map=lambda i, j, k: (i, 0),
            b_block=(None, D, 512), b_map=lambda i, j, k: (j, 0, 0),
            out_shape=(L, INW), o_block=(bm, 512), o_map=lambda i, j, k: (i, j))
    qkv = _qknorm_fwd(p, wl["gqk"])
    oa, couts = _attn_fwd(qkv, comm)
    os_, s_re, s_im = _s5_fwd2(p, wl["abar_re"], wl["abar_im"], wl["bbr"], wl["bbi"], wl["ccr"], wl["cci"],
                              wl["dsk"], wl["gw"], wl["gb"])
    o = _outnorm_fwd(oa, os_, wl["attn_out_g"], wl["ssm_out_g"])
    proj = _mm("mm_out", o, wl["w_out"], dims=NN, grid=(nb, 2, 1),
               a_block=(bm, D), a_map=lambda i, j, k: (i, 0),
               b_block=(D, 512), b_map=lambda i, j, k: (0, j),
               out_shape=(L, D), o_block=(bm, 512), o_map=lambda i, j, k: (i, j))
    x1, h2 = _rmsmod_fwd(xin, proj, g1, wl["norm2_g"], sh2, sc2)
    up_pre = _mm("mm_up", h2, wl["w_up"], dims=NN, grid=(nb, 4, 1),
                 a_block=(bm, D), a_map=lambda i, j, k: (i, 0),
                 b_block=(None, D, 1408), b_map=lambda i, j, k: (j, 0, 0),
                 out_shape=(L, DUP), o_block=(bm, 1408), o_map=lambda i, j, k: (i, j))
    a = _convglu_fwd(up_pre, wl["conv_w"], wl["conv_b"])
    down = _mm("mm_down", a, wl["w_down"], dims=NN, grid=(nb, 2, 1),
               a_block=(bm, DFF), a_map=lambda i, j, k: (i, 0),
               b_block=(DFF, 512), b_map=lambda i, j, k: (0, j),
               out_shape=(L, D), o_block=(bm, 512), o_map=lambda i, j, k: (i, j))
    sv.update(xin=xin, h=h, p=p, qkv=qkv, oa=oa, os=os_, s_re=s_re, s_im=s_im, o=o, proj=proj,
              x1=x1, h2=h2, up_pre=up_pre, a=a, down=down,
              sc1=sc1, g1=g1, sc2=sc2, g2=g2)
    return x1, down, g2, sv, couts


def _big_grads(g):
    return [g["w_in"], g["w_out"].reshape(NCHIP, D // NCHIP, D), g["w_up"],
            g["w_down"].reshape(NCHIP, DFF // NCHIP, D)]


def _chip_sums(grads, recv):
    core = lax.axis_index("c").astype(jnp.int32).reshape((1,))

    def half_sum(a, b):
        _, half, cols = b.shape
        bm = half if half * cols * 4 <= (3 << 19) else half // 2
        nrb = half // bm

        def body(core_ref, a_ref, b_ref, o_ref):
            o_ref[...] = (a_ref[...] + b_ref[...]).astype(o_ref.dtype)

        blk = pl.BlockSpec((None, bm, cols), lambda j, i, core_ref: (j, i, 0))
        return pl.pallas_call(
            body, name="chip_sum",
            grid_spec=pltpu.PrefetchScalarGridSpec(
                num_scalar_prefetch=1, grid=(NCHIP, nrb),
                in_specs=[pl.BlockSpec((None, bm, cols), lambda j, i, core_ref: (j, core_ref[0] * nrb + i, 0)),
                          blk],
                out_specs=blk),
            out_shape=jax.ShapeDtypeStruct(b.shape, bf16),
            compiler_params=_cp(("parallel", "parallel")),
        )(core, a, b)

    return [half_sum(a, b) for a, b in zip(grads, recv)]


def _reduced(parts, sums):
    halves = [_sum_parts(p) for p in _scatter_done(parts, sums)]
    return _merge_done(_comm_call("grad_merge", _merge_comm(halves)), halves)


def _layer_bwd(dx_out, wl, sv, prev=None):
    L = dx_out.shape[0]
    bm = _row_block(L, 1024)
    nb = L // bm
    bk = _row_block(L, 512)
    nk = L // bk
    g = {}
    dd, dg2 = _gate_bwd(dx_out, sv["down"], sv["g2"])
    da = _mm("mm_down_dx", dd, wl["w_down"], dims=NT, grid=(nb, 2, 1),
             a_block=(bm, D), a_map=lambda i, j, k: (i, 0),
             b_block=(1408, D), b_map=lambda i, j, k: (j, 0),
             out_shape=(L, DFF), o_block=(bm, 1408), o_map=lambda i, j, k: (i, j))
    g["w_down"] = _mm("mm_down_dw", sv["a"], dd, dims=TN, grid=(1, 2, nk),
                      a_block=(bk, DFF), a_map=lambda i, j, k: (k, 0),
                      b_block=(bk, 512), b_map=lambda i, j, k: (k, j),
                      out_shape=(DFF, D), o_block=(DFF, 512), o_map=lambda i, j, k: (0, j))
    dup, g["conv_w"], g["conv_b"], recv = _convglu_bwd(da, sv["up_pre"], wl["conv_w"], wl["conv_b"],
                                                       _swap_comm(prev) if prev else None)
    sums = _chip_sums(prev, recv) if prev else None
    dh2 = _mm("mm_up_dx", dup, wl["w_up"], dims=NT, grid=(nb, 1, 4),
              a_block=(None, bm, 1408), a_map=lambda i, j, k: (k // 2, i, k % 2),
              b_block=(None, D, 1408), b_map=lambda i, j, k: (k, 0, 0),
              out_shape=(L, D), o_block=(bm, D), o_map=lambda i, j, k: (i, 0))
    g["w_up"] = _mm("mm_up_dw", sv["h2"], dup, dims=TN, grid=(1, 4, nk),
                    a_block=(bk, D), a_map=lambda i, j, k: (k, 0),
                    b_block=(None, bk, 1408), b_map=lambda i, j, k: (j // 2, k, j % 2),
                    out_shape=(NCHIP, D, 1408), o_block=(None, D, 1408), o_map=lambda i, j, k: (j, 0, 0))
    dx1, dsh2, dsc2, g["norm2_g"] = _rmsmod_bwd(dh2, sv["x1"], wl["norm2_g"], sv["sc2"], dx_out)
    dproj, dg1 = _gate_bwd(dx1, sv["proj"], sv["g1"])
    do = _mm("mm_out_dx", dproj, wl["w_out"], dims=NT, grid=(nb, 2, 1),
             a_block=(bm, D), a_map=lambda i, j, k: (i, 0),
             b_block=(512, D), b_map=lambda i, j, k: (j, 0),
             out_shape=(L, D), o_block=(bm, 512), o_map=lambda i, j, k: (i, j))
    g["w_out"] = _mm("mm_out_dw", sv["o"], dproj, dims=TN, grid=(1, 2, nk),
                     a_block=(bk, D), a_map=lambda i, j, k: (k, 0),
                     b_block=(bk, 512), b_map=lambda i, j, k: (k, j),
                     out_shape=(D, D), o_block=(D, 512), o_map=lambda i, j, k: (0, j))
    doa, dos, g["attn_out_g"], g["ssm_out_g"] = _outnorm_bwd(do, sv["oa"], sv["os"],
                                                             wl["attn_out_g"], wl["ssm_out_g"])
    (dqn, dkn, dv), parts = _attn_bwd(sv["qkv"], sv["oa"], doa, _scatter_comm(sums) if prev else None)
    prev_reduced = _reduced(parts, sums) if prev else None
    (du, d_abar_re, d_abar_im, g["ssm_d"], g["glu_b"], d_ccr, d_cci, d_bbre, d_bbim, d_gw) = _s5_bwd2(
        dos, sv["p"], sv["s_re"], sv["s_im"], wl["abar_re"], wl["abar_im"], wl["bbr"], wl["bbi"],
        wl["ccr"], wl["cci"], wl["dsk"], wl["gw"], wl["gb"])
    g["ssm_c_re"] = d_ccr.reshape(NG, NP, NH16).transpose(0, 2, 1)
    g["ssm_c_im"] = -d_cci.reshape(NG, NP, NH16).transpose(0, 2, 1)
    g["glu_w"] = d_gw.reshape(NG, NH16, NH16)
    d_ldt, d_ar, d_ai, d_br, d_bi = _s5_disc_bwd(
        wl["ldt_col"], wl["ar_col"], wl["ai_col"], wl["br_mat"], wl["bi_mat"],
        d_abar_re.reshape(NS, 1), d_abar_im.reshape(NS, 1), d_bbre, d_bbim)
    g["ssm_log_dt"] = d_ldt[:, 0]
    g["ssm_a_re"] = d_ar.reshape(NG, NP)
    g["ssm_a_im"] = d_ai.reshape(NG, NP)
    g["ssm_b_re"] = d_br.reshape(NG, NP, NH16)
    g["ssm_b_im"] = d_bi.reshape(NG, NP, NH16)
    dp, dgqk = _qknorm_bwd(dqn, dkn, dv, du, sv["p"], wl["gqk"])
    g["q_norm_g"] = dgqk[0, :HD]
    g["k_norm_g"] = dgqk[0, HD:2 * HD]
    dh = _mm("mm_in_dx", dp, wl["w_in"], dims=NT, grid=(nb, 1, 4),
             a_block=(bm, 512), a_map=lambda i, j, k: (i, k),
             b_block=(None, D, 512), b_map=lambda i, j, k: (k, 0, 0),
             out_shape=(L, D), o_block=(bm, D), o_map=lambda i, j, k: (i, 0))
    g["w_in"] = _mm("mm_in_dw", sv["h"], dp, dims=TN, grid=(1, 4, nk),
                    a_block=(bk, D), a_map=lambda i, j, k: (k, 0),
                    b_block=(bk, 512), b_map=lambda i, j, k: (k, j),
                    out_shape=(NCHIP, D, 512), o_block=(None, D, 512), o_map=lambda i, j, k: (j, 0, 0))
    dx_in, dsh1, dsc1, g["norm1_g"] = _rmsmod_bwd(dh, sv["xin"], wl["norm1_g"], sv["sc1"], dx1)
    g["dmod"] = jnp.concatenate([dsh1, dsc1, dg1, dsh2, dsc2, dg2], axis=1)[0]
    return dx_in, g, prev_reduced


def _prep_layer(l, conv_w, small):
    wl = {}
    wl["conv_w"] = conv_w[l]
    wl["conv_b"] = small["ffn_conv_b"][l][None, :]
    wl["norm1_g"] = small["norm1_g"][l][None, :]
    wl["norm2_g"] = small["norm2_g"][l][None, :]
    wl["attn_out_g"] = small["attn_out_g"][l][None, :]
    wl["ssm_out_g"] = small["ssm_out_g"][l][None, :]
    wl["gqk"] = jnp.concatenate([jnp.tile(small["q_norm_g"][l], 8), jnp.tile(small["k_norm_g"][l], 8)])[None, :]
    wl["ldt_col"] = jnp.repeat(small["ssm_log_dt"][l], NP)[:, None]
    wl["ar_col"] = small["ssm_a_re"][l].reshape(NS, 1)
    wl["ai_col"] = small["ssm_a_im"][l].reshape(NS, 1)
    wl["br_mat"] = small["ssm_b_re"][l].reshape(NS, NH16)
    wl["bi_mat"] = small["ssm_b_im"][l].reshape(NS, NH16)
    abar_re, abar_im, bb_re, bb_im = _s5_disc_fwd(wl["ldt_col"], wl["ar_col"], wl["ai_col"],
                                                  wl["br_mat"], wl["bi_mat"])
    wl["abar_re"] = abar_re.reshape(1, NS)
    wl["abar_im"] = abar_im.reshape(1, NS)
    wl["bbr"] = _block_diag(bb_re.astype(bf16).reshape(NG, NP, NH16).transpose(0, 2, 1))
    wl["bbi"] = _block_diag(bb_im.astype(bf16).reshape(NG, NP, NH16).transpose(0, 2, 1))
    wl["ccr"] = _block_diag(small["ssm_c_re"][l].astype(bf16).transpose(0, 2, 1))
    wl["cci"] = _block_diag(small["ssm_c_im"][l].astype(bf16).transpose(0, 2, 1))
    wl["gw"] = _block_diag(small["glu_w"][l].astype(bf16))
    wl["dsk"] = small["ssm_d"][l].reshape(1, SSMW)
    wl["gb"] = small["glu_b"][l].reshape(1, SSMW)
    return wl


def _local_step(x, target, mods, shards, conv_w, small):
    def set_big(wl, gathered):
        w_in, w_out, w_up, w_down = gathered
        wl.update(w_in=w_in, w_up=w_up, w_out=w_out.reshape(D, D), w_down=w_down.reshape(DFF, D))

    wls = [_prep_layer(l, conv_w, small) for l in range(DEPTH)]
    set_big(wls[0], _comm_call("weight_gather", _gather_comm(shards, 0)))
    saved = []
    cur, res, res_gate = x, None, None
    for l in range(DEPTH):
        nxt = _gather_comm(shards, l + 1) if l + 1 < DEPTH else None
        cur, res, res_gate, sv, couts = _layer_fwd(cur, res, res_gate, wls[l], mods[l], nxt)
        saved.append(sv)
        if nxt is not None:
            set_big(wls[l + 1], couts)
    loss, dx = _loss_fwd_bwd(cur, res, res_gate, target)
    grads = [None] * DEPTH
    big = [None] * DEPTH
    prev = None
    for l in reversed(range(DEPTH)):
        dx, grads[l], done = _layer_bwd(dx, wls[l], saved[l], prev)
        if prev is not None:
            big[l + 1] = done
        prev = _big_grads(grads[l])
    recv = _comm_call("grad_swap", _swap_comm(prev))
    sums = _chip_sums(prev, recv)
    big[0] = _reduced(_comm_call("grad_scatter", _scatter_comm(sums)), sums)
    return loss, dx, grads, big


BIG = ("w_in", "w_out", "ffn_w_up", "ffn_w_down")
SMALL = ("ada_b", "norm1_g", "q_norm_g", "k_norm_g", "ssm_a_re", "ssm_a_im", "ssm_log_dt",
         "ssm_b_re", "ssm_b_im", "ssm_c_re", "ssm_c_im", "ssm_d", "glu_w", "glu_b",
         "attn_out_g", "ssm_out_g", "norm2_g", "ffn_conv_b")
NAMES = ("ada_w", "ada_b", "norm1_g", "w_in", "q_norm_g", "k_norm_g", "ssm_a_re", "ssm_a_im",
         "ssm_log_dt", "ssm_b_re", "ssm_b_im", "ssm_c_re", "ssm_c_im", "ssm_d", "glu_w", "glu_b",
         "attn_out_g", "ssm_out_g", "w_out", "norm2_g", "ffn_w_up", "ffn_conv_w", "ffn_conv_b",
         "ffn_w_down")
PACK_COLS = 1024


def _pack(arrs):
    flat = jnp.concatenate([a.reshape(-1) for a in arrs])
    rows = -(-flat.shape[0] // PACK_COLS)
    rows = -(-rows // 8) * 8
    flat = jnp.pad(flat, (0, rows * PACK_COLS - flat.shape[0]))
    return flat.reshape(rows, PACK_COLS)


def _unpack(packed, shapes):
    flat = packed.reshape(-1)
    out, off = [], 0
    for s in shapes:
        n = math.prod(s)
        out.append(flat[off:off + n].reshape(s))
        off += n
    return out


def kernel(x, c, ada_w, ada_b, norm1_g, w_in, q_norm_g, k_norm_g, ssm_a_re, ssm_a_im, ssm_log_dt, ssm_b_re, ssm_b_im, ssm_c_re, ssm_c_im, ssm_d, glu_w, glu_b, attn_out_g, ssm_out_g, w_out, norm2_g, ffn_w_up, ffn_conv_w, ffn_conv_b, ffn_w_down, loss_target, m_ada_w, m_ada_b, m_norm1_g, m_w_in, m_q_norm_g, m_k_norm_g, m_ssm_a_re, m_ssm_a_im, m_ssm_log_dt, m_ssm_b_re, m_ssm_b_im, m_ssm_c_re, m_ssm_c_im, m_ssm_d, m_glu_w, m_glu_b, m_attn_out_g, m_ssm_out_g, m_w_out, m_norm2_g, m_ffn_w_up, m_ffn_conv_w, m_ffn_conv_b, m_ffn_w_down, v_ada_w, v_ada_b, v_norm1_g, v_w_in, v_q_norm_g, v_k_norm_g, v_ssm_a_re, v_ssm_a_im, v_ssm_log_dt, v_ssm_b_re, v_ssm_b_im, v_ssm_c_re, v_ssm_c_im, v_ssm_d, v_glu_w, v_glu_b, v_attn_out_g, v_ssm_out_g, v_w_out, v_norm2_g, v_ffn_w_up, v_ffn_conv_w, v_ffn_conv_b, v_ffn_w_down):
    env = dict(locals())
    w = {n: env[n] for n in NAMES}
    m = {n: env["m_" + n] for n in NAMES}
    v = {n: env["v_" + n] for n in NAMES}
    mx, my, mc = _my_pos()
    chip = 2 * mx + my
    dev = 4 * mx + 2 * my + mc
    xl = x[0]
    tl = loss_target[0]

    c_all = _allgather8("gather_c", jnp.pad(c, ((0, 7), (0, 0))))[:, 0, :]
    c16 = jnp.pad(c_all, ((0, 8), (0, 0)))
    nloc = NMOD * D // NCHIP
    ada_b_loc = lax.dynamic_slice(ada_b, (0, chip * nloc), (DEPTH, nloc))[:, None, :]
    mod_loc = _mod_fwd(c16, ada_w, ada_b_loc)
    mod_g = _allgather8("gather_mod", mod_loc.reshape(DEPTH * 16, nloc))
    mod_all = mod_g[0::2].reshape(NCHIP, DEPTH, 16, nloc).transpose(1, 2, 0, 3).reshape(DEPTH, 16, NMOD * D)
    mods = lax.dynamic_index_in_dim(mod_all, dev, axis=1, keepdims=False)

    shards = [_cast_bf16(w[n]) for n in BIG]
    ncw = DUP // NCHIP
    conv_g = _allgather8("gather_conv", jnp.pad(w["ffn_conv_w"].reshape(DEPTH * 3, ncw), ((0, 4), (0, 0))))
    conv_w = conv_g[0::2, :DEPTH * 3].reshape(NCHIP, DEPTH, 3, ncw).transpose(1, 2, 0, 3).reshape(DEPTH, 3, DUP)
    small = {n: w[n] for n in SMALL}

    loss_loc, grad_x, grads, big = _local_step(xl, tl, mods, shards, conv_w, small)
    loss = lax.psum(loss_loc, ("x", "y", "c"))

    outs = {}
    for k, name in enumerate(BIG):
        g_big = jnp.stack([big[l][k] for l in range(DEPTH)])
        outs[name] = _adamw(g_big[None], w[name], m[name], v[name])

    small_names = SMALL[1:] + ("ffn_conv_w_full",)
    small_grads = []
    for n in SMALL[1:]:
        key = {"ffn_conv_b": "conv_b"}.get(n, n)
        small_grads.append(jnp.stack([grads[l][key].reshape(w[n].shape[1:]) for l in range(DEPTH)]))
    small_grads.append(jnp.stack([grads[l]["conv_w"] for l in range(DEPTH)]))
    dmod = jnp.stack([grads[l]["dmod"] for l in range(DEPTH)])
    packed = _pack([dmod] + small_grads)
    allp = _allgather8("gather_small", packed)
    shapes = [dmod.shape] + [a.shape for a in small_grads]
    dmod_all = allp[:, :DEPTH * NMOD, :].reshape(NDEV, DEPTH, NMOD * D)
    dmod_loc = lax.dynamic_slice(dmod_all, (0, 0, chip * nloc), (NDEV, DEPTH, nloc)).transpose(1, 0, 2)
    g_ada = _ada_grad(c16, jnp.pad(dmod_loc, ((0, 0), (0, 8), (0, 0))))
    outs["ada_w"] = _adamw(g_ada[None], w["ada_w"], m["ada_w"], v["ada_w"])
    conv_w_shape = (DEPTH, 3, DUP)
    w_small = _pack([w[n] for n in SMALL] + [jnp.zeros(conv_w_shape, f32)])
    m_small = _pack([m[n] for n in SMALL] + [jnp.zeros(conv_w_shape, f32)])
    v_small = _pack([v[n] for n in SMALL] + [jnp.ones(conv_w_shape, f32)])
    res_small = _adamw(allp, w_small, m_small, v_small)
    unpacked = [_unpack(r, shapes) for r in res_small]
    for i, n in enumerate(SMALL):
        outs[n] = tuple(unpacked[k][i] for k in range(4))
    g_conv_full = unpacked[0][len(SMALL)]
    ncw = DUP // NCHIP
    g_conv = lax.dynamic_slice(g_conv_full, (0, 0, chip * ncw), (DEPTH, 3, ncw))
    outs["ffn_conv_w"] = _adamw(g_conv[None], w["ffn_conv_w"], m["ffn_conv_w"], v["ffn_conv_w"])

    result = [loss, grad_x[None]]
    for k in range(4):
        result += [outs[n][k] for n in NAMES]
    return tuple(result)
```

```python
import functools
import math

import jax
import jax.numpy as jnp
from jax import lax
from jax.experimental import pallas as pl
from jax.experimental.pallas import tpu as pltpu

f32 = jnp.float32
bf16 = jnp.bfloat16
_MXU = jnp.bfloat16

D = 1024
ATT = 512
SSMW = 512
HD = 64
NG = 32
NP = 64
NH16 = 16
NS = NG * NP
INW = 2048
DFF = 2816
DUP = 2 * DFF
NMOD = 6
DEPTH = 4
EPS = 1e-6
NCHIP = 4
NDEV = 8

ADAM_LR = 0.001
ADAM_B1 = 0.9
ADAM_B2 = 0.999
ADAM_EPS = 1e-08
ADAM_WD = 0.01
ADAM_STEP = 10

VMEM_LIMIT = 56 * 1024 * 1024
MESH = pl.DeviceIdType.MESH

NN = (((1,), (0,)), ((), ()))
NT = (((1,), (1,)), ((), ()))
TN = (((0,), (0,)), ((), ()))


def _cp(sem=None):
    if sem is None:
        return pltpu.CompilerParams(vmem_limit_bytes=VMEM_LIMIT)
    return pltpu.CompilerParams(dimension_semantics=sem, vmem_limit_bytes=VMEM_LIMIT)


def _dot(a, b, dims=NN):
    return lax.dot_general(a.astype(_MXU), b.astype(_MXU), dims, preferred_element_type=f32)


def _dot_exact(x, m):
    hi = x.astype(bf16)
    r1 = x - hi.astype(f32)
    mid = r1.astype(bf16)
    lo = (r1 - mid.astype(f32)).astype(bf16)
    d = lambda p: lax.dot_general(p, m, NN, preferred_element_type=f32)
    return d(hi) + d(mid) + d(lo)


def _gelu(x):
    c = math.sqrt(2.0 / math.pi)
    return 0.5 * x * (1.0 + jnp.tanh(c * (x + 0.044715 * (x * x * x))))


def _gelu_grad(x):
    c = math.sqrt(2.0 / math.pi)
    t = jnp.tanh(c * (x + 0.044715 * (x * x * x)))
    return 0.5 * (1.0 + t) + 0.5 * x * (1.0 - t * t) * c * (1.0 + 3.0 * 0.044715 * (x * x))


def _sigmoid(x):
    return 1.0 / (1.0 + jnp.exp(-x))


def _mm(name, a, b, *, dims, grid, a_block, a_map, b_block, b_map, out_shape, o_block, o_map,
        out_dtype=f32):
    nk = grid[2]
    acc_shape = tuple(s for s in o_block if s is not None)

    def body(a_ref, b_ref, o_ref, acc_ref):
        k = pl.program_id(2)
        part = _dot(a_ref[...], b_ref[...], dims)
        if nk == 1:
            o_ref[...] = part.astype(o_ref.dtype)
        else:
            @pl.when(k == 0)
            def _():
                acc_ref[...] = part

            @pl.when(k > 0)
            def _():
                acc_ref[...] += part

            @pl.when(k == nk - 1)
            def _():
                o_ref[...] = acc_ref[...].astype(o_ref.dtype)

    return pl.pallas_call(
        body, name=name, grid=grid,
        in_specs=[pl.BlockSpec(a_block, a_map), pl.BlockSpec(b_block, b_map)],
        out_specs=pl.BlockSpec(o_block, o_map),
        out_shape=jax.ShapeDtypeStruct(out_shape, out_dtype),
        scratch_shapes=[pltpu.VMEM(acc_shape if nk > 1 else (8, 128), f32)],
        compiler_params=_cp(("parallel", "parallel", "arbitrary")),
    )(a, b)


def _row_block(L, want):
    return want if L % want == 0 else L


def _rmsmod_fwd(x, res, gate, g, sh, sc):
    L = x.shape[0]
    bm = _row_block(L, 256)
    with_res = res is not None

    def body(*refs):
        if with_res:
            x_ref, r_ref, gt_ref, g_ref, sh_ref, sc_ref, xo_ref, h_ref = refs
            xin = x_ref[...] + gt_ref[...] * r_ref[...]
            xo_ref[...] = xin
        else:
            x_ref, g_ref, sh_ref, sc_ref, h_ref = refs
            xin = x_ref[...]
        inv = lax.rsqrt(jnp.mean(xin * xin, axis=-1, keepdims=True) + EPS)
        xn = xin * inv * g_ref[...]
        h_ref[...] = (xn * (1.0 + sc_ref[...]) + sh_ref[...]).astype(h_ref.dtype)

    row = pl.BlockSpec((bm, D), lambda i: (i, 0))
    vec = pl.BlockSpec((1, D), lambda i: (0, 0))
    if with_res:
        return pl.pallas_call(
            body, name="rmsmod_res_fwd", grid=(L // bm,),
            in_specs=[row, row, vec, vec, vec, vec], out_specs=[row, row],
            out_shape=[jax.ShapeDtypeStruct((L, D), f32), jax.ShapeDtypeStruct((L, D), bf16)],
            compiler_params=_cp(("parallel",)),
        )(x, res, gate, g, sh, sc)
    h = pl.pallas_call(
        body, name="rmsmod_fwd", grid=(L // bm,),
        in_specs=[row, vec, vec, vec], out_specs=row,
        out_shape=jax.ShapeDtypeStruct((L, D), bf16),
        compiler_params=_cp(("parallel",)),
    )(x, g, sh, sc)
    return None, h


def _rmsmod_bwd(dh, x, g, sc, dres):
    L = x.shape[0]
    bm = _row_block(L, 256)

    def body(dh_ref, x_ref, g_ref, sc_ref, dr_ref, dx_ref, dsh_ref, dsc_ref, dg_ref):
        i = pl.program_id(0)
        xv = x_ref[...]
        dhv = dh_ref[...]
        inv = lax.rsqrt(jnp.mean(xv * xv, axis=-1, keepdims=True) + EPS)
        xh = xv * inv
        gv = g_ref[...]
        xn = xh * gv
        dxn = dhv * (1.0 + sc_ref[...])
        dxh = dxn * gv
        dx_ref[...] = inv * (dxh - xh * jnp.mean(dxh * xh, axis=-1, keepdims=True)) + dr_ref[...]
        p_sh = jnp.sum(dhv, axis=0, keepdims=True)
        p_sc = jnp.sum(dhv * xn, axis=0, keepdims=True)
        p_g = jnp.sum(dxn * xh, axis=0, keepdims=True)

        @pl.when(i == 0)
        def _():
            dsh_ref[...] = p_sh
            dsc_ref[...] = p_sc
            dg_ref[...] = p_g

        @pl.when(i > 0)
        def _():
            dsh_ref[...] += p_sh
            dsc_ref[...] += p_sc
            dg_ref[...] += p_g

    row = pl.BlockSpec((bm, D), lambda i: (i, 0))
    vec = pl.BlockSpec((1, D), lambda i: (0, 0))
    return pl.pallas_call(
        body, name="rmsmod_bwd", grid=(L // bm,),
        in_specs=[row, row, vec, vec, row], out_specs=[row, vec, vec, vec],
        out_shape=[jax.ShapeDtypeStruct((L, D), f32)] + [jax.ShapeDtypeStruct((1, D), f32)] * 3,
        compiler_params=_cp(("arbitrary",)),
    )(dh, x, g, sc, dres)


def _gate_bwd(dx, y, gate):
    L = dx.shape[0]
    bm = _row_block(L, 256)

    def body(dx_ref, y_ref, gt_ref, dy_ref, dgt_ref):
        i = pl.program_id(0)
        dxv = dx_ref[...]
        dy_ref[...] = (gt_ref[...] * dxv).astype(dy_ref.dtype)
        part = jnp.sum(dxv * y_ref[...], axis=0, keepdims=True)

        @pl.when(i == 0)
        def _():
            dgt_ref[...] = part

        @pl.when(i > 0)
        def _():
            dgt_ref[...] += part

    row = pl.BlockSpec((bm, D), lambda i: (i, 0))
    vec = pl.BlockSpec((1, D), lambda i: (0, 0))
    return pl.pallas_call(
        body, name="gate_bwd", grid=(L // bm,),
        in_specs=[row, row, vec], out_specs=[row, vec],
        out_shape=[jax.ShapeDtypeStruct((L, D), bf16), jax.ShapeDtypeStruct((1, D), f32)],
        compiler_params=_cp(("arbitrary",)),
    )(dx, y, gate)


def _loss_fwd_bwd(x1, down, gate, target):
    L = x1.shape[0]
    bm = _row_block(L, 256)
    nsteps = L // bm

    def body(x_ref, d_ref, gt_ref, t_ref, dy_ref, loss_ref, acc_ref):
        i = pl.program_id(0)
        diff = x_ref[...] + gt_ref[...] * d_ref[...] - t_ref[...]
        dy_ref[...] = diff * (1.0 / D)
        part = jnp.sum(diff * diff, axis=0, keepdims=True)

        @pl.when(i == 0)
        def _():
            acc_ref[...] = part

        @pl.when(i > 0)
        def _():
            acc_ref[...] += part

        @pl.when(i == nsteps - 1)
        def _():
            tot = jnp.sum(acc_ref[...], axis=1, keepdims=True) * (0.5 / D)
            loss_ref[...] = jnp.broadcast_to(tot, (8, 128))

    row = pl.BlockSpec((bm, D), lambda i: (i, 0))
    vec = pl.BlockSpec((1, D), lambda i: (0, 0))
    dy, loss = pl.pallas_call(
        body, name="loss_fwd_bwd", grid=(nsteps,),
        in_specs=[row, row, vec, row],
        out_specs=[row, pl.BlockSpec((8, 128), lambda i: (0, 0))],
        out_shape=[jax.ShapeDtypeStruct((L, D), f32), jax.ShapeDtypeStruct((8, 128), f32)],
        scratch_shapes=[pltpu.VMEM((1, D), f32)],
        compiler_params=_cp(("arbitrary",)),
    )(x1, down, gate, target)
    return loss[0, 0], dy


def _head_mean_matrix():
    r = lax.broadcasted_iota(jnp.int32, (128, 128), 0) // HD
    c = lax.broadcasted_iota(jnp.int32, (128, 128), 1) // HD
    return jnp.where(r == c, 1.0 / HD, 0.0).astype(bf16)


def _qknorm_fwd(p, gqk):
    L = p.shape[0]
    bm = _row_block(L, 256)

    def body(p_ref, g_ref, o_ref):
        e = _head_mean_matrix()
        for c in range(8):
            sl = slice(128 * c, 128 * (c + 1))
            xv = p_ref[:, sl]
            inv = lax.rsqrt(_dot_exact2(xv * xv, e) + EPS)
            yv = xv * inv * g_ref[:, sl]
            if c < 4:
                yv = yv * (1.0 / math.sqrt(HD))
            o_ref[:, sl] = yv.astype(o_ref.dtype)
        o_ref[:, 1024:1536] = p_ref[:, 1024:1536].astype(o_ref.dtype)

    return pl.pallas_call(
        body, name="qknorm_fwd", grid=(L // bm,),
        in_specs=[pl.BlockSpec((bm, 1536), lambda i: (i, 0)), pl.BlockSpec((1, 1024), lambda i: (0, 0))],
        out_specs=pl.BlockSpec((bm, 1536), lambda i: (i, 0)),
        out_shape=jax.ShapeDtypeStruct((L, 1536), bf16),
        compiler_params=_cp(("parallel",)),
    )(p, gqk)


def _qknorm_bwd(dqn, dkn, dval, du, p, gqk):
    L = p.shape[0]
    bm = _row_block(L, 256)
    nsteps = L // bm

    def body(dq_ref, dk_ref, dv_ref, du_ref, p_ref, g_ref, o_ref, dg_ref, acc_ref):
        i = pl.program_id(0)
        e = _head_mean_matrix()
        o_ref[:, 1024:1536] = dv_ref[...].astype(o_ref.dtype)
        o_ref[:, 1536:2048] = du_ref[...].astype(o_ref.dtype)
        for c in range(8):
            sl = slice(128 * c, 128 * (c + 1))
            xv = p_ref[:, sl]
            d_ref = dq_ref if c < 4 else dk_ref
            dv = d_ref[:, 128 * (c % 4):128 * (c % 4 + 1)]
            gv = g_ref[:, sl]
            inv = lax.rsqrt(_dot_exact2(xv * xv, e) + EPS)
            xh = xv * inv
            dxh = dv * gv
            o_ref[:, sl] = (inv * (dxh - xh * _dot_exact2(dxh * xh, e))).astype(o_ref.dtype)
            part = jnp.sum(dv * xh, axis=0, keepdims=True)

            @pl.when(i == 0)
            def _():
                acc_ref[:, sl] = part

            @pl.when(i > 0)
            def _():
                acc_ref[:, sl] += part

        @pl.when(i == nsteps - 1)
        def _():
            r = lax.broadcasted_iota(jnp.int32, (1024, 128), 0)
            col = lax.broadcasted_iota(jnp.int32, (1024, 128), 1)
            fold = jnp.where(col == (r // 512) * HD + r % HD, 1.0, 0.0).astype(bf16)
            dg_ref[...] = _dot_exact(jnp.broadcast_to(acc_ref[...], (8, 1024)), fold)

    half = pl.BlockSpec((bm, 512), lambda i: (i, 0))
    return pl.pallas_call(
        body, name="qknorm_bwd", grid=(nsteps,),
        in_specs=[half, half, half, half, pl.BlockSpec((bm, 1024), lambda i: (i, 0)),
                  pl.BlockSpec((1, 1024), lambda i: (0, 0))],
        out_specs=[pl.BlockSpec((bm, INW), lambda i: (i, 0)), pl.BlockSpec((8, 128), lambda i: (0, 0))],
        out_shape=[jax.ShapeDtypeStruct((L, INW), bf16), jax.ShapeDtypeStruct((8, 128), f32)],
        scratch_shapes=[pltpu.VMEM((1, 1024), f32)],
        compiler_params=_cp(("arbitrary",)),
    )(dqn, dkn, dval, du, p, gqk)


def _outnorm_fwd(oa, os_, ga, gs):
    L = oa.shape[0]
    bm = _row_block(L, 256)

    def body(a_ref, s_ref, ga_ref, gs_ref, o_ref):
        for x_ref, g_ref, off in ((a_ref, ga_ref, 0), (s_ref, gs_ref, 512)):
            xv = x_ref[...]
            inv = lax.rsqrt(jnp.mean(xv * xv, axis=-1, keepdims=True) + EPS)
            o_ref[:, off:off + 512] = (xv * inv * g_ref[...]).astype(o_ref.dtype)

    half = pl.BlockSpec((bm, 512), lambda i: (i, 0))
    vec = pl.BlockSpec((1, 512), lambda i: (0, 0))
    return pl.pallas_call(
        body, name="outnorm_fwd", grid=(L // bm,),
        in_specs=[half, half, vec, vec], out_specs=pl.BlockSpec((bm, D), lambda i: (i, 0)),
        out_shape=jax.ShapeDtypeStruct((L, D), bf16),
        compiler_params=_cp(("parallel",)),
    )(oa, os_, ga, gs)


def _outnorm_bwd(do, oa, os_, ga, gs):
    L = oa.shape[0]
    bm = _row_block(L, 256)

    def body(do_ref, a_ref, s_ref, ga_ref, gs_ref, da_ref, ds_ref, dga_ref, dgs_ref):
        i = pl.program_id(0)
        for x_ref, g_ref, dx_ref, dg_ref, off in ((a_ref, ga_ref, da_ref, dga_ref, 0),
                                                  (s_ref, gs_ref, ds_ref, dgs_ref, 512)):
            xv = x_ref[...]
            dv = do_ref[:, off:off + 512]
            inv = lax.rsqrt(jnp.mean(xv * xv, axis=-1, keepdims=True) + EPS)
            xh = xv * inv
            dxh = dv * g_ref[...]
            dx_ref[...] = inv * (dxh - xh * jnp.mean(dxh * xh, axis=-1, keepdims=True))
            part = jnp.sum(dv * xh, axis=0, keepdims=True)

            @pl.when(i == 0)
            def _():
                dg_ref[...] = part

            @pl.when(i > 0)
            def _():
                dg_ref[...] += part

    half = pl.BlockSpec((bm, 512), lambda i: (i, 0))
    vec = pl.BlockSpec((1, 512), lambda i: (0, 0))
    return pl.pallas_call(
        body, name="outnorm_bwd", grid=(L // bm,),
        in_specs=[pl.BlockSpec((bm, D), lambda i: (i, 0)), half, half, vec, vec],
        out_specs=[half, half, vec, vec],
        out_shape=[jax.ShapeDtypeStruct((L, 512), f32)] * 2 + [jax.ShapeDtypeStruct((1, 512), f32)] * 2,
        compiler_params=_cp(("arbitrary",)),
    )(do, oa, os_, ga, gs)


NCB = DFF // 128


def _conv_rows(L):
    return 64 if L % 64 == 0 else 32


def _conv_chunk(x_ref, w_ref, b_ref, i, nch, R, L, ahead):
    base = pl.multiple_of(i * R, R)
    prev = x_ref[pl.ds(pl.multiple_of(jnp.maximum(base - 8, 0), 8), 8), :]
    parts = [jnp.where(i > 0, prev, 0.0), x_ref[pl.ds(base, R), :]]
    if ahead:
        nxt = x_ref[pl.ds(pl.multiple_of(jnp.minimum(base + R, L - 8), 8), 8), :]
        parts.append(jnp.where(i < nch - 1, nxt, 0.0))
    xe = jnp.concatenate(parts, axis=0)
    x1 = pltpu.roll(xe, 1, 0)
    x2 = pltpu.roll(xe, 2, 0)
    u = b_ref[...] + x2 * w_ref[0:1, :]
    u = u + x1 * w_ref[1:2, :]
    u = u + xe * w_ref[2:3, :]
    return u[8:], xe[8:], x1[8:], x2[8:]


def _fold8(x):
    out = x[0:8]
    for q in range(1, x.shape[0] // 8):
        out = out + x[8 * q:8 * (q + 1)]
    return out


def _convglu_fwd(up_pre, cw, cb):
    L = up_pre.shape[0]

    def body(v_ref, g_ref, wv_ref, wg_ref, bv_ref, bg_ref, a_ref):
        row = lax.broadcasted_iota(jnp.int32, (L, 128), 0)

        def conv(x_ref, w_ref, b_ref):
            x = x_ref[...]
            out = b_ref[...] + jnp.where(row >= 2, pltpu.roll(x, 2, 0), 0.0) * w_ref[0:1, :]
            out = out + jnp.where(row >= 1, pltpu.roll(x, 1, 0), 0.0) * w_ref[1:2, :]
            return out + x * w_ref[2:3, :]

        val = conv(v_ref, wv_ref, bv_ref)
        a_ref[...] = (_gelu(conv(g_ref, wg_ref, bg_ref)) * val).astype(a_ref.dtype)

    col = lambda off: pl.BlockSpec((L, 128), lambda j: (0, j + off))
    wsp = lambda off: pl.BlockSpec((3, 128), lambda j: (0, j + off))
    bsp = lambda off: pl.BlockSpec((1, 128), lambda j: (0, j + off))
    return pl.pallas_call(
        body, name="convglu_fwd", grid=(NCB,),
        in_specs=[col(0), col(NCB), wsp(0), wsp(NCB), bsp(0), bsp(NCB)],
        out_specs=pl.BlockSpec((L, 128), lambda j: (0, j)),
        out_shape=jax.ShapeDtypeStruct((L, DFF), bf16),
        compiler_params=_cp(("parallel",)),
    )(up_pre, up_pre, cw, cw, cb, cb)


def _convglu_bwd(da, up_pre, cw, cb, comm=None):
    L = up_pre.shape[0]
    R = _conv_rows(L)
    nch = L // R
    n = R + 8

    def body(da_ref, v_ref, g_ref, wv_ref, wg_ref, bv_ref, bg_ref, dx_ref, dw_ref, db_ref):
        def chunk(i, acc):
            base = pl.multiple_of(i * R, R)
            val, xv, xv1, xv2 = _conv_chunk(v_ref, wv_ref, bv_ref, i, nch, R, L, True)
            gate, xg, xg1, xg2 = _conv_chunk(g_ref, wg_ref, bg_ref, i, nch, R, L, True)
            nxt = da_ref[pl.ds(pl.multiple_of(jnp.minimum(base + R, L - 8), 8), 8), :]
            dav = jnp.concatenate([da_ref[pl.ds(base, R), :], jnp.where(i < nch - 1, nxt, 0.0)], axis=0)
            c = math.sqrt(2.0 / math.pi)
            t = jnp.tanh(c * (gate + 0.044715 * (gate * gate * gate)))
            d_val = dav * (0.5 * gate * (1.0 + t))
            d_gate = dav * val * (0.5 * (1.0 + t)
                                  + 0.5 * gate * (1.0 - t * t) * c * (1.0 + 3.0 * 0.044715 * (gate * gate)))
            new = []
            for k, (d, w_ref, xs) in enumerate(((d_val, wv_ref, (xv2, xv1, xv)), (d_gate, wg_ref, (xg2, xg1, xg)))):
                dx = d * w_ref[2:3, :] + pltpu.roll(d, n - 1, 0) * w_ref[1:2, :]
                dx = dx + pltpu.roll(d, n - 2, 0) * w_ref[0:1, :]
                dx_ref[k, pl.ds(base, R), :] = dx[:R].astype(dx_ref.dtype)
                dr = d[:R]
                new += [_fold8(dr * x[:R]) for x in xs] + [_fold8(dr)]
            return tuple(a + b for a, b in zip(acc, new))

        acc = lax.fori_loop(0, nch, chunk, (jnp.zeros((8, 128), f32),) * 8)
        for k in range(2):
            for r in range(3):
                dw_ref[k, r:r + 1, :] = jnp.sum(acc[4 * k + r], axis=0, keepdims=True)
            db_ref[k] = jnp.sum(acc[4 * k + 3], axis=0, keepdims=True)

    col = lambda off: pl.BlockSpec((L, 128), lambda j: (0, j + off))
    wsp = lambda off: pl.BlockSpec((3, 128), lambda j: (0, j + off))
    bsp = lambda off: pl.BlockSpec((1, 128), lambda j: (0, j + off))
    (dx, dw, db), couts = _call(
        body, comm, name="convglu_bwd", grid=(NCB,),
        in_specs=[col(0), col(0), col(NCB), wsp(0), wsp(NCB), bsp(0), bsp(NCB)],
        out_specs=[pl.BlockSpec((2, L, 128), lambda j: (0, 0, j)), pl.BlockSpec((2, 3, 128), lambda j: (0, 0, j)),
                   pl.BlockSpec((2, 1, 128), lambda j: (0, 0, j))],
        out_shape=[jax.ShapeDtypeStruct((2, L, DFF), bf16), jax.ShapeDtypeStruct((2, 3, DFF), f32),
                   jax.ShapeDtypeStruct((2, 1, DFF), f32)], scratch_shapes=[],
        sem=("parallel",), args=(da, up_pre, up_pre, cw, cw, cb, cb))
    return dx, dw.transpose(1, 0, 2).reshape(3, DUP), db.transpose(1, 0, 2).reshape(1, DUP), couts


def _attn_block(L):
    return 256 if L % 256 == 0 else 128


TAIL_DEAD = -110.0


def _dot_exact2(x, m):
    hi = x.astype(bf16)
    lo = (x - hi.astype(f32)).astype(bf16)
    return (lax.dot_general(hi, m, NN, preferred_element_type=f32)
            + lax.dot_general(lo, m, NN, preferred_element_type=f32))


def _sb_weights(z, mask, tri_gt, carry):
    l1p = jnp.log(1.0 + jnp.exp(-jnp.abs(z)))
    ls_pos = jnp.minimum(z, 0.0) - l1p
    lm = ls_pos - z
    if mask is not None:
        lm = jnp.where(mask, lm, 0.0)
    tail = _dot_exact2(lm, tri_gt) + carry
    w = jnp.exp(ls_pos + tail)
    if mask is not None:
        w = jnp.where(mask, w, 0.0)
    return w, ls_pos, lm


def _head_masks(B):
    lane = lax.broadcasted_iota(jnp.int32, (B, 128), 1)
    return lane < HD, lane >= HD


def _attn_fwd(qkv, comm=None):
    L = qkv.shape[0]
    B = _attn_block(L)
    nq = L // B

    def body(q_ref, k_ref, v_ref, o_ref):
        qi = pl.program_id(1)
        heads = _head_masks(B)
        ti = lax.broadcasted_iota(jnp.int32, (B, B), 0)
        si = lax.broadcasted_iota(jnp.int32, (B, B), 1)
        tri_gt = jnp.where(ti > si, 1.0, 0.0).astype(bf16)
        diag = si < ti
        qv = q_ref[...]
        zero16 = jnp.zeros((), bf16)

        def tile(jb, mask, acc, carries):
            ks = pl.multiple_of(jb * B, B)
            kb = k_ref[pl.ds(ks, B), :]
            vb = v_ref[pl.ds(ks, B), :]
            out = []
            for in_head, carry in zip(heads, carries):
                kh = jnp.where(in_head, kb, zero16)
                vh = jnp.where(in_head, vb, zero16)
                z = lax.dot_general(qv, kh, NT, preferred_element_type=f32)
                w, _, lm = _sb_weights(z, mask, tri_gt, carry)
                acc = acc + lax.dot_general(w.astype(bf16), vh, NN, preferred_element_type=f32)
                out.append(carry + jnp.sum(lm, axis=1, keepdims=True))
            return acc, out

        zc = jnp.zeros((B, 1), f32)
        acc, (c0, c1) = tile(qi, diag, jnp.zeros((B, 128), f32), (zc, zc))
        acc, (c0, c1) = tile(jnp.maximum(qi - 1, 0), jnp.broadcast_to(qi > 0, (B, B)), acc, (c0, c1))

        def cond(st):
            return jnp.logical_and(st[0] <= qi, st[4] > TAIL_DEAD)

        def step(st):
            n, a, p0, p1, _ = st
            a, (p0, p1) = tile(qi - n, None, a, (p0, p1))
            return n + 1, a, p0, p1, jnp.maximum(jnp.max(p0), jnp.max(p1))

        st = lax.while_loop(cond, step, (jnp.int32(2), acc, c0, c1, jnp.maximum(jnp.max(c0), jnp.max(c1))))
        o_ref[...] = st[1]

    (o,), couts = _call(
        body, comm, name="attn_fwd", grid=(4, nq),
        in_specs=[pl.BlockSpec((B, 128), lambda hp, i: (i, hp)),
                  pl.BlockSpec((L, 128), lambda hp, i: (0, 4 + hp)),
                  pl.BlockSpec((L, 128), lambda hp, i: (0, 8 + hp))],
        out_specs=[pl.BlockSpec((B, 128), lambda hp, i: (i, hp))],
        out_shape=[jax.ShapeDtypeStruct((L, ATT), f32)], scratch_shapes=[],
        sem=("parallel", "parallel"), args=(qkv, qkv, qkv))
    return o, couts


def _attn_bwd(qkv, o, do, comm=None):
    L = qkv.shape[0]
    B = _attn_block(L)
    nq = L // B

    def body(q_ref, k_ref, v_ref, o_ref, do_ref, dq_ref, dk_ref, dv_ref):
        qi = pl.program_id(1)

        @pl.when(qi == 0)
        def _():
            dk_ref[...] = jnp.zeros_like(dk_ref)
            dv_ref[...] = jnp.zeros_like(dv_ref)

        heads = _head_masks(B)
        ti = lax.broadcasted_iota(jnp.int32, (B, B), 0)
        si = lax.broadcasted_iota(jnp.int32, (B, B), 1)
        tri_gt = jnp.where(ti > si, 1.0, 0.0).astype(bf16)
        tri_ge = jnp.where(ti >= si, 1.0, 0.0).astype(bf16)
        diag = si < ti
        zero16 = jnp.zeros((), bf16)
        qv = q_ref[...]
        do16 = do_ref[...].astype(bf16)
        dsum_lanes = do16.astype(f32) * o_ref[...]
        qhs = [jnp.where(m, qv, zero16) for m in heads]
        dohs = [jnp.where(m, do16, zero16) for m in heads]
        dsums = [jnp.sum(jnp.where(m, dsum_lanes, 0.0), axis=1, keepdims=True) for m in heads]

        def tile(jb, mask, dq_a, state):
            ks = pl.multiple_of(jb * B, B)
            kb = k_ref[pl.ds(ks, B), :]
            vb = v_ref[pl.ds(ks, B), :]
            dk_blk = jnp.zeros((B, 128), f32)
            dv_blk = jnp.zeros((B, 128), f32)
            out = []
            for in_head, qh, doh, dsum, (carry, suffix) in zip(heads, qhs, dohs, dsums, state):
                kh = jnp.where(in_head, kb, zero16)
                vh = jnp.where(in_head, vb, zero16)
                z = lax.dot_general(qv, kh, NT, preferred_element_type=f32)
                w, ls_pos, lm = _sb_weights(z, mask, tri_gt, carry)
                w16 = w.astype(bf16)
                dw = lax.dot_general(doh, vh, NT, preferred_element_type=f32)
                da = w16.astype(f32) * dw
                sig = jnp.exp(ls_pos)
                suf = _dot_exact2(da, tri_ge) + suffix
                dz = da * (1.0 - sig) - sig * (dsum - suf)
                if mask is not None:
                    dz = jnp.where(mask, dz, 0.0)
                dz16 = dz.astype(bf16)
                dq_a = dq_a + lax.dot_general(dz16, kh, NN, preferred_element_type=f32)
                dk_blk = dk_blk + lax.dot_general(dz16, qh, TN, preferred_element_type=f32)
                dv_blk = dv_blk + lax.dot_general(w16, doh, TN, preferred_element_type=f32)
                out.append((carry + jnp.sum(lm, axis=1, keepdims=True),
                            suffix + jnp.sum(da, axis=1, keepdims=True)))
            dk_ref[pl.ds(ks, B), :] += dk_blk
            dv_ref[pl.ds(ks, B), :] += dv_blk
            return dq_a, out

        def alive(state):
            return jnp.maximum(jnp.max(state[0][0]), jnp.max(state[1][0]))

        zc = jnp.zeros((B, 1), f32)
        dq_acc, state = tile(qi, diag, jnp.zeros((B, 128), f32), ((zc, zc), (zc, zc)))
        dq_acc, state = tile(jnp.maximum(qi - 1, 0), jnp.broadcast_to(qi > 0, (B, B)), dq_acc, state)

        def cond(st):
            return jnp.logical_and(st[0] <= qi, st[6] > TAIL_DEAD)

        def step(st):
            n, a, c0, s0, c1, s1, _ = st
            a, new = tile(qi - n, None, a, ((c0, s0), (c1, s1)))
            return n + 1, a, new[0][0], new[0][1], new[1][0], new[1][1], alive(new)

        st = lax.while_loop(cond, step, (jnp.int32(2), dq_acc, state[0][0], state[0][1],
                                         state[1][0], state[1][1], alive(state)))
        dq_ref[...] = st[1] * (1.0 / math.sqrt(HD))

    blk = pl.BlockSpec((B, 128), lambda hp, i: (i, hp))
    return _call(
        body, comm, name="attn_bwd", grid=(4, nq),
        in_specs=[blk,
                  pl.BlockSpec((L, 128), lambda hp, i: (0, 4 + hp)),
                  pl.BlockSpec((L, 128), lambda hp, i: (0, 8 + hp)),
                  blk, blk],
        out_specs=[blk, pl.BlockSpec((L, 128), lambda hp, i: (0, hp)),
                   pl.BlockSpec((L, 128), lambda hp, i: (0, hp))],
        out_shape=[jax.ShapeDtypeStruct((L, ATT), f32)] * 3, scratch_shapes=[],
        sem=("parallel", "arbitrary"), args=(qkv, qkv, qkv, o, do))


def _s5_disc(ldt, ar, ai, br, bi):
    dt = jnp.exp(ldt)
    mag = jnp.exp(dt * ar)
    abar_re = mag * jnp.cos(dt * ai)
    abar_im = mag * jnp.sin(dt * ai)
    em_re = abar_re - 1.0
    em_im = abar_im
    den = ar * ar + ai * ai
    f_re = (em_re * ar + em_im * ai) / den
    f_im = (em_im * ar - em_re * ai) / den
    bb_re = f_re * br - f_im * bi
    bb_im = f_re * bi + f_im * br
    return abar_re, abar_im, bb_re, bb_im


def _s5_disc_fwd(ldt, ar, ai, br, bi):
    def body(ldt_ref, ar_ref, ai_ref, br_ref, bi_ref, o1, o2, o3, o4):
        outs = _s5_disc(ldt_ref[...], ar_ref[...], ai_ref[...], br_ref[...], bi_ref[...])
        for o_ref, v in zip((o1, o2, o3, o4), outs):
            o_ref[...] = v

    col = jax.ShapeDtypeStruct((NS, 1), f32)
    mat = jax.ShapeDtypeStruct((NS, NH16), f32)
    return pl.pallas_call(body, name="s5_disc_fwd", out_shape=[col, col, mat, mat],
                          compiler_params=_cp())(ldt, ar, ai, br, bi)


def _s5_disc_bwd(ldt, ar, ai, br, bi, d_are, d_aim, d_bbre, d_bbim):
    def body(ldt_ref, ar_ref, ai_ref, br_ref, bi_ref, c1, c2, c3, c4, o_ldt, o_ar, o_ai, o_br, o_bi):
        prim = (ldt_ref[...], ar_ref[...], ai_ref[...], br_ref[...], bi_ref[...])
        _, vjp = jax.vjp(_s5_disc, *prim)
        g_ldt, g_ar, g_ai, g_br, g_bi = vjp((c1[...], c2[...], c3[...], c4[...]))
        o_ar[...] = g_ar
        o_ai[...] = g_ai
        o_br[...] = g_br
        o_bi[...] = g_bi
        r = lax.broadcasted_iota(jnp.int32, (NG, NS), 0)
        c = lax.broadcasted_iota(jnp.int32, (NG, NS), 1)
        fold = jnp.where(c // NP == r, 1.0, 0.0).astype(bf16)
        hi = jnp.broadcast_to(g_ldt, (NS, 128))
        p1 = hi.astype(bf16)
        r1 = hi - p1.astype(f32)
        p2 = r1.astype(bf16)
        p3 = (r1 - p2.astype(f32)).astype(bf16)
        dd = lambda p: lax.dot_general(fold, p, NN, preferred_element_type=f32)
        o_ldt[...] = dd(p1) + dd(p2) + dd(p3)

    col = jax.ShapeDtypeStruct((NS, 1), f32)
    mat = jax.ShapeDtypeStruct((NS, NH16), f32)
    return pl.pallas_call(
        body, name="s5_disc_bwd",
        out_shape=[jax.ShapeDtypeStruct((NG, 128), f32), col, col, mat, mat],
        compiler_params=_cp())(ldt, ar, ai, br, bi, d_are, d_aim, d_bbre, d_bbim)


def _cmul(ar, ai, br, bi):
    return ar * br - ai * bi, ar * bi + ai * br


def _s5_chunk(L):
    return 256 if L % 256 == 0 else L


def _diag_fold(acc, rpg):
    aw = acc.shape[0]
    r = lax.broadcasted_iota(jnp.int32, (aw, 128), 0) // rpg
    c = lax.broadcasted_iota(jnp.int32, (aw, 128), 1) // NH16
    fr = lax.broadcasted_iota(jnp.int32, (128, NH16), 0) % NH16
    fc = lax.broadcasted_iota(jnp.int32, (128, NH16), 1)
    fold = jnp.where(fr == fc, 1.0, 0.0).astype(bf16)
    return _dot_exact(jnp.where(r == c, acc, 0.0), fold)


def _diag_fold_t(acc):
    r = lax.broadcasted_iota(jnp.int32, (128, 8 * NP), 0) // NH16
    c = lax.broadcasted_iota(jnp.int32, (128, 8 * NP), 1) // NP
    fr = lax.broadcasted_iota(jnp.int32, (128, NH16), 0) % NH16
    fc = lax.broadcasted_iota(jnp.int32, (128, NH16), 1)
    fold = jnp.where(fr == fc, 1.0, 0.0).astype(bf16)
    x = jnp.where(r == c, acc, 0.0)
    hi = x.astype(bf16)
    r1 = x - hi.astype(f32)
    mid = r1.astype(bf16)
    lo = (r1 - mid.astype(f32)).astype(bf16)
    d = lambda piece: lax.dot_general(piece, fold, TN, preferred_element_type=f32)
    return d(hi) + d(mid) + d(lo)


NTILE = NS // 128
TILE_GROUP = 4


def _seg_entry(lr, li, ar, ai, cr, ci, seg_rows, reverse, row):
    pr, pi = ar, ai
    for _ in range(seg_rows.bit_length() - 1):
        pr, pi = _cmul(pr, pi, pr, pi)
    if reverse:
        xr = jnp.where(row == 7, cr, pltpu.roll(lr, 7, 0))
        xi = jnp.where(row == 7, ci, pltpu.roll(li, 7, 0))
    else:
        xr = jnp.where(row == 0, cr, pltpu.roll(lr, 1, 0))
        xi = jnp.where(row == 0, ci, pltpu.roll(li, 1, 0))
    for d in (1, 2, 4):
        keep = (row < 8 - d) if reverse else (row >= d)
        shift = 8 - d if reverse else d
        mr, mi = _cmul(pr, pi, jnp.where(keep, pltpu.roll(xr, shift, 0), 0.0),
                       jnp.where(keep, pltpu.roll(xi, shift, 0), 0.0))
        xr, xi = xr + mr, xi + mi
        pr, pi = _cmul(pr, pi, pr, pi)
    return xr, xi


def _seg_scan(xr3, xi3, abar_re, abar_im, cr_ref, ci_ref, reverse, on_step=None):
    pitch = xr3.shape[1] // 8
    seg = pitch - 8
    row = lax.broadcasted_iota(jnp.int32, (8, 128), 0)
    for g0 in range(0, NTILE, TILE_GROUP):
        tiles = list(range(g0, g0 + TILE_GROUP))
        lanes = [slice(128 * j, 128 * (j + 1)) for j in tiles]
        ar = [jnp.broadcast_to(abar_re[:, ls], (8, 128)) for ls in lanes]
        ai = [jnp.broadcast_to(-abar_im[:, ls] if reverse else abar_im[:, ls], (8, 128)) for ls in lanes]

        def sweep(store):
            def step(n, st):
                k = (seg - 1 - n) if reverse else n
                rows = pl.ds(k, 8, stride=pitch)
                new = []
                for q, j in enumerate(tiles):
                    sr, si = st[2 * q], st[2 * q + 1]
                    if store and on_step is not None:
                        on_step(j, rows, sr, si)
                    nr = ar[q] * sr - ai[q] * si + xr3.at[j][rows, :]
                    ni = ar[q] * si + ai[q] * sr + xi3.at[j][rows, :]
                    if store:
                        xr3.at[j][rows, :] = nr
                        xi3.at[j][rows, :] = ni
                    new += [nr, ni]
                return tuple(new)
            return step

        zero = jnp.zeros((8, 128), f32)
        ends = lax.fori_loop(0, seg, sweep(False), (zero,) * (2 * TILE_GROUP))
        init = []
        for q, ls in enumerate(lanes):
            init += _seg_entry(ends[2 * q], ends[2 * q + 1], ar[q], ai[q], cr_ref[:, ls], ci_ref[:, ls],
                               seg, reverse, row)
        fin = lax.fori_loop(0, seg, sweep(True), tuple(init))
        edge = slice(0, 1) if reverse else slice(7, 8)
        for q, ls in enumerate(lanes):
            cr_ref[:, ls] = fin[2 * q][edge, :]
            ci_ref[:, ls] = fin[2 * q + 1][edge, :]


def _coarse(i):
    return slice(128 * i, 128 * (i + 1)), slice(512 * i, 512 * (i + 1))


def _seg_pitch(T):
    return T // 8 + 8


def _put_rows(x3, j, val):
    pitch = x3.shape[1] // 8
    seg = pitch - 8
    for s in range(8):
        x3[j, pitch * s:pitch * s + seg, :] = val[seg * s:seg * (s + 1), :]


def _get_rows(x3, j):
    pitch = x3.shape[1] // 8
    seg = pitch - 8
    return jnp.concatenate([x3[j, pitch * s:pitch * s + seg, :] for s in range(8)], axis=0)


def _tiles_cat(x3, i):
    return jnp.concatenate([_get_rows(x3, 4 * i + q) for q in range(4)], axis=1)


def _s5_gate2(sr3, si3, u, ccr_ref, cci_ref, dsk_ref, gw_ref, gb_ref):
    ys = []
    for i in range(4):
        ch, st = _coarse(i)
        ys.append(_dot(_tiles_cat(sr3, i), ccr_ref[st, ch]) - _dot(_tiles_cat(si3, i), cci_ref[st, ch]))
    y = jnp.concatenate(ys, axis=1) + dsk_ref[...] * u
    yg = _gelu(y)
    gl = jnp.concatenate([_dot(yg[:, _coarse(i)[0]], gw_ref[_coarse(i)[0], _coarse(i)[0]]) for i in range(4)],
                         axis=1)
    return y, yg, _sigmoid(gl + gb_ref[...])


def _s5_fwd2(p, abar_re, abar_im, bbr, bbi, ccr, cci, dsk, gw, gb):
    L = p.shape[0]
    T = _s5_chunk(L)

    def body(u_ref, are_ref, aim_ref, bbr_ref, bbi_ref, ccr_ref, cci_ref, dsk_ref, gw_ref, gb_ref,
             o_ref, sr3, si3, cr_ref, ci_ref):
        @pl.when(pl.program_id(0) == 0)
        def _():
            cr_ref[...] = jnp.zeros_like(cr_ref)
            ci_ref[...] = jnp.zeros_like(ci_ref)

        u = u_ref[...]
        for i in range(4):
            ch, st = _coarse(i)
            br = _dot(u[:, ch], bbr_ref[ch, st])
            bi = _dot(u[:, ch], bbi_ref[ch, st])
            for q in range(4):
                _put_rows(sr3, 4 * i + q, br[:, 128 * q:128 * (q + 1)])
                _put_rows(si3, 4 * i + q, bi[:, 128 * q:128 * (q + 1)])
        _seg_scan(sr3, si3, are_ref[...], aim_ref[...], cr_ref, ci_ref, reverse=False)
        _, yg, gate = _s5_gate2(sr3, si3, u, ccr_ref, cci_ref, dsk_ref, gw_ref, gb_ref)
        o_ref[...] = yg * gate

    full = lambda shape: pl.BlockSpec(shape, lambda i: (0, 0))
    rows = 8 * _seg_pitch(T)
    st3 = pl.BlockSpec((NTILE, rows, 128), lambda i: (0, i, 0))
    return pl.pallas_call(
        body, name="s5_fwd", grid=(L // T,),
        in_specs=[pl.BlockSpec((T, SSMW), lambda i: (i, 3)), full((1, NS)), full((1, NS)),
                  full((SSMW, NS)), full((SSMW, NS)), full((NS, SSMW)), full((NS, SSMW)),
                  full((1, SSMW)), full((SSMW, SSMW)), full((1, SSMW))],
        out_specs=[pl.BlockSpec((T, SSMW), lambda i: (i, 0)), st3, st3],
        out_shape=[jax.ShapeDtypeStruct((L, SSMW), f32)]
                  + [jax.ShapeDtypeStruct((NTILE, (L // T) * rows, 128), f32)] * 2,
        scratch_shapes=[pltpu.VMEM((1, NS), f32), pltpu.VMEM((1, NS), f32)],
        compiler_params=_cp(("arbitrary",)),
    )(p, abar_re, abar_im, bbr, bbi, ccr, cci, dsk, gw, gb)


def _s5_bwd2(dout, p, s_re, s_im, abar_re, abar_im, bbr, bbi, ccr, cci, dsk, gw, gb):
    L = p.shape[0]
    T = _s5_chunk(L)
    nchunk = L // T

    def body(do_ref, u_ref, sr3, si3, are_ref, aim_ref, bbr_ref, bbi_ref, ccr_ref, cci_ref,
             dsk_ref, gw_ref, gb_ref,
             du_ref, dare_ref, daim_ref, dd_ref, dgb_ref, dccr_ref, dcci_ref, dbbr_ref, dbbi_ref, dgw_ref,
             lr3, li3, cr_ref, ci_ref, accr3, acci3, wacc_ref, gacc_ref):
        i = pl.program_id(0)

        @pl.when(i == 0)
        def _():
            for ref in (cr_ref, ci_ref, accr3, acci3, dd_ref, dgb_ref, wacc_ref, gacc_ref):
                ref[...] = jnp.zeros_like(ref)

        u = u_ref[...]
        dov = do_ref[...]
        y, yg, gate = _s5_gate2(sr3, si3, u, ccr_ref, cci_ref, dsk_ref, gw_ref, gb_ref)
        dgl = dov * yg * gate * (1.0 - gate)
        dyg = dov * gate + jnp.concatenate(
            [_dot(dgl[:, _coarse(q)[0]], gw_ref[_coarse(q)[0], _coarse(q)[0]], NT) for q in range(4)], axis=1)
        dy = dyg * _gelu_grad(y)
        u16, dy16, yg16, dgl16 = (t.astype(_MXU) for t in (u, dy, yg, dgl))
        dd_ref[...] += jnp.sum(dy * u, axis=0, keepdims=True)
        dgb_ref[...] += jnp.sum(dgl, axis=0, keepdims=True)
        for q in range(4):
            ch, st = _coarse(q)
            wacc_ref[0, q] += _dot(dy16[:, ch], _tiles_cat(sr3, q), TN)
            wacc_ref[1, q] += _dot(dy16[:, ch], _tiles_cat(si3, q), TN)
            gacc_ref[q] += _dot(yg16[:, ch], dgl16[:, ch], TN)
            gr = _dot(dy16[:, ch], ccr_ref[st, ch], NT)
            gi = -_dot(dy16[:, ch], cci_ref[st, ch], NT)
            for t in range(4):
                _put_rows(lr3, 4 * q + t, gr[:, 128 * t:128 * (t + 1)])
                _put_rows(li3, 4 * q + t, gi[:, 128 * t:128 * (t + 1)])

        def on_step(j, rows, nxt_r, nxt_i):
            s_r = sr3.at[j][rows, :]
            s_i = si3.at[j][rows, :]
            accr3[j] += nxt_r * s_r + nxt_i * s_i
            acci3[j] += nxt_i * s_r - nxt_r * s_i

        _seg_scan(lr3, li3, are_ref[...], aim_ref[...], cr_ref, ci_ref, reverse=True, on_step=on_step)
        dus = []
        for q in range(4):
            ch, st = _coarse(q)
            lam_r, lam_i = _tiles_cat(lr3, q), _tiles_cat(li3, q)
            wacc_ref[2, q] += _dot(u16[:, ch], lam_r, TN)
            wacc_ref[3, q] += _dot(u16[:, ch], lam_i, TN)
            dus.append(_dot(lam_r, bbr_ref[ch, st], NT) + _dot(lam_i, bbi_ref[ch, st], NT))
        du_ref[...] = dy * dsk_ref[...] + jnp.concatenate(dus, axis=1)

        @pl.when(i == nchunk - 1)
        def _():
            for j in range(NTILE):
                ls = slice(128 * j, 128 * (j + 1))
                dare_ref[:, ls] = jnp.sum(accr3[j], axis=0, keepdims=True)
                daim_ref[:, ls] = jnp.sum(acci3[j], axis=0, keepdims=True)
            for m, o_ref in enumerate((dccr_ref, dcci_ref, dbbr_ref, dbbi_ref)):
                for q in range(4):
                    o_ref[512 * q:512 * (q + 1), :] = _diag_fold_t(wacc_ref[m, q])
            for q in range(4):
                dgw_ref[128 * q:128 * (q + 1), :] = _diag_fold(gacc_ref[q], NH16)

    rev = lambda i: (nchunk - 1 - i, 0)
    full = lambda shape: pl.BlockSpec(shape, lambda i: (0, 0))
    half = pl.BlockSpec((T, SSMW), rev)
    rows = 8 * _seg_pitch(T)
    st3 = pl.BlockSpec((NTILE, rows, 128), lambda i: (0, nchunk - 1 - i, 0))
    return pl.pallas_call(
        body, name="s5_bwd", grid=(nchunk,),
        in_specs=[half, pl.BlockSpec((T, SSMW), lambda i: (nchunk - 1 - i, 3)), st3, st3,
                  full((1, NS)), full((1, NS)), full((SSMW, NS)), full((SSMW, NS)),
                  full((NS, SSMW)), full((NS, SSMW)), full((1, SSMW)), full((SSMW, SSMW)), full((1, SSMW))],
        out_specs=[half, full((1, NS)), full((1, NS)), full((1, SSMW)), full((1, SSMW)),
                   full((NS, NH16)), full((NS, NH16)), full((NS, NH16)), full((NS, NH16)),
                   full((SSMW, NH16))],
        out_shape=[jax.ShapeDtypeStruct((L, SSMW), f32)] + [jax.ShapeDtypeStruct((1, NS), f32)] * 2
                  + [jax.ShapeDtypeStruct((1, SSMW), f32)] * 2 + [jax.ShapeDtypeStruct((NS, NH16), f32)] * 4
                  + [jax.ShapeDtypeStruct((SSMW, NH16), f32)],
        scratch_shapes=[pltpu.VMEM((NTILE, rows, 128), f32), pltpu.VMEM((NTILE, rows, 128), f32),
                        pltpu.VMEM((1, NS), f32), pltpu.VMEM((1, NS), f32),
                        pltpu.VMEM((NTILE, 8, 128), f32), pltpu.VMEM((NTILE, 8, 128), f32),
                        pltpu.VMEM((4, 4, 128, 512), f32), pltpu.VMEM((4, 128, 128), f32)],
        compiler_params=_cp(("arbitrary",)),
    )(dout, p, s_re, s_im, abar_re, abar_im, bbr, bbi, ccr, cci, dsk, gw, gb)


def _block_diag(x):
    g, r, c = x.shape
    eye = jnp.eye(g, dtype=x.dtype)
    return (x[:, :, None, :] * eye[:, None, :, None]).reshape(g * r, g * c)


def _mod_fwd(c16, ada_w, ada_b_loc):
    nloc = ada_w.shape[2]

    def body(c_ref, w_ref, b_ref, o_ref):
        cv = c_ref[...]
        act = cv * _sigmoid(cv)
        o_ref[...] = _dot(act, w_ref[...]) + b_ref[...]

    return pl.pallas_call(
        body, name="mod_fwd", grid=(DEPTH,),
        in_specs=[pl.BlockSpec((16, D), lambda l: (0, 0)), pl.BlockSpec((None, D, nloc), lambda l: (l, 0, 0)),
                  pl.BlockSpec((None, 1, nloc), lambda l: (l, 0, 0))],
        out_specs=pl.BlockSpec((None, 16, nloc), lambda l: (l, 0, 0)),
        out_shape=jax.ShapeDtypeStruct((DEPTH, 16, nloc), f32),
        compiler_params=_cp(("parallel",)),
    )(c16, ada_w, ada_b_loc)


def _ada_grad(c16, dmod16):
    nloc = dmod16.shape[2]

    def body(c_ref, d_ref, o_ref):
        cv = c_ref[...]
        act = cv * _sigmoid(cv)
        o_ref[...] = _dot(act, d_ref[...], TN)

    return pl.pallas_call(
        body, name="ada_grad", grid=(DEPTH,),
        in_specs=[pl.BlockSpec((16, D), lambda l: (0, 0)), pl.BlockSpec((None, 16, nloc), lambda l: (l, 0, 0))],
        out_specs=pl.BlockSpec((None, D, nloc), lambda l: (l, 0, 0)),
        out_shape=jax.ShapeDtypeStruct((DEPTH, D, nloc), f32),
        compiler_params=_cp(("parallel",)),
    )(c16, dmod16)


def _as2d(a):
    return a.reshape(-1, a.shape[-1])


def _ew_rows(rows):
    for cand in (512, 256, 128, 64, 32, 16, 8):
        if rows % cand == 0:
            return cand
    return rows


def _cast_bf16(w):
    w2 = _as2d(w)
    rows, cols = w2.shape
    bm = _ew_rows(rows)

    def body(x_ref, o_ref):
        o_ref[...] = x_ref[...].astype(bf16)

    spec = pl.BlockSpec((bm, cols), lambda i: (i, 0))
    out = pl.pallas_call(body, name="cast_bf16", grid=(rows // bm,), in_specs=[spec], out_specs=spec,
                         out_shape=jax.ShapeDtypeStruct((rows, cols), bf16),
                         compiler_params=_cp(("parallel",)))(w2)
    return out.reshape(w.shape)


def _adamw(parts, w, m, v):
    n = parts.shape[0]
    w2, m2, v2 = _as2d(w), _as2d(m), _as2d(v)
    rows, cols = w2.shape
    p3 = parts.reshape(n, rows, cols)
    bm = _ew_rows(rows)
    if bm * cols * 4 > (1 << 20) and bm % 16 == 0:
        bm //= 2

    def body(p_ref, w_ref, m_ref, v_ref, g_ref, d_ref, nm_ref, nv_ref):
        g = p_ref[0]
        for k in range(1, n):
            g = g + p_ref[k]
        mn = ADAM_B1 * m_ref[...] + (1.0 - ADAM_B1) * g
        vn = ADAM_B2 * v_ref[...] + (1.0 - ADAM_B2) * (g * g)
        m_hat = mn / (1.0 - ADAM_B1 ** ADAM_STEP)
        v_hat = vn / (1.0 - ADAM_B2 ** ADAM_STEP)
        g_ref[...] = g
        d_ref[...] = -ADAM_LR * (m_hat / (jnp.sqrt(v_hat) + ADAM_EPS) + ADAM_WD * w_ref[...])
        nm_ref[...] = mn
        nv_ref[...] = vn

    spec = pl.BlockSpec((bm, cols), lambda i: (i, 0))
    outs = pl.pallas_call(
        body, name="adamw", grid=(rows // bm,),
        in_specs=[pl.BlockSpec((n, bm, cols), lambda i: (0, i, 0)), spec, spec, spec],
        out_specs=[spec] * 4, out_shape=[jax.ShapeDtypeStruct((rows, cols), f32)] * 4,
        compiler_params=_cp(("parallel",)),
    )(p3, w2, m2, v2)
    return tuple(o.reshape(w.shape) for o in outs)


ANY = pl.BlockSpec(memory_space=pl.ANY)


def _my_pos():
    return lax.axis_index("x"), lax.axis_index("y"), lax.axis_index("c")


def _allgather8(name, x):
    def body(x_ref, out_ref, send_sems, recv_sems):
        mx, my, mc = _my_pos()
        me, sibling = (mx, my, mc), (mx, my, 1 - mc)
        chips = [(1 - mx, my), (mx, 1 - my), (1 - mx, 1 - my)]

        def slot(px, py, pc):
            return out_ref.at[4 * px + 2 * py + pc]

        def copy(k, block, to, src=None):
            return pltpu.make_async_remote_copy(
                src_ref=slot(*block) if src is None else src, dst_ref=slot(*block),
                send_sem=send_sems.at[k], recv_sem=recv_sems.at[k], device_id=to, device_id_type=MESH)

        first = [copy(0, me, sibling, src=x_ref)]
        first += [copy(1 + j, me, (*chip, mc), src=x_ref) for j, chip in enumerate(chips)]
        for cp in first:
            cp.start()
        passed = [copy(4 + j, (*chip, mc), sibling) for j, chip in enumerate(chips)]
        for j, chip in enumerate(chips):
            copy(1 + j, (*chip, mc), me).wait_recv()
            passed[j].start()
        copy(0, sibling, me).wait_recv()
        for j, chip in enumerate(chips):
            copy(4 + j, (*chip, 1 - mc), me).wait_recv()
        for cp in first + passed:
            cp.wait_send()

    out = pl.pallas_call(
        body, name=name, in_specs=[ANY], out_specs=ANY,
        out_shape=jax.ShapeDtypeStruct((NDEV,) + x.shape, x.dtype),
        scratch_shapes=[pltpu.SemaphoreType.DMA((7,)), pltpu.SemaphoreType.DMA((7,))],
    )(x)
    dev = 4 * lax.axis_index("x") + 2 * lax.axis_index("y") + lax.axis_index("c")
    return lax.dynamic_update_index_in_dim(out, x, dev, 0)


def _sum_parts(parts):
    n = parts.shape[0]
    p3 = parts.reshape(n, -1, parts.shape[-1])
    rows, cols = p3.shape[1:]
    bm = _ew_rows(rows)
    if bm * cols * 4 > (1 << 20) and bm % 16 == 0:
        bm //= 2

    def body(p_ref, o_ref):
        g = p_ref[0].astype(f32)
        for k in range(1, n):
            g = g + p_ref[k].astype(f32)
        o_ref[...] = g

    out = pl.pallas_call(
        body, name="sum_parts", grid=(rows // bm,),
        in_specs=[pl.BlockSpec((n, bm, cols), lambda i: (0, i, 0))],
        out_specs=pl.BlockSpec((bm, cols), lambda i: (i, 0)),
        out_shape=jax.ShapeDtypeStruct((rows, cols), f32),
        compiler_params=_cp(("parallel",)),
    )(p3)
    return out.reshape(parts.shape[1:])


def _rc(src, dst, ss, rs, k, to):
    return pltpu.make_async_remote_copy(src_ref=src, dst_ref=dst, send_sem=ss.at[k], recv_sem=rs.at[k],
                                        device_id=to, device_id_type=MESH)


def _half_rows(rows, c):
    return pl.ds((rows // 2) * c, rows // 2)


def _gather_comm(shards, l):
    n = len(shards)

    def plan(ins, outs, ss, rs):
        mx, my, mc = _my_pos()
        chips = [(1 - mx, my), (mx, 1 - my), (1 - mx, 1 - my)]
        mine = 2 * mx + my
        sibling = (mx, my, 1 - mc)
        sends, lands, fwds, fwd_lands, own = [], [], [], [], []
        for a in range(n):
            src = ins[a].at[l]
            rows = ins[a].shape[1]
            mr, sr = _half_rows(rows, mc), _half_rows(rows, 1 - mc)
            for j, (px, py) in enumerate(chips):
                k, src_chip = 3 * a + j, 2 * px + py
                sends.append(_rc(src.at[mr], outs[a].at[mine, mr], ss, rs, k, (px, py, mc)))
                lands.append(_rc(src.at[mr], outs[a].at[src_chip, mr], ss, rs, k, (px, py, mc)))
                blk, sblk = outs[a].at[src_chip, mr], outs[a].at[src_chip, sr]
                fwds.append(_rc(blk, blk, ss, rs, 3 * n + k, sibling))
                fwd_lands.append(_rc(sblk, sblk, ss, rs, 3 * n + k, sibling))
            own.append(_rc(src, outs[a].at[mine], ss, rs, 6 * n + a, sibling))
        return sends, lands, fwds, fwd_lands, own

    def start(ins, outs, ss, rs):
        sends, _, _, _, own = plan(ins, outs, ss, rs)
        for cp in sends + own:
            cp.start()

    def finish(ins, outs, ss, rs):
        sends, lands, fwds, fwd_lands, own = plan(ins, outs, ss, rs)
        for land, fwd in zip(lands, fwds):
            land.wait_recv()
            fwd.start()
        for cp in fwd_lands:
            cp.wait_recv()
        for cp in sends + fwds:
            cp.wait_send()
        for cp in own:
            cp.wait()

    return dict(ins=list(shards), nsem=7 * n, start=start, finish=finish,
                out_shapes=[jax.ShapeDtypeStruct((NCHIP,) + a.shape[1:], a.dtype) for a in shards])


def _swap_comm(grads):
    n = len(grads)

    def plan(ins, outs, ss, rs):
        mx, my, mc = _my_pos()
        return [_rc(ins[a].at[j, _half_rows(ins[a].shape[1], 1 - mc)], outs[a].at[j], ss, rs,
                    NCHIP * a + j, (mx, my, 1 - mc)) for a in range(n) for j in range(NCHIP)]

    def start(ins, outs, ss, rs):
        for cp in plan(ins, outs, ss, rs):
            cp.start()

    def finish(ins, outs, ss, rs):
        for cp in plan(ins, outs, ss, rs):
            cp.wait()

    return dict(ins=list(grads), nsem=NCHIP * n, start=start, finish=finish,
                out_shapes=[jax.ShapeDtypeStruct((NCHIP, a.shape[1] // 2) + a.shape[2:], a.dtype)
                            for a in grads])


def _scatter_comm(sums):
    n = len(sums)

    def plan(ins, outs, ss, rs):
        mx, my, mc = _my_pos()
        chips = [(1 - mx, my), (mx, 1 - my), (1 - mx, 1 - my)]
        mine = 2 * mx + my
        sends, lands = [], []
        for a in range(n):
            for j, (px, py) in enumerate(chips):
                sends.append(_rc(ins[a].at[2 * px + py], outs[a].at[mine], ss, rs, 3 * a + j, (px, py, mc)))
                lands.append(_rc(ins[a].at[mine], outs[a].at[2 * px + py], ss, rs, 3 * a + j, (px, py, mc)))
        return sends, lands

    def start(ins, outs, ss, rs):
        for cp in plan(ins, outs, ss, rs)[0]:
            cp.start()

    def finish(ins, outs, ss, rs):
        sends, lands = plan(ins, outs, ss, rs)
        for cp in lands:
            cp.wait_recv()
        for cp in sends:
            cp.wait_send()

    return dict(ins=list(sums), nsem=3 * n, start=start, finish=finish,
                out_shapes=[jax.ShapeDtypeStruct(a.shape, a.dtype) for a in sums])


def _scatter_done(outs, sums):
    mine = 2 * lax.axis_index("x") + lax.axis_index("y")
    return [lax.dynamic_update_index_in_dim(o, lax.dynamic_index_in_dim(a, mine, 0, keepdims=False), mine, 0)
            for o, a in zip(outs, sums)]


def _merge_comm(halves):
    n = len(halves)

    def plan(ins, outs, ss, rs):
        mx, my, mc = _my_pos()
        return [_rc(ins[a], outs[a].at[_half_rows(outs[a].shape[0], mc)], ss, rs, a, (mx, my, 1 - mc))
                for a in range(n)]

    def start(ins, outs, ss, rs):
        for cp in plan(ins, outs, ss, rs):
            cp.start()

    def finish(ins, outs, ss, rs):
        for cp in plan(ins, outs, ss, rs):
            cp.wait()

    return dict(ins=list(halves), nsem=n, start=start, finish=finish,
                out_shapes=[jax.ShapeDtypeStruct((2 * a.shape[0],) + a.shape[1:], a.dtype) for a in halves])


def _merge_done(outs, halves):
    mc = lax.axis_index("c")
    return [lax.dynamic_update_slice_in_dim(o, a, a.shape[0] * mc, 0) for o, a in zip(outs, halves)]


def _comm_call(name, comm):
    ni, no = len(comm["ins"]), len(comm["out_shapes"])

    def body(*refs):
        ins, outs, (ss, rs) = refs[:ni], refs[ni:ni + no], refs[ni + no:]
        comm["start"](ins, outs, ss, rs)
        comm["finish"](ins, outs, ss, rs)

    return pl.pallas_call(
        body, name=name, in_specs=[ANY] * ni, out_specs=[ANY] * no, out_shape=comm["out_shapes"],
        scratch_shapes=[pltpu.SemaphoreType.DMA((comm["nsem"],)), pltpu.SemaphoreType.DMA((comm["nsem"],))],
    )(*comm["ins"])


def _call(body, comm, *, name, grid, in_specs, out_specs, out_shape, scratch_shapes, sem, args):
    if comm is None:
        outs = pl.pallas_call(body, name=name, grid=grid, in_specs=in_specs, out_specs=out_specs,
                              out_shape=out_shape, scratch_shapes=scratch_shapes,
                              compiler_params=_cp(sem))(*args)
        return outs, None
    ni, no = len(in_specs), len(out_specs)
    ci, co = len(comm["ins"]), len(comm["out_shapes"])
    nscr = len(scratch_shapes)

    def carried(*refs):
        ins, cins = refs[:ni], refs[ni:ni + ci]
        outs, couts = refs[ni + ci:ni + ci + no], refs[ni + ci + no:ni + ci + no + co]
        scr = refs[ni + ci + no + co:]
        ss, rs = scr[nscr], scr[nscr + 1]
        pids = [pl.program_id(ax) for ax in range(len(grid))]
        first = functools.reduce(jnp.logical_and, [p == 0 for p in pids])
        last = functools.reduce(jnp.logical_and, [p == g - 1 for p, g in zip(pids, grid)])

        @pl.when(first)
        def _():
            comm["start"](cins, couts, ss, rs)

        body(*ins, *outs, *scr[:nscr])

        @pl.when(last)
        def _():
            comm["finish"](cins, couts, ss, rs)

    outs = pl.pallas_call(
        carried, name=name + "_x", grid=grid, in_specs=list(in_specs) + [ANY] * ci,
        out_specs=list(out_specs) + [ANY] * co, out_shape=list(out_shape) + comm["out_shapes"],
        scratch_shapes=list(scratch_shapes) + [pltpu.SemaphoreType.DMA((comm["nsem"],)),
                                               pltpu.SemaphoreType.DMA((comm["nsem"],))],
        compiler_params=_cp(("arbitrary",) * len(grid)),
    )(*args, *comm["ins"])
    return outs[:no], outs[no:]


def _layer_fwd(x, res, res_gate, wl, mod, comm=None):
    L = x.shape[0]
    bm = _row_block(L, 1024)
    nb = L // bm
    sh1, sc1, g1, sh2, sc2, g2 = [mod[k * D:(k + 1) * D][None, :] for k in range(NMOD)]
    sv = {}
    xin, h = _rmsmod_fwd(x, res, res_gate, wl["norm1_g"], sh1, sc1)
    if xin is None:
        xin = x
    p = _mm("mm_in", h, wl["w_in"], dims=NN, grid=(nb, 4, 1),
            a_block=(bm, D), a_map=lambda i, j, k: (i, 0),
            b_block=(None, D, 512), b_map=lambda i, j, k: (j, 0, 0),
            out_shape=(L, INW), o_block=(bm, 512), o_map=lambda i, j, k: (i, j))
    qkv = _qknorm_fwd(p, wl["gqk"])
    oa, couts = _attn_fwd(qkv, comm)
    os_, s_re, s_im = _s5_fwd2(p, wl["abar_re"], wl["abar_im"], wl["bbr"], wl["bbi"], wl["ccr"], wl["cci"],
                              wl["dsk"], wl["gw"], wl["gb"])
    o = _outnorm_fwd(oa, os_, wl["attn_out_g"], wl["ssm_out_g"])
    proj = _mm("mm_out", o, wl["w_out"], dims=NN, grid=(nb, 2, 1),
               a_block=(bm, D), a_map=lambda i, j, k: (i, 0),
               b_block=(D, 512), b_map=lambda i, j, k: (0, j),
               out_shape=(L, D), o_block=(bm, 512), o_map=lambda i, j, k: (i, j))
    x1, h2 = _rmsmod_fwd(xin, proj, g1, wl["norm2_g"], sh2, sc2)
    up_pre = _mm("mm_up", h2, wl["w_up"], dims=NN, grid=(nb, 4, 1),
                 a_block=(bm, D), a_map=lambda i, j, k: (i, 0),
                 b_block=(None, D, 1408), b_map=lambda i, j, k: (j, 0, 0),
                 out_shape=(L, DUP), o_block=(bm, 1408), o_map=lambda i, j, k: (i, j))
    a = _convglu_fwd(up_pre, wl["conv_w"], wl["conv_b"])
    down = _mm("mm_down", a, wl["w_down"], dims=NN, grid=(nb, 2, 1),
               a_block=(bm, DFF), a_map=lambda i, j, k: (i, 0),
               b_block=(DFF, 512), b_map=lambda i, j, k: (0, j),
               out_shape=(L, D), o_block=(bm, 512), o_map=lambda i, j, k: (i, j))
    sv.update(xin=xin, h=h, p=p, qkv=qkv, oa=oa, os=os_, s_re=s_re, s_im=s_im, o=o, proj=proj,
              x1=x1, h2=h2, up_pre=up_pre, a=a, down=down,
              sc1=sc1, g1=g1, sc2=sc2, g2=g2)
    return x1, down, g2, sv, couts


def _big_grads(g):
    return [g["w_in"], g["w_out"].reshape(NCHIP, D // NCHIP, D), g["w_up"],
            g["w_down"].reshape(NCHIP, DFF // NCHIP, D)]


def _chip_sums(grads, recv):
    core = lax.axis_index("c").astype(jnp.int32).reshape((1,))

    def half_sum(a, b):
        _, half, cols = b.shape
        bm = half if half * cols * 4 <= (3 << 19) else half // 2
        nrb = half // bm

        def body(core_ref, a_ref, b_ref, o_ref):
            o_ref[...] = (a_ref[...] + b_ref[...]).astype(o_ref.dtype)

        blk = pl.BlockSpec((None, bm, cols), lambda j, i, core_ref: (j, i, 0))
        return pl.pallas_call(
            body, name="chip_sum",
            grid_spec=pltpu.PrefetchScalarGridSpec(
                num_scalar_prefetch=1, grid=(NCHIP, nrb),
                in_specs=[pl.BlockSpec((None, bm, cols), lambda j, i, core_ref: (j, core_ref[0] * nrb + i, 0)),
                          blk],
                out_specs=blk),
            out_shape=jax.ShapeDtypeStruct(b.shape, bf16),
            compiler_params=_cp(("parallel", "parallel")),
        )(core, a, b)

    return [half_sum(a, b) for a, b in zip(grads, recv)]


def _reduced(parts, sums):
    halves = [_sum_parts(p) for p in _scatter_done(parts, sums)]
    return _merge_done(_comm_call("grad_merge", _merge_comm(halves)), halves)


def _layer_bwd(dx_out, wl, sv, prev=None):
    L = dx_out.shape[0]
    bm = _row_block(L, 1024)
    nb = L // bm
    bk = _row_block(L, 512)
    nk = L // bk
    g = {}
    dd, dg2 = _gate_bwd(dx_out, sv["down"], sv["g2"])
    da = _mm("mm_down_dx", dd, wl["w_down"], dims=NT, grid=(nb, 2, 1),
             a_block=(bm, D), a_map=lambda i, j, k: (i, 0),
             b_block=(1408, D), b_map=lambda i, j, k: (j, 0),
             out_shape=(L, DFF), o_block=(bm, 1408), o_map=lambda i, j, k: (i, j))
    g["w_down"] = _mm("mm_down_dw", sv["a"], dd, dims=TN, grid=(1, 2, nk),
                      a_block=(bk, DFF), a_map=lambda i, j, k: (k, 0),
                      b_block=(bk, 512), b_map=lambda i, j, k: (k, j),
                      out_shape=(DFF, D), o_block=(DFF, 512), o_map=lambda i, j, k: (0, j))
    dup, g["conv_w"], g["conv_b"], recv = _convglu_bwd(da, sv["up_pre"], wl["conv_w"], wl["conv_b"],
                                                       _swap_comm(prev) if prev else None)
    sums = _chip_sums(prev, recv) if prev else None
    dh2 = _mm("mm_up_dx", dup, wl["w_up"], dims=NT, grid=(nb, 1, 4),
              a_block=(None, bm, 1408), a_map=lambda i, j, k: (k // 2, i, k % 2),
              b_block=(None, D, 1408), b_map=lambda i, j, k: (k, 0, 0),
              out_shape=(L, D), o_block=(bm, D), o_map=lambda i, j, k: (i, 0))
    g["w_up"] = _mm("mm_up_dw", sv["h2"], dup, dims=TN, grid=(1, 4, nk),
                    a_block=(bk, D), a_map=lambda i, j, k: (k, 0),
                    b_block=(None, bk, 1408), b_map=lambda i, j, k: (j // 2, k, j % 2),
                    out_shape=(NCHIP, D, 1408), o_block=(None, D, 1408), o_map=lambda i, j, k: (j, 0, 0))
    dx1, dsh2, dsc2, g["norm2_g"] = _rmsmod_bwd(dh2, sv["x1"], wl["norm2_g"], sv["sc2"], dx_out)
    dproj, dg1 = _gate_bwd(dx1, sv["proj"], sv["g1"])
    do = _mm("mm_out_dx", dproj, wl["w_out"], dims=NT, grid=(nb, 2, 1),
             a_block=(bm, D), a_map=lambda i, j, k: (i, 0),
             b_block=(512, D), b_map=lambda i, j, k: (j, 0),
             out_shape=(L, D), o_block=(bm, 512), o_map=lambda i, j, k: (i, j))
    g["w_out"] = _mm("mm_out_dw", sv["o"], dproj, dims=TN, grid=(1, 2, nk),
                     a_block=(bk, D), a_map=lambda i, j, k: (k, 0),
                     b_block=(bk, 512), b_map=lambda i, j, k: (k, j),
                     out_shape=(D, D), o_block=(D, 512), o_map=lambda i, j, k: (0, j))
    doa, dos, g["attn_out_g"], g["ssm_out_g"] = _outnorm_bwd(do, sv["oa"], sv["os"],
                                                             wl["attn_out_g"], wl["ssm_out_g"])
    (dqn, dkn, dv), parts = _attn_bwd(sv["qkv"], sv["oa"], doa, _scatter_comm(sums) if prev else None)
    prev_reduced = _reduced(parts, sums) if prev else None
    (du, d_abar_re, d_abar_im, g["ssm_d"], g["glu_b"], d_ccr, d_cci, d_bbre, d_bbim, d_gw) = _s5_bwd2(
        dos, sv["p"], sv["s_re"], sv["s_im"], wl["abar_re"], wl["abar_im"], wl["bbr"], wl["bbi"],
        wl["ccr"], wl["cci"], wl["dsk"], wl["gw"], wl["gb"])
    g["ssm_c_re"] = d_ccr.reshape(NG, NP, NH16).transpose(0, 2, 1)
    g["ssm_c_im"] = -d_cci.reshape(NG, NP, NH16).transpose(0, 2, 1)
    g["glu_w"] = d_gw.reshape(NG, NH16, NH16)
    d_ldt, d_ar, d_ai, d_br, d_bi = _s5_disc_bwd(
        wl["ldt_col"], wl["ar_col"], wl["ai_col"], wl["br_mat"], wl["bi_mat"],
        d_abar_re.reshape(NS, 1), d_abar_im.reshape(NS, 1), d_bbre, d_bbim)
    g["ssm_log_dt"] = d_ldt[:, 0]
    g["ssm_a_re"] = d_ar.reshape(NG, NP)
    g["ssm_a_im"] = d_ai.reshape(NG, NP)
    g["ssm_b_re"] = d_br.reshape(NG, NP, NH16)
    g["ssm_b_im"] = d_bi.reshape(NG, NP, NH16)
    dp, dgqk = _qknorm_bwd(dqn, dkn, dv, du, sv["p"], wl["gqk"])
    g["q_norm_g"] = dgqk[0, :HD]
    g["k_norm_g"] = dgqk[0, HD:2 * HD]
    dh = _mm("mm_in_dx", dp, wl["w_in"], dims=NT, grid=(nb, 1, 4),
             a_block=(bm, 512), a_map=lambda i, j, k: (i, k),
             b_block=(None, D, 512), b_map=lambda i, j, k: (k, 0, 0),
             out_shape=(L, D), o_block=(bm, D), o_map=lambda i, j, k: (i, 0))
    g["w_in"] = _mm("mm_in_dw", sv["h"], dp, dims=TN, grid=(1, 4, nk),
                    a_block=(bk, D), a_map=lambda i, j, k: (k, 0),
                    b_block=(bk, 512), b_map=lambda i, j, k: (k, j),
                    out_shape=(NCHIP, D, 512), o_block=(None, D, 512), o_map=lambda i, j, k: (j, 0, 0))
    dx_in, dsh1, dsc1, g["norm1_g"] = _rmsmod_bwd(dh, sv["xin"], wl["norm1_g"], sv["sc1"], dx1)
    g["dmod"] = jnp.concatenate([dsh1, dsc1, dg1, dsh2, dsc2, dg2], axis=1)[0]
    return dx_in, g, prev_reduced


def _prep_layer(l, conv_w, small):
    wl = {}
    wl["conv_w"] = conv_w[l]
    wl["conv_b"] = small["ffn_conv_b"][l][None, :]
    wl["norm1_g"] = small["norm1_g"][l][None, :]
    wl["norm2_g"] = small["norm2_g"][l][None, :]
    wl["attn_out_g"] = small["attn_out_g"][l][None, :]
    wl["ssm_out_g"] = small["ssm_out_g"][l][None, :]
    wl["gqk"] = jnp.concatenate([jnp.tile(small["q_norm_g"][l], 8), jnp.tile(small["k_norm_g"][l], 8)])[None, :]
    wl["ldt_col"] = jnp.repeat(small["ssm_log_dt"][l], NP)[:, None]
    wl["ar_col"] = small["ssm_a_re"][l].reshape(NS, 1)
    wl["ai_col"] = small["ssm_a_im"][l].reshape(NS, 1)
    wl["br_mat"] = small["ssm_b_re"][l].reshape(NS, NH16)
    wl["bi_mat"] = small["ssm_b_im"][l].reshape(NS, NH16)
    abar_re, abar_im, bb_re, bb_im = _s5_disc_fwd(wl["ldt_col"], wl["ar_col"], wl["ai_col"],
                                                  wl["br_mat"], wl["bi_mat"])
    wl["abar_re"] = abar_re.reshape(1, NS)
    wl["abar_im"] = abar_im.reshape(1, NS)
    wl["bbr"] = _block_diag(bb_re.astype(bf16).reshape(NG, NP, NH16).transpose(0, 2, 1))
    wl["bbi"] = _block_diag(bb_im.astype(bf16).reshape(NG, NP, NH16).transpose(0, 2, 1))
    wl["ccr"] = _block_diag(small["ssm_c_re"][l].astype(bf16).transpose(0, 2, 1))
    wl["cci"] = _block_diag(small["ssm_c_im"][l].astype(bf16).transpose(0, 2, 1))
    wl["gw"] = _block_diag(small["glu_w"][l].astype(bf16))
    wl["dsk"] = small["ssm_d"][l].reshape(1, SSMW)
    wl["gb"] = small["glu_b"][l].reshape(1, SSMW)
    return wl


def _local_step(x, target, mods, shards, conv_w, small):
    def set_big(wl, gathered):
        w_in, w_out, w_up, w_down = gathered
        wl.update(w_in=w_in, w_up=w_up, w_out=w_out.reshape(D, D), w_down=w_down.reshape(DFF, D))

    wls = [_prep_layer(l, conv_w, small) for l in range(DEPTH)]
    set_big(wls[0], _comm_call("weight_gather", _gather_comm(shards, 0)))
    saved = []
    cur, res, res_gate = x, None, None
    for l in range(DEPTH):
        nxt = _gather_comm(shards, l + 1) if l + 1 < DEPTH else None
        cur, res, res_gate, sv, couts = _layer_fwd(cur, res, res_gate, wls[l], mods[l], nxt)
        saved.append(sv)
        if nxt is not None:
            set_big(wls[l + 1], couts)
    loss, dx = _loss_fwd_bwd(cur, res, res_gate, target)
    grads = [None] * DEPTH
    big = [None] * DEPTH
    prev = None
    for l in reversed(range(DEPTH)):
        dx, grads[l], done = _layer_bwd(dx, wls[l], saved[l], prev)
        if prev is not None:
            big[l + 1] = done
        prev = _big_grads(grads[l])
    recv = _comm_call("grad_swap", _swap_comm(prev))
    sums = _chip_sums(prev, recv)
    big[0] = _reduced(_comm_call("grad_scatter", _scatter_comm(sums)), sums)
    return loss, dx, grads, big


BIG = ("w_in", "w_out", "ffn_w_up", "ffn_w_down")
SMALL = ("ada_b", "norm1_g", "q_norm_g", "k_norm_g", "ssm_a_re", "ssm_a_im", "ssm_log_dt",
         "ssm_b_re", "ssm_b_im", "ssm_c_re", "ssm_c_im", "ssm_d", "glu_w", "glu_b",
         "attn_out_g", "ssm_out_g", "norm2_g", "ffn_conv_b")
NAMES = ("ada_w", "ada_b", "norm1_g", "w_in", "q_norm_g", "k_norm_g", "ssm_a_re", "ssm_a_im",
         "ssm_log_dt", "ssm_b_re", "ssm_b_im", "ssm_c_re", "ssm_c_im", "ssm_d", "glu_w", "glu_b",
         "attn_out_g", "ssm_out_g", "w_out", "norm2_g", "ffn_w_up", "ffn_conv_w", "ffn_conv_b",
         "ffn_w_down")
PACK_COLS = 1024


def _pack(arrs):
    flat = jnp.concatenate([a.reshape(-1) for a in arrs])
    rows = -(-flat.shape[0] // PACK_COLS)
    rows = -(-rows // 8) * 8
    flat = jnp.pad(flat, (0, rows * PACK_COLS - flat.shape[0]))
    return flat.reshape(rows, PACK_COLS)


def _unpack(packed, shapes):
    flat = packed.reshape(-1)
    out, off = [], 0
    for s in shapes:
        n = math.prod(s)
        out.append(flat[off:off + n].reshape(s))
        off += n
    return out


def kernel(x, c, ada_w, ada_b, norm1_g, w_in, q_norm_g, k_norm_g, ssm_a_re, ssm_a_im, ssm_log_dt, ssm_b_re, ssm_b_im, ssm_c_re, ssm_c_im, ssm_d, glu_w, glu_b, attn_out_g, ssm_out_g, w_out, norm2_g, ffn_w_up, ffn_conv_w, ffn_conv_b, ffn_w_down, loss_target, m_ada_w, m_ada_b, m_norm1_g, m_w_in, m_q_norm_g, m_k_norm_g, m_ssm_a_re, m_ssm_a_im, m_ssm_log_dt, m_ssm_b_re, m_ssm_b_im, m_ssm_c_re, m_ssm_c_im, m_ssm_d, m_glu_w, m_glu_b, m_attn_out_g, m_ssm_out_g, m_w_out, m_norm2_g, m_ffn_w_up, m_ffn_conv_w, m_ffn_conv_b, m_ffn_w_down, v_ada_w, v_ada_b, v_norm1_g, v_w_in, v_q_norm_g, v_k_norm_g, v_ssm_a_re, v_ssm_a_im, v_ssm_log_dt, v_ssm_b_re, v_ssm_b_im, v_ssm_c_re, v_ssm_c_im, v_ssm_d, v_glu_w, v_glu_b, v_attn_out_g, v_ssm_out_g, v_w_out, v_norm2_g, v_ffn_w_up, v_ffn_conv_w, v_ffn_conv_b, v_ffn_w_down):
    env = dict(locals())
    w = {n: env[n] for n in NAMES}
    m = {n: env["m_" + n] for n in NAMES}
    v = {n: env["v_" + n] for n in NAMES}
    mx, my, mc = _my_pos()
    chip = 2 * mx + my
    dev = 4 * mx + 2 * my + mc
    xl = x[0]
    tl = loss_target[0]

    c_all = _allgather8("gather_c", jnp.pad(c, ((0, 7), (0, 0))))[:, 0, :]
    c16 = jnp.pad(c_all, ((0, 8), (0, 0)))
    nloc = NMOD * D // NCHIP
    ada_b_loc = lax.dynamic_slice(ada_b, (0, chip * nloc), (DEPTH, nloc))[:, None, :]
    mod_loc = _mod_fwd(c16, ada_w, ada_b_loc)
    mod_g = _allgather8("gather_mod", mod_loc.reshape(DEPTH * 16, nloc))
    mod_all = mod_g[0::2].reshape(NCHIP, DEPTH, 16, nloc).transpose(1, 2, 0, 3).reshape(DEPTH, 16, NMOD * D)
    mods = lax.dynamic_index_in_dim(mod_all, dev, axis=1, keepdims=False)

    shards = [_cast_bf16(w[n]) for n in BIG]
    ncw = DUP // NCHIP
    conv_g = _allgather8("gather_conv", jnp.pad(w["ffn_conv_w"].reshape(DEPTH * 3, ncw), ((0, 4), (0, 0))))
    conv_w = conv_g[0::2, :DEPTH * 3].reshape(NCHIP, DEPTH, 3, ncw).transpose(1, 2, 0, 3).reshape(DEPTH, 3, DUP)
    small = {n: w[n] for n in SMALL}

    loss_loc, grad_x, grads, big = _local_step(xl, tl, mods, shards, conv_w, small)
    loss = lax.psum(loss_loc, ("x", "y", "c"))

    outs = {}
    for k, name in enumerate(BIG):
        g_big = jnp.stack([big[l][k] for l in range(DEPTH)])
        outs[name] = _adamw(g_big[None], w[name], m[name], v[name])

    small_names = SMALL[1:] + ("ffn_conv_w_full",)
    small_grads = []
    for n in SMALL[1:]:
        key = {"ffn_conv_b": "conv_b"}.get(n, n)
        small_grads.append(jnp.stack([grads[l][key].reshape(w[n].shape[1:]) for l in range(DEPTH)]))
    small_grads.append(jnp.stack([grads[l]["conv_w"] for l in range(DEPTH)]))
    dmod = jnp.stack([grads[l]["dmod"] for l in range(DEPTH)])
    packed = _pack([dmod] + small_grads)
    allp = _allgather8("gather_small", packed)
    shapes = [dmod.shape] + [a.shape for a in small_grads]
    dmod_all = allp[:, :DEPTH * NMOD, :].reshape(NDEV, DEPTH, NMOD * D)
    dmod_loc = lax.dynamic_slice(dmod_all, (0, 0, chip * nloc), (NDEV, DEPTH, nloc)).transpose(1, 0, 2)
    g_ada = _ada_grad(c16, jnp.pad(dmod_loc, ((0, 0), (0, 8), (0, 0))))
    outs["ada_w"] = _adamw(g_ada[None], w["ada_w"], m["ada_w"], v["ada_w"])
    conv_w_shape = (DEPTH, 3, DUP)
    w_small = _pack([w[n] for n in SMALL] + [jnp.zeros(conv_w_shape, f32)])
    m_small = _pack([m[n] for n in SMALL] + [jnp.zeros(conv_w_shape, f32)])
    v_small = _pack([v[n] for n in SMALL] + [jnp.ones(conv_w_shape, f32)])
    res_small = _adamw(allp, w_small, m_small, v_small)
    unpacked = [_unpack(r, shapes) for r in res_small]
    for i, n in enumerate(SMALL):
        outs[n] = tuple(unpacked[k][i] for k in range(4))
    g_conv_full = unpacked[0][len(SMALL)]
    ncw = DUP // NCHIP
    g_conv = lax.dynamic_slice(g_conv_full, (0, 0, chip * ncw), (DEPTH, 3, ncw))
    outs["ffn_conv_w"] = _adamw(g_conv[None], w["ffn_conv_w"], m["ffn_conv_w"], v["ffn_conv_w"])

    result = [loss, grad_x[None]]
    for k in range(4):
        result += [outs[n][k] for n in NAMES]
    return tuple(result)
```

```python
import functools
import math

import jax
import jax.numpy as jnp
from jax import lax
from jax.experimental import pallas as pl
from jax.experimental.pallas import tpu as pltpu

f32 = jnp.float32
bf16 = jnp.bfloat16
_MXU = jnp.bfloat16

D = 1024
ATT = 512
SSMW = 512
HD = 64
NG = 32
NP = 64
NH16 = 16
NS = NG * NP
INW = 2048
DFF = 2816
DUP = 2 * DFF
NMOD = 6
DEPTH = 4
EPS = 1e-6
NCHIP = 4
NDEV = 8

ADAM_LR = 0.001
ADAM_B1 = 0.9
ADAM_B2 = 0.999
ADAM_EPS = 1e-08
ADAM_WD = 0.01
ADAM_STEP = 10

VMEM_LIMIT = 56 * 1024 * 1024
MESH = pl.DeviceIdType.MESH

NN = (((1,), (0,)), ((), ()))
NT = (((1,), (1,)), ((), ()))
TN = (((0,), (0,)), ((), ()))


def _cp(sem=None):
    if sem is None:
        return pltpu.CompilerParams(vmem_limit_bytes=VMEM_LIMIT)
    return pltpu.CompilerParams(dimension_semantics=sem, vmem_limit_bytes=VMEM_LIMIT)


def _dot(a, b, dims=NN):
    return lax.dot_general(a.astype(_MXU), b.astype(_MXU), dims, preferred_element_type=f32)


def _dot_exact(x, m):
    hi = x.astype(bf16)
    r1 = x - hi.astype(f32)
    mid = r1.astype(bf16)
    lo = (r1 - mid.astype(f32)).astype(bf16)
    d = lambda p: lax.dot_general(p, m, NN, preferred_element_type=f32)
    return d(hi) + d(mid) + d(lo)


def _gelu(x):
    c = math.sqrt(2.0 / math.pi)
    return 0.5 * x * (1.0 + jnp.tanh(c * (x + 0.044715 * (x * x * x))))


def _gelu_grad(x):
    c = math.sqrt(2.0 / math.pi)
    t = jnp.tanh(c * (x + 0.044715 * (x * x * x)))
    return 0.5 * (1.0 + t) + 0.5 * x * (1.0 - t * t) * c * (1.0 + 3.0 * 0.044715 * (x * x))


def _sigmoid(x):
    return 1.0 / (1.0 + jnp.exp(-x))


def _mm(name, a, b, *, dims, grid, a_block, a_map, b_block, b_map, out_shape, o_block, o_map,
        out_dtype=f32):
    nk = grid[2]
    acc_shape = tuple(s for s in o_block if s is not None)

    def body(a_ref, b_ref, o_ref, acc_ref):
        k = pl.program_id(2)
        part = _dot(a_ref[...], b_ref[...], dims)
        if nk == 1:
            o_ref[...] = part.astype(o_ref.dtype)
        else:
            @pl.when(k == 0)
            def _():
                acc_ref[...] = part

            @pl.when(k > 0)
            def _():
                acc_ref[...] += part

            @pl.when(k == nk - 1)
            def _():
                o_ref[...] = acc_ref[...].astype(o_ref.dtype)

    return pl.pallas_call(
        body, name=name, grid=grid,
        in_specs=[pl.BlockSpec(a_block, a_map), pl.BlockSpec(b_block, b_map)],
        out_specs=pl.BlockSpec(o_block, o_map),
        out_shape=jax.ShapeDtypeStruct(out_shape, out_dtype),
        scratch_shapes=[pltpu.VMEM(acc_shape if nk > 1 else (8, 128), f32)],
        compiler_params=_cp(("parallel", "parallel", "arbitrary")),
    )(a, b)


def _row_block(L, want):
    return want if L % want == 0 else L


def _rmsmod_fwd(x, res, gate, g, sh, sc):
    L = x.shape[0]
    bm = _row_block(L, 256)
    with_res = res is not None

    def body(*refs):
        if with_res:
            x_ref, r_ref, gt_ref, g_ref, sh_ref, sc_ref, xo_ref, h_ref = refs
            xin = x_ref[...] + gt_ref[...] * r_ref[...]
            xo_ref[...] = xin
        else:
            x_ref, g_ref, sh_ref, sc_ref, h_ref = refs
            xin = x_ref[...]
        inv = lax.rsqrt(jnp.mean(xin * xin, axis=-1, keepdims=True) + EPS)
        xn = xin * inv * g_ref[...]
        h_ref[...] = (xn * (1.0 + sc_ref[...]) + sh_ref[...]).astype(h_ref.dtype)

    row = pl.BlockSpec((bm, D), lambda i: (i, 0))
    vec = pl.BlockSpec((1, D), lambda i: (0, 0))
    if with_res:
        return pl.pallas_call(
            body, name="rmsmod_res_fwd", grid=(L // bm,),
            in_specs=[row, row, vec, vec, vec, vec], out_specs=[row, row],
            out_shape=[jax.ShapeDtypeStruct((L, D), f32), jax.ShapeDtypeStruct((L, D), bf16)],
            compiler_params=_cp(("parallel",)),
        )(x, res, gate, g, sh, sc)
    h = pl.pallas_call(
        body, name="rmsmod_fwd", grid=(L // bm,),
        in_specs=[row, vec, vec, vec], out_specs=row,
        out_shape=jax.ShapeDtypeStruct((L, D), bf16),
        compiler_params=_cp(("parallel",)),
    )(x, g, sh, sc)
    return None, h


def _rmsmod_bwd(dh, x, g, sc, dres):
    L = x.shape[0]
    bm = _row_block(L, 256)

    def body(dh_ref, x_ref, g_ref, sc_ref, dr_ref, dx_ref, dsh_ref, dsc_ref, dg_ref):
        i = pl.program_id(0)
        xv = x_ref[...]
        dhv = dh_ref[...]
        inv = lax.rsqrt(jnp.mean(xv * xv, axis=-1, keepdims=True) + EPS)
        xh = xv * inv
        gv = g_ref[...]
        xn = xh * gv
        dxn = dhv * (1.0 + sc_ref[...])
        dxh = dxn * gv
        dx_ref[...] = inv * (dxh - xh * jnp.mean(dxh * xh, axis=-1, keepdims=True)) + dr_ref[...]
        p_sh = jnp.sum(dhv, axis=0, keepdims=True)
        p_sc = jnp.sum(dhv * xn, axis=0, keepdims=True)
        p_g = jnp.sum(dxn * xh, axis=0, keepdims=True)

        @pl.when(i == 0)
        def _():
            dsh_ref[...] = p_sh
            dsc_ref[...] = p_sc
            dg_ref[...] = p_g

        @pl.when(i > 0)
        def _():
            dsh_ref[...] += p_sh
            dsc_ref[...] += p_sc
            dg_ref[...] += p_g

    row = pl.BlockSpec((bm, D), lambda i: (i, 0))
    vec = pl.BlockSpec((1, D), lambda i: (0, 0))
    return pl.pallas_call(
        body, name="rmsmod_bwd", grid=(L // bm,),
        in_specs=[row, row, vec, vec, row], out_specs=[row, vec, vec, vec],
        out_shape=[jax.ShapeDtypeStruct((L, D), f32)] + [jax.ShapeDtypeStruct((1, D), f32)] * 3,
        compiler_params=_cp(("arbitrary",)),
    )(dh, x, g, sc, dres)


def _gate_bwd(dx, y, gate):
    L = dx.shape[0]
    bm = _row_block(L, 256)

    def body(dx_ref, y_ref, gt_ref, dy_ref, dgt_ref):
        i = pl.program_id(0)
        dxv = dx_ref[...]
        dy_ref[...] = (gt_ref[...] * dxv).astype(dy_ref.dtype)
        part = jnp.sum(dxv * y_ref[...], axis=0, keepdims=True)

        @pl.when(i == 0)
        def _():
            dgt_ref[...] = part

        @pl.when(i > 0)
        def _():
            dgt_ref[...] += part

    row = pl.BlockSpec((bm, D), lambda i: (i, 0))
    vec = pl.BlockSpec((1, D), lambda i: (0, 0))
    return pl.pallas_call(
        body, name="gate_bwd", grid=(L // bm,),
        in_specs=[row, row, vec], out_specs=[row, vec],
        out_shape=[jax.ShapeDtypeStruct((L, D), bf16), jax.ShapeDtypeStruct((1, D), f32)],
        compiler_params=_cp(("arbitrary",)),
    )(dx, y, gate)


def _loss_fwd_bwd(x1, down, gate, target):
    L = x1.shape[0]
    bm = _row_block(L, 256)
    nsteps = L // bm

    def body(x_ref, d_ref, gt_ref, t_ref, dy_ref, loss_ref, acc_ref):
        i = pl.program_id(0)
        diff = x_ref[...] + gt_ref[...] * d_ref[...] - t_ref[...]
        dy_ref[...] = diff * (1.0 / D)
        part = jnp.sum(diff * diff, axis=0, keepdims=True)

        @pl.when(i == 0)
        def _():
            acc_ref[...] = part

        @pl.when(i > 0)
        def _():
            acc_ref[...] += part

        @pl.when(i == nsteps - 1)
        def _():
            tot = jnp.sum(acc_ref[...], axis=1, keepdims=True) * (0.5 / D)
            loss_ref[...] = jnp.broadcast_to(tot, (8, 128))

    row = pl.BlockSpec((bm, D), lambda i: (i, 0))
    vec = pl.BlockSpec((1, D), lambda i: (0, 0))
    dy, loss = pl.pallas_call(
        body, name="loss_fwd_bwd", grid=(nsteps,),
        in_specs=[row, row, vec, row],
        out_specs=[row, pl.BlockSpec((8, 128), lambda i: (0, 0))],
        out_shape=[jax.ShapeDtypeStruct((L, D), f32), jax.ShapeDtypeStruct((8, 128), f32)],
        scratch_shapes=[pltpu.VMEM((1, D), f32)],
        compiler_params=_cp(("arbitrary",)),
    )(x1, down, gate, target)
    return loss[0, 0], dy


def _head_mean_matrix():
    r = lax.broadcasted_iota(jnp.int32, (128, 128), 0) // HD
    c = lax.broadcasted_iota(jnp.int32, (128, 128), 1) // HD
    return jnp.where(r == c, 1.0 / HD, 0.0).astype(bf16)


def _qknorm_fwd(p, gqk):
    L = p.shape[0]
    bm = _row_block(L, 256)

    def body(p_ref, g_ref, o_ref):
        e = _head_mean_matrix()
        for c in range(8):
            sl = slice(128 * c, 128 * (c + 1))
            xv = p_ref[:, sl]
            inv = lax.rsqrt(_dot_exact2(xv * xv, e) + EPS)
            yv = xv * inv * g_ref[:, sl]
            if c < 4:
                yv = yv * (1.0 / math.sqrt(HD))
            o_ref[:, sl] = yv.astype(o_ref.dtype)
        o_ref[:, 1024:1536] = p_ref[:, 1024:1536].astype(o_ref.dtype)

    return pl.pallas_call(
        body, name="qknorm_fwd", grid=(L // bm,),
        in_specs=[pl.BlockSpec((bm, 1536), lambda i: (i, 0)), pl.BlockSpec((1, 1024), lambda i: (0, 0))],
        out_specs=pl.BlockSpec((bm, 1536), lambda i: (i, 0)),
        out_shape=jax.ShapeDtypeStruct((L, 1536), bf16),
        compiler_params=_cp(("parallel",)),
    )(p, gqk)


def _qknorm_bwd(dqn, dkn, dval, du, p, gqk):
    L = p.shape[0]
    bm = _row_block(L, 256)
    nsteps = L // bm

    def body(dq_ref, dk_ref, dv_ref, du_ref, p_ref, g_ref, o_ref, dg_ref, acc_ref):
        i = pl.program_id(0)
        e = _head_mean_matrix()
        o_ref[:, 1024:1536] = dv_ref[...].astype(o_ref.dtype)
        o_ref[:, 1536:2048] = du_ref[...].astype(o_ref.dtype)
        for c in range(8):
            sl = slice(128 * c, 128 * (c + 1))
            xv = p_ref[:, sl]
            d_ref = dq_ref if c < 4 else dk_ref
            dv = d_ref[:, 128 * (c % 4):128 * (c % 4 + 1)]
            gv = g_ref[:, sl]
            inv = lax.rsqrt(_dot_exact2(xv * xv, e) + EPS)
            xh = xv * inv
            dxh = dv * gv
            o_ref[:, sl] = (inv * (dxh - xh * _dot_exact2(dxh * xh, e))).astype(o_ref.dtype)
            part = jnp.sum(dv * xh, axis=0, keepdims=True)

            @pl.when(i == 0)
            def _():
                acc_ref[:, sl] = part

            @pl.when(i > 0)
            def _():
                acc_ref[:, sl] += part

        @pl.when(i == nsteps - 1)
        def _():
            r = lax.broadcasted_iota(jnp.int32, (1024, 128), 0)
            col = lax.broadcasted_iota(jnp.int32, (1024, 128), 1)
            fold = jnp.where(col == (r // 512) * HD + r % HD, 1.0, 0.0).astype(bf16)
            dg_ref[...] = _dot_exact(jnp.broadcast_to(acc_ref[...], (8, 1024)), fold)

    half = pl.BlockSpec((bm, 512), lambda i: (i, 0))
    return pl.pallas_call(
        body, name="qknorm_bwd", grid=(nsteps,),
        in_specs=[half, half, half, half, pl.BlockSpec((bm, 1024), lambda i: (i, 0)),
                  pl.BlockSpec((1, 1024), lambda i: (0, 0))],
        out_specs=[pl.BlockSpec((bm, INW), lambda i: (i, 0)), pl.BlockSpec((8, 128), lambda i: (0, 0))],
        out_shape=[jax.ShapeDtypeStruct((L, INW), bf16), jax.ShapeDtypeStruct((8, 128), f32)],
        scratch_shapes=[pltpu.VMEM((1, 1024), f32)],
        compiler_params=_cp(("arbitrary",)),
    )(dqn, dkn, dval, du, p, gqk)


def _outnorm_fwd(oa, os_, ga, gs):
    L = oa.shape[0]
    bm = _row_block(L, 256)

    def body(a_ref, s_ref, ga_ref, gs_ref, o_ref):
        for x_ref, g_ref, off in ((a_ref, ga_ref, 0), (s_ref, gs_ref, 512)):
            xv = x_ref[...]
            inv = lax.rsqrt(jnp.mean(xv * xv, axis=-1, keepdims=True) + EPS)
            o_ref[:, off:off + 512] = (xv * inv * g_ref[...]).astype(o_ref.dtype)

    half = pl.BlockSpec((bm, 512), lambda i: (i, 0))
    vec = pl.BlockSpec((1, 512), lambda i: (0, 0))
    return pl.pallas_call(
        body, name="outnorm_fwd", grid=(L // bm,),
        in_specs=[half, half, vec, vec], out_specs=pl.BlockSpec((bm, D), lambda i: (i, 0)),
        out_shape=jax.ShapeDtypeStruct((L, D), bf16),
        compiler_params=_cp(("parallel",)),
    )(oa, os_, ga, gs)


def _outnorm_bwd(do, oa, os_, ga, gs):
    L = oa.shape[0]
    bm = _row_block(L, 256)

    def body(do_ref, a_ref, s_ref, ga_ref, gs_ref, da_ref, ds_ref, dga_ref, dgs_ref):
        i = pl.program_id(0)
        for x_ref, g_ref, dx_ref, dg_ref, off in ((a_ref, ga_ref, da_ref, dga_ref, 0),
                                                  (s_ref, gs_ref, ds_ref, dgs_ref, 512)):
            xv = x_ref[...]
            dv = do_ref[:, off:off + 512]
            inv = lax.rsqrt(jnp.mean(xv * xv, axis=-1, keepdims=True) + EPS)
            xh = xv * inv
            dxh = dv * g_ref[...]
            dx_ref[...] = inv * (dxh - xh * jnp.mean(dxh * xh, axis=-1, keepdims=True))
            part = jnp.sum(dv * xh, axis=0, keepdims=True)

            @pl.when(i == 0)
            def _():
                dg_ref[...] = part

            @pl.when(i > 0)
            def _():
                dg_ref[...] += part

    half = pl.BlockSpec((bm, 512), lambda i: (i, 0))
    vec = pl.BlockSpec((1, 512), lambda i: (0, 0))
    return pl.pallas_call(
        body, name="outnorm_bwd", grid=(L // bm,),
        in_specs=[pl.BlockSpec((bm, D), lambda i: (i, 0)), half, half, vec, vec],
        out_specs=[half, half, vec, vec],
        out_shape=[jax.ShapeDtypeStruct((L, 512), f32)] * 2 + [jax.ShapeDtypeStruct((1, 512), f32)] * 2,
        compiler_params=_cp(("arbitrary",)),
    )(do, oa, os_, ga, gs)


NCB = DFF // 128


def _conv_rows(L):
    return 64 if L % 64 == 0 else 32


def _conv_chunk(x_ref, w_ref, b_ref, i, nch, R, L, ahead):
    base = pl.multiple_of(i * R, R)
    prev = x_ref[pl.ds(pl.multiple_of(jnp.maximum(base - 8, 0), 8), 8), :]
    parts = [jnp.where(i > 0, prev, 0.0), x_ref[pl.ds(base, R), :]]
    if ahead:
        nxt = x_ref[pl.ds(pl.multiple_of(jnp.minimum(base + R, L - 8), 8), 8), :]
        parts.append(jnp.where(i < nch - 1, nxt, 0.0))
    xe = jnp.concatenate(parts, axis=0)
    x1 = pltpu.roll(xe, 1, 0)
    x2 = pltpu.roll(xe, 2, 0)
    u = b_ref[...] + x2 * w_ref[0:1, :]
    u = u + x1 * w_ref[1:2, :]
    u = u + xe * w_ref[2:3, :]
    return u[8:], xe[8:], x1[8:], x2[8:]


def _fold8(x):
    out = x[0:8]
    for q in range(1, x.shape[0] // 8):
        out = out + x[8 * q:8 * (q + 1)]
    return out


def _convglu_fwd(up_pre, cw, cb):
    L = up_pre.shape[0]

    def body(v_ref, g_ref, wv_ref, wg_ref, bv_ref, bg_ref, a_ref):
        row = lax.broadcasted_iota(jnp.int32, (L, 128), 0)

        def conv(x_ref, w_ref, b_ref):
            x = x_ref[...]
            out = b_ref[...] + jnp.where(row >= 2, pltpu.roll(x, 2, 0), 0.0) * w_ref[0:1, :]
            out = out + jnp.where(row >= 1, pltpu.roll(x, 1, 0), 0.0) * w_ref[1:2, :]
            return out + x * w_ref[2:3, :]

        val = conv(v_ref, wv_ref, bv_ref)
        a_ref[...] = (_gelu(conv(g_ref, wg_ref, bg_ref)) * val).astype(a_ref.dtype)

    col = lambda off: pl.BlockSpec((L, 128), lambda j: (0, j + off))
    wsp = lambda off: pl.BlockSpec((3, 128), lambda j: (0, j + off))
    bsp = lambda off: pl.BlockSpec((1, 128), lambda j: (0, j + off))
    return pl.pallas_call(
        body, name="convglu_fwd", grid=(NCB,),
        in_specs=[col(0), col(NCB), wsp(0), wsp(NCB), bsp(0), bsp(NCB)],
        out_specs=pl.BlockSpec((L, 128), lambda j: (0, j)),
        out_shape=jax.ShapeDtypeStruct((L, DFF), bf16),
        compiler_params=_cp(("parallel",)),
    )(up_pre, up_pre, cw, cw, cb, cb)


def _convglu_bwd(da, up_pre, cw, cb, comm=None):
    L = up_pre.shape[0]
    R = _conv_rows(L)
    nch = L // R
    n = R + 8

    def body(da_ref, v_ref, g_ref, wv_ref, wg_ref, bv_ref, bg_ref, dx_ref, dw_ref, db_ref):
        def chunk(i, acc):
            base = pl.multiple_of(i * R, R)
            val, xv, xv1, xv2 = _conv_chunk(v_ref, wv_ref, bv_ref, i, nch, R, L, True)
            gate, xg, xg1, xg2 = _conv_chunk(g_ref, wg_ref, bg_ref, i, nch, R, L, True)
            nxt = da_ref[pl.ds(pl.multiple_of(jnp.minimum(base + R, L - 8), 8), 8), :]
            dav = jnp.concatenate([da_ref[pl.ds(base, R), :], jnp.where(i < nch - 1, nxt, 0.0)], axis=0)
            c = math.sqrt(2.0 / math.pi)
            t = jnp.tanh(c * (gate + 0.044715 * (gate * gate * gate)))
            d_val = dav * (0.5 * gate * (1.0 + t))
            d_gate = dav * val * (0.5 * (1.0 + t)
                                  + 0.5 * gate * (1.0 - t * t) * c * (1.0 + 3.0 * 0.044715 * (gate * gate)))
            new = []
            for k, (d, w_ref, xs) in enumerate(((d_val, wv_ref, (xv2, xv1, xv)), (d_gate, wg_ref, (xg2, xg1, xg)))):
                dx = d * w_ref[2:3, :] + pltpu.roll(d, n - 1, 0) * w_ref[1:2, :]
                dx = dx + pltpu.roll(d, n - 2, 0) * w_ref[0:1, :]
                dx_ref[k, pl.ds(base, R), :] = dx[:R].astype(dx_ref.dtype)
                dr = d[:R]
                new += [_fold8(dr * x[:R]) for x in xs] + [_fold8(dr)]
            return tuple(a + b for a, b in zip(acc, new))

        acc = lax.fori_loop(0, nch, chunk, (jnp.zeros((8, 128), f32),) * 8)
        for k in range(2):
            for r in range(3):
                dw_ref[k, r:r + 1, :] = jnp.sum(acc[4 * k + r], axis=0, keepdims=True)
            db_ref[k] = jnp.sum(acc[4 * k + 3], axis=0, keepdims=True)

    col = lambda off: pl.BlockSpec((L, 128), lambda j: (0, j + off))
    wsp = lambda off: pl.BlockSpec((3, 128), lambda j: (0, j + off))
    bsp = lambda off: pl.BlockSpec((1, 128), lambda j: (0, j + off))
    (dx, dw, db), couts = _call(
        body, comm, name="convglu_bwd", grid=(NCB,),
        in_specs=[col(0), col(0), col(NCB), wsp(0), wsp(NCB), bsp(0), bsp(NCB)],
        out_specs=[pl.BlockSpec((2, L, 128), lambda j: (0, 0, j)), pl.BlockSpec((2, 3, 128), lambda j: (0, 0, j)),
                   pl.BlockSpec((2, 1, 128), lambda j: (0, 0, j))],
        out_shape=[jax.ShapeDtypeStruct((2, L, DFF), bf16), jax.ShapeDtypeStruct((2, 3, DFF), f32),
                   jax.ShapeDtypeStruct((2, 1, DFF), f32)], scratch_shapes=[],
        sem=("parallel",), args=(da, up_pre, up_pre, cw, cw, cb, cb))
    return dx, dw.transpose(1, 0, 2).reshape(3, DUP), db.transpose(1, 0, 2).reshape(1, DUP), couts


def _attn_block(L):
    return 256 if L % 256 == 0 else 128


TAIL_DEAD = -110.0


def _dot_exact2(x, m):
    hi = x.astype(bf16)
    lo = (x - hi.astype(f32)).astype(bf16)
    return (lax.dot_general(hi, m, NN, preferred_element_type=f32)
            + lax.dot_general(lo, m, NN, preferred_element_type=f32))


def _sb_weights(z, mask, tri_gt, carry):
    l1p = jnp.log(1.0 + jnp.exp(-jnp.abs(z)))
    ls_pos = jnp.minimum(z, 0.0) - l1p
    lm = ls_pos - z
    if mask is not None:
        lm = jnp.where(mask, lm, 0.0)
    tail = _dot_exact2(lm, tri_gt) + carry
    w = jnp.exp(ls_pos + tail)
    if mask is not None:
        w = jnp.where(mask, w, 0.0)
    return w, ls_pos, lm


def _head_masks(B):
    lane = lax.broadcasted_iota(jnp.int32, (B, 128), 1)
    return lane < HD, lane >= HD


def _attn_fwd(qkv, comm=None):
    L = qkv.shape[0]
    B = _attn_block(L)
    nq = L // B

    def body(q_ref, k_ref, v_ref, o_ref):
        qi = pl.program_id(1)
        heads = _head_masks(B)
        ti = lax.broadcasted_iota(jnp.int32, (B, B), 0)
        si = lax.broadcasted_iota(jnp.int32, (B, B), 1)
        tri_gt = jnp.where(ti > si, 1.0, 0.0).astype(bf16)
        diag = si < ti
        qv = q_ref[...]
        zero16 = jnp.zeros((), bf16)

        def tile(jb, mask, acc, carries):
            ks = pl.multiple_of(jb * B, B)
            kb = k_ref[pl.ds(ks, B), :]
            vb = v_ref[pl.ds(ks, B), :]
            out = []
            for in_head, carry in zip(heads, carries):
                kh = jnp.where(in_head, kb, zero16)
                vh = jnp.where(in_head, vb, zero16)
                z = lax.dot_general(qv, kh, NT, preferred_element_type=f32)
                w, _, lm = _sb_weights(z, mask, tri_gt, carry)
                acc = acc + lax.dot_general(w.astype(bf16), vh, NN, preferred_element_type=f32)
                out.append(carry + jnp.sum(lm, axis=1, keepdims=True))
            return acc, out

        zc = jnp.zeros((B, 1), f32)
        acc, (c0, c1) = tile(qi, diag, jnp.zeros((B, 128), f32), (zc, zc))
        acc, (c0, c1) = tile(jnp.maximum(qi - 1, 0), jnp.broadcast_to(qi > 0, (B, B)), acc, (c0, c1))

        def cond(st):
            return jnp.logical_and(st[0] <= qi, st[4] > TAIL_DEAD)

        def step(st):
            n, a, p0, p1, _ = st
            a, (p0, p1) = tile(qi - n, None, a, (p0, p1))
            return n + 1, a, p0, p1, jnp.maximum(jnp.max(p0), jnp.max(p1))

        st = lax.while_loop(cond, step, (jnp.int32(2), acc, c0, c1, jnp.maximum(jnp.max(c0), jnp.max(c1))))
        o_ref[...] = st[1]

    (o,), couts = _call(
        body, comm, name="attn_fwd", grid=(4, nq),
        in_specs=[pl.BlockSpec((B, 128), lambda hp, i: (i, hp)),
                  pl.BlockSpec((L, 128), lambda hp, i: (0, 4 + hp)),
                  pl.BlockSpec((L, 128), lambda hp, i: (0, 8 + hp))],
        out_specs=[pl.BlockSpec((B, 128), lambda hp, i: (i, hp))],
        out_shape=[jax.ShapeDtypeStruct((L, ATT), f32)], scratch_shapes=[],
        sem=("parallel", "parallel"), args=(qkv, qkv, qkv))
    return o, couts


def _attn_bwd(qkv, o, do, comm=None):
    L = qkv.shape[0]
    B = _attn_block(L)
    nq = L // B

    def body(q_ref, k_ref, v_ref, o_ref, do_ref, dq_ref, dk_ref, dv_ref):
        qi = pl.program_id(1)

        @pl.when(qi == 0)
        def _():
            dk_ref[...] = jnp.zeros_like(dk_ref)
            dv_ref[...] = jnp.zeros_like(dv_ref)

        heads = _head_masks(B)
        ti = lax.broadcasted_iota(jnp.int32, (B, B), 0)
        si = lax.broadcasted_iota(jnp.int32, (B, B), 1)
        tri_gt = jnp.where(ti > si, 1.0, 0.0).astype(bf16)
        tri_ge = jnp.where(ti >= si, 1.0, 0.0).astype(bf16)
        diag = si < ti
        zero16 = jnp.zeros((), bf16)
        qv = q_ref[...]
        do16 = do_ref[...].astype(bf16)
        dsum_lanes = do16.astype(f32) * o_ref[...]
        qhs = [jnp.where(m, qv, zero16) for m in heads]
        dohs = [jnp.where(m, do16, zero16) for m in heads]
        dsums = [jnp.sum(jnp.where(m, dsum_lanes, 0.0), axis=1, keepdims=True) for m in heads]

        def tile(jb, mask, dq_a, state):
            ks = pl.multiple_of(jb * B, B)
            kb = k_ref[pl.ds(ks, B), :]
            vb = v_ref[pl.ds(ks, B), :]
            dk_blk = jnp.zeros((B, 128), f32)
            dv_blk = jnp.zeros((B, 128), f32)
            out = []
            for in_head, qh, doh, dsum, (carry, suffix) in zip(heads, qhs, dohs, dsums, state):
                kh = jnp.where(in_head, kb, zero16)
                vh = jnp.where(in_head, vb, zero16)
                z = lax.dot_general(qv, kh, NT, preferred_element_type=f32)
                w, ls_pos, lm = _sb_weights(z, mask, tri_gt, carry)
                w16 = w.astype(bf16)
                dw = lax.dot_general(doh, vh, NT, preferred_element_type=f32)
                da = w16.astype(f32) * dw
                sig = jnp.exp(ls_pos)
                suf = _dot_exact2(da, tri_ge) + suffix
                dz = da * (1.0 - sig) - sig * (dsum - suf)
                if mask is not None:
                    dz = jnp.where(mask, dz, 0.0)
                dz16 = dz.astype(bf16)
                dq_a = dq_a + lax.dot_general(dz16, kh, NN, preferred_element_type=f32)
                dk_blk = dk_blk + lax.dot_general(dz16, qh, TN, preferred_element_type=f32)
                dv_blk = dv_blk + lax.dot_general(w16, doh, TN, preferred_element_type=f32)
                out.append((carry + jnp.sum(lm, axis=1, keepdims=True),
                            suffix + jnp.sum(da, axis=1, keepdims=True)))
            dk_ref[pl.ds(ks, B), :] += dk_blk
            dv_ref[pl.ds(ks, B), :] += dv_blk
            return dq_a, out

        def alive(state):
            return jnp.maximum(jnp.max(state[0][0]), jnp.max(state[1][0]))

        zc = jnp.zeros((B, 1), f32)
        dq_acc, state = tile(qi, diag, jnp.zeros((B, 128), f32), ((zc, zc), (zc, zc)))
        dq_acc, state = tile(jnp.maximum(qi - 1, 0), jnp.broadcast_to(qi > 0, (B, B)), dq_acc, state)

        def cond(st):
            return jnp.logical_and(st[0] <= qi, st[6] > TAIL_DEAD)

        def step(st):
            n, a, c0, s0, c1, s1, _ = st
            a, new = tile(qi - n, None, a, ((c0, s0), (c1, s1)))
            return n + 1, a, new[0][0], new[0][1], new[1][0], new[1][1], alive(new)

        st = lax.while_loop(cond, step, (jnp.int32(2), dq_acc, state[0][0], state[0][1],
                                         state[1][0], state[1][1], alive(state)))
        dq_ref[...] = st[1] * (1.0 / math.sqrt(HD))

    blk = pl.BlockSpec((B, 128), lambda hp, i: (i, hp))
    return _call(
        body, comm, name="attn_bwd", grid=(4, nq),
        in_specs=[blk,
                  pl.BlockSpec((L, 128), lambda hp, i: (0, 4 + hp)),
                  pl.BlockSpec((L, 128), lambda hp, i: (0, 8 + hp)),
                  blk, blk],
        out_specs=[blk, pl.BlockSpec((L, 128), lambda hp, i: (0, hp)),
                   pl.BlockSpec((L, 128), lambda hp, i: (0, hp))],
        out_shape=[jax.ShapeDtypeStruct((L, ATT), f32)] * 3, scratch_shapes=[],
        sem=("parallel", "arbitrary"), args=(qkv, qkv, qkv, o, do))


def _s5_disc(ldt, ar, ai, br, bi):
    dt = jnp.exp(ldt)
    mag = jnp.exp(dt * ar)
    abar_re = mag * jnp.cos(dt * ai)
    abar_im = mag * jnp.sin(dt * ai)
    em_re = abar_re - 1.0
    em_im = abar_im
    den = ar * ar + ai * ai
    f_re = (em_re * ar + em_im * ai) / den
    f_im = (em_im * ar - em_re * ai) / den
    bb_re = f_re * br - f_im * bi
    bb_im = f_re * bi + f_im * br
    return abar_re, abar_im, bb_re, bb_im


def _s5_disc_fwd(ldt, ar, ai, br, bi):
    def body(ldt_ref, ar_ref, ai_ref, br_ref, bi_ref, o1, o2, o3, o4):
        outs = _s5_disc(ldt_ref[...], ar_ref[...], ai_ref[...], br_ref[...], bi_ref[...])
        for o_ref, v in zip((o1, o2, o3, o4), outs):
            o_ref[...] = v

    col = jax.ShapeDtypeStruct((NS, 1), f32)
    mat = jax.ShapeDtypeStruct((NS, NH16), f32)
    return pl.pallas_call(body, name="s5_disc_fwd", out_shape=[col, col, mat, mat],
                          compiler_params=_cp())(ldt, ar, ai, br, bi)


def _s5_disc_bwd(ldt, ar, ai, br, bi, d_are, d_aim, d_bbre, d_bbim):
    def body(ldt_ref, ar_ref, ai_ref, br_ref, bi_ref, c1, c2, c3, c4, o_ldt, o_ar, o_ai, o_br, o_bi):
        prim = (ldt_ref[...], ar_ref[...], ai_ref[...], br_ref[...], bi_ref[...])
        _, vjp = jax.vjp(_s5_disc, *prim)
        g_ldt, g_ar, g_ai, g_br, g_bi = vjp((c1[...], c2[...], c3[...], c4[...]))
        o_ar[...] = g_ar
        o_ai[...] = g_ai
        o_br[...] = g_br
        o_bi[...] = g_bi
        r = lax.broadcasted_iota(jnp.int32, (NG, NS), 0)
        c = lax.broadcasted_iota(jnp.int32, (NG, NS), 1)
        fold = jnp.where(c // NP == r, 1.0, 0.0).astype(bf16)
        hi = jnp.broadcast_to(g_ldt, (NS, 128))
        p1 = hi.astype(bf16)
        r1 = hi - p1.astype(f32)
        p2 = r1.astype(bf16)
        p3 = (r1 - p2.astype(f32)).astype(bf16)
        dd = lambda p: lax.dot_general(fold, p, NN, preferred_element_type=f32)
        o_ldt[...] = dd(p1) + dd(p2) + dd(p3)

    col = jax.ShapeDtypeStruct((NS, 1), f32)
    mat = jax.ShapeDtypeStruct((NS, NH16), f32)
    return pl.pallas_call(
        body, name="s5_disc_bwd",
        out_shape=[jax.ShapeDtypeStruct((NG, 128), f32), col, col, mat, mat],
        compiler_params=_cp())(ldt, ar, ai, br, bi, d_are, d_aim, d_bbre, d_bbim)


def _cmul(ar, ai, br, bi):
    return ar * br - ai * bi, ar * bi + ai * br


def _s5_chunk(L):
    return 256 if L % 256 == 0 else L


def _diag_fold(acc, rpg):
    aw = acc.shape[0]
    r = lax.broadcasted_iota(jnp.int32, (aw, 128), 0) // rpg
    c = lax.broadcasted_iota(jnp.int32, (aw, 128), 1) // NH16
    fr = lax.broadcasted_iota(jnp.int32, (128, NH16), 0) % NH16
    fc = lax.broadcasted_iota(jnp.int32, (128, NH16), 1)
    fold = jnp.where(fr == fc, 1.0, 0.0).astype(bf16)
    return _dot_exact(jnp.where(r == c, acc, 0.0), fold)


def _diag_fold_t(acc):
    r = lax.broadcasted_iota(jnp.int32, (128, 8 * NP), 0) // NH16
    c = lax.broadcasted_iota(jnp.int32, (128, 8 * NP), 1) // NP
    fr = lax.broadcasted_iota(jnp.int32, (128, NH16), 0) % NH16
    fc = lax.broadcasted_iota(jnp.int32, (128, NH16), 1)
    fold = jnp.where(fr == fc, 1.0, 0.0).astype(bf16)
    x = jnp.where(r == c, acc, 0.0)
    hi = x.astype(bf16)
    r1 = x - hi.astype(f32)
    mid = r1.astype(bf16)
    lo = (r1 - mid.astype(f32)).astype(bf16)
    d = lambda piece: lax.dot_general(piece, fold, TN, preferred_element_type=f32)
    return d(hi) + d(mid) + d(lo)


NTILE = NS // 128
TILE_GROUP = 4


def _seg_entry(lr, li, ar, ai, cr, ci, seg_rows, reverse, row):
    pr, pi = ar, ai
    for _ in range(seg_rows.bit_length() - 1):
        pr, pi = _cmul(pr, pi, pr, pi)
    if reverse:
        xr = jnp.where(row == 7, cr, pltpu.roll(lr, 7, 0))
        xi = jnp.where(row == 7, ci, pltpu.roll(li, 7, 0))
    else:
        xr = jnp.where(row == 0, cr, pltpu.roll(lr, 1, 0))
        xi = jnp.where(row == 0, ci, pltpu.roll(li, 1, 0))
    for d in (1, 2, 4):
        keep = (row < 8 - d) if reverse else (row >= d)
        shift = 8 - d if reverse else d
        mr, mi = _cmul(pr, pi, jnp.where(keep, pltpu.roll(xr, shift, 0), 0.0),
                       jnp.where(keep, pltpu.roll(xi, shift, 0), 0.0))
        xr, xi = xr + mr, xi + mi
        pr, pi = _cmul(pr, pi, pr, pi)
    return xr, xi


def _seg_scan(xr3, xi3, abar_re, abar_im, cr_ref, ci_ref, reverse, on_step=None):
    pitch = xr3.shape[1] // 8
    seg = pitch - 8
    row = lax.broadcasted_iota(jnp.int32, (8, 128), 0)
    for g0 in range(0, NTILE, TILE_GROUP):
        tiles = list(range(g0, g0 + TILE_GROUP))
        lanes = [slice(128 * j, 128 * (j + 1)) for j in tiles]
        ar = [jnp.broadcast_to(abar_re[:, ls], (8, 128)) for ls in lanes]
        ai = [jnp.broadcast_to(-abar_im[:, ls] if reverse else abar_im[:, ls], (8, 128)) for ls in lanes]

        def sweep(store):
            def step(n, st):
                k = (seg - 1 - n) if reverse else n
                rows = pl.ds(k, 8, stride=pitch)
                new = []
                for q, j in enumerate(tiles):
                    sr, si = st[2 * q], st[2 * q + 1]
                    if store and on_step is not None:
                        on_step(j, rows, sr, si)
                    nr = ar[q] * sr - ai[q] * si + xr3.at[j][rows, :]
                    ni = ar[q] * si + ai[q] * sr + xi3.at[j][rows, :]
                    if store:
                        xr3.at[j][rows, :] = nr
                        xi3.at[j][rows, :] = ni
                    new += [nr, ni]
                return tuple(new)
            return step

        zero = jnp.zeros((8, 128), f32)
        ends = lax.fori_loop(0, seg, sweep(False), (zero,) * (2 * TILE_GROUP))
        init = []
        for q, ls in enumerate(lanes):
            init += _seg_entry(ends[2 * q], ends[2 * q + 1], ar[q], ai[q], cr_ref[:, ls], ci_ref[:, ls],
                               seg, reverse, row)
        fin = lax.fori_loop(0, seg, sweep(True), tuple(init))
        edge = slice(0, 1) if reverse else slice(7, 8)
        for q, ls in enumerate(lanes):
            cr_ref[:, ls] = fin[2 * q][edge, :]
            ci_ref[:, ls] = fin[2 * q + 1][edge, :]


def _coarse(i):
    return slice(128 * i, 128 * (i + 1)), slice(512 * i, 512 * (i + 1))


def _seg_pitch(T):
    return T // 8 + 8


def _put_rows(x3, j, val):
    pitch = x3.shape[1] // 8
    seg = pitch - 8
    for s in range(8):
        x3[j, pitch * s:pitch * s + seg, :] = val[seg * s:seg * (s + 1), :]


def _get_rows(x3, j):
    pitch = x3.shape[1] // 8
    seg = pitch - 8
    return jnp.concatenate([x3[j, pitch * s:pitch * s + seg, :] for s in range(8)], axis=0)


def _tiles_cat(x3, i):
    return jnp.concatenate([_get_rows(x3, 4 * i + q) for q in range(4)], axis=1)


def _s5_gate2(sr3, si3, u, ccr_ref, cci_ref, dsk_ref, gw_ref, gb_ref):
    ys = []
    for i in range(4):
        ch, st = _coarse(i)
        ys.append(_dot(_tiles_cat(sr3, i), ccr_ref[st, ch]) - _dot(_tiles_cat(si3, i), cci_ref[st, ch]))
    y = jnp.concatenate(ys, axis=1) + dsk_ref[...] * u
    yg = _gelu(y)
    gl = jnp.concatenate([_dot(yg[:, _coarse(i)[0]], gw_ref[_coarse(i)[0], _coarse(i)[0]]) for i in range(4)],
                         axis=1)
    return y, yg, _sigmoid(gl + gb_ref[...])


def _s5_fwd2(p, abar_re, abar_im, bbr, bbi, ccr, cci, dsk, gw, gb):
    L = p.shape[0]
    T = _s5_chunk(L)

    def body(u_ref, are_ref, aim_ref, bbr_ref, bbi_ref, ccr_ref, cci_ref, dsk_ref, gw_ref, gb_ref,
             o_ref, sr3, si3, cr_ref, ci_ref):
        @pl.when(pl.program_id(0) == 0)
        def _():
            cr_ref[...] = jnp.zeros_like(cr_ref)
            ci_ref[...] = jnp.zeros_like(ci_ref)

        u = u_ref[...]
        for i in range(4):
            ch, st = _coarse(i)
            br = _dot(u[:, ch], bbr_ref[ch, st])
            bi = _dot(u[:, ch], bbi_ref[ch, st])
            for q in range(4):
                _put_rows(sr3, 4 * i + q, br[:, 128 * q:128 * (q + 1)])
                _put_rows(si3, 4 * i + q, bi[:, 128 * q:128 * (q + 1)])
        _seg_scan(sr3, si3, are_ref[...], aim_ref[...], cr_ref, ci_ref, reverse=False)
        _, yg, gate = _s5_gate2(sr3, si3, u, ccr_ref, cci_ref, dsk_ref, gw_ref, gb_ref)
        o_ref[...] = yg * gate

    full = lambda shape: pl.BlockSpec(shape, lambda i: (0, 0))
    rows = 8 * _seg_pitch(T)
    st3 = pl.BlockSpec((NTILE, rows, 128), lambda i: (0, i, 0))
    return pl.pallas_call(
        body, name="s5_fwd", grid=(L // T,),
        in_specs=[pl.BlockSpec((T, SSMW), lambda i: (i, 3)), full((1, NS)), full((1, NS)),
                  full((SSMW, NS)), full((SSMW, NS)), full((NS, SSMW)), full((NS, SSMW)),
                  full((1, SSMW)), full((SSMW, SSMW)), full((1, SSMW))],
        out_specs=[pl.BlockSpec((T, SSMW), lambda i: (i, 0)), st3, st3],
        out_shape=[jax.ShapeDtypeStruct((L, SSMW), f32)]
                  + [jax.ShapeDtypeStruct((NTILE, (L // T) * rows, 128), f32)] * 2,
        scratch_shapes=[pltpu.VMEM((1, NS), f32), pltpu.VMEM((1, NS), f32)],
        compiler_params=_cp(("arbitrary",)),
    )(p, abar_re, abar_im, bbr, bbi, ccr, cci, dsk, gw, gb)


def _s5_bwd2(dout, p, s_re, s_im, abar_re, abar_im, bbr, bbi, ccr, cci, dsk, gw, gb):
    L = p.shape[0]
    T = _s5_chunk(L)
    nchunk = L // T

    def body(do_ref, u_ref, sr3, si3, are_ref, aim_ref, bbr_ref, bbi_ref, ccr_ref, cci_ref,
             dsk_ref, gw_ref, gb_ref,
             du_ref, dare_ref, daim_ref, dd_ref, dgb_ref, dccr_ref, dcci_ref, dbbr_ref, dbbi_ref, dgw_ref,
             lr3, li3, cr_ref, ci_ref, accr3, acci3, wacc_ref, gacc_ref):
        i = pl.program_id(0)

        @pl.when(i == 0)
        def _():
            for ref in (cr_ref, ci_ref, accr3, acci3, dd_ref, dgb_ref, wacc_ref, gacc_ref):
                ref[...] = jnp.zeros_like(ref)

        u = u_ref[...]
        dov = do_ref[...]
        y, yg, gate = _s5_gate2(sr3, si3, u, ccr_ref, cci_ref, dsk_ref, gw_ref, gb_ref)
        dgl = dov * yg * gate * (1.0 - gate)
        dyg = dov * gate + jnp.concatenate(
            [_dot(dgl[:, _coarse(q)[0]], gw_ref[_coarse(q)[0], _coarse(q)[0]], NT) for q in range(4)], axis=1)
        dy = dyg * _gelu_grad(y)
        u16, dy16, yg16, dgl16 = (t.astype(_MXU) for t in (u, dy, yg, dgl))
        dd_ref[...] += jnp.sum(dy * u, axis=0, keepdims=True)
        dgb_ref[...] += jnp.sum(dgl, axis=0, keepdims=True)
        for q in range(4):
            ch, st = _coarse(q)
            wacc_ref[0, q] += _dot(dy16[:, ch], _tiles_cat(sr3, q), TN)
            wacc_ref[1, q] += _dot(dy16[:, ch], _tiles_cat(si3, q), TN)
            gacc_ref[q] += _dot(yg16[:, ch], dgl16[:, ch], TN)
            gr = _dot(dy16[:, ch], ccr_ref[st, ch], NT)
            gi = -_dot(dy16[:, ch], cci_ref[st, ch], NT)
            for t in range(4):
                _put_rows(lr3, 4 * q + t, gr[:, 128 * t:128 * (t + 1)])
                _put_rows(li3, 4 * q + t, gi[:, 128 * t:128 * (t + 1)])

        def on_step(j, rows, nxt_r, nxt_i):
            s_r = sr3.at[j][rows, :]
            s_i = si3.at[j][rows, :]
            accr3[j] += nxt_r * s_r + nxt_i * s_i
            acci3[j] += nxt_i * s_r - nxt_r * s_i

        _seg_scan(lr3, li3, are_ref[...], aim_ref[...], cr_ref, ci_ref, reverse=True, on_step=on_step)
        dus = []
        for q in range(4):
            ch, st = _coarse(q)
            lam_r, lam_i = _tiles_cat(lr3, q), _tiles_cat(li3, q)
            wacc_ref[2, q] += _dot(u16[:, ch], lam_r, TN)
            wacc_ref[3, q] += _dot(u16[:, ch], lam_i, TN)
            dus.append(_dot(lam_r, bbr_ref[ch, st], NT) + _dot(lam_i, bbi_ref[ch, st], NT))
        du_ref[...] = dy * dsk_ref[...] + jnp.concatenate(dus, axis=1)

        @pl.when(i == nchunk - 1)
        def _():
            for j in range(NTILE):
                ls = slice(128 * j, 128 * (j + 1))
                dare_ref[:, ls] = jnp.sum(accr3[j], axis=0, keepdims=True)
                daim_ref[:, ls] = jnp.sum(acci3[j], axis=0, keepdims=True)
            for m, o_ref in enumerate((dccr_ref, dcci_ref, dbbr_ref, dbbi_ref)):
                for q in range(4):
                    o_ref[512 * q:512 * (q + 1), :] = _diag_fold_t(wacc_ref[m, q])
            for q in range(4):
                dgw_ref[128 * q:128 * (q + 1), :] = _diag_fold(gacc_ref[q], NH16)

    rev = lambda i: (nchunk - 1 - i, 0)
    full = lambda shape: pl.BlockSpec(shape, lambda i: (0, 0))
    half = pl.BlockSpec((T, SSMW), rev)
    rows = 8 * _seg_pitch(T)
    st3 = pl.BlockSpec((NTILE, rows, 128), lambda i: (0, nchunk - 1 - i, 0))
    return pl.pallas_call(
        body, name="s5_bwd", grid=(nchunk,),
        in_specs=[half, pl.BlockSpec((T, SSMW), lambda i: (nchunk - 1 - i, 3)), st3, st3,
                  full((1, NS)), full((1, NS)), full((SSMW, NS)), full((SSMW, NS)),
                  full((NS, SSMW)), full((NS, SSMW)), full((1, SSMW)), full((SSMW, SSMW)), full((1, SSMW))],
        out_specs=[half, full((1, NS)), full((1, NS)), full((1, SSMW)), full((1, SSMW)),
                   full((NS, NH16)), full((NS, NH16)), full((NS, NH16)), full((NS, NH16)),
                   full((SSMW, NH16))],
        out_shape=[jax.ShapeDtypeStruct((L, SSMW), f32)] + [jax.ShapeDtypeStruct((1, NS), f32)] * 2
                  + [jax.ShapeDtypeStruct((1, SSMW), f32)] * 2 + [jax.ShapeDtypeStruct((NS, NH16), f32)] * 4
                  + [jax.ShapeDtypeStruct((SSMW, NH16), f32)],
        scratch_shapes=[pltpu.VMEM((NTILE, rows, 128), f32), pltpu.VMEM((NTILE, rows, 128), f32),
                        pltpu.VMEM((1, NS), f32), pltpu.VMEM((1, NS), f32),
                        pltpu.VMEM((NTILE, 8, 128), f32), pltpu.VMEM((NTILE, 8, 128), f32),
                        pltpu.VMEM((4, 4, 128, 512), f32), pltpu.VMEM((4, 128, 128), f32)],
        compiler_params=_cp(("arbitrary",)),
    )(dout, p, s_re, s_im, abar_re, abar_im, bbr, bbi, ccr, cci, dsk, gw, gb)


def _block_diag(x):
    g, r, c = x.shape
    eye = jnp.eye(g, dtype=x.dtype)
    return (x[:, :, None, :] * eye[:, None, :, None]).reshape(g * r, g * c)


def _mod_fwd(c16, ada_w, ada_b_loc):
    nloc = ada_w.shape[2]

    def body(c_ref, w_ref, b_ref, o_ref):
        cv = c_ref[...]
        act = cv * _sigmoid(cv)
        o_ref[...] = _dot(act, w_ref[...]) + b_ref[...]

    return pl.pallas_call(
        body, name="mod_fwd", grid=(DEPTH,),
        in_specs=[pl.BlockSpec((16, D), lambda l: (0, 0)), pl.BlockSpec((None, D, nloc), lambda l: (l, 0, 0)),
                  pl.BlockSpec((None, 1, nloc), lambda l: (l, 0, 0))],
        out_specs=pl.BlockSpec((None, 16, nloc), lambda l: (l, 0, 0)),
        out_shape=jax.ShapeDtypeStruct((DEPTH, 16, nloc), f32),
        compiler_params=_cp(("parallel",)),
    )(c16, ada_w, ada_b_loc)


def _ada_grad(c16, dmod16):
    nloc = dmod16.shape[2]

    def body(c_ref, d_ref, o_ref):
        cv = c_ref[...]
        act = cv * _sigmoid(cv)
        o_ref[...] = _dot(act, d_ref[...], TN)

    return pl.pallas_call(
        body, name="ada_grad", grid=(DEPTH,),
        in_specs=[pl.BlockSpec((16, D), lambda l: (0, 0)), pl.BlockSpec((None, 16, nloc), lambda l: (l, 0, 0))],
        out_specs=pl.BlockSpec((None, D, nloc), lambda l: (l, 0, 0)),
        out_shape=jax.ShapeDtypeStruct((DEPTH, D, nloc), f32),
        compiler_params=_cp(("parallel",)),
    )(c16, dmod16)


def _as2d(a):
    return a.reshape(-1, a.shape[-1])


def _ew_rows(rows):
    for cand in (512, 256, 128, 64, 32, 16, 8):
        if rows % cand == 0:
            return cand
    return rows


def _cast_bf16(w):
    w2 = _as2d(w)
    rows, cols = w2.shape
    bm = _ew_rows(rows)

    def body(x_ref, o_ref):
        o_ref[...] = x_ref[...].astype(bf16)

    spec = pl.BlockSpec((bm, cols), lambda i: (i, 0))
    out = pl.pallas_call(body, name="cast_bf16", grid=(rows // bm,), in_specs=[spec], out_specs=spec,
                         out_shape=jax.ShapeDtypeStruct((rows, cols), bf16),
                         compiler_params=_cp(("parallel",)))(w2)
    return out.reshape(w.shape)


def _adamw(parts, w, m, v):
    n = parts.shape[0]
    w2, m2, v2 = _as2d(w), _as2d(m), _as2d(v)
    rows, cols = w2.shape
    p3 = parts.reshape(n, rows, cols)
    bm = _ew_rows(rows)
    if bm * cols * 4 > (1 << 20) and bm % 16 == 0:
        bm //= 2

    def body(p_ref, w_ref, m_ref, v_ref, g_ref, d_ref, nm_ref, nv_ref):
        g = p_ref[0]
        for k in range(1, n):
            g = g + p_ref[k]
        mn = ADAM_B1 * m_ref[...] + (1.0 - ADAM_B1) * g
        vn = ADAM_B2 * v_ref[...] + (1.0 - ADAM_B2) * (g * g)
        m_hat = mn / (1.0 - ADAM_B1 ** ADAM_STEP)
        v_hat = vn / (1.0 - ADAM_B2 ** ADAM_STEP)
        g_ref[...] = g
        d_ref[...] = -ADAM_LR * (m_hat / (jnp.sqrt(v_hat) + ADAM_EPS) + ADAM_WD * w_ref[...])
        nm_ref[...] = mn
        nv_ref[...] = vn

    spec = pl.BlockSpec((bm, cols), lambda i: (i, 0))
    outs = pl.pallas_call(
        body, name="adamw", grid=(rows // bm,),
        in_specs=[pl.BlockSpec((n, bm, cols), lambda i: (0, i, 0)), spec, spec, spec],
        out_specs=[spec] * 4, out_shape=[jax.ShapeDtypeStruct((rows, cols), f32)] * 4,
        compiler_params=_cp(("parallel",)),
    )(p3, w2, m2, v2)
    return tuple(o.reshape(w.shape) for o in outs)


ANY = pl.BlockSpec(memory_space=pl.ANY)


def _my_pos():
    return lax.axis_index("x"), lax.axis_index("y"), lax.axis_index("c")


def _allgather8(name, x):
    def body(x_ref, out_ref, send_sems, recv_sems):
        mx, my, mc = _my_pos()
        me, sibling = (mx, my, mc), (mx, my, 1 - mc)
        chips = [(1 - mx, my), (mx, 1 - my), (1 - mx, 1 - my)]

        def slot(px, py, pc):
            return out_ref.at[4 * px + 2 * py + pc]

        def copy(k, block, to, src=None):
            return pltpu.make_async_remote_copy(
                src_ref=slot(*block) if src is None else src, dst_ref=slot(*block),
                send_sem=send_sems.at[k], recv_sem=recv_sems.at[k], device_id=to, device_id_type=MESH)

        first = [copy(0, me, sibling, src=x_ref)]
        first += [copy(1 + j, me, (*chip, mc), src=x_ref) for j, chip in enumerate(chips)]
        for cp in first:
            cp.start()
        passed = [copy(4 + j, (*chip, mc), sibling) for j, chip in enumerate(chips)]
        for j, chip in enumerate(chips):
            copy(1 + j, (*chip, mc), me).wait_recv()
            passed[j].start()
        copy(0, sibling, me).wait_recv()
        for j, chip in enumerate(chips):
            copy(4 + j, (*chip, 1 - mc), me).wait_recv()
        for cp in first + passed:
            cp.wait_send()

    out = pl.pallas_call(
        body, name=name, in_specs=[ANY], out_specs=ANY,
        out_shape=jax.ShapeDtypeStruct((NDEV,) + x.shape, x.dtype),
        scratch_shapes=[pltpu.SemaphoreType.DMA((7,)), pltpu.SemaphoreType.DMA((7,))],
    )(x)
    dev = 4 * lax.axis_index("x") + 2 * lax.axis_index("y") + lax.axis_index("c")
    return lax.dynamic_update_index_in_dim(out, x, dev, 0)


def _sum_parts(parts):
    n = parts.shape[0]
    p3 = parts.reshape(n, -1, parts.shape[-1])
    rows, cols = p3.shape[1:]
    bm = _ew_rows(rows)
    if bm * cols * 4 > (1 << 20) and bm % 16 == 0:
        bm //= 2

    def body(p_ref, o_ref):
        g = p_ref[0].astype(f32)
        for k in range(1, n):
            g = g + p_ref[k].astype(f32)
        o_ref[...] = g

    out = pl.pallas_call(
        body, name="sum_parts", grid=(rows // bm,),
        in_specs=[pl.BlockSpec((n, bm, cols), lambda i: (0, i, 0))],
        out_specs=pl.BlockSpec((bm, cols), lambda i: (i, 0)),
        out_shape=jax.ShapeDtypeStruct((rows, cols), f32),
        compiler_params=_cp(("parallel",)),
    )(p3)
    return out.reshape(parts.shape[1:])


def _rc(src, dst, ss, rs, k, to):
    return pltpu.make_async_remote_copy(src_ref=src, dst_ref=dst, send_sem=ss.at[k], recv_sem=rs.at[k],
                                        device_id=to, device_id_type=MESH)


def _half_rows(rows, c):
    return pl.ds((rows // 2) * c, rows // 2)


def _gather_comm(shards, l):
    n = len(shards)

    def plan(ins, outs, ss, rs):
        mx, my, mc = _my_pos()
        chips = [(1 - mx, my), (mx, 1 - my), (1 - mx, 1 - my)]
        mine = 2 * mx + my
        sibling = (mx, my, 1 - mc)
        sends, lands, fwds, fwd_lands, own = [], [], [], [], []
        for a in range(n):
            src = ins[a].at[l]
            rows = ins[a].shape[1]
            mr, sr = _half_rows(rows, mc), _half_rows(rows, 1 - mc)
            for j, (px, py) in enumerate(chips):
                k, src_chip = 3 * a + j, 2 * px + py
                sends.append(_rc(src.at[mr], outs[a].at[mine, mr], ss, rs, k, (px, py, mc)))
                lands.append(_rc(src.at[mr], outs[a].at[src_chip, mr], ss, rs, k, (px, py, mc)))
                blk, sblk = outs[a].at[src_chip, mr], outs[a].at[src_chip, sr]
                fwds.append(_rc(blk, blk, ss, rs, 3 * n + k, sibling))
                fwd_lands.append(_rc(sblk, sblk, ss, rs, 3 * n + k, sibling))
            own.append(_rc(src, outs[a].at[mine], ss, rs, 6 * n + a, sibling))
        return sends, lands, fwds, fwd_lands, own

    def start(ins, outs, ss, rs):
        sends, _, _, _, own = plan(ins, outs, ss, rs)
        for cp in sends + own:
            cp.start()

    def finish(ins, outs, ss, rs):
        sends, lands, fwds, fwd_lands, own = plan(ins, outs, ss, rs)
        for land, fwd in zip(lands, fwds):
            land.wait_recv()
            fwd.start()
        for cp in fwd_lands:
            cp.wait_recv()
        for cp in sends + fwds:
            cp.wait_send()
        for cp in own:
            cp.wait()

    return dict(ins=list(shards), nsem=7 * n, start=start, finish=finish,
                out_shapes=[jax.ShapeDtypeStruct((NCHIP,) + a.shape[1:], a.dtype) for a in shards])


def _swap_comm(grads):
    n = len(grads)

    def plan(ins, outs, ss, rs):
        mx, my, mc = _my_pos()
        return [_rc(ins[a].at[j, _half_rows(ins[a].shape[1], 1 - mc)], outs[a].at[j], ss, rs,
                    NCHIP * a + j, (mx, my, 1 - mc)) for a in range(n) for j in range(NCHIP)]

    def start(ins, outs, ss, rs):
        for cp in plan(ins, outs, ss, rs):
            cp.start()

    def finish(ins, outs, ss, rs):
        for cp in plan(ins, outs, ss, rs):
            cp.wait()

    return dict(ins=list(grads), nsem=NCHIP * n, start=start, finish=finish,
                out_shapes=[jax.ShapeDtypeStruct((NCHIP, a.shape[1] // 2) + a.shape[2:], a.dtype)
                            for a in grads])


def _scatter_comm(sums):
    n = len(sums)

    def plan(ins, outs, ss, rs):
        mx, my, mc = _my_pos()
        chips = [(1 - mx, my), (mx, 1 - my), (1 - mx, 1 - my)]
        mine = 2 * mx + my
        sends, lands = [], []
        for a in range(n):
            for j, (px, py) in enumerate(chips):
                sends.append(_rc(ins[a].at[2 * px + py], outs[a].at[mine], ss, rs, 3 * a + j, (px, py, mc)))
                lands.append(_rc(ins[a].at[mine], outs[a].at[2 * px + py], ss, rs, 3 * a + j, (px, py, mc)))
        return sends, lands

    def start(ins, outs, ss, rs):
        for cp in plan(ins, outs, ss, rs)[0]:
            cp.start()

    def finish(ins, outs, ss, rs):
        sends, lands = plan(ins, outs, ss, rs)
        for cp in lands:
            cp.wait_recv()
        for cp in sends:
            cp.wait_send()

    return dict(ins=list(sums), nsem=3 * n, start=start, finish=finish,
                out_shapes=[jax.ShapeDtypeStruct(a.shape, a.dtype) for a in sums])


def _scatter_done(outs, sums):
    mine = 2 * lax.axis_index("x") + lax.axis_index("y")
    return [lax.dynamic_update_index_in_dim(o, lax.dynamic_index_in_dim(a, mine, 0, keepdims=False), mine, 0)
            for o, a in zip(outs, sums)]


def _merge_comm(halves):
    n = len(halves)

    def plan(ins, outs, ss, rs):
        mx, my, mc = _my_pos()
        return [_rc(ins[a], outs[a].at[_half_rows(outs[a].shape[0], mc)], ss, rs, a, (mx, my, 1 - mc))
                for a in range(n)]

    def start(ins, outs, ss, rs):
        for cp in plan(ins, outs, ss, rs):
            cp.start()

    def finish(ins, outs, ss, rs):
        for cp in plan(ins, outs, ss, rs):
            cp.wait()

    return dict(ins=list(halves), nsem=n, start=start, finish=finish,
                out_shapes=[jax.ShapeDtypeStruct((2 * a.shape[0],) + a.shape[1:], a.dtype) for a in halves])


def _merge_done(outs, halves):
    mc = lax.axis_index("c")
    return [lax.dynamic_update_slice_in_dim(o, a, a.shape[0] * mc, 0) for o, a in zip(outs, halves)]


def _comm_call(name, comm):
    ni, no = len(comm["ins"]), len(comm["out_shapes"])

    def body(*refs):
        ins, outs, (ss, rs) = refs[:ni], refs[ni:ni + no], refs[ni + no:]
        comm["start"](ins, outs, ss, rs)
        comm["finish"](ins, outs, ss, rs)

    return pl.pallas_call(
        body, name=name, in_specs=[ANY] * ni, out_specs=[ANY] * no, out_shape=comm["out_shapes"],
        scratch_shapes=[pltpu.SemaphoreType.DMA((comm["nsem"],)), pltpu.SemaphoreType.DMA((comm["nsem"],))],
    )(*comm["ins"])


def _call(body, comm, *, name, grid, in_specs, out_specs, out_shape, scratch_shapes, sem, args):
    if comm is None:
        outs = pl.pallas_call(body, name=name, grid=grid, in_specs=in_specs, out_specs=out_specs,
                              out_shape=out_shape, scratch_shapes=scratch_shapes,
                              compiler_params=_cp(sem))(*args)
        return outs, None
    ni, no = len(in_specs), len(out_specs)
    ci, co = len(comm["ins"]), len(comm["out_shapes"])
    nscr = len(scratch_shapes)

    def carried(*refs):
        ins, cins = refs[:ni], refs[ni:ni + ci]
        outs, couts = refs[ni + ci:ni + ci + no], refs[ni + ci + no:ni + ci + no + co]
        scr = refs[ni + ci + no + co:]
        ss, rs = scr[nscr], scr[nscr + 1]
        pids = [pl.program_id(ax) for ax in range(len(grid))]
        first = functools.reduce(jnp.logical_and, [p == 0 for p in pids])
        last = functools.reduce(jnp.logical_and, [p == g - 1 for p, g in zip(pids, grid)])

        @pl.when(first)
        def _():
            comm["start"](cins, couts, ss, rs)

        body(*ins, *outs, *scr[:nscr])

        @pl.when(last)
        def _():
            comm["finish"](cins, couts, ss, rs)

    outs = pl.pallas_call(
        carried, name=name + "_x", grid=grid, in_specs=list(in_specs) + [ANY] * ci,
        out_specs=list(out_specs) + [ANY] * co, out_shape=list(out_shape) + comm["out_shapes"],
        scratch_shapes=list(scratch_shapes) + [pltpu.SemaphoreType.DMA((comm["nsem"],)),
                                               pltpu.SemaphoreType.DMA((comm["nsem"],))],
        compiler_params=_cp(("arbitrary",) * len(grid)),
    )(*args, *comm["ins"])
    return outs[:no], outs[no:]


def _layer_fwd(x, res, res_gate, wl, mod, comm=None):
    L = x.shape[0]
    bm = _row_block(L, 1024)
    nb = L // bm
    sh1, sc1, g1, sh2, sc2, g2 = [mod[k * D:(k + 1) * D][None, :] for k in range(NMOD)]
    sv = {}
    xin, h = _rmsmod_fwd(x, res, res_gate, wl["norm1_g"], sh1, sc1)
    if xin is None:
        xin = x
    p = _mm("mm_in", h, wl["w_in"], dims=NN, grid=(nb, 4, 1),
            a_block=(bm, D), a_map=lambda i, j, k: (i, 0),
            b_block=(None, D, 512), b_map=lambda i, j, k: (j, 0, 0),
            out_shape=(L, INW), o_block=(bm, 512), o_map=lambda i, j, k: (i, j))
    qkv = _qknorm_fwd(p, wl["gqk"])
    oa, couts = _attn_fwd(qkv, comm)
    os_, s_re, s_im = _s5_fwd2(p, wl["abar_re"], wl["abar_im"], wl["bbr"], wl["bbi"], wl["ccr"], wl["cci"],
                              wl["dsk"], wl["gw"], wl["gb"])
    o = _outnorm_fwd(oa, os_, wl["attn_out_g"], wl["ssm_out_g"])
    proj = _mm("mm_out", o, wl["w_out"], dims=NN, grid=(nb, 2, 1),
               a_block=(bm, D), a_map=lambda i, j, k: (i, 0),
               b_block=(D, 512), b_map=lambda i, j, k: (0, j),
               out_shape=(L, D), o_block=(bm, 512), o_map=lambda i, j, k: (i, j))
    x1, h2 = _rmsmod_fwd(xin, proj, g1, wl["norm2_g"], sh2, sc2)
    up_pre = _mm("mm_up", h2, wl["w_up"], dims=NN, grid=(nb, 4, 1),
                 a_block=(bm, D), a_map=lambda i, j, k: (i, 0),
                 b_block=(None, D, 1408), b_map=lambda i, j, k: (j, 0, 0),
                 out_shape=(L, DUP), o_block=(bm, 1408), o_map=lambda i, j, k: (i, j))
    a = _convglu_fwd(up_pre, wl["conv_w"], wl["conv_b"])
    down = _mm("mm_down", a, wl["w_down"], dims=NN, grid=(nb, 2, 1),
               a_block=(bm, DFF), a_map=lambda i, j, k: (i, 0),
               b_block=(DFF, 512), b_map=lambda i, j, k: (0, j),
               out_shape=(L, D), o_block=(bm, 512), o_map=lambda i, j, k: (i, j))
    sv.update(xin=xin, h=h, p=p, qkv=qkv, oa=oa, os=os_, s_re=s_re, s_im=s_im, o=o, proj=proj,
              x1=x1, h2=h2, up_pre=up_pre, a=a, down=down,
              sc1=sc1, g1=g1, sc2=sc2, g2=g2)
    return x1, down, g2, sv, couts


def _big_grads(g):
    return [g["w_in"], g["w_out"].reshape(NCHIP, D // NCHIP, D), g["w_up"],
            g["w_down"].reshape(NCHIP, DFF // NCHIP, D)]


def _chip_sums(grads, recv):
    core = lax.axis_index("c").astype(jnp.int32).reshape((1,))

    def half_sum(a, b):
        _, half, cols = b.shape
        bm = half if half * cols * 4 <= (3 << 19) else half // 2
        nrb = half // bm

        def body(core_ref, a_ref, b_ref, o_ref):
            o_ref[...] = (a_ref[...] + b_ref[...]).astype(o_ref.dtype)

        blk = pl.BlockSpec((None, bm, cols), lambda j, i, core_ref: (j, i, 0))
        return pl.pallas_call(
            body, name="chip_sum",
            grid_spec=pltpu.PrefetchScalarGridSpec(
                num_scalar_prefetch=1, grid=(NCHIP, nrb),
                in_specs=[pl.BlockSpec((None, bm, cols), lambda j, i, core_ref: (j, core_ref[0] * nrb + i, 0)),
                          blk],
                out_specs=blk),
            out_shape=jax.ShapeDtypeStruct(b.shape, bf16),
            compiler_params=_cp(("parallel", "parallel")),
        )(core, a, b)

    return [half_sum(a, b) for a, b in zip(grads, recv)]


def _reduced(parts, sums):
    halves = [_sum_parts(p) for p in _scatter_done(parts, sums)]
    return _merge_done(_comm_call("grad_merge", _merge_comm(halves)), halves)


def _layer_bwd(dx_out, wl, sv, prev=None):
    L = dx_out.shape[0]
    bm = _row_block(L, 1024)
    nb = L // bm
    bk = _row_block(L, 1024)
    nk = L // bk
    g = {}
    dd, dg2 = _gate_bwd(dx_out, sv["down"], sv["g2"])
    da = _mm("mm_down_dx", dd, wl["w_down"], dims=NT, grid=(nb, 2, 1),
             a_block=(bm, D), a_map=lambda i, j, k: (i, 0),
             b_block=(1408, D), b_map=lambda i, j, k: (j, 0),
             out_shape=(L, DFF), o_block=(bm, 1408), o_map=lambda i, j, k: (i, j))
    g["w_down"] = _mm("mm_down_dw", sv["a"], dd, dims=TN, grid=(1, 2, nk),
                      a_block=(bk, DFF), a_map=lambda i, j, k: (k, 0),
                      b_block=(bk, 512), b_map=lambda i, j, k: (k, j),
                      out_shape=(DFF, D), o_block=(DFF, 512), o_map=lambda i, j, k: (0, j))
    dup, g["conv_w"], g["conv_b"], recv = _convglu_bwd(da, sv["up_pre"], wl["conv_w"], wl["conv_b"],
                                                       _swap_comm(prev) if prev else None)
    sums = _chip_sums(prev, recv) if prev else None
    dh2 = _mm("mm_up_dx", dup, wl["w_up"], dims=NT, grid=(nb, 1, 4),
              a_block=(None, bm, 1408), a_map=lambda i, j, k: (k // 2, i, k % 2),
              b_block=(None, D, 1408), b_map=lambda i, j, k: (k, 0, 0),
              out_shape=(L, D), o_block=(bm, D), o_map=lambda i, j, k: (i, 0))
    g["w_up"] = _mm("mm_up_dw", sv["h2"], dup, dims=TN, grid=(1, 4, nk),
                    a_block=(bk, D), a_map=lambda i, j, k: (k, 0),
                    b_block=(None, bk, 1408), b_map=lambda i, j, k: (j // 2, k, j % 2),
                    out_shape=(NCHIP, D, 1408), o_block=(None, D, 1408), o_map=lambda i, j, k: (j, 0, 0))
    dx1, dsh2, dsc2, g["norm2_g"] = _rmsmod_bwd(dh2, sv["x1"], wl["norm2_g"], sv["sc2"], dx_out)
    dproj, dg1 = _gate_bwd(dx1, sv["proj"], sv["g1"])
    do = _mm("mm_out_dx", dproj, wl["w_out"], dims=NT, grid=(nb, 2, 1),
             a_block=(bm, D), a_map=lambda i, j, k: (i, 0),
             b_block=(512, D), b_map=lambda i, j, k: (j, 0),
             out_shape=(L, D), o_block=(bm, 512), o_map=lambda i, j, k: (i, j))
    g["w_out"] = _mm("mm_out_dw", sv["o"], dproj, dims=TN, grid=(1, 2, nk),
                     a_block=(bk, D), a_map=lambda i, j, k: (k, 0),
                     b_block=(bk, 512), b_map=lambda i, j, k: (k, j),
                     out_shape=(D, D), o_block=(D, 512), o_map=lambda i, j, k: (0, j))
    doa, dos, g["attn_out_g"], g["ssm_out_g"] = _outnorm_bwd(do, sv["oa"], sv["os"],
                                                             wl["attn_out_g"], wl["ssm_out_g"])
    (dqn, dkn, dv), parts = _attn_bwd(sv["qkv"], sv["oa"], doa, _scatter_comm(sums) if prev else None)
    prev_reduced = _reduced(parts, sums) if prev else None
    (du, d_abar_re, d_abar_im, g["ssm_d"], g["glu_b"], d_ccr, d_cci, d_bbre, d_bbim, d_gw) = _s5_bwd2(
        dos, sv["p"], sv["s_re"], sv["s_im"], wl["abar_re"], wl["abar_im"], wl["bbr"], wl["bbi"],
        wl["ccr"], wl["cci"], wl["dsk"], wl["gw"], wl["gb"])
    g["ssm_c_re"] = d_ccr.reshape(NG, NP, NH16).transpose(0, 2, 1)
    g["ssm_c_im"] = -d_cci.reshape(NG, NP, NH16).transpose(0, 2, 1)
    g["glu_w"] = d_gw.reshape(NG, NH16, NH16)
    d_ldt, d_ar, d_ai, d_br, d_bi = _s5_disc_bwd(
        wl["ldt_col"], wl["ar_col"], wl["ai_col"], wl["br_mat"], wl["bi_mat"],
        d_abar_re.reshape(NS, 1), d_abar_im.reshape(NS, 1), d_bbre, d_bbim)
    g["ssm_log_dt"] = d_ldt[:, 0]
    g["ssm_a_re"] = d_ar.reshape(NG, NP)
    g["ssm_a_im"] = d_ai.reshape(NG, NP)
    g["ssm_b_re"] = d_br.reshape(NG, NP, NH16)
    g["ssm_b_im"] = d_bi.reshape(NG, NP, NH16)
    dp, dgqk = _qknorm_bwd(dqn, dkn, dv, du, sv["p"], wl["gqk"])
    g["q_norm_g"] = dgqk[0, :HD]
    g["k_norm_g"] = dgqk[0, HD:2 * HD]
    dh = _mm("mm_in_dx", dp, wl["w_in"], dims=NT, grid=(nb, 1, 4),
             a_block=(bm, 512), a_map=lambda i, j, k: (i, k),
             b_block=(None, D, 512), b_map=lambda i, j, k: (k, 0, 0),
             out_shape=(L, D), o_block=(bm, D), o_map=lambda i, j, k: (i, 0))
    g["w_in"] = _mm("mm_in_dw", sv["h"], dp, dims=TN, grid=(1, 4, nk),
                    a_block=(bk, D), a_map=lambda i, j, k: (k, 0),
                    b_block=(bk, 512), b_map=lambda i, j, k: (k, j),
                    out_shape=(NCHIP, D, 512), o_block=(None, D, 512), o_map=lambda i, j, k: (j, 0, 0))
    dx_in, dsh1, dsc1, g["norm1_g"] = _rmsmod_bwd(dh, sv["xin"], wl["norm1_g"], sv["sc1"], dx1)
    g["dmod"] = jnp.concatenate([dsh1, dsc1, dg1, dsh2, dsc2, dg2], axis=1)[0]
    return dx_in, g, prev_reduced


def _prep_layer(l, conv_w, small):
    wl = {}
    wl["conv_w"] = conv_w[l]
    wl["conv_b"] = small["ffn_conv_b"][l][None, :]
    wl["norm1_g"] = small["norm1_g"][l][None, :]
    wl["norm2_g"] = small["norm2_g"][l][None, :]
    wl["attn_out_g"] = small["attn_out_g"][l][None, :]
    wl["ssm_out_g"] = small["ssm_out_g"][l][None, :]
    wl["gqk"] = jnp.concatenate([jnp.tile(small["q_norm_g"][l], 8), jnp.tile(small["k_norm_g"][l], 8)])[None, :]
    wl["ldt_col"] = jnp.repeat(small["ssm_log_dt"][l], NP)[:, None]
    wl["ar_col"] = small["ssm_a_re"][l].reshape(NS, 1)
    wl["ai_col"] = small["ssm_a_im"][l].reshape(NS, 1)
    wl["br_mat"] = small["ssm_b_re"][l].reshape(NS, NH16)
    wl["bi_mat"] = small["ssm_b_im"][l].reshape(NS, NH16)
    abar_re, abar_im, bb_re, bb_im = _s5_disc_fwd(wl["ldt_col"], wl["ar_col"], wl["ai_col"],
                                                  wl["br_mat"], wl["bi_mat"])
    wl["abar_re"] = abar_re.reshape(1, NS)
    wl["abar_im"] = abar_im.reshape(1, NS)
    wl["bbr"] = _block_diag(bb_re.astype(bf16).reshape(NG, NP, NH16).transpose(0, 2, 1))
    wl["bbi"] = _block_diag(bb_im.astype(bf16).reshape(NG, NP, NH16).transpose(0, 2, 1))
    wl["ccr"] = _block_diag(small["ssm_c_re"][l].astype(bf16).transpose(0, 2, 1))
    wl["cci"] = _block_diag(small["ssm_c_im"][l].astype(bf16).transpose(0, 2, 1))
    wl["gw"] = _block_diag(small["glu_w"][l].astype(bf16))
    wl["dsk"] = small["ssm_d"][l].reshape(1, SSMW)
    wl["gb"] = small["glu_b"][l].reshape(1, SSMW)
    return wl


def _local_step(x, target, mods, shards, conv_w, small):
    def set_big(wl, gathered):
        w_in, w_out, w_up, w_down = gathered
        wl.update(w_in=w_in, w_up=w_up, w_out=w_out.reshape(D, D), w_down=w_down.reshape(DFF, D))

    wls = [_prep_layer(l, conv_w, small) for l in range(DEPTH)]
    set_big(wls[0], _comm_call("weight_gather", _gather_comm(shards, 0)))
    saved = []
    cur, res, res_gate = x, None, None
    for l in range(DEPTH):
        nxt = _gather_comm(shards, l + 1) if l + 1 < DEPTH else None
        cur, res, res_gate, sv, couts = _layer_fwd(cur, res, res_gate, wls[l], mods[l], nxt)
        saved.append(sv)
        if nxt is not None:
            set_big(wls[l + 1], couts)
    loss, dx = _loss_fwd_bwd(cur, res, res_gate, target)
    grads = [None] * DEPTH
    big = [None] * DEPTH
    prev = None
    for l in reversed(range(DEPTH)):
        dx, grads[l], done = _layer_bwd(dx, wls[l], saved[l], prev)
        if prev is not None:
            big[l + 1] = done
        prev = _big_grads(grads[l])
    recv = _comm_call("grad_swap", _swap_comm(prev))
    sums = _chip_sums(prev, recv)
    big[0] = _reduced(_comm_call("grad_scatter", _scatter_comm(sums)), sums)
    return loss, dx, grads, big


BIG = ("w_in", "w_out", "ffn_w_up", "ffn_w_down")
SMALL = ("ada_b", "norm1_g", "q_norm_g", "k_norm_g", "ssm_a_re", "ssm_a_im", "ssm_log_dt",
         "ssm_b_re", "ssm_b_im", "ssm_c_re", "ssm_c_im", "ssm_d", "glu_w", "glu_b",
         "attn_out_g", "ssm_out_g", "norm2_g", "ffn_conv_b")
NAMES = ("ada_w", "ada_b", "norm1_g", "w_in", "q_norm_g", "k_norm_g", "ssm_a_re", "ssm_a_im",
         "ssm_log_dt", "ssm_b_re", "ssm_b_im", "ssm_c_re", "ssm_c_im", "ssm_d", "glu_w", "glu_b",
         "attn_out_g", "ssm_out_g", "w_out", "norm2_g", "ffn_w_up", "ffn_conv_w", "ffn_conv_b",
         "ffn_w_down")
PACK_COLS = 1024


def _pack(arrs):
    flat = jnp.concatenate([a.reshape(-1) for a in arrs])
    rows = -(-flat.shape[0] // PACK_COLS)
    rows = -(-rows // 8) * 8
    flat = jnp.pad(flat, (0, rows * PACK_COLS - flat.shape[0]))
    return flat.reshape(rows, PACK_COLS)


def _unpack(packed, shapes):
    flat = packed.reshape(-1)
    out, off = [], 0
    for s in shapes:
        n = math.prod(s)
        out.append(flat[off:off + n].reshape(s))
        off += n
    return out


def kernel(x, c, ada_w, ada_b, norm1_g, w_in, q_norm_g, k_norm_g, ssm_a_re, ssm_a_im, ssm_log_dt, ssm_b_re, ssm_b_im, ssm_c_re, ssm_c_im, ssm_d, glu_w, glu_b, attn_out_g, ssm_out_g, w_out, norm2_g, ffn_w_up, ffn_conv_w, ffn_conv_b, ffn_w_down, loss_target, m_ada_w, m_ada_b, m_norm1_g, m_w_in, m_q_norm_g, m_k_norm_g, m_ssm_a_re, m_ssm_a_im, m_ssm_log_dt, m_ssm_b_re, m_ssm_b_im, m_ssm_c_re, m_ssm_c_im, m_ssm_d, m_glu_w, m_glu_b, m_attn_out_g, m_ssm_out_g, m_w_out, m_norm2_g, m_ffn_w_up, m_ffn_conv_w, m_ffn_conv_b, m_ffn_w_down, v_ada_w, v_ada_b, v_norm1_g, v_w_in, v_q_norm_g, v_k_norm_g, v_ssm_a_re, v_ssm_a_im, v_ssm_log_dt, v_ssm_b_re, v_ssm_b_im, v_ssm_c_re, v_ssm_c_im, v_ssm_d, v_glu_w, v_glu_b, v_attn_out_g, v_ssm_out_g, v_w_out, v_norm2_g, v_ffn_w_up, v_ffn_conv_w, v_ffn_conv_b, v_ffn_w_down):
    env = dict(locals())
    w = {n: env[n] for n in NAMES}
    m = {n: env["m_" + n] for n in NAMES}
    v = {n: env["v_" + n] for n in NAMES}
    mx, my, mc = _my_pos()
    chip = 2 * mx + my
    dev = 4 * mx + 2 * my + mc
    xl = x[0]
    tl = loss_target[0]

    c_all = _allgather8("gather_c", jnp.pad(c, ((0, 7), (0, 0))))[:, 0, :]
    c16 = jnp.pad(c_all, ((0, 8), (0, 0)))
    nloc = NMOD * D // NCHIP
    ada_b_loc = lax.dynamic_slice(ada_b, (0, chip * nloc), (DEPTH, nloc))[:, None, :]
    mod_loc = _mod_fwd(c16, ada_w, ada_b_loc)
    mod_g = _allgather8("gather_mod", mod_loc.reshape(DEPTH * 16, nloc))
    mod_all = mod_g[0::2].reshape(NCHIP, DEPTH, 16, nloc).transpose(1, 2, 0, 3).reshape(DEPTH, 16, NMOD * D)
    mods = lax.dynamic_index_in_dim(mod_all, dev, axis=1, keepdims=False)

    shards = [_cast_bf16(w[n]) for n in BIG]
    ncw = DUP // NCHIP
    conv_g = _allgather8("gather_conv", jnp.pad(w["ffn_conv_w"].reshape(DEPTH * 3, ncw), ((0, 4), (0, 0))))
    conv_w = conv_g[0::2, :DEPTH * 3].reshape(NCHIP, DEPTH, 3, ncw).transpose(1, 2, 0, 3).reshape(DEPTH, 3, DUP)
    small = {n: w[n] for n in SMALL}

    loss_loc, grad_x, grads, big = _local_step(xl, tl, mods, shards, conv_w, small)
    loss = lax.psum(loss_loc, ("x", "y", "c"))

    outs = {}
    for k, name in enumerate(BIG):
        g_big = jnp.stack([big[l][k] for l in range(DEPTH)])
        outs[name] = _adamw(g_big[None], w[name], m[name], v[name])

    small_names = SMALL[1:] + ("ffn_conv_w_full",)
    small_grads = []
    for n in SMALL[1:]:
        key = {"ffn_conv_b": "conv_b"}.get(n, n)
        small_grads.append(jnp.stack([grads[l][key].reshape(w[n].shape[1:]) for l in range(DEPTH)]))
    small_grads.append(jnp.stack([grads[l]["conv_w"] for l in range(DEPTH)]))
    dmod = jnp.stack([grads[l]["dmod"] for l in range(DEPTH)])
    packed = _pack([dmod] + small_grads)
    allp = _allgather8("gather_small", packed)
    shapes = [dmod.shape] + [a.shape for a in small_grads]
    dmod_all = allp[:, :DEPTH * NMOD, :].reshape(NDEV, DEPTH, NMOD * D)
    dmod_loc = lax.dynamic_slice(dmod_all, (0, 0, chip * nloc), (NDEV, DEPTH, nloc)).transpose(1, 0, 2)
    g_ada = _ada_grad(c16, jnp.pad(dmod_loc, ((0, 0), (0, 8), (0, 0))))
    outs["ada_w"] = _adamw(g_ada[None], w["ada_w"], m["ada_w"], v["ada_w"])
    conv_w_shape = (DEPTH, 3, DUP)
    w_small = _pack([w[n] for n in SMALL] + [jnp.zeros(conv_w_shape, f32)])
    m_small = _pack([m[n] for n in SMALL] + [jnp.zeros(conv_w_shape, f32)])
    v_small = _pack([v[n] for n in SMALL] + [jnp.ones(conv_w_shape, f32)])
    res_small = _adamw(allp, w_small, m_small, v_small)
    unpacked = [_unpack(r, shapes) for r in res_small]
    for i, n in enumerate(SMALL):
        outs[n] = tuple(unpacked[k][i] for k in range(4))
    g_conv_full = unpacked[0][len(SMALL)]
    ncw = DUP // NCHIP
    g_conv = lax.dynamic_slice(g_conv_full, (0, 0, chip * ncw), (DEPTH, 3, ncw))
    outs["ffn_conv_w"] = _adamw(g_conv[None], w["ffn_conv_w"], m["ffn_conv_w"], v["ffn_conv_w"])

    result = [loss, grad_x[None]]
    for k in range(4):
        result += [outs[n][k] for n in NAMES]
    return tuple(result)
```
